```python
import math
import jax, jax.numpy as jnp
from jax import lax
import numpy as np

D_MODEL = 1024
BATCH = 8
SEQ = 4096
DEPTH = 4

ATTN_WIDTH = D_MODEL // 2
SSM_WIDTH = D_MODEL - ATTN_WIDTH
HEAD_DIM = 64
N_Q_HEADS = ATTN_WIDTH // HEAD_DIM
N_KV_HEADS = 2
Q_PER_KV = N_Q_HEADS // N_KV_HEADS
KV_WIDTH = N_KV_HEADS * HEAD_DIM
WINDOW = 128
BLOCK = 128
SSM_GROUP = 16
N_SSM_GROUPS = SSM_WIDTH // SSM_GROUP
STATE = 64
D_FF = 4 * D_MODEL
IN_WIDTH = ATTN_WIDTH + 2 * KV_WIDTH + SSM_WIDTH
N_MOD = 6
EPS = 1e-6
NEG_INF = -1e30
DT_MIN = 1e-3
DT_MAX = 1e-1

kernel_name = "hymba_swa_s5_sqrelu_adaln"


def rmsnorm(x, g):
    xf = x.astype(jnp.float32)
    y = xf * lax.rsqrt(jnp.mean(xf * xf, axis=-1, keepdims=True) + EPS)
    return (y * g.astype(jnp.float32)).astype(x.dtype)


def alibi_slopes():
    s = 2.0 ** (-8.0 * np.arange(1, N_Q_HEADS + 1) / N_Q_HEADS)
    return jnp.asarray(s, dtype=jnp.float32).reshape(N_KV_HEADS, Q_PER_KV)


def sliding_window_attention(q, k, v, sinks):
    b, l = q.shape[0], q.shape[1]
    nb = l // BLOCK
    qb = q.reshape(b, nb, BLOCK, N_KV_HEADS, Q_PER_KV, HEAD_DIM)

    def band(t):
        t = t.reshape(b, l, N_KV_HEADS, HEAD_DIM)
        tp = jnp.pad(t, ((0, 0), (BLOCK, 0), (0, 0), (0, 0)))
        tb = tp.reshape(b, nb + 1, BLOCK, N_KV_HEADS, HEAD_DIM)
        return jnp.concatenate([tb[:, :-1], tb[:, 1:]], axis=2)

    kb, vb = band(k), band(v)
    scores = jnp.einsum('bnqhgd,bnshd->bnhgqs', qb, kb).astype(jnp.float32) * (HEAD_DIM ** -0.5)

    r = jnp.arange(BLOCK)[:, None]
    j = jnp.arange(2 * BLOCK)[None, :]
    diff = BLOCK + r - j
    key_pos = (jnp.arange(nb)[:, None, None] - 1) * BLOCK + j[None]
    valid = ((diff >= 0) & (diff < WINDOW))[None] & (key_pos >= 0)
    bias = -alibi_slopes()[:, :, None, None] * diff.astype(jnp.float32)
    scores = jnp.where(valid[None, :, None, None], scores + bias, NEG_INF)

    sink = jnp.broadcast_to(sinks.astype(jnp.float32).reshape(1, 1, N_KV_HEADS, Q_PER_KV, 1, 1),
                            scores.shape[:-1] + (1,))
    probs = jax.nn.softmax(jnp.concatenate([scores, sink], axis=-1), axis=-1)[..., :-1]
    out = jnp.einsum('bnhgqs,bnshd->bnqhgd', probs.astype(v.dtype), vb)
    return out.reshape(b, l, ATTN_WIDTH)


def s5_mixer(u, lam_re, lam_im, log_dt, b_re, b_im, c_re, c_im, d_skip, w_glu, b_glu):
    bsz, l = u.shape[0], u.shape[1]
    uf = u.astype(jnp.float32)
    ug = uf.reshape(bsz, l, N_SSM_GROUPS, SSM_GROUP)
    dt = jnp.exp(log_dt.astype(jnp.float32))[:, None]
    lr = lam_re.astype(jnp.float32)
    li = lam_im.astype(jnp.float32)
    mag = jnp.exp(lr * dt)
    ang = li * dt
    ab_r = mag * jnp.cos(ang)
    ab_i = mag * jnp.sin(ang)
    nr = ab_r - 1.0
    ni = ab_i
    den = lr * lr + li * li
    f_r = (nr * lr + ni * li) / den
    f_i = (ni * lr - nr * li) / den
    br = b_re.astype(jnp.float32)
    bi = b_im.astype(jnp.float32)
    bb_r = f_r[..., None] * br - f_i[..., None] * bi
    bb_i = f_r[..., None] * bi + f_i[..., None] * br
    bu_r = jnp.einsum('blgc,gpc->blgp', ug, bb_r)
    bu_i = jnp.einsum('blgc,gpc->blgp', ug, bb_i)
    a_r = jnp.broadcast_to(ab_r, bu_r.shape)
    a_i = jnp.broadcast_to(ab_i, bu_i.shape)

    def combine(e1, e2):
        a1r, a1i, b1r, b1i = e1
        a2r, a2i, b2r, b2i = e2
        return (a2r * a1r - a2i * a1i,
                a2r * a1i + a2i * a1r,
                a2r * b1r - a2i * b1i + b2r,
                a2r * b1i + a2i * b1r + b2i)

    _, _, h_r, h_i = lax.associative_scan(combine, (a_r, a_i, bu_r, bu_i), axis=1)
    y = (jnp.einsum('blgp,gcp->blgc', h_r, c_re.astype(jnp.float32))
         - jnp.einsum('blgp,gcp->blgc', h_i, c_im.astype(jnp.float32)))
    y = y.reshape(bsz, l, SSM_WIDTH) + d_skip.astype(jnp.float32) * uf
    z = jax.nn.gelu(y).astype(u.dtype)
    return z * jax.nn.sigmoid(z @ w_glu + b_glu)


def _fwd_setup_inputs(seed: int = 0) -> dict:
    key = jax.random.key(seed)
    ks = jax.random.split(key, 32)
    f32 = jnp.float32

    def nrm(k, shape, scale):
        return jax.random.normal(k, shape, f32) * scale

    def gain(k, shape):
        return 1.0 + 0.05 * jax.random.normal(k, shape, f32)

    L, G, P, C = DEPTH, N_SSM_GROUPS, STATE, SSM_GROUP
    n_idx = jnp.arange(P, dtype=f32)[None, None, :]
    return {
        "x": nrm(ks[0], (BATCH, SEQ, D_MODEL), 1.0),
        "c": nrm(ks[1], (BATCH, D_MODEL), 1.0),
        "w_ada": nrm(ks[2], (L, D_MODEL, N_MOD * D_MODEL), 0.5 * D_MODEL ** -0.5),
        "b_ada": nrm(ks[3], (L, N_MOD * D_MODEL), 0.02),
        "pre_mix_g": gain(ks[4], (L, D_MODEL)),
        "w_in": nrm(ks[5], (L, D_MODEL, IN_WIDTH), D_MODEL ** -0.5),
        "attn_sinks": nrm(ks[6], (L, N_Q_HEADS), 0.5),
        "lam_re": -0.5 * jnp.exp(0.05 * jax.random.normal(ks[7], (L, G, P), f32)),
        "lam_im": math.pi * n_idx + 0.01 * jax.random.normal(ks[8], (L, G, P), f32),
        "log_dt": jax.random.uniform(ks[9], (L, G), f32, math.log(DT_MIN), math.log(DT_MAX)),
        "b_re": nrm(ks[10], (L, G, P, C), (2.0 * C) ** -0.5),
        "b_im": nrm(ks[11], (L, G, P, C), (2.0 * C) ** -0.5),
        "c_re": nrm(ks[12], (L, G, C, P), (2.0 * P) ** -0.5 * 4.0),
        "c_im": nrm(ks[13], (L, G, C, P), (2.0 * P) ** -0.5 * 4.0),
        "d_skip": nrm(ks[14], (L, SSM_WIDTH), 1.0),
        "w_glu": nrm(ks[15], (L, SSM_WIDTH, SSM_WIDTH), SSM_WIDTH ** -0.5),
        "b_glu": nrm(ks[16], (L, SSM_WIDTH), 0.02),
        "attn_out_g": gain(ks[17], (L, ATTN_WIDTH)),
        "ssm_out_g": gain(ks[18], (L, SSM_WIDTH)),
        "w_out": nrm(ks[19], (L, D_MODEL, D_MODEL), D_MODEL ** -0.5),
        "post_mix_g": gain(ks[20], (L, D_MODEL)),
        "pre_mlp_g": gain(ks[21], (L, D_MODEL)),
        "w_mlp_in": nrm(ks[22], (L, D_MODEL, D_FF), D_MODEL ** -0.5),
        "w_mlp_out": nrm(ks[23], (L, D_FF, D_MODEL), D_FF ** -0.5),
        "post_mlp_g": gain(ks[24], (L, D_MODEL)),
    }


def _fwd_reference(x, c, w_ada, b_ada, pre_mix_g, w_in, attn_sinks, lam_re, lam_im, log_dt,
              b_re, b_im, c_re, c_im, d_skip, w_glu, b_glu, attn_out_g, ssm_out_g, w_out,
              post_mix_g, pre_mlp_g, w_mlp_in, w_mlp_out, post_mlp_g):
    c_act = jax.nn.silu(c)
    split_pts = [ATTN_WIDTH, ATTN_WIDTH + KV_WIDTH, ATTN_WIDTH + 2 * KV_WIDTH]
    for i in range(DEPTH):
        mod = c_act @ w_ada[i] + b_ada[i]
        sh1, sc1, g1, sh2, sc2, g2 = [m[:, None, :] for m in jnp.split(mod, N_MOD, axis=-1)]

        h = rmsnorm(x, pre_mix_g[i]) * (1.0 + sc1) + sh1
        proj = h @ w_in[i]
        q, k, v, u = jnp.split(proj, split_pts, axis=-1)
        attn = sliding_window_attention(q, k, v, attn_sinks[i])
        ssm = s5_mixer(u, lam_re[i], lam_im[i], log_dt[i], b_re[i], b_im[i], c_re[i], c_im[i],
                       d_skip[i], w_glu[i], b_glu[i])
        heads = jnp.concatenate([rmsnorm(attn, attn_out_g[i]), rmsnorm(ssm, ssm_out_g[i])], axis=-1)
        mixed = heads @ w_out[i]
        x = x + g1 * rmsnorm(mixed, post_mix_g[i])

        h = rmsnorm(x, pre_mlp_g[i]) * (1.0 + sc2) + sh2
        f = jnp.square(jax.nn.relu(h @ w_mlp_in[i])) @ w_mlp_out[i]
        x = x + g2 * rmsnorm(f, post_mlp_g[i])
    return x


import jax as _jax
import jax.numpy as _jnp

TWIN_FORMAT = 'train_step'
FWD_PARAMS = ['x', 'c', 'w_ada', 'b_ada', 'pre_mix_g', 'w_in', 'attn_sinks', 'lam_re', 'lam_im', 'log_dt', 'b_re', 'b_im', 'c_re', 'c_im', 'd_skip', 'w_glu', 'b_glu', 'attn_out_g', 'ssm_out_g', 'w_out', 'post_mix_g', 'pre_mlp_g', 'w_mlp_in', 'w_mlp_out', 'post_mlp_g']
TWIN_WEIGHTS = ['w_ada', 'b_ada', 'pre_mix_g', 'w_in', 'attn_sinks', 'lam_re', 'lam_im', 'log_dt', 'b_re', 'b_im', 'c_re', 'c_im', 'd_skip', 'w_glu', 'b_glu', 'attn_out_g', 'ssm_out_g', 'w_out', 'post_mix_g', 'pre_mlp_g', 'w_mlp_in', 'w_mlp_out', 'post_mlp_g']
TWIN_DIFF_INPUT = 'x'
TWIN_INPUTS = ['x', 'c', 'w_ada', 'b_ada', 'pre_mix_g', 'w_in', 'attn_sinks', 'lam_re', 'lam_im', 'log_dt', 'b_re', 'b_im', 'c_re', 'c_im', 'd_skip', 'w_glu', 'b_glu', 'attn_out_g', 'ssm_out_g', 'w_out', 'post_mix_g', 'pre_mlp_g', 'w_mlp_in', 'w_mlp_out', 'post_mlp_g', 'loss_target', 'm_w_ada', 'm_b_ada', 'm_pre_mix_g', 'm_w_in', 'm_attn_sinks', 'm_lam_re', 'm_lam_im', 'm_log_dt', 'm_b_re', 'm_b_im', 'm_c_re', 'm_c_im', 'm_d_skip', 'm_w_glu', 'm_b_glu', 'm_attn_out_g', 'm_ssm_out_g', 'm_w_out', 'm_post_mix_g', 'm_pre_mlp_g', 'm_w_mlp_in', 'm_w_mlp_out', 'm_post_mlp_g', 'v_w_ada', 'v_b_ada', 'v_pre_mix_g', 'v_w_in', 'v_attn_sinks', 'v_lam_re', 'v_lam_im', 'v_log_dt', 'v_b_re', 'v_b_im', 'v_c_re', 'v_c_im', 'v_d_skip', 'v_w_glu', 'v_b_glu', 'v_attn_out_g', 'v_ssm_out_g', 'v_w_out', 'v_post_mix_g', 'v_pre_mlp_g', 'v_w_mlp_in', 'v_w_mlp_out', 'v_post_mlp_g']
TWIN_OUTPUTS = ['loss', 'grad_x', 'grad_w_ada', 'grad_b_ada', 'grad_pre_mix_g', 'grad_w_in', 'grad_attn_sinks', 'grad_lam_re', 'grad_lam_im', 'grad_log_dt', 'grad_b_re', 'grad_b_im', 'grad_c_re', 'grad_c_im', 'grad_d_skip', 'grad_w_glu', 'grad_b_glu', 'grad_attn_out_g', 'grad_ssm_out_g', 'grad_w_out', 'grad_post_mix_g', 'grad_pre_mlp_g', 'grad_w_mlp_in', 'grad_w_mlp_out', 'grad_post_mlp_g', 'delta_w_ada', 'delta_b_ada', 'delta_pre_mix_g', 'delta_w_in', 'delta_attn_sinks', 'delta_lam_re', 'delta_lam_im', 'delta_log_dt', 'delta_b_re', 'delta_b_im', 'delta_c_re', 'delta_c_im', 'delta_d_skip', 'delta_w_glu', 'delta_b_glu', 'delta_attn_out_g', 'delta_ssm_out_g', 'delta_w_out', 'delta_post_mix_g', 'delta_pre_mlp_g', 'delta_w_mlp_in', 'delta_w_mlp_out', 'delta_post_mlp_g', 'new_m_w_ada', 'new_m_b_ada', 'new_m_pre_mix_g', 'new_m_w_in', 'new_m_attn_sinks', 'new_m_lam_re', 'new_m_lam_im', 'new_m_log_dt', 'new_m_b_re', 'new_m_b_im', 'new_m_c_re', 'new_m_c_im', 'new_m_d_skip', 'new_m_w_glu', 'new_m_b_glu', 'new_m_attn_out_g', 'new_m_ssm_out_g', 'new_m_w_out', 'new_m_post_mix_g', 'new_m_pre_mlp_g', 'new_m_w_mlp_in', 'new_m_w_mlp_out', 'new_m_post_mlp_g', 'new_v_w_ada', 'new_v_b_ada', 'new_v_pre_mix_g', 'new_v_w_in', 'new_v_attn_sinks', 'new_v_lam_re', 'new_v_lam_im', 'new_v_log_dt', 'new_v_b_re', 'new_v_b_im', 'new_v_c_re', 'new_v_c_im', 'new_v_d_skip', 'new_v_w_glu', 'new_v_b_glu', 'new_v_attn_out_g', 'new_v_ssm_out_g', 'new_v_w_out', 'new_v_post_mix_g', 'new_v_pre_mlp_g', 'new_v_w_mlp_in', 'new_v_w_mlp_out', 'new_v_post_mlp_g']
TWIN_LEAF_KINDS = {'loss': 'loss', 'grad_x': 'grad_x', 'grad_w_ada': 'grad_w', 'grad_b_ada': 'grad_w', 'grad_pre_mix_g': 'grad_w', 'grad_w_in': 'grad_w', 'grad_attn_sinks': 'grad_w', 'grad_lam_re': 'grad_w', 'grad_lam_im': 'grad_w', 'grad_log_dt': 'grad_w', 'grad_b_re': 'grad_w', 'grad_b_im': 'grad_w', 'grad_c_re': 'grad_w', 'grad_c_im': 'grad_w', 'grad_d_skip': 'grad_w', 'grad_w_glu': 'grad_w', 'grad_b_glu': 'grad_w', 'grad_attn_out_g': 'grad_w', 'grad_ssm_out_g': 'grad_w', 'grad_w_out': 'grad_w', 'grad_post_mix_g': 'grad_w', 'grad_pre_mlp_g': 'grad_w', 'grad_w_mlp_in': 'grad_w', 'grad_w_mlp_out': 'grad_w', 'grad_post_mlp_g': 'grad_w', 'delta_w_ada': 'delta_w', 'delta_b_ada': 'delta_w', 'delta_pre_mix_g': 'delta_w', 'delta_w_in': 'delta_w', 'delta_attn_sinks': 'delta_w', 'delta_lam_re': 'delta_w', 'delta_lam_im': 'delta_w', 'delta_log_dt': 'delta_w', 'delta_b_re': 'delta_w', 'delta_b_im': 'delta_w', 'delta_c_re': 'delta_w', 'delta_c_im': 'delta_w', 'delta_d_skip': 'delta_w', 'delta_w_glu': 'delta_w', 'delta_b_glu': 'delta_w', 'delta_attn_out_g': 'delta_w', 'delta_ssm_out_g': 'delta_w', 'delta_w_out': 'delta_w', 'delta_post_mix_g': 'delta_w', 'delta_pre_mlp_g': 'delta_w', 'delta_w_mlp_in': 'delta_w', 'delta_w_mlp_out': 'delta_w', 'delta_post_mlp_g': 'delta_w', 'new_m_w_ada': 'new_m', 'new_m_b_ada': 'new_m', 'new_m_pre_mix_g': 'new_m', 'new_m_w_in': 'new_m', 'new_m_attn_sinks': 'new_m', 'new_m_lam_re': 'new_m', 'new_m_lam_im': 'new_m', 'new_m_log_dt': 'new_m', 'new_m_b_re': 'new_m', 'new_m_b_im': 'new_m', 'new_m_c_re': 'new_m', 'new_m_c_im': 'new_m', 'new_m_d_skip': 'new_m', 'new_m_w_glu': 'new_m', 'new_m_b_glu': 'new_m', 'new_m_attn_out_g': 'new_m', 'new_m_ssm_out_g': 'new_m', 'new_m_w_out': 'new_m', 'new_m_post_mix_g': 'new_m', 'new_m_pre_mlp_g': 'new_m', 'new_m_w_mlp_in': 'new_m', 'new_m_w_mlp_out': 'new_m', 'new_m_post_mlp_g': 'new_m', 'new_v_w_ada': 'new_v', 'new_v_b_ada': 'new_v', 'new_v_pre_mix_g': 'new_v', 'new_v_w_in': 'new_v', 'new_v_attn_sinks': 'new_v', 'new_v_lam_re': 'new_v', 'new_v_lam_im': 'new_v', 'new_v_log_dt': 'new_v', 'new_v_b_re': 'new_v', 'new_v_b_im': 'new_v', 'new_v_c_re': 'new_v', 'new_v_c_im': 'new_v', 'new_v_d_skip': 'new_v', 'new_v_w_glu': 'new_v', 'new_v_b_glu': 'new_v', 'new_v_attn_out_g': 'new_v', 'new_v_ssm_out_g': 'new_v', 'new_v_w_out': 'new_v', 'new_v_post_mix_g': 'new_v', 'new_v_pre_mlp_g': 'new_v', 'new_v_w_mlp_in': 'new_v', 'new_v_w_mlp_out': 'new_v', 'new_v_post_mlp_g': 'new_v'}


def _forward(args):
    return _fwd_reference(*[args[k] for k in FWD_PARAMS])


def _output_shape():
    out = _jax.eval_shape(lambda: _forward(_fwd_setup_inputs(0)))
    return out.shape, out.dtype

N_MICROBATCH = 1
ADAM_LR = 0.001
ADAM_B1 = 0.9
ADAM_B2 = 0.999
ADAM_EPS = 1e-08
ADAM_WD = 0.01
ADAM_STEP = 10
PER_EXAMPLE_BATCH_AXIS = {'x': 0, 'c': 0, 'loss_target': 0}
SHARED_INPUTS = []
_WEIGHT_DTYPES = {'w_ada': _jnp.float32, 'b_ada': _jnp.float32, 'pre_mix_g': _jnp.float32, 'w_in': _jnp.float32, 'attn_sinks': _jnp.float32, 'lam_re': _jnp.float32, 'lam_im': _jnp.float32, 'log_dt': _jnp.float32, 'b_re': _jnp.float32, 'b_im': _jnp.float32, 'c_re': _jnp.float32, 'c_im': _jnp.float32, 'd_skip': _jnp.float32, 'w_glu': _jnp.float32, 'b_glu': _jnp.float32, 'attn_out_g': _jnp.float32, 'ssm_out_g': _jnp.float32, 'w_out': _jnp.float32, 'post_mix_g': _jnp.float32, 'pre_mlp_g': _jnp.float32, 'w_mlp_in': _jnp.float32, 'w_mlp_out': _jnp.float32, 'post_mlp_g': _jnp.float32}
MOMENT_SCALE = {'w_ada': 3.064794e+00, 'b_ada': 5.195761e+00, 'pre_mix_g': 9.855751e-01, 'w_in': 1.564185e+00, 'attn_sinks': 8.748554e-01, 'lam_re': 2.628043e-01, 'lam_im': 2.016725e-01, 'log_dt': 2.193479e+01, 'b_re': 2.026650e-01, 'b_im': 1.885499e-01, 'c_re': 8.801187e-02, 'c_im': 9.730129e-02, 'd_skip': 1.622351e+00, 'w_glu': 3.238589e-01, 'b_glu': 7.498770e-01, 'attn_out_g': 2.296163e+00, 'ssm_out_g': 1.768285e+00, 'w_out': 2.017939e+00, 'post_mix_g': 4.491414e+00, 'pre_mlp_g': 5.046171e-01, 'w_mlp_in': 3.831416e-01, 'w_mlp_out': 1.423510e+00, 'post_mlp_g': 4.132816e+00}


def _to_microbatches(a, axis):
    t = _jnp.moveaxis(a, axis, 0)
    t = t.reshape((N_MICROBATCH, t.shape[0] // N_MICROBATCH) + t.shape[1:])
    return _jnp.moveaxis(t, 1, axis + 1)


def setup_inputs(seed: int = 0) -> dict:
    inp = _fwd_setup_inputs(seed)
    key = _jax.random.fold_in(_jax.random.key(seed), 7919)
    shape, _ = _output_shape()
    out = dict(inp)
    out["loss_target"] = _jax.random.normal(_jax.random.fold_in(key, 0), shape, _jnp.float32)
    for i, name in enumerate(TWIN_WEIGHTS):
        w = inp[name].astype(_jnp.float32)
        if MOMENT_SCALE is None:
            s = _jnp.sqrt(_jnp.mean(_jnp.square(w)) + 1e-30)
        else:
            s = MOMENT_SCALE[name]
        km, kv = _jax.random.split(_jax.random.fold_in(key, i + 1))
        out[name] = w
        out["m_" + name] = s * _jax.random.normal(km, w.shape, _jnp.float32)
        out["v_" + name] = (s * s) * _jax.random.uniform(kv, w.shape, _jnp.float32, 0.5, 1.5)
    if N_MICROBATCH > 1:
        for name, axis in PER_EXAMPLE_BATCH_AXIS.items():
            out[name] = _to_microbatches(out[name], axis)
    return {'x': out['x'], 'c': out['c'], 'w_ada': out['w_ada'], 'b_ada': out['b_ada'], 'pre_mix_g': out['pre_mix_g'], 'w_in': out['w_in'], 'attn_sinks': out['attn_sinks'], 'lam_re': out['lam_re'], 'lam_im': out['lam_im'], 'log_dt': out['log_dt'], 'b_re': out['b_re'], 'b_im': out['b_im'], 'c_re': out['c_re'], 'c_im': out['c_im'], 'd_skip': out['d_skip'], 'w_glu': out['w_glu'], 'b_glu': out['b_glu'], 'attn_out_g': out['attn_out_g'], 'ssm_out_g': out['ssm_out_g'], 'w_out': out['w_out'], 'post_mix_g': out['post_mix_g'], 'pre_mlp_g': out['pre_mlp_g'], 'w_mlp_in': out['w_mlp_in'], 'w_mlp_out': out['w_mlp_out'], 'post_mlp_g': out['post_mlp_g'], 'loss_target': out['loss_target'], 'm_w_ada': out['m_w_ada'], 'm_b_ada': out['m_b_ada'], 'm_pre_mix_g': out['m_pre_mix_g'], 'm_w_in': out['m_w_in'], 'm_attn_sinks': out['m_attn_sinks'], 'm_lam_re': out['m_lam_re'], 'm_lam_im': out['m_lam_im'], 'm_log_dt': out['m_log_dt'], 'm_b_re': out['m_b_re'], 'm_b_im': out['m_b_im'], 'm_c_re': out['m_c_re'], 'm_c_im': out['m_c_im'], 'm_d_skip': out['m_d_skip'], 'm_w_glu': out['m_w_glu'], 'm_b_glu': out['m_b_glu'], 'm_attn_out_g': out['m_attn_out_g'], 'm_ssm_out_g': out['m_ssm_out_g'], 'm_w_out': out['m_w_out'], 'm_post_mix_g': out['m_post_mix_g'], 'm_pre_mlp_g': out['m_pre_mlp_g'], 'm_w_mlp_in': out['m_w_mlp_in'], 'm_w_mlp_out': out['m_w_mlp_out'], 'm_post_mlp_g': out['m_post_mlp_g'], 'v_w_ada': out['v_w_ada'], 'v_b_ada': out['v_b_ada'], 'v_pre_mix_g': out['v_pre_mix_g'], 'v_w_in': out['v_w_in'], 'v_attn_sinks': out['v_attn_sinks'], 'v_lam_re': out['v_lam_re'], 'v_lam_im': out['v_lam_im'], 'v_log_dt': out['v_log_dt'], 'v_b_re': out['v_b_re'], 'v_b_im': out['v_b_im'], 'v_c_re': out['v_c_re'], 'v_c_im': out['v_c_im'], 'v_d_skip': out['v_d_skip'], 'v_w_glu': out['v_w_glu'], 'v_b_glu': out['v_b_glu'], 'v_attn_out_g': out['v_attn_out_g'], 'v_ssm_out_g': out['v_ssm_out_g'], 'v_w_out': out['v_w_out'], 'v_post_mix_g': out['v_post_mix_g'], 'v_pre_mlp_g': out['v_pre_mlp_g'], 'v_w_mlp_in': out['v_w_mlp_in'], 'v_w_mlp_out': out['v_w_mlp_out'], 'v_post_mlp_g': out['v_post_mlp_g']}


def _loss(weights, diff, rest, loss_target):
    with _jax.named_scope("forward"):
        args = {**rest, TWIN_DIFF_INPUT: diff, **{k: w.astype(_WEIGHT_DTYPES[k]) for k, w in weights.items()}}
        y = _forward(args)
    with _jax.named_scope("loss_head"):
        err = _jnp.square(y.astype(_jnp.float32) - loss_target)
        return 0.5 * _jnp.sum(_jnp.mean(err, axis=-1)) if err.ndim else 0.5 * err


def _adamw(w, g, m, v):
    m = ADAM_B1 * m + (1.0 - ADAM_B1) * g
    v = ADAM_B2 * v + (1.0 - ADAM_B2) * _jnp.square(g)
    m_hat = m / (1.0 - ADAM_B1 ** ADAM_STEP)
    v_hat = v / (1.0 - ADAM_B2 ** ADAM_STEP)
    delta = -ADAM_LR * (m_hat / (_jnp.sqrt(v_hat) + ADAM_EPS) + ADAM_WD * w)
    return delta, m, v


def reference(x, c, w_ada, b_ada, pre_mix_g, w_in, attn_sinks, lam_re, lam_im, log_dt, b_re, b_im, c_re, c_im, d_skip, w_glu, b_glu, attn_out_g, ssm_out_g, w_out, post_mix_g, pre_mlp_g, w_mlp_in, w_mlp_out, post_mlp_g, loss_target, m_w_ada, m_b_ada, m_pre_mix_g, m_w_in, m_attn_sinks, m_lam_re, m_lam_im, m_log_dt, m_b_re, m_b_im, m_c_re, m_c_im, m_d_skip, m_w_glu, m_b_glu, m_attn_out_g, m_ssm_out_g, m_w_out, m_post_mix_g, m_pre_mlp_g, m_w_mlp_in, m_w_mlp_out, m_post_mlp_g, v_w_ada, v_b_ada, v_pre_mix_g, v_w_in, v_attn_sinks, v_lam_re, v_lam_im, v_log_dt, v_b_re, v_b_im, v_c_re, v_c_im, v_d_skip, v_w_glu, v_b_glu, v_attn_out_g, v_ssm_out_g, v_w_out, v_post_mix_g, v_pre_mlp_g, v_w_mlp_in, v_w_mlp_out, v_post_mlp_g):
    given = dict(x=x, c=c, w_ada=w_ada, b_ada=b_ada, pre_mix_g=pre_mix_g, w_in=w_in, attn_sinks=attn_sinks, lam_re=lam_re, lam_im=lam_im, log_dt=log_dt, b_re=b_re, b_im=b_im, c_re=c_re, c_im=c_im, d_skip=d_skip, w_glu=w_glu, b_glu=b_glu, attn_out_g=attn_out_g, ssm_out_g=ssm_out_g, w_out=w_out, post_mix_g=post_mix_g, pre_mlp_g=pre_mlp_g, w_mlp_in=w_mlp_in, w_mlp_out=w_mlp_out, post_mlp_g=post_mlp_g, loss_target=loss_target, m_w_ada=m_w_ada, m_b_ada=m_b_ada, m_pre_mix_g=m_pre_mix_g, m_w_in=m_w_in, m_attn_sinks=m_attn_sinks, m_lam_re=m_lam_re, m_lam_im=m_lam_im, m_log_dt=m_log_dt, m_b_re=m_b_re, m_b_im=m_b_im, m_c_re=m_c_re, m_c_im=m_c_im, m_d_skip=m_d_skip, m_w_glu=m_w_glu, m_b_glu=m_b_glu, m_attn_out_g=m_attn_out_g, m_ssm_out_g=m_ssm_out_g, m_w_out=m_w_out, m_post_mix_g=m_post_mix_g, m_pre_mlp_g=m_pre_mlp_g, m_w_mlp_in=m_w_mlp_in, m_w_mlp_out=m_w_mlp_out, m_post_mlp_g=m_post_mlp_g, v_w_ada=v_w_ada, v_b_ada=v_b_ada, v_pre_mix_g=v_pre_mix_g, v_w_in=v_w_in, v_attn_sinks=v_attn_sinks, v_lam_re=v_lam_re, v_lam_im=v_lam_im, v_log_dt=v_log_dt, v_b_re=v_b_re, v_b_im=v_b_im, v_c_re=v_c_re, v_c_im=v_c_im, v_d_skip=v_d_skip, v_w_glu=v_w_glu, v_b_glu=v_b_glu, v_attn_out_g=v_attn_out_g, v_ssm_out_g=v_ssm_out_g, v_w_out=v_w_out, v_post_mix_g=v_post_mix_g, v_pre_mlp_g=v_pre_mlp_g, v_w_mlp_in=v_w_mlp_in, v_w_mlp_out=v_w_mlp_out, v_post_mlp_g=v_post_mlp_g)
    weights = {n: given[n] for n in TWIN_WEIGHTS}
    shared = {n: given[n] for n in SHARED_INPUTS}
    per_example = {n: given[n] for n in ['x', 'c']}
    grad_fn = _jax.value_and_grad(_loss, argnums=(0, 1))

    def one_microbatch(ex, loss_target):
        ex = dict(ex)
        diff = ex.pop(TWIN_DIFF_INPUT)
        return grad_fn(weights, diff, {**shared, **ex}, loss_target)

    if N_MICROBATCH == 1:
        loss, (grad_w, grad_x) = one_microbatch(per_example, given["loss_target"])
    else:
        def body(carry, xs):
            loss_sum, grad_sum = carry
            l_k, (gw_k, gx_k) = one_microbatch(xs[0], xs[1])
            with _jax.named_scope("update"):
                return (loss_sum + l_k, _jax.tree.map(_jnp.add, grad_sum, gw_k)), gx_k

        init = (_jnp.zeros((), _jnp.float32), _jax.tree.map(_jnp.zeros_like, weights))
        (loss, grad_w), grad_x = _jax.lax.scan(body, init, (per_example, given["loss_target"]))
    with _jax.named_scope("update"):
        delta_w, new_m, new_v = {}, {}, {}
        for n in TWIN_WEIGHTS:
            delta_w[n], new_m[n], new_v[n] = _adamw(weights[n], grad_w[n], given["m_" + n], given["v_" + n])
    return (loss, grad_x, *[grad_w[n] for n in TWIN_WEIGHTS], *[delta_w[n] for n in TWIN_WEIGHTS],
            *[new_m[n] for n in TWIN_WEIGHTS], *[new_v[n] for n in TWIN_WEIGHTS])
```

```python
import functools
import math

import numpy as np
import jax
import jax.numpy as jnp
from jax import lax
from jax.experimental import pallas as pl
from jax.experimental.pallas import tpu as pltpu

F32 = jnp.float32
BF16 = jnp.bfloat16

D_MODEL = 1024
ATTN_W = 512
SSM_W = 512
HEAD_DIM = 64
N_Q = 8
N_KV = 2
Q_PER_KV = 4
KV_W = 128
WINDOW = 128
BLOCK = 128
N_GROUPS = 32
GROUP_W = 16
STATE = 64
N_CH = N_GROUPS * STATE
HALF_CH = N_CH // 2
D_FF = 4096
IN_W = 1280
N_MOD = 6
EPS = 1e-6
NEG_INF = -1e30

ADAM_LR = 0.001
ADAM_B1 = 0.9
ADAM_B2 = 0.999
ADAM_EPS = 1e-08
ADAM_WD = 0.01
ADAM_STEP = 10

ROW_TILE = 256
CHUNK = 256
SEGS = 8
STEPS = CHUNK // SEGS
STRIP = 1024
VMEM_LIMIT_V7X = 56 * 1024 * 1024
LANE = 128
SUBLANE = 8

GELU_K0 = math.sqrt(2.0 / math.pi)
GELU_K1 = 0.044715

V_SH1, V_SC1, V_G1, V_SH2, V_SC2, V_G2, V_PRE_MIX, V_POST_MIX, V_PRE_MLP, V_POST_MLP = range(10)
H_ATTN_G, H_SSM_G, H_DSKIP, H_BGLU = range(4)

HBM = pl.BlockSpec(memory_space=pltpu.HBM)
MESH_ID = pl.DeviceIdType.MESH


def _nn(a, b):
    return lax.dot_general(a, b, (((1,), (0,)), ((), ())), preferred_element_type=F32)


def _nt(a, b):
    return lax.dot_general(a, b, (((1,), (1,)), ((), ())), preferred_element_type=F32)


def _tn(a, b):
    return lax.dot_general(a, b, (((0,), (0,)), ((), ())), preferred_element_type=F32)


def _params(sem):
    return pltpu.CompilerParams(dimension_semantics=sem, vmem_limit_bytes=VMEM_LIMIT_V7X)


def _rms_fwd(x, g):
    r = lax.rsqrt(jnp.mean(x * x, axis=-1, keepdims=True) + EPS)
    xh = x * r
    return xh * g, xh, r


def _rms_bwd(dy, xh, r, g):
    dxh = dy * g
    dx = r * (dxh - xh * jnp.mean(dxh * xh, axis=-1, keepdims=True))
    return dx, dy * xh


def _colsum(t):
    return jnp.sum(t, axis=0, keepdims=True)


def _gelu(y):
    t = jnp.tanh(GELU_K0 * (y + GELU_K1 * (y * y * y)))
    return 0.5 * y * (1.0 + t), t


def _gelu_grad(y, t):
    return 0.5 * (1.0 + t) + 0.5 * y * (1.0 - t * t) * GELU_K0 * (1.0 + 3.0 * GELU_K1 * y * y)


def _alibi_slopes():
    return [float(s) for s in 2.0 ** (-8.0 * np.arange(1, N_Q + 1) / N_Q)]


def _pick_rows(rows, bytes_per_row, budget):
    t = rows
    while t % (2 * SUBLANE) == 0 and t * bytes_per_row > budget:
        t //= 2
    return t


def _load_once(step, pairs, sems):
    @pl.when(step == 0)
    def _():
        cps = [pltpu.make_async_copy(src, dst, sems.at[k]) for k, (src, dst) in enumerate(pairs)]
        for cp in cps:
            cp.start()
        for cp in cps:
            cp.wait()


_GROUPS = {
    "all": ([(0, 0, 1), (0, 1, 0), (0, 1, 1), (1, 0, 0), (1, 0, 1), (1, 1, 0), (1, 1, 1)], (4, 2, 1), 8),
    "chips": ([(1, 0, 0), (0, 1, 0), (1, 1, 0)], (2, 1, 0), 4),
    "pair": ([(0, 0, 1)], (0, 0, 1), 2),
}


def _flip(v, f):
    return 1 - v if f else v


def _gather(arrs, kind, name):
    masks, wts, n = _GROUPS[kind]
    na, nm = len(arrs), len(masks)

    def body(*refs):
        ins, outs = refs[:na], refs[na:2 * na]
        ssem, rsem, lsem = refs[2 * na:]
        x, y, c = lax.axis_index("x"), lax.axis_index("y"), lax.axis_index("c")
        me = wts[0] * x + wts[1] * y + wts[2] * c
        local = [pltpu.make_async_copy(ins[k], outs[k].at[:, pl.ds(me, 1)], lsem.at[k]) for k in range(na)]
        for cp in local:
            cp.start()
        remote = []
        for k in range(na):
            for mi, (fx, fy, fc) in enumerate(masks):
                peer = (_flip(x, fx), _flip(y, fy), _flip(c, fc))
                remote.append(pltpu.make_async_remote_copy(
                    src_ref=ins[k], dst_ref=outs[k].at[:, pl.ds(me, 1)],
                    send_sem=ssem.at[k * nm + mi], recv_sem=rsem.at[k * nm + mi],
                    device_id=peer, device_id_type=MESH_ID))
        for cp in remote:
            cp.start()
        for cp in remote:
            cp.wait()
        for cp in local:
            cp.wait()

    outs = pl.pallas_call(
        body, name=name,
        out_shape=[jax.ShapeDtypeStruct((a.shape[0], n) + a.shape[2:], a.dtype) for a in arrs],
        in_specs=[HBM] * na, out_specs=[HBM] * na,
        scratch_shapes=[pltpu.SemaphoreType.DMA((na * nm,)), pltpu.SemaphoreType.DMA((na * nm,)),
                        pltpu.SemaphoreType.DMA((na,))],
    )(*arrs)
    return list(outs)


def _gather_weights(arrs, name):
    masks, wts, n = _GROUPS["chips"]
    na, nm = len(arrs), len(masks)
    nl = arrs[0].shape[0]
    half = nl // 2

    def body(*refs):
        ins, outs = refs[:na], refs[na:2 * na]
        ssem, rsem, fssem, frsem, lsem = refs[2 * na:]
        x, y, c = lax.axis_index("x"), lax.axis_index("y"), lax.axis_index("c")
        me = 2 * x + y
        local = [pltpu.make_async_copy(ins[k], outs[k].at[:, pl.ds(me, 1)], lsem.at[k]) for k in range(na)]
        for cp in local:
            cp.start()
        for cc in range(2):
            @pl.when(c == cc)
            def _(cc=cc):
                own, other = pl.ds(cc * half, half), pl.ds((1 - cc) * half, half)
                chips = [(_flip(x, fx), _flip(y, fy)) for fx, fy, _ in masks]
                sends = []
                for k in range(na):
                    for mi, (px, py) in enumerate(chips):
                        sends.append(pltpu.make_async_remote_copy(
                            src_ref=ins[k].at[own], dst_ref=outs[k].at[own, pl.ds(me, 1)],
                            send_sem=ssem.at[k * nm + mi], recv_sem=rsem.at[k * nm + mi],
                            device_id=(px, py, c), device_id_type=MESH_ID))
                for cp in sends:
                    cp.start()
                passed = []
                for k in range(na):
                    for mi, (px, py) in enumerate(chips):
                        slot = pl.ds(2 * px + py, 1)
                        pltpu.make_async_remote_copy(
                            src_ref=ins[k].at[own], dst_ref=outs[k].at[own, slot],
                            send_sem=ssem.at[k * nm + mi], recv_sem=rsem.at[k * nm + mi],
                            device_id=(px, py, c), device_id_type=MESH_ID).wait_recv()
                        fw = pltpu.make_async_remote_copy(
                            src_ref=outs[k].at[own, slot], dst_ref=outs[k].at[own, slot],
                            send_sem=fssem.at[k * nm + mi], recv_sem=frsem.at[k * nm + mi],
                            device_id=(x, y, 1 - c), device_id_type=MESH_ID)
                        fw.start()
                        passed.append(fw)
                for k in range(na):
                    for mi, (px, py) in enumerate(chips):
                        slot = pl.ds(2 * px + py, 1)
                        pltpu.make_async_remote_copy(
                            src_ref=outs[k].at[other, slot], dst_ref=outs[k].at[other, slot],
                            send_sem=fssem.at[k * nm + mi], recv_sem=frsem.at[k * nm + mi],
                            device_id=(x, y, 1 - c), device_id_type=MESH_ID).wait_recv()
                for cp in sends + passed:
                    cp.wait_send()
        for cp in local:
            cp.wait()

    outs = pl.pallas_call(
        body, name=name,
        out_shape=[jax.ShapeDtypeStruct((a.shape[0], n) + a.shape[2:], a.dtype) for a in arrs],
        in_specs=[HBM] * na, out_specs=[HBM] * na,
        scratch_shapes=[pltpu.SemaphoreType.DMA((na * nm,)) for _ in range(4)] + [pltpu.SemaphoreType.DMA((na,))],
    )(*arrs)
    return list(outs)


def _exchange_chips(arrs, name):
    masks, wts, n = _GROUPS["chips"]
    na, nm = len(arrs), len(masks)

    def body(*refs):
        ins, outs = refs[:na], refs[na:2 * na]
        ssem, rsem, lsem = refs[2 * na:]
        x, y, c = lax.axis_index("x"), lax.axis_index("y"), lax.axis_index("c")
        me = 2 * x + y
        mine = pl.ds(me, 1)
        local = [pltpu.make_async_copy(ins[k].at[:, mine], outs[k].at[:, mine], lsem.at[k]) for k in range(na)]
        for cp in local:
            cp.start()
        remote = []
        for k in range(na):
            for mi, (fx, fy, _) in enumerate(masks):
                px, py = _flip(x, fx), _flip(y, fy)
                remote.append(pltpu.make_async_remote_copy(
                    src_ref=ins[k].at[:, pl.ds(2 * px + py, 1)], dst_ref=outs[k].at[:, mine],
                    send_sem=ssem.at[k * nm + mi], recv_sem=rsem.at[k * nm + mi],
                    device_id=(px, py, c), device_id_type=MESH_ID))
        for cp in remote:
            cp.start()
        for cp in remote:
            cp.wait()
        for cp in local:
            cp.wait()

    outs = pl.pallas_call(
        body, name=name,
        out_shape=[jax.ShapeDtypeStruct(a.shape, a.dtype) for a in arrs],
        in_specs=[HBM] * na, out_specs=[HBM] * na,
        scratch_shapes=[pltpu.SemaphoreType.DMA((na * nm,)), pltpu.SemaphoreType.DMA((na * nm,)),
                        pltpu.SemaphoreType.DMA((na,))],
    )(*arrs)
    return list(outs)


def _sum_slots(gs, name):
    a, s, r, c = gs.shape
    tr = _pick_rows(r, c * 4 * (s + 1), 4 << 20)

    def body(g_ref, o_ref):
        acc = g_ref[0, 0].astype(F32)
        for j in range(1, s):
            acc = acc + g_ref[0, j].astype(F32)
        o_ref[0] = acc

    return pl.pallas_call(
        body, name=name, grid=(a, r // tr),
        in_specs=[pl.BlockSpec((1, s, tr, c), lambda i, j: (i, 0, j, 0))],
        out_specs=pl.BlockSpec((1, tr, c), lambda i, j: (i, j, 0)),
        out_shape=jax.ShapeDtypeStruct((a, r, c), F32),
        compiler_params=_params(("parallel", "parallel")),
    )(gs)


def _adamw(gs, w, m, v, name):
    a, s, r, c = gs.shape
    tr = _pick_rows(r, c * 4 * (s + 7), 6 << 20)

    def body(g_ref, w_ref, m_ref, v_ref, go_ref, d_ref, mo_ref, vo_ref):
        g = g_ref[0, 0].astype(F32)
        for j in range(1, s):
            g = g + g_ref[0, j].astype(F32)
        mn = ADAM_B1 * m_ref[0] + (1.0 - ADAM_B1) * g
        vn = ADAM_B2 * v_ref[0] + (1.0 - ADAM_B2) * jnp.square(g)
        m_hat = mn / (1.0 - ADAM_B1 ** ADAM_STEP)
        v_hat = vn / (1.0 - ADAM_B2 ** ADAM_STEP)
        go_ref[0] = g
        d_ref[0] = -ADAM_LR * (m_hat / (jnp.sqrt(v_hat) + ADAM_EPS) + ADAM_WD * w_ref[0])
        mo_ref[0] = mn
        vo_ref[0] = vn

    blk = pl.BlockSpec((1, tr, c), lambda i, j: (i, j, 0))
    sds = jax.ShapeDtypeStruct((a, r, c), F32)
    return pl.pallas_call(
        body, name=name, grid=(a, r // tr),
        in_specs=[pl.BlockSpec((1, s, tr, c), lambda i, j: (i, 0, j, 0)), blk, blk, blk],
        out_specs=[blk, blk, blk, blk], out_shape=[sds, sds, sds, sds],
        compiler_params=_params(("parallel", "parallel")),
    )(gs, w, m, v)


def _ada_forward(c_all, w_ada, b_sh):
    nl, d, w = w_ada.shape
    tw = 512

    def body(c_ref, w_ref, b_ref, o_ref):
        cv = c_ref[...]
        act = (cv * jax.nn.sigmoid(cv)).astype(BF16)
        o_ref[0] = _nn(act, w_ref[0].astype(BF16)) + b_ref[0]

    return pl.pallas_call(
        body, name="ada_forward", grid=(nl, w // tw),
        in_specs=[pl.BlockSpec((8, d), lambda l, j: (0, 0)),
                  pl.BlockSpec((1, d, tw), lambda l, j: (l, 0, j)),
                  pl.BlockSpec((1, 1, tw), lambda l, j: (l, 0, j))],
        out_specs=pl.BlockSpec((1, 8, tw), lambda l, j: (l, 0, j)),
        out_shape=jax.ShapeDtypeStruct((nl, 8, w), F32),
        compiler_params=_params(("parallel", "parallel")),
    )(c_all, w_ada, b_sh)


def _ada_weight_grad(c_all_t, dmod):
    nl, nb, w = dmod.shape
    d = c_all_t.shape[0]
    tw = 512

    def body(c_ref, g_ref, o_ref):
        cv = c_ref[...]
        act = cv * jax.nn.sigmoid(cv)
        gv = g_ref[0]
        acc = act[:, 0:1] * gv[0:1, :]
        for b in range(1, nb):
            acc = acc + act[:, b:b + 1] * gv[b:b + 1, :]
        o_ref[0] = acc

    return pl.pallas_call(
        body, name="ada_weight_grad", grid=(nl, w // tw),
        in_specs=[pl.BlockSpec((d, nb), lambda l, j: (0, 0)),
                  pl.BlockSpec((1, nb, tw), lambda l, j: (l, 0, j))],
        out_specs=pl.BlockSpec((1, d, tw), lambda l, j: (l, 0, j)),
        out_shape=jax.ShapeDtypeStruct((nl, d, w), F32),
        compiler_params=_params(("parallel", "parallel")),
    )(c_all_t, dmod)


def _in_proj_fwd(x, vec, w_in_t, layer):
    seq = x.shape[0]
    tm = ROW_TILE

    def body(x_ref, vec_ref, w_ref, q_ref, kv_ref, u4_ref, h_ref):
        n, _, _ = _rms_fwd(x_ref[...], vec_ref[0, V_PRE_MIX:V_PRE_MIX + 1, :])
        h = (n * (1.0 + vec_ref[0, V_SC1:V_SC1 + 1, :]) + vec_ref[0, V_SH1:V_SH1 + 1, :]).astype(BF16)
        h_ref[...] = h
        proj = _nt(h, w_ref[0])
        q_ref[...] = proj[:, :ATTN_W].astype(BF16)
        kv_ref[...] = proj[:, ATTN_W:ATTN_W + 2 * KV_W].astype(BF16)
        u0 = ATTN_W + 2 * KV_W
        for j in range(4):
            u4_ref[j] = proj[:, u0 + j * LANE:u0 + (j + 1) * LANE]

    return pl.pallas_call(
        body, name="in_proj_fwd", grid=(seq // tm,),
        in_specs=[pl.BlockSpec((tm, D_MODEL), lambda i: (i, 0)),
                  pl.BlockSpec((1, 16, D_MODEL), lambda i: (layer, 0, 0)),
                  pl.BlockSpec((1, IN_W, D_MODEL), lambda i: (layer, 0, 0))],
        out_specs=[pl.BlockSpec((tm, ATTN_W), lambda i: (i, 0)),
                   pl.BlockSpec((tm, 2 * KV_W), lambda i: (i, 0)),
                   pl.BlockSpec((4, tm, LANE), lambda i: (0, i, 0)),
                   pl.BlockSpec((tm, D_MODEL), lambda i: (i, 0))],
        out_shape=[jax.ShapeDtypeStruct((seq, ATTN_W), BF16), jax.ShapeDtypeStruct((seq, 2 * KV_W), BF16),
                   jax.ShapeDtypeStruct((4, seq, LANE), F32), jax.ShapeDtypeStruct((seq, D_MODEL), BF16)],
        compiler_params=_params(("parallel",)),
    )(x, vec, w_in_t)


def _in_proj_bwd(dx1, dq, dkv, du4, x, vec, w_in_t, layer):
    seq = x.shape[0]
    tm = ROW_TILE

    def body(dx1_ref, dq_ref, dkv_ref, du4_ref, x_ref, vec_ref, w_ref, dx_ref, dp_ref, dvec_ref):
        i = pl.program_id(0)

        @pl.when(i == 0)
        def _():
            dvec_ref[...] = jnp.zeros_like(dvec_ref)

        dproj = jnp.concatenate([dq_ref[...], dkv_ref[...]] + [du4_ref[j] for j in range(4)], axis=1).astype(BF16)
        dp_ref[...] = dproj
        dh = _nn(dproj, w_ref[0])
        g = vec_ref[0, V_PRE_MIX:V_PRE_MIX + 1, :]
        n, xh, r = _rms_fwd(x_ref[...], g)
        dn = dh * (1.0 + vec_ref[0, V_SC1:V_SC1 + 1, :])
        dxn, dg_rows = _rms_bwd(dn, xh, r, g)
        dx_ref[...] = dx1_ref[...] + dxn
        dvec_ref[V_SH1:V_SH1 + 1, :] += _colsum(dh)
        dvec_ref[V_SC1:V_SC1 + 1, :] += _colsum(dh * n)
        dvec_ref[V_PRE_MIX:V_PRE_MIX + 1, :] += _colsum(dg_rows)

    row = pl.BlockSpec((tm, D_MODEL), lambda i: (i, 0))
    return pl.pallas_call(
        body, name="in_proj_bwd", grid=(seq // tm,),
        in_specs=[row, pl.BlockSpec((tm, ATTN_W), lambda i: (i, 0)), pl.BlockSpec((tm, 2 * KV_W), lambda i: (i, 0)),
                  pl.BlockSpec((4, tm, LANE), lambda i: (0, i, 0)), row,
                  pl.BlockSpec((1, 16, D_MODEL), lambda i: (layer, 0, 0)),
                  pl.BlockSpec((1, IN_W, D_MODEL), lambda i: (layer, 0, 0))],
        out_specs=[row, pl.BlockSpec((tm, IN_W), lambda i: (i, 0)), pl.BlockSpec((16, D_MODEL), lambda i: (0, 0))],
        out_shape=[jax.ShapeDtypeStruct((seq, D_MODEL), F32), jax.ShapeDtypeStruct((seq, IN_W), BF16),
                   jax.ShapeDtypeStruct((16, D_MODEL), F32)],
        compiler_params=_params(("arbitrary",)),
    )(dx1, dq, dkv, du4, x, vec, w_in_t)


def _heads(attn_ref, s4_ref, vec512_ref):
    ga = vec512_ref[0, H_ATTN_G:H_ATTN_G + 1, :]
    gs = vec512_ref[0, H_SSM_G:H_SSM_G + 1, :]
    sv = jnp.concatenate([s4_ref[j] for j in range(4)], axis=1)
    na, ah, ar = _rms_fwd(attn_ref[...], ga)
    ns, sh, sr = _rms_fwd(sv, gs)
    return jnp.concatenate([na, ns], axis=1), (ah, ar, ga), (sh, sr, gs)


def _out_proj_fwd(x, attn, s4, vec, vec512, w_out, layer):
    seq = x.shape[0]
    tm = ROW_TILE

    def body(x_ref, attn_ref, s4_ref, vec_ref, vec512_ref, w_ref, x1_ref):
        heads, _, _ = _heads(attn_ref, s4_ref, vec512_ref)
        mixed = _nn(heads.astype(BF16), w_ref[0])
        nm, _, _ = _rms_fwd(mixed, vec_ref[0, V_POST_MIX:V_POST_MIX + 1, :])
        x1_ref[...] = x_ref[...] + vec_ref[0, V_G1:V_G1 + 1, :] * nm

    row = pl.BlockSpec((tm, D_MODEL), lambda i: (i, 0))
    return pl.pallas_call(
        body, name="out_proj_fwd", grid=(seq // tm,),
        in_specs=[row, pl.BlockSpec((tm, ATTN_W), lambda i: (i, 0)), pl.BlockSpec((4, tm, LANE), lambda i: (0, i, 0)),
                  pl.BlockSpec((1, 16, D_MODEL), lambda i: (layer, 0, 0)),
                  pl.BlockSpec((1, 8, SSM_W), lambda i: (layer, 0, 0)),
                  pl.BlockSpec((1, D_MODEL, D_MODEL), lambda i: (layer, 0, 0))],
        out_specs=row, out_shape=jax.ShapeDtypeStruct((seq, D_MODEL), F32),
        compiler_params=_params(("parallel",)),
    )(x, attn, s4, vec, vec512, w_out)


def _out_proj_bwd(dx1, attn, s4, vec, vec512, w_out, layer):
    seq = dx1.shape[0]
    tm = ROW_TILE

    def body(dx1_ref, attn_ref, s4_ref, vec_ref, vec512_ref, w_ref,
             dattn_ref, ds4_ref, heads_ref, dmixed_ref, dvec_ref, dvec512_ref):
        i = pl.program_id(0)

        @pl.when(i == 0)
        def _():
            dvec_ref[...] = jnp.zeros_like(dvec_ref)
            dvec512_ref[...] = jnp.zeros_like(dvec512_ref)

        heads, (ah, ar, ga), (sh, sr, gs) = _heads(attn_ref, s4_ref, vec512_ref)
        hb = heads.astype(BF16)
        heads_ref[...] = hb
        gm = vec_ref[0, V_POST_MIX:V_POST_MIX + 1, :]
        nm, mh, mr = _rms_fwd(_nn(hb, w_ref[0]), gm)
        dx1v = dx1_ref[...]
        dvec_ref[V_G1:V_G1 + 1, :] += _colsum(dx1v * nm)
        dmixed, dgm_rows = _rms_bwd(dx1v * vec_ref[0, V_G1:V_G1 + 1, :], mh, mr, gm)
        dvec_ref[V_POST_MIX:V_POST_MIX + 1, :] += _colsum(dgm_rows)
        dmb = dmixed.astype(BF16)
        dmixed_ref[...] = dmb
        dheads = _nt(dmb, w_ref[0])
        dattn, dga_rows = _rms_bwd(dheads[:, :ATTN_W], ah, ar, ga)
        ds, dgs_rows = _rms_bwd(dheads[:, ATTN_W:], sh, sr, gs)
        dattn_ref[...] = dattn
        for j in range(4):
            ds4_ref[j] = ds[:, j * LANE:(j + 1) * LANE]
        dvec512_ref[H_ATTN_G:H_ATTN_G + 1, :] += _colsum(dga_rows)
        dvec512_ref[H_SSM_G:H_SSM_G + 1, :] += _colsum(dgs_rows)

    row = pl.BlockSpec((tm, D_MODEL), lambda i: (i, 0))
    return pl.pallas_call(
        body, name="out_proj_bwd", grid=(seq // tm,),
        in_specs=[row, pl.BlockSpec((tm, ATTN_W), lambda i: (i, 0)), pl.BlockSpec((4, tm, LANE), lambda i: (0, i, 0)),
                  pl.BlockSpec((1, 16, D_MODEL), lambda i: (layer, 0, 0)),
                  pl.BlockSpec((1, 8, SSM_W), lambda i: (layer, 0, 0)),
                  pl.BlockSpec((1, D_MODEL, D_MODEL), lambda i: (layer, 0, 0))],
        out_specs=[pl.BlockSpec((tm, ATTN_W), lambda i: (i, 0)), pl.BlockSpec((4, tm, LANE), lambda i: (0, i, 0)),
                   row, row, pl.BlockSpec((16, D_MODEL), lambda i: (0, 0)), pl.BlockSpec((8, SSM_W), lambda i: (0, 0))],
        out_shape=[jax.ShapeDtypeStruct((seq, ATTN_W), F32), jax.ShapeDtypeStruct((4, seq, LANE), F32),
                   jax.ShapeDtypeStruct((seq, D_MODEL), BF16), jax.ShapeDtypeStruct((seq, D_MODEL), BF16),
                   jax.ShapeDtypeStruct((16, D_MODEL), F32), jax.ShapeDtypeStruct((8, SSM_W), F32)],
        compiler_params=_params(("arbitrary",)),
    )(dx1, attn, s4, vec, vec512, w_out)


def _mlp_fwd(x1, vec, w_in_t, w_out, layer):
    seq = x1.shape[0]
    tm = ROW_TILE

    def body(x1_ref, vec_ref, wi_hbm, wo_hbm, x2_ref, wi, wo, sems):
        _load_once(pl.program_id(0), [(wi_hbm.at[layer], wi), (wo_hbm.at[layer], wo)], sems)
        x1v = x1_ref[...]
        n, _, _ = _rms_fwd(x1v, vec_ref[0, V_PRE_MLP:V_PRE_MLP + 1, :])
        h = (n * (1.0 + vec_ref[0, V_SC2:V_SC2 + 1, :]) + vec_ref[0, V_SH2:V_SH2 + 1, :]).astype(BF16)
        a = _nt(h, wi[...])
        r = jnp.square(jnp.maximum(a, 0.0)).astype(BF16)
        nf, _, _ = _rms_fwd(_nn(r, wo[...]), vec_ref[0, V_POST_MLP:V_POST_MLP + 1, :])
        x2_ref[...] = x1v + vec_ref[0, V_G2:V_G2 + 1, :] * nf

    row = pl.BlockSpec((tm, D_MODEL), lambda i: (i, 0))
    return pl.pallas_call(
        body, name="mlp_fwd", grid=(seq // tm,),
        in_specs=[row, pl.BlockSpec((1, 16, D_MODEL), lambda i: (layer, 0, 0)), HBM, HBM],
        out_specs=row, out_shape=jax.ShapeDtypeStruct((seq, D_MODEL), F32),
        scratch_shapes=[pltpu.VMEM((D_FF, D_MODEL), BF16), pltpu.VMEM((D_FF, D_MODEL), BF16),
                        pltpu.SemaphoreType.DMA((2,))],
        compiler_params=_params(("arbitrary",)),
    )(x1, vec, w_in_t, w_out)


def _mlp_bwd(dx2, x1, vec, w_in_t, w_out, layer):
    seq = x1.shape[0]
    tm = ROW_TILE

    def body(dx2_ref, x1_ref, vec_ref, wi_hbm, wo_hbm, dx1_ref, h_ref, da_ref, r_ref, df_ref, dvec_ref, wi, wo, sems):
        i = pl.program_id(0)
        _load_once(i, [(wi_hbm.at[layer], wi), (wo_hbm.at[layer], wo)], sems)

        @pl.when(i == 0)
        def _():
            dvec_ref[...] = jnp.zeros_like(dvec_ref)

        g_pre = vec_ref[0, V_PRE_MLP:V_PRE_MLP + 1, :]
        g_post = vec_ref[0, V_POST_MLP:V_POST_MLP + 1, :]
        sc2 = vec_ref[0, V_SC2:V_SC2 + 1, :]
        n, xh, xr = _rms_fwd(x1_ref[...], g_pre)
        h = (n * (1.0 + sc2) + vec_ref[0, V_SH2:V_SH2 + 1, :]).astype(BF16)
        h_ref[...] = h
        a = _nt(h, wi[...])
        relu = jnp.maximum(a, 0.0)
        r = jnp.square(relu).astype(BF16)
        r_ref[...] = r
        nf, fh, fr = _rms_fwd(_nn(r, wo[...]), g_post)
        dx2v = dx2_ref[...]
        dvec_ref[V_G2:V_G2 + 1, :] += _colsum(dx2v * nf)
        df, dgp_rows = _rms_bwd(dx2v * vec_ref[0, V_G2:V_G2 + 1, :], fh, fr, g_post)
        dvec_ref[V_POST_MLP:V_POST_MLP + 1, :] += _colsum(dgp_rows)
        dfb = df.astype(BF16)
        df_ref[...] = dfb
        da = (_nt(dfb, wo[...]) * (2.0 * relu)).astype(BF16)
        da_ref[...] = da
        dh = _nn(da, wi[...])
        dvec_ref[V_SH2:V_SH2 + 1, :] += _colsum(dh)
        dvec_ref[V_SC2:V_SC2 + 1, :] += _colsum(dh * n)
        dxn, dg_rows = _rms_bwd(dh * (1.0 + sc2), xh, xr, g_pre)
        dvec_ref[V_PRE_MLP:V_PRE_MLP + 1, :] += _colsum(dg_rows)
        dx1_ref[...] = dx2v + dxn

    row = pl.BlockSpec((tm, D_MODEL), lambda i: (i, 0))
    wide = pl.BlockSpec((tm, D_FF), lambda i: (i, 0))
    return pl.pallas_call(
        body, name="mlp_bwd", grid=(seq // tm,),
        in_specs=[row, row, pl.BlockSpec((1, 16, D_MODEL), lambda i: (layer, 0, 0)), HBM, HBM],
        out_specs=[row, row, wide, wide, row, pl.BlockSpec((16, D_MODEL), lambda i: (0, 0))],
        out_shape=[jax.ShapeDtypeStruct((seq, D_MODEL), F32), jax.ShapeDtypeStruct((seq, D_MODEL), BF16),
                   jax.ShapeDtypeStruct((seq, D_FF), BF16), jax.ShapeDtypeStruct((seq, D_FF), BF16),
                   jax.ShapeDtypeStruct((seq, D_MODEL), BF16), jax.ShapeDtypeStruct((16, D_MODEL), F32)],
        scratch_shapes=[pltpu.VMEM((D_FF, D_MODEL), BF16), pltpu.VMEM((D_FF, D_MODEL), BF16),
                        pltpu.SemaphoreType.DMA((2,))],
        compiler_params=_params(("arbitrary",)),
    )(dx2, x1, vec, w_in_t, w_out)


def _loss_head(y, target):
    seq = y.shape[0]
    tm = ROW_TILE

    def body(y_ref, t_ref, dy_ref, part_ref):
        e = y_ref[...] - t_ref[...]
        dy_ref[...] = e * (1.0 / D_MODEL)
        tot = jnp.sum(jnp.sum(e * e, axis=1, keepdims=True), axis=0, keepdims=True) * (0.5 / D_MODEL)
        part_ref[0] = jnp.broadcast_to(tot, (SUBLANE, LANE))

    row = pl.BlockSpec((tm, D_MODEL), lambda i: (i, 0))
    return pl.pallas_call(
        body, name="loss_head", grid=(seq // tm,),
        in_specs=[row, row],
        out_specs=[row, pl.BlockSpec((1, SUBLANE, LANE), lambda i: (i, 0, 0))],
        out_shape=[jax.ShapeDtypeStruct((seq, D_MODEL), F32), jax.ShapeDtypeStruct((seq // tm, SUBLANE, LANE), F32)],
        compiler_params=_params(("parallel",)),
    )(y, target)


def _matmul_tn(a, b, out_dtype, name, pieces=1):
    kk, m = a.shape
    n = b.shape[1]
    tm = min(m, 512)
    tn = n // pieces if pieces > 1 else min(n, 1280)
    tk = 512
    nk = kk // tk

    def body(a_ref, b_ref, o_ref, acc):
        k = pl.program_id(2)

        @pl.when(k == 0)
        def _():
            acc[...] = jnp.zeros_like(acc)

        acc[...] += _tn(a_ref[...], b_ref[...])

        @pl.when(k == nk - 1)
        def _():
            if pieces > 1:
                o_ref[0] = acc[...].astype(out_dtype)
            else:
                o_ref[...] = acc[...].astype(out_dtype)

    if pieces > 1:
        out_spec = pl.BlockSpec((1, tm, tn), lambda i, j, k: (j, i, 0))
        out_shape = jax.ShapeDtypeStruct((pieces, m, tn), out_dtype)
    else:
        out_spec = pl.BlockSpec((tm, tn), lambda i, j, k: (i, j))
        out_shape = jax.ShapeDtypeStruct((m, n), out_dtype)
    return pl.pallas_call(
        body, name=name, grid=(m // tm, n // tn, nk),
        in_specs=[pl.BlockSpec((tk, tm), lambda i, j, k: (k, i)), pl.BlockSpec((tk, tn), lambda i, j, k: (k, j))],
        out_specs=out_spec, out_shape=out_shape,
        scratch_shapes=[pltpu.VMEM((tm, tn), F32)],
        compiler_params=_params(("parallel", "parallel", "arbitrary")),
    )(a, b)


def _attn_probs(i, q_h, kband, slope, sink):
    rr = lax.broadcasted_iota(jnp.int32, (BLOCK, 2 * BLOCK), 0)
    jj = lax.broadcasted_iota(jnp.int32, (BLOCK, 2 * BLOCK), 1)
    diff = BLOCK + rr - jj
    valid = (diff >= 0) & (diff < WINDOW) & ((jj >= BLOCK) | (i > 0))
    s = _nt(q_h, kband) * (HEAD_DIM ** -0.5)
    s = jnp.where(valid, s - slope * diff.astype(F32), NEG_INF)
    m = jnp.maximum(jnp.max(s, axis=1, keepdims=True), sink)
    p = jnp.exp(s - m)
    ps = jnp.exp(sink - m)
    inv = 1.0 / (jnp.sum(p, axis=1, keepdims=True) + ps)
    return p * inv, ps * inv


def _bands(kvp, kvc, h):
    kband = jnp.concatenate([kvp[:, h * HEAD_DIM:(h + 1) * HEAD_DIM], kvc[:, h * HEAD_DIM:(h + 1) * HEAD_DIM]], axis=0)
    v0 = KV_W + h * HEAD_DIM
    vband = jnp.concatenate([kvp[:, v0:v0 + HEAD_DIM], kvc[:, v0:v0 + HEAD_DIM]], axis=0)
    return kband, vband


def _attn_fwd(q, kv, sinks):
    seq = q.shape[0]
    nb = seq // BLOCK
    slopes = _alibi_slopes()

    def body(sink_ref, q_ref, kvp_ref, kvc_ref, o_ref):
        i = pl.program_id(0)
        qv, kvp, kvc = q_ref[...], kvp_ref[...], kvc_ref[...]
        for h in range(N_KV):
            kband, vband = _bands(kvp, kvc, h)
            for g in range(Q_PER_KV):
                hq = h * Q_PER_KV + g
                cols = slice(hq * HEAD_DIM, (hq + 1) * HEAD_DIM)
                pr, _ = _attn_probs(i, qv[:, cols], kband, slopes[hq], sink_ref[hq])
                o_ref[:, cols] = _nn(pr.astype(BF16), vband)

    return pl.pallas_call(
        body, name="attn_fwd", grid=(nb,),
        in_specs=[pl.BlockSpec(memory_space=pltpu.SMEM),
                  pl.BlockSpec((BLOCK, ATTN_W), lambda i: (i, 0)),
                  pl.BlockSpec((BLOCK, 2 * KV_W), lambda i: (jnp.maximum(i - 1, 0), 0)),
                  pl.BlockSpec((BLOCK, 2 * KV_W), lambda i: (i, 0))],
        out_specs=pl.BlockSpec((BLOCK, ATTN_W), lambda i: (i, 0)),
        out_shape=jax.ShapeDtypeStruct((seq, ATTN_W), F32),
        compiler_params=_params(("parallel",)),
    )(sinks, q, kv, kv)


def _attn_bwd(q, kv, sinks, dout):
    seq = q.shape[0]
    nb = seq // BLOCK
    slopes = _alibi_slopes()
    scale = HEAD_DIM ** -0.5

    def body(sink_ref, q_ref, kvp_ref, kvc_ref, do_ref, dq_ref, dkv_ref, dsk_ref, prev):
        step = pl.program_id(0)
        i = nb - 1 - step

        @pl.when(step == 0)
        def _():
            prev[...] = jnp.zeros_like(prev)

        qv, kvp, kvc = q_ref[...], kvp_ref[...], kvc_ref[...]
        dov = do_ref[...].astype(BF16)
        dk, dv, dsk = [], [], []
        for h in range(N_KV):
            kband, vband = _bands(kvp, kvc, h)
            dk_h = jnp.zeros((2 * BLOCK, HEAD_DIM), F32)
            dv_h = jnp.zeros((2 * BLOCK, HEAD_DIM), F32)
            for g in range(Q_PER_KV):
                hq = h * Q_PER_KV + g
                cols = slice(hq * HEAD_DIM, (hq + 1) * HEAD_DIM)
                q_h, do_h = qv[:, cols], dov[:, cols]
                pr, ps = _attn_probs(i, q_h, kband, slopes[hq], sink_ref[hq])
                dp = _nt(do_h, vband)
                delta = jnp.sum(pr * dp, axis=1, keepdims=True)
                ds = (pr * (dp - delta) * scale).astype(BF16)
                dsk.append(jnp.broadcast_to(-_colsum(ps * delta), (1, LANE)))
                dq_ref[:, cols] = _nn(ds, kband)
                dk_h = dk_h + _tn(ds, q_h)
                dv_h = dv_h + _tn(pr.astype(BF16), do_h)
            dk.append(dk_h)
            dv.append(dv_h)
        band = jnp.concatenate(dk + dv, axis=1)
        dkv_ref[...] = band[BLOCK:, :] + prev[...]
        prev[...] = band[:BLOCK, :]
        dsk_ref[0] = jnp.concatenate(dsk, axis=0)

    return pl.pallas_call(
        body, name="attn_bwd", grid=(nb,),
        in_specs=[pl.BlockSpec(memory_space=pltpu.SMEM),
                  pl.BlockSpec((BLOCK, ATTN_W), lambda s: (nb - 1 - s, 0)),
                  pl.BlockSpec((BLOCK, 2 * KV_W), lambda s: (jnp.maximum(nb - 2 - s, 0), 0)),
                  pl.BlockSpec((BLOCK, 2 * KV_W), lambda s: (nb - 1 - s, 0)),
                  pl.BlockSpec((BLOCK, ATTN_W), lambda s: (nb - 1 - s, 0))],
        out_specs=[pl.BlockSpec((BLOCK, ATTN_W), lambda s: (nb - 1 - s, 0)),
                   pl.BlockSpec((BLOCK, 2 * KV_W), lambda s: (nb - 1 - s, 0)),
                   pl.BlockSpec((1, N_Q, LANE), lambda s: (nb - 1 - s, 0, 0))],
        out_shape=[jax.ShapeDtypeStruct((seq, ATTN_W), F32), jax.ShapeDtypeStruct((seq, 2 * KV_W), F32),
                   jax.ShapeDtypeStruct((nb, N_Q, LANE), F32)],
        scratch_shapes=[pltpu.VMEM((BLOCK, 2 * KV_W), F32)],
        compiler_params=_params(("arbitrary",)),
    )(sinks, q, kv, kv, dout)


def _discretize(lr, li, ldt, br, bi):
    dt = jnp.exp(ldt)
    mag = jnp.exp(lr * dt)
    ang = li * dt
    ab_r = mag * jnp.cos(ang)
    ab_i = mag * jnp.sin(ang)
    nr = ab_r - 1.0
    ni = ab_i
    den = lr * lr + li * li
    f_r = (nr * lr + ni * li) / den
    f_i = (ni * lr - nr * li) / den
    return ab_r, ab_i, f_r * br - f_i * bi, f_r * bi + f_i * br


def _ssm_prepare(lr, li, ldt, br, bi):
    n = lr.shape[0]
    tn = N_CH
    col = pl.BlockSpec((tn, 1), lambda i: (i, 0))
    mat = pl.BlockSpec((tn, GROUP_W), lambda i: (i, 0))

    def body(lr_ref, li_ref, ldt_ref, br_ref, bi_ref, ar_ref, ai_ref, bbr_ref, bbi_ref):
        ar, ai, bbr, bbi = _discretize(lr_ref[...], li_ref[...], ldt_ref[...], br_ref[...], bi_ref[...])
        ar_ref[...] = ar
        ai_ref[...] = ai
        bbr_ref[...] = bbr
        bbi_ref[...] = bbi

    cs = jax.ShapeDtypeStruct((n, 1), F32)
    ms = jax.ShapeDtypeStruct((n, GROUP_W), F32)
    return pl.pallas_call(
        body, name="ssm_prepare", grid=(n // tn,),
        in_specs=[col, col, col, mat, mat], out_specs=[col, col, mat, mat], out_shape=[cs, cs, ms, ms],
        compiler_params=_params(("parallel",)),
    )(lr, li, ldt, br, bi)


def _ssm_prepare_bwd(lr, li, ldt, br, bi, dar, dai, dbbr, dbbi):
    n = lr.shape[0]
    tn = N_CH
    col = pl.BlockSpec((tn, 1), lambda i: (i, 0))
    mat = pl.BlockSpec((tn, GROUP_W), lambda i: (i, 0))

    def body(lr_ref, li_ref, ldt_ref, br_ref, bi_ref, dar_ref, dai_ref, dbbr_ref, dbbi_ref,
             dlr_ref, dli_ref, dldt_ref, dbr_ref, dbi_ref):
        _, vjp = jax.vjp(_discretize, lr_ref[...], li_ref[...], ldt_ref[...], br_ref[...], bi_ref[...])
        dlr, dli, dldt, dbr, dbi = vjp((dar_ref[...], dai_ref[...], dbbr_ref[...], dbbi_ref[...]))
        dlr_ref[...] = dlr
        dli_ref[...] = dli
        dldt_ref[...] = dldt
        dbr_ref[...] = dbr
        dbi_ref[...] = dbi

    cs = jax.ShapeDtypeStruct((n, 1), F32)
    ms = jax.ShapeDtypeStruct((n, GROUP_W), F32)
    return pl.pallas_call(
        body, name="ssm_prepare_bwd", grid=(n // tn,),
        in_specs=[col, col, col, mat, mat, col, col, mat, mat],
        out_specs=[col, col, col, mat, mat], out_shape=[cs, cs, cs, ms, ms],
        compiler_params=_params(("parallel",)),
    )(lr, li, ldt, br, bi, dar, dai, dbbr, dbbi)


def _load_slabs(src4_ref, dst):
    for s in range(STEPS):
        dst[s * SEGS:(s + 1) * SEGS, :] = jnp.concatenate(
            [src4_ref[j, pl.ds(s, SEGS, stride=STEPS), :] for j in range(4)], axis=1)


def _store_slabs(src, dst4_ref):
    for s in range(STEPS):
        for j in range(4):
            dst4_ref[j, pl.ds(s, SEGS, stride=STEPS), :] = src[s * SEGS:(s + 1) * SEGS, j * LANE:(j + 1) * LANE]


def _power_table(ar_ref, ai_ref, pwr, pwi):
    ar, ai = ar_ref[0], ai_ref[0]
    pr, pi = ar, ai
    pwr[0:1, :] = pr
    pwi[0:1, :] = pi
    for k in range(1, STEPS):
        pr, pi = pr * ar - pi * ai, pr * ai + pi * ar
        pwr[k:k + 1, :] = pr
        pwi[k:k + 1, :] = pi


def _scan_states(ubf, ar_ref, ai_ref, bbr_ref, bbi_ref, pwr, pwi, cin_r, cin_i, hr, hi):
    for k in range(2):
        rows = slice(k * 256, (k + 1) * 256)
        cols = slice(k * HALF_CH, (k + 1) * HALF_CH)
        hr[:, cols] = _nn(ubf[:, rows], bbr_ref[0, k])
        hi[:, cols] = _nn(ubf[:, rows], bbi_ref[0, k])
    for st in range(N_CH // STRIP):
        cs = slice(st * STRIP, (st + 1) * STRIP)
        arb = jnp.broadcast_to(ar_ref[0, :, cs], (SEGS, STRIP))
        aib = jnp.broadcast_to(ai_ref[0, :, cs], (SEGS, STRIP))

        def step(s, carry, cs=cs, arb=arb, aib=aib):
            cr, ci = carry
            rows = pl.ds(pl.multiple_of(s * SEGS, SEGS), SEGS)
            nr = arb * cr - aib * ci + hr[rows, cs]
            ni = arb * ci + aib * cr + hi[rows, cs]
            hr[rows, cs] = nr
            hi[rows, cs] = ni
            return nr, ni

        zero = jnp.zeros((SEGS, STRIP), F32)
        lax.fori_loop(0, STEPS, step, (zero, zero))
    last = slice((STEPS - 1) * SEGS, STEPS * SEGS)
    end_r, end_i = hr[last, :], hi[last, :]
    a64r, a64i = pwr[STEPS - 1:STEPS, :], pwi[STEPS - 1:STEPS, :]
    cr, ci = cin_r, cin_i
    rows_r, rows_i = [], []
    for j in range(SEGS):
        rows_r.append(cr)
        rows_i.append(ci)
        cr, ci = (a64r * cr - a64i * ci + end_r[j:j + 1, :], a64r * ci + a64i * cr + end_i[j:j + 1, :])
    cm_r, cm_i = jnp.concatenate(rows_r, axis=0), jnp.concatenate(rows_i, axis=0)
    for st in range(N_CH // STRIP):
        cs = slice(st * STRIP, (st + 1) * STRIP)
        cmr, cmi = cm_r[:, cs], cm_i[:, cs]

        def fix(s, carry, cs=cs, cmr=cmr, cmi=cmi):
            rows = pl.ds(pl.multiple_of(s * SEGS, SEGS), SEGS)
            pr, pi = pwr[pl.ds(s, 1), cs], pwi[pl.ds(s, 1), cs]
            hr[rows, cs] = hr[rows, cs] + (pr * cmr - pi * cmi)
            hi[rows, cs] = hi[rows, cs] + (pr * cmi + pi * cmr)
            return carry

        lax.fori_loop(0, STEPS, fix, 0)
    return (cm_r, cm_i), (cr, ci)


def _ssm_outputs(u, hr, hi, crt_ref, cit_ref, vec512_ref, wg_ref):
    ys = []
    for k in range(2):
        cols = slice(k * HALF_CH, (k + 1) * HALF_CH)
        ys.append(_nn(hr[:, cols].astype(BF16), crt_ref[0, k]) - _nn(hi[:, cols].astype(BF16), cit_ref[0, k]))
    y = jnp.concatenate(ys, axis=1) + vec512_ref[0, H_DSKIP:H_DSKIP + 1, :] * u
    z, t = _gelu(y)
    gate = jax.nn.sigmoid(_nn(z.astype(BF16), wg_ref[0]) + vec512_ref[0, H_BGLU:H_BGLU + 1, :])
    return y, z, t, gate


def _ssm_specs(layer, nck, rev):
    def chunk(i):
        return nck - 1 - i if rev else i

    return [pl.BlockSpec((4, CHUNK, LANE), lambda i: (0, chunk(i), 0)),
            pl.BlockSpec((1, 1, N_CH), lambda i: (layer, 0, 0)),
            pl.BlockSpec((1, 1, N_CH), lambda i: (layer, 0, 0)),
            pl.BlockSpec((1, 2, 256, HALF_CH), lambda i: (layer, 0, 0, 0)),
            pl.BlockSpec((1, 2, 256, HALF_CH), lambda i: (layer, 0, 0, 0)),
            pl.BlockSpec((1, 2, HALF_CH, 256), lambda i: (layer, 0, 0, 0)),
            pl.BlockSpec((1, 2, HALF_CH, 256), lambda i: (layer, 0, 0, 0)),
            pl.BlockSpec((1, 8, SSM_W), lambda i: (layer, 0, 0)),
            pl.BlockSpec((1, SSM_W, SSM_W), lambda i: (layer, 0, 0))]


def _ssm_fwd(u4, a_r, a_i, bb_r, bb_i, c_rt, c_it, vec512, w_glu, layer):
    seq = u4.shape[1]
    nck = seq // CHUNK

    def body(u4_ref, ar_ref, ai_ref, bbr_ref, bbi_ref, crt_ref, cit_ref, vec512_ref, wg_ref,
             s4_ref, hcr_ref, hci_ref, hr, hi, pwr, pwi, car, cai, ubuf, obuf):
        i = pl.program_id(0)

        @pl.when(i == 0)
        def _():
            car[...] = jnp.zeros_like(car)
            cai[...] = jnp.zeros_like(cai)
            _power_table(ar_ref, ai_ref, pwr, pwi)

        _load_slabs(u4_ref, ubuf)
        u = ubuf[...]
        cin_r, cin_i = car[...], cai[...]
        hcr_ref[0] = jnp.broadcast_to(cin_r, (SEGS, N_CH))
        hci_ref[0] = jnp.broadcast_to(cin_i, (SEGS, N_CH))
        _, (er, ei) = _scan_states(u.astype(BF16), ar_ref, ai_ref, bbr_ref, bbi_ref, pwr, pwi, cin_r, cin_i, hr, hi)
        car[...] = er
        cai[...] = ei
        _, z, _, gate = _ssm_outputs(u, hr, hi, crt_ref, cit_ref, vec512_ref, wg_ref)
        obuf[...] = z * gate
        _store_slabs(obuf, s4_ref)

    return pl.pallas_call(
        body, name="ssm_fwd", grid=(nck,),
        in_specs=_ssm_specs(layer, nck, False),
        out_specs=[pl.BlockSpec((4, CHUNK, LANE), lambda i: (0, i, 0)),
                   pl.BlockSpec((1, SEGS, N_CH), lambda i: (i, 0, 0)),
                   pl.BlockSpec((1, SEGS, N_CH), lambda i: (i, 0, 0))],
        out_shape=[jax.ShapeDtypeStruct((4, seq, LANE), F32), jax.ShapeDtypeStruct((nck, SEGS, N_CH), F32),
                   jax.ShapeDtypeStruct((nck, SEGS, N_CH), F32)],
        scratch_shapes=[pltpu.VMEM((CHUNK, N_CH), F32), pltpu.VMEM((CHUNK, N_CH), F32),
                        pltpu.VMEM((STEPS, N_CH), F32), pltpu.VMEM((STEPS, N_CH), F32),
                        pltpu.VMEM((1, N_CH), F32), pltpu.VMEM((1, N_CH), F32),
                        pltpu.VMEM((CHUNK, SSM_W), F32), pltpu.VMEM((CHUNK, SSM_W), F32)],
        compiler_params=_params(("arbitrary",)),
    )(u4, a_r, a_i, bb_r, bb_i, c_rt, c_it, vec512, w_glu)


def _ssm_bwd(u4, ds4, hc_r, hc_i, a_r, a_i, bb_r, bb_i, c_rt, c_it, vec512, w_glu, layer):
    seq = u4.shape[1]
    nck = seq // CHUNK

    def body(u4_ref, ar_ref, ai_ref, bbr_ref, bbi_ref, crt_ref, cit_ref, vec512_ref, wg_ref, ds4_ref, hcr_ref, hci_ref,
             du4_ref, dbbr_ref, dbbi_ref, dcrt_ref, dcit_ref, dar_ref, dai_ref, dwg_ref, dvec_ref,
             hr, hi, gr, gi, pwr, pwi, gcr, gci, accr, acci, ubuf, dbuf):
        i = pl.program_id(0)

        @pl.when(i == 0)
        def _():
            for ref in (gcr, gci, accr, acci, dbbr_ref, dbbi_ref, dcrt_ref, dcit_ref, dwg_ref, dvec_ref):
                ref[...] = jnp.zeros_like(ref)
            _power_table(ar_ref, ai_ref, pwr, pwi)

        _load_slabs(u4_ref, ubuf)
        u = ubuf[...]
        ubf = u.astype(BF16)
        cin_r, cin_i = hcr_ref[0, 0:1, :], hci_ref[0, 0:1, :]
        (cm_r, cm_i), _ = _scan_states(ubf, ar_ref, ai_ref, bbr_ref, bbi_ref, pwr, pwi, cin_r, cin_i, hr, hi)
        y, z, t, gate = _ssm_outputs(u, hr, hi, crt_ref, cit_ref, vec512_ref, wg_ref)
        _load_slabs(ds4_ref, dbuf)
        ds = dbuf[...]
        da = ds * z * gate * (1.0 - gate)
        dab = da.astype(BF16)
        dz = ds * gate + _nt(dab, wg_ref[0])
        dwg_ref[...] += _tn(z.astype(BF16), dab)
        dvec_ref[H_BGLU:H_BGLU + 1, :] += _colsum(da)
        dy = dz * _gelu_grad(y, t)
        dvec_ref[H_DSKIP:H_DSKIP + 1, :] += _colsum(dy * u)
        du_skip = dy * vec512_ref[0, H_DSKIP:H_DSKIP + 1, :]
        dyb = dy.astype(BF16)
        for k in range(2):
            rows = slice(k * 256, (k + 1) * 256)
            cols = slice(k * HALF_CH, (k + 1) * HALF_CH)
            dcrt_ref[k] += _tn(hr[:, cols].astype(BF16), dyb[:, rows])
            dcit_ref[k] -= _tn(hi[:, cols].astype(BF16), dyb[:, rows])
            gr[:, cols] = _nt(dyb[:, rows], crt_ref[0, k])
            gi[:, cols] = -_nt(dyb[:, rows], cit_ref[0, k])
        for st in range(N_CH // STRIP):
            cs = slice(st * STRIP, (st + 1) * STRIP)
            arb = jnp.broadcast_to(ar_ref[0, :, cs], (SEGS, STRIP))
            aib = jnp.broadcast_to(ai_ref[0, :, cs], (SEGS, STRIP))

            def step(k, carry, cs=cs, arb=arb, aib=aib):
                cr, ci = carry
                rows = pl.ds(pl.multiple_of((STEPS - 1 - k) * SEGS, SEGS), SEGS)
                nr = gr[rows, cs] + (arb * cr + aib * ci)
                ni = gi[rows, cs] + (arb * ci - aib * cr)
                gr[rows, cs] = nr
                gi[rows, cs] = ni
                return nr, ni

            zero = jnp.zeros((SEGS, STRIP), F32)
            lax.fori_loop(0, STEPS, step, (zero, zero))
        first_r, first_i = gr[0:SEGS, :], gi[0:SEGS, :]
        a64r, a64i = pwr[STEPS - 1:STEPS, :], pwi[STEPS - 1:STEPS, :]
        dr_, di_ = gcr[...], gci[...]
        rows_r, rows_i = [None] * SEGS, [None] * SEGS
        for j in range(SEGS - 1, -1, -1):
            rows_r[j], rows_i[j] = dr_, di_
            dr_, di_ = (first_r[j:j + 1, :] + (a64r * dr_ + a64i * di_), first_i[j:j + 1, :] + (a64r * di_ - a64i * dr_))
        gcr[...] = dr_
        gci[...] = di_
        dm_r, dm_i = jnp.concatenate(rows_r, axis=0), jnp.concatenate(rows_i, axis=0)
        for st in range(N_CH // STRIP):
            cs = slice(st * STRIP, (st + 1) * STRIP)
            dmr, dmi = dm_r[:, cs], dm_i[:, cs]

            def fixed(s, cs=cs, dmr=dmr, dmi=dmi):
                rows = pl.ds(pl.multiple_of(s * SEGS, SEGS), SEGS)
                pr, pi = pwr[pl.ds(STEPS - 1 - s, 1), cs], pwi[pl.ds(STEPS - 1 - s, 1), cs]
                g_r = gr[rows, cs] + (pr * dmr + pi * dmi)
                g_i = gi[rows, cs] + (pr * dmi - pi * dmr)
                gr[rows, cs] = g_r
                gi[rows, cs] = g_i
                return g_r, g_i

            g_r, g_i = fixed(jnp.int32(0))
            acc0 = (g_r * cm_r[:, cs] + g_i * cm_i[:, cs], g_i * cm_r[:, cs] - g_r * cm_i[:, cs])

            def step(s, carry, cs=cs, fixed=fixed):
                sr, si = carry
                g_r, g_i = fixed(s)
                prev = pl.ds(pl.multiple_of((s - 1) * SEGS, SEGS), SEGS)
                hpr, hpi = hr[prev, cs], hi[prev, cs]
                return sr + (g_r * hpr + g_i * hpi), si + (g_i * hpr - g_r * hpi)

            sr, si = lax.fori_loop(1, STEPS, step, acc0)
            accr[:, cs] += sr
            acci[:, cs] += si
        grb, gib = gr[...].astype(BF16), gi[...].astype(BF16)
        dus = []
        for k in range(2):
            rows = slice(k * 256, (k + 1) * 256)
            cols = slice(k * HALF_CH, (k + 1) * HALF_CH)
            dus.append(_nt(grb[:, cols], bbr_ref[0, k]) + _nt(gib[:, cols], bbi_ref[0, k]))
            dbbr_ref[k] += _tn(ubf[:, rows], grb[:, cols])
            dbbi_ref[k] += _tn(ubf[:, rows], gib[:, cols])
        dbuf[...] = jnp.concatenate(dus, axis=1) + du_skip
        _store_slabs(dbuf, du4_ref)

        @pl.when(i == nck - 1)
        def _():
            dar_ref[...] = _colsum(accr[...])
            dai_ref[...] = _colsum(acci[...])

    rev4 = pl.BlockSpec((4, CHUNK, LANE), lambda i: (0, nck - 1 - i, 0))
    hc_spec = pl.BlockSpec((1, SEGS, N_CH), lambda i: (nck - 1 - i, 0, 0))
    fixed2 = lambda shape: pl.BlockSpec(shape, lambda i: (0,) * len(shape))
    return pl.pallas_call(
        body, name="ssm_bwd", grid=(nck,),
        in_specs=_ssm_specs(layer, nck, True) + [rev4, hc_spec, hc_spec],
        out_specs=[rev4, fixed2((2, 256, HALF_CH)), fixed2((2, 256, HALF_CH)), fixed2((2, HALF_CH, 256)),
                   fixed2((2, HALF_CH, 256)), fixed2((1, N_CH)), fixed2((1, N_CH)), fixed2((SSM_W, SSM_W)),
                   fixed2((8, SSM_W))],
        out_shape=[jax.ShapeDtypeStruct((4, seq, LANE), F32),
                   jax.ShapeDtypeStruct((2, 256, HALF_CH), F32), jax.ShapeDtypeStruct((2, 256, HALF_CH), F32),
                   jax.ShapeDtypeStruct((2, HALF_CH, 256), F32), jax.ShapeDtypeStruct((2, HALF_CH, 256), F32),
                   jax.ShapeDtypeStruct((1, N_CH), F32), jax.ShapeDtypeStruct((1, N_CH), F32),
                   jax.ShapeDtypeStruct((SSM_W, SSM_W), F32), jax.ShapeDtypeStruct((8, SSM_W), F32)],
        scratch_shapes=[pltpu.VMEM((CHUNK, N_CH), F32), pltpu.VMEM((CHUNK, N_CH), F32),
                        pltpu.VMEM((CHUNK, N_CH), F32), pltpu.VMEM((CHUNK, N_CH), F32),
                        pltpu.VMEM((STEPS, N_CH), F32), pltpu.VMEM((STEPS, N_CH), F32),
                        pltpu.VMEM((1, N_CH), F32), pltpu.VMEM((1, N_CH), F32),
                        pltpu.VMEM((SEGS, N_CH), F32), pltpu.VMEM((SEGS, N_CH), F32),
                        pltpu.VMEM((CHUNK, SSM_W), F32), pltpu.VMEM((CHUNK, SSM_W), F32)],
        compiler_params=_params(("arbitrary",)),
    )(u4, a_r, a_i, bb_r, bb_i, c_rt, c_it, vec512, w_glu, ds4, hc_r, hc_i)


def _block_diag(t):
    nl, _, ng, a, b = t.shape
    eye = jnp.eye(ng, dtype=t.dtype)
    return jnp.einsum("gh,lkgab->lkgahb", eye, t).reshape(nl, 2, ng * a, ng * b)


def _diag_blocks(t, a, b):
    nl = t.shape[0]
    ng = N_GROUPS // 2
    return jnp.einsum("lkgagb->lkgab", t.reshape(nl, 2, ng, a, ng, b))


def _local_step(x, loss_target, mod, p, w):
    nl = mod.shape[0]
    pad1024 = jnp.zeros((nl, 16 - 10, D_MODEL), F32)
    vec = jnp.concatenate([mod.reshape(nl, N_MOD, D_MODEL), p["pre_mix_g"][:, None], p["post_mix_g"][:, None],
                           p["pre_mlp_g"][:, None], p["post_mlp_g"][:, None], pad1024], axis=1)
    vec512 = jnp.concatenate([p["attn_out_g"][:, None], p["ssm_out_g"][:, None], p["d_skip"][:, None],
                              p["b_glu"][:, None], jnp.zeros((nl, 4, SSM_W), F32)], axis=1)
    n_all = nl * N_CH
    lr = p["lam_re"].reshape(n_all, 1)
    li = p["lam_im"].reshape(n_all, 1)
    ldt = jnp.broadcast_to(p["log_dt"][:, :, None], (nl, N_GROUPS, STATE)).reshape(n_all, 1)
    br = p["b_re"].reshape(n_all, GROUP_W)
    bi = p["b_im"].reshape(n_all, GROUP_W)
    ab_r, ab_i, bb_r, bb_i = _ssm_prepare(lr, li, ldt, br, bi)
    a_r = ab_r.reshape(nl, 1, N_CH)
    a_i = ab_i.reshape(nl, 1, N_CH)

    def dense_b(bb):
        return _block_diag(bb.reshape(nl, 2, 16, STATE, GROUP_W).transpose(0, 1, 2, 4, 3)).astype(BF16)

    def dense_c(cc):
        return _block_diag(cc.reshape(nl, 2, 16, GROUP_W, STATE).transpose(0, 1, 2, 4, 3)).astype(BF16)

    bbr_d, bbi_d = dense_b(bb_r), dense_b(bb_i)
    crt_d, cit_d = dense_c(p["c_re"]), dense_c(p["c_im"])

    saved = []
    xl = x
    for l in range(nl):
        q, kv, u4, h1 = _in_proj_fwd(xl, vec, w["w_in_t"], l)
        attn = _attn_fwd(q, kv, p["attn_sinks"][l])
        s4, hc_r, hc_i = _ssm_fwd(u4, a_r, a_i, bbr_d, bbi_d, crt_d, cit_d, vec512, w["w_glu"], l)
        x1 = _out_proj_fwd(xl, attn, s4, vec, vec512, w["w_out"], l)
        x2 = _mlp_fwd(x1, vec, w["w_mlp_in_t"], w["w_mlp_out"], l)
        saved.append((xl, q, kv, u4, h1, attn, s4, hc_r, hc_i, x1))
        xl = x2

    dx, loss_parts = _loss_head(xl, loss_target)
    loss = jnp.sum(loss_parts[:, 0, 0])

    big = {k: [None] * nl for k in ("w_in", "w_glu", "w_out", "w_mlp_in", "w_mlp_out")}
    dvec_l, dvec512_l, dsink_l = [None] * nl, [None] * nl, [None] * nl
    dab_r, dab_i, dbb_r, dbb_i, dc_re, dc_im = ([None] * nl for _ in range(6))
    for l in range(nl - 1, -1, -1):
        xl, q, kv, u4, h1, attn, s4, hc_r, hc_i, x1 = saved[l]
        dx1, h2, da, r, df, dvec_m = _mlp_bwd(dx, x1, vec, w["w_mlp_in_t"], w["w_mlp_out"], l)
        big["w_mlp_out"][l] = _matmul_tn(r, df, BF16, "dw_mlp_out").reshape(4, D_FF // 4, D_MODEL)
        big["w_mlp_in"][l] = _matmul_tn(h2, da, BF16, "dw_mlp_in", pieces=4)
        dattn, ds4, heads, dmixed, dvec_o, dvec512_o = _out_proj_bwd(dx1, attn, s4, vec, vec512, w["w_out"], l)
        big["w_out"][l] = _matmul_tn(heads, dmixed, BF16, "dw_out").reshape(4, D_MODEL // 4, D_MODEL)
        dq, dkv, dsk = _attn_bwd(q, kv, p["attn_sinks"][l], dattn)
        (du4, dbbr, dbbi, dcrt, dcit, dar, dai, dwg, dvec512_s) = _ssm_bwd(
            u4, ds4, hc_r, hc_i, a_r, a_i, bbr_d, bbi_d, crt_d, cit_d, vec512, w["w_glu"], l)
        big["w_glu"][l] = dwg.astype(BF16).reshape(4, SSM_W // 4, SSM_W)
        dx, dproj, dvec_i = _in_proj_bwd(dx1, dq, dkv, du4, xl, vec, w["w_in_t"], l)
        dw_in = _matmul_tn(h1, dproj, BF16, "dw_in")
        big["w_in"][l] = dw_in.reshape(D_MODEL, 4, IN_W // 4).transpose(1, 0, 2)
        dvec_l[l] = dvec_m + dvec_o + dvec_i
        dvec512_l[l] = dvec512_o + dvec512_s
        dsink_l[l] = jnp.sum(dsk[:, :, 0], axis=0)
        dab_r[l], dab_i[l], dbb_r[l], dbb_i[l], dc_re[l], dc_im[l] = dar, dai, dbbr, dbbi, dcrt, dcit

    dvec = jnp.stack(dvec_l)
    dvec512 = jnp.stack(dvec512_l)
    dbb_r_c = _diag_blocks(jnp.stack(dbb_r), GROUP_W, STATE).transpose(0, 1, 2, 4, 3).reshape(n_all, GROUP_W)
    dbb_i_c = _diag_blocks(jnp.stack(dbb_i), GROUP_W, STATE).transpose(0, 1, 2, 4, 3).reshape(n_all, GROUP_W)
    dlr, dli, dldt, dbr, dbi = _ssm_prepare_bwd(
        lr, li, ldt, br, bi, jnp.stack(dab_r).reshape(n_all, 1), jnp.stack(dab_i).reshape(n_all, 1), dbb_r_c, dbb_i_c)
    c_shape = (nl, N_GROUPS, GROUP_W, STATE)
    small = {
        "b_ada": dvec[:, :N_MOD].reshape(nl, N_MOD * D_MODEL),
        "pre_mix_g": dvec[:, V_PRE_MIX], "post_mix_g": dvec[:, V_POST_MIX],
        "pre_mlp_g": dvec[:, V_PRE_MLP], "post_mlp_g": dvec[:, V_POST_MLP],
        "attn_sinks": jnp.stack(dsink_l),
        "lam_re": dlr.reshape(nl, N_GROUPS, STATE), "lam_im": dli.reshape(nl, N_GROUPS, STATE),
        "log_dt": jnp.sum(dldt.reshape(nl, N_GROUPS, STATE), axis=-1),
        "b_re": dbr.reshape(nl, N_GROUPS, STATE, GROUP_W), "b_im": dbi.reshape(nl, N_GROUPS, STATE, GROUP_W),
        "c_re": _diag_blocks(jnp.stack(dc_re), STATE, GROUP_W).transpose(0, 1, 2, 4, 3).reshape(c_shape),
        "c_im": _diag_blocks(jnp.stack(dc_im), STATE, GROUP_W).transpose(0, 1, 2, 4, 3).reshape(c_shape),
        "d_skip": dvec512[:, H_DSKIP], "b_glu": dvec512[:, H_BGLU],
        "attn_out_g": dvec512[:, H_ATTN_G], "ssm_out_g": dvec512[:, H_SSM_G],
    }
    return loss, dx, big, small, small["b_ada"]


WEIGHTS = ["w_ada", "b_ada", "pre_mix_g", "w_in", "attn_sinks", "lam_re", "lam_im", "log_dt", "b_re", "b_im", "c_re",
           "c_im", "d_skip", "w_glu", "b_glu", "attn_out_g", "ssm_out_g", "w_out", "post_mix_g", "pre_mlp_g",
           "w_mlp_in", "w_mlp_out", "post_mlp_g"]
BIG = ["w_in", "w_glu", "w_out", "w_mlp_in", "w_mlp_out"]
SMALL = [n for n in WEIGHTS if n not in BIG and n != "w_ada"]
PACK_ROWS = 256


def _pack(parts):
    rows = []
    for n in SMALL:
        flat = parts[n].reshape(-1)
        pad = (-flat.shape[0]) % (PACK_ROWS * LANE)
        rows.append(jnp.pad(flat, (0, pad)).reshape(-1, LANE))
    return jnp.concatenate(rows, axis=0)


def _unpack(packed, shapes):
    out, r0 = {}, 0
    for n in SMALL:
        size = int(np.prod(shapes[n]))
        rows = -(-size // (PACK_ROWS * LANE)) * PACK_ROWS
        out[n] = packed[r0:r0 + rows].reshape(-1)[:size].reshape(shapes[n])
        r0 += rows
    return out


def kernel(x, c, w_ada, b_ada, pre_mix_g, w_in, attn_sinks, lam_re, lam_im, log_dt, b_re, b_im, c_re, c_im, d_skip, w_glu, b_glu, attn_out_g, ssm_out_g, w_out, post_mix_g, pre_mlp_g, w_mlp_in, w_mlp_out, post_mlp_g, loss_target, m_w_ada, m_b_ada, m_pre_mix_g, m_w_in, m_attn_sinks, m_lam_re, m_lam_im, m_log_dt, m_b_re, m_b_im, m_c_re, m_c_im, m_d_skip, m_w_glu, m_b_glu, m_attn_out_g, m_ssm_out_g, m_w_out, m_post_mix_g, m_pre_mlp_g, m_w_mlp_in, m_w_mlp_out, m_post_mlp_g, v_w_ada, v_b_ada, v_pre_mix_g, v_w_in, v_attn_sinks, v_lam_re, v_lam_im, v_log_dt, v_b_re, v_b_im, v_c_re, v_c_im, v_d_skip, v_w_glu, v_b_glu, v_attn_out_g, v_ssm_out_g, v_w_out, v_post_mix_g, v_pre_mlp_g, v_w_mlp_in, v_w_mlp_out, v_post_mlp_g):
    args = locals()
    wts = {n: args[n] for n in WEIGHTS}
    mom = {n: args["m_" + n] for n in WEIGHTS}
    var = {n: args["v_" + n] for n in WEIGHTS}
    nl = w_in.shape[0]
    ix, iy, ic = lax.axis_index("x"), lax.axis_index("y"), lax.axis_index("c")
    chip = 2 * ix + iy
    me = 4 * ix + 2 * iy + ic
    wcols = w_ada.shape[2]

    c_all = _gather([c.reshape(1, 1, 1, D_MODEL)], "all", "gather_c")[0].reshape(8, D_MODEL)
    b_sh = lax.dynamic_slice(b_ada, (0, chip * wcols), (nl, wcols)).reshape(nl, 1, wcols)
    mod_sh = _ada_forward(c_all, w_ada, b_sh)
    mod_all = _gather([mod_sh.reshape(1, 1, nl * 8, wcols)], "chips", "gather_mod")[0].reshape(4, nl, 8, wcols)
    mod = lax.dynamic_index_in_dim(mod_all, me, axis=2, keepdims=False)
    mod = mod.transpose(1, 0, 2).reshape(nl, 4 * wcols)

    shards = [w_in.astype(BF16).transpose(0, 2, 1), w_glu.astype(BF16), w_out.astype(BF16),
              w_mlp_in.astype(BF16).transpose(0, 2, 1), w_mlp_out.astype(BF16)]
    full = _gather_weights([s[:, None] for s in shards], "gather_weights")
    full = [f.reshape(nl, 4 * f.shape[2], f.shape[3]) for f in full]
    gathered = dict(zip(["w_in_t", "w_glu", "w_out", "w_mlp_in_t", "w_mlp_out"], full))

    small_p = {n: wts[n] for n in SMALL}
    loss, grad_x, big, small, dmod = _local_step(x[0], loss_target[0], mod, small_p, gathered)
    loss = lax.psum(loss, ("x", "y", "c"))

    pieces = [jnp.stack(big[n]) for n in BIG]
    recv = _exchange_chips(pieces, "scatter_grads")
    partial = [_sum_slots(r, "sum_pieces") for r in recv]
    pairs = _gather([pt[:, None] for pt in partial], "pair", "pair_grads")
    res = {}
    for n, gs in zip(BIG, pairs):
        res[n] = [o.reshape(wts[n].shape) for o in
                  _adamw(gs, wts[n].reshape(gs.shape[0], gs.shape[2], gs.shape[3]),
                         mom[n].reshape(gs.shape[0], gs.shape[2], gs.shape[3]),
                         var[n].reshape(gs.shape[0], gs.shape[2], gs.shape[3]), "adamw_" + n)]

    dmod_all = _gather([dmod.reshape(1, 1, nl, N_MOD * D_MODEL)], "all", "gather_dmod")[0][0]
    dmod_sh = lax.dynamic_slice(dmod_all, (0, 0, chip * wcols), (8, nl, wcols)).transpose(1, 0, 2)
    g_ada = _ada_weight_grad(c_all.T, dmod_sh)
    res["w_ada"] = _adamw(g_ada[:, None], w_ada, m_w_ada, v_w_ada, "adamw_w_ada")

    packed = _pack(small)
    rows = packed.shape[0]
    pair = _gather([packed.reshape(1, 1, rows, LANE)], "pair", "pair_small")[0]
    chip_sum = _sum_slots(pair, "sum_pair_small")
    quad = _gather([chip_sum[:, None]], "chips", "gather_small")[0]
    outs = _adamw(quad, _pack({n: wts[n] for n in SMALL})[None], _pack({n: mom[n] for n in SMALL})[None],
                  _pack({n: var[n] for n in SMALL})[None], "adamw_small")
    shapes = {n: wts[n].shape for n in SMALL}
    unpacked = [_unpack(o[0], shapes) for o in outs]
    for n in SMALL:
        res[n] = [u[n] for u in unpacked]

    return (loss, grad_x[None], *[res[n][0] for n in WEIGHTS], *[res[n][1] for n in WEIGHTS],
            *[res[n][2] for n in WEIGHTS], *[res[n][3] for n in WEIGHTS])
```

```python
import functools
import math

import numpy as np
import jax
import jax.numpy as jnp
from jax import lax
from jax.experimental import pallas as pl
from jax.experimental.pallas import tpu as pltpu

F32 = jnp.float32
BF16 = jnp.bfloat16

D_MODEL = 1024
ATTN_W = 512
SSM_W = 512
HEAD_DIM = 64
N_Q = 8
N_KV = 2
Q_PER_KV = 4
KV_W = 128
WINDOW = 128
BLOCK = 128
N_GROUPS = 32
GROUP_W = 16
STATE = 64
N_CH = N_GROUPS * STATE
HALF_CH = N_CH // 2
D_FF = 4096
IN_W = 1280
N_MOD = 6
EPS = 1e-6
NEG_INF = -1e30

ADAM_LR = 0.001
ADAM_B1 = 0.9
ADAM_B2 = 0.999
ADAM_EPS = 1e-08
ADAM_WD = 0.01
ADAM_STEP = 10

ROW_TILE = 256
CHUNK = 256
SEGS = 8
STEPS = CHUNK // SEGS
STRIP = 1024
VMEM_LIMIT_V7X = 56 * 1024 * 1024
LANE = 128
SUBLANE = 8

GELU_K0 = math.sqrt(2.0 / math.pi)
GELU_K1 = 0.044715

V_SH1, V_SC1, V_G1, V_SH2, V_SC2, V_G2, V_PRE_MIX, V_POST_MIX, V_PRE_MLP, V_POST_MLP = range(10)
H_ATTN_G, H_SSM_G, H_DSKIP, H_BGLU = range(4)

HBM = pl.BlockSpec(memory_space=pltpu.HBM)
SEM = pl.BlockSpec(memory_space=pltpu.SEMAPHORE)
EFFECT = pltpu.SideEffectType.DATAFLOW_SIDE_EFFECTING
MESH_ID = pl.DeviceIdType.MESH


def _nn(a, b):
    return lax.dot_general(a, b, (((1,), (0,)), ((), ())), preferred_element_type=F32)


def _nt(a, b):
    return lax.dot_general(a, b, (((1,), (1,)), ((), ())), preferred_element_type=F32)


def _tn(a, b):
    return lax.dot_general(a, b, (((0,), (0,)), ((), ())), preferred_element_type=F32)


def _params(sem):
    return pltpu.CompilerParams(dimension_semantics=sem, vmem_limit_bytes=VMEM_LIMIT_V7X)


def _rms_fwd(x, g):
    r = lax.rsqrt(jnp.mean(x * x, axis=-1, keepdims=True) + EPS)
    xh = x * r
    return xh * g, xh, r


def _rms_bwd(dy, xh, r, g):
    dxh = dy * g
    dx = r * (dxh - xh * jnp.mean(dxh * xh, axis=-1, keepdims=True))
    return dx, dy * xh


def _colsum(t):
    return jnp.sum(t, axis=0, keepdims=True)


def _gelu(y):
    t = jnp.tanh(GELU_K0 * (y + GELU_K1 * (y * y * y)))
    return 0.5 * y * (1.0 + t), t


def _gelu_grad(y, t):
    return 0.5 * (1.0 + t) + 0.5 * y * (1.0 - t * t) * GELU_K0 * (1.0 + 3.0 * GELU_K1 * y * y)


def _alibi_slopes():
    return [float(s) for s in 2.0 ** (-8.0 * np.arange(1, N_Q + 1) / N_Q)]


def _pick_rows(rows, bytes_per_row, budget):
    t = rows
    while t % (2 * SUBLANE) == 0 and t * bytes_per_row > budget:
        t //= 2
    return t


def _load_once(step, pairs, sems):
    @pl.when(step == 0)
    def _():
        cps = [pltpu.make_async_copy(src, dst, sems.at[k]) for k, (src, dst) in enumerate(pairs)]
        for cp in cps:
            cp.start()
        for cp in cps:
            cp.wait()


_GROUPS = {
    "all": ([(0, 0, 1), (0, 1, 0), (0, 1, 1), (1, 0, 0), (1, 0, 1), (1, 1, 0), (1, 1, 1)], (4, 2, 1), 8),
    "chips": ([(1, 0, 0), (0, 1, 0), (1, 1, 0)], (2, 1, 0), 4),
    "pair": ([(0, 0, 1)], (0, 0, 1), 2),
}


def _flip(v, f):
    return 1 - v if f else v


def _gather(arrs, kind, name):
    masks, wts, n = _GROUPS[kind]
    na, nm = len(arrs), len(masks)

    def body(*refs):
        ins, outs = refs[:na], refs[na:2 * na]
        ssem, rsem, lsem = refs[2 * na:]
        x, y, c = lax.axis_index("x"), lax.axis_index("y"), lax.axis_index("c")
        me = wts[0] * x + wts[1] * y + wts[2] * c
        local = [pltpu.make_async_copy(ins[k], outs[k].at[:, pl.ds(me, 1)], lsem.at[k]) for k in range(na)]
        for cp in local:
            cp.start()
        remote = []
        for k in range(na):
            for mi, (fx, fy, fc) in enumerate(masks):
                peer = (_flip(x, fx), _flip(y, fy), _flip(c, fc))
                remote.append(pltpu.make_async_remote_copy(
                    src_ref=ins[k], dst_ref=outs[k].at[:, pl.ds(me, 1)],
                    send_sem=ssem.at[k * nm + mi], recv_sem=rsem.at[k * nm + mi],
                    device_id=peer, device_id_type=MESH_ID))
        for cp in remote:
            cp.start()
        for cp in remote:
            cp.wait()
        for cp in local:
            cp.wait()

    outs = pl.pallas_call(
        body, name=name,
        out_shape=[jax.ShapeDtypeStruct((a.shape[0], n) + a.shape[2:], a.dtype) for a in arrs],
        in_specs=[HBM] * na, out_specs=[HBM] * na,
        scratch_shapes=[pltpu.SemaphoreType.DMA((na * nm,)), pltpu.SemaphoreType.DMA((na * nm,)),
                        pltpu.SemaphoreType.DMA((na,))],
    )(*arrs)
    return list(outs)


def _hbm(a):
    return pltpu.with_memory_space_constraint(a, pltpu.HBM)


def _after(x, *deps):
    return lax.optimization_barrier((x,) + tuple(deps))[0]


def _exchange_start(name, n_copies, plan, bufs):
    n = len(bufs)

    def body(*refs):
        ssem, rsem, token = refs[n], refs[n + 1], refs[2 * n + 2]
        for k, (src, dst, dev) in enumerate(plan(refs[:n])):
            pltpu.make_async_remote_copy(src_ref=src, dst_ref=dst, send_sem=ssem.at[k], recv_sem=rsem.at[k],
                                         device_id=dev, device_id_type=MESH_ID).start()
        token[...] = jnp.zeros_like(token)

    outs = pl.pallas_call(
        body, name=name,
        out_shape=(pltpu.SemaphoreType.DMA((n_copies,)), pltpu.SemaphoreType.DMA((n_copies,)),
                   *[pltpu.HBM(b.shape, b.dtype) for b in bufs], jax.ShapeDtypeStruct((SUBLANE, LANE), F32)),
        in_specs=[HBM] * n, out_specs=(SEM, SEM, *[HBM] * n, pl.BlockSpec(memory_space=pltpu.VMEM)),
        input_output_aliases={i: 2 + i for i in range(n)},
        compiler_params=pltpu.CompilerParams(has_side_effects=EFFECT),
    )(*[_hbm(b) for b in bufs])
    return outs[0], outs[1], list(outs[2:2 + n]), outs[2 + n]


def _exchange_wait(name, n_copies, plan, started, after):
    ssem, rsem, bufs, _ = started
    n = len(bufs)

    def body(*refs):
        ssem_ref, rsem_ref = refs[n], refs[n + 1]
        for k, (src, dst, dev) in enumerate(plan(refs[:n])):
            cp = pltpu.make_async_remote_copy(src_ref=src, dst_ref=dst, send_sem=ssem_ref.at[k], recv_sem=rsem_ref.at[k],
                                              device_id=dev, device_id_type=MESH_ID)
            cp.wait_send()
            cp.wait_recv()

    outs = pl.pallas_call(
        body, name=name,
        out_shape=tuple(pltpu.HBM(b.shape, b.dtype) for b in bufs),
        in_specs=[HBM] * n + [SEM, SEM, pl.BlockSpec(memory_space=pl.ANY)], out_specs=tuple([HBM] * n),
        input_output_aliases={i: i for i in range(n)},
        compiler_params=pltpu.CompilerParams(has_side_effects=EFFECT),
    )(*bufs, ssem, rsem, after)
    return list(outs)


def _position():
    x, y, c = lax.axis_index("x"), lax.axis_index("y"), lax.axis_index("c")
    return x, y, c, [(1 - x, y), (x, 1 - y), (1 - x, 1 - y)]


def _plan_gather(na):
    def plan(refs):
        x, y, c, chips = _position()
        return [(refs[k], refs[na + k].at[2 * x + y], (px, py, c)) for k in range(na) for px, py in chips]
    return plan


def _plan_scatter(na):
    def plan(refs):
        x, y, c, chips = _position()
        return [(refs[k].at[2 * px + py], refs[na + k].at[2 * x + y], (px, py, c))
                for k in range(na) for px, py in chips]
    return plan


def _plan_pair(na):
    def plan(refs):
        x, y, c, _ = _position()
        return [(refs[k], refs[na + k], (x, y, 1 - c)) for k in range(na)]
    return plan


def _sum_list(arrs, name):
    n = len(arrs)
    r, c = arrs[0].shape
    tr = _pick_rows(r, c * 4 * (n + 1), 4 << 20)

    def body(*refs):
        acc = refs[0][...].astype(F32)
        for j in range(1, n):
            acc = acc + refs[j][...].astype(F32)
        refs[n][...] = acc

    blk = pl.BlockSpec((tr, c), lambda i: (i, 0))
    return pl.pallas_call(
        body, name=name, grid=(r // tr,), in_specs=[blk] * n, out_specs=blk,
        out_shape=jax.ShapeDtypeStruct((r, c), F32), compiler_params=_params(("parallel",)),
    )(*arrs)


def _sum_pieces(own, recv, chip, name):
    _, r, c = own.shape
    tr = _pick_rows(r, c * 2 * 6, 4 << 20)

    def body(chip_ref, own_ref, recv_ref, o_ref):
        acc = own_ref[0].astype(F32)
        for j in range(4):
            acc = acc + jnp.where(chip_ref[0] == j, 0.0, recv_ref[j].astype(F32))
        o_ref[...] = acc.astype(BF16)

    return pl.pallas_call(
        body, name=name,
        grid_spec=pltpu.PrefetchScalarGridSpec(
            num_scalar_prefetch=1, grid=(r // tr,),
            in_specs=[pl.BlockSpec((1, tr, c), lambda i, chip_ref: (chip_ref[0], i, 0)),
                      pl.BlockSpec((4, tr, c), lambda i, chip_ref: (0, i, 0))],
            out_specs=pl.BlockSpec((tr, c), lambda i, chip_ref: (i, 0))),
        out_shape=jax.ShapeDtypeStruct((r, c), BF16), compiler_params=_params(("parallel",)),
    )(chip, own, recv)


def _adam_update(g, w, m, v):
    mn = ADAM_B1 * m + (1.0 - ADAM_B1) * g
    vn = ADAM_B2 * v + (1.0 - ADAM_B2) * jnp.square(g)
    m_hat = mn / (1.0 - ADAM_B1 ** ADAM_STEP)
    v_hat = vn / (1.0 - ADAM_B2 ** ADAM_STEP)
    return -ADAM_LR * (m_hat / (jnp.sqrt(v_hat) + ADAM_EPS) + ADAM_WD * w), mn, vn


def _adamw_layer(grads, w, m, v, layer, prev, name):
    ng = len(grads)
    nl, r, c = w.shape
    tr = _pick_rows(r, c * 4 * (ng + 7), 6 << 20)
    if prev is None:
        prev = [lax.empty((nl, r, c), F32) for _ in range(4)]

    def body(*refs):
        g = refs[0][...].astype(F32)
        for j in range(1, ng):
            g = g + refs[j][...].astype(F32)
        w_ref, m_ref, v_ref = refs[ng:ng + 3]
        go_ref, d_ref, mo_ref, vo_ref = refs[ng + 7:ng + 11]
        d, mn, vn = _adam_update(g, w_ref[0], m_ref[0], v_ref[0])
        go_ref[0] = g
        d_ref[0] = d
        mo_ref[0] = mn
        vo_ref[0] = vn

    gblk = pl.BlockSpec((tr, c), lambda i: (i, 0))
    blk = pl.BlockSpec((1, tr, c), lambda i: (layer, i, 0))
    keep = pl.BlockSpec(memory_space=pl.ANY)
    sds = jax.ShapeDtypeStruct((nl, r, c), F32)
    return pl.pallas_call(
        body, name=name, grid=(r // tr,),
        in_specs=[gblk] * ng + [blk] * 3 + [keep] * 4,
        out_specs=[blk] * 4, out_shape=[sds] * 4,
        input_output_aliases={ng + 3 + i: i for i in range(4)},
        compiler_params=_params(("parallel",)),
    )(*grads, w, m, v, *prev)


def _adamw(gs, w, m, v, name):
    a, s, r, c = gs.shape
    tr = _pick_rows(r, c * 4 * (s + 7), 6 << 20)

    def body(g_ref, w_ref, m_ref, v_ref, go_ref, d_ref, mo_ref, vo_ref):
        g = g_ref[0, 0].astype(F32)
        for j in range(1, s):
            g = g + g_ref[0, j].astype(F32)
        d, mn, vn = _adam_update(g, w_ref[0], m_ref[0], v_ref[0])
        go_ref[0] = g
        d_ref[0] = d
        mo_ref[0] = mn
        vo_ref[0] = vn

    blk = pl.BlockSpec((1, tr, c), lambda i, j: (i, j, 0))
    sds = jax.ShapeDtypeStruct((a, r, c), F32)
    return pl.pallas_call(
        body, name=name, grid=(a, r // tr),
        in_specs=[pl.BlockSpec((1, s, tr, c), lambda i, j: (i, 0, j, 0)), blk, blk, blk],
        out_specs=[blk, blk, blk, blk], out_shape=[sds, sds, sds, sds],
        compiler_params=_params(("parallel", "parallel")),
    )(gs, w, m, v)


def _ada_forward(c_all, w_ada, b_sh):
    nl, d, w = w_ada.shape
    tw = 512

    def body(c_ref, w_ref, b_ref, o_ref):
        cv = c_ref[...]
        act = (cv * jax.nn.sigmoid(cv)).astype(BF16)
        o_ref[0] = _nn(act, w_ref[0].astype(BF16)) + b_ref[0]

    return pl.pallas_call(
        body, name="ada_forward", grid=(nl, w // tw),
        in_specs=[pl.BlockSpec((8, d), lambda l, j: (0, 0)),
                  pl.BlockSpec((1, d, tw), lambda l, j: (l, 0, j)),
                  pl.BlockSpec((1, 1, tw), lambda l, j: (l, 0, j))],
        out_specs=pl.BlockSpec((1, 8, tw), lambda l, j: (l, 0, j)),
        out_shape=jax.ShapeDtypeStruct((nl, 8, w), F32),
        compiler_params=_params(("parallel", "parallel")),
    )(c_all, w_ada, b_sh)


def _ada_weight_grad(c_all_t, dmod):
    nl, nb, w = dmod.shape
    d = c_all_t.shape[0]
    tw = 512

    def body(c_ref, g_ref, o_ref):
        cv = c_ref[...]
        act = cv * jax.nn.sigmoid(cv)
        gv = g_ref[0]
        acc = act[:, 0:1] * gv[0:1, :]
        for b in range(1, nb):
            acc = acc + act[:, b:b + 1] * gv[b:b + 1, :]
        o_ref[0] = acc

    return pl.pallas_call(
        body, name="ada_weight_grad", grid=(nl, w // tw),
        in_specs=[pl.BlockSpec((d, nb), lambda l, j: (0, 0)),
                  pl.BlockSpec((1, nb, tw), lambda l, j: (l, 0, j))],
        out_specs=pl.BlockSpec((1, d, tw), lambda l, j: (l, 0, j)),
        out_shape=jax.ShapeDtypeStruct((nl, d, w), F32),
        compiler_params=_params(("parallel", "parallel")),
    )(c_all_t, dmod)


def _in_proj_fwd(x, vec, w_in_t, layer):
    seq = x.shape[0]
    tm = ROW_TILE

    def body(x_ref, vec_ref, w_ref, q_ref, kv_ref, u4_ref, h_ref):
        n, _, _ = _rms_fwd(x_ref[...], vec_ref[0, V_PRE_MIX:V_PRE_MIX + 1, :])
        h = (n * (1.0 + vec_ref[0, V_SC1:V_SC1 + 1, :]) + vec_ref[0, V_SH1:V_SH1 + 1, :]).astype(BF16)
        h_ref[...] = h
        proj = _nt(h, w_ref[...])
        q_ref[...] = proj[:, :ATTN_W].astype(BF16)
        kv_ref[...] = proj[:, ATTN_W:ATTN_W + 2 * KV_W].astype(BF16)
        u0 = ATTN_W + 2 * KV_W
        for j in range(4):
            u4_ref[j] = proj[:, u0 + j * LANE:u0 + (j + 1) * LANE]

    return pl.pallas_call(
        body, name="in_proj_fwd", grid=(seq // tm,),
        in_specs=[pl.BlockSpec((tm, D_MODEL), lambda i: (i, 0)),
                  pl.BlockSpec((1, 16, D_MODEL), lambda i: (layer, 0, 0)),
                  pl.BlockSpec((IN_W, D_MODEL), lambda i: (0, 0))],
        out_specs=[pl.BlockSpec((tm, ATTN_W), lambda i: (i, 0)),
                   pl.BlockSpec((tm, 2 * KV_W), lambda i: (i, 0)),
                   pl.BlockSpec((4, tm, LANE), lambda i: (0, i, 0)),
                   pl.BlockSpec((tm, D_MODEL), lambda i: (i, 0))],
        out_shape=[jax.ShapeDtypeStruct((seq, ATTN_W), BF16), jax.ShapeDtypeStruct((seq, 2 * KV_W), BF16),
                   jax.ShapeDtypeStruct((4, seq, LANE), F32), jax.ShapeDtypeStruct((seq, D_MODEL), BF16)],
        compiler_params=_params(("parallel",)),
    )(x, vec, w_in_t)


def _in_proj_bwd(dx1, dq, dkv, du4, x, vec, w_in_t, layer):
    seq = x.shape[0]
    tm = ROW_TILE

    def body(dx1_ref, dq_ref, dkv_ref, du4_ref, x_ref, vec_ref, w_ref, dx_ref, dp_ref, dvec_ref):
        i = pl.program_id(0)

        @pl.when(i == 0)
        def _():
            dvec_ref[...] = jnp.zeros_like(dvec_ref)

        dproj = jnp.concatenate([dq_ref[...], dkv_ref[...]] + [du4_ref[j] for j in range(4)], axis=1).astype(BF16)
        dp_ref[...] = dproj
        dh = _nn(dproj, w_ref[...])
        g = vec_ref[0, V_PRE_MIX:V_PRE_MIX + 1, :]
        n, xh, r = _rms_fwd(x_ref[...], g)
        dn = dh * (1.0 + vec_ref[0, V_SC1:V_SC1 + 1, :])
        dxn, dg_rows = _rms_bwd(dn, xh, r, g)
        dx_ref[...] = dx1_ref[...] + dxn
        dvec_ref[V_SH1:V_SH1 + 1, :] += _colsum(dh)
        dvec_ref[V_SC1:V_SC1 + 1, :] += _colsum(dh * n)
        dvec_ref[V_PRE_MIX:V_PRE_MIX + 1, :] += _colsum(dg_rows)

    row = pl.BlockSpec((tm, D_MODEL), lambda i: (i, 0))
    return pl.pallas_call(
        body, name="in_proj_bwd", grid=(seq // tm,),
        in_specs=[row, pl.BlockSpec((tm, ATTN_W), lambda i: (i, 0)), pl.BlockSpec((tm, 2 * KV_W), lambda i: (i, 0)),
                  pl.BlockSpec((4, tm, LANE), lambda i: (0, i, 0)), row,
                  pl.BlockSpec((1, 16, D_MODEL), lambda i: (layer, 0, 0)),
                  pl.BlockSpec((IN_W, D_MODEL), lambda i: (0, 0))],
        out_specs=[row, pl.BlockSpec((tm, IN_W), lambda i: (i, 0)), pl.BlockSpec((16, D_MODEL), lambda i: (0, 0))],
        out_shape=[jax.ShapeDtypeStruct((seq, D_MODEL), F32), jax.ShapeDtypeStruct((seq, IN_W), BF16),
                   jax.ShapeDtypeStruct((16, D_MODEL), F32)],
        compiler_params=_params(("arbitrary",)),
    )(dx1, dq, dkv, du4, x, vec, w_in_t)


def _heads(attn_ref, s4_ref, vec512_ref):
    ga = vec512_ref[0, H_ATTN_G:H_ATTN_G + 1, :]
    gs = vec512_ref[0, H_SSM_G:H_SSM_G + 1, :]
    sv = jnp.concatenate([s4_ref[j] for j in range(4)], axis=1)
    na, ah, ar = _rms_fwd(attn_ref[...], ga)
    ns, sh, sr = _rms_fwd(sv, gs)
    return jnp.concatenate([na, ns], axis=1), (ah, ar, ga), (sh, sr, gs)


def _out_proj_fwd(x, attn, s4, vec, vec512, w_out, layer):
    seq = x.shape[0]
    tm = ROW_TILE

    def body(x_ref, attn_ref, s4_ref, vec_ref, vec512_ref, w_ref, x1_ref):
        heads, _, _ = _heads(attn_ref, s4_ref, vec512_ref)
        mixed = _nn(heads.astype(BF16), w_ref[...])
        nm, _, _ = _rms_fwd(mixed, vec_ref[0, V_POST_MIX:V_POST_MIX + 1, :])
        x1_ref[...] = x_ref[...] + vec_ref[0, V_G1:V_G1 + 1, :] * nm

    row = pl.BlockSpec((tm, D_MODEL), lambda i: (i, 0))
    return pl.pallas_call(
        body, name="out_proj_fwd", grid=(seq // tm,),
        in_specs=[row, pl.BlockSpec((tm, ATTN_W), lambda i: (i, 0)), pl.BlockSpec((4, tm, LANE), lambda i: (0, i, 0)),
                  pl.BlockSpec((1, 16, D_MODEL), lambda i: (layer, 0, 0)),
                  pl.BlockSpec((1, 8, SSM_W), lambda i: (layer, 0, 0)),
                  pl.BlockSpec((D_MODEL, D_MODEL), lambda i: (0, 0))],
        out_specs=row, out_shape=jax.ShapeDtypeStruct((seq, D_MODEL), F32),
        compiler_params=_params(("parallel",)),
    )(x, attn, s4, vec, vec512, w_out)


def _out_proj_bwd(dx1, attn, s4, vec, vec512, w_out, layer):
    seq = dx1.shape[0]
    tm = ROW_TILE

    def body(dx1_ref, attn_ref, s4_ref, vec_ref, vec512_ref, w_ref,
             dattn_ref, ds4_ref, heads_ref, dmixed_ref, dvec_ref, dvec512_ref):
        i = pl.program_id(0)

        @pl.when(i == 0)
        def _():
            dvec_ref[...] = jnp.zeros_like(dvec_ref)
            dvec512_ref[...] = jnp.zeros_like(dvec512_ref)

        heads, (ah, ar, ga), (sh, sr, gs) = _heads(attn_ref, s4_ref, vec512_ref)
        hb = heads.astype(BF16)
        heads_ref[...] = hb
        gm = vec_ref[0, V_POST_MIX:V_POST_MIX + 1, :]
        nm, mh, mr = _rms_fwd(_nn(hb, w_ref[...]), gm)
        dx1v = dx1_ref[...]
        dvec_ref[V_G1:V_G1 + 1, :] += _colsum(dx1v * nm)
        dmixed, dgm_rows = _rms_bwd(dx1v * vec_ref[0, V_G1:V_G1 + 1, :], mh, mr, gm)
        dvec_ref[V_POST_MIX:V_POST_MIX + 1, :] += _colsum(dgm_rows)
        dmb = dmixed.astype(BF16)
        dmixed_ref[...] = dmb
        dheads = _nt(dmb, w_ref[...])
        dattn, dga_rows = _rms_bwd(dheads[:, :ATTN_W], ah, ar, ga)
        ds, dgs_rows = _rms_bwd(dheads[:, ATTN_W:], sh, sr, gs)
        dattn_ref[...] = dattn
        for j in range(4):
            ds4_ref[j] = ds[:, j * LANE:(j + 1) * LANE]
        dvec512_ref[H_ATTN_G:H_ATTN_G + 1, :] += _colsum(dga_rows)
        dvec512_ref[H_SSM_G:H_SSM_G + 1, :] += _colsum(dgs_rows)

    row = pl.BlockSpec((tm, D_MODEL), lambda i: (i, 0))
    return pl.pallas_call(
        body, name="out_proj_bwd", grid=(seq // tm,),
        in_specs=[row, pl.BlockSpec((tm, ATTN_W), lambda i: (i, 0)), pl.BlockSpec((4, tm, LANE), lambda i: (0, i, 0)),
                  pl.BlockSpec((1, 16, D_MODEL), lambda i: (layer, 0, 0)),
                  pl.BlockSpec((1, 8, SSM_W), lambda i: (layer, 0, 0)),
                  pl.BlockSpec((D_MODEL, D_MODEL), lambda i: (0, 0))],
        out_specs=[pl.BlockSpec((tm, ATTN_W), lambda i: (i, 0)), pl.BlockSpec((4, tm, LANE), lambda i: (0, i, 0)),
                   row, row, pl.BlockSpec((16, D_MODEL), lambda i: (0, 0)), pl.BlockSpec((8, SSM_W), lambda i: (0, 0))],
        out_shape=[jax.ShapeDtypeStruct((seq, ATTN_W), F32), jax.ShapeDtypeStruct((4, seq, LANE), F32),
                   jax.ShapeDtypeStruct((seq, D_MODEL), BF16), jax.ShapeDtypeStruct((seq, D_MODEL), BF16),
                   jax.ShapeDtypeStruct((16, D_MODEL), F32), jax.ShapeDtypeStruct((8, SSM_W), F32)],
        compiler_params=_params(("arbitrary",)),
    )(dx1, attn, s4, vec, vec512, w_out)


def _mlp_fwd(x1, vec, w_in_t, w_out, layer):
    seq = x1.shape[0]
    tm = ROW_TILE

    def body(x1_ref, vec_ref, wi_hbm, wo_hbm, x2_ref, wi, wo, sems):
        _load_once(pl.program_id(0), [(wi_hbm, wi), (wo_hbm, wo)], sems)
        x1v = x1_ref[...]
        n, _, _ = _rms_fwd(x1v, vec_ref[0, V_PRE_MLP:V_PRE_MLP + 1, :])
        h = (n * (1.0 + vec_ref[0, V_SC2:V_SC2 + 1, :]) + vec_ref[0, V_SH2:V_SH2 + 1, :]).astype(BF16)
        a = _nt(h, wi[...])
        r = jnp.square(jnp.maximum(a, 0.0)).astype(BF16)
        nf, _, _ = _rms_fwd(_nn(r, wo[...]), vec_ref[0, V_POST_MLP:V_POST_MLP + 1, :])
        x2_ref[...] = x1v + vec_ref[0, V_G2:V_G2 + 1, :] * nf

    row = pl.BlockSpec((tm, D_MODEL), lambda i: (i, 0))
    return pl.pallas_call(
        body, name="mlp_fwd", grid=(seq // tm,),
        in_specs=[row, pl.BlockSpec((1, 16, D_MODEL), lambda i: (layer, 0, 0)), HBM, HBM],
        out_specs=row, out_shape=jax.ShapeDtypeStruct((seq, D_MODEL), F32),
        scratch_shapes=[pltpu.VMEM((D_FF, D_MODEL), BF16), pltpu.VMEM((D_FF, D_MODEL), BF16),
                        pltpu.SemaphoreType.DMA((2,))],
        compiler_params=_params(("arbitrary",)),
    )(x1, vec, w_in_t, w_out)


def _mlp_bwd(dx2, x1, vec, w_in_t, w_out, layer):
    seq = x1.shape[0]
    tm = ROW_TILE

    def body(dx2_ref, x1_ref, vec_ref, wi_hbm, wo_hbm, dx1_ref, h_ref, da_ref, r_ref, df_ref, dvec_ref, wi, wo, sems):
        i = pl.program_id(0)
        _load_once(i, [(wi_hbm, wi), (wo_hbm, wo)], sems)

        @pl.when(i == 0)
        def _():
            dvec_ref[...] = jnp.zeros_like(dvec_ref)

        g_pre = vec_ref[0, V_PRE_MLP:V_PRE_MLP + 1, :]
        g_post = vec_ref[0, V_POST_MLP:V_POST_MLP + 1, :]
        sc2 = vec_ref[0, V_SC2:V_SC2 + 1, :]
        n, xh, xr = _rms_fwd(x1_ref[...], g_pre)
        h = (n * (1.0 + sc2) + vec_ref[0, V_SH2:V_SH2 + 1, :]).astype(BF16)
        h_ref[...] = h
        a = _nt(h, wi[...])
        relu = jnp.maximum(a, 0.0)
        r = jnp.square(relu).astype(BF16)
        r_ref[...] = r
        nf, fh, fr = _rms_fwd(_nn(r, wo[...]), g_post)
        dx2v = dx2_ref[...]
        dvec_ref[V_G2:V_G2 + 1, :] += _colsum(dx2v * nf)
        df, dgp_rows = _rms_bwd(dx2v * vec_ref[0, V_G2:V_G2 + 1, :], fh, fr, g_post)
        dvec_ref[V_POST_MLP:V_POST_MLP + 1, :] += _colsum(dgp_rows)
        dfb = df.astype(BF16)
        df_ref[...] = dfb
        da = (_nt(dfb, wo[...]) * (2.0 * relu)).astype(BF16)
        da_ref[...] = da
        dh = _nn(da, wi[...])
        dvec_ref[V_SH2:V_SH2 + 1, :] += _colsum(dh)
        dvec_ref[V_SC2:V_SC2 + 1, :] += _colsum(dh * n)
        dxn, dg_rows = _rms_bwd(dh * (1.0 + sc2), xh, xr, g_pre)
        dvec_ref[V_PRE_MLP:V_PRE_MLP + 1, :] += _colsum(dg_rows)
        dx1_ref[...] = dx2v + dxn

    row = pl.BlockSpec((tm, D_MODEL), lambda i: (i, 0))
    wide = pl.BlockSpec((tm, D_FF), lambda i: (i, 0))
    return pl.pallas_call(
        body, name="mlp_bwd", grid=(seq // tm,),
        in_specs=[row, row, pl.BlockSpec((1, 16, D_MODEL), lambda i: (layer, 0, 0)), HBM, HBM],
        out_specs=[row, row, wide, wide, row, pl.BlockSpec((16, D_MODEL), lambda i: (0, 0))],
        out_shape=[jax.ShapeDtypeStruct((seq, D_MODEL), F32), jax.ShapeDtypeStruct((seq, D_MODEL), BF16),
                   jax.ShapeDtypeStruct((seq, D_FF), BF16), jax.ShapeDtypeStruct((seq, D_FF), BF16),
                   jax.ShapeDtypeStruct((seq, D_MODEL), BF16), jax.ShapeDtypeStruct((16, D_MODEL), F32)],
        scratch_shapes=[pltpu.VMEM((D_FF, D_MODEL), BF16), pltpu.VMEM((D_FF, D_MODEL), BF16),
                        pltpu.SemaphoreType.DMA((2,))],
        compiler_params=_params(("arbitrary",)),
    )(dx2, x1, vec, w_in_t, w_out)


def _loss_head(y, target):
    seq = y.shape[0]
    tm = ROW_TILE

    def body(y_ref, t_ref, dy_ref, part_ref):
        e = y_ref[...] - t_ref[...]
        dy_ref[...] = e * (1.0 / D_MODEL)
        tot = jnp.sum(jnp.sum(e * e, axis=1, keepdims=True), axis=0, keepdims=True) * (0.5 / D_MODEL)
        part_ref[0] = jnp.broadcast_to(tot, (SUBLANE, LANE))

    row = pl.BlockSpec((tm, D_MODEL), lambda i: (i, 0))
    return pl.pallas_call(
        body, name="loss_head", grid=(seq // tm,),
        in_specs=[row, row],
        out_specs=[row, pl.BlockSpec((1, SUBLANE, LANE), lambda i: (i, 0, 0))],
        out_shape=[jax.ShapeDtypeStruct((seq, D_MODEL), F32), jax.ShapeDtypeStruct((seq // tm, SUBLANE, LANE), F32)],
        compiler_params=_params(("parallel",)),
    )(y, target)


def _matmul_tn(a, b, out_dtype, name, pieces=1):
    kk, m = a.shape
    n = b.shape[1]
    tm = min(m, 512)
    tn = n // pieces if pieces > 1 else min(n, 1280)
    tk = 512
    nk = kk // tk

    def body(a_ref, b_ref, o_ref, acc):
        k = pl.program_id(2)

        @pl.when(k == 0)
        def _():
            acc[...] = jnp.zeros_like(acc)

        acc[...] += _tn(a_ref[...], b_ref[...])

        @pl.when(k == nk - 1)
        def _():
            if pieces > 1:
                o_ref[0] = acc[...].astype(out_dtype)
            else:
                o_ref[...] = acc[...].astype(out_dtype)

    if pieces > 1:
        out_spec = pl.BlockSpec((1, tm, tn), lambda i, j, k: (j, i, 0))
        out_shape = jax.ShapeDtypeStruct((pieces, m, tn), out_dtype)
    else:
        out_spec = pl.BlockSpec((tm, tn), lambda i, j, k: (i, j))
        out_shape = jax.ShapeDtypeStruct((m, n), out_dtype)
    return pl.pallas_call(
        body, name=name, grid=(m // tm, n // tn, nk),
        in_specs=[pl.BlockSpec((tk, tm), lambda i, j, k: (k, i)), pl.BlockSpec((tk, tn), lambda i, j, k: (k, j))],
        out_specs=out_spec, out_shape=out_shape,
        scratch_shapes=[pltpu.VMEM((tm, tn), F32)],
        compiler_params=_params(("parallel", "parallel", "arbitrary")),
    )(a, b)


def _attn_probs(i, q_h, kband, slope, sink):
    rr = lax.broadcasted_iota(jnp.int32, (BLOCK, 2 * BLOCK), 0)
    jj = lax.broadcasted_iota(jnp.int32, (BLOCK, 2 * BLOCK), 1)
    diff = BLOCK + rr - jj
    valid = (diff >= 0) & (diff < WINDOW) & ((jj >= BLOCK) | (i > 0))
    s = _nt(q_h, kband) * (HEAD_DIM ** -0.5)
    s = jnp.where(valid, s - slope * diff.astype(F32), NEG_INF)
    m = jnp.maximum(jnp.max(s, axis=1, keepdims=True), sink)
    p = jnp.exp(s - m)
    ps = jnp.exp(sink - m)
    inv = 1.0 / (jnp.sum(p, axis=1, keepdims=True) + ps)
    return p * inv, ps * inv


def _bands(kvp, kvc, h):
    kband = jnp.concatenate([kvp[:, h * HEAD_DIM:(h + 1) * HEAD_DIM], kvc[:, h * HEAD_DIM:(h + 1) * HEAD_DIM]], axis=0)
    v0 = KV_W + h * HEAD_DIM
    vband = jnp.concatenate([kvp[:, v0:v0 + HEAD_DIM], kvc[:, v0:v0 + HEAD_DIM]], axis=0)
    return kband, vband


def _attn_fwd(q, kv, sinks):
    seq = q.shape[0]
    nb = seq // BLOCK
    slopes = _alibi_slopes()

    def body(sink_ref, q_ref, kvp_ref, kvc_ref, o_ref):
        i = pl.program_id(0)
        qv, kvp, kvc = q_ref[...], kvp_ref[...], kvc_ref[...]
        for h in range(N_KV):
            kband, vband = _bands(kvp, kvc, h)
            for g in range(Q_PER_KV):
                hq = h * Q_PER_KV + g
                cols = slice(hq * HEAD_DIM, (hq + 1) * HEAD_DIM)
                pr, _ = _attn_probs(i, qv[:, cols], kband, slopes[hq], sink_ref[hq])
                o_ref[:, cols] = _nn(pr.astype(BF16), vband)

    return pl.pallas_call(
        body, name="attn_fwd", grid=(nb,),
        in_specs=[pl.BlockSpec(memory_space=pltpu.SMEM),
                  pl.BlockSpec((BLOCK, ATTN_W), lambda i: (i, 0)),
                  pl.BlockSpec((BLOCK, 2 * KV_W), lambda i: (jnp.maximum(i - 1, 0), 0)),
                  pl.BlockSpec((BLOCK, 2 * KV_W), lambda i: (i, 0))],
        out_specs=pl.BlockSpec((BLOCK, ATTN_W), lambda i: (i, 0)),
        out_shape=jax.ShapeDtypeStruct((seq, ATTN_W), F32),
        compiler_params=_params(("parallel",)),
    )(sinks, q, kv, kv)


def _attn_bwd(q, kv, sinks, dout):
    seq = q.shape[0]
    nb = seq // BLOCK
    slopes = _alibi_slopes()
    scale = HEAD_DIM ** -0.5

    def body(sink_ref, q_ref, kvp_ref, kvc_ref, do_ref, dq_ref, dkv_ref, dsk_ref, prev):
        step = pl.program_id(0)
        i = nb - 1 - step

        @pl.when(step == 0)
        def _():
            prev[...] = jnp.zeros_like(prev)

        qv, kvp, kvc = q_ref[...], kvp_ref[...], kvc_ref[...]
        dov = do_ref[...].astype(BF16)
        dk, dv, dsk = [], [], []
        for h in range(N_KV):
            kband, vband = _bands(kvp, kvc, h)
            dk_h = jnp.zeros((2 * BLOCK, HEAD_DIM), F32)
            dv_h = jnp.zeros((2 * BLOCK, HEAD_DIM), F32)
            for g in range(Q_PER_KV):
                hq = h * Q_PER_KV + g
                cols = slice(hq * HEAD_DIM, (hq + 1) * HEAD_DIM)
                q_h, do_h = qv[:, cols], dov[:, cols]
                pr, ps = _attn_probs(i, q_h, kband, slopes[hq], sink_ref[hq])
                dp = _nt(do_h, vband)
                delta = jnp.sum(pr * dp, axis=1, keepdims=True)
                ds = (pr * (dp - delta) * scale).astype(BF16)
                dsk.append(jnp.broadcast_to(-_colsum(ps * delta), (1, LANE)))
                dq_ref[:, cols] = _nn(ds, kband)
                dk_h = dk_h + _tn(ds, q_h)
                dv_h = dv_h + _tn(pr.astype(BF16), do_h)
            dk.append(dk_h)
            dv.append(dv_h)
        band = jnp.concatenate(dk + dv, axis=1)
        dkv_ref[...] = band[BLOCK:, :] + prev[...]
        prev[...] = band[:BLOCK, :]
        dsk_ref[0] = jnp.concatenate(dsk, axis=0)

    return pl.pallas_call(
        body, name="attn_bwd", grid=(nb,),
        in_specs=[pl.BlockSpec(memory_space=pltpu.SMEM),
                  pl.BlockSpec((BLOCK, ATTN_W), lambda s: (nb - 1 - s, 0)),
                  pl.BlockSpec((BLOCK, 2 * KV_W), lambda s: (jnp.maximum(nb - 2 - s, 0), 0)),
                  pl.BlockSpec((BLOCK, 2 * KV_W), lambda s: (nb - 1 - s, 0)),
                  pl.BlockSpec((BLOCK, ATTN_W), lambda s: (nb - 1 - s, 0))],
        out_specs=[pl.BlockSpec((BLOCK, ATTN_W), lambda s: (nb - 1 - s, 0)),
                   pl.BlockSpec((BLOCK, 2 * KV_W), lambda s: (nb - 1 - s, 0)),
                   pl.BlockSpec((1, N_Q, LANE), lambda s: (nb - 1 - s, 0, 0))],
        out_shape=[jax.ShapeDtypeStruct((seq, ATTN_W), F32), jax.ShapeDtypeStruct((seq, 2 * KV_W), F32),
                   jax.ShapeDtypeStruct((nb, N_Q, LANE), F32)],
        scratch_shapes=[pltpu.VMEM((BLOCK, 2 * KV_W), F32)],
        compiler_params=_params(("arbitrary",)),
    )(sinks, q, kv, kv, dout)


def _discretize(lr, li, ldt, br, bi):
    dt = jnp.exp(ldt)
    mag = jnp.exp(lr * dt)
    ang = li * dt
    ab_r = mag * jnp.cos(ang)
    ab_i = mag * jnp.sin(ang)
    nr = ab_r - 1.0
    ni = ab_i
    den = lr * lr + li * li
    f_r = (nr * lr + ni * li) / den
    f_i = (ni * lr - nr * li) / den
    return ab_r, ab_i, f_r * br - f_i * bi, f_r * bi + f_i * br


def _ssm_prepare(lr, li, ldt, br, bi):
    n = lr.shape[0]
    tn = N_CH
    col = pl.BlockSpec((tn, 1), lambda i: (i, 0))
    mat = pl.BlockSpec((tn, GROUP_W), lambda i: (i, 0))

    def body(lr_ref, li_ref, ldt_ref, br_ref, bi_ref, ar_ref, ai_ref, bbr_ref, bbi_ref):
        ar, ai, bbr, bbi = _discretize(lr_ref[...], li_ref[...], ldt_ref[...], br_ref[...], bi_ref[...])
        ar_ref[...] = ar
        ai_ref[...] = ai
        bbr_ref[...] = bbr
        bbi_ref[...] = bbi

    cs = jax.ShapeDtypeStruct((n, 1), F32)
    ms = jax.ShapeDtypeStruct((n, GROUP_W), F32)
    return pl.pallas_call(
        body, name="ssm_prepare", grid=(n // tn,),
        in_specs=[col, col, col, mat, mat], out_specs=[col, col, mat, mat], out_shape=[cs, cs, ms, ms],
        compiler_params=_params(("parallel",)),
    )(lr, li, ldt, br, bi)


def _ssm_prepare_bwd(lr, li, ldt, br, bi, dar, dai, dbbr, dbbi):
    n = lr.shape[0]
    tn = N_CH
    col = pl.BlockSpec((tn, 1), lambda i: (i, 0))
    mat = pl.BlockSpec((tn, GROUP_W), lambda i: (i, 0))

    def body(lr_ref, li_ref, ldt_ref, br_ref, bi_ref, dar_ref, dai_ref, dbbr_ref, dbbi_ref,
             dlr_ref, dli_ref, dldt_ref, dbr_ref, dbi_ref):
        _, vjp = jax.vjp(_discretize, lr_ref[...], li_ref[...], ldt_ref[...], br_ref[...], bi_ref[...])
        dlr, dli, dldt, dbr, dbi = vjp((dar_ref[...], dai_ref[...], dbbr_ref[...], dbbi_ref[...]))
        dlr_ref[...] = dlr
        dli_ref[...] = dli
        dldt_ref[...] = dldt
        dbr_ref[...] = dbr
        dbi_ref[...] = dbi

    cs = jax.ShapeDtypeStruct((n, 1), F32)
    ms = jax.ShapeDtypeStruct((n, GROUP_W), F32)
    return pl.pallas_call(
        body, name="ssm_prepare_bwd", grid=(n // tn,),
        in_specs=[col, col, col, mat, mat, col, col, mat, mat],
        out_specs=[col, col, col, mat, mat], out_shape=[cs, cs, cs, ms, ms],
        compiler_params=_params(("parallel",)),
    )(lr, li, ldt, br, bi, dar, dai, dbbr, dbbi)


def _load_slabs(src4_ref, dst):
    for s in range(STEPS):
        dst[s * SEGS:(s + 1) * SEGS, :] = jnp.concatenate(
            [src4_ref[j, pl.ds(s, SEGS, stride=STEPS), :] for j in range(4)], axis=1)


def _store_slabs(src, dst4_ref):
    for s in range(STEPS):
        for j in range(4):
            dst4_ref[j, pl.ds(s, SEGS, stride=STEPS), :] = src[s * SEGS:(s + 1) * SEGS, j * LANE:(j + 1) * LANE]


def _power_table(ar_ref, ai_ref, pwr, pwi):
    ar, ai = ar_ref[0], ai_ref[0]
    pr, pi = ar, ai
    pwr[0:1, :] = pr
    pwi[0:1, :] = pi
    for k in range(1, STEPS):
        pr, pi = pr * ar - pi * ai, pr * ai + pi * ar
        pwr[k:k + 1, :] = pr
        pwi[k:k + 1, :] = pi


def _scan_states(ubf, ar_ref, ai_ref, bbr_ref, bbi_ref, pwr, pwi, cin_r, cin_i, hr, hi):
    for k in range(2):
        rows = slice(k * 256, (k + 1) * 256)
        cols = slice(k * HALF_CH, (k + 1) * HALF_CH)
        hr[:, cols] = _nn(ubf[:, rows], bbr_ref[0, k])
        hi[:, cols] = _nn(ubf[:, rows], bbi_ref[0, k])
    for st in range(N_CH // STRIP):
        cs = slice(st * STRIP, (st + 1) * STRIP)
        arb = jnp.broadcast_to(ar_ref[0, :, cs], (SEGS, STRIP))
        aib = jnp.broadcast_to(ai_ref[0, :, cs], (SEGS, STRIP))

        def step(s, carry, cs=cs, arb=arb, aib=aib):
            cr, ci = carry
            rows = pl.ds(pl.multiple_of(s * SEGS, SEGS), SEGS)
            nr = arb * cr - aib * ci + hr[rows, cs]
            ni = arb * ci + aib * cr + hi[rows, cs]
            hr[rows, cs] = nr
            hi[rows, cs] = ni
            return nr, ni

        zero = jnp.zeros((SEGS, STRIP), F32)
        lax.fori_loop(0, STEPS, step, (zero, zero))
    last = slice((STEPS - 1) * SEGS, STEPS * SEGS)
    end_r, end_i = hr[last, :], hi[last, :]
    a64r, a64i = pwr[STEPS - 1:STEPS, :], pwi[STEPS - 1:STEPS, :]
    cr, ci = cin_r, cin_i
    rows_r, rows_i = [], []
    for j in range(SEGS):
        rows_r.append(cr)
        rows_i.append(ci)
        cr, ci = (a64r * cr - a64i * ci + end_r[j:j + 1, :], a64r * ci + a64i * cr + end_i[j:j + 1, :])
    cm_r, cm_i = jnp.concatenate(rows_r, axis=0), jnp.concatenate(rows_i, axis=0)
    for st in range(N_CH // STRIP):
        cs = slice(st * STRIP, (st + 1) * STRIP)
        cmr, cmi = cm_r[:, cs], cm_i[:, cs]

        def fix(s, carry, cs=cs, cmr=cmr, cmi=cmi):
            rows = pl.ds(pl.multiple_of(s * SEGS, SEGS), SEGS)
            pr, pi = pwr[pl.ds(s, 1), cs], pwi[pl.ds(s, 1), cs]
            hr[rows, cs] = hr[rows, cs] + (pr * cmr - pi * cmi)
            hi[rows, cs] = hi[rows, cs] + (pr * cmi + pi * cmr)
            return carry

        lax.fori_loop(0, STEPS, fix, 0)
    return (cm_r, cm_i), (cr, ci)


def _ssm_outputs(u, hr, hi, crt_ref, cit_ref, vec512_ref, wg_ref):
    ys = []
    for k in range(2):
        cols = slice(k * HALF_CH, (k + 1) * HALF_CH)
        ys.append(_nn(hr[:, cols].astype(BF16), crt_ref[0, k]) - _nn(hi[:, cols].astype(BF16), cit_ref[0, k]))
    y = jnp.concatenate(ys, axis=1) + vec512_ref[0, H_DSKIP:H_DSKIP + 1, :] * u
    z, t = _gelu(y)
    gate = jax.nn.sigmoid(_nn(z.astype(BF16), wg_ref[...]) + vec512_ref[0, H_BGLU:H_BGLU + 1, :])
    return y, z, t, gate


def _ssm_specs(layer, nck, rev):
    def chunk(i):
        return nck - 1 - i if rev else i

    return [pl.BlockSpec((4, CHUNK, LANE), lambda i: (0, chunk(i), 0)),
            pl.BlockSpec((1, 1, N_CH), lambda i: (layer, 0, 0)),
            pl.BlockSpec((1, 1, N_CH), lambda i: (layer, 0, 0)),
            pl.BlockSpec((1, 2, 256, HALF_CH), lambda i: (layer, 0, 0, 0)),
            pl.BlockSpec((1, 2, 256, HALF_CH), lambda i: (layer, 0, 0, 0)),
            pl.BlockSpec((1, 2, HALF_CH, 256), lambda i: (layer, 0, 0, 0)),
            pl.BlockSpec((1, 2, HALF_CH, 256), lambda i: (layer, 0, 0, 0)),
            pl.BlockSpec((1, 8, SSM_W), lambda i: (layer, 0, 0)),
            pl.BlockSpec((SSM_W, SSM_W), lambda i: (0, 0))]


def _ssm_fwd(u4, a_r, a_i, bb_r, bb_i, c_rt, c_it, vec512, w_glu, layer):
    seq = u4.shape[1]
    nck = seq // CHUNK

    def body(u4_ref, ar_ref, ai_ref, bbr_ref, bbi_ref, crt_ref, cit_ref, vec512_ref, wg_ref,
             s4_ref, hcr_ref, hci_ref, hr, hi, pwr, pwi, car, cai, ubuf, obuf):
        i = pl.program_id(0)

        @pl.when(i == 0)
        def _():
            car[...] = jnp.zeros_like(car)
            cai[...] = jnp.zeros_like(cai)
            _power_table(ar_ref, ai_ref, pwr, pwi)

        _load_slabs(u4_ref, ubuf)
        u = ubuf[...]
        cin_r, cin_i = car[...], cai[...]
        hcr_ref[0] = jnp.broadcast_to(cin_r, (SEGS, N_CH))
        hci_ref[0] = jnp.broadcast_to(cin_i, (SEGS, N_CH))
        _, (er, ei) = _scan_states(u.astype(BF16), ar_ref, ai_ref, bbr_ref, bbi_ref, pwr, pwi, cin_r, cin_i, hr, hi)
        car[...] = er
        cai[...] = ei
        _, z, _, gate = _ssm_outputs(u, hr, hi, crt_ref, cit_ref, vec512_ref, wg_ref)
        obuf[...] = z * gate
        _store_slabs(obuf, s4_ref)

    return pl.pallas_call(
        body, name="ssm_fwd", grid=(nck,),
        in_specs=_ssm_specs(layer, nck, False),
        out_specs=[pl.BlockSpec((4, CHUNK, LANE), lambda i: (0, i, 0)),
                   pl.BlockSpec((1, SEGS, N_CH), lambda i: (i, 0, 0)),
                   pl.BlockSpec((1, SEGS, N_CH), lambda i: (i, 0, 0))],
        out_shape=[jax.ShapeDtypeStruct((4, seq, LANE), F32), jax.ShapeDtypeStruct((nck, SEGS, N_CH), F32),
                   jax.ShapeDtypeStruct((nck, SEGS, N_CH), F32)],
        scratch_shapes=[pltpu.VMEM((CHUNK, N_CH), F32), pltpu.VMEM((CHUNK, N_CH), F32),
                        pltpu.VMEM((STEPS, N_CH), F32), pltpu.VMEM((STEPS, N_CH), F32),
                        pltpu.VMEM((1, N_CH), F32), pltpu.VMEM((1, N_CH), F32),
                        pltpu.VMEM((CHUNK, SSM_W), F32), pltpu.VMEM((CHUNK, SSM_W), F32)],
        compiler_params=_params(("arbitrary",)),
    )(u4, a_r, a_i, bb_r, bb_i, c_rt, c_it, vec512, w_glu)


def _ssm_bwd(u4, ds4, hc_r, hc_i, a_r, a_i, bb_r, bb_i, c_rt, c_it, vec512, w_glu, layer):
    seq = u4.shape[1]
    nck = seq // CHUNK

    def body(u4_ref, ar_ref, ai_ref, bbr_ref, bbi_ref, crt_ref, cit_ref, vec512_ref, wg_ref, ds4_ref, hcr_ref, hci_ref,
             du4_ref, dbbr_ref, dbbi_ref, dcrt_ref, dcit_ref, dar_ref, dai_ref, dwg_ref, dvec_ref,
             hr, hi, gr, gi, pwr, pwi, gcr, gci, accr, acci, ubuf, dbuf):
        i = pl.program_id(0)

        @pl.when(i == 0)
        def _():
            for ref in (gcr, gci, accr, acci, dbbr_ref, dbbi_ref, dcrt_ref, dcit_ref, dwg_ref, dvec_ref):
                ref[...] = jnp.zeros_like(ref)
            _power_table(ar_ref, ai_ref, pwr, pwi)

        _load_slabs(u4_ref, ubuf)
        u = ubuf[...]
        ubf = u.astype(BF16)
        cin_r, cin_i = hcr_ref[0, 0:1, :], hci_ref[0, 0:1, :]
        (cm_r, cm_i), _ = _scan_states(ubf, ar_ref, ai_ref, bbr_ref, bbi_ref, pwr, pwi, cin_r, cin_i, hr, hi)
        y, z, t, gate = _ssm_outputs(u, hr, hi, crt_ref, cit_ref, vec512_ref, wg_ref)
        _load_slabs(ds4_ref, dbuf)
        ds = dbuf[...]
        da = ds * z * gate * (1.0 - gate)
        dab = da.astype(BF16)
        dz = ds * gate + _nt(dab, wg_ref[...])
        dwg_ref[...] += _tn(z.astype(BF16), dab)
        dvec_ref[H_BGLU:H_BGLU + 1, :] += _colsum(da)
        dy = dz * _gelu_grad(y, t)
        dvec_ref[H_DSKIP:H_DSKIP + 1, :] += _colsum(dy * u)
        du_skip = dy * vec512_ref[0, H_DSKIP:H_DSKIP + 1, :]
        dyb = dy.astype(BF16)
        for k in range(2):
            rows = slice(k * 256, (k + 1) * 256)
            cols = slice(k * HALF_CH, (k + 1) * HALF_CH)
            dcrt_ref[k] += _tn(hr[:, cols].astype(BF16), dyb[:, rows])
            dcit_ref[k] -= _tn(hi[:, cols].astype(BF16), dyb[:, rows])
            gr[:, cols] = _nt(dyb[:, rows], crt_ref[0, k])
            gi[:, cols] = -_nt(dyb[:, rows], cit_ref[0, k])
        for st in range(N_CH // STRIP):
            cs = slice(st * STRIP, (st + 1) * STRIP)
            arb = jnp.broadcast_to(ar_ref[0, :, cs], (SEGS, STRIP))
            aib = jnp.broadcast_to(ai_ref[0, :, cs], (SEGS, STRIP))

            def step(k, carry, cs=cs, arb=arb, aib=aib):
                cr, ci = carry
                rows = pl.ds(pl.multiple_of((STEPS - 1 - k) * SEGS, SEGS), SEGS)
                nr = gr[rows, cs] + (arb * cr + aib * ci)
                ni = gi[rows, cs] + (arb * ci - aib * cr)
                gr[rows, cs] = nr
                gi[rows, cs] = ni
                return nr, ni

            zero = jnp.zeros((SEGS, STRIP), F32)
            lax.fori_loop(0, STEPS, step, (zero, zero))
        first_r, first_i = gr[0:SEGS, :], gi[0:SEGS, :]
        a64r, a64i = pwr[STEPS - 1:STEPS, :], pwi[STEPS - 1:STEPS, :]
        dr_, di_ = gcr[...], gci[...]
        rows_r, rows_i = [None] * SEGS, [None] * SEGS
        for j in range(SEGS - 1, -1, -1):
            rows_r[j], rows_i[j] = dr_, di_
            dr_, di_ = (first_r[j:j + 1, :] + (a64r * dr_ + a64i * di_), first_i[j:j + 1, :] + (a64r * di_ - a64i * dr_))
        gcr[...] = dr_
        gci[...] = di_
        dm_r, dm_i = jnp.concatenate(rows_r, axis=0), jnp.concatenate(rows_i, axis=0)
        for st in range(N_CH // STRIP):
            cs = slice(st * STRIP, (st + 1) * STRIP)
            dmr, dmi = dm_r[:, cs], dm_i[:, cs]

            def fixed(s, cs=cs, dmr=dmr, dmi=dmi):
                rows = pl.ds(pl.multiple_of(s * SEGS, SEGS), SEGS)
                pr, pi = pwr[pl.ds(STEPS - 1 - s, 1), cs], pwi[pl.ds(STEPS - 1 - s, 1), cs]
                g_r = gr[rows, cs] + (pr * dmr + pi * dmi)
                g_i = gi[rows, cs] + (pr * dmi - pi * dmr)
                gr[rows, cs] = g_r
                gi[rows, cs] = g_i
                return g_r, g_i

            g_r, g_i = fixed(jnp.int32(0))
            acc0 = (g_r * cm_r[:, cs] + g_i * cm_i[:, cs], g_i * cm_r[:, cs] - g_r * cm_i[:, cs])

            def step(s, carry, cs=cs, fixed=fixed):
                sr, si = carry
                g_r, g_i = fixed(s)
                prev = pl.ds(pl.multiple_of((s - 1) * SEGS, SEGS), SEGS)
                hpr, hpi = hr[prev, cs], hi[prev, cs]
                return sr + (g_r * hpr + g_i * hpi), si + (g_i * hpr - g_r * hpi)

            sr, si = lax.fori_loop(1, STEPS, step, acc0)
            accr[:, cs] += sr
            acci[:, cs] += si
        grb, gib = gr[...].astype(BF16), gi[...].astype(BF16)
        dus = []
        for k in range(2):
            rows = slice(k * 256, (k + 1) * 256)
            cols = slice(k * HALF_CH, (k + 1) * HALF_CH)
            dus.append(_nt(grb[:, cols], bbr_ref[0, k]) + _nt(gib[:, cols], bbi_ref[0, k]))
            dbbr_ref[k] += _tn(ubf[:, rows], grb[:, cols])
            dbbi_ref[k] += _tn(ubf[:, rows], gib[:, cols])
        dbuf[...] = jnp.concatenate(dus, axis=1) + du_skip
        _store_slabs(dbuf, du4_ref)

        @pl.when(i == nck - 1)
        def _():
            dar_ref[...] = _colsum(accr[...])
            dai_ref[...] = _colsum(acci[...])

    rev4 = pl.BlockSpec((4, CHUNK, LANE), lambda i: (0, nck - 1 - i, 0))
    hc_spec = pl.BlockSpec((1, SEGS, N_CH), lambda i: (nck - 1 - i, 0, 0))
    fixed2 = lambda shape: pl.BlockSpec(shape, lambda i: (0,) * len(shape))
    return pl.pallas_call(
        body, name="ssm_bwd", grid=(nck,),
        in_specs=_ssm_specs(layer, nck, True) + [rev4, hc_spec, hc_spec],
        out_specs=[rev4, fixed2((2, 256, HALF_CH)), fixed2((2, 256, HALF_CH)), fixed2((2, HALF_CH, 256)),
                   fixed2((2, HALF_CH, 256)), fixed2((1, N_CH)), fixed2((1, N_CH)), fixed2((SSM_W, SSM_W)),
                   fixed2((8, SSM_W))],
        out_shape=[jax.ShapeDtypeStruct((4, seq, LANE), F32),
                   jax.ShapeDtypeStruct((2, 256, HALF_CH), F32), jax.ShapeDtypeStruct((2, 256, HALF_CH), F32),
                   jax.ShapeDtypeStruct((2, HALF_CH, 256), F32), jax.ShapeDtypeStruct((2, HALF_CH, 256), F32),
                   jax.ShapeDtypeStruct((1, N_CH), F32), jax.ShapeDtypeStruct((1, N_CH), F32),
                   jax.ShapeDtypeStruct((SSM_W, SSM_W), F32), jax.ShapeDtypeStruct((8, SSM_W), F32)],
        scratch_shapes=[pltpu.VMEM((CHUNK, N_CH), F32), pltpu.VMEM((CHUNK, N_CH), F32),
                        pltpu.VMEM((CHUNK, N_CH), F32), pltpu.VMEM((CHUNK, N_CH), F32),
                        pltpu.VMEM((STEPS, N_CH), F32), pltpu.VMEM((STEPS, N_CH), F32),
                        pltpu.VMEM((1, N_CH), F32), pltpu.VMEM((1, N_CH), F32),
                        pltpu.VMEM((SEGS, N_CH), F32), pltpu.VMEM((SEGS, N_CH), F32),
                        pltpu.VMEM((CHUNK, SSM_W), F32), pltpu.VMEM((CHUNK, SSM_W), F32)],
        compiler_params=_params(("arbitrary",)),
    )(u4, a_r, a_i, bb_r, bb_i, c_rt, c_it, vec512, w_glu, ds4, hc_r, hc_i)


def _block_diag(t):
    nl, _, ng, a, b = t.shape
    eye = jnp.eye(ng, dtype=t.dtype)
    return jnp.einsum("gh,lkgab->lkgahb", eye, t).reshape(nl, 2, ng * a, ng * b)


def _diag_blocks(t, a, b):
    nl = t.shape[0]
    ng = N_GROUPS // 2
    return jnp.einsum("lkgagb->lkgab", t.reshape(nl, 2, ng, a, ng, b))


def _local_step(x, loss_target, mod, p, comm):
    nl = mod.shape[0]
    pad1024 = jnp.zeros((nl, 16 - 10, D_MODEL), F32)
    vec = jnp.concatenate([mod.reshape(nl, N_MOD, D_MODEL), p["pre_mix_g"][:, None], p["post_mix_g"][:, None],
                           p["pre_mlp_g"][:, None], p["post_mlp_g"][:, None], pad1024], axis=1)
    vec512 = jnp.concatenate([p["attn_out_g"][:, None], p["ssm_out_g"][:, None], p["d_skip"][:, None],
                              p["b_glu"][:, None], jnp.zeros((nl, 4, SSM_W), F32)], axis=1)
    n_all = nl * N_CH
    lr = p["lam_re"].reshape(n_all, 1)
    li = p["lam_im"].reshape(n_all, 1)
    ldt = jnp.broadcast_to(p["log_dt"][:, :, None], (nl, N_GROUPS, STATE)).reshape(n_all, 1)
    br = p["b_re"].reshape(n_all, GROUP_W)
    bi = p["b_im"].reshape(n_all, GROUP_W)
    ab_r, ab_i, bb_r, bb_i = _ssm_prepare(lr, li, ldt, br, bi)
    a_r = ab_r.reshape(nl, 1, N_CH)
    a_i = ab_i.reshape(nl, 1, N_CH)

    def dense_b(bb):
        return _block_diag(bb.reshape(nl, 2, 16, STATE, GROUP_W).transpose(0, 1, 2, 4, 3)).astype(BF16)

    def dense_c(cc):
        return _block_diag(cc.reshape(nl, 2, 16, GROUP_W, STATE).transpose(0, 1, 2, 4, 3)).astype(BF16)

    bbr_d, bbi_d = dense_b(bb_r), dense_b(bb_i)
    crt_d, cit_d = dense_c(p["c_re"]), dense_c(p["c_im"])

    saved = []
    xl = x
    mixer_w, mlp_w = [None] * nl, [None] * nl
    for l in range(nl):
        mixer_w[l], tok = comm.mixer_weights(l, xl)
        w_in_t, w_glu, w_out = mixer_w[l]
        q, kv, u4, h1 = _in_proj_fwd(_after(xl, *tok), vec, w_in_t, l)
        attn = _attn_fwd(q, kv, p["attn_sinks"][l])
        s4, hc_r, hc_i = _ssm_fwd(u4, a_r, a_i, bbr_d, bbi_d, crt_d, cit_d, vec512, w_glu, l)
        x1 = _out_proj_fwd(xl, attn, s4, vec, vec512, w_out, l)
        mlp_w[l] = comm.mlp_weights(l, x1)
        x2 = _mlp_fwd(x1, vec, mlp_w[l][0], mlp_w[l][1], l)
        saved.append((xl, q, kv, u4, h1, attn, s4, hc_r, hc_i, x1))
        xl = x2

    dx, loss_parts = _loss_head(xl, loss_target)
    loss = jnp.sum(loss_parts[:, 0, 0])

    dvec_l, dvec512_l, dsink_l = [None] * nl, [None] * nl, [None] * nl
    dab_r, dab_i, dbb_r, dbb_i, dc_re, dc_im = ([None] * nl for _ in range(6))
    for l in range(nl - 1, -1, -1):
        xl, q, kv, u4, h1, attn, s4, hc_r, hc_i, x1 = saved[l]
        w_in_t, w_glu, w_out = mixer_w[l]
        dx1, h2, da, r, df, dvec_m = _mlp_bwd(dx, x1, vec, mlp_w[l][0], mlp_w[l][1], l)
        tok_a = comm.after_mlp_bwd(l, dx1)
        dw_mlp_out = _matmul_tn(_after(r, *tok_a), df, BF16, "dw_mlp_out").reshape(4, D_FF // 4, D_MODEL)
        dw_mlp_in = _matmul_tn(h2, da, BF16, "dw_mlp_in", pieces=4)
        tok_b = comm.mlp_grads(l, [dw_mlp_in, dw_mlp_out])
        dattn, ds4, heads, dmixed, dvec_o, dvec512_o = _out_proj_bwd(
            _after(dx1, *tok_b), attn, s4, vec, vec512, w_out, l)
        dw_out = _matmul_tn(heads, dmixed, BF16, "dw_out").reshape(4, D_MODEL // 4, D_MODEL)
        dq, dkv, dsk = _attn_bwd(q, kv, p["attn_sinks"][l], dattn)
        (du4, dbbr, dbbi, dcrt, dcit, dar, dai, dwg, dvec512_s) = _ssm_bwd(
            u4, ds4, hc_r, hc_i, a_r, a_i, bbr_d, bbi_d, crt_d, cit_d, vec512, w_glu, l)
        dw_glu = dwg.astype(BF16).reshape(4, SSM_W // 4, SSM_W)
        dx, dproj, dvec_i = _in_proj_bwd(dx1, dq, dkv, du4, xl, vec, w_in_t, l)
        tok_c = comm.after_in_proj_bwd(l, dx)
        dw_in = _matmul_tn(_after(h1, *tok_c), dproj, BF16, "dw_in")
        tok_d = comm.mixer_grads(l, [dw_in.reshape(D_MODEL, 4, IN_W // 4).transpose(1, 0, 2), dw_glu, dw_out])
        dx = _after(dx, *tok_d)
        dvec_l[l] = dvec_m + dvec_o + dvec_i
        dvec512_l[l] = dvec512_o + dvec512_s
        dsink_l[l] = jnp.sum(dsk[:, :, 0], axis=0)
        dab_r[l], dab_i[l], dbb_r[l], dbb_i[l], dc_re[l], dc_im[l] = dar, dai, dbbr, dbbi, dcrt, dcit

    dvec = jnp.stack(dvec_l)
    dvec512 = jnp.stack(dvec512_l)
    dbb_r_c = _diag_blocks(jnp.stack(dbb_r), GROUP_W, STATE).transpose(0, 1, 2, 4, 3).reshape(n_all, GROUP_W)
    dbb_i_c = _diag_blocks(jnp.stack(dbb_i), GROUP_W, STATE).transpose(0, 1, 2, 4, 3).reshape(n_all, GROUP_W)
    dlr, dli, dldt, dbr, dbi = _ssm_prepare_bwd(
        lr, li, ldt, br, bi, jnp.stack(dab_r).reshape(n_all, 1), jnp.stack(dab_i).reshape(n_all, 1), dbb_r_c, dbb_i_c)
    c_shape = (nl, N_GROUPS, GROUP_W, STATE)
    small = {
        "b_ada": dvec[:, :N_MOD].reshape(nl, N_MOD * D_MODEL),
        "pre_mix_g": dvec[:, V_PRE_MIX], "post_mix_g": dvec[:, V_POST_MIX],
        "pre_mlp_g": dvec[:, V_PRE_MLP], "post_mlp_g": dvec[:, V_POST_MLP],
        "attn_sinks": jnp.stack(dsink_l),
        "lam_re": dlr.reshape(nl, N_GROUPS, STATE), "lam_im": dli.reshape(nl, N_GROUPS, STATE),
        "log_dt": jnp.sum(dldt.reshape(nl, N_GROUPS, STATE), axis=-1),
        "b_re": dbr.reshape(nl, N_GROUPS, STATE, GROUP_W), "b_im": dbi.reshape(nl, N_GROUPS, STATE, GROUP_W),
        "c_re": _diag_blocks(jnp.stack(dc_re), STATE, GROUP_W).transpose(0, 1, 2, 4, 3).reshape(c_shape),
        "c_im": _diag_blocks(jnp.stack(dc_im), STATE, GROUP_W).transpose(0, 1, 2, 4, 3).reshape(c_shape),
        "d_skip": dvec512[:, H_DSKIP], "b_glu": dvec512[:, H_BGLU],
        "attn_out_g": dvec512[:, H_ATTN_G], "ssm_out_g": dvec512[:, H_SSM_G],
    }
    return loss, dx, small, small["b_ada"]


WEIGHTS = ["w_ada", "b_ada", "pre_mix_g", "w_in", "attn_sinks", "lam_re", "lam_im", "log_dt", "b_re", "b_im", "c_re",
           "c_im", "d_skip", "w_glu", "b_glu", "attn_out_g", "ssm_out_g", "w_out", "post_mix_g", "pre_mlp_g",
           "w_mlp_in", "w_mlp_out", "post_mlp_g"]
BIG = ["w_in", "w_glu", "w_out", "w_mlp_in", "w_mlp_out"]
SMALL = [n for n in WEIGHTS if n not in BIG and n != "w_ada"]
PACK_ROWS = 256


def _pack(parts):
    rows = []
    for n in SMALL:
        flat = parts[n].reshape(-1)
        pad = (-flat.shape[0]) % (PACK_ROWS * LANE)
        rows.append(jnp.pad(flat, (0, pad)).reshape(-1, LANE))
    return jnp.concatenate(rows, axis=0)


def _unpack(packed, shapes):
    out, r0 = {}, 0
    for n in SMALL:
        size = int(np.prod(shapes[n]))
        rows = -(-size // (PACK_ROWS * LANE)) * PACK_ROWS
        out[n] = packed[r0:r0 + rows].reshape(-1)[:size].reshape(shapes[n])
        r0 += rows
    return out


MIXER = ["w_in", "w_glu", "w_out"]
MLP = ["w_mlp_in", "w_mlp_out"]


class _Exchanges:
    def __init__(self, shards, wts, mom, var, chip):
        self.shards, self.wts, self.mom, self.var, self.chip = shards, wts, mom, var, chip
        self.chip_arr = jnp.reshape(chip, (1,)).astype(jnp.int32)
        self.nl = len(shards["w_in"])
        self.gathers, self.scatters, self.pairs = {}, {}, {}
        self.res = {n: None for n in BIG}

    def _start_gather(self, group, tag, l):
        srcs = [self.shards[n][l] for n in group]
        lands = [lax.dynamic_update_slice(lax.empty((4,) + s.shape, s.dtype), s[None], (self.chip, 0, 0)) for s in srcs]
        plan = _plan_gather(len(srcs))
        st = _exchange_start(f"gather_{tag}{l}_start", 3 * len(srcs), plan, srcs + lands)
        self.gathers[tag, l] = (plan, st)
        return st[3]

    def _wait_gather(self, tag, l, after):
        plan, st = self.gathers.pop((tag, l))
        n = len(st[2]) // 2
        bufs = _exchange_wait(f"gather_{tag}{l}_wait", 3 * n, plan, st, after)
        return [b.reshape(4 * b.shape[1], b.shape[2]) for b in bufs[n:]]

    def begin(self):
        return [self._start_gather(MIXER, "mixer", 0), self._start_gather(MLP, "mlp", 0)]

    def mixer_weights(self, l, after):
        w = self._wait_gather("mixer", l, after)
        toks = []
        if l + 1 < self.nl:
            toks = [self._start_gather(MIXER, "mixer", l + 1), self._start_gather(MLP, "mlp", l + 1)]
        return w, toks

    def mlp_weights(self, l, after):
        return self._wait_gather("mlp", l, after)

    def _start_scatter(self, tag, l, group, pieces):
        plan = _plan_scatter(len(pieces))
        st = _exchange_start(f"scatter_{tag}{l}_start", 3 * len(pieces), plan,
                             list(pieces) + [lax.empty(p.shape, p.dtype) for p in pieces])
        self.scatters[tag] = (l, group, plan, st)
        return [st[3]]

    def _finish_scatter(self, tag, after):
        l, group, plan, st = self.scatters.pop(tag)
        n = len(group)
        bufs = _exchange_wait(f"scatter_{tag}{l}_wait", 3 * n, plan, st, after)
        sums = [_sum_pieces(bufs[k], bufs[n + k], self.chip_arr, "sum_" + group[k]) for k in range(n)]
        plan2 = _plan_pair(n)
        st2 = _exchange_start(f"pair_{tag}{l}_start", n, plan2, sums + [lax.empty(s.shape, s.dtype) for s in sums])
        self.pairs[tag] = (l, group, plan2, st2)
        return [st2[3]]

    def _finish_pair(self, tag, after):
        l, group, plan, st = self.pairs.pop(tag)
        n = len(group)
        bufs = _exchange_wait(f"pair_{tag}{l}_wait", n, plan, st, after)
        for k, name in enumerate(group):
            self.res[name] = _adamw_layer([bufs[k], bufs[n + k]], self.wts[name], self.mom[name], self.var[name],
                                          l, self.res[name], "adamw_" + name)

    def after_mlp_bwd(self, l, after):
        toks = self._finish_scatter("mixer", after) if "mixer" in self.scatters else []
        if "mlp" in self.pairs:
            self._finish_pair("mlp", after)
        return toks

    def mlp_grads(self, l, pieces):
        return self._start_scatter("mlp", l, MLP, pieces)

    def after_in_proj_bwd(self, l, after):
        toks = self._finish_scatter("mlp", after)
        if "mixer" in self.pairs:
            self._finish_pair("mixer", after)
        return toks

    def mixer_grads(self, l, pieces):
        return self._start_scatter("mixer", l, MIXER, pieces)

    def finish_mixer_scatter(self, after):
        return self._finish_scatter("mixer", after)

    def finish_mlp(self, after):
        self._finish_pair("mlp", after)

    def finish_mixer(self, after):
        self._finish_pair("mixer", after)

    def results(self):
        return self.res


def kernel(x, c, w_ada, b_ada, pre_mix_g, w_in, attn_sinks, lam_re, lam_im, log_dt, b_re, b_im, c_re, c_im, d_skip, w_glu, b_glu, attn_out_g, ssm_out_g, w_out, post_mix_g, pre_mlp_g, w_mlp_in, w_mlp_out, post_mlp_g, loss_target, m_w_ada, m_b_ada, m_pre_mix_g, m_w_in, m_attn_sinks, m_lam_re, m_lam_im, m_log_dt, m_b_re, m_b_im, m_c_re, m_c_im, m_d_skip, m_w_glu, m_b_glu, m_attn_out_g, m_ssm_out_g, m_w_out, m_post_mix_g, m_pre_mlp_g, m_w_mlp_in, m_w_mlp_out, m_post_mlp_g, v_w_ada, v_b_ada, v_pre_mix_g, v_w_in, v_attn_sinks, v_lam_re, v_lam_im, v_log_dt, v_b_re, v_b_im, v_c_re, v_c_im, v_d_skip, v_w_glu, v_b_glu, v_attn_out_g, v_ssm_out_g, v_w_out, v_post_mix_g, v_pre_mlp_g, v_w_mlp_in, v_w_mlp_out, v_post_mlp_g):
    args = locals()
    wts = {n: args[n] for n in WEIGHTS}
    mom = {n: args["m_" + n] for n in WEIGHTS}
    var = {n: args["v_" + n] for n in WEIGHTS}
    nl = w_in.shape[0]
    ix, iy, ic = lax.axis_index("x"), lax.axis_index("y"), lax.axis_index("c")
    chip = 2 * ix + iy
    me = 4 * ix + 2 * iy + ic
    wcols = w_ada.shape[2]

    shards = {"w_in": [w_in[l].astype(BF16).T for l in range(nl)], "w_glu": [w_glu[l].astype(BF16) for l in range(nl)],
              "w_out": [w_out[l].astype(BF16) for l in range(nl)],
              "w_mlp_in": [w_mlp_in[l].astype(BF16).T for l in range(nl)],
              "w_mlp_out": [w_mlp_out[l].astype(BF16) for l in range(nl)]}
    comm = _Exchanges(shards, wts, mom, var, chip)
    c = _after(c, *comm.begin())

    c_all = _gather([c.reshape(1, 1, 1, D_MODEL)], "all", "gather_c")[0].reshape(8, D_MODEL)
    b_sh = lax.dynamic_slice(b_ada, (0, chip * wcols), (nl, wcols)).reshape(nl, 1, wcols)
    mod_sh = _ada_forward(c_all, w_ada, b_sh)
    mod_all = _gather([mod_sh.reshape(1, 1, nl * 8, wcols)], "chips", "gather_mod")[0].reshape(4, nl, 8, wcols)
    mod = lax.dynamic_index_in_dim(mod_all, me, axis=2, keepdims=False)
    mod = mod.transpose(1, 0, 2).reshape(nl, 4 * wcols)

    small_p = {n: wts[n] for n in SMALL}
    loss, grad_x, small, dmod = _local_step(x[0], loss_target[0], mod, small_p, comm)
    loss = lax.psum(loss, ("x", "y", "c"))

    packed = _pack(small)
    rows = packed.shape[0]
    pair_plan = _plan_pair(1)
    pair_small = _exchange_start("pair_small_start", 1, pair_plan, [packed, lax.empty((rows, LANE), F32)])
    grad_x = _after(grad_x, pair_small[3])
    grad_x = _after(grad_x, *comm.finish_mixer_scatter(grad_x))
    comm.finish_mlp(grad_x)

    dmod_all = _gather([dmod.reshape(1, 1, nl, N_MOD * D_MODEL)], "all", "gather_dmod")[0][0]
    dmod_sh = lax.dynamic_slice(dmod_all, (0, 0, chip * wcols), (8, nl, wcols)).transpose(1, 0, 2)
    g_ada = _ada_weight_grad(c_all.T, dmod_sh)
    res = {"w_ada": _adamw(g_ada[:, None], w_ada, m_w_ada, v_w_ada, "adamw_w_ada")}

    own, other = _exchange_wait("pair_small_wait", 1, pair_plan, pair_small, res["w_ada"][0])
    chip_sum = _sum_list([own, other], "sum_pair_small")
    quad_plan = _plan_gather(1)
    quad0 = lax.dynamic_update_slice(lax.empty((4, rows, LANE), F32), chip_sum[None], (chip, 0, 0))
    quad_small = _exchange_start("gather_small_start", 3, quad_plan, [chip_sum, quad0])
    comm.finish_mixer(_after(res["w_ada"][1], quad_small[3]))
    res.update(comm.results())
    quad = _exchange_wait("gather_small_wait", 3, quad_plan, quad_small, res["w_in"][0])[1]
    outs = _adamw(quad[None], _pack({n: wts[n] for n in SMALL})[None], _pack({n: mom[n] for n in SMALL})[None],
                  _pack({n: var[n] for n in SMALL})[None], "adamw_small")
    shapes = {n: wts[n].shape for n in SMALL}
    unpacked = [_unpack(o[0], shapes) for o in outs]
    for n in SMALL:
        res[n] = [u[n] for u in unpacked]

    return (loss, grad_x[None], *[res[n][0] for n in WEIGHTS], *[res[n][1] for n in WEIGHTS],
            *[res[n][2] for n in WEIGHTS], *[res[n][3] for n in WEIGHTS])
```

```python
import functools
import math

import numpy as np
import jax
import jax.numpy as jnp
from jax import lax
from jax.experimental import pallas as pl
from jax.experimental.pallas import tpu as pltpu

F32 = jnp.float32
BF16 = jnp.bfloat16

D_MODEL = 1024
ATTN_W = 512
SSM_W = 512
HEAD_DIM = 64
N_Q = 8
N_KV = 2
Q_PER_KV = 4
KV_W = 128
WINDOW = 128
BLOCK = 128
N_GROUPS = 32
GROUP_W = 16
STATE = 64
N_CH = N_GROUPS * STATE
HALF_CH = N_CH // 2
D_FF = 4096
IN_W = 1280
N_MOD = 6
EPS = 1e-6
NEG_INF = -1e30

ADAM_LR = 0.001
ADAM_B1 = 0.9
ADAM_B2 = 0.999
ADAM_EPS = 1e-08
ADAM_WD = 0.01
ADAM_STEP = 10

ROW_TILE = 256
CHUNK = 256
SEGS = 8
STEPS = CHUNK // SEGS
STRIP = 1024
VMEM_LIMIT_V7X = 56 * 1024 * 1024
LANE = 128
SUBLANE = 8

GELU_K0 = math.sqrt(2.0 / math.pi)
GELU_K1 = 0.044715

V_SH1, V_SC1, V_G1, V_SH2, V_SC2, V_G2, V_PRE_MIX, V_POST_MIX, V_PRE_MLP, V_POST_MLP = range(10)
H_ATTN_G, H_SSM_G, H_DSKIP, H_BGLU = range(4)

HBM = pl.BlockSpec(memory_space=pltpu.HBM)
SEM = pl.BlockSpec(memory_space=pltpu.SEMAPHORE)
EFFECT = pltpu.SideEffectType.DATAFLOW_SIDE_EFFECTING
MESH_ID = pl.DeviceIdType.MESH


def _nn(a, b):
    return lax.dot_general(a, b, (((1,), (0,)), ((), ())), preferred_element_type=F32)


def _nt(a, b):
    return lax.dot_general(a, b, (((1,), (1,)), ((), ())), preferred_element_type=F32)


def _tn(a, b):
    return lax.dot_general(a, b, (((0,), (0,)), ((), ())), preferred_element_type=F32)


def _params(sem):
    return pltpu.CompilerParams(dimension_semantics=sem, vmem_limit_bytes=VMEM_LIMIT_V7X)


def _rms_fwd(x, g):
    r = lax.rsqrt(jnp.mean(x * x, axis=-1, keepdims=True) + EPS)
    xh = x * r
    return xh * g, xh, r


def _rms_bwd(dy, xh, r, g):
    dxh = dy * g
    dx = r * (dxh - xh * jnp.mean(dxh * xh, axis=-1, keepdims=True))
    return dx, dy * xh


def _colsum(t):
    return jnp.sum(t, axis=0, keepdims=True)


def _gelu(y):
    t = jnp.tanh(GELU_K0 * (y + GELU_K1 * (y * y * y)))
    return 0.5 * y * (1.0 + t), t


def _gelu_grad(y, t):
    return 0.5 * (1.0 + t) + 0.5 * y * (1.0 - t * t) * GELU_K0 * (1.0 + 3.0 * GELU_K1 * y * y)


def _alibi_slopes():
    return [float(s) for s in 2.0 ** (-8.0 * np.arange(1, N_Q + 1) / N_Q)]


def _pick_rows(rows, bytes_per_row, budget):
    t = rows
    while t % (2 * SUBLANE) == 0 and t * bytes_per_row > budget:
        t //= 2
    return t


def _load_once(step, pairs, sems):
    @pl.when(step == 0)
    def _():
        cps = [pltpu.make_async_copy(src, dst, sems.at[k]) for k, (src, dst) in enumerate(pairs)]
        for cp in cps:
            cp.start()
        for cp in cps:
            cp.wait()


_GROUPS = {
    "all": ([(0, 0, 1), (0, 1, 0), (0, 1, 1), (1, 0, 0), (1, 0, 1), (1, 1, 0), (1, 1, 1)], (4, 2, 1), 8),
    "chips": ([(1, 0, 0), (0, 1, 0), (1, 1, 0)], (2, 1, 0), 4),
    "pair": ([(0, 0, 1)], (0, 0, 1), 2),
}


def _flip(v, f):
    return 1 - v if f else v


def _gather(arrs, kind, name):
    masks, wts, n = _GROUPS[kind]
    na, nm = len(arrs), len(masks)

    def body(*refs):
        ins, outs = refs[:na], refs[na:2 * na]
        ssem, rsem, lsem = refs[2 * na:]
        x, y, c = lax.axis_index("x"), lax.axis_index("y"), lax.axis_index("c")
        me = wts[0] * x + wts[1] * y + wts[2] * c
        local = [pltpu.make_async_copy(ins[k], outs[k].at[:, pl.ds(me, 1)], lsem.at[k]) for k in range(na)]
        for cp in local:
            cp.start()
        remote = []
        for k in range(na):
            for mi, (fx, fy, fc) in enumerate(masks):
                peer = (_flip(x, fx), _flip(y, fy), _flip(c, fc))
                remote.append(pltpu.make_async_remote_copy(
                    src_ref=ins[k], dst_ref=outs[k].at[:, pl.ds(me, 1)],
                    send_sem=ssem.at[k * nm + mi], recv_sem=rsem.at[k * nm + mi],
                    device_id=peer, device_id_type=MESH_ID))
        for cp in remote:
            cp.start()
        for cp in remote:
            cp.wait()
        for cp in local:
            cp.wait()

    outs = pl.pallas_call(
        body, name=name,
        out_shape=[jax.ShapeDtypeStruct((a.shape[0], n) + a.shape[2:], a.dtype) for a in arrs],
        in_specs=[HBM] * na, out_specs=[HBM] * na,
        scratch_shapes=[pltpu.SemaphoreType.DMA((na * nm,)), pltpu.SemaphoreType.DMA((na * nm,)),
                        pltpu.SemaphoreType.DMA((na,))],
    )(*arrs)
    return list(outs)


def _hbm(a):
    return pltpu.with_memory_space_constraint(a, pltpu.HBM)


def _after(x, *tokens):
    for t in tokens:
        x = x + t[0, 0].astype(x.dtype)
    return x


def _exchange_start(name, n_copies, plan, bufs, after=()):
    n, na = len(bufs), len(after)

    def body(*refs):
        ssem, rsem, token = refs[n + na], refs[n + na + 1], refs[2 * n + na + 2]
        for k, (src, dst, dev) in enumerate(plan(refs[:n])):
            pltpu.make_async_remote_copy(src_ref=src, dst_ref=dst, send_sem=ssem.at[k], recv_sem=rsem.at[k],
                                         device_id=dev, device_id_type=MESH_ID).start()
        token[...] = jnp.zeros_like(token)

    outs = pl.pallas_call(
        body, name=name,
        out_shape=(pltpu.SemaphoreType.DMA((n_copies,)), pltpu.SemaphoreType.DMA((n_copies,)),
                   *[pltpu.HBM(b.shape, b.dtype) for b in bufs], jax.ShapeDtypeStruct((SUBLANE, LANE), F32)),
        in_specs=[HBM] * n + [pl.BlockSpec(memory_space=pl.ANY)] * na,
        out_specs=(SEM, SEM, *[HBM] * n, pl.BlockSpec(memory_space=pltpu.VMEM)),
        input_output_aliases={i: 2 + i for i in range(n)},
        compiler_params=pltpu.CompilerParams(has_side_effects=EFFECT),
    )(*[_hbm(b) for b in bufs], *after)
    return outs[0], outs[1], list(outs[2:2 + n]), outs[2 + n]


def _exchange_wait(name, n_copies, plan, started, after):
    ssem, rsem, bufs, _ = started
    n = len(bufs)
    after = list(after) if isinstance(after, (list, tuple)) else [after]

    def body(*refs):
        ssem_ref, rsem_ref = refs[n], refs[n + 1]
        for k, (src, dst, dev) in enumerate(plan(refs[:n])):
            cp = pltpu.make_async_remote_copy(src_ref=src, dst_ref=dst, send_sem=ssem_ref.at[k], recv_sem=rsem_ref.at[k],
                                              device_id=dev, device_id_type=MESH_ID)
            cp.wait_send()
            cp.wait_recv()

    outs = pl.pallas_call(
        body, name=name,
        out_shape=tuple(pltpu.HBM(b.shape, b.dtype) for b in bufs),
        in_specs=[HBM] * n + [SEM, SEM] + [pl.BlockSpec(memory_space=pl.ANY)] * len(after), out_specs=tuple([HBM] * n),
        input_output_aliases={i: i for i in range(n)},
        compiler_params=pltpu.CompilerParams(has_side_effects=EFFECT),
    )(*bufs, ssem, rsem, *after)
    return list(outs)


def _position():
    x, y, c = lax.axis_index("x"), lax.axis_index("y"), lax.axis_index("c")
    return x, y, c, [(1 - x, y), (x, 1 - y), (1 - x, 1 - y)]


def _plan_gather(na):
    def plan(refs):
        x, y, c, chips = _position()
        return [(refs[k], refs[na + k].at[2 * x + y], (px, py, c)) for k in range(na) for px, py in chips]
    return plan


def _plan_scatter(na):
    def plan(refs):
        x, y, c, chips = _position()
        return [(refs[k].at[2 * px + py], refs[na + k].at[2 * x + y], (px, py, c))
                for k in range(na) for px, py in chips]
    return plan


def _plan_pair(na):
    def plan(refs):
        x, y, c, _ = _position()
        return [(refs[k], refs[na + k], (x, y, 1 - c)) for k in range(na)]
    return plan


def _sum_list(arrs, name):
    n = len(arrs)
    r, c = arrs[0].shape
    tr = _pick_rows(r, c * 4 * (n + 1), 4 << 20)

    def body(*refs):
        acc = refs[0][...].astype(F32)
        for j in range(1, n):
            acc = acc + refs[j][...].astype(F32)
        refs[n][...] = acc

    blk = pl.BlockSpec((tr, c), lambda i: (i, 0))
    return pl.pallas_call(
        body, name=name, grid=(r // tr,), in_specs=[blk] * n, out_specs=blk,
        out_shape=jax.ShapeDtypeStruct((r, c), F32), compiler_params=_params(("parallel",)),
    )(*arrs)


def _sum_pieces(own, recv, chip, name):
    _, r, c = own.shape
    tr = _pick_rows(r, c * 2 * 6, 4 << 20)

    def body(chip_ref, own_ref, recv_ref, o_ref):
        acc = own_ref[0].astype(F32)
        for j in range(4):
            acc = acc + jnp.where(chip_ref[0] == j, 0.0, recv_ref[j].astype(F32))
        o_ref[...] = acc.astype(BF16)

    return pl.pallas_call(
        body, name=name,
        grid_spec=pltpu.PrefetchScalarGridSpec(
            num_scalar_prefetch=1, grid=(r // tr,),
            in_specs=[pl.BlockSpec((1, tr, c), lambda i, chip_ref: (chip_ref[0], i, 0)),
                      pl.BlockSpec((4, tr, c), lambda i, chip_ref: (0, i, 0))],
            out_specs=pl.BlockSpec((tr, c), lambda i, chip_ref: (i, 0))),
        out_shape=jax.ShapeDtypeStruct((r, c), BF16), compiler_params=_params(("parallel",)),
    )(chip, own, recv)


def _adam_update(g, w, m, v):
    mn = ADAM_B1 * m + (1.0 - ADAM_B1) * g
    vn = ADAM_B2 * v + (1.0 - ADAM_B2) * jnp.square(g)
    m_hat = mn / (1.0 - ADAM_B1 ** ADAM_STEP)
    v_hat = vn / (1.0 - ADAM_B2 ** ADAM_STEP)
    return -ADAM_LR * (m_hat / (jnp.sqrt(v_hat) + ADAM_EPS) + ADAM_WD * w), mn, vn


def _adamw_layer(grads, w, m, v, layer, prev, name):
    ng = len(grads)
    nl, r, c = w.shape
    tr = _pick_rows(r, c * 4 * (ng + 7), 6 << 20)
    if prev is None:
        prev = [lax.empty((nl, r, c), F32) for _ in range(4)]

    def body(*refs):
        g = refs[0][...].astype(F32)
        for j in range(1, ng):
            g = g + refs[j][...].astype(F32)
        w_ref, m_ref, v_ref = refs[ng:ng + 3]
        go_ref, d_ref, mo_ref, vo_ref = refs[ng + 7:ng + 11]
        d, mn, vn = _adam_update(g, w_ref[0], m_ref[0], v_ref[0])
        go_ref[0] = g
        d_ref[0] = d
        mo_ref[0] = mn
        vo_ref[0] = vn

    gblk = pl.BlockSpec((tr, c), lambda i: (i, 0))
    blk = pl.BlockSpec((1, tr, c), lambda i: (layer, i, 0))
    keep = pl.BlockSpec(memory_space=pl.ANY)
    sds = jax.ShapeDtypeStruct((nl, r, c), F32)
    return pl.pallas_call(
        body, name=name, grid=(r // tr,),
        in_specs=[gblk] * ng + [blk] * 3 + [keep] * 4,
        out_specs=[blk] * 4, out_shape=[sds] * 4,
        input_output_aliases={ng + 3 + i: i for i in range(4)},
        compiler_params=_params(("parallel",)),
    )(*grads, w, m, v, *prev)


def _adamw(gs, w, m, v, name):
    a, s, r, c = gs.shape
    tr = _pick_rows(r, c * 4 * (s + 7), 6 << 20)

    def body(g_ref, w_ref, m_ref, v_ref, go_ref, d_ref, mo_ref, vo_ref):
        g = g_ref[0, 0].astype(F32)
        for j in range(1, s):
            g = g + g_ref[0, j].astype(F32)
        d, mn, vn = _adam_update(g, w_ref[0], m_ref[0], v_ref[0])
        go_ref[0] = g
        d_ref[0] = d
        mo_ref[0] = mn
        vo_ref[0] = vn

    blk = pl.BlockSpec((1, tr, c), lambda i, j: (i, j, 0))
    sds = jax.ShapeDtypeStruct((a, r, c), F32)
    return pl.pallas_call(
        body, name=name, grid=(a, r // tr),
        in_specs=[pl.BlockSpec((1, s, tr, c), lambda i, j: (i, 0, j, 0)), blk, blk, blk],
        out_specs=[blk, blk, blk, blk], out_shape=[sds, sds, sds, sds],
        compiler_params=_params(("parallel", "parallel")),
    )(gs, w, m, v)


def _ada_forward(c_all, w_ada, b_sh):
    nl, d, w = w_ada.shape
    tw = 512

    def body(c_ref, w_ref, b_ref, o_ref):
        cv = c_ref[...]
        act = (cv * jax.nn.sigmoid(cv)).astype(BF16)
        o_ref[0] = _nn(act, w_ref[0].astype(BF16)) + b_ref[0]

    return pl.pallas_call(
        body, name="ada_forward", grid=(nl, w // tw),
        in_specs=[pl.BlockSpec((8, d), lambda l, j: (0, 0)),
                  pl.BlockSpec((1, d, tw), lambda l, j: (l, 0, j)),
                  pl.BlockSpec((1, 1, tw), lambda l, j: (l, 0, j))],
        out_specs=pl.BlockSpec((1, 8, tw), lambda l, j: (l, 0, j)),
        out_shape=jax.ShapeDtypeStruct((nl, 8, w), F32),
        compiler_params=_params(("parallel", "parallel")),
    )(c_all, w_ada, b_sh)


def _ada_weight_grad(c_all_t, dmod):
    nl, nb, w = dmod.shape
    d = c_all_t.shape[0]
    tw = 512

    def body(c_ref, g_ref, o_ref):
        cv = c_ref[...]
        act = cv * jax.nn.sigmoid(cv)
        gv = g_ref[0]
        acc = act[:, 0:1] * gv[0:1, :]
        for b in range(1, nb):
            acc = acc + act[:, b:b + 1] * gv[b:b + 1, :]
        o_ref[0] = acc

    return pl.pallas_call(
        body, name="ada_weight_grad", grid=(nl, w // tw),
        in_specs=[pl.BlockSpec((d, nb), lambda l, j: (0, 0)),
                  pl.BlockSpec((1, nb, tw), lambda l, j: (l, 0, j))],
        out_specs=pl.BlockSpec((1, d, tw), lambda l, j: (l, 0, j)),
        out_shape=jax.ShapeDtypeStruct((nl, d, w), F32),
        compiler_params=_params(("parallel", "parallel")),
    )(c_all_t, dmod)


def _in_proj_fwd(x, vec, w_in_t, layer):
    seq = x.shape[0]
    tm = ROW_TILE

    def body(x_ref, vec_ref, w_ref, q_ref, kv_ref, u4_ref, h_ref):
        n, _, _ = _rms_fwd(x_ref[...], vec_ref[0, V_PRE_MIX:V_PRE_MIX + 1, :])
        h = (n * (1.0 + vec_ref[0, V_SC1:V_SC1 + 1, :]) + vec_ref[0, V_SH1:V_SH1 + 1, :]).astype(BF16)
        h_ref[...] = h
        proj = _nt(h, w_ref[...])
        q_ref[...] = proj[:, :ATTN_W].astype(BF16)
        kv_ref[...] = proj[:, ATTN_W:ATTN_W + 2 * KV_W].astype(BF16)
        u0 = ATTN_W + 2 * KV_W
        for j in range(4):
            u4_ref[j] = proj[:, u0 + j * LANE:u0 + (j + 1) * LANE]

    return pl.pallas_call(
        body, name="in_proj_fwd", grid=(seq // tm,),
        in_specs=[pl.BlockSpec((tm, D_MODEL), lambda i: (i, 0)),
                  pl.BlockSpec((1, 16, D_MODEL), lambda i: (layer, 0, 0)),
                  pl.BlockSpec((IN_W, D_MODEL), lambda i: (0, 0))],
        out_specs=[pl.BlockSpec((tm, ATTN_W), lambda i: (i, 0)),
                   pl.BlockSpec((tm, 2 * KV_W), lambda i: (i, 0)),
                   pl.BlockSpec((4, tm, LANE), lambda i: (0, i, 0)),
                   pl.BlockSpec((tm, D_MODEL), lambda i: (i, 0))],
        out_shape=[jax.ShapeDtypeStruct((seq, ATTN_W), BF16), jax.ShapeDtypeStruct((seq, 2 * KV_W), BF16),
                   jax.ShapeDtypeStruct((4, seq, LANE), F32), jax.ShapeDtypeStruct((seq, D_MODEL), BF16)],
        compiler_params=_params(("parallel",)),
    )(x, vec, w_in_t)


def _in_proj_bwd(dx1, dq, dkv, du4, x, vec, w_in_t, layer):
    seq = x.shape[0]
    tm = ROW_TILE

    def body(dx1_ref, dq_ref, dkv_ref, du4_ref, x_ref, vec_ref, w_ref, dx_ref, dp_ref, dvec_ref):
        i = pl.program_id(0)

        @pl.when(i == 0)
        def _():
            dvec_ref[...] = jnp.zeros_like(dvec_ref)

        dproj = jnp.concatenate([dq_ref[...], dkv_ref[...]] + [du4_ref[j] for j in range(4)], axis=1).astype(BF16)
        dp_ref[...] = dproj
        dh = _nn(dproj, w_ref[...])
        g = vec_ref[0, V_PRE_MIX:V_PRE_MIX + 1, :]
        n, xh, r = _rms_fwd(x_ref[...], g)
        dn = dh * (1.0 + vec_ref[0, V_SC1:V_SC1 + 1, :])
        dxn, dg_rows = _rms_bwd(dn, xh, r, g)
        dx_ref[...] = dx1_ref[...] + dxn
        dvec_ref[V_SH1:V_SH1 + 1, :] += _colsum(dh)
        dvec_ref[V_SC1:V_SC1 + 1, :] += _colsum(dh * n)
        dvec_ref[V_PRE_MIX:V_PRE_MIX + 1, :] += _colsum(dg_rows)

    row = pl.BlockSpec((tm, D_MODEL), lambda i: (i, 0))
    return pl.pallas_call(
        body, name="in_proj_bwd", grid=(seq // tm,),
        in_specs=[row, pl.BlockSpec((tm, ATTN_W), lambda i: (i, 0)), pl.BlockSpec((tm, 2 * KV_W), lambda i: (i, 0)),
                  pl.BlockSpec((4, tm, LANE), lambda i: (0, i, 0)), row,
                  pl.BlockSpec((1, 16, D_MODEL), lambda i: (layer, 0, 0)),
                  pl.BlockSpec((IN_W, D_MODEL), lambda i: (0, 0))],
        out_specs=[row, pl.BlockSpec((tm, IN_W), lambda i: (i, 0)), pl.BlockSpec((16, D_MODEL), lambda i: (0, 0))],
        out_shape=[jax.ShapeDtypeStruct((seq, D_MODEL), F32), jax.ShapeDtypeStruct((seq, IN_W), BF16),
                   jax.ShapeDtypeStruct((16, D_MODEL), F32)],
        compiler_params=_params(("arbitrary",)),
    )(dx1, dq, dkv, du4, x, vec, w_in_t)


def _heads(attn_ref, s4_ref, vec512_ref):
    ga = vec512_ref[0, H_ATTN_G:H_ATTN_G + 1, :]
    gs = vec512_ref[0, H_SSM_G:H_SSM_G + 1, :]
    sv = jnp.concatenate([s4_ref[j] for j in range(4)], axis=1)
    na, ah, ar = _rms_fwd(attn_ref[...], ga)
    ns, sh, sr = _rms_fwd(sv, gs)
    return jnp.concatenate([na, ns], axis=1), (ah, ar, ga), (sh, sr, gs)


def _out_proj_fwd(x, attn, s4, vec, vec512, w_out, layer):
    seq = x.shape[0]
    tm = ROW_TILE

    def body(x_ref, attn_ref, s4_ref, vec_ref, vec512_ref, w_ref, x1_ref):
        heads, _, _ = _heads(attn_ref, s4_ref, vec512_ref)
        mixed = _nn(heads.astype(BF16), w_ref[...])
        nm, _, _ = _rms_fwd(mixed, vec_ref[0, V_POST_MIX:V_POST_MIX + 1, :])
        x1_ref[...] = x_ref[...] + vec_ref[0, V_G1:V_G1 + 1, :] * nm

    row = pl.BlockSpec((tm, D_MODEL), lambda i: (i, 0))
    return pl.pallas_call(
        body, name="out_proj_fwd", grid=(seq // tm,),
        in_specs=[row, pl.BlockSpec((tm, ATTN_W), lambda i: (i, 0)), pl.BlockSpec((4, tm, LANE), lambda i: (0, i, 0)),
                  pl.BlockSpec((1, 16, D_MODEL), lambda i: (layer, 0, 0)),
                  pl.BlockSpec((1, 8, SSM_W), lambda i: (layer, 0, 0)),
                  pl.BlockSpec((D_MODEL, D_MODEL), lambda i: (0, 0))],
        out_specs=row, out_shape=jax.ShapeDtypeStruct((seq, D_MODEL), F32),
        compiler_params=_params(("parallel",)),
    )(x, attn, s4, vec, vec512, w_out)


def _out_proj_bwd(dx1, attn, s4, vec, vec512, w_out, layer):
    seq = dx1.shape[0]
    tm = ROW_TILE

    def body(dx1_ref, attn_ref, s4_ref, vec_ref, vec512_ref, w_ref,
             dattn_ref, ds4_ref, heads_ref, dmixed_ref, dvec_ref, dvec512_ref):
        i = pl.program_id(0)

        @pl.when(i == 0)
        def _():
            dvec_ref[...] = jnp.zeros_like(dvec_ref)
            dvec512_ref[...] = jnp.zeros_like(dvec512_ref)

        heads, (ah, ar, ga), (sh, sr, gs) = _heads(attn_ref, s4_ref, vec512_ref)
        hb = heads.astype(BF16)
        heads_ref[...] = hb
        gm = vec_ref[0, V_POST_MIX:V_POST_MIX + 1, :]
        nm, mh, mr = _rms_fwd(_nn(hb, w_ref[...]), gm)
        dx1v = dx1_ref[...]
        dvec_ref[V_G1:V_G1 + 1, :] += _colsum(dx1v * nm)
        dmixed, dgm_rows = _rms_bwd(dx1v * vec_ref[0, V_G1:V_G1 + 1, :], mh, mr, gm)
        dvec_ref[V_POST_MIX:V_POST_MIX + 1, :] += _colsum(dgm_rows)
        dmb = dmixed.astype(BF16)
        dmixed_ref[...] = dmb
        dheads = _nt(dmb, w_ref[...])
        dattn, dga_rows = _rms_bwd(dheads[:, :ATTN_W], ah, ar, ga)
        ds, dgs_rows = _rms_bwd(dheads[:, ATTN_W:], sh, sr, gs)
        dattn_ref[...] = dattn
        for j in range(4):
            ds4_ref[j] = ds[:, j * LANE:(j + 1) * LANE]
        dvec512_ref[H_ATTN_G:H_ATTN_G + 1, :] += _colsum(dga_rows)
        dvec512_ref[H_SSM_G:H_SSM_G + 1, :] += _colsum(dgs_rows)

    row = pl.BlockSpec((tm, D_MODEL), lambda i: (i, 0))
    return pl.pallas_call(
        body, name="out_proj_bwd", grid=(seq // tm,),
        in_specs=[row, pl.BlockSpec((tm, ATTN_W), lambda i: (i, 0)), pl.BlockSpec((4, tm, LANE), lambda i: (0, i, 0)),
                  pl.BlockSpec((1, 16, D_MODEL), lambda i: (layer, 0, 0)),
                  pl.BlockSpec((1, 8, SSM_W), lambda i: (layer, 0, 0)),
                  pl.BlockSpec((D_MODEL, D_MODEL), lambda i: (0, 0))],
        out_specs=[pl.BlockSpec((tm, ATTN_W), lambda i: (i, 0)), pl.BlockSpec((4, tm, LANE), lambda i: (0, i, 0)),
                   row, row, pl.BlockSpec((16, D_MODEL), lambda i: (0, 0)), pl.BlockSpec((8, SSM_W), lambda i: (0, 0))],
        out_shape=[jax.ShapeDtypeStruct((seq, ATTN_W), F32), jax.ShapeDtypeStruct((4, seq, LANE), F32),
                   jax.ShapeDtypeStruct((seq, D_MODEL), BF16), jax.ShapeDtypeStruct((seq, D_MODEL), BF16),
                   jax.ShapeDtypeStruct((16, D_MODEL), F32), jax.ShapeDtypeStruct((8, SSM_W), F32)],
        compiler_params=_params(("arbitrary",)),
    )(dx1, attn, s4, vec, vec512, w_out)


def _mlp_fwd(x1, vec, w_in_t, w_out, layer):
    seq = x1.shape[0]
    tm = ROW_TILE

    def body(x1_ref, vec_ref, wi_hbm, wo_hbm, x2_ref, wi, wo, sems):
        _load_once(pl.program_id(0), [(wi_hbm, wi), (wo_hbm, wo)], sems)
        x1v = x1_ref[...]
        n, _, _ = _rms_fwd(x1v, vec_ref[0, V_PRE_MLP:V_PRE_MLP + 1, :])
        h = (n * (1.0 + vec_ref[0, V_SC2:V_SC2 + 1, :]) + vec_ref[0, V_SH2:V_SH2 + 1, :]).astype(BF16)
        a = _nt(h, wi[...])
        r = jnp.square(jnp.maximum(a, 0.0)).astype(BF16)
        nf, _, _ = _rms_fwd(_nn(r, wo[...]), vec_ref[0, V_POST_MLP:V_POST_MLP + 1, :])
        x2_ref[...] = x1v + vec_ref[0, V_G2:V_G2 + 1, :] * nf

    row = pl.BlockSpec((tm, D_MODEL), lambda i: (i, 0))
    return pl.pallas_call(
        body, name="mlp_fwd", grid=(seq // tm,),
        in_specs=[row, pl.BlockSpec((1, 16, D_MODEL), lambda i: (layer, 0, 0)), HBM, HBM],
        out_specs=row, out_shape=jax.ShapeDtypeStruct((seq, D_MODEL), F32),
        scratch_shapes=[pltpu.VMEM((D_FF, D_MODEL), BF16), pltpu.VMEM((D_FF, D_MODEL), BF16),
                        pltpu.SemaphoreType.DMA((2,))],
        compiler_params=_params(("arbitrary",)),
    )(x1, vec, w_in_t, w_out)


def _mlp_bwd(dx2, x1, vec, w_in_t, w_out, layer):
    seq = x1.shape[0]
    tm = ROW_TILE

    def body(dx2_ref, x1_ref, vec_ref, wi_hbm, wo_hbm, dx1_ref, h_ref, da_ref, r_ref, df_ref, dvec_ref, wi, wo, sems):
        i = pl.program_id(0)
        _load_once(i, [(wi_hbm, wi), (wo_hbm, wo)], sems)

        @pl.when(i == 0)
        def _():
            dvec_ref[...] = jnp.zeros_like(dvec_ref)

        g_pre = vec_ref[0, V_PRE_MLP:V_PRE_MLP + 1, :]
        g_post = vec_ref[0, V_POST_MLP:V_POST_MLP + 1, :]
        sc2 = vec_ref[0, V_SC2:V_SC2 + 1, :]
        n, xh, xr = _rms_fwd(x1_ref[...], g_pre)
        h = (n * (1.0 + sc2) + vec_ref[0, V_SH2:V_SH2 + 1, :]).astype(BF16)
        h_ref[...] = h
        a = _nt(h, wi[...])
        relu = jnp.maximum(a, 0.0)
        r = jnp.square(relu).astype(BF16)
        r_ref[...] = r
        nf, fh, fr = _rms_fwd(_nn(r, wo[...]), g_post)
        dx2v = dx2_ref[...]
        dvec_ref[V_G2:V_G2 + 1, :] += _colsum(dx2v * nf)
        df, dgp_rows = _rms_bwd(dx2v * vec_ref[0, V_G2:V_G2 + 1, :], fh, fr, g_post)
        dvec_ref[V_POST_MLP:V_POST_MLP + 1, :] += _colsum(dgp_rows)
        dfb = df.astype(BF16)
        df_ref[...] = dfb
        da = (_nt(dfb, wo[...]) * (2.0 * relu)).astype(BF16)
        da_ref[...] = da
        dh = _nn(da, wi[...])
        dvec_ref[V_SH2:V_SH2 + 1, :] += _colsum(dh)
        dvec_ref[V_SC2:V_SC2 + 1, :] += _colsum(dh * n)
        dxn, dg_rows = _rms_bwd(dh * (1.0 + sc2), xh, xr, g_pre)
        dvec_ref[V_PRE_MLP:V_PRE_MLP + 1, :] += _colsum(dg_rows)
        dx1_ref[...] = dx2v + dxn

    row = pl.BlockSpec((tm, D_MODEL), lambda i: (i, 0))
    wide = pl.BlockSpec((tm, D_FF), lambda i: (i, 0))
    return pl.pallas_call(
        body, name="mlp_bwd", grid=(seq // tm,),
        in_specs=[row, row, pl.BlockSpec((1, 16, D_MODEL), lambda i: (layer, 0, 0)), HBM, HBM],
        out_specs=[row, row, wide, wide, row, pl.BlockSpec((16, D_MODEL), lambda i: (0, 0))],
        out_shape=[jax.ShapeDtypeStruct((seq, D_MODEL), F32), jax.ShapeDtypeStruct((seq, D_MODEL), BF16),
                   jax.ShapeDtypeStruct((seq, D_FF), BF16), jax.ShapeDtypeStruct((seq, D_FF), BF16),
                   jax.ShapeDtypeStruct((seq, D_MODEL), BF16), jax.ShapeDtypeStruct((16, D_MODEL), F32)],
        scratch_shapes=[pltpu.VMEM((D_FF, D_MODEL), BF16), pltpu.VMEM((D_FF, D_MODEL), BF16),
                        pltpu.SemaphoreType.DMA((2,))],
        compiler_params=_params(("arbitrary",)),
    )(dx2, x1, vec, w_in_t, w_out)


def _loss_head(y, target):
    seq = y.shape[0]
    tm = ROW_TILE

    def body(y_ref, t_ref, dy_ref, part_ref):
        e = y_ref[...] - t_ref[...]
        dy_ref[...] = e * (1.0 / D_MODEL)
        tot = jnp.sum(jnp.sum(e * e, axis=1, keepdims=True), axis=0, keepdims=True) * (0.5 / D_MODEL)
        part_ref[0] = jnp.broadcast_to(tot, (SUBLANE, LANE))

    row = pl.BlockSpec((tm, D_MODEL), lambda i: (i, 0))
    return pl.pallas_call(
        body, name="loss_head", grid=(seq // tm,),
        in_specs=[row, row],
        out_specs=[row, pl.BlockSpec((1, SUBLANE, LANE), lambda i: (i, 0, 0))],
        out_shape=[jax.ShapeDtypeStruct((seq, D_MODEL), F32), jax.ShapeDtypeStruct((seq // tm, SUBLANE, LANE), F32)],
        compiler_params=_params(("parallel",)),
    )(y, target)


def _matmul_tn(a, b, out_dtype, name, pieces=1):
    kk, m = a.shape
    n = b.shape[1]
    tm = min(m, 512)
    tn = n // pieces if pieces > 1 else min(n, 1280)
    tk = 512
    nk = kk // tk

    def body(a_ref, b_ref, o_ref, acc):
        k = pl.program_id(2)

        @pl.when(k == 0)
        def _():
            acc[...] = jnp.zeros_like(acc)

        acc[...] += _tn(a_ref[...], b_ref[...])

        @pl.when(k == nk - 1)
        def _():
            if pieces > 1:
                o_ref[0] = acc[...].astype(out_dtype)
            else:
                o_ref[...] = acc[...].astype(out_dtype)

    if pieces > 1:
        out_spec = pl.BlockSpec((1, tm, tn), lambda i, j, k: (j, i, 0))
        out_shape = jax.ShapeDtypeStruct((pieces, m, tn), out_dtype)
    else:
        out_spec = pl.BlockSpec((tm, tn), lambda i, j, k: (i, j))
        out_shape = jax.ShapeDtypeStruct((m, n), out_dtype)
    return pl.pallas_call(
        body, name=name, grid=(m // tm, n // tn, nk),
        in_specs=[pl.BlockSpec((tk, tm), lambda i, j, k: (k, i)), pl.BlockSpec((tk, tn), lambda i, j, k: (k, j))],
        out_specs=out_spec, out_shape=out_shape,
        scratch_shapes=[pltpu.VMEM((tm, tn), F32)],
        compiler_params=_params(("parallel", "parallel", "arbitrary")),
    )(a, b)


def _attn_probs(i, q_h, kband, slope, sink):
    rr = lax.broadcasted_iota(jnp.int32, (BLOCK, 2 * BLOCK), 0)
    jj = lax.broadcasted_iota(jnp.int32, (BLOCK, 2 * BLOCK), 1)
    diff = BLOCK + rr - jj
    valid = (diff >= 0) & (diff < WINDOW) & ((jj >= BLOCK) | (i > 0))
    s = _nt(q_h, kband) * (HEAD_DIM ** -0.5)
    s = jnp.where(valid, s - slope * diff.astype(F32), NEG_INF)
    m = jnp.maximum(jnp.max(s, axis=1, keepdims=True), sink)
    p = jnp.exp(s - m)
    ps = jnp.exp(sink - m)
    inv = 1.0 / (jnp.sum(p, axis=1, keepdims=True) + ps)
    return p * inv, ps * inv


def _bands(kvp, kvc, h):
    kband = jnp.concatenate([kvp[:, h * HEAD_DIM:(h + 1) * HEAD_DIM], kvc[:, h * HEAD_DIM:(h + 1) * HEAD_DIM]], axis=0)
    v0 = KV_W + h * HEAD_DIM
    vband = jnp.concatenate([kvp[:, v0:v0 + HEAD_DIM], kvc[:, v0:v0 + HEAD_DIM]], axis=0)
    return kband, vband


def _attn_fwd(q, kv, sinks):
    seq = q.shape[0]
    nb = seq // BLOCK
    slopes = _alibi_slopes()

    def body(sink_ref, q_ref, kvp_ref, kvc_ref, o_ref):
        i = pl.program_id(0)
        qv, kvp, kvc = q_ref[...], kvp_ref[...], kvc_ref[...]
        for h in range(N_KV):
            kband, vband = _bands(kvp, kvc, h)
            for g in range(Q_PER_KV):
                hq = h * Q_PER_KV + g
                cols = slice(hq * HEAD_DIM, (hq + 1) * HEAD_DIM)
                pr, _ = _attn_probs(i, qv[:, cols], kband, slopes[hq], sink_ref[hq])
                o_ref[:, cols] = _nn(pr.astype(BF16), vband)

    return pl.pallas_call(
        body, name="attn_fwd", grid=(nb,),
        in_specs=[pl.BlockSpec(memory_space=pltpu.SMEM),
                  pl.BlockSpec((BLOCK, ATTN_W), lambda i: (i, 0)),
                  pl.BlockSpec((BLOCK, 2 * KV_W), lambda i: (jnp.maximum(i - 1, 0), 0)),
                  pl.BlockSpec((BLOCK, 2 * KV_W), lambda i: (i, 0))],
        out_specs=pl.BlockSpec((BLOCK, ATTN_W), lambda i: (i, 0)),
        out_shape=jax.ShapeDtypeStruct((seq, ATTN_W), F32),
        compiler_params=_params(("parallel",)),
    )(sinks, q, kv, kv)


def _attn_bwd(q, kv, sinks, dout):
    seq = q.shape[0]
    nb = seq // BLOCK
    slopes = _alibi_slopes()
    scale = HEAD_DIM ** -0.5

    def body(sink_ref, q_ref, kvp_ref, kvc_ref, do_ref, dq_ref, dkv_ref, dsk_ref, prev):
        step = pl.program_id(0)
        i = nb - 1 - step

        @pl.when(step == 0)
        def _():
            prev[...] = jnp.zeros_like(prev)

        qv, kvp, kvc = q_ref[...], kvp_ref[...], kvc_ref[...]
        dov = do_ref[...].astype(BF16)
        dk, dv, dsk = [], [], []
        for h in range(N_KV):
            kband, vband = _bands(kvp, kvc, h)
            dk_h = jnp.zeros((2 * BLOCK, HEAD_DIM), F32)
            dv_h = jnp.zeros((2 * BLOCK, HEAD_DIM), F32)
            for g in range(Q_PER_KV):
                hq = h * Q_PER_KV + g
                cols = slice(hq * HEAD_DIM, (hq + 1) * HEAD_DIM)
                q_h, do_h = qv[:, cols], dov[:, cols]
                pr, ps = _attn_probs(i, q_h, kband, slopes[hq], sink_ref[hq])
                dp = _nt(do_h, vband)
                delta = jnp.sum(pr * dp, axis=1, keepdims=True)
                ds = (pr * (dp - delta) * scale).astype(BF16)
                dsk.append(jnp.broadcast_to(-_colsum(ps * delta), (1, LANE)))
                dq_ref[:, cols] = _nn(ds, kband)
                dk_h = dk_h + _tn(ds, q_h)
                dv_h = dv_h + _tn(pr.astype(BF16), do_h)
            dk.append(dk_h)
            dv.append(dv_h)
        band = jnp.concatenate(dk + dv, axis=1)
        dkv_ref[...] = band[BLOCK:, :] + prev[...]
        prev[...] = band[:BLOCK, :]
        dsk_ref[0] = jnp.concatenate(dsk, axis=0)

    return pl.pallas_call(
        body, name="attn_bwd", grid=(nb,),
        in_specs=[pl.BlockSpec(memory_space=pltpu.SMEM),
                  pl.BlockSpec((BLOCK, ATTN_W), lambda s: (nb - 1 - s, 0)),
                  pl.BlockSpec((BLOCK, 2 * KV_W), lambda s: (jnp.maximum(nb - 2 - s, 0), 0)),
                  pl.BlockSpec((BLOCK, 2 * KV_W), lambda s: (nb - 1 - s, 0)),
                  pl.BlockSpec((BLOCK, ATTN_W), lambda s: (nb - 1 - s, 0))],
        out_specs=[pl.BlockSpec((BLOCK, ATTN_W), lambda s: (nb - 1 - s, 0)),
                   pl.BlockSpec((BLOCK, 2 * KV_W), lambda s: (nb - 1 - s, 0)),
                   pl.BlockSpec((1, N_Q, LANE), lambda s: (nb - 1 - s, 0, 0))],
        out_shape=[jax.ShapeDtypeStruct((seq, ATTN_W), F32), jax.ShapeDtypeStruct((seq, 2 * KV_W), F32),
                   jax.ShapeDtypeStruct((nb, N_Q, LANE), F32)],
        scratch_shapes=[pltpu.VMEM((BLOCK, 2 * KV_W), F32)],
        compiler_params=_params(("arbitrary",)),
    )(sinks, q, kv, kv, dout)


def _discretize(lr, li, ldt, br, bi):
    dt = jnp.exp(ldt)
    mag = jnp.exp(lr * dt)
    ang = li * dt
    ab_r = mag * jnp.cos(ang)
    ab_i = mag * jnp.sin(ang)
    nr = ab_r - 1.0
    ni = ab_i
    den = lr * lr + li * li
    f_r = (nr * lr + ni * li) / den
    f_i = (ni * lr - nr * li) / den
    return ab_r, ab_i, f_r * br - f_i * bi, f_r * bi + f_i * br


def _ssm_prepare(lr, li, ldt, br, bi):
    n = lr.shape[0]
    tn = N_CH
    col = pl.BlockSpec((tn, 1), lambda i: (i, 0))
    mat = pl.BlockSpec((tn, GROUP_W), lambda i: (i, 0))

    def body(lr_ref, li_ref, ldt_ref, br_ref, bi_ref, ar_ref, ai_ref, bbr_ref, bbi_ref):
        ar, ai, bbr, bbi = _discretize(lr_ref[...], li_ref[...], ldt_ref[...], br_ref[...], bi_ref[...])
        ar_ref[...] = ar
        ai_ref[...] = ai
        bbr_ref[...] = bbr
        bbi_ref[...] = bbi

    cs = jax.ShapeDtypeStruct((n, 1), F32)
    ms = jax.ShapeDtypeStruct((n, GROUP_W), F32)
    return pl.pallas_call(
        body, name="ssm_prepare", grid=(n // tn,),
        in_specs=[col, col, col, mat, mat], out_specs=[col, col, mat, mat], out_shape=[cs, cs, ms, ms],
        compiler_params=_params(("parallel",)),
    )(lr, li, ldt, br, bi)


def _ssm_prepare_bwd(lr, li, ldt, br, bi, dar, dai, dbbr, dbbi):
    n = lr.shape[0]
    tn = N_CH
    col = pl.BlockSpec((tn, 1), lambda i: (i, 0))
    mat = pl.BlockSpec((tn, GROUP_W), lambda i: (i, 0))

    def body(lr_ref, li_ref, ldt_ref, br_ref, bi_ref, dar_ref, dai_ref, dbbr_ref, dbbi_ref,
             dlr_ref, dli_ref, dldt_ref, dbr_ref, dbi_ref):
        _, vjp = jax.vjp(_discretize, lr_ref[...], li_ref[...], ldt_ref[...], br_ref[...], bi_ref[...])
        dlr, dli, dldt, dbr, dbi = vjp((dar_ref[...], dai_ref[...], dbbr_ref[...], dbbi_ref[...]))
        dlr_ref[...] = dlr
        dli_ref[...] = dli
        dldt_ref[...] = dldt
        dbr_ref[...] = dbr
        dbi_ref[...] = dbi

    cs = jax.ShapeDtypeStruct((n, 1), F32)
    ms = jax.ShapeDtypeStruct((n, GROUP_W), F32)
    return pl.pallas_call(
        body, name="ssm_prepare_bwd", grid=(n // tn,),
        in_specs=[col, col, col, mat, mat, col, col, mat, mat],
        out_specs=[col, col, col, mat, mat], out_shape=[cs, cs, cs, ms, ms],
        compiler_params=_params(("parallel",)),
    )(lr, li, ldt, br, bi, dar, dai, dbbr, dbbi)


def _load_slabs(src4_ref, dst):
    for s in range(STEPS):
        dst[s * SEGS:(s + 1) * SEGS, :] = jnp.concatenate(
            [src4_ref[j, pl.ds(s, SEGS, stride=STEPS), :] for j in range(4)], axis=1)


def _store_slabs(src, dst4_ref):
    for s in range(STEPS):
        for j in range(4):
            dst4_ref[j, pl.ds(s, SEGS, stride=STEPS), :] = src[s * SEGS:(s + 1) * SEGS, j * LANE:(j + 1) * LANE]


def _power_table(ar_ref, ai_ref, pwr, pwi):
    ar, ai = ar_ref[0], ai_ref[0]
    pr, pi = ar, ai
    pwr[0:1, :] = pr
    pwi[0:1, :] = pi
    for k in range(1, STEPS):
        pr, pi = pr * ar - pi * ai, pr * ai + pi * ar
        pwr[k:k + 1, :] = pr
        pwi[k:k + 1, :] = pi


def _scan_states(ubf, ar_ref, ai_ref, bbr_ref, bbi_ref, pwr, pwi, cin_r, cin_i, hr, hi):
    for k in range(2):
        rows = slice(k * 256, (k + 1) * 256)
        cols = slice(k * HALF_CH, (k + 1) * HALF_CH)
        hr[:, cols] = _nn(ubf[:, rows], bbr_ref[0, k])
        hi[:, cols] = _nn(ubf[:, rows], bbi_ref[0, k])
    for st in range(N_CH // STRIP):
        cs = slice(st * STRIP, (st + 1) * STRIP)
        arb = jnp.broadcast_to(ar_ref[0, :, cs], (SEGS, STRIP))
        aib = jnp.broadcast_to(ai_ref[0, :, cs], (SEGS, STRIP))

        def step(s, carry, cs=cs, arb=arb, aib=aib):
            cr, ci = carry
            rows = pl.ds(pl.multiple_of(s * SEGS, SEGS), SEGS)
            nr = arb * cr - aib * ci + hr[rows, cs]
            ni = arb * ci + aib * cr + hi[rows, cs]
            hr[rows, cs] = nr
            hi[rows, cs] = ni
            return nr, ni

        zero = jnp.zeros((SEGS, STRIP), F32)
        lax.fori_loop(0, STEPS, step, (zero, zero))
    last = slice((STEPS - 1) * SEGS, STEPS * SEGS)
    end_r, end_i = hr[last, :], hi[last, :]
    a64r, a64i = pwr[STEPS - 1:STEPS, :], pwi[STEPS - 1:STEPS, :]
    cr, ci = cin_r, cin_i
    rows_r, rows_i = [], []
    for j in range(SEGS):
        rows_r.append(cr)
        rows_i.append(ci)
        cr, ci = (a64r * cr - a64i * ci + end_r[j:j + 1, :], a64r * ci + a64i * cr + end_i[j:j + 1, :])
    cm_r, cm_i = jnp.concatenate(rows_r, axis=0), jnp.concatenate(rows_i, axis=0)
    for st in range(N_CH // STRIP):
        cs = slice(st * STRIP, (st + 1) * STRIP)
        cmr, cmi = cm_r[:, cs], cm_i[:, cs]

        def fix(s, carry, cs=cs, cmr=cmr, cmi=cmi):
            rows = pl.ds(pl.multiple_of(s * SEGS, SEGS), SEGS)
            pr, pi = pwr[pl.ds(s, 1), cs], pwi[pl.ds(s, 1), cs]
            hr[rows, cs] = hr[rows, cs] + (pr * cmr - pi * cmi)
            hi[rows, cs] = hi[rows, cs] + (pr * cmi + pi * cmr)
            return carry

        lax.fori_loop(0, STEPS, fix, 0)
    return (cm_r, cm_i), (cr, ci)


def _ssm_outputs(u, hr, hi, crt_ref, cit_ref, vec512_ref, wg_ref):
    ys = []
    for k in range(2):
        cols = slice(k * HALF_CH, (k + 1) * HALF_CH)
        ys.append(_nn(hr[:, cols].astype(BF16), crt_ref[0, k]) - _nn(hi[:, cols].astype(BF16), cit_ref[0, k]))
    y = jnp.concatenate(ys, axis=1) + vec512_ref[0, H_DSKIP:H_DSKIP + 1, :] * u
    z, t = _gelu(y)
    gate = jax.nn.sigmoid(_nn(z.astype(BF16), wg_ref[...]) + vec512_ref[0, H_BGLU:H_BGLU + 1, :])
    return y, z, t, gate


def _ssm_specs(layer, nck, rev):
    def chunk(i):
        return nck - 1 - i if rev else i

    return [pl.BlockSpec((4, CHUNK, LANE), lambda i: (0, chunk(i), 0)),
            pl.BlockSpec((1, 1, N_CH), lambda i: (layer, 0, 0)),
            pl.BlockSpec((1, 1, N_CH), lambda i: (layer, 0, 0)),
            pl.BlockSpec((1, 2, 256, HALF_CH), lambda i: (layer, 0, 0, 0)),
            pl.BlockSpec((1, 2, 256, HALF_CH), lambda i: (layer, 0, 0, 0)),
            pl.BlockSpec((1, 2, HALF_CH, 256), lambda i: (layer, 0, 0, 0)),
            pl.BlockSpec((1, 2, HALF_CH, 256), lambda i: (layer, 0, 0, 0)),
            pl.BlockSpec((1, 8, SSM_W), lambda i: (layer, 0, 0)),
            pl.BlockSpec((SSM_W, SSM_W), lambda i: (0, 0))]


def _ssm_fwd(u4, a_r, a_i, bb_r, bb_i, c_rt, c_it, vec512, w_glu, layer):
    seq = u4.shape[1]
    nck = seq // CHUNK

    def body(u4_ref, ar_ref, ai_ref, bbr_ref, bbi_ref, crt_ref, cit_ref, vec512_ref, wg_ref,
             s4_ref, hcr_ref, hci_ref, hr, hi, pwr, pwi, car, cai, ubuf, obuf):
        i = pl.program_id(0)

        @pl.when(i == 0)
        def _():
            car[...] = jnp.zeros_like(car)
            cai[...] = jnp.zeros_like(cai)
            _power_table(ar_ref, ai_ref, pwr, pwi)

        _load_slabs(u4_ref, ubuf)
        u = ubuf[...]
        cin_r, cin_i = car[...], cai[...]
        hcr_ref[0] = jnp.broadcast_to(cin_r, (SEGS, N_CH))
        hci_ref[0] = jnp.broadcast_to(cin_i, (SEGS, N_CH))
        _, (er, ei) = _scan_states(u.astype(BF16), ar_ref, ai_ref, bbr_ref, bbi_ref, pwr, pwi, cin_r, cin_i, hr, hi)
        car[...] = er
        cai[...] = ei
        _, z, _, gate = _ssm_outputs(u, hr, hi, crt_ref, cit_ref, vec512_ref, wg_ref)
        obuf[...] = z * gate
        _store_slabs(obuf, s4_ref)

    return pl.pallas_call(
        body, name="ssm_fwd", grid=(nck,),
        in_specs=_ssm_specs(layer, nck, False),
        out_specs=[pl.BlockSpec((4, CHUNK, LANE), lambda i: (0, i, 0)),
                   pl.BlockSpec((1, SEGS, N_CH), lambda i: (i, 0, 0)),
                   pl.BlockSpec((1, SEGS, N_CH), lambda i: (i, 0, 0))],
        out_shape=[jax.ShapeDtypeStruct((4, seq, LANE), F32), jax.ShapeDtypeStruct((nck, SEGS, N_CH), F32),
                   jax.ShapeDtypeStruct((nck, SEGS, N_CH), F32)],
        scratch_shapes=[pltpu.VMEM((CHUNK, N_CH), F32), pltpu.VMEM((CHUNK, N_CH), F32),
                        pltpu.VMEM((STEPS, N_CH), F32), pltpu.VMEM((STEPS, N_CH), F32),
                        pltpu.VMEM((1, N_CH), F32), pltpu.VMEM((1, N_CH), F32),
                        pltpu.VMEM((CHUNK, SSM_W), F32), pltpu.VMEM((CHUNK, SSM_W), F32)],
        compiler_params=_params(("arbitrary",)),
    )(u4, a_r, a_i, bb_r, bb_i, c_rt, c_it, vec512, w_glu)


def _ssm_bwd(u4, ds4, hc_r, hc_i, a_r, a_i, bb_r, bb_i, c_rt, c_it, vec512, w_glu, layer):
    seq = u4.shape[1]
    nck = seq // CHUNK

    def body(u4_ref, ar_ref, ai_ref, bbr_ref, bbi_ref, crt_ref, cit_ref, vec512_ref, wg_ref, ds4_ref, hcr_ref, hci_ref,
             du4_ref, dbbr_ref, dbbi_ref, dcrt_ref, dcit_ref, dar_ref, dai_ref, dwg_ref, dvec_ref,
             hr, hi, gr, gi, pwr, pwi, gcr, gci, accr, acci, ubuf, dbuf):
        i = pl.program_id(0)

        @pl.when(i == 0)
        def _():
            for ref in (gcr, gci, accr, acci, dbbr_ref, dbbi_ref, dcrt_ref, dcit_ref, dwg_ref, dvec_ref):
                ref[...] = jnp.zeros_like(ref)
            _power_table(ar_ref, ai_ref, pwr, pwi)

        _load_slabs(u4_ref, ubuf)
        u = ubuf[...]
        ubf = u.astype(BF16)
        cin_r, cin_i = hcr_ref[0, 0:1, :], hci_ref[0, 0:1, :]
        (cm_r, cm_i), _ = _scan_states(ubf, ar_ref, ai_ref, bbr_ref, bbi_ref, pwr, pwi, cin_r, cin_i, hr, hi)
        y, z, t, gate = _ssm_outputs(u, hr, hi, crt_ref, cit_ref, vec512_ref, wg_ref)
        _load_slabs(ds4_ref, dbuf)
        ds = dbuf[...]
        da = ds * z * gate * (1.0 - gate)
        dab = da.astype(BF16)
        dz = ds * gate + _nt(dab, wg_ref[...])
        dwg_ref[...] += _tn(z.astype(BF16), dab)
        dvec_ref[H_BGLU:H_BGLU + 1, :] += _colsum(da)
        dy = dz * _gelu_grad(y, t)
        dvec_ref[H_DSKIP:H_DSKIP + 1, :] += _colsum(dy * u)
        du_skip = dy * vec512_ref[0, H_DSKIP:H_DSKIP + 1, :]
        dyb = dy.astype(BF16)
        for k in range(2):
            rows = slice(k * 256, (k + 1) * 256)
            cols = slice(k * HALF_CH, (k + 1) * HALF_CH)
            dcrt_ref[k] += _tn(hr[:, cols].astype(BF16), dyb[:, rows])
            dcit_ref[k] -= _tn(hi[:, cols].astype(BF16), dyb[:, rows])
            gr[:, cols] = _nt(dyb[:, rows], crt_ref[0, k])
            gi[:, cols] = -_nt(dyb[:, rows], cit_ref[0, k])
        for st in range(N_CH // STRIP):
            cs = slice(st * STRIP, (st + 1) * STRIP)
            arb = jnp.broadcast_to(ar_ref[0, :, cs], (SEGS, STRIP))
            aib = jnp.broadcast_to(ai_ref[0, :, cs], (SEGS, STRIP))

            def step(k, carry, cs=cs, arb=arb, aib=aib):
                cr, ci = carry
                rows = pl.ds(pl.multiple_of((STEPS - 1 - k) * SEGS, SEGS), SEGS)
                nr = gr[rows, cs] + (arb * cr + aib * ci)
                ni = gi[rows, cs] + (arb * ci - aib * cr)
                gr[rows, cs] = nr
                gi[rows, cs] = ni
                return nr, ni

            zero = jnp.zeros((SEGS, STRIP), F32)
            lax.fori_loop(0, STEPS, step, (zero, zero))
        first_r, first_i = gr[0:SEGS, :], gi[0:SEGS, :]
        a64r, a64i = pwr[STEPS - 1:STEPS, :], pwi[STEPS - 1:STEPS, :]
        dr_, di_ = gcr[...], gci[...]
        rows_r, rows_i = [None] * SEGS, [None] * SEGS
        for j in range(SEGS - 1, -1, -1):
            rows_r[j], rows_i[j] = dr_, di_
            dr_, di_ = (first_r[j:j + 1, :] + (a64r * dr_ + a64i * di_), first_i[j:j + 1, :] + (a64r * di_ - a64i * dr_))
        gcr[...] = dr_
        gci[...] = di_
        dm_r, dm_i = jnp.concatenate(rows_r, axis=0), jnp.concatenate(rows_i, axis=0)
        for st in range(N_CH // STRIP):
            cs = slice(st * STRIP, (st + 1) * STRIP)
            dmr, dmi = dm_r[:, cs], dm_i[:, cs]

            def fixed(s, cs=cs, dmr=dmr, dmi=dmi):
                rows = pl.ds(pl.multiple_of(s * SEGS, SEGS), SEGS)
                pr, pi = pwr[pl.ds(STEPS - 1 - s, 1), cs], pwi[pl.ds(STEPS - 1 - s, 1), cs]
                g_r = gr[rows, cs] + (pr * dmr + pi * dmi)
                g_i = gi[rows, cs] + (pr * dmi - pi * dmr)
                gr[rows, cs] = g_r
                gi[rows, cs] = g_i
                return g_r, g_i

            g_r, g_i = fixed(jnp.int32(0))
            acc0 = (g_r * cm_r[:, cs] + g_i * cm_i[:, cs], g_i * cm_r[:, cs] - g_r * cm_i[:, cs])

            def step(s, carry, cs=cs, fixed=fixed):
                sr, si = carry
                g_r, g_i = fixed(s)
                prev = pl.ds(pl.multiple_of((s - 1) * SEGS, SEGS), SEGS)
                hpr, hpi = hr[prev, cs], hi[prev, cs]
                return sr + (g_r * hpr + g_i * hpi), si + (g_i * hpr - g_r * hpi)

            sr, si = lax.fori_loop(1, STEPS, step, acc0)
            accr[:, cs] += sr
            acci[:, cs] += si
        grb, gib = gr[...].astype(BF16), gi[...].astype(BF16)
        dus = []
        for k in range(2):
            rows = slice(k * 256, (k + 1) * 256)
            cols = slice(k * HALF_CH, (k + 1) * HALF_CH)
            dus.append(_nt(grb[:, cols], bbr_ref[0, k]) + _nt(gib[:, cols], bbi_ref[0, k]))
            dbbr_ref[k] += _tn(ubf[:, rows], grb[:, cols])
            dbbi_ref[k] += _tn(ubf[:, rows], gib[:, cols])
        dbuf[...] = jnp.concatenate(dus, axis=1) + du_skip
        _store_slabs(dbuf, du4_ref)

        @pl.when(i == nck - 1)
        def _():
            dar_ref[...] = _colsum(accr[...])
            dai_ref[...] = _colsum(acci[...])

    rev4 = pl.BlockSpec((4, CHUNK, LANE), lambda i: (0, nck - 1 - i, 0))
    hc_spec = pl.BlockSpec((1, SEGS, N_CH), lambda i: (nck - 1 - i, 0, 0))
    fixed2 = lambda shape: pl.BlockSpec(shape, lambda i: (0,) * len(shape))
    return pl.pallas_call(
        body, name="ssm_bwd", grid=(nck,),
        in_specs=_ssm_specs(layer, nck, True) + [rev4, hc_spec, hc_spec],
        out_specs=[rev4, fixed2((2, 256, HALF_CH)), fixed2((2, 256, HALF_CH)), fixed2((2, HALF_CH, 256)),
                   fixed2((2, HALF_CH, 256)), fixed2((1, N_CH)), fixed2((1, N_CH)), fixed2((SSM_W, SSM_W)),
                   fixed2((8, SSM_W))],
        out_shape=[jax.ShapeDtypeStruct((4, seq, LANE), F32),
                   jax.ShapeDtypeStruct((2, 256, HALF_CH), F32), jax.ShapeDtypeStruct((2, 256, HALF_CH), F32),
                   jax.ShapeDtypeStruct((2, HALF_CH, 256), F32), jax.ShapeDtypeStruct((2, HALF_CH, 256), F32),
                   jax.ShapeDtypeStruct((1, N_CH), F32), jax.ShapeDtypeStruct((1, N_CH), F32),
                   jax.ShapeDtypeStruct((SSM_W, SSM_W), F32), jax.ShapeDtypeStruct((8, SSM_W), F32)],
        scratch_shapes=[pltpu.VMEM((CHUNK, N_CH), F32), pltpu.VMEM((CHUNK, N_CH), F32),
                        pltpu.VMEM((CHUNK, N_CH), F32), pltpu.VMEM((CHUNK, N_CH), F32),
                        pltpu.VMEM((STEPS, N_CH), F32), pltpu.VMEM((STEPS, N_CH), F32),
                        pltpu.VMEM((1, N_CH), F32), pltpu.VMEM((1, N_CH), F32),
                        pltpu.VMEM((SEGS, N_CH), F32), pltpu.VMEM((SEGS, N_CH), F32),
                        pltpu.VMEM((CHUNK, SSM_W), F32), pltpu.VMEM((CHUNK, SSM_W), F32)],
        compiler_params=_params(("arbitrary",)),
    )(u4, a_r, a_i, bb_r, bb_i, c_rt, c_it, vec512, w_glu, ds4, hc_r, hc_i)


def _block_diag(t):
    nl, _, ng, a, b = t.shape
    eye = jnp.eye(ng, dtype=t.dtype)
    return jnp.einsum("gh,lkgab->lkgahb", eye, t).reshape(nl, 2, ng * a, ng * b)


def _diag_blocks(t, a, b):
    nl = t.shape[0]
    ng = N_GROUPS // 2
    return jnp.einsum("lkgagb->lkgab", t.reshape(nl, 2, ng, a, ng, b))


def _local_step(x, loss_target, mod, p, comm):
    nl = mod.shape[0]
    pad1024 = jnp.zeros((nl, 16 - 10, D_MODEL), F32)
    vec = jnp.concatenate([mod.reshape(nl, N_MOD, D_MODEL), p["pre_mix_g"][:, None], p["post_mix_g"][:, None],
                           p["pre_mlp_g"][:, None], p["post_mlp_g"][:, None], pad1024], axis=1)
    vec512 = jnp.concatenate([p["attn_out_g"][:, None], p["ssm_out_g"][:, None], p["d_skip"][:, None],
                              p["b_glu"][:, None], jnp.zeros((nl, 4, SSM_W), F32)], axis=1)
    n_all = nl * N_CH
    lr = p["lam_re"].reshape(n_all, 1)
    li = p["lam_im"].reshape(n_all, 1)
    ldt = jnp.broadcast_to(p["log_dt"][:, :, None], (nl, N_GROUPS, STATE)).reshape(n_all, 1)
    br = p["b_re"].reshape(n_all, GROUP_W)
    bi = p["b_im"].reshape(n_all, GROUP_W)
    ab_r, ab_i, bb_r, bb_i = _ssm_prepare(lr, li, ldt, br, bi)
    a_r = ab_r.reshape(nl, 1, N_CH)
    a_i = ab_i.reshape(nl, 1, N_CH)

    def dense_b(bb):
        return _block_diag(bb.reshape(nl, 2, 16, STATE, GROUP_W).transpose(0, 1, 2, 4, 3)).astype(BF16)

    def dense_c(cc):
        return _block_diag(cc.reshape(nl, 2, 16, GROUP_W, STATE).transpose(0, 1, 2, 4, 3)).astype(BF16)

    bbr_d, bbi_d = dense_b(bb_r), dense_b(bb_i)
    crt_d, cit_d = dense_c(p["c_re"]), dense_c(p["c_im"])

    saved = []
    xl = x
    mixer_w, mlp_w = [None] * nl, [None] * nl
    for l in range(nl):
        mixer_w[l], tok = comm.mixer_weights(l, xl)
        w_in_t, w_glu, w_out = mixer_w[l]
        q, kv, u4, h1 = _in_proj_fwd(xl, _after(vec, *tok), w_in_t, l)
        attn = _attn_fwd(q, kv, p["attn_sinks"][l])
        s4, hc_r, hc_i = _ssm_fwd(u4, a_r, a_i, bbr_d, bbi_d, crt_d, cit_d, vec512, w_glu, l)
        x1 = _out_proj_fwd(xl, attn, s4, vec, vec512, w_out, l)
        mlp_w[l] = comm.mlp_weights(l, x1)
        x2 = _mlp_fwd(x1, vec, mlp_w[l][0], mlp_w[l][1], l)
        saved.append((xl, q, kv, u4, h1, attn, s4, hc_r, hc_i, x1))
        xl = x2

    dx, loss_parts = _loss_head(xl, loss_target)
    loss = jnp.sum(loss_parts[:, 0, 0])

    dvec_l, dvec512_l, dsink_l = [None] * nl, [None] * nl, [None] * nl
    dab_r, dab_i, dbb_r, dbb_i, dc_re, dc_im = ([None] * nl for _ in range(6))
    toks = []
    for l in range(nl - 1, -1, -1):
        xl, q, kv, u4, h1, attn, s4, hc_r, hc_i, x1 = saved[l]
        w_in_t, w_glu, w_out = mixer_w[l]
        dx1, h2, da, r, df, dvec_m = _mlp_bwd(dx, x1, _after(vec, *toks), mlp_w[l][0], mlp_w[l][1], l)
        toks = comm.after_mlp_bwd(l, dx1)
        dw_mlp_out = _matmul_tn(r, df, BF16, "dw_mlp_out").reshape(4, D_FF // 4, D_MODEL)
        dw_mlp_in = _matmul_tn(h2, da, BF16, "dw_mlp_in", pieces=4)
        toks = toks + comm.mlp_grads(l, [dw_mlp_in, dw_mlp_out])
        dattn, ds4, heads, dmixed, dvec_o, dvec512_o = _out_proj_bwd(
            dx1, attn, s4, _after(vec, *toks), vec512, w_out, l)
        dw_out = _matmul_tn(heads, dmixed, BF16, "dw_out").reshape(4, D_MODEL // 4, D_MODEL)
        dq, dkv, dsk = _attn_bwd(q, kv, p["attn_sinks"][l], dattn)
        (du4, dbbr, dbbi, dcrt, dcit, dar, dai, dwg, dvec512_s) = _ssm_bwd(
            u4, ds4, hc_r, hc_i, a_r, a_i, bbr_d, bbi_d, crt_d, cit_d, vec512, w_glu, l)
        dw_glu = dwg.astype(BF16).reshape(4, SSM_W // 4, SSM_W)
        dx, dproj, dvec_i = _in_proj_bwd(dx1, dq, dkv, du4, xl, vec, w_in_t, l)
        toks = comm.after_in_proj_bwd(l, dx)
        dw_in = _matmul_tn(h1, dproj, BF16, "dw_in")
        toks = toks + comm.mixer_grads(l, [dw_in.reshape(D_MODEL, 4, IN_W // 4).transpose(1, 0, 2), dw_glu, dw_out])
        dvec_l[l] = dvec_m + dvec_o + dvec_i
        dvec512_l[l] = dvec512_o + dvec512_s
        dsink_l[l] = jnp.sum(dsk[:, :, 0], axis=0)
        dab_r[l], dab_i[l], dbb_r[l], dbb_i[l], dc_re[l], dc_im[l] = dar, dai, dbbr, dbbi, dcrt, dcit

    dvec = _after(jnp.stack(dvec_l), *toks)
    dvec512 = jnp.stack(dvec512_l)
    dbb_r_c = _diag_blocks(jnp.stack(dbb_r), GROUP_W, STATE).transpose(0, 1, 2, 4, 3).reshape(n_all, GROUP_W)
    dbb_i_c = _diag_blocks(jnp.stack(dbb_i), GROUP_W, STATE).transpose(0, 1, 2, 4, 3).reshape(n_all, GROUP_W)
    dlr, dli, dldt, dbr, dbi = _ssm_prepare_bwd(
        lr, li, ldt, br, bi, jnp.stack(dab_r).reshape(n_all, 1), jnp.stack(dab_i).reshape(n_all, 1), dbb_r_c, dbb_i_c)
    c_shape = (nl, N_GROUPS, GROUP_W, STATE)
    small = {
        "b_ada": dvec[:, :N_MOD].reshape(nl, N_MOD * D_MODEL),
        "pre_mix_g": dvec[:, V_PRE_MIX], "post_mix_g": dvec[:, V_POST_MIX],
        "pre_mlp_g": dvec[:, V_PRE_MLP], "post_mlp_g": dvec[:, V_POST_MLP],
        "attn_sinks": jnp.stack(dsink_l),
        "lam_re": dlr.reshape(nl, N_GROUPS, STATE), "lam_im": dli.reshape(nl, N_GROUPS, STATE),
        "log_dt": jnp.sum(dldt.reshape(nl, N_GROUPS, STATE), axis=-1),
        "b_re": dbr.reshape(nl, N_GROUPS, STATE, GROUP_W), "b_im": dbi.reshape(nl, N_GROUPS, STATE, GROUP_W),
        "c_re": _diag_blocks(jnp.stack(dc_re), STATE, GROUP_W).transpose(0, 1, 2, 4, 3).reshape(c_shape),
        "c_im": _diag_blocks(jnp.stack(dc_im), STATE, GROUP_W).transpose(0, 1, 2, 4, 3).reshape(c_shape),
        "d_skip": dvec512[:, H_DSKIP], "b_glu": dvec512[:, H_BGLU],
        "attn_out_g": dvec512[:, H_ATTN_G], "ssm_out_g": dvec512[:, H_SSM_G],
    }
    return loss, dx, small, small["b_ada"]


WEIGHTS = ["w_ada", "b_ada", "pre_mix_g", "w_in", "attn_sinks", "lam_re", "lam_im", "log_dt", "b_re", "b_im", "c_re",
           "c_im", "d_skip", "w_glu", "b_glu", "attn_out_g", "ssm_out_g", "w_out", "post_mix_g", "pre_mlp_g",
           "w_mlp_in", "w_mlp_out", "post_mlp_g"]
BIG = ["w_in", "w_glu", "w_out", "w_mlp_in", "w_mlp_out"]
SMALL = [n for n in WEIGHTS if n not in BIG and n != "w_ada"]
PACK_ROWS = 256


def _pack(parts):
    rows = []
    for n in SMALL:
        flat = parts[n].reshape(-1)
        pad = (-flat.shape[0]) % (PACK_ROWS * LANE)
        rows.append(jnp.pad(flat, (0, pad)).reshape(-1, LANE))
    return jnp.concatenate(rows, axis=0)


def _unpack(packed, shapes):
    out, r0 = {}, 0
    for n in SMALL:
        size = int(np.prod(shapes[n]))
        rows = -(-size // (PACK_ROWS * LANE)) * PACK_ROWS
        out[n] = packed[r0:r0 + rows].reshape(-1)[:size].reshape(shapes[n])
        r0 += rows
    return out


MIXER = ["w_in", "w_glu", "w_out"]
MLP = ["w_mlp_in", "w_mlp_out"]


class _Exchanges:
    def __init__(self, shards, wts, mom, var, chip):
        self.shards, self.wts, self.mom, self.var, self.chip = shards, wts, mom, var, chip
        self.chip_arr = jnp.reshape(chip, (1,)).astype(jnp.int32)
        self.nl = len(shards["w_in"])
        self.gathers, self.scatters, self.pairs = {}, {}, {}
        self.res = {n: None for n in BIG}

    def _start_gather(self, group, tag, l, after=()):
        srcs = [self.shards[n][l] for n in group]
        lands = [lax.dynamic_update_slice(lax.empty((4,) + s.shape, s.dtype), s[None], (self.chip, 0, 0)) for s in srcs]
        plan = _plan_gather(len(srcs))
        st = _exchange_start(f"gather_{tag}{l}_start", 3 * len(srcs), plan, srcs + lands, after)
        self.gathers[tag, l] = (plan, st)
        return st[3]

    def _wait_gather(self, tag, l, after):
        plan, st = self.gathers.pop((tag, l))
        n = len(st[2]) // 2
        bufs = _exchange_wait(f"gather_{tag}{l}_wait", 3 * n, plan, st, after)
        return [b.reshape(4 * b.shape[1], b.shape[2]) for b in bufs[n:]]

    def begin(self):
        return [self._start_gather(MIXER, "mixer", 0), self._start_gather(MLP, "mlp", 0)]

    def mixer_weights(self, l, after):
        w = self._wait_gather("mixer", l, after)
        toks = []
        if l + 1 < self.nl:
            toks = [self._start_gather(MIXER, "mixer", l + 1, w[:1]), self._start_gather(MLP, "mlp", l + 1, w[:1])]
        return w, toks

    def mlp_weights(self, l, after):
        return self._wait_gather("mlp", l, after)

    def _start_scatter(self, tag, l, group, pieces):
        plan = _plan_scatter(len(pieces))
        st = _exchange_start(f"scatter_{tag}{l}_start", 3 * len(pieces), plan,
                             list(pieces) + [lax.empty(p.shape, p.dtype) for p in pieces])
        self.scatters[tag] = (l, group, plan, st)
        return [st[3]]

    def _finish_scatter(self, tag, after):
        l, group, plan, st = self.scatters.pop(tag)
        n = len(group)
        bufs = _exchange_wait(f"scatter_{tag}{l}_wait", 3 * n, plan, st, after)
        sums = [_sum_pieces(bufs[k], bufs[n + k], self.chip_arr, "sum_" + group[k]) for k in range(n)]
        plan2 = _plan_pair(n)
        st2 = _exchange_start(f"pair_{tag}{l}_start", n, plan2, sums + [lax.empty(s.shape, s.dtype) for s in sums])
        self.pairs[tag] = (l, group, plan2, st2)
        return [st2[3]]

    def _finish_pair(self, tag, after):
        l, group, plan, st = self.pairs.pop(tag)
        n = len(group)
        bufs = _exchange_wait(f"pair_{tag}{l}_wait", n, plan, st, after)
        for k, name in enumerate(group):
            self.res[name] = _adamw_layer([bufs[k], bufs[n + k]], self.wts[name], self.mom[name], self.var[name],
                                          l, self.res[name], "adamw_" + name)

    def after_mlp_bwd(self, l, after):
        toks = self._finish_scatter("mixer", after) if "mixer" in self.scatters else []
        if "mlp" in self.pairs:
            self._finish_pair("mlp", after)
        return toks

    def mlp_grads(self, l, pieces):
        return self._start_scatter("mlp", l, MLP, pieces)

    def after_in_proj_bwd(self, l, after):
        toks = self._finish_scatter("mlp", after)
        if "mixer" in self.pairs:
            self._finish_pair("mixer", after)
        return toks

    def mixer_grads(self, l, pieces):
        return self._start_scatter("mixer", l, MIXER, pieces)

    def finish_mixer_scatter(self, after):
        return self._finish_scatter("mixer", after)

    def finish_mlp(self, after):
        self._finish_pair("mlp", after)

    def finish_mixer(self, after):
        self._finish_pair("mixer", after)

    def results(self):
        return self.res


def kernel(x, c, w_ada, b_ada, pre_mix_g, w_in, attn_sinks, lam_re, lam_im, log_dt, b_re, b_im, c_re, c_im, d_skip, w_glu, b_glu, attn_out_g, ssm_out_g, w_out, post_mix_g, pre_mlp_g, w_mlp_in, w_mlp_out, post_mlp_g, loss_target, m_w_ada, m_b_ada, m_pre_mix_g, m_w_in, m_attn_sinks, m_lam_re, m_lam_im, m_log_dt, m_b_re, m_b_im, m_c_re, m_c_im, m_d_skip, m_w_glu, m_b_glu, m_attn_out_g, m_ssm_out_g, m_w_out, m_post_mix_g, m_pre_mlp_g, m_w_mlp_in, m_w_mlp_out, m_post_mlp_g, v_w_ada, v_b_ada, v_pre_mix_g, v_w_in, v_attn_sinks, v_lam_re, v_lam_im, v_log_dt, v_b_re, v_b_im, v_c_re, v_c_im, v_d_skip, v_w_glu, v_b_glu, v_attn_out_g, v_ssm_out_g, v_w_out, v_post_mix_g, v_pre_mlp_g, v_w_mlp_in, v_w_mlp_out, v_post_mlp_g):
    args = locals()
    wts = {n: args[n] for n in WEIGHTS}
    mom = {n: args["m_" + n] for n in WEIGHTS}
    var = {n: args["v_" + n] for n in WEIGHTS}
    nl = w_in.shape[0]
    ix, iy, ic = lax.axis_index("x"), lax.axis_index("y"), lax.axis_index("c")
    chip = 2 * ix + iy
    me = 4 * ix + 2 * iy + ic
    wcols = w_ada.shape[2]

    shards = {"w_in": [w_in[l].astype(BF16).T for l in range(nl)], "w_glu": [w_glu[l].astype(BF16) for l in range(nl)],
              "w_out": [w_out[l].astype(BF16) for l in range(nl)],
              "w_mlp_in": [w_mlp_in[l].astype(BF16).T for l in range(nl)],
              "w_mlp_out": [w_mlp_out[l].astype(BF16) for l in range(nl)]}
    comm = _Exchanges(shards, wts, mom, var, chip)
    c = _after(c, *comm.begin())

    c_all = _gather([c.reshape(1, 1, 1, D_MODEL)], "all", "gather_c")[0].reshape(8, D_MODEL)
    b_sh = lax.dynamic_slice(b_ada, (0, chip * wcols), (nl, wcols)).reshape(nl, 1, wcols)
    mod_sh = _ada_forward(c_all, w_ada, b_sh)
    mod_all = _gather([mod_sh.reshape(1, 1, nl * 8, wcols)], "chips", "gather_mod")[0].reshape(4, nl, 8, wcols)
    mod = lax.dynamic_index_in_dim(mod_all, me, axis=2, keepdims=False)
    mod = mod.transpose(1, 0, 2).reshape(nl, 4 * wcols)

    small_p = {n: wts[n] for n in SMALL}
    loss, grad_x, small, dmod = _local_step(x[0], loss_target[0], mod, small_p, comm)
    loss = lax.psum(loss, ("x", "y", "c"))

    packed = _pack(small)
    rows = packed.shape[0]
    pair_plan = _plan_pair(1)
    pair_small = _exchange_start("pair_small_start", 1, pair_plan, [packed, lax.empty((rows, LANE), F32)])

    dmod = _after(dmod, pair_small[3])
    dmod_all = _gather([dmod.reshape(1, 1, nl, N_MOD * D_MODEL)], "all", "gather_dmod")[0][0]
    dmod_sh = lax.dynamic_slice(dmod_all, (0, 0, chip * wcols), (8, nl, wcols)).transpose(1, 0, 2)
    g_ada = _ada_weight_grad(c_all.T, dmod_sh)
    res = {"w_ada": _adamw(g_ada[:, None], w_ada, m_w_ada, v_w_ada, "adamw_w_ada")}

    comm.finish_mlp(res["w_ada"][0])
    toks = comm.finish_mixer_scatter(res["w_ada"][0])

    own, other = _exchange_wait("pair_small_wait", 1, pair_plan, pair_small, res["w_ada"][0])
    chip_sum = _after(_sum_list([own, other], "sum_pair_small"), *toks)
    quad_plan = _plan_gather(1)
    quad0 = lax.dynamic_update_slice(lax.empty((4, rows, LANE), F32), chip_sum[None], (chip, 0, 0))
    quad_small = _exchange_start("gather_small_start", 3, quad_plan, [chip_sum, quad0])
    comm.finish_mixer([comm.results()[n][0] for n in MLP] + [_after(dmod, quad_small[3])])
    res.update(comm.results())
    quad = _exchange_wait("gather_small_wait", 3, quad_plan, quad_small, [res[n][0] for n in BIG])[1]
    outs = _adamw(quad[None], _pack({n: wts[n] for n in SMALL})[None], _pack({n: mom[n] for n in SMALL})[None],
                  _pack({n: var[n] for n in SMALL})[None], "adamw_small")
    shapes = {n: wts[n].shape for n in SMALL}
    unpacked = [_unpack(o[0], shapes) for o in outs]
    for n in SMALL:
        res[n] = [u[n] for u in unpacked]

    return (loss, grad_x[None], *[res[n][0] for n in WEIGHTS], *[res[n][1] for n in WEIGHTS],
            *[res[n][2] for n in WEIGHTS], *[res[n][3] for n in WEIGHTS])
```

```python
import functools
import math

import numpy as np
import jax
import jax.numpy as jnp
from jax import lax
from jax.experimental import pallas as pl
from jax.experimental.pallas import tpu as pltpu

F32 = jnp.float32
BF16 = jnp.bfloat16

D_MODEL = 1024
ATTN_W = 512
SSM_W = 512
HEAD_DIM = 64
N_Q = 8
N_KV = 2
Q_PER_KV = 4
KV_W = 128
WINDOW = 128
BLOCK = 128
N_GROUPS = 32
GROUP_W = 16
STATE = 64
N_CH = N_GROUPS * STATE
HALF_CH = N_CH // 2
D_FF = 4096
IN_W = 1280
N_MOD = 6
EPS = 1e-6
NEG_INF = -1e30

ADAM_LR = 0.001
ADAM_B1 = 0.9
ADAM_B2 = 0.999
ADAM_EPS = 1e-08
ADAM_WD = 0.01
ADAM_STEP = 10

ROW_TILE = 256
CHUNK = 256
SEGS = 8
STEPS = CHUNK // SEGS
STRIP = 1024
VMEM_LIMIT_V7X = 56 * 1024 * 1024
LANE = 128
SUBLANE = 8

GELU_K0 = math.sqrt(2.0 / math.pi)
GELU_K1 = 0.044715

V_SH1, V_SC1, V_G1, V_SH2, V_SC2, V_G2, V_PRE_MIX, V_POST_MIX, V_PRE_MLP, V_POST_MLP = range(10)
H_ATTN_G, H_SSM_G, H_DSKIP, H_BGLU = range(4)

HBM = pl.BlockSpec(memory_space=pltpu.HBM)
SEM = pl.BlockSpec(memory_space=pltpu.SEMAPHORE)
EFFECT = pltpu.SideEffectType.DATAFLOW_SIDE_EFFECTING
MESH_ID = pl.DeviceIdType.MESH


def _nn(a, b):
    return lax.dot_general(a, b, (((1,), (0,)), ((), ())), preferred_element_type=F32)


def _nt(a, b):
    return lax.dot_general(a, b, (((1,), (1,)), ((), ())), preferred_element_type=F32)


def _tn(a, b):
    return lax.dot_general(a, b, (((0,), (0,)), ((), ())), preferred_element_type=F32)


def _params(sem):
    return pltpu.CompilerParams(dimension_semantics=sem, vmem_limit_bytes=VMEM_LIMIT_V7X)


def _rms_fwd(x, g):
    r = lax.rsqrt(jnp.mean(x * x, axis=-1, keepdims=True) + EPS)
    xh = x * r
    return xh * g, xh, r


def _rms_bwd(dy, xh, r, g):
    dxh = dy * g
    dx = r * (dxh - xh * jnp.mean(dxh * xh, axis=-1, keepdims=True))
    return dx, dy * xh


def _colsum(t):
    return jnp.sum(t, axis=0, keepdims=True)


def _gelu(y):
    t = jnp.tanh(GELU_K0 * (y + GELU_K1 * (y * y * y)))
    return 0.5 * y * (1.0 + t), t


def _gelu_grad(y, t):
    return 0.5 * (1.0 + t) + 0.5 * y * (1.0 - t * t) * GELU_K0 * (1.0 + 3.0 * GELU_K1 * y * y)


def _alibi_slopes():
    return [float(s) for s in 2.0 ** (-8.0 * np.arange(1, N_Q + 1) / N_Q)]


def _pick_rows(rows, bytes_per_row, budget):
    t = rows
    while t % (2 * SUBLANE) == 0 and t * bytes_per_row > budget:
        t //= 2
    return t


def _load_once(step, pairs, sems):
    @pl.when(step == 0)
    def _():
        cps = [pltpu.make_async_copy(src, dst, sems.at[k]) for k, (src, dst) in enumerate(pairs)]
        for cp in cps:
            cp.start()
        for cp in cps:
            cp.wait()


_GROUPS = {
    "all": ([(0, 0, 1), (0, 1, 0), (0, 1, 1), (1, 0, 0), (1, 0, 1), (1, 1, 0), (1, 1, 1)], (4, 2, 1), 8),
    "chips": ([(1, 0, 0), (0, 1, 0), (1, 1, 0)], (2, 1, 0), 4),
    "pair": ([(0, 0, 1)], (0, 0, 1), 2),
}


def _flip(v, f):
    return 1 - v if f else v


def _gather(arrs, kind, name):
    masks, wts, n = _GROUPS[kind]
    na, nm = len(arrs), len(masks)

    def body(*refs):
        ins, outs = refs[:na], refs[na:2 * na]
        ssem, rsem, lsem = refs[2 * na:]
        x, y, c = lax.axis_index("x"), lax.axis_index("y"), lax.axis_index("c")
        me = wts[0] * x + wts[1] * y + wts[2] * c
        local = [pltpu.make_async_copy(ins[k], outs[k].at[:, pl.ds(me, 1)], lsem.at[k]) for k in range(na)]
        for cp in local:
            cp.start()
        remote = []
        for k in range(na):
            for mi, (fx, fy, fc) in enumerate(masks):
                peer = (_flip(x, fx), _flip(y, fy), _flip(c, fc))
                remote.append(pltpu.make_async_remote_copy(
                    src_ref=ins[k], dst_ref=outs[k].at[:, pl.ds(me, 1)],
                    send_sem=ssem.at[k * nm + mi], recv_sem=rsem.at[k * nm + mi],
                    device_id=peer, device_id_type=MESH_ID))
        for cp in remote:
            cp.start()
        for cp in remote:
            cp.wait()
        for cp in local:
            cp.wait()

    outs = pl.pallas_call(
        body, name=name,
        out_shape=[jax.ShapeDtypeStruct((a.shape[0], n) + a.shape[2:], a.dtype) for a in arrs],
        in_specs=[HBM] * na, out_specs=[HBM] * na,
        scratch_shapes=[pltpu.SemaphoreType.DMA((na * nm,)), pltpu.SemaphoreType.DMA((na * nm,)),
                        pltpu.SemaphoreType.DMA((na,))],
    )(*arrs)
    return list(outs)


def _hbm(a):
    return pltpu.with_memory_space_constraint(a, pltpu.HBM)


def _after(x, *tokens):
    for t in tokens:
        x = x + t[0, 0].astype(x.dtype)
    return x


def _exchange_start(name, n_copies, plan, bufs, after=()):
    n, na = len(bufs), len(after)

    def body(*refs):
        ssem, rsem, token = refs[n + na], refs[n + na + 1], refs[2 * n + na + 2]
        for k, (src, dst, dev) in enumerate(plan(refs[:n])):
            pltpu.make_async_remote_copy(src_ref=src, dst_ref=dst, send_sem=ssem.at[k], recv_sem=rsem.at[k],
                                         device_id=dev, device_id_type=MESH_ID).start()
        token[...] = jnp.zeros_like(token)

    outs = pl.pallas_call(
        body, name=name,
        out_shape=(pltpu.SemaphoreType.DMA((n_copies,)), pltpu.SemaphoreType.DMA((n_copies,)),
                   *[pltpu.HBM(b.shape, b.dtype) for b in bufs], jax.ShapeDtypeStruct((SUBLANE, LANE), F32)),
        in_specs=[HBM] * n + [pl.BlockSpec(memory_space=pl.ANY)] * na,
        out_specs=(SEM, SEM, *[HBM] * n, pl.BlockSpec(memory_space=pltpu.VMEM)),
        input_output_aliases={i: 2 + i for i in range(n)},
        compiler_params=pltpu.CompilerParams(has_side_effects=EFFECT),
    )(*[_hbm(b) for b in bufs], *after)
    return outs[0], outs[1], list(outs[2:2 + n]), outs[2 + n]


def _exchange_wait(name, n_copies, plan, started, after):
    ssem, rsem, bufs, _ = started
    n = len(bufs)
    after = list(after) if isinstance(after, (list, tuple)) else [after]

    def body(*refs):
        ssem_ref, rsem_ref = refs[n], refs[n + 1]
        for k, (src, dst, dev) in enumerate(plan(refs[:n])):
            cp = pltpu.make_async_remote_copy(src_ref=src, dst_ref=dst, send_sem=ssem_ref.at[k], recv_sem=rsem_ref.at[k],
                                              device_id=dev, device_id_type=MESH_ID)
            cp.wait_send()
            cp.wait_recv()

    outs = pl.pallas_call(
        body, name=name,
        out_shape=tuple(pltpu.HBM(b.shape, b.dtype) for b in bufs),
        in_specs=[HBM] * n + [SEM, SEM] + [pl.BlockSpec(memory_space=pl.ANY)] * len(after), out_specs=tuple([HBM] * n),
        input_output_aliases={i: i for i in range(n)},
        compiler_params=pltpu.CompilerParams(has_side_effects=EFFECT),
    )(*bufs, ssem, rsem, *after)
    return list(outs)


def _position():
    x, y, c = lax.axis_index("x"), lax.axis_index("y"), lax.axis_index("c")
    return x, y, c, [(1 - x, y), (x, 1 - y), (1 - x, 1 - y)]


def _plan_gather(na):
    def plan(refs):
        x, y, c, chips = _position()
        return [(refs[k], refs[na + k].at[2 * x + y], (px, py, c)) for k in range(na) for px, py in chips]
    return plan


def _plan_scatter(na):
    def plan(refs):
        x, y, c, chips = _position()
        return [(refs[k].at[2 * px + py], refs[na + k].at[2 * x + y], (px, py, c))
                for k in range(na) for px, py in chips]
    return plan


def _plan_pair(na):
    def plan(refs):
        x, y, c, _ = _position()
        return [(refs[k], refs[na + k], (x, y, 1 - c)) for k in range(na)]
    return plan


def _sum_list(arrs, name):
    n = len(arrs)
    r, c = arrs[0].shape
    tr = _pick_rows(r, c * 4 * (n + 1), 4 << 20)

    def body(*refs):
        acc = refs[0][...].astype(F32)
        for j in range(1, n):
            acc = acc + refs[j][...].astype(F32)
        refs[n][...] = acc

    blk = pl.BlockSpec((tr, c), lambda i: (i, 0))
    return pl.pallas_call(
        body, name=name, grid=(r // tr,), in_specs=[blk] * n, out_specs=blk,
        out_shape=jax.ShapeDtypeStruct((r, c), F32), compiler_params=_params(("parallel",)),
    )(*arrs)


def _sum_pieces(own, recv, chip, name):
    _, r, c = own.shape
    tr = _pick_rows(r, c * 2 * 6, 4 << 20)

    def body(chip_ref, own_ref, recv_ref, o_ref):
        acc = own_ref[0].astype(F32)
        for j in range(4):
            acc = acc + jnp.where(chip_ref[0] == j, 0.0, recv_ref[j].astype(F32))
        o_ref[...] = acc.astype(BF16)

    return pl.pallas_call(
        body, name=name,
        grid_spec=pltpu.PrefetchScalarGridSpec(
            num_scalar_prefetch=1, grid=(r // tr,),
            in_specs=[pl.BlockSpec((1, tr, c), lambda i, chip_ref: (chip_ref[0], i, 0)),
                      pl.BlockSpec((4, tr, c), lambda i, chip_ref: (0, i, 0))],
            out_specs=pl.BlockSpec((tr, c), lambda i, chip_ref: (i, 0))),
        out_shape=jax.ShapeDtypeStruct((r, c), BF16), compiler_params=_params(("parallel",)),
    )(chip, own, recv)


def _adam_update(g, w, m, v):
    mn = ADAM_B1 * m + (1.0 - ADAM_B1) * g
    vn = ADAM_B2 * v + (1.0 - ADAM_B2) * jnp.square(g)
    m_hat = mn / (1.0 - ADAM_B1 ** ADAM_STEP)
    v_hat = vn / (1.0 - ADAM_B2 ** ADAM_STEP)
    return -ADAM_LR * (m_hat / (jnp.sqrt(v_hat) + ADAM_EPS) + ADAM_WD * w), mn, vn


def _adamw_layer(grads, w, m, v, layer, prev, name):
    ng = len(grads)
    nl, r, c = w.shape
    tr = _pick_rows(r, c * 4 * (ng + 7), 6 << 20)
    if prev is None:
        prev = [lax.empty((nl, r, c), F32) for _ in range(4)]

    def body(*refs):
        g = refs[0][...].astype(F32)
        for j in range(1, ng):
            g = g + refs[j][...].astype(F32)
        w_ref, m_ref, v_ref = refs[ng:ng + 3]
        go_ref, d_ref, mo_ref, vo_ref = refs[ng + 7:ng + 11]
        d, mn, vn = _adam_update(g, w_ref[0], m_ref[0], v_ref[0])
        go_ref[0] = g
        d_ref[0] = d
        mo_ref[0] = mn
        vo_ref[0] = vn

    gblk = pl.BlockSpec((tr, c), lambda i: (i, 0))
    blk = pl.BlockSpec((1, tr, c), lambda i: (layer, i, 0))
    keep = pl.BlockSpec(memory_space=pl.ANY)
    sds = jax.ShapeDtypeStruct((nl, r, c), F32)
    return pl.pallas_call(
        body, name=name, grid=(r // tr,),
        in_specs=[gblk] * ng + [blk] * 3 + [keep] * 4,
        out_specs=[blk] * 4, out_shape=[sds] * 4,
        input_output_aliases={ng + 3 + i: i for i in range(4)},
        compiler_params=_params(("parallel",)),
    )(*grads, w, m, v, *prev)


def _adamw(gs, w, m, v, name):
    a, s, r, c = gs.shape
    tr = _pick_rows(r, c * 4 * (s + 7), 6 << 20)

    def body(g_ref, w_ref, m_ref, v_ref, go_ref, d_ref, mo_ref, vo_ref):
        g = g_ref[0, 0].astype(F32)
        for j in range(1, s):
            g = g + g_ref[0, j].astype(F32)
        d, mn, vn = _adam_update(g, w_ref[0], m_ref[0], v_ref[0])
        go_ref[0] = g
        d_ref[0] = d
        mo_ref[0] = mn
        vo_ref[0] = vn

    blk = pl.BlockSpec((1, tr, c), lambda i, j: (i, j, 0))
    sds = jax.ShapeDtypeStruct((a, r, c), F32)
    return pl.pallas_call(
        body, name=name, grid=(a, r // tr),
        in_specs=[pl.BlockSpec((1, s, tr, c), lambda i, j: (i, 0, j, 0)), blk, blk, blk],
        out_specs=[blk, blk, blk, blk], out_shape=[sds, sds, sds, sds],
        compiler_params=_params(("parallel", "parallel")),
    )(gs, w, m, v)


def _ada_forward(c_all, w_ada, b_sh):
    nl, d, w = w_ada.shape
    tw = 512

    def body(c_ref, w_ref, b_ref, o_ref):
        cv = c_ref[...]
        act = (cv * jax.nn.sigmoid(cv)).astype(BF16)
        o_ref[0] = _nn(act, w_ref[0].astype(BF16)) + b_ref[0]

    return pl.pallas_call(
        body, name="ada_forward", grid=(nl, w // tw),
        in_specs=[pl.BlockSpec((8, d), lambda l, j: (0, 0)),
                  pl.BlockSpec((1, d, tw), lambda l, j: (l, 0, j)),
                  pl.BlockSpec((1, 1, tw), lambda l, j: (l, 0, j))],
        out_specs=pl.BlockSpec((1, 8, tw), lambda l, j: (l, 0, j)),
        out_shape=jax.ShapeDtypeStruct((nl, 8, w), F32),
        compiler_params=_params(("parallel", "parallel")),
    )(c_all, w_ada, b_sh)


def _ada_weight_grad(c_all_t, dmod):
    nl, nb, w = dmod.shape
    d = c_all_t.shape[0]
    tw = 512

    def body(c_ref, g_ref, o_ref):
        cv = c_ref[...]
        act = cv * jax.nn.sigmoid(cv)
        gv = g_ref[0]
        acc = act[:, 0:1] * gv[0:1, :]
        for b in range(1, nb):
            acc = acc + act[:, b:b + 1] * gv[b:b + 1, :]
        o_ref[0] = acc

    return pl.pallas_call(
        body, name="ada_weight_grad", grid=(nl, w // tw),
        in_specs=[pl.BlockSpec((d, nb), lambda l, j: (0, 0)),
                  pl.BlockSpec((1, nb, tw), lambda l, j: (l, 0, j))],
        out_specs=pl.BlockSpec((1, d, tw), lambda l, j: (l, 0, j)),
        out_shape=jax.ShapeDtypeStruct((nl, d, w), F32),
        compiler_params=_params(("parallel", "parallel")),
    )(c_all_t, dmod)


def _in_proj_fwd(x, vec, w_in_t, layer):
    seq = x.shape[0]
    tm = ROW_TILE

    def body(x_ref, vec_ref, w_ref, q_ref, kv_ref, u4_ref, h_ref):
        n, _, _ = _rms_fwd(x_ref[...], vec_ref[0, V_PRE_MIX:V_PRE_MIX + 1, :])
        h = (n * (1.0 + vec_ref[0, V_SC1:V_SC1 + 1, :]) + vec_ref[0, V_SH1:V_SH1 + 1, :]).astype(BF16)
        h_ref[...] = h
        proj = _nt(h, w_ref[...])
        q_ref[...] = proj[:, :ATTN_W].astype(BF16)
        kv_ref[...] = proj[:, ATTN_W:ATTN_W + 2 * KV_W].astype(BF16)
        u0 = ATTN_W + 2 * KV_W
        for j in range(4):
            u4_ref[j] = proj[:, u0 + j * LANE:u0 + (j + 1) * LANE]

    return pl.pallas_call(
        body, name="in_proj_fwd", grid=(seq // tm,),
        in_specs=[pl.BlockSpec((tm, D_MODEL), lambda i: (i, 0)),
                  pl.BlockSpec((1, 16, D_MODEL), lambda i: (layer, 0, 0)),
                  pl.BlockSpec((IN_W, D_MODEL), lambda i: (0, 0))],
        out_specs=[pl.BlockSpec((tm, ATTN_W), lambda i: (i, 0)),
                   pl.BlockSpec((tm, 2 * KV_W), lambda i: (i, 0)),
                   pl.BlockSpec((4, tm, LANE), lambda i: (0, i, 0)),
                   pl.BlockSpec((tm, D_MODEL), lambda i: (i, 0))],
        out_shape=[jax.ShapeDtypeStruct((seq, ATTN_W), BF16), jax.ShapeDtypeStruct((seq, 2 * KV_W), BF16),
                   jax.ShapeDtypeStruct((4, seq, LANE), F32), jax.ShapeDtypeStruct((seq, D_MODEL), BF16)],
        compiler_params=_params(("parallel",)),
    )(x, vec, w_in_t)


def _in_proj_bwd(dx1, dq, dkv, du4, x, vec, w_in_t, layer):
    seq = x.shape[0]
    tm = ROW_TILE

    def body(dx1_ref, dq_ref, dkv_ref, du4_ref, x_ref, vec_ref, w_ref, dx_ref, dp_ref, dvec_ref):
        i = pl.program_id(0)

        @pl.when(i == 0)
        def _():
            dvec_ref[...] = jnp.zeros_like(dvec_ref)

        dproj = jnp.concatenate([dq_ref[...], dkv_ref[...]] + [du4_ref[j] for j in range(4)], axis=1).astype(BF16)
        dp_ref[...] = dproj
        dh = _nn(dproj, w_ref[...])
        g = vec_ref[0, V_PRE_MIX:V_PRE_MIX + 1, :]
        n, xh, r = _rms_fwd(x_ref[...], g)
        dn = dh * (1.0 + vec_ref[0, V_SC1:V_SC1 + 1, :])
        dxn, dg_rows = _rms_bwd(dn, xh, r, g)
        dx_ref[...] = dx1_ref[...] + dxn
        dvec_ref[V_SH1:V_SH1 + 1, :] += _colsum(dh)
        dvec_ref[V_SC1:V_SC1 + 1, :] += _colsum(dh * n)
        dvec_ref[V_PRE_MIX:V_PRE_MIX + 1, :] += _colsum(dg_rows)

    row = pl.BlockSpec((tm, D_MODEL), lambda i: (i, 0))
    return pl.pallas_call(
        body, name="in_proj_bwd", grid=(seq // tm,),
        in_specs=[row, pl.BlockSpec((tm, ATTN_W), lambda i: (i, 0)), pl.BlockSpec((tm, 2 * KV_W), lambda i: (i, 0)),
                  pl.BlockSpec((4, tm, LANE), lambda i: (0, i, 0)), row,
                  pl.BlockSpec((1, 16, D_MODEL), lambda i: (layer, 0, 0)),
                  pl.BlockSpec((IN_W, D_MODEL), lambda i: (0, 0))],
        out_specs=[row, pl.BlockSpec((tm, IN_W), lambda i: (i, 0)), pl.BlockSpec((16, D_MODEL), lambda i: (0, 0))],
        out_shape=[jax.ShapeDtypeStruct((seq, D_MODEL), F32), jax.ShapeDtypeStruct((seq, IN_W), BF16),
                   jax.ShapeDtypeStruct((16, D_MODEL), F32)],
        compiler_params=_params(("arbitrary",)),
    )(dx1, dq, dkv, du4, x, vec, w_in_t)


def _heads(attn_ref, s4_ref, vec512_ref):
    ga = vec512_ref[0, H_ATTN_G:H_ATTN_G + 1, :]
    gs = vec512_ref[0, H_SSM_G:H_SSM_G + 1, :]
    sv = jnp.concatenate([s4_ref[j] for j in range(4)], axis=1)
    na, ah, ar = _rms_fwd(attn_ref[...], ga)
    ns, sh, sr = _rms_fwd(sv, gs)
    return jnp.concatenate([na, ns], axis=1), (ah, ar, ga), (sh, sr, gs)


def _out_proj_fwd(x, attn, s4, vec, vec512, w_out, layer):
    seq = x.shape[0]
    tm = ROW_TILE

    def body(x_ref, attn_ref, s4_ref, vec_ref, vec512_ref, w_ref, x1_ref):
        heads, _, _ = _heads(attn_ref, s4_ref, vec512_ref)
        mixed = _nn(heads.astype(BF16), w_ref[...])
        nm, _, _ = _rms_fwd(mixed, vec_ref[0, V_POST_MIX:V_POST_MIX + 1, :])
        x1_ref[...] = x_ref[...] + vec_ref[0, V_G1:V_G1 + 1, :] * nm

    row = pl.BlockSpec((tm, D_MODEL), lambda i: (i, 0))
    return pl.pallas_call(
        body, name="out_proj_fwd", grid=(seq // tm,),
        in_specs=[row, pl.BlockSpec((tm, ATTN_W), lambda i: (i, 0)), pl.BlockSpec((4, tm, LANE), lambda i: (0, i, 0)),
                  pl.BlockSpec((1, 16, D_MODEL), lambda i: (layer, 0, 0)),
                  pl.BlockSpec((1, 8, SSM_W), lambda i: (layer, 0, 0)),
                  pl.BlockSpec((D_MODEL, D_MODEL), lambda i: (0, 0))],
        out_specs=row, out_shape=jax.ShapeDtypeStruct((seq, D_MODEL), F32),
        compiler_params=_params(("parallel",)),
    )(x, attn, s4, vec, vec512, w_out)


def _out_proj_bwd(dx1, attn, s4, vec, vec512, w_out, layer):
    seq = dx1.shape[0]
    tm = ROW_TILE

    def body(dx1_ref, attn_ref, s4_ref, vec_ref, vec512_ref, w_ref,
             dattn_ref, ds4_ref, heads_ref, dmixed_ref, dvec_ref, dvec512_ref):
        i = pl.program_id(0)

        @pl.when(i == 0)
        def _():
            dvec_ref[...] = jnp.zeros_like(dvec_ref)
            dvec512_ref[...] = jnp.zeros_like(dvec512_ref)

        heads, (ah, ar, ga), (sh, sr, gs) = _heads(attn_ref, s4_ref, vec512_ref)
        hb = heads.astype(BF16)
        heads_ref[...] = hb
        gm = vec_ref[0, V_POST_MIX:V_POST_MIX + 1, :]
        nm, mh, mr = _rms_fwd(_nn(hb, w_ref[...]), gm)
        dx1v = dx1_ref[...]
        dvec_ref[V_G1:V_G1 + 1, :] += _colsum(dx1v * nm)
        dmixed, dgm_rows = _rms_bwd(dx1v * vec_ref[0, V_G1:V_G1 + 1, :], mh, mr, gm)
        dvec_ref[V_POST_MIX:V_POST_MIX + 1, :] += _colsum(dgm_rows)
        dmb = dmixed.astype(BF16)
        dmixed_ref[...] = dmb
        dheads = _nt(dmb, w_ref[...])
        dattn, dga_rows = _rms_bwd(dheads[:, :ATTN_W], ah, ar, ga)
        ds, dgs_rows = _rms_bwd(dheads[:, ATTN_W:], sh, sr, gs)
        dattn_ref[...] = dattn
        for j in range(4):
            ds4_ref[j] = ds[:, j * LANE:(j + 1) * LANE]
        dvec512_ref[H_ATTN_G:H_ATTN_G + 1, :] += _colsum(dga_rows)
        dvec512_ref[H_SSM_G:H_SSM_G + 1, :] += _colsum(dgs_rows)

    row = pl.BlockSpec((tm, D_MODEL), lambda i: (i, 0))
    return pl.pallas_call(
        body, name="out_proj_bwd", grid=(seq // tm,),
        in_specs=[row, pl.BlockSpec((tm, ATTN_W), lambda i: (i, 0)), pl.BlockSpec((4, tm, LANE), lambda i: (0, i, 0)),
                  pl.BlockSpec((1, 16, D_MODEL), lambda i: (layer, 0, 0)),
                  pl.BlockSpec((1, 8, SSM_W), lambda i: (layer, 0, 0)),
                  pl.BlockSpec((D_MODEL, D_MODEL), lambda i: (0, 0))],
        out_specs=[pl.BlockSpec((tm, ATTN_W), lambda i: (i, 0)), pl.BlockSpec((4, tm, LANE), lambda i: (0, i, 0)),
                   row, row, pl.BlockSpec((16, D_MODEL), lambda i: (0, 0)), pl.BlockSpec((8, SSM_W), lambda i: (0, 0))],
        out_shape=[jax.ShapeDtypeStruct((seq, ATTN_W), F32), jax.ShapeDtypeStruct((4, seq, LANE), F32),
                   jax.ShapeDtypeStruct((seq, D_MODEL), BF16), jax.ShapeDtypeStruct((seq, D_MODEL), BF16),
                   jax.ShapeDtypeStruct((16, D_MODEL), F32), jax.ShapeDtypeStruct((8, SSM_W), F32)],
        compiler_params=_params(("arbitrary",)),
    )(dx1, attn, s4, vec, vec512, w_out)


def _mlp_fwd(x1, vec, w_in_t, w_out, layer):
    seq = x1.shape[0]
    tm = ROW_TILE

    def body(x1_ref, vec_ref, wi_hbm, wo_hbm, x2_ref, wi, wo, sems):
        _load_once(pl.program_id(0), [(wi_hbm, wi), (wo_hbm, wo)], sems)
        x1v = x1_ref[...]
        n, _, _ = _rms_fwd(x1v, vec_ref[0, V_PRE_MLP:V_PRE_MLP + 1, :])
        h = (n * (1.0 + vec_ref[0, V_SC2:V_SC2 + 1, :]) + vec_ref[0, V_SH2:V_SH2 + 1, :]).astype(BF16)
        a = _nt(h, wi[...])
        r = jnp.square(jnp.maximum(a, 0.0)).astype(BF16)
        nf, _, _ = _rms_fwd(_nn(r, wo[...]), vec_ref[0, V_POST_MLP:V_POST_MLP + 1, :])
        x2_ref[...] = x1v + vec_ref[0, V_G2:V_G2 + 1, :] * nf

    row = pl.BlockSpec((tm, D_MODEL), lambda i: (i, 0))
    return pl.pallas_call(
        body, name="mlp_fwd", grid=(seq // tm,),
        in_specs=[row, pl.BlockSpec((1, 16, D_MODEL), lambda i: (layer, 0, 0)), HBM, HBM],
        out_specs=row, out_shape=jax.ShapeDtypeStruct((seq, D_MODEL), F32),
        scratch_shapes=[pltpu.VMEM((D_FF, D_MODEL), BF16), pltpu.VMEM((D_FF, D_MODEL), BF16),
                        pltpu.SemaphoreType.DMA((2,))],
        compiler_params=_params(("arbitrary",)),
    )(x1, vec, w_in_t, w_out)


def _mlp_bwd(dx2, x1, vec, w_in_t, w_out, layer):
    seq = x1.shape[0]
    tm = ROW_TILE

    def body(dx2_ref, x1_ref, vec_ref, wi_hbm, wo_hbm, dx1_ref, h_ref, da_ref, r_ref, df_ref, dvec_ref, wi, wo, sems):
        i = pl.program_id(0)
        _load_once(i, [(wi_hbm, wi), (wo_hbm, wo)], sems)

        @pl.when(i == 0)
        def _():
            dvec_ref[...] = jnp.zeros_like(dvec_ref)

        g_pre = vec_ref[0, V_PRE_MLP:V_PRE_MLP + 1, :]
        g_post = vec_ref[0, V_POST_MLP:V_POST_MLP + 1, :]
        sc2 = vec_ref[0, V_SC2:V_SC2 + 1, :]
        n, xh, xr = _rms_fwd(x1_ref[...], g_pre)
        h = (n * (1.0 + sc2) + vec_ref[0, V_SH2:V_SH2 + 1, :]).astype(BF16)
        h_ref[...] = h
        a = _nt(h, wi[...])
        relu = jnp.maximum(a, 0.0)
        r = jnp.square(relu).astype(BF16)
        r_ref[...] = r
        nf, fh, fr = _rms_fwd(_nn(r, wo[...]), g_post)
        dx2v = dx2_ref[...]
        dvec_ref[V_G2:V_G2 + 1, :] += _colsum(dx2v * nf)
        df, dgp_rows = _rms_bwd(dx2v * vec_ref[0, V_G2:V_G2 + 1, :], fh, fr, g_post)
        dvec_ref[V_POST_MLP:V_POST_MLP + 1, :] += _colsum(dgp_rows)
        dfb = df.astype(BF16)
        df_ref[...] = dfb
        da = (_nt(dfb, wo[...]) * (2.0 * relu)).astype(BF16)
        da_ref[...] = da
        dh = _nn(da, wi[...])
        dvec_ref[V_SH2:V_SH2 + 1, :] += _colsum(dh)
        dvec_ref[V_SC2:V_SC2 + 1, :] += _colsum(dh * n)
        dxn, dg_rows = _rms_bwd(dh * (1.0 + sc2), xh, xr, g_pre)
        dvec_ref[V_PRE_MLP:V_PRE_MLP + 1, :] += _colsum(dg_rows)
        dx1_ref[...] = dx2v + dxn

    row = pl.BlockSpec((tm, D_MODEL), lambda i: (i, 0))
    wide = pl.BlockSpec((tm, D_FF), lambda i: (i, 0))
    return pl.pallas_call(
        body, name="mlp_bwd", grid=(seq // tm,),
        in_specs=[row, row, pl.BlockSpec((1, 16, D_MODEL), lambda i: (layer, 0, 0)), HBM, HBM],
        out_specs=[row, row, wide, wide, row, pl.BlockSpec((16, D_MODEL), lambda i: (0, 0))],
        out_shape=[jax.ShapeDtypeStruct((seq, D_MODEL), F32), jax.ShapeDtypeStruct((seq, D_MODEL), BF16),
                   jax.ShapeDtypeStruct((seq, D_FF), BF16), jax.ShapeDtypeStruct((seq, D_FF), BF16),
                   jax.ShapeDtypeStruct((seq, D_MODEL), BF16), jax.ShapeDtypeStruct((16, D_MODEL), F32)],
        scratch_shapes=[pltpu.VMEM((D_FF, D_MODEL), BF16), pltpu.VMEM((D_FF, D_MODEL), BF16),
                        pltpu.SemaphoreType.DMA((2,))],
        compiler_params=_params(("arbitrary",)),
    )(dx2, x1, vec, w_in_t, w_out)


def _loss_head(y, target):
    seq = y.shape[0]
    tm = ROW_TILE

    def body(y_ref, t_ref, dy_ref, part_ref):
        e = y_ref[...] - t_ref[...]
        dy_ref[...] = e * (1.0 / D_MODEL)
        tot = jnp.sum(jnp.sum(e * e, axis=1, keepdims=True), axis=0, keepdims=True) * (0.5 / D_MODEL)
        part_ref[0] = jnp.broadcast_to(tot, (SUBLANE, LANE))

    row = pl.BlockSpec((tm, D_MODEL), lambda i: (i, 0))
    return pl.pallas_call(
        body, name="loss_head", grid=(seq // tm,),
        in_specs=[row, row],
        out_specs=[row, pl.BlockSpec((1, SUBLANE, LANE), lambda i: (i, 0, 0))],
        out_shape=[jax.ShapeDtypeStruct((seq, D_MODEL), F32), jax.ShapeDtypeStruct((seq // tm, SUBLANE, LANE), F32)],
        compiler_params=_params(("parallel",)),
    )(y, target)


def _matmul_tn(a, b, out_dtype, name, pieces=1):
    kk, m = a.shape
    n = b.shape[1]
    tm = min(m, 512)
    tn = n // pieces if pieces > 1 else min(n, 1280)
    tk = min(kk, 2048)
    nk = kk // tk

    def body(a_ref, b_ref, o_ref, acc):
        k = pl.program_id(2)

        @pl.when(k == 0)
        def _():
            acc[...] = jnp.zeros_like(acc)

        acc[...] += _tn(a_ref[...], b_ref[...])

        @pl.when(k == nk - 1)
        def _():
            if pieces > 1:
                o_ref[0] = acc[...].astype(out_dtype)
            else:
                o_ref[...] = acc[...].astype(out_dtype)

    if pieces > 1:
        out_spec = pl.BlockSpec((1, tm, tn), lambda i, j, k: (j, i, 0))
        out_shape = jax.ShapeDtypeStruct((pieces, m, tn), out_dtype)
    else:
        out_spec = pl.BlockSpec((tm, tn), lambda i, j, k: (i, j))
        out_shape = jax.ShapeDtypeStruct((m, n), out_dtype)
    return pl.pallas_call(
        body, name=name, grid=(m // tm, n // tn, nk),
        in_specs=[pl.BlockSpec((tk, tm), lambda i, j, k: (k, i)), pl.BlockSpec((tk, tn), lambda i, j, k: (k, j))],
        out_specs=out_spec, out_shape=out_shape,
        scratch_shapes=[pltpu.VMEM((tm, tn), F32)],
        compiler_params=_params(("parallel", "parallel", "arbitrary")),
    )(a, b)


def _attn_probs(i, q_h, kband, slope, sink):
    rr = lax.broadcasted_iota(jnp.int32, (BLOCK, 2 * BLOCK), 0)
    jj = lax.broadcasted_iota(jnp.int32, (BLOCK, 2 * BLOCK), 1)
    diff = BLOCK + rr - jj
    valid = (diff >= 0) & (diff < WINDOW) & ((jj >= BLOCK) | (i > 0))
    s = _nt(q_h, kband) * (HEAD_DIM ** -0.5)
    s = jnp.where(valid, s - slope * diff.astype(F32), NEG_INF)
    m = jnp.maximum(jnp.max(s, axis=1, keepdims=True), sink)
    p = jnp.exp(s - m)
    ps = jnp.exp(sink - m)
    inv = 1.0 / (jnp.sum(p, axis=1, keepdims=True) + ps)
    return p * inv, ps * inv


def _bands(kvp, kvc, h):
    kband = jnp.concatenate([kvp[:, h * HEAD_DIM:(h + 1) * HEAD_DIM], kvc[:, h * HEAD_DIM:(h + 1) * HEAD_DIM]], axis=0)
    v0 = KV_W + h * HEAD_DIM
    vband = jnp.concatenate([kvp[:, v0:v0 + HEAD_DIM], kvc[:, v0:v0 + HEAD_DIM]], axis=0)
    return kband, vband


def _attn_fwd(q, kv, sinks):
    seq = q.shape[0]
    nb = seq // BLOCK
    slopes = _alibi_slopes()

    def body(sink_ref, q_ref, kvp_ref, kvc_ref, o_ref):
        i = pl.program_id(0)
        qv, kvp, kvc = q_ref[...], kvp_ref[...], kvc_ref[...]
        for h in range(N_KV):
            kband, vband = _bands(kvp, kvc, h)
            for g in range(Q_PER_KV):
                hq = h * Q_PER_KV + g
                cols = slice(hq * HEAD_DIM, (hq + 1) * HEAD_DIM)
                pr, _ = _attn_probs(i, qv[:, cols], kband, slopes[hq], sink_ref[hq])
                o_ref[:, cols] = _nn(pr.astype(BF16), vband)

    return pl.pallas_call(
        body, name="attn_fwd", grid=(nb,),
        in_specs=[pl.BlockSpec(memory_space=pltpu.SMEM),
                  pl.BlockSpec((BLOCK, ATTN_W), lambda i: (i, 0)),
                  pl.BlockSpec((BLOCK, 2 * KV_W), lambda i: (jnp.maximum(i - 1, 0), 0)),
                  pl.BlockSpec((BLOCK, 2 * KV_W), lambda i: (i, 0))],
        out_specs=pl.BlockSpec((BLOCK, ATTN_W), lambda i: (i, 0)),
        out_shape=jax.ShapeDtypeStruct((seq, ATTN_W), F32),
        compiler_params=_params(("parallel",)),
    )(sinks, q, kv, kv)


def _attn_bwd(q, kv, sinks, dout):
    seq = q.shape[0]
    nb = seq // BLOCK
    slopes = _alibi_slopes()
    scale = HEAD_DIM ** -0.5

    def body(sink_ref, q_ref, kvp_ref, kvc_ref, do_ref, dq_ref, dkv_ref, dsk_ref, prev):
        step = pl.program_id(0)
        i = nb - 1 - step

        @pl.when(step == 0)
        def _():
            prev[...] = jnp.zeros_like(prev)

        qv, kvp, kvc = q_ref[...], kvp_ref[...], kvc_ref[...]
        dov = do_ref[...].astype(BF16)
        dk, dv, dsk = [], [], []
        for h in range(N_KV):
            kband, vband = _bands(kvp, kvc, h)
            dk_h = jnp.zeros((2 * BLOCK, HEAD_DIM), F32)
            dv_h = jnp.zeros((2 * BLOCK, HEAD_DIM), F32)
            for g in range(Q_PER_KV):
                hq = h * Q_PER_KV + g
                cols = slice(hq * HEAD_DIM, (hq + 1) * HEAD_DIM)
                q_h, do_h = qv[:, cols], dov[:, cols]
                pr, ps = _attn_probs(i, q_h, kband, slopes[hq], sink_ref[hq])
                dp = _nt(do_h, vband)
                delta = jnp.sum(pr * dp, axis=1, keepdims=True)
                ds = (pr * (dp - delta) * scale).astype(BF16)
                dsk.append(jnp.broadcast_to(-_colsum(ps * delta), (1, LANE)))
                dq_ref[:, cols] = _nn(ds, kband)
                dk_h = dk_h + _tn(ds, q_h)
                dv_h = dv_h + _tn(pr.astype(BF16), do_h)
            dk.append(dk_h)
            dv.append(dv_h)
        band = jnp.concatenate(dk + dv, axis=1)
        dkv_ref[...] = band[BLOCK:, :] + prev[...]
        prev[...] = band[:BLOCK, :]
        dsk_ref[0] = jnp.concatenate(dsk, axis=0)

    return pl.pallas_call(
        body, name="attn_bwd", grid=(nb,),
        in_specs=[pl.BlockSpec(memory_space=pltpu.SMEM),
                  pl.BlockSpec((BLOCK, ATTN_W), lambda s: (nb - 1 - s, 0)),
                  pl.BlockSpec((BLOCK, 2 * KV_W), lambda s: (jnp.maximum(nb - 2 - s, 0), 0)),
                  pl.BlockSpec((BLOCK, 2 * KV_W), lambda s: (nb - 1 - s, 0)),
                  pl.BlockSpec((BLOCK, ATTN_W), lambda s: (nb - 1 - s, 0))],
        out_specs=[pl.BlockSpec((BLOCK, ATTN_W), lambda s: (nb - 1 - s, 0)),
                   pl.BlockSpec((BLOCK, 2 * KV_W), lambda s: (nb - 1 - s, 0)),
                   pl.BlockSpec((1, N_Q, LANE), lambda s: (nb - 1 - s, 0, 0))],
        out_shape=[jax.ShapeDtypeStruct((seq, ATTN_W), F32), jax.ShapeDtypeStruct((seq, 2 * KV_W), F32),
                   jax.ShapeDtypeStruct((nb, N_Q, LANE), F32)],
        scratch_shapes=[pltpu.VMEM((BLOCK, 2 * KV_W), F32)],
        compiler_params=_params(("arbitrary",)),
    )(sinks, q, kv, kv, dout)


def _discretize(lr, li, ldt, br, bi):
    dt = jnp.exp(ldt)
    mag = jnp.exp(lr * dt)
    ang = li * dt
    ab_r = mag * jnp.cos(ang)
    ab_i = mag * jnp.sin(ang)
    nr = ab_r - 1.0
    ni = ab_i
    den = lr * lr + li * li
    f_r = (nr * lr + ni * li) / den
    f_i = (ni * lr - nr * li) / den
    return ab_r, ab_i, f_r * br - f_i * bi, f_r * bi + f_i * br


def _ssm_prepare(lr, li, ldt, br, bi):
    n = lr.shape[0]
    tn = N_CH
    col = pl.BlockSpec((tn, 1), lambda i: (i, 0))
    mat = pl.BlockSpec((tn, GROUP_W), lambda i: (i, 0))

    def body(lr_ref, li_ref, ldt_ref, br_ref, bi_ref, ar_ref, ai_ref, bbr_ref, bbi_ref):
        ar, ai, bbr, bbi = _discretize(lr_ref[...], li_ref[...], ldt_ref[...], br_ref[...], bi_ref[...])
        ar_ref[...] = ar
        ai_ref[...] = ai
        bbr_ref[...] = bbr
        bbi_ref[...] = bbi

    cs = jax.ShapeDtypeStruct((n, 1), F32)
    ms = jax.ShapeDtypeStruct((n, GROUP_W), F32)
    return pl.pallas_call(
        body, name="ssm_prepare", grid=(n // tn,),
        in_specs=[col, col, col, mat, mat], out_specs=[col, col, mat, mat], out_shape=[cs, cs, ms, ms],
        compiler_params=_params(("parallel",)),
    )(lr, li, ldt, br, bi)


def _ssm_prepare_bwd(lr, li, ldt, br, bi, dar, dai, dbbr, dbbi):
    n = lr.shape[0]
    tn = N_CH
    col = pl.BlockSpec((tn, 1), lambda i: (i, 0))
    mat = pl.BlockSpec((tn, GROUP_W), lambda i: (i, 0))

    def body(lr_ref, li_ref, ldt_ref, br_ref, bi_ref, dar_ref, dai_ref, dbbr_ref, dbbi_ref,
             dlr_ref, dli_ref, dldt_ref, dbr_ref, dbi_ref):
        _, vjp = jax.vjp(_discretize, lr_ref[...], li_ref[...], ldt_ref[...], br_ref[...], bi_ref[...])
        dlr, dli, dldt, dbr, dbi = vjp((dar_ref[...], dai_ref[...], dbbr_ref[...], dbbi_ref[...]))
        dlr_ref[...] = dlr
        dli_ref[...] = dli
        dldt_ref[...] = dldt
        dbr_ref[...] = dbr
        dbi_ref[...] = dbi

    cs = jax.ShapeDtypeStruct((n, 1), F32)
    ms = jax.ShapeDtypeStruct((n, GROUP_W), F32)
    return pl.pallas_call(
        body, name="ssm_prepare_bwd", grid=(n // tn,),
        in_specs=[col, col, col, mat, mat, col, col, mat, mat],
        out_specs=[col, col, col, mat, mat], out_shape=[cs, cs, cs, ms, ms],
        compiler_params=_params(("parallel",)),
    )(lr, li, ldt, br, bi, dar, dai, dbbr, dbbi)


def _load_slabs(src4_ref, dst):
    for s in range(STEPS):
        dst[s * SEGS:(s + 1) * SEGS, :] = jnp.concatenate(
            [src4_ref[j, pl.ds(s, SEGS, stride=STEPS), :] for j in range(4)], axis=1)


def _store_slabs(src, dst4_ref):
    for s in range(STEPS):
        for j in range(4):
            dst4_ref[j, pl.ds(s, SEGS, stride=STEPS), :] = src[s * SEGS:(s + 1) * SEGS, j * LANE:(j + 1) * LANE]


def _power_table(ar_ref, ai_ref, pwr, pwi):
    ar, ai = ar_ref[0], ai_ref[0]
    pr, pi = ar, ai
    pwr[0:1, :] = pr
    pwi[0:1, :] = pi
    for k in range(1, STEPS):
        pr, pi = pr * ar - pi * ai, pr * ai + pi * ar
        pwr[k:k + 1, :] = pr
        pwi[k:k + 1, :] = pi


def _scan_states(ubf, ar_ref, ai_ref, bbr_ref, bbi_ref, pwr, pwi, cin_r, cin_i, hr, hi):
    for k in range(2):
        rows = slice(k * 256, (k + 1) * 256)
        cols = slice(k * HALF_CH, (k + 1) * HALF_CH)
        hr[:, cols] = _nn(ubf[:, rows], bbr_ref[0, k])
        hi[:, cols] = _nn(ubf[:, rows], bbi_ref[0, k])
    for st in range(N_CH // STRIP):
        cs = slice(st * STRIP, (st + 1) * STRIP)
        arb = jnp.broadcast_to(ar_ref[0, :, cs], (SEGS, STRIP))
        aib = jnp.broadcast_to(ai_ref[0, :, cs], (SEGS, STRIP))

        def step(s, carry, cs=cs, arb=arb, aib=aib):
            cr, ci = carry
            rows = pl.ds(pl.multiple_of(s * SEGS, SEGS), SEGS)
            nr = arb * cr - aib * ci + hr[rows, cs]
            ni = arb * ci + aib * cr + hi[rows, cs]
            hr[rows, cs] = nr
            hi[rows, cs] = ni
            return nr, ni

        zero = jnp.zeros((SEGS, STRIP), F32)
        lax.fori_loop(0, STEPS, step, (zero, zero))
    last = slice((STEPS - 1) * SEGS, STEPS * SEGS)
    end_r, end_i = hr[last, :], hi[last, :]
    a64r, a64i = pwr[STEPS - 1:STEPS, :], pwi[STEPS - 1:STEPS, :]
    cr, ci = cin_r, cin_i
    rows_r, rows_i = [], []
    for j in range(SEGS):
        rows_r.append(cr)
        rows_i.append(ci)
        cr, ci = (a64r * cr - a64i * ci + end_r[j:j + 1, :], a64r * ci + a64i * cr + end_i[j:j + 1, :])
    cm_r, cm_i = jnp.concatenate(rows_r, axis=0), jnp.concatenate(rows_i, axis=0)
    for st in range(N_CH // STRIP):
        cs = slice(st * STRIP, (st + 1) * STRIP)
        cmr, cmi = cm_r[:, cs], cm_i[:, cs]

        def fix(s, carry, cs=cs, cmr=cmr, cmi=cmi):
            rows = pl.ds(pl.multiple_of(s * SEGS, SEGS), SEGS)
            pr, pi = pwr[pl.ds(s, 1), cs], pwi[pl.ds(s, 1), cs]
            hr[rows, cs] = hr[rows, cs] + (pr * cmr - pi * cmi)
            hi[rows, cs] = hi[rows, cs] + (pr * cmi + pi * cmr)
            return carry

        lax.fori_loop(0, STEPS, fix, 0)
    return (cm_r, cm_i), (cr, ci)


def _ssm_outputs(u, hr, hi, crt_ref, cit_ref, vec512_ref, wg_ref):
    ys = []
    for k in range(2):
        cols = slice(k * HALF_CH, (k + 1) * HALF_CH)
        ys.append(_nn(hr[:, cols].astype(BF16), crt_ref[0, k]) - _nn(hi[:, cols].astype(BF16), cit_ref[0, k]))
    y = jnp.concatenate(ys, axis=1) + vec512_ref[0, H_DSKIP:H_DSKIP + 1, :] * u
    z, t = _gelu(y)
    gate = jax.nn.sigmoid(_nn(z.astype(BF16), wg_ref[...]) + vec512_ref[0, H_BGLU:H_BGLU + 1, :])
    return y, z, t, gate


def _ssm_specs(layer, nck, rev):
    def chunk(i):
        return nck - 1 - i if rev else i

    return [pl.BlockSpec((4, CHUNK, LANE), lambda i: (0, chunk(i), 0)),
            pl.BlockSpec((1, 1, N_CH), lambda i: (layer, 0, 0)),
            pl.BlockSpec((1, 1, N_CH), lambda i: (layer, 0, 0)),
            pl.BlockSpec((1, 2, 256, HALF_CH), lambda i: (layer, 0, 0, 0)),
            pl.BlockSpec((1, 2, 256, HALF_CH), lambda i: (layer, 0, 0, 0)),
            pl.BlockSpec((1, 2, HALF_CH, 256), lambda i: (layer, 0, 0, 0)),
            pl.BlockSpec((1, 2, HALF_CH, 256), lambda i: (layer, 0, 0, 0)),
            pl.BlockSpec((1, 8, SSM_W), lambda i: (layer, 0, 0)),
            pl.BlockSpec((SSM_W, SSM_W), lambda i: (0, 0))]


def _ssm_fwd(u4, a_r, a_i, bb_r, bb_i, c_rt, c_it, vec512, w_glu, layer):
    seq = u4.shape[1]
    nck = seq // CHUNK

    def body(u4_ref, ar_ref, ai_ref, bbr_ref, bbi_ref, crt_ref, cit_ref, vec512_ref, wg_ref,
             s4_ref, hcr_ref, hci_ref, hr, hi, pwr, pwi, car, cai, ubuf, obuf):
        i = pl.program_id(0)

        @pl.when(i == 0)
        def _():
            car[...] = jnp.zeros_like(car)
            cai[...] = jnp.zeros_like(cai)
            _power_table(ar_ref, ai_ref, pwr, pwi)

        _load_slabs(u4_ref, ubuf)
        u = ubuf[...]
        cin_r, cin_i = car[...], cai[...]
        hcr_ref[0] = jnp.broadcast_to(cin_r, (SEGS, N_CH))
        hci_ref[0] = jnp.broadcast_to(cin_i, (SEGS, N_CH))
        _, (er, ei) = _scan_states(u.astype(BF16), ar_ref, ai_ref, bbr_ref, bbi_ref, pwr, pwi, cin_r, cin_i, hr, hi)
        car[...] = er
        cai[...] = ei
        _, z, _, gate = _ssm_outputs(u, hr, hi, crt_ref, cit_ref, vec512_ref, wg_ref)
        obuf[...] = z * gate
        _store_slabs(obuf, s4_ref)

    return pl.pallas_call(
        body, name="ssm_fwd", grid=(nck,),
        in_specs=_ssm_specs(layer, nck, False),
        out_specs=[pl.BlockSpec((4, CHUNK, LANE), lambda i: (0, i, 0)),
                   pl.BlockSpec((1, SEGS, N_CH), lambda i: (i, 0, 0)),
                   pl.BlockSpec((1, SEGS, N_CH), lambda i: (i, 0, 0))],
        out_shape=[jax.ShapeDtypeStruct((4, seq, LANE), F32), jax.ShapeDtypeStruct((nck, SEGS, N_CH), F32),
                   jax.ShapeDtypeStruct((nck, SEGS, N_CH), F32)],
        scratch_shapes=[pltpu.VMEM((CHUNK, N_CH), F32), pltpu.VMEM((CHUNK, N_CH), F32),
                        pltpu.VMEM((STEPS, N_CH), F32), pltpu.VMEM((STEPS, N_CH), F32),
                        pltpu.VMEM((1, N_CH), F32), pltpu.VMEM((1, N_CH), F32),
                        pltpu.VMEM((CHUNK, SSM_W), F32), pltpu.VMEM((CHUNK, SSM_W), F32)],
        compiler_params=_params(("arbitrary",)),
    )(u4, a_r, a_i, bb_r, bb_i, c_rt, c_it, vec512, w_glu)


def _ssm_bwd(u4, ds4, hc_r, hc_i, a_r, a_i, bb_r, bb_i, c_rt, c_it, vec512, w_glu, layer):
    seq = u4.shape[1]
    nck = seq // CHUNK

    def body(u4_ref, ar_ref, ai_ref, bbr_ref, bbi_ref, crt_ref, cit_ref, vec512_ref, wg_ref, ds4_ref, hcr_ref, hci_ref,
             du4_ref, dbbr_out, dbbi_out, dcrt_out, dcit_out, dar_ref, dai_ref, dwg_ref, dvec_ref,
             hr, hi, gr, gi, pwr, pwi, gcr, gci, accr, acci, ubuf, dbuf, dbbr_ref, dbbi_ref, dcrt_ref, dcit_ref):
        i = pl.program_id(0)

        @pl.when(i == 0)
        def _():
            for ref in (gcr, gci, accr, acci, dbbr_ref, dbbi_ref, dcrt_ref, dcit_ref, dwg_ref, dvec_ref):
                ref[...] = jnp.zeros_like(ref)
            _power_table(ar_ref, ai_ref, pwr, pwi)

        _load_slabs(u4_ref, ubuf)
        u = ubuf[...]
        ubf = u.astype(BF16)
        cin_r, cin_i = hcr_ref[0, 0:1, :], hci_ref[0, 0:1, :]
        (cm_r, cm_i), _ = _scan_states(ubf, ar_ref, ai_ref, bbr_ref, bbi_ref, pwr, pwi, cin_r, cin_i, hr, hi)
        y, z, t, gate = _ssm_outputs(u, hr, hi, crt_ref, cit_ref, vec512_ref, wg_ref)
        _load_slabs(ds4_ref, dbuf)
        ds = dbuf[...]
        da = ds * z * gate * (1.0 - gate)
        dab = da.astype(BF16)
        dz = ds * gate + _nt(dab, wg_ref[...])
        dwg_ref[...] += _tn(z.astype(BF16), dab)
        dvec_ref[H_BGLU:H_BGLU + 1, :] += _colsum(da)
        dy = dz * _gelu_grad(y, t)
        dvec_ref[H_DSKIP:H_DSKIP + 1, :] += _colsum(dy * u)
        du_skip = dy * vec512_ref[0, H_DSKIP:H_DSKIP + 1, :]
        dyb = dy.astype(BF16)
        for k in range(2):
            rows = slice(k * 256, (k + 1) * 256)
            cols = slice(k * HALF_CH, (k + 1) * HALF_CH)
            dcrt_ref[k] += _tn(hr[:, cols].astype(BF16), dyb[:, rows])
            dcit_ref[k] -= _tn(hi[:, cols].astype(BF16), dyb[:, rows])
            gr[:, cols] = _nt(dyb[:, rows], crt_ref[0, k])
            gi[:, cols] = -_nt(dyb[:, rows], cit_ref[0, k])
        for st in range(N_CH // STRIP):
            cs = slice(st * STRIP, (st + 1) * STRIP)
            arb = jnp.broadcast_to(ar_ref[0, :, cs], (SEGS, STRIP))
            aib = jnp.broadcast_to(ai_ref[0, :, cs], (SEGS, STRIP))

            def step(k, carry, cs=cs, arb=arb, aib=aib):
                cr, ci = carry
                rows = pl.ds(pl.multiple_of((STEPS - 1 - k) * SEGS, SEGS), SEGS)
                nr = gr[rows, cs] + (arb * cr + aib * ci)
                ni = gi[rows, cs] + (arb * ci - aib * cr)
                gr[rows, cs] = nr
                gi[rows, cs] = ni
                return nr, ni

            zero = jnp.zeros((SEGS, STRIP), F32)
            lax.fori_loop(0, STEPS, step, (zero, zero))
        first_r, first_i = gr[0:SEGS, :], gi[0:SEGS, :]
        a64r, a64i = pwr[STEPS - 1:STEPS, :], pwi[STEPS - 1:STEPS, :]
        dr_, di_ = gcr[...], gci[...]
        rows_r, rows_i = [None] * SEGS, [None] * SEGS
        for j in range(SEGS - 1, -1, -1):
            rows_r[j], rows_i[j] = dr_, di_
            dr_, di_ = (first_r[j:j + 1, :] + (a64r * dr_ + a64i * di_), first_i[j:j + 1, :] + (a64r * di_ - a64i * dr_))
        gcr[...] = dr_
        gci[...] = di_
        dm_r, dm_i = jnp.concatenate(rows_r, axis=0), jnp.concatenate(rows_i, axis=0)
        for st in range(N_CH // STRIP):
            cs = slice(st * STRIP, (st + 1) * STRIP)
            dmr, dmi = dm_r[:, cs], dm_i[:, cs]

            def fixed(s, cs=cs, dmr=dmr, dmi=dmi):
                rows = pl.ds(pl.multiple_of(s * SEGS, SEGS), SEGS)
                pr, pi = pwr[pl.ds(STEPS - 1 - s, 1), cs], pwi[pl.ds(STEPS - 1 - s, 1), cs]
                g_r = gr[rows, cs] + (pr * dmr + pi * dmi)
                g_i = gi[rows, cs] + (pr * dmi - pi * dmr)
                gr[rows, cs] = g_r
                gi[rows, cs] = g_i
                return g_r, g_i

            g_r, g_i = fixed(jnp.int32(0))
            acc0 = (g_r * cm_r[:, cs] + g_i * cm_i[:, cs], g_i * cm_r[:, cs] - g_r * cm_i[:, cs])

            def step(s, carry, cs=cs, fixed=fixed):
                sr, si = carry
                g_r, g_i = fixed(s)
                prev = pl.ds(pl.multiple_of((s - 1) * SEGS, SEGS), SEGS)
                hpr, hpi = hr[prev, cs], hi[prev, cs]
                return sr + (g_r * hpr + g_i * hpi), si + (g_i * hpr - g_r * hpi)

            sr, si = lax.fori_loop(1, STEPS, step, acc0)
            accr[:, cs] += sr
            acci[:, cs] += si
        grb, gib = gr[...].astype(BF16), gi[...].astype(BF16)
        dus = []
        for k in range(2):
            rows = slice(k * 256, (k + 1) * 256)
            cols = slice(k * HALF_CH, (k + 1) * HALF_CH)
            dus.append(_nt(grb[:, cols], bbr_ref[0, k]) + _nt(gib[:, cols], bbi_ref[0, k]))
            dbbr_ref[k] += _tn(ubf[:, rows], grb[:, cols])
            dbbi_ref[k] += _tn(ubf[:, rows], gib[:, cols])
        dbuf[...] = jnp.concatenate(dus, axis=1) + du_skip
        _store_slabs(dbuf, du4_ref)

        @pl.when(i == nck - 1)
        def _():
            dar_ref[...] = _colsum(accr[...])
            dai_ref[...] = _colsum(acci[...])
            ng = N_GROUPS // 2
            for k in range(2):
                for acc, out in ((dbbr_ref, dbbr_out), (dbbi_ref, dbbi_out)):
                    out[k] = jnp.concatenate(
                        [acc[k, g * GROUP_W:(g + 1) * GROUP_W, g * STATE:(g + 1) * STATE] for g in range(ng)], axis=0)
                for acc, out in ((dcrt_ref, dcrt_out), (dcit_ref, dcit_out)):
                    out[k] = jnp.concatenate(
                        [acc[k, g * STATE:(g + 1) * STATE, g * GROUP_W:(g + 1) * GROUP_W] for g in range(ng)], axis=0)

    rev4 = pl.BlockSpec((4, CHUNK, LANE), lambda i: (0, nck - 1 - i, 0))
    hc_spec = pl.BlockSpec((1, SEGS, N_CH), lambda i: (nck - 1 - i, 0, 0))
    fixed2 = lambda shape: pl.BlockSpec(shape, lambda i: (0,) * len(shape))
    return pl.pallas_call(
        body, name="ssm_bwd", grid=(nck,),
        in_specs=_ssm_specs(layer, nck, True) + [rev4, hc_spec, hc_spec],
        out_specs=[rev4, fixed2((2, 256, STATE)), fixed2((2, 256, STATE)), fixed2((2, HALF_CH, GROUP_W)),
                   fixed2((2, HALF_CH, GROUP_W)), fixed2((1, N_CH)), fixed2((1, N_CH)), fixed2((SSM_W, SSM_W)),
                   fixed2((8, SSM_W))],
        out_shape=[jax.ShapeDtypeStruct((4, seq, LANE), F32),
                   jax.ShapeDtypeStruct((2, 256, STATE), F32), jax.ShapeDtypeStruct((2, 256, STATE), F32),
                   jax.ShapeDtypeStruct((2, HALF_CH, GROUP_W), F32), jax.ShapeDtypeStruct((2, HALF_CH, GROUP_W), F32),
                   jax.ShapeDtypeStruct((1, N_CH), F32), jax.ShapeDtypeStruct((1, N_CH), F32),
                   jax.ShapeDtypeStruct((SSM_W, SSM_W), F32), jax.ShapeDtypeStruct((8, SSM_W), F32)],
        scratch_shapes=[pltpu.VMEM((CHUNK, N_CH), F32), pltpu.VMEM((CHUNK, N_CH), F32),
                        pltpu.VMEM((CHUNK, N_CH), F32), pltpu.VMEM((CHUNK, N_CH), F32),
                        pltpu.VMEM((STEPS, N_CH), F32), pltpu.VMEM((STEPS, N_CH), F32),
                        pltpu.VMEM((1, N_CH), F32), pltpu.VMEM((1, N_CH), F32),
                        pltpu.VMEM((SEGS, N_CH), F32), pltpu.VMEM((SEGS, N_CH), F32),
                        pltpu.VMEM((CHUNK, SSM_W), F32), pltpu.VMEM((CHUNK, SSM_W), F32),
                        pltpu.VMEM((2, 256, HALF_CH), F32), pltpu.VMEM((2, 256, HALF_CH), F32),
                        pltpu.VMEM((2, HALF_CH, 256), F32), pltpu.VMEM((2, HALF_CH, 256), F32)],
        compiler_params=_params(("arbitrary",)),
    )(u4, a_r, a_i, bb_r, bb_i, c_rt, c_it, vec512, w_glu, ds4, hc_r, hc_i)


def _block_diag(t):
    nl, _, ng, a, b = t.shape
    eye = jnp.eye(ng, dtype=t.dtype)
    return jnp.einsum("gh,lkgab->lkgahb", eye, t).reshape(nl, 2, ng * a, ng * b)


def _local_step(x, loss_target, mod, p, comm):
    nl = mod.shape[0]
    pad1024 = jnp.zeros((nl, 16 - 10, D_MODEL), F32)
    vec = jnp.concatenate([mod.reshape(nl, N_MOD, D_MODEL), p["pre_mix_g"][:, None], p["post_mix_g"][:, None],
                           p["pre_mlp_g"][:, None], p["post_mlp_g"][:, None], pad1024], axis=1)
    vec512 = jnp.concatenate([p["attn_out_g"][:, None], p["ssm_out_g"][:, None], p["d_skip"][:, None],
                              p["b_glu"][:, None], jnp.zeros((nl, 4, SSM_W), F32)], axis=1)
    n_all = nl * N_CH
    lr = p["lam_re"].reshape(n_all, 1)
    li = p["lam_im"].reshape(n_all, 1)
    ldt = jnp.broadcast_to(p["log_dt"][:, :, None], (nl, N_GROUPS, STATE)).reshape(n_all, 1)
    br = p["b_re"].reshape(n_all, GROUP_W)
    bi = p["b_im"].reshape(n_all, GROUP_W)
    ab_r, ab_i, bb_r, bb_i = _ssm_prepare(lr, li, ldt, br, bi)
    a_r = ab_r.reshape(nl, 1, N_CH)
    a_i = ab_i.reshape(nl, 1, N_CH)

    def dense_b(bb):
        return _block_diag(bb.reshape(nl, 2, 16, STATE, GROUP_W).transpose(0, 1, 2, 4, 3)).astype(BF16)

    def dense_c(cc):
        return _block_diag(cc.reshape(nl, 2, 16, GROUP_W, STATE).transpose(0, 1, 2, 4, 3)).astype(BF16)

    bbr_d, bbi_d = dense_b(bb_r), dense_b(bb_i)
    crt_d, cit_d = dense_c(p["c_re"]), dense_c(p["c_im"])

    saved = []
    xl = x
    mixer_w, mlp_w = [None] * nl, [None] * nl
    for l in range(nl):
        mixer_w[l], tok = comm.mixer_weights(l, [xl, bbr_d, bbi_d, crt_d, cit_d] if l == 0 else xl)
        w_in_t, w_glu, w_out = mixer_w[l]
        q, kv, u4, h1 = _in_proj_fwd(xl, _after(vec, *tok), w_in_t, l)
        attn = _attn_fwd(q, kv, p["attn_sinks"][l])
        s4, hc_r, hc_i = _ssm_fwd(u4, a_r, a_i, bbr_d, bbi_d, crt_d, cit_d, vec512, w_glu, l)
        x1 = _out_proj_fwd(xl, attn, s4, vec, vec512, w_out, l)
        mlp_w[l] = comm.mlp_weights(l, x1)
        x2 = _mlp_fwd(x1, vec, mlp_w[l][0], mlp_w[l][1], l)
        saved.append((xl, q, kv, u4, h1, attn, s4, hc_r, hc_i, x1))
        xl = x2

    dx, loss_parts = _loss_head(xl, loss_target)
    loss = jnp.sum(loss_parts[:, 0, 0])

    dvec_l, dvec512_l, dsink_l = [None] * nl, [None] * nl, [None] * nl
    dab_r, dab_i, dbb_r, dbb_i, dc_re, dc_im = ([None] * nl for _ in range(6))
    toks = []
    for l in range(nl - 1, -1, -1):
        xl, q, kv, u4, h1, attn, s4, hc_r, hc_i, x1 = saved[l]
        w_in_t, w_glu, w_out = mixer_w[l]
        dx1, h2, da, r, df, dvec_m = _mlp_bwd(dx, x1, _after(vec, *toks), mlp_w[l][0], mlp_w[l][1], l)
        toks = comm.after_mlp_bwd(l, dx1)
        dw_mlp_out = _matmul_tn(r, df, BF16, "dw_mlp_out").reshape(4, D_FF // 4, D_MODEL)
        dw_mlp_in = _matmul_tn(h2, da, BF16, "dw_mlp_in", pieces=4)
        toks = toks + comm.mlp_grads(l, [dw_mlp_in, dw_mlp_out])
        dattn, ds4, heads, dmixed, dvec_o, dvec512_o = _out_proj_bwd(
            dx1, attn, s4, _after(vec, *toks), vec512, w_out, l)
        dw_out = _matmul_tn(heads, dmixed, BF16, "dw_out").reshape(4, D_MODEL // 4, D_MODEL)
        dq, dkv, dsk = _attn_bwd(q, kv, p["attn_sinks"][l], dattn)
        (du4, dbbr, dbbi, dcrt, dcit, dar, dai, dwg, dvec512_s) = _ssm_bwd(
            u4, ds4, hc_r, hc_i, a_r, a_i, bbr_d, bbi_d, crt_d, cit_d, vec512, w_glu, l)
        dw_glu = dwg.astype(BF16).reshape(4, SSM_W // 4, SSM_W)
        dx, dproj, dvec_i = _in_proj_bwd(dx1, dq, dkv, du4, xl, vec, w_in_t, l)
        toks = comm.after_in_proj_bwd(l, dx)
        dw_in = _matmul_tn(h1, dproj, BF16, "dw_in")
        toks = toks + comm.mixer_grads(l, [dw_in.reshape(D_MODEL, 4, IN_W // 4).transpose(1, 0, 2), dw_glu, dw_out])
        dvec_l[l] = dvec_m + dvec_o + dvec_i
        dvec512_l[l] = dvec512_o + dvec512_s
        dsink_l[l] = jnp.sum(dsk[:, :, 0], axis=0)
        dab_r[l], dab_i[l], dbb_r[l], dbb_i[l], dc_re[l], dc_im[l] = dar, dai, dbbr, dbbi, dcrt, dcit

    dvec = _after(jnp.stack(dvec_l), *toks)
    dvec512 = jnp.stack(dvec512_l)
    ng = N_GROUPS // 2

    def b_cols(d):
        return jnp.stack(d).reshape(nl, 2, ng, GROUP_W, STATE).transpose(0, 1, 2, 4, 3).reshape(n_all, GROUP_W)

    def c_param(d):
        return jnp.stack(d).reshape(nl, 2, ng, STATE, GROUP_W).transpose(0, 1, 2, 4, 3).reshape(c_shape)

    dbb_r_c, dbb_i_c = b_cols(dbb_r), b_cols(dbb_i)
    c_shape = (nl, N_GROUPS, GROUP_W, STATE)
    dlr, dli, dldt, dbr, dbi = _ssm_prepare_bwd(
        lr, li, ldt, br, bi, jnp.stack(dab_r).reshape(n_all, 1), jnp.stack(dab_i).reshape(n_all, 1), dbb_r_c, dbb_i_c)
    small = {
        "b_ada": dvec[:, :N_MOD].reshape(nl, N_MOD * D_MODEL),
        "pre_mix_g": dvec[:, V_PRE_MIX], "post_mix_g": dvec[:, V_POST_MIX],
        "pre_mlp_g": dvec[:, V_PRE_MLP], "post_mlp_g": dvec[:, V_POST_MLP],
        "attn_sinks": jnp.stack(dsink_l),
        "lam_re": dlr.reshape(nl, N_GROUPS, STATE), "lam_im": dli.reshape(nl, N_GROUPS, STATE),
        "log_dt": jnp.sum(dldt.reshape(nl, N_GROUPS, STATE), axis=-1),
        "b_re": dbr.reshape(nl, N_GROUPS, STATE, GROUP_W), "b_im": dbi.reshape(nl, N_GROUPS, STATE, GROUP_W),
        "c_re": c_param(dc_re), "c_im": c_param(dc_im),
        "d_skip": dvec512[:, H_DSKIP], "b_glu": dvec512[:, H_BGLU],
        "attn_out_g": dvec512[:, H_ATTN_G], "ssm_out_g": dvec512[:, H_SSM_G],
    }
    return loss, dx, small, small["b_ada"]


WEIGHTS = ["w_ada", "b_ada", "pre_mix_g", "w_in", "attn_sinks", "lam_re", "lam_im", "log_dt", "b_re", "b_im", "c_re",
           "c_im", "d_skip", "w_glu", "b_glu", "attn_out_g", "ssm_out_g", "w_out", "post_mix_g", "pre_mlp_g",
           "w_mlp_in", "w_mlp_out", "post_mlp_g"]
BIG = ["w_in", "w_glu", "w_out", "w_mlp_in", "w_mlp_out"]
SMALL = [n for n in WEIGHTS if n not in BIG and n != "w_ada"]
PACK_ROWS = 256


def _pack(parts):
    rows = []
    for n in SMALL:
        flat = parts[n].reshape(-1)
        pad = (-flat.shape[0]) % (PACK_ROWS * LANE)
        rows.append(jnp.pad(flat, (0, pad)).reshape(-1, LANE))
    return jnp.concatenate(rows, axis=0)


def _unpack(packed, shapes):
    out, r0 = {}, 0
    for n in SMALL:
        size = int(np.prod(shapes[n]))
        rows = -(-size // (PACK_ROWS * LANE)) * PACK_ROWS
        out[n] = packed[r0:r0 + rows].reshape(-1)[:size].reshape(shapes[n])
        r0 += rows
    return out


MIXER = ["w_in", "w_glu", "w_out"]
MLP = ["w_mlp_in", "w_mlp_out"]


class _Exchanges:
    def __init__(self, shards, wts, mom, var, chip):
        self.shards, self.wts, self.mom, self.var, self.chip = shards, wts, mom, var, chip
        self.chip_arr = jnp.reshape(chip, (1,)).astype(jnp.int32)
        self.nl = len(shards["w_in"])
        self.gathers, self.scatters, self.pairs = {}, {}, {}
        self.res = {n: None for n in BIG}

    def _start_gather(self, group, tag, l, after=()):
        srcs = [self.shards[n][l] for n in group]
        lands = [lax.dynamic_update_slice(lax.empty((4,) + s.shape, s.dtype), s[None], (self.chip, 0, 0)) for s in srcs]
        plan = _plan_gather(len(srcs))
        st = _exchange_start(f"gather_{tag}{l}_start", 3 * len(srcs), plan, srcs + lands, after)
        self.gathers[tag, l] = (plan, st)
        return st[3]

    def _wait_gather(self, tag, l, after):
        plan, st = self.gathers.pop((tag, l))
        n = len(st[2]) // 2
        bufs = _exchange_wait(f"gather_{tag}{l}_wait", 3 * n, plan, st, after)
        return [b.reshape(4 * b.shape[1], b.shape[2]) for b in bufs[n:]]

    def begin(self, after):
        return [self._start_gather(MIXER, "mixer", 0, after), self._start_gather(MLP, "mlp", 0, after)]

    def mixer_weights(self, l, after):
        w = self._wait_gather("mixer", l, after)
        toks = []
        if l + 1 < self.nl:
            toks = [self._start_gather(MIXER, "mixer", l + 1, w[:1]), self._start_gather(MLP, "mlp", l + 1, w[:1])]
        return w, toks

    def mlp_weights(self, l, after):
        return self._wait_gather("mlp", l, after)

    def _start_scatter(self, tag, l, group, pieces):
        plan = _plan_scatter(len(pieces))
        st = _exchange_start(f"scatter_{tag}{l}_start", 3 * len(pieces), plan,
                             list(pieces) + [lax.empty(p.shape, p.dtype) for p in pieces])
        self.scatters[tag] = (l, group, plan, st)
        return [st[3]]

    def _finish_scatter(self, tag, after):
        l, group, plan, st = self.scatters.pop(tag)
        n = len(group)
        bufs = _exchange_wait(f"scatter_{tag}{l}_wait", 3 * n, plan, st, after)
        sums = [_sum_pieces(bufs[k], bufs[n + k], self.chip_arr, "sum_" + group[k]) for k in range(n)]
        plan2 = _plan_pair(n)
        st2 = _exchange_start(f"pair_{tag}{l}_start", n, plan2, sums + [lax.empty(s.shape, s.dtype) for s in sums])
        self.pairs[tag] = (l, group, plan2, st2)
        return [st2[3]]

    def _finish_pair(self, tag, after):
        l, group, plan, st = self.pairs.pop(tag)
        n = len(group)
        bufs = _exchange_wait(f"pair_{tag}{l}_wait", n, plan, st, after)
        for k, name in enumerate(group):
            self.res[name] = _adamw_layer([bufs[k], bufs[n + k]], self.wts[name], self.mom[name], self.var[name],
                                          l, self.res[name], "adamw_" + name)

    def after_mlp_bwd(self, l, after):
        toks = self._finish_scatter("mixer", after) if "mixer" in self.scatters else []
        if "mlp" in self.pairs:
            self._finish_pair("mlp", after)
        return toks

    def mlp_grads(self, l, pieces):
        return self._start_scatter("mlp", l, MLP, pieces)

    def after_in_proj_bwd(self, l, after):
        toks = self._finish_scatter("mlp", after)
        if "mixer" in self.pairs:
            self._finish_pair("mixer", after)
        return toks

    def mixer_grads(self, l, pieces):
        return self._start_scatter("mixer", l, MIXER, pieces)

    def finish_mixer_scatter(self, after):
        return self._finish_scatter("mixer", after)

    def finish_mlp(self, after):
        self._finish_pair("mlp", after)

    def finish_mixer(self, after):
        self._finish_pair("mixer", after)

    def results(self):
        return self.res


def kernel(x, c, w_ada, b_ada, pre_mix_g, w_in, attn_sinks, lam_re, lam_im, log_dt, b_re, b_im, c_re, c_im, d_skip, w_glu, b_glu, attn_out_g, ssm_out_g, w_out, post_mix_g, pre_mlp_g, w_mlp_in, w_mlp_out, post_mlp_g, loss_target, m_w_ada, m_b_ada, m_pre_mix_g, m_w_in, m_attn_sinks, m_lam_re, m_lam_im, m_log_dt, m_b_re, m_b_im, m_c_re, m_c_im, m_d_skip, m_w_glu, m_b_glu, m_attn_out_g, m_ssm_out_g, m_w_out, m_post_mix_g, m_pre_mlp_g, m_w_mlp_in, m_w_mlp_out, m_post_mlp_g, v_w_ada, v_b_ada, v_pre_mix_g, v_w_in, v_attn_sinks, v_lam_re, v_lam_im, v_log_dt, v_b_re, v_b_im, v_c_re, v_c_im, v_d_skip, v_w_glu, v_b_glu, v_attn_out_g, v_ssm_out_g, v_w_out, v_post_mix_g, v_pre_mlp_g, v_w_mlp_in, v_w_mlp_out, v_post_mlp_g):
    args = locals()
    wts = {n: args[n] for n in WEIGHTS}
    mom = {n: args["m_" + n] for n in WEIGHTS}
    var = {n: args["v_" + n] for n in WEIGHTS}
    nl = w_in.shape[0]
    ix, iy, ic = lax.axis_index("x"), lax.axis_index("y"), lax.axis_index("c")
    chip = 2 * ix + iy
    me = 4 * ix + 2 * iy + ic
    wcols = w_ada.shape[2]

    shards = {"w_in": [w_in[l].astype(BF16).T for l in range(nl)], "w_glu": [w_glu[l].astype(BF16) for l in range(nl)],
              "w_out": [w_out[l].astype(BF16) for l in range(nl)],
              "w_mlp_in": [w_mlp_in[l].astype(BF16).T for l in range(nl)],
              "w_mlp_out": [w_mlp_out[l].astype(BF16) for l in range(nl)]}
    comm = _Exchanges(shards, wts, mom, var, chip)

    c_all = _gather([c.reshape(1, 1, 1, D_MODEL)], "all", "gather_c")[0].reshape(8, D_MODEL)
    b_sh = lax.dynamic_slice(b_ada, (0, chip * wcols), (nl, wcols)).reshape(nl, 1, wcols)
    mod_sh = _ada_forward(c_all, w_ada, b_sh)
    mod_all = _gather([mod_sh.reshape(1, 1, nl * 8, wcols)], "chips", "gather_mod")[0]
    toks = comm.begin([mod_all])
    mod = lax.dynamic_index_in_dim(mod_all.reshape(4, nl, 8, wcols), me, axis=2, keepdims=False)
    mod = mod.transpose(1, 0, 2).reshape(nl, 4 * wcols)

    small_p = {n: wts[n] for n in SMALL}
    small_p["log_dt"] = _after(log_dt, *toks)
    loss, grad_x, small, dmod = _local_step(x[0], loss_target[0], mod, small_p, comm)
    loss = lax.psum(loss, ("x", "y", "c"))

    packed = _pack(small)
    rows = packed.shape[0]
    pair_plan = _plan_pair(1)
    pair_small = _exchange_start("pair_small_start", 1, pair_plan, [packed, lax.empty((rows, LANE), F32)])

    dmod = _after(dmod, pair_small[3])
    dmod_all = _gather([dmod.reshape(1, 1, nl, N_MOD * D_MODEL)], "all", "gather_dmod")[0][0]
    dmod_sh = lax.dynamic_slice(dmod_all, (0, 0, chip * wcols), (8, nl, wcols)).transpose(1, 0, 2)
    g_ada = _ada_weight_grad(c_all.T, dmod_sh)
    res = {"w_ada": _adamw(g_ada[:, None], w_ada, m_w_ada, v_w_ada, "adamw_w_ada")}

    comm.finish_mlp(res["w_ada"][0])
    toks = comm.finish_mixer_scatter(res["w_ada"][0])

    own, other = _exchange_wait("pair_small_wait", 1, pair_plan, pair_small, res["w_ada"][0])
    chip_sum = _after(_sum_list([own, other], "sum_pair_small"), *toks)
    quad_plan = _plan_gather(1)
    quad0 = lax.dynamic_update_slice(lax.empty((4, rows, LANE), F32), chip_sum[None], (chip, 0, 0))
    quad_small = _exchange_start("gather_small_start", 3, quad_plan, [chip_sum, quad0])
    comm.finish_mixer([comm.results()[n][0] for n in MLP] + [_after(dmod, quad_small[3])])
    res.update(comm.results())
    quad = _exchange_wait("gather_small_wait", 3, quad_plan, quad_small, [res[n][0] for n in BIG])[1]
    outs = _adamw(quad[None], _pack({n: wts[n] for n in SMALL})[None], _pack({n: mom[n] for n in SMALL})[None],
                  _pack({n: var[n] for n in SMALL})[None], "adamw_small")
    shapes = {n: wts[n].shape for n in SMALL}
    unpacked = [_unpack(o[0], shapes) for o in outs]
    for n in SMALL:
        res[n] = [u[n] for u in unpacked]

    return (loss, grad_x[None], *[res[n][0] for n in WEIGHTS], *[res[n][1] for n in WEIGHTS],
            *[res[n][2] for n in WEIGHTS], *[res[n][3] for n in WEIGHTS])
```

```python
import functools
import math

import numpy as np
import jax
import jax.numpy as jnp
from jax import lax
from jax.experimental import pallas as pl
from jax.experimental.pallas import tpu as pltpu

F32 = jnp.float32
BF16 = jnp.bfloat16

D_MODEL = 1024
ATTN_W = 512
SSM_W = 512
HEAD_DIM = 64
N_Q = 8
N_KV = 2
Q_PER_KV = 4
KV_W = 128
WINDOW = 128
BLOCK = 128
N_GROUPS = 32
GROUP_W = 16
STATE = 64
N_CH = N_GROUPS * STATE
HALF_CH = N_CH // 2
D_FF = 4096
IN_W = 1280
N_MOD = 6
EPS = 1e-6
NEG_INF = -1e30

ADAM_LR = 0.001
ADAM_B1 = 0.9
ADAM_B2 = 0.999
ADAM_EPS = 1e-08
ADAM_WD = 0.01
ADAM_STEP = 10

ROW_TILE = 256
CHUNK = 256
SEGS = 8
STEPS = CHUNK // SEGS
STRIP = 1024
VMEM_LIMIT_V7X = 56 * 1024 * 1024
LANE = 128
SUBLANE = 8

GELU_K0 = math.sqrt(2.0 / math.pi)
GELU_K1 = 0.044715

V_SH1, V_SC1, V_G1, V_SH2, V_SC2, V_G2, V_PRE_MIX, V_POST_MIX, V_PRE_MLP, V_POST_MLP = range(10)
H_ATTN_G, H_SSM_G, H_DSKIP, H_BGLU = range(4)

HBM = pl.BlockSpec(memory_space=pltpu.HBM)
SEM = pl.BlockSpec(memory_space=pltpu.SEMAPHORE)
EFFECT = pltpu.SideEffectType.DATAFLOW_SIDE_EFFECTING
MESH_ID = pl.DeviceIdType.MESH


def _nn(a, b):
    return lax.dot_general(a, b, (((1,), (0,)), ((), ())), preferred_element_type=F32)


def _nt(a, b):
    return lax.dot_general(a, b, (((1,), (1,)), ((), ())), preferred_element_type=F32)


def _tn(a, b):
    return lax.dot_general(a, b, (((0,), (0,)), ((), ())), preferred_element_type=F32)


def _params(sem):
    return pltpu.CompilerParams(dimension_semantics=sem, vmem_limit_bytes=VMEM_LIMIT_V7X)


def _rms_fwd(x, g):
    r = lax.rsqrt(jnp.mean(x * x, axis=-1, keepdims=True) + EPS)
    xh = x * r
    return xh * g, xh, r


def _rms_bwd(dy, xh, r, g):
    dxh = dy * g
    dx = r * (dxh - xh * jnp.mean(dxh * xh, axis=-1, keepdims=True))
    return dx, dy * xh


def _colsum(t):
    return jnp.sum(t, axis=0, keepdims=True)


def _gelu(y):
    t = jnp.tanh(GELU_K0 * (y + GELU_K1 * (y * y * y)))
    return 0.5 * y * (1.0 + t), t


def _gelu_grad(y, t):
    return 0.5 * (1.0 + t) + 0.5 * y * (1.0 - t * t) * GELU_K0 * (1.0 + 3.0 * GELU_K1 * y * y)


def _alibi_slopes():
    return [float(s) for s in 2.0 ** (-8.0 * np.arange(1, N_Q + 1) / N_Q)]


def _pick_rows(rows, bytes_per_row, budget):
    t = rows
    while t % (2 * SUBLANE) == 0 and t * bytes_per_row > budget:
        t //= 2
    return t


def _load_once(step, pairs, sems):
    @pl.when(step == 0)
    def _():
        cps = [pltpu.make_async_copy(src, dst, sems.at[k]) for k, (src, dst) in enumerate(pairs)]
        for cp in cps:
            cp.start()
        for cp in cps:
            cp.wait()


_GROUPS = {
    "all": ([(0, 0, 1), (0, 1, 0), (0, 1, 1), (1, 0, 0), (1, 0, 1), (1, 1, 0), (1, 1, 1)], (4, 2, 1), 8),
    "chips": ([(1, 0, 0), (0, 1, 0), (1, 1, 0)], (2, 1, 0), 4),
    "pair": ([(0, 0, 1)], (0, 0, 1), 2),
}


def _flip(v, f):
    return 1 - v if f else v


def _gather(arrs, kind, name):
    masks, wts, n = _GROUPS[kind]
    na, nm = len(arrs), len(masks)

    def body(*refs):
        ins, outs = refs[:na], refs[na:2 * na]
        ssem, rsem, lsem = refs[2 * na:]
        x, y, c = lax.axis_index("x"), lax.axis_index("y"), lax.axis_index("c")
        me = wts[0] * x + wts[1] * y + wts[2] * c
        local = [pltpu.make_async_copy(ins[k], outs[k].at[:, pl.ds(me, 1)], lsem.at[k]) for k in range(na)]
        for cp in local:
            cp.start()
        remote = []
        for k in range(na):
            for mi, (fx, fy, fc) in enumerate(masks):
                peer = (_flip(x, fx), _flip(y, fy), _flip(c, fc))
                remote.append(pltpu.make_async_remote_copy(
                    src_ref=ins[k], dst_ref=outs[k].at[:, pl.ds(me, 1)],
                    send_sem=ssem.at[k * nm + mi], recv_sem=rsem.at[k * nm + mi],
                    device_id=peer, device_id_type=MESH_ID))
        for cp in remote:
            cp.start()
        for cp in remote:
            cp.wait()
        for cp in local:
            cp.wait()

    outs = pl.pallas_call(
        body, name=name,
        out_shape=[jax.ShapeDtypeStruct((a.shape[0], n) + a.shape[2:], a.dtype) for a in arrs],
        in_specs=[HBM] * na, out_specs=[HBM] * na,
        scratch_shapes=[pltpu.SemaphoreType.DMA((na * nm,)), pltpu.SemaphoreType.DMA((na * nm,)),
                        pltpu.SemaphoreType.DMA((na,))],
    )(*arrs)
    return list(outs)


def _hbm(a):
    return pltpu.with_memory_space_constraint(a, pltpu.HBM)


def _after(x, *tokens):
    for t in tokens:
        x = x + t[0, 0].astype(x.dtype)
    return x


def _exchange_start(name, n_copies, plan, bufs, after=()):
    n, na = len(bufs), len(after)

    def body(*refs):
        ssem, rsem, token = refs[n + na], refs[n + na + 1], refs[2 * n + na + 2]
        for k, (src, dst, dev) in enumerate(plan(refs[:n])):
            pltpu.make_async_remote_copy(src_ref=src, dst_ref=dst, send_sem=ssem.at[k], recv_sem=rsem.at[k],
                                         device_id=dev, device_id_type=MESH_ID).start()
        token[...] = jnp.zeros_like(token)

    outs = pl.pallas_call(
        body, name=name,
        out_shape=(pltpu.SemaphoreType.DMA((n_copies,)), pltpu.SemaphoreType.DMA((n_copies,)),
                   *[pltpu.HBM(b.shape, b.dtype) for b in bufs], jax.ShapeDtypeStruct((SUBLANE, LANE), F32)),
        in_specs=[HBM] * n + [pl.BlockSpec(memory_space=pl.ANY)] * na,
        out_specs=(SEM, SEM, *[HBM] * n, pl.BlockSpec(memory_space=pltpu.VMEM)),
        input_output_aliases={i: 2 + i for i in range(n)},
        compiler_params=pltpu.CompilerParams(has_side_effects=EFFECT),
    )(*[_hbm(b) for b in bufs], *after)
    return outs[0], outs[1], list(outs[2:2 + n]), outs[2 + n]


def _exchange_wait(name, n_copies, plan, started, after):
    ssem, rsem, bufs, _ = started
    n = len(bufs)
    after = list(after) if isinstance(after, (list, tuple)) else [after]

    def body(*refs):
        ssem_ref, rsem_ref = refs[n], refs[n + 1]
        for k, (src, dst, dev) in enumerate(plan(refs[:n])):
            cp = pltpu.make_async_remote_copy(src_ref=src, dst_ref=dst, send_sem=ssem_ref.at[k], recv_sem=rsem_ref.at[k],
                                              device_id=dev, device_id_type=MESH_ID)
            cp.wait_send()
            cp.wait_recv()

    outs = pl.pallas_call(
        body, name=name,
        out_shape=tuple(pltpu.HBM(b.shape, b.dtype) for b in bufs),
        in_specs=[HBM] * n + [SEM, SEM] + [pl.BlockSpec(memory_space=pl.ANY)] * len(after), out_specs=tuple([HBM] * n),
        input_output_aliases={i: i for i in range(n)},
        compiler_params=pltpu.CompilerParams(has_side_effects=EFFECT),
    )(*bufs, ssem, rsem, *after)
    return list(outs)


def _position():
    x, y, c = lax.axis_index("x"), lax.axis_index("y"), lax.axis_index("c")
    return x, y, c, [(1 - x, y), (x, 1 - y), (1 - x, 1 - y)]


def _plan_gather(na):
    def plan(refs):
        x, y, c, chips = _position()
        return [(refs[k], refs[na + k].at[2 * x + y], (px, py, c)) for k in range(na) for px, py in chips]
    return plan


def _plan_scatter(na):
    def plan(refs):
        x, y, c, chips = _position()
        return [(refs[k].at[2 * px + py], refs[na + k].at[2 * x + y], (px, py, c))
                for k in range(na) for px, py in chips]
    return plan


def _plan_pair(na):
    def plan(refs):
        x, y, c, _ = _position()
        return [(refs[k], refs[na + k], (x, y, 1 - c)) for k in range(na)]
    return plan


def _sum_list(arrs, name):
    n = len(arrs)
    r, c = arrs[0].shape
    tr = _pick_rows(r, c * 4 * (n + 1), 4 << 20)

    def body(*refs):
        acc = refs[0][...].astype(F32)
        for j in range(1, n):
            acc = acc + refs[j][...].astype(F32)
        refs[n][...] = acc

    blk = pl.BlockSpec((tr, c), lambda i: (i, 0))
    return pl.pallas_call(
        body, name=name, grid=(r // tr,), in_specs=[blk] * n, out_specs=blk,
        out_shape=jax.ShapeDtypeStruct((r, c), F32), compiler_params=_params(("parallel",)),
    )(*arrs)


def _sum_pieces(own, recv, chip, name):
    _, r, c = own.shape
    tr = _pick_rows(r, c * 2 * 6, 4 << 20)

    def body(chip_ref, own_ref, recv_ref, o_ref):
        acc = own_ref[0].astype(F32)
        for j in range(4):
            acc = acc + jnp.where(chip_ref[0] == j, 0.0, recv_ref[j].astype(F32))
        o_ref[...] = acc.astype(BF16)

    return pl.pallas_call(
        body, name=name,
        grid_spec=pltpu.PrefetchScalarGridSpec(
            num_scalar_prefetch=1, grid=(r // tr,),
            in_specs=[pl.BlockSpec((1, tr, c), lambda i, chip_ref: (chip_ref[0], i, 0)),
                      pl.BlockSpec((4, tr, c), lambda i, chip_ref: (0, i, 0))],
            out_specs=pl.BlockSpec((tr, c), lambda i, chip_ref: (i, 0))),
        out_shape=jax.ShapeDtypeStruct((r, c), BF16), compiler_params=_params(("parallel",)),
    )(chip, own, recv)


def _adam_update(g, w, m, v):
    mn = ADAM_B1 * m + (1.0 - ADAM_B1) * g
    vn = ADAM_B2 * v + (1.0 - ADAM_B2) * jnp.square(g)
    m_hat = mn / (1.0 - ADAM_B1 ** ADAM_STEP)
    v_hat = vn / (1.0 - ADAM_B2 ** ADAM_STEP)
    return -ADAM_LR * (m_hat / (jnp.sqrt(v_hat) + ADAM_EPS) + ADAM_WD * w), mn, vn


def _adamw_layer(grads, w, m, v, layer, prev, name):
    ng = len(grads)
    nl, r, c = w.shape
    tr = _pick_rows(r, c * 4 * (ng + 7), 6 << 20)
    if prev is None:
        prev = [lax.empty((nl, r, c), F32) for _ in range(4)]

    def body(*refs):
        g = refs[0][...].astype(F32)
        for j in range(1, ng):
            g = g + refs[j][...].astype(F32)
        w_ref, m_ref, v_ref = refs[ng:ng + 3]
        go_ref, d_ref, mo_ref, vo_ref = refs[ng + 7:ng + 11]
        d, mn, vn = _adam_update(g, w_ref[0], m_ref[0], v_ref[0])
        go_ref[0] = g
        d_ref[0] = d
        mo_ref[0] = mn
        vo_ref[0] = vn

    gblk = pl.BlockSpec((tr, c), lambda i: (i, 0))
    blk = pl.BlockSpec((1, tr, c), lambda i: (layer, i, 0))
    keep = pl.BlockSpec(memory_space=pl.ANY)
    sds = jax.ShapeDtypeStruct((nl, r, c), F32)
    return pl.pallas_call(
        body, name=name, grid=(r // tr,),
        in_specs=[gblk] * ng + [blk] * 3 + [keep] * 4,
        out_specs=[blk] * 4, out_shape=[sds] * 4,
        input_output_aliases={ng + 3 + i: i for i in range(4)},
        compiler_params=_params(("parallel",)),
    )(*grads, w, m, v, *prev)


def _adamw(gs, w, m, v, name):
    a, s, r, c = gs.shape
    tr = _pick_rows(r, c * 4 * (s + 7), 6 << 20)

    def body(g_ref, w_ref, m_ref, v_ref, go_ref, d_ref, mo_ref, vo_ref):
        g = g_ref[0, 0].astype(F32)
        for j in range(1, s):
            g = g + g_ref[0, j].astype(F32)
        d, mn, vn = _adam_update(g, w_ref[0], m_ref[0], v_ref[0])
        go_ref[0] = g
        d_ref[0] = d
        mo_ref[0] = mn
        vo_ref[0] = vn

    blk = pl.BlockSpec((1, tr, c), lambda i, j: (i, j, 0))
    sds = jax.ShapeDtypeStruct((a, r, c), F32)
    return pl.pallas_call(
        body, name=name, grid=(a, r // tr),
        in_specs=[pl.BlockSpec((1, s, tr, c), lambda i, j: (i, 0, j, 0)), blk, blk, blk],
        out_specs=[blk, blk, blk, blk], out_shape=[sds, sds, sds, sds],
        compiler_params=_params(("parallel", "parallel")),
    )(gs, w, m, v)


def _ada_forward(c_all, w_ada, b_sh):
    nl, d, w = w_ada.shape
    tw = 512

    def body(c_ref, w_ref, b_ref, o_ref):
        cv = c_ref[...]
        act = (cv * jax.nn.sigmoid(cv)).astype(BF16)
        o_ref[0] = _nn(act, w_ref[0].astype(BF16)) + b_ref[0]

    return pl.pallas_call(
        body, name="ada_forward", grid=(nl, w // tw),
        in_specs=[pl.BlockSpec((8, d), lambda l, j: (0, 0)),
                  pl.BlockSpec((1, d, tw), lambda l, j: (l, 0, j)),
                  pl.BlockSpec((1, 1, tw), lambda l, j: (l, 0, j))],
        out_specs=pl.BlockSpec((1, 8, tw), lambda l, j: (l, 0, j)),
        out_shape=jax.ShapeDtypeStruct((nl, 8, w), F32),
        compiler_params=_params(("parallel", "parallel")),
    )(c_all, w_ada, b_sh)


def _ada_weight_grad(c_all_t, dmod):
    nl, nb, w = dmod.shape
    d = c_all_t.shape[0]
    tw = 512

    def body(c_ref, g_ref, o_ref):
        cv = c_ref[...]
        act = cv * jax.nn.sigmoid(cv)
        gv = g_ref[0]
        acc = act[:, 0:1] * gv[0:1, :]
        for b in range(1, nb):
            acc = acc + act[:, b:b + 1] * gv[b:b + 1, :]
        o_ref[0] = acc

    return pl.pallas_call(
        body, name="ada_weight_grad", grid=(nl, w // tw),
        in_specs=[pl.BlockSpec((d, nb), lambda l, j: (0, 0)),
                  pl.BlockSpec((1, nb, tw), lambda l, j: (l, 0, j))],
        out_specs=pl.BlockSpec((1, d, tw), lambda l, j: (l, 0, j)),
        out_shape=jax.ShapeDtypeStruct((nl, d, w), F32),
        compiler_params=_params(("parallel", "parallel")),
    )(c_all_t, dmod)


def _in_proj_fwd(x, vec, w_in_t, layer):
    seq = x.shape[0]
    tm = ROW_TILE

    def body(x_ref, vec_ref, w_ref, q_ref, kv_ref, u4_ref, h_ref):
        n, _, _ = _rms_fwd(x_ref[...], vec_ref[0, V_PRE_MIX:V_PRE_MIX + 1, :])
        h = (n * (1.0 + vec_ref[0, V_SC1:V_SC1 + 1, :]) + vec_ref[0, V_SH1:V_SH1 + 1, :]).astype(BF16)
        h_ref[...] = h
        proj = _nt(h, w_ref[...])
        q_ref[...] = proj[:, :ATTN_W].astype(BF16)
        kv_ref[...] = proj[:, ATTN_W:ATTN_W + 2 * KV_W].astype(BF16)
        u0 = ATTN_W + 2 * KV_W
        for j in range(4):
            u4_ref[j] = proj[:, u0 + j * LANE:u0 + (j + 1) * LANE]

    return pl.pallas_call(
        body, name="in_proj_fwd", grid=(seq // tm,),
        in_specs=[pl.BlockSpec((tm, D_MODEL), lambda i: (i, 0)),
                  pl.BlockSpec((1, 16, D_MODEL), lambda i: (layer, 0, 0)),
                  pl.BlockSpec((IN_W, D_MODEL), lambda i: (0, 0))],
        out_specs=[pl.BlockSpec((tm, ATTN_W), lambda i: (i, 0)),
                   pl.BlockSpec((tm, 2 * KV_W), lambda i: (i, 0)),
                   pl.BlockSpec((4, tm, LANE), lambda i: (0, i, 0)),
                   pl.BlockSpec((tm, D_MODEL), lambda i: (i, 0))],
        out_shape=[jax.ShapeDtypeStruct((seq, ATTN_W), BF16), jax.ShapeDtypeStruct((seq, 2 * KV_W), BF16),
                   jax.ShapeDtypeStruct((4, seq, LANE), F32), jax.ShapeDtypeStruct((seq, D_MODEL), BF16)],
        compiler_params=_params(("parallel",)),
    )(x, vec, w_in_t)


def _in_proj_bwd(dx1, dq, dkv, du4, x, vec, w_in_t, layer):
    seq = x.shape[0]
    tm = ROW_TILE

    def body(dx1_ref, dq_ref, dkv_ref, du4_ref, x_ref, vec_ref, w_ref, dx_ref, dp_ref, dvec_ref):
        i = pl.program_id(0)

        @pl.when(i == 0)
        def _():
            dvec_ref[...] = jnp.zeros_like(dvec_ref)

        dproj = jnp.concatenate([dq_ref[...], dkv_ref[...]] + [du4_ref[j] for j in range(4)], axis=1).astype(BF16)
        dp_ref[...] = dproj
        dh = _nn(dproj, w_ref[...])
        g = vec_ref[0, V_PRE_MIX:V_PRE_MIX + 1, :]
        n, xh, r = _rms_fwd(x_ref[...], g)
        dn = dh * (1.0 + vec_ref[0, V_SC1:V_SC1 + 1, :])
        dxn, dg_rows = _rms_bwd(dn, xh, r, g)
        dx_ref[...] = dx1_ref[...] + dxn
        dvec_ref[V_SH1:V_SH1 + 1, :] += _colsum(dh)
        dvec_ref[V_SC1:V_SC1 + 1, :] += _colsum(dh * n)
        dvec_ref[V_PRE_MIX:V_PRE_MIX + 1, :] += _colsum(dg_rows)

    row = pl.BlockSpec((tm, D_MODEL), lambda i: (i, 0))
    return pl.pallas_call(
        body, name="in_proj_bwd", grid=(seq // tm,),
        in_specs=[row, pl.BlockSpec((tm, ATTN_W), lambda i: (i, 0)), pl.BlockSpec((tm, 2 * KV_W), lambda i: (i, 0)),
                  pl.BlockSpec((4, tm, LANE), lambda i: (0, i, 0)), row,
                  pl.BlockSpec((1, 16, D_MODEL), lambda i: (layer, 0, 0)),
                  pl.BlockSpec((IN_W, D_MODEL), lambda i: (0, 0))],
        out_specs=[row, pl.BlockSpec((tm, IN_W), lambda i: (i, 0)), pl.BlockSpec((16, D_MODEL), lambda i: (0, 0))],
        out_shape=[jax.ShapeDtypeStruct((seq, D_MODEL), F32), jax.ShapeDtypeStruct((seq, IN_W), BF16),
                   jax.ShapeDtypeStruct((16, D_MODEL), F32)],
        compiler_params=_params(("arbitrary",)),
    )(dx1, dq, dkv, du4, x, vec, w_in_t)


def _heads(attn_ref, s4_ref, vec512_ref):
    ga = vec512_ref[0, H_ATTN_G:H_ATTN_G + 1, :]
    gs = vec512_ref[0, H_SSM_G:H_SSM_G + 1, :]
    sv = jnp.concatenate([s4_ref[j] for j in range(4)], axis=1)
    na, ah, ar = _rms_fwd(attn_ref[...], ga)
    ns, sh, sr = _rms_fwd(sv, gs)
    return jnp.concatenate([na, ns], axis=1), (ah, ar, ga), (sh, sr, gs)


def _out_proj_fwd(x, attn, s4, vec, vec512, w_out, layer):
    seq = x.shape[0]
    tm = ROW_TILE

    def body(x_ref, attn_ref, s4_ref, vec_ref, vec512_ref, w_ref, x1_ref):
        heads, _, _ = _heads(attn_ref, s4_ref, vec512_ref)
        mixed = _nn(heads.astype(BF16), w_ref[...])
        nm, _, _ = _rms_fwd(mixed, vec_ref[0, V_POST_MIX:V_POST_MIX + 1, :])
        x1_ref[...] = x_ref[...] + vec_ref[0, V_G1:V_G1 + 1, :] * nm

    row = pl.BlockSpec((tm, D_MODEL), lambda i: (i, 0))
    return pl.pallas_call(
        body, name="out_proj_fwd", grid=(seq // tm,),
        in_specs=[row, pl.BlockSpec((tm, ATTN_W), lambda i: (i, 0)), pl.BlockSpec((4, tm, LANE), lambda i: (0, i, 0)),
                  pl.BlockSpec((1, 16, D_MODEL), lambda i: (layer, 0, 0)),
                  pl.BlockSpec((1, 8, SSM_W), lambda i: (layer, 0, 0)),
                  pl.BlockSpec((D_MODEL, D_MODEL), lambda i: (0, 0))],
        out_specs=row, out_shape=jax.ShapeDtypeStruct((seq, D_MODEL), F32),
        compiler_params=_params(("parallel",)),
    )(x, attn, s4, vec, vec512, w_out)


def _out_proj_bwd(dx1, attn, s4, vec, vec512, w_out, layer):
    seq = dx1.shape[0]
    tm = ROW_TILE

    def body(dx1_ref, attn_ref, s4_ref, vec_ref, vec512_ref, w_ref,
             dattn_ref, ds4_ref, heads_ref, dmixed_ref, dvec_ref, dvec512_ref):
        i = pl.program_id(0)

        @pl.when(i == 0)
        def _():
            dvec_ref[...] = jnp.zeros_like(dvec_ref)
            dvec512_ref[...] = jnp.zeros_like(dvec512_ref)

        heads, (ah, ar, ga), (sh, sr, gs) = _heads(attn_ref, s4_ref, vec512_ref)
        hb = heads.astype(BF16)
        heads_ref[...] = hb
        gm = vec_ref[0, V_POST_MIX:V_POST_MIX + 1, :]
        nm, mh, mr = _rms_fwd(_nn(hb, w_ref[...]), gm)
        dx1v = dx1_ref[...]
        dvec_ref[V_G1:V_G1 + 1, :] += _colsum(dx1v * nm)
        dmixed, dgm_rows = _rms_bwd(dx1v * vec_ref[0, V_G1:V_G1 + 1, :], mh, mr, gm)
        dvec_ref[V_POST_MIX:V_POST_MIX + 1, :] += _colsum(dgm_rows)
        dmb = dmixed.astype(BF16)
        dmixed_ref[...] = dmb
        dheads = _nt(dmb, w_ref[...])
        dattn, dga_rows = _rms_bwd(dheads[:, :ATTN_W], ah, ar, ga)
        ds, dgs_rows = _rms_bwd(dheads[:, ATTN_W:], sh, sr, gs)
        dattn_ref[...] = dattn
        for j in range(4):
            ds4_ref[j] = ds[:, j * LANE:(j + 1) * LANE]
        dvec512_ref[H_ATTN_G:H_ATTN_G + 1, :] += _colsum(dga_rows)
        dvec512_ref[H_SSM_G:H_SSM_G + 1, :] += _colsum(dgs_rows)

    row = pl.BlockSpec((tm, D_MODEL), lambda i: (i, 0))
    return pl.pallas_call(
        body, name="out_proj_bwd", grid=(seq // tm,),
        in_specs=[row, pl.BlockSpec((tm, ATTN_W), lambda i: (i, 0)), pl.BlockSpec((4, tm, LANE), lambda i: (0, i, 0)),
                  pl.BlockSpec((1, 16, D_MODEL), lambda i: (layer, 0, 0)),
                  pl.BlockSpec((1, 8, SSM_W), lambda i: (layer, 0, 0)),
                  pl.BlockSpec((D_MODEL, D_MODEL), lambda i: (0, 0))],
        out_specs=[pl.BlockSpec((tm, ATTN_W), lambda i: (i, 0)), pl.BlockSpec((4, tm, LANE), lambda i: (0, i, 0)),
                   row, row, pl.BlockSpec((16, D_MODEL), lambda i: (0, 0)), pl.BlockSpec((8, SSM_W), lambda i: (0, 0))],
        out_shape=[jax.ShapeDtypeStruct((seq, ATTN_W), F32), jax.ShapeDtypeStruct((4, seq, LANE), F32),
                   jax.ShapeDtypeStruct((seq, D_MODEL), BF16), jax.ShapeDtypeStruct((seq, D_MODEL), BF16),
                   jax.ShapeDtypeStruct((16, D_MODEL), F32), jax.ShapeDtypeStruct((8, SSM_W), F32)],
        compiler_params=_params(("arbitrary",)),
    )(dx1, attn, s4, vec, vec512, w_out)


def _mlp_fwd(x1, vec, w_in_t, w_out, layer):
    seq = x1.shape[0]
    tm = ROW_TILE

    def body(x1_ref, vec_ref, wi_hbm, wo_hbm, x2_ref, wi, wo, sems):
        _load_once(pl.program_id(0), [(wi_hbm, wi), (wo_hbm, wo)], sems)
        x1v = x1_ref[...]
        n, _, _ = _rms_fwd(x1v, vec_ref[0, V_PRE_MLP:V_PRE_MLP + 1, :])
        h = (n * (1.0 + vec_ref[0, V_SC2:V_SC2 + 1, :]) + vec_ref[0, V_SH2:V_SH2 + 1, :]).astype(BF16)
        a = _nt(h, wi[...])
        r = jnp.square(jnp.maximum(a, 0.0)).astype(BF16)
        nf, _, _ = _rms_fwd(_nn(r, wo[...]), vec_ref[0, V_POST_MLP:V_POST_MLP + 1, :])
        x2_ref[...] = x1v + vec_ref[0, V_G2:V_G2 + 1, :] * nf

    row = pl.BlockSpec((tm, D_MODEL), lambda i: (i, 0))
    return pl.pallas_call(
        body, name="mlp_fwd", grid=(seq // tm,),
        in_specs=[row, pl.BlockSpec((1, 16, D_MODEL), lambda i: (layer, 0, 0)), HBM, HBM],
        out_specs=row, out_shape=jax.ShapeDtypeStruct((seq, D_MODEL), F32),
        scratch_shapes=[pltpu.VMEM((D_FF, D_MODEL), BF16), pltpu.VMEM((D_FF, D_MODEL), BF16),
                        pltpu.SemaphoreType.DMA((2,))],
        compiler_params=_params(("arbitrary",)),
    )(x1, vec, w_in_t, w_out)


def _mlp_bwd(dx2, x1, vec, w_in_t, w_out, layer):
    seq = x1.shape[0]
    tm = ROW_TILE

    def body(dx2_ref, x1_ref, vec_ref, wi_hbm, wo_hbm, dx1_ref, h_ref, da_ref, r_ref, df_ref, dvec_ref, wi, wo, sems):
        i = pl.program_id(0)
        _load_once(i, [(wi_hbm, wi), (wo_hbm, wo)], sems)

        @pl.when(i == 0)
        def _():
            dvec_ref[...] = jnp.zeros_like(dvec_ref)

        g_pre = vec_ref[0, V_PRE_MLP:V_PRE_MLP + 1, :]
        g_post = vec_ref[0, V_POST_MLP:V_POST_MLP + 1, :]
        sc2 = vec_ref[0, V_SC2:V_SC2 + 1, :]
        n, xh, xr = _rms_fwd(x1_ref[...], g_pre)
        h = (n * (1.0 + sc2) + vec_ref[0, V_SH2:V_SH2 + 1, :]).astype(BF16)
        h_ref[...] = h
        a = _nt(h, wi[...])
        relu = jnp.maximum(a, 0.0)
        r = jnp.square(relu).astype(BF16)
        r_ref[...] = r
        nf, fh, fr = _rms_fwd(_nn(r, wo[...]), g_post)
        dx2v = dx2_ref[...]
        dvec_ref[V_G2:V_G2 + 1, :] += _colsum(dx2v * nf)
        df, dgp_rows = _rms_bwd(dx2v * vec_ref[0, V_G2:V_G2 + 1, :], fh, fr, g_post)
        dvec_ref[V_POST_MLP:V_POST_MLP + 1, :] += _colsum(dgp_rows)
        dfb = df.astype(BF16)
        df_ref[...] = dfb
        da = (_nt(dfb, wo[...]) * (2.0 * relu)).astype(BF16)
        da_ref[...] = da
        dh = _nn(da, wi[...])
        dvec_ref[V_SH2:V_SH2 + 1, :] += _colsum(dh)
        dvec_ref[V_SC2:V_SC2 + 1, :] += _colsum(dh * n)
        dxn, dg_rows = _rms_bwd(dh * (1.0 + sc2), xh, xr, g_pre)
        dvec_ref[V_PRE_MLP:V_PRE_MLP + 1, :] += _colsum(dg_rows)
        dx1_ref[...] = dx2v + dxn

    row = pl.BlockSpec((tm, D_MODEL), lambda i: (i, 0))
    wide = pl.BlockSpec((tm, D_FF), lambda i: (i, 0))
    return pl.pallas_call(
        body, name="mlp_bwd", grid=(seq // tm,),
        in_specs=[row, row, pl.BlockSpec((1, 16, D_MODEL), lambda i: (layer, 0, 0)), HBM, HBM],
        out_specs=[row, row, wide, wide, row, pl.BlockSpec((16, D_MODEL), lambda i: (0, 0))],
        out_shape=[jax.ShapeDtypeStruct((seq, D_MODEL), F32), jax.ShapeDtypeStruct((seq, D_MODEL), BF16),
                   jax.ShapeDtypeStruct((seq, D_FF), BF16), jax.ShapeDtypeStruct((seq, D_FF), BF16),
                   jax.ShapeDtypeStruct((seq, D_MODEL), BF16), jax.ShapeDtypeStruct((16, D_MODEL), F32)],
        scratch_shapes=[pltpu.VMEM((D_FF, D_MODEL), BF16), pltpu.VMEM((D_FF, D_MODEL), BF16),
                        pltpu.SemaphoreType.DMA((2,))],
        compiler_params=_params(("arbitrary",)),
    )(dx2, x1, vec, w_in_t, w_out)


def _loss_head(y, target):
    seq = y.shape[0]
    tm = ROW_TILE

    def body(y_ref, t_ref, dy_ref, part_ref):
        e = y_ref[...] - t_ref[...]
        dy_ref[...] = e * (1.0 / D_MODEL)
        tot = jnp.sum(jnp.sum(e * e, axis=1, keepdims=True), axis=0, keepdims=True) * (0.5 / D_MODEL)
        part_ref[0] = jnp.broadcast_to(tot, (SUBLANE, LANE))

    row = pl.BlockSpec((tm, D_MODEL), lambda i: (i, 0))
    return pl.pallas_call(
        body, name="loss_head", grid=(seq // tm,),
        in_specs=[row, row],
        out_specs=[row, pl.BlockSpec((1, SUBLANE, LANE), lambda i: (i, 0, 0))],
        out_shape=[jax.ShapeDtypeStruct((seq, D_MODEL), F32), jax.ShapeDtypeStruct((seq // tm, SUBLANE, LANE), F32)],
        compiler_params=_params(("parallel",)),
    )(y, target)


def _matmul_tn(a, b, out_dtype, name, pieces=1):
    kk, m = a.shape
    n = b.shape[1]
    tm = min(m, 512)
    tn = n // pieces if pieces > 1 else min(n, 1280)
    tk = min(kk, 2048)
    nk = kk // tk

    def body(a_ref, b_ref, o_ref, acc):
        k = pl.program_id(2)

        @pl.when(k == 0)
        def _():
            acc[...] = jnp.zeros_like(acc)

        acc[...] += _tn(a_ref[...], b_ref[...])

        @pl.when(k == nk - 1)
        def _():
            if pieces > 1:
                o_ref[0] = acc[...].astype(out_dtype)
            else:
                o_ref[...] = acc[...].astype(out_dtype)

    if pieces > 1:
        out_spec = pl.BlockSpec((1, tm, tn), lambda i, j, k: (j, i, 0))
        out_shape = jax.ShapeDtypeStruct((pieces, m, tn), out_dtype)
    else:
        out_spec = pl.BlockSpec((tm, tn), lambda i, j, k: (i, j))
        out_shape = jax.ShapeDtypeStruct((m, n), out_dtype)
    return pl.pallas_call(
        body, name=name, grid=(m // tm, n // tn, nk),
        in_specs=[pl.BlockSpec((tk, tm), lambda i, j, k: (k, i)), pl.BlockSpec((tk, tn), lambda i, j, k: (k, j))],
        out_specs=out_spec, out_shape=out_shape,
        scratch_shapes=[pltpu.VMEM((tm, tn), F32)],
        compiler_params=_params(("parallel", "parallel", "arbitrary")),
    )(a, b)


GROUP_ROWS = Q_PER_KV * BLOCK


def _stack_heads(t, h):
    return jnp.concatenate([t[:, (h * Q_PER_KV + g) * HEAD_DIM:(h * Q_PER_KV + g + 1) * HEAD_DIM]
                            for g in range(Q_PER_KV)], axis=0)


def _head_columns(vals):
    return jnp.concatenate([jnp.full((BLOCK, 1), v, F32) for v in vals], axis=0)


def _attn_probs(i, q4, kband, slopes, sinks):
    rr = lax.broadcasted_iota(jnp.int32, (GROUP_ROWS, 2 * BLOCK), 0) % BLOCK
    jj = lax.broadcasted_iota(jnp.int32, (GROUP_ROWS, 2 * BLOCK), 1)
    diff = BLOCK + rr - jj
    valid = (diff >= 0) & (diff < WINDOW) & ((jj >= BLOCK) | (i > 0))
    sink = _head_columns(sinks)
    s = _nt(q4, kband) * (HEAD_DIM ** -0.5)
    s = jnp.where(valid, s - _head_columns(slopes) * diff.astype(F32), NEG_INF)
    m = jnp.maximum(jnp.max(s, axis=1, keepdims=True), sink)
    p = jnp.exp(s - m)
    ps = jnp.exp(sink - m)
    inv = 1.0 / (jnp.sum(p, axis=1, keepdims=True) + ps)
    return p * inv, ps * inv


def _bands(kvp, kvc, h):
    kband = jnp.concatenate([kvp[:, h * HEAD_DIM:(h + 1) * HEAD_DIM], kvc[:, h * HEAD_DIM:(h + 1) * HEAD_DIM]], axis=0)
    v0 = KV_W + h * HEAD_DIM
    vband = jnp.concatenate([kvp[:, v0:v0 + HEAD_DIM], kvc[:, v0:v0 + HEAD_DIM]], axis=0)
    return kband, vband


def _attn_fwd(q, kv, sinks):
    seq = q.shape[0]
    nb = seq // BLOCK
    slopes = _alibi_slopes()

    def body(sink_ref, q_ref, kvp_ref, kvc_ref, o_ref):
        i = pl.program_id(0)
        qv, kvp, kvc = q_ref[...], kvp_ref[...], kvc_ref[...]
        for h in range(N_KV):
            kband, vband = _bands(kvp, kvc, h)
            heads = range(h * Q_PER_KV, (h + 1) * Q_PER_KV)
            pr, _ = _attn_probs(i, _stack_heads(qv, h), kband, [slopes[hq] for hq in heads],
                                [sink_ref[hq] for hq in heads])
            o4 = _nn(pr.astype(BF16), vband)
            for g, hq in enumerate(heads):
                o_ref[:, hq * HEAD_DIM:(hq + 1) * HEAD_DIM] = o4[g * BLOCK:(g + 1) * BLOCK, :]

    return pl.pallas_call(
        body, name="attn_fwd", grid=(nb,),
        in_specs=[pl.BlockSpec(memory_space=pltpu.SMEM),
                  pl.BlockSpec((BLOCK, ATTN_W), lambda i: (i, 0)),
                  pl.BlockSpec((BLOCK, 2 * KV_W), lambda i: (jnp.maximum(i - 1, 0), 0)),
                  pl.BlockSpec((BLOCK, 2 * KV_W), lambda i: (i, 0))],
        out_specs=pl.BlockSpec((BLOCK, ATTN_W), lambda i: (i, 0)),
        out_shape=jax.ShapeDtypeStruct((seq, ATTN_W), F32),
        compiler_params=_params(("parallel",)),
    )(sinks, q, kv, kv)


def _attn_bwd(q, kv, sinks, dout):
    seq = q.shape[0]
    nb = seq // BLOCK
    slopes = _alibi_slopes()
    scale = HEAD_DIM ** -0.5

    def body(sink_ref, q_ref, kvp_ref, kvc_ref, do_ref, dq_ref, dkv_ref, dsk_ref, prev):
        step = pl.program_id(0)
        i = nb - 1 - step

        @pl.when(step == 0)
        def _():
            prev[...] = jnp.zeros_like(prev)

        qv, kvp, kvc = q_ref[...], kvp_ref[...], kvc_ref[...]
        dov = do_ref[...].astype(BF16)
        dk, dv, dsk = [], [], []
        for h in range(N_KV):
            kband, vband = _bands(kvp, kvc, h)
            heads = range(h * Q_PER_KV, (h + 1) * Q_PER_KV)
            q4, do4 = _stack_heads(qv, h), _stack_heads(dov, h)
            pr, ps = _attn_probs(i, q4, kband, [slopes[hq] for hq in heads], [sink_ref[hq] for hq in heads])
            dp = _nt(do4, vband)
            delta = jnp.sum(pr * dp, axis=1, keepdims=True)
            ds = (pr * (dp - delta) * scale).astype(BF16)
            dq4 = _nn(ds, kband)
            sink_rows = ps * delta
            for g, hq in enumerate(heads):
                rows = slice(g * BLOCK, (g + 1) * BLOCK)
                dq_ref[:, hq * HEAD_DIM:(hq + 1) * HEAD_DIM] = dq4[rows, :]
                dsk.append(jnp.broadcast_to(-_colsum(sink_rows[rows, :]), (1, LANE)))
            dk.append(_tn(ds, q4))
            dv.append(_tn(pr.astype(BF16), do4))
        band = jnp.concatenate(dk + dv, axis=1)
        dkv_ref[...] = band[BLOCK:, :] + prev[...]
        prev[...] = band[:BLOCK, :]
        dsk_ref[0] = jnp.concatenate(dsk, axis=0)

    return pl.pallas_call(
        body, name="attn_bwd", grid=(nb,),
        in_specs=[pl.BlockSpec(memory_space=pltpu.SMEM),
                  pl.BlockSpec((BLOCK, ATTN_W), lambda s: (nb - 1 - s, 0)),
                  pl.BlockSpec((BLOCK, 2 * KV_W), lambda s: (jnp.maximum(nb - 2 - s, 0), 0)),
                  pl.BlockSpec((BLOCK, 2 * KV_W), lambda s: (nb - 1 - s, 0)),
                  pl.BlockSpec((BLOCK, ATTN_W), lambda s: (nb - 1 - s, 0))],
        out_specs=[pl.BlockSpec((BLOCK, ATTN_W), lambda s: (nb - 1 - s, 0)),
                   pl.BlockSpec((BLOCK, 2 * KV_W), lambda s: (nb - 1 - s, 0)),
                   pl.BlockSpec((1, N_Q, LANE), lambda s: (nb - 1 - s, 0, 0))],
        out_shape=[jax.ShapeDtypeStruct((seq, ATTN_W), F32), jax.ShapeDtypeStruct((seq, 2 * KV_W), F32),
                   jax.ShapeDtypeStruct((nb, N_Q, LANE), F32)],
        scratch_shapes=[pltpu.VMEM((BLOCK, 2 * KV_W), F32)],
        compiler_params=_params(("arbitrary",)),
    )(sinks, q, kv, kv, dout)


def _discretize(lr, li, ldt, br, bi):
    dt = jnp.exp(ldt)
    mag = jnp.exp(lr * dt)
    ang = li * dt
    ab_r = mag * jnp.cos(ang)
    ab_i = mag * jnp.sin(ang)
    nr = ab_r - 1.0
    ni = ab_i
    den = lr * lr + li * li
    f_r = (nr * lr + ni * li) / den
    f_i = (ni * lr - nr * li) / den
    return ab_r, ab_i, f_r * br - f_i * bi, f_r * bi + f_i * br


def _ssm_prepare(lr, li, ldt, br, bi):
    n = lr.shape[0]
    tn = N_CH
    col = pl.BlockSpec((tn, 1), lambda i: (i, 0))
    mat = pl.BlockSpec((tn, GROUP_W), lambda i: (i, 0))

    def body(lr_ref, li_ref, ldt_ref, br_ref, bi_ref, ar_ref, ai_ref, bbr_ref, bbi_ref):
        ar, ai, bbr, bbi = _discretize(lr_ref[...], li_ref[...], ldt_ref[...], br_ref[...], bi_ref[...])
        ar_ref[...] = ar
        ai_ref[...] = ai
        bbr_ref[...] = bbr
        bbi_ref[...] = bbi

    cs = jax.ShapeDtypeStruct((n, 1), F32)
    ms = jax.ShapeDtypeStruct((n, GROUP_W), F32)
    return pl.pallas_call(
        body, name="ssm_prepare", grid=(n // tn,),
        in_specs=[col, col, col, mat, mat], out_specs=[col, col, mat, mat], out_shape=[cs, cs, ms, ms],
        compiler_params=_params(("parallel",)),
    )(lr, li, ldt, br, bi)


def _ssm_prepare_bwd(lr, li, ldt, br, bi, dar, dai, dbbr, dbbi):
    n = lr.shape[0]
    tn = N_CH
    col = pl.BlockSpec((tn, 1), lambda i: (i, 0))
    mat = pl.BlockSpec((tn, GROUP_W), lambda i: (i, 0))

    def body(lr_ref, li_ref, ldt_ref, br_ref, bi_ref, dar_ref, dai_ref, dbbr_ref, dbbi_ref,
             dlr_ref, dli_ref, dldt_ref, dbr_ref, dbi_ref):
        _, vjp = jax.vjp(_discretize, lr_ref[...], li_ref[...], ldt_ref[...], br_ref[...], bi_ref[...])
        dlr, dli, dldt, dbr, dbi = vjp((dar_ref[...], dai_ref[...], dbbr_ref[...], dbbi_ref[...]))
        dlr_ref[...] = dlr
        dli_ref[...] = dli
        dldt_ref[...] = dldt
        dbr_ref[...] = dbr
        dbi_ref[...] = dbi

    cs = jax.ShapeDtypeStruct((n, 1), F32)
    ms = jax.ShapeDtypeStruct((n, GROUP_W), F32)
    return pl.pallas_call(
        body, name="ssm_prepare_bwd", grid=(n // tn,),
        in_specs=[col, col, col, mat, mat, col, col, mat, mat],
        out_specs=[col, col, col, mat, mat], out_shape=[cs, cs, cs, ms, ms],
        compiler_params=_params(("parallel",)),
    )(lr, li, ldt, br, bi, dar, dai, dbbr, dbbi)


def _load_slabs(src4_ref, dst):
    for s in range(STEPS):
        dst[s * SEGS:(s + 1) * SEGS, :] = jnp.concatenate(
            [src4_ref[j, pl.ds(s, SEGS, stride=STEPS), :] for j in range(4)], axis=1)


def _store_slabs(src, dst4_ref):
    for s in range(STEPS):
        for j in range(4):
            dst4_ref[j, pl.ds(s, SEGS, stride=STEPS), :] = src[s * SEGS:(s + 1) * SEGS, j * LANE:(j + 1) * LANE]


def _power_table(ar_ref, ai_ref, pwr, pwi):
    ar, ai = ar_ref[0], ai_ref[0]
    pr, pi = ar, ai
    pwr[0:1, :] = pr
    pwi[0:1, :] = pi
    for k in range(1, STEPS):
        pr, pi = pr * ar - pi * ai, pr * ai + pi * ar
        pwr[k:k + 1, :] = pr
        pwi[k:k + 1, :] = pi


def _scan_states(ubf, ar_ref, ai_ref, bbr_ref, bbi_ref, pwr, pwi, cin_r, cin_i, hr, hi):
    for k in range(2):
        rows = slice(k * 256, (k + 1) * 256)
        cols = slice(k * HALF_CH, (k + 1) * HALF_CH)
        hr[:, cols] = _nn(ubf[:, rows], bbr_ref[0, k])
        hi[:, cols] = _nn(ubf[:, rows], bbi_ref[0, k])
    for st in range(N_CH // STRIP):
        cs = slice(st * STRIP, (st + 1) * STRIP)
        arb = jnp.broadcast_to(ar_ref[0, :, cs], (SEGS, STRIP))
        aib = jnp.broadcast_to(ai_ref[0, :, cs], (SEGS, STRIP))

        def step(s, carry, cs=cs, arb=arb, aib=aib):
            cr, ci = carry
            rows = pl.ds(pl.multiple_of(s * SEGS, SEGS), SEGS)
            nr = arb * cr - aib * ci + hr[rows, cs]
            ni = arb * ci + aib * cr + hi[rows, cs]
            hr[rows, cs] = nr
            hi[rows, cs] = ni
            return nr, ni

        zero = jnp.zeros((SEGS, STRIP), F32)
        lax.fori_loop(0, STEPS, step, (zero, zero))
    last = slice((STEPS - 1) * SEGS, STEPS * SEGS)
    end_r, end_i = hr[last, :], hi[last, :]
    a64r, a64i = pwr[STEPS - 1:STEPS, :], pwi[STEPS - 1:STEPS, :]
    cr, ci = cin_r, cin_i
    rows_r, rows_i = [], []
    for j in range(SEGS):
        rows_r.append(cr)
        rows_i.append(ci)
        cr, ci = (a64r * cr - a64i * ci + end_r[j:j + 1, :], a64r * ci + a64i * cr + end_i[j:j + 1, :])
    cm_r, cm_i = jnp.concatenate(rows_r, axis=0), jnp.concatenate(rows_i, axis=0)
    for st in range(N_CH // STRIP):
        cs = slice(st * STRIP, (st + 1) * STRIP)
        cmr, cmi = cm_r[:, cs], cm_i[:, cs]

        def fix(s, carry, cs=cs, cmr=cmr, cmi=cmi):
            rows = pl.ds(pl.multiple_of(s * SEGS, SEGS), SEGS)
            pr, pi = pwr[pl.ds(s, 1), cs], pwi[pl.ds(s, 1), cs]
            hr[rows, cs] = hr[rows, cs] + (pr * cmr - pi * cmi)
            hi[rows, cs] = hi[rows, cs] + (pr * cmi + pi * cmr)
            return carry

        lax.fori_loop(0, STEPS, fix, 0)
    return (cm_r, cm_i), (cr, ci)


def _ssm_outputs(u, hr, hi, crt_ref, cit_ref, vec512_ref, wg_ref):
    ys = []
    for k in range(2):
        cols = slice(k * HALF_CH, (k + 1) * HALF_CH)
        ys.append(_nn(hr[:, cols].astype(BF16), crt_ref[0, k]) - _nn(hi[:, cols].astype(BF16), cit_ref[0, k]))
    y = jnp.concatenate(ys, axis=1) + vec512_ref[0, H_DSKIP:H_DSKIP + 1, :] * u
    z, t = _gelu(y)
    gate = jax.nn.sigmoid(_nn(z.astype(BF16), wg_ref[...]) + vec512_ref[0, H_BGLU:H_BGLU + 1, :])
    return y, z, t, gate


def _ssm_specs(layer, nck, rev):
    def chunk(i):
        return nck - 1 - i if rev else i

    return [pl.BlockSpec((4, CHUNK, LANE), lambda i: (0, chunk(i), 0)),
            pl.BlockSpec((1, 1, N_CH), lambda i: (layer, 0, 0)),
            pl.BlockSpec((1, 1, N_CH), lambda i: (layer, 0, 0)),
            pl.BlockSpec((1, 2, 256, HALF_CH), lambda i: (layer, 0, 0, 0)),
            pl.BlockSpec((1, 2, 256, HALF_CH), lambda i: (layer, 0, 0, 0)),
            pl.BlockSpec((1, 2, HALF_CH, 256), lambda i: (layer, 0, 0, 0)),
            pl.BlockSpec((1, 2, HALF_CH, 256), lambda i: (layer, 0, 0, 0)),
            pl.BlockSpec((1, 8, SSM_W), lambda i: (layer, 0, 0)),
            pl.BlockSpec((SSM_W, SSM_W), lambda i: (0, 0))]


def _ssm_fwd(u4, a_r, a_i, bb_r, bb_i, c_rt, c_it, vec512, w_glu, layer):
    seq = u4.shape[1]
    nck = seq // CHUNK

    def body(u4_ref, ar_ref, ai_ref, bbr_ref, bbi_ref, crt_ref, cit_ref, vec512_ref, wg_ref,
             s4_ref, hcr_ref, hci_ref, hr, hi, pwr, pwi, car, cai, ubuf, obuf):
        i = pl.program_id(0)

        @pl.when(i == 0)
        def _():
            car[...] = jnp.zeros_like(car)
            cai[...] = jnp.zeros_like(cai)
            _power_table(ar_ref, ai_ref, pwr, pwi)

        _load_slabs(u4_ref, ubuf)
        u = ubuf[...]
        cin_r, cin_i = car[...], cai[...]
        hcr_ref[0] = jnp.broadcast_to(cin_r, (SEGS, N_CH))
        hci_ref[0] = jnp.broadcast_to(cin_i, (SEGS, N_CH))
        _, (er, ei) = _scan_states(u.astype(BF16), ar_ref, ai_ref, bbr_ref, bbi_ref, pwr, pwi, cin_r, cin_i, hr, hi)
        car[...] = er
        cai[...] = ei
        _, z, _, gate = _ssm_outputs(u, hr, hi, crt_ref, cit_ref, vec512_ref, wg_ref)
        obuf[...] = z * gate
        _store_slabs(obuf, s4_ref)

    return pl.pallas_call(
        body, name="ssm_fwd", grid=(nck,),
        in_specs=_ssm_specs(layer, nck, False),
        out_specs=[pl.BlockSpec((4, CHUNK, LANE), lambda i: (0, i, 0)),
                   pl.BlockSpec((1, SEGS, N_CH), lambda i: (i, 0, 0)),
                   pl.BlockSpec((1, SEGS, N_CH), lambda i: (i, 0, 0))],
        out_shape=[jax.ShapeDtypeStruct((4, seq, LANE), F32), jax.ShapeDtypeStruct((nck, SEGS, N_CH), F32),
                   jax.ShapeDtypeStruct((nck, SEGS, N_CH), F32)],
        scratch_shapes=[pltpu.VMEM((CHUNK, N_CH), F32), pltpu.VMEM((CHUNK, N_CH), F32),
                        pltpu.VMEM((STEPS, N_CH), F32), pltpu.VMEM((STEPS, N_CH), F32),
                        pltpu.VMEM((1, N_CH), F32), pltpu.VMEM((1, N_CH), F32),
                        pltpu.VMEM((CHUNK, SSM_W), F32), pltpu.VMEM((CHUNK, SSM_W), F32)],
        compiler_params=_params(("arbitrary",)),
    )(u4, a_r, a_i, bb_r, bb_i, c_rt, c_it, vec512, w_glu)


def _ssm_bwd(u4, ds4, hc_r, hc_i, a_r, a_i, bb_r, bb_i, c_rt, c_it, vec512, w_glu, layer):
    seq = u4.shape[1]
    nck = seq // CHUNK

    def body(u4_ref, ar_ref, ai_ref, bbr_ref, bbi_ref, crt_ref, cit_ref, vec512_ref, wg_ref, ds4_ref, hcr_ref, hci_ref,
             du4_ref, dbbr_out, dbbi_out, dcrt_out, dcit_out, dar_ref, dai_ref, dwg_ref, dvec_ref,
             hr, hi, gr, gi, pwr, pwi, gcr, gci, accr, acci, ubuf, dbuf, dbbr_ref, dbbi_ref, dcrt_ref, dcit_ref):
        i = pl.program_id(0)

        @pl.when(i == 0)
        def _():
            for ref in (gcr, gci, accr, acci, dbbr_ref, dbbi_ref, dcrt_ref, dcit_ref, dwg_ref, dvec_ref):
                ref[...] = jnp.zeros_like(ref)
            _power_table(ar_ref, ai_ref, pwr, pwi)

        _load_slabs(u4_ref, ubuf)
        u = ubuf[...]
        ubf = u.astype(BF16)
        cin_r, cin_i = hcr_ref[0, 0:1, :], hci_ref[0, 0:1, :]
        (cm_r, cm_i), _ = _scan_states(ubf, ar_ref, ai_ref, bbr_ref, bbi_ref, pwr, pwi, cin_r, cin_i, hr, hi)
        y, z, t, gate = _ssm_outputs(u, hr, hi, crt_ref, cit_ref, vec512_ref, wg_ref)
        _load_slabs(ds4_ref, dbuf)
        ds = dbuf[...]
        da = ds * z * gate * (1.0 - gate)
        dab = da.astype(BF16)
        dz = ds * gate + _nt(dab, wg_ref[...])
        dwg_ref[...] += _tn(z.astype(BF16), dab)
        dvec_ref[H_BGLU:H_BGLU + 1, :] += _colsum(da)
        dy = dz * _gelu_grad(y, t)
        dvec_ref[H_DSKIP:H_DSKIP + 1, :] += _colsum(dy * u)
        du_skip = dy * vec512_ref[0, H_DSKIP:H_DSKIP + 1, :]
        dyb = dy.astype(BF16)
        for k in range(2):
            rows = slice(k * 256, (k + 1) * 256)
            cols = slice(k * HALF_CH, (k + 1) * HALF_CH)
            dcrt_ref[k] += _tn(hr[:, cols].astype(BF16), dyb[:, rows])
            dcit_ref[k] -= _tn(hi[:, cols].astype(BF16), dyb[:, rows])
            gr[:, cols] = _nt(dyb[:, rows], crt_ref[0, k])
            gi[:, cols] = -_nt(dyb[:, rows], cit_ref[0, k])
        for st in range(N_CH // STRIP):
            cs = slice(st * STRIP, (st + 1) * STRIP)
            arb = jnp.broadcast_to(ar_ref[0, :, cs], (SEGS, STRIP))
            aib = jnp.broadcast_to(ai_ref[0, :, cs], (SEGS, STRIP))

            def step(k, carry, cs=cs, arb=arb, aib=aib):
                cr, ci = carry
                rows = pl.ds(pl.multiple_of((STEPS - 1 - k) * SEGS, SEGS), SEGS)
                nr = gr[rows, cs] + (arb * cr + aib * ci)
                ni = gi[rows, cs] + (arb * ci - aib * cr)
                gr[rows, cs] = nr
                gi[rows, cs] = ni
                return nr, ni

            zero = jnp.zeros((SEGS, STRIP), F32)
            lax.fori_loop(0, STEPS, step, (zero, zero))
        first_r, first_i = gr[0:SEGS, :], gi[0:SEGS, :]
        a64r, a64i = pwr[STEPS - 1:STEPS, :], pwi[STEPS - 1:STEPS, :]
        dr_, di_ = gcr[...], gci[...]
        rows_r, rows_i = [None] * SEGS, [None] * SEGS
        for j in range(SEGS - 1, -1, -1):
            rows_r[j], rows_i[j] = dr_, di_
            dr_, di_ = (first_r[j:j + 1, :] + (a64r * dr_ + a64i * di_), first_i[j:j + 1, :] + (a64r * di_ - a64i * dr_))
        gcr[...] = dr_
        gci[...] = di_
        dm_r, dm_i = jnp.concatenate(rows_r, axis=0), jnp.concatenate(rows_i, axis=0)
        for st in range(N_CH // STRIP):
            cs = slice(st * STRIP, (st + 1) * STRIP)
            dmr, dmi = dm_r[:, cs], dm_i[:, cs]

            def fixed(s, cs=cs, dmr=dmr, dmi=dmi):
                rows = pl.ds(pl.multiple_of(s * SEGS, SEGS), SEGS)
                pr, pi = pwr[pl.ds(STEPS - 1 - s, 1), cs], pwi[pl.ds(STEPS - 1 - s, 1), cs]
                g_r = gr[rows, cs] + (pr * dmr + pi * dmi)
                g_i = gi[rows, cs] + (pr * dmi - pi * dmr)
                gr[rows, cs] = g_r
                gi[rows, cs] = g_i
                return g_r, g_i

            g_r, g_i = fixed(jnp.int32(0))
            acc0 = (g_r * cm_r[:, cs] + g_i * cm_i[:, cs], g_i * cm_r[:, cs] - g_r * cm_i[:, cs])

            def step(s, carry, cs=cs, fixed=fixed):
                sr, si = carry
                g_r, g_i = fixed(s)
                prev = pl.ds(pl.multiple_of((s - 1) * SEGS, SEGS), SEGS)
                hpr, hpi = hr[prev, cs], hi[prev, cs]
                return sr + (g_r * hpr + g_i * hpi), si + (g_i * hpr - g_r * hpi)

            sr, si = lax.fori_loop(1, STEPS, step, acc0)
            accr[:, cs] += sr
            acci[:, cs] += si
        grb, gib = gr[...].astype(BF16), gi[...].astype(BF16)
        dus = []
        for k in range(2):
            rows = slice(k * 256, (k + 1) * 256)
            cols = slice(k * HALF_CH, (k + 1) * HALF_CH)
            dus.append(_nt(grb[:, cols], bbr_ref[0, k]) + _nt(gib[:, cols], bbi_ref[0, k]))
            dbbr_ref[k] += _tn(ubf[:, rows], grb[:, cols])
            dbbi_ref[k] += _tn(ubf[:, rows], gib[:, cols])
        dbuf[...] = jnp.concatenate(dus, axis=1) + du_skip
        _store_slabs(dbuf, du4_ref)

        @pl.when(i == nck - 1)
        def _():
            dar_ref[...] = _colsum(accr[...])
            dai_ref[...] = _colsum(acci[...])
            ng = N_GROUPS // 2
            for k in range(2):
                for acc, out in ((dbbr_ref, dbbr_out), (dbbi_ref, dbbi_out)):
                    out[k] = jnp.concatenate(
                        [acc[k, g * GROUP_W:(g + 1) * GROUP_W, g * STATE:(g + 1) * STATE] for g in range(ng)], axis=0)
                for acc, out in ((dcrt_ref, dcrt_out), (dcit_ref, dcit_out)):
                    out[k] = jnp.concatenate(
                        [acc[k, g * STATE:(g + 1) * STATE, g * GROUP_W:(g + 1) * GROUP_W] for g in range(ng)], axis=0)

    rev4 = pl.BlockSpec((4, CHUNK, LANE), lambda i: (0, nck - 1 - i, 0))
    hc_spec = pl.BlockSpec((1, SEGS, N_CH), lambda i: (nck - 1 - i, 0, 0))
    fixed2 = lambda shape: pl.BlockSpec(shape, lambda i: (0,) * len(shape))
    return pl.pallas_call(
        body, name="ssm_bwd", grid=(nck,),
        in_specs=_ssm_specs(layer, nck, True) + [rev4, hc_spec, hc_spec],
        out_specs=[rev4, fixed2((2, 256, STATE)), fixed2((2, 256, STATE)), fixed2((2, HALF_CH, GROUP_W)),
                   fixed2((2, HALF_CH, GROUP_W)), fixed2((1, N_CH)), fixed2((1, N_CH)), fixed2((SSM_W, SSM_W)),
                   fixed2((8, SSM_W))],
        out_shape=[jax.ShapeDtypeStruct((4, seq, LANE), F32),
                   jax.ShapeDtypeStruct((2, 256, STATE), F32), jax.ShapeDtypeStruct((2, 256, STATE), F32),
                   jax.ShapeDtypeStruct((2, HALF_CH, GROUP_W), F32), jax.ShapeDtypeStruct((2, HALF_CH, GROUP_W), F32),
                   jax.ShapeDtypeStruct((1, N_CH), F32), jax.ShapeDtypeStruct((1, N_CH), F32),
                   jax.ShapeDtypeStruct((SSM_W, SSM_W), F32), jax.ShapeDtypeStruct((8, SSM_W), F32)],
        scratch_shapes=[pltpu.VMEM((CHUNK, N_CH), F32), pltpu.VMEM((CHUNK, N_CH), F32),
                        pltpu.VMEM((CHUNK, N_CH), F32), pltpu.VMEM((CHUNK, N_CH), F32),
                        pltpu.VMEM((STEPS, N_CH), F32), pltpu.VMEM((STEPS, N_CH), F32),
                        pltpu.VMEM((1, N_CH), F32), pltpu.VMEM((1, N_CH), F32),
                        pltpu.VMEM((SEGS, N_CH), F32), pltpu.VMEM((SEGS, N_CH), F32),
                        pltpu.VMEM((CHUNK, SSM_W), F32), pltpu.VMEM((CHUNK, SSM_W), F32),
                        pltpu.VMEM((2, 256, HALF_CH), F32), pltpu.VMEM((2, 256, HALF_CH), F32),
                        pltpu.VMEM((2, HALF_CH, 256), F32), pltpu.VMEM((2, HALF_CH, 256), F32)],
        compiler_params=_params(("arbitrary",)),
    )(u4, a_r, a_i, bb_r, bb_i, c_rt, c_it, vec512, w_glu, ds4, hc_r, hc_i)


def _block_diag(t):
    nl, _, ng, a, b = t.shape
    eye = jnp.eye(ng, dtype=t.dtype)
    return jnp.einsum("gh,lkgab->lkgahb", eye, t).reshape(nl, 2, ng * a, ng * b)


def _local_step(x, loss_target, mod, p, comm):
    nl = mod.shape[0]
    pad1024 = jnp.zeros((nl, 16 - 10, D_MODEL), F32)
    vec = jnp.concatenate([mod.reshape(nl, N_MOD, D_MODEL), p["pre_mix_g"][:, None], p["post_mix_g"][:, None],
                           p["pre_mlp_g"][:, None], p["post_mlp_g"][:, None], pad1024], axis=1)
    vec512 = jnp.concatenate([p["attn_out_g"][:, None], p["ssm_out_g"][:, None], p["d_skip"][:, None],
                              p["b_glu"][:, None], jnp.zeros((nl, 4, SSM_W), F32)], axis=1)
    n_all = nl * N_CH
    lr = p["lam_re"].reshape(n_all, 1)
    li = p["lam_im"].reshape(n_all, 1)
    ldt = jnp.broadcast_to(p["log_dt"][:, :, None], (nl, N_GROUPS, STATE)).reshape(n_all, 1)
    br = p["b_re"].reshape(n_all, GROUP_W)
    bi = p["b_im"].reshape(n_all, GROUP_W)
    ab_r, ab_i, bb_r, bb_i = _ssm_prepare(lr, li, ldt, br, bi)
    a_r = ab_r.reshape(nl, 1, N_CH)
    a_i = ab_i.reshape(nl, 1, N_CH)

    def dense_b(bb):
        return _block_diag(bb.reshape(nl, 2, 16, STATE, GROUP_W).transpose(0, 1, 2, 4, 3)).astype(BF16)

    def dense_c(cc):
        return _block_diag(cc.reshape(nl, 2, 16, GROUP_W, STATE).transpose(0, 1, 2, 4, 3)).astype(BF16)

    bbr_d, bbi_d = dense_b(bb_r), dense_b(bb_i)
    crt_d, cit_d = dense_c(p["c_re"]), dense_c(p["c_im"])

    saved = []
    xl = x
    mixer_w, mlp_w = [None] * nl, [None] * nl
    for l in range(nl):
        mixer_w[l], tok = comm.mixer_weights(l, [xl, bbr_d, bbi_d, crt_d, cit_d] if l == 0 else xl)
        w_in_t, w_glu, w_out = mixer_w[l]
        q, kv, u4, h1 = _in_proj_fwd(xl, _after(vec, *tok), w_in_t, l)
        attn = _attn_fwd(q, kv, p["attn_sinks"][l])
        s4, hc_r, hc_i = _ssm_fwd(u4, a_r, a_i, bbr_d, bbi_d, crt_d, cit_d, vec512, w_glu, l)
        x1 = _out_proj_fwd(xl, attn, s4, vec, vec512, w_out, l)
        mlp_w[l] = comm.mlp_weights(l, x1)
        x2 = _mlp_fwd(x1, vec, mlp_w[l][0], mlp_w[l][1], l)
        saved.append((xl, q, kv, u4, h1, attn, s4, hc_r, hc_i, x1))
        xl = x2

    dx, loss_parts = _loss_head(xl, loss_target)
    loss = jnp.sum(loss_parts[:, 0, 0])

    dvec_l, dvec512_l, dsink_l = [None] * nl, [None] * nl, [None] * nl
    dab_r, dab_i, dbb_r, dbb_i, dc_re, dc_im = ([None] * nl for _ in range(6))
    toks = []
    for l in range(nl - 1, -1, -1):
        xl, q, kv, u4, h1, attn, s4, hc_r, hc_i, x1 = saved[l]
        w_in_t, w_glu, w_out = mixer_w[l]
        dx1, h2, da, r, df, dvec_m = _mlp_bwd(dx, x1, _after(vec, *toks), mlp_w[l][0], mlp_w[l][1], l)
        toks = comm.after_mlp_bwd(l, dx1)
        dw_mlp_out = _matmul_tn(r, df, BF16, "dw_mlp_out").reshape(4, D_FF // 4, D_MODEL)
        dw_mlp_in = _matmul_tn(h2, da, BF16, "dw_mlp_in", pieces=4)
        toks = toks + comm.mlp_grads(l, [dw_mlp_in, dw_mlp_out])
        dattn, ds4, heads, dmixed, dvec_o, dvec512_o = _out_proj_bwd(
            dx1, attn, s4, _after(vec, *toks), vec512, w_out, l)
        dw_out = _matmul_tn(heads, dmixed, BF16, "dw_out").reshape(4, D_MODEL // 4, D_MODEL)
        dq, dkv, dsk = _attn_bwd(q, kv, p["attn_sinks"][l], dattn)
        (du4, dbbr, dbbi, dcrt, dcit, dar, dai, dwg, dvec512_s) = _ssm_bwd(
            u4, ds4, hc_r, hc_i, a_r, a_i, bbr_d, bbi_d, crt_d, cit_d, vec512, w_glu, l)
        dw_glu = dwg.astype(BF16).reshape(4, SSM_W // 4, SSM_W)
        dx, dproj, dvec_i = _in_proj_bwd(dx1, dq, dkv, du4, xl, vec, w_in_t, l)
        toks = comm.after_in_proj_bwd(l, dx)
        dw_in = _matmul_tn(h1, dproj, BF16, "dw_in")
        toks = toks + comm.mixer_grads(l, [dw_in.reshape(D_MODEL, 4, IN_W // 4).transpose(1, 0, 2), dw_glu, dw_out])
        dvec_l[l] = dvec_m + dvec_o + dvec_i
        dvec512_l[l] = dvec512_o + dvec512_s
        dsink_l[l] = jnp.sum(dsk[:, :, 0], axis=0)
        dab_r[l], dab_i[l], dbb_r[l], dbb_i[l], dc_re[l], dc_im[l] = dar, dai, dbbr, dbbi, dcrt, dcit

    dvec = _after(jnp.stack(dvec_l), *toks)
    dvec512 = jnp.stack(dvec512_l)
    ng = N_GROUPS // 2

    def b_cols(d):
        return jnp.stack(d).reshape(nl, 2, ng, GROUP_W, STATE).transpose(0, 1, 2, 4, 3).reshape(n_all, GROUP_W)

    def c_param(d):
        return jnp.stack(d).reshape(nl, 2, ng, STATE, GROUP_W).transpose(0, 1, 2, 4, 3).reshape(c_shape)

    dbb_r_c, dbb_i_c = b_cols(dbb_r), b_cols(dbb_i)
    c_shape = (nl, N_GROUPS, GROUP_W, STATE)
    dlr, dli, dldt, dbr, dbi = _ssm_prepare_bwd(
        lr, li, ldt, br, bi, jnp.stack(dab_r).reshape(n_all, 1), jnp.stack(dab_i).reshape(n_all, 1), dbb_r_c, dbb_i_c)
    small = {
        "b_ada": dvec[:, :N_MOD].reshape(nl, N_MOD * D_MODEL),
        "pre_mix_g": dvec[:, V_PRE_MIX], "post_mix_g": dvec[:, V_POST_MIX],
        "pre_mlp_g": dvec[:, V_PRE_MLP], "post_mlp_g": dvec[:, V_POST_MLP],
        "attn_sinks": jnp.stack(dsink_l),
        "lam_re": dlr.reshape(nl, N_GROUPS, STATE), "lam_im": dli.reshape(nl, N_GROUPS, STATE),
        "log_dt": jnp.sum(dldt.reshape(nl, N_GROUPS, STATE), axis=-1),
        "b_re": dbr.reshape(nl, N_GROUPS, STATE, GROUP_W), "b_im": dbi.reshape(nl, N_GROUPS, STATE, GROUP_W),
        "c_re": c_param(dc_re), "c_im": c_param(dc_im),
        "d_skip": dvec512[:, H_DSKIP], "b_glu": dvec512[:, H_BGLU],
        "attn_out_g": dvec512[:, H_ATTN_G], "ssm_out_g": dvec512[:, H_SSM_G],
    }
    return loss, dx, small, small["b_ada"]


WEIGHTS = ["w_ada", "b_ada", "pre_mix_g", "w_in", "attn_sinks", "lam_re", "lam_im", "log_dt", "b_re", "b_im", "c_re",
           "c_im", "d_skip", "w_glu", "b_glu", "attn_out_g", "ssm_out_g", "w_out", "post_mix_g", "pre_mlp_g",
           "w_mlp_in", "w_mlp_out", "post_mlp_g"]
BIG = ["w_in", "w_glu", "w_out", "w_mlp_in", "w_mlp_out"]
SMALL = [n for n in WEIGHTS if n not in BIG and n != "w_ada"]
PACK_ROWS = 256


def _pack(parts):
    rows = []
    for n in SMALL:
        flat = parts[n].reshape(-1)
        pad = (-flat.shape[0]) % (PACK_ROWS * LANE)
        rows.append(jnp.pad(flat, (0, pad)).reshape(-1, LANE))
    return jnp.concatenate(rows, axis=0)


def _unpack(packed, shapes):
    out, r0 = {}, 0
    for n in SMALL:
        size = int(np.prod(shapes[n]))
        rows = -(-size // (PACK_ROWS * LANE)) * PACK_ROWS
        out[n] = packed[r0:r0 + rows].reshape(-1)[:size].reshape(shapes[n])
        r0 += rows
    return out


MIXER = ["w_in", "w_glu", "w_out"]
MLP = ["w_mlp_in", "w_mlp_out"]


class _Exchanges:
    def __init__(self, shards, wts, mom, var, chip):
        self.shards, self.wts, self.mom, self.var, self.chip = shards, wts, mom, var, chip
        self.chip_arr = jnp.reshape(chip, (1,)).astype(jnp.int32)
        self.nl = len(shards["w_in"])
        self.gathers, self.scatters, self.pairs = {}, {}, {}
        self.res = {n: None for n in BIG}

    def _start_gather(self, group, tag, l, after=()):
        srcs = [self.shards[n][l] for n in group]
        lands = [lax.dynamic_update_slice(lax.empty((4,) + s.shape, s.dtype), s[None], (self.chip, 0, 0)) for s in srcs]
        plan = _plan_gather(len(srcs))
        st = _exchange_start(f"gather_{tag}{l}_start", 3 * len(srcs), plan, srcs + lands, after)
        self.gathers[tag, l] = (plan, st)
        return st[3]

    def _wait_gather(self, tag, l, after):
        plan, st = self.gathers.pop((tag, l))
        n = len(st[2]) // 2
        bufs = _exchange_wait(f"gather_{tag}{l}_wait", 3 * n, plan, st, after)
        return [b.reshape(4 * b.shape[1], b.shape[2]) for b in bufs[n:]]

    def _start_layer(self, l, after):
        tok = self._start_gather(MIXER, "mixer", l, after)
        return [tok, self._start_gather(MLP, "mlp", l, [tok])]

    def begin(self, after):
        toks = self._start_layer(0, after)
        return toks + (self._start_layer(1, toks[1:]) if self.nl > 1 else [])

    def mixer_weights(self, l, after):
        w = self._wait_gather("mixer", l, after)
        toks = self._start_layer(l + 2, w[:1]) if l + 2 < self.nl else []
        return w, toks

    def mlp_weights(self, l, after):
        return self._wait_gather("mlp", l, after)

    def _start_scatter(self, tag, l, group, pieces):
        plan = _plan_scatter(len(pieces))
        st = _exchange_start(f"scatter_{tag}{l}_start", 3 * len(pieces), plan,
                             list(pieces) + [lax.empty(p.shape, p.dtype) for p in pieces])
        self.scatters[tag] = (l, group, plan, st)
        return [st[3]]

    def _finish_scatter(self, tag, after):
        l, group, plan, st = self.scatters.pop(tag)
        n = len(group)
        bufs = _exchange_wait(f"scatter_{tag}{l}_wait", 3 * n, plan, st, after)
        sums = [_sum_pieces(bufs[k], bufs[n + k], self.chip_arr, "sum_" + group[k]) for k in range(n)]
        plan2 = _plan_pair(n)
        st2 = _exchange_start(f"pair_{tag}{l}_start", n, plan2, sums + [lax.empty(s.shape, s.dtype) for s in sums])
        self.pairs[tag] = (l, group, plan2, st2)
        return [st2[3]]

    def _finish_pair(self, tag, after):
        l, group, plan, st = self.pairs.pop(tag)
        n = len(group)
        bufs = _exchange_wait(f"pair_{tag}{l}_wait", n, plan, st, after)
        for k, name in enumerate(group):
            self.res[name] = _adamw_layer([bufs[k], bufs[n + k]], self.wts[name], self.mom[name], self.var[name],
                                          l, self.res[name], "adamw_" + name)

    def after_mlp_bwd(self, l, after):
        toks = self._finish_scatter("mixer", after) if "mixer" in self.scatters else []
        if "mlp" in self.pairs:
            self._finish_pair("mlp", after)
        return toks

    def mlp_grads(self, l, pieces):
        return self._start_scatter("mlp", l, MLP, pieces)

    def after_in_proj_bwd(self, l, after):
        toks = self._finish_scatter("mlp", after)
        if "mixer" in self.pairs:
            self._finish_pair("mixer", after)
        return toks

    def mixer_grads(self, l, pieces):
        return self._start_scatter("mixer", l, MIXER, pieces)

    def finish_mixer_scatter(self, after):
        return self._finish_scatter("mixer", after)

    def finish_mlp(self, after):
        self._finish_pair("mlp", after)

    def finish_mixer(self, after):
        self._finish_pair("mixer", after)

    def results(self):
        return self.res


def kernel(x, c, w_ada, b_ada, pre_mix_g, w_in, attn_sinks, lam_re, lam_im, log_dt, b_re, b_im, c_re, c_im, d_skip, w_glu, b_glu, attn_out_g, ssm_out_g, w_out, post_mix_g, pre_mlp_g, w_mlp_in, w_mlp_out, post_mlp_g, loss_target, m_w_ada, m_b_ada, m_pre_mix_g, m_w_in, m_attn_sinks, m_lam_re, m_lam_im, m_log_dt, m_b_re, m_b_im, m_c_re, m_c_im, m_d_skip, m_w_glu, m_b_glu, m_attn_out_g, m_ssm_out_g, m_w_out, m_post_mix_g, m_pre_mlp_g, m_w_mlp_in, m_w_mlp_out, m_post_mlp_g, v_w_ada, v_b_ada, v_pre_mix_g, v_w_in, v_attn_sinks, v_lam_re, v_lam_im, v_log_dt, v_b_re, v_b_im, v_c_re, v_c_im, v_d_skip, v_w_glu, v_b_glu, v_attn_out_g, v_ssm_out_g, v_w_out, v_post_mix_g, v_pre_mlp_g, v_w_mlp_in, v_w_mlp_out, v_post_mlp_g):
    args = locals()
    wts = {n: args[n] for n in WEIGHTS}
    mom = {n: args["m_" + n] for n in WEIGHTS}
    var = {n: args["v_" + n] for n in WEIGHTS}
    nl = w_in.shape[0]
    ix, iy, ic = lax.axis_index("x"), lax.axis_index("y"), lax.axis_index("c")
    chip = 2 * ix + iy
    me = 4 * ix + 2 * iy + ic
    wcols = w_ada.shape[2]

    shards = {"w_in": [w_in[l].astype(BF16).T for l in range(nl)], "w_glu": [w_glu[l].astype(BF16) for l in range(nl)],
              "w_out": [w_out[l].astype(BF16) for l in range(nl)],
              "w_mlp_in": [w_mlp_in[l].astype(BF16).T for l in range(nl)],
              "w_mlp_out": [w_mlp_out[l].astype(BF16) for l in range(nl)]}
    comm = _Exchanges(shards, wts, mom, var, chip)

    c_all = _gather([c.reshape(1, 1, 1, D_MODEL)], "all", "gather_c")[0].reshape(8, D_MODEL)
    b_sh = lax.dynamic_slice(b_ada, (0, chip * wcols), (nl, wcols)).reshape(nl, 1, wcols)
    mod_sh = _ada_forward(c_all, w_ada, b_sh)
    mod_all = _gather([mod_sh.reshape(1, 1, nl * 8, wcols)], "chips", "gather_mod")[0]
    toks = comm.begin([mod_all])
    mod = lax.dynamic_index_in_dim(mod_all.reshape(4, nl, 8, wcols), me, axis=2, keepdims=False)
    mod = mod.transpose(1, 0, 2).reshape(nl, 4 * wcols)

    small_p = {n: wts[n] for n in SMALL}
    small_p["log_dt"] = _after(log_dt, *toks)
    loss, grad_x, small, dmod = _local_step(x[0], loss_target[0], mod, small_p, comm)
    loss = lax.psum(loss, ("x", "y", "c"))

    packed = _pack(small)
    rows = packed.shape[0]
    pair_plan = _plan_pair(1)
    pair_small = _exchange_start("pair_small_start", 1, pair_plan, [packed, lax.empty((rows, LANE), F32)])

    dmod = _after(dmod, pair_small[3])
    dmod_all = _gather([dmod.reshape(1, 1, nl, N_MOD * D_MODEL)], "all", "gather_dmod")[0][0]
    dmod_sh = lax.dynamic_slice(dmod_all, (0, 0, chip * wcols), (8, nl, wcols)).transpose(1, 0, 2)
    g_ada = _ada_weight_grad(c_all.T, dmod_sh)
    res = {"w_ada": _adamw(g_ada[:, None], w_ada, m_w_ada, v_w_ada, "adamw_w_ada")}

    comm.finish_mlp(res["w_ada"][0])
    toks = comm.finish_mixer_scatter(res["w_ada"][0])

    own, other = _exchange_wait("pair_small_wait", 1, pair_plan, pair_small, res["w_ada"][0])
    chip_sum = _after(_sum_list([own, other], "sum_pair_small"), *toks)
    quad_plan = _plan_gather(1)
    quad0 = lax.dynamic_update_slice(lax.empty((4, rows, LANE), F32), chip_sum[None], (chip, 0, 0))
    quad_small = _exchange_start("gather_small_start", 3, quad_plan, [chip_sum, quad0])
    comm.finish_mixer([comm.results()[n][0] for n in MLP] + [_after(dmod, quad_small[3])])
    res.update(comm.results())
    quad = _exchange_wait("gather_small_wait", 3, quad_plan, quad_small, [res[n][0] for n in BIG])[1]
    outs = _adamw(quad[None], _pack({n: wts[n] for n in SMALL})[None], _pack({n: mom[n] for n in SMALL})[None],
                  _pack({n: var[n] for n in SMALL})[None], "adamw_small")
    shapes = {n: wts[n].shape for n in SMALL}
    unpacked = [_unpack(o[0], shapes) for o in outs]
    for n in SMALL:
        res[n] = [u[n] for u in unpacked]

    return (loss, grad_x[None], *[res[n][0] for n in WEIGHTS], *[res[n][1] for n in WEIGHTS],
            *[res[n][2] for n in WEIGHTS], *[res[n][3] for n in WEIGHTS])
```

```python
import functools
import math

import numpy as np
import jax
import jax.numpy as jnp
from jax import lax
from jax.experimental import pallas as pl
from jax.experimental.pallas import tpu as pltpu

F32 = jnp.float32
BF16 = jnp.bfloat16

D_MODEL = 1024
ATTN_W = 512
SSM_W = 512
HEAD_DIM = 64
N_Q = 8
N_KV = 2
Q_PER_KV = 4
KV_W = 128
WINDOW = 128
BLOCK = 128
N_GROUPS = 32
GROUP_W = 16
STATE = 64
N_CH = N_GROUPS * STATE
HALF_CH = N_CH // 2
D_FF = 4096
IN_W = 1280
N_MOD = 6
EPS = 1e-6
NEG_INF = -1e30

ADAM_LR = 0.001
ADAM_B1 = 0.9
ADAM_B2 = 0.999
ADAM_EPS = 1e-08
ADAM_WD = 0.01
ADAM_STEP = 10

ROW_TILE = 256
CHUNK = 256
SEGS = 8
STEPS = CHUNK // SEGS
STRIP = 1024
VMEM_LIMIT_V7X = 56 * 1024 * 1024
LANE = 128
SUBLANE = 8

GELU_K0 = math.sqrt(2.0 / math.pi)
GELU_K1 = 0.044715

V_SH1, V_SC1, V_G1, V_SH2, V_SC2, V_G2, V_PRE_MIX, V_POST_MIX, V_PRE_MLP, V_POST_MLP = range(10)
H_ATTN_G, H_SSM_G, H_DSKIP, H_BGLU = range(4)

HBM = pl.BlockSpec(memory_space=pltpu.HBM)
SEM = pl.BlockSpec(memory_space=pltpu.SEMAPHORE)
EFFECT = pltpu.SideEffectType.DATAFLOW_SIDE_EFFECTING
MESH_ID = pl.DeviceIdType.MESH


def _nn(a, b):
    return lax.dot_general(a, b, (((1,), (0,)), ((), ())), preferred_element_type=F32)


def _nt(a, b):
    return lax.dot_general(a, b, (((1,), (1,)), ((), ())), preferred_element_type=F32)


def _tn(a, b):
    return lax.dot_general(a, b, (((0,), (0,)), ((), ())), preferred_element_type=F32)


def _params(sem):
    return pltpu.CompilerParams(dimension_semantics=sem, vmem_limit_bytes=VMEM_LIMIT_V7X)


def _rms_fwd(x, g):
    r = lax.rsqrt(jnp.mean(x * x, axis=-1, keepdims=True) + EPS)
    xh = x * r
    return xh * g, xh, r


def _rms_bwd(dy, xh, r, g):
    dxh = dy * g
    dx = r * (dxh - xh * jnp.mean(dxh * xh, axis=-1, keepdims=True))
    return dx, dy * xh


def _colsum(t):
    return jnp.sum(t, axis=0, keepdims=True)


def _gelu(y):
    t = jnp.tanh(GELU_K0 * (y + GELU_K1 * (y * y * y)))
    return 0.5 * y * (1.0 + t), t


def _gelu_grad(y, t):
    return 0.5 * (1.0 + t) + 0.5 * y * (1.0 - t * t) * GELU_K0 * (1.0 + 3.0 * GELU_K1 * y * y)


def _alibi_slopes():
    return [float(s) for s in 2.0 ** (-8.0 * np.arange(1, N_Q + 1) / N_Q)]


def _pick_rows(rows, bytes_per_row, budget):
    t = rows
    while t % (2 * SUBLANE) == 0 and t * bytes_per_row > budget:
        t //= 2
    return t


def _load_once(step, pairs, sems):
    @pl.when(step == 0)
    def _():
        cps = [pltpu.make_async_copy(src, dst, sems.at[k]) for k, (src, dst) in enumerate(pairs)]
        for cp in cps:
            cp.start()
        for cp in cps:
            cp.wait()


_GROUPS = {
    "all": ([(0, 0, 1), (0, 1, 0), (0, 1, 1), (1, 0, 0), (1, 0, 1), (1, 1, 0), (1, 1, 1)], (4, 2, 1), 8),
    "chips": ([(1, 0, 0), (0, 1, 0), (1, 1, 0)], (2, 1, 0), 4),
    "pair": ([(0, 0, 1)], (0, 0, 1), 2),
}


def _flip(v, f):
    return 1 - v if f else v


def _gather(arrs, kind, name):
    masks, wts, n = _GROUPS[kind]
    na, nm = len(arrs), len(masks)

    def body(*refs):
        ins, outs = refs[:na], refs[na:2 * na]
        ssem, rsem, lsem = refs[2 * na:]
        x, y, c = lax.axis_index("x"), lax.axis_index("y"), lax.axis_index("c")
        me = wts[0] * x + wts[1] * y + wts[2] * c
        local = [pltpu.make_async_copy(ins[k], outs[k].at[:, pl.ds(me, 1)], lsem.at[k]) for k in range(na)]
        for cp in local:
            cp.start()
        remote = []
        for k in range(na):
            for mi, (fx, fy, fc) in enumerate(masks):
                peer = (_flip(x, fx), _flip(y, fy), _flip(c, fc))
                remote.append(pltpu.make_async_remote_copy(
                    src_ref=ins[k], dst_ref=outs[k].at[:, pl.ds(me, 1)],
                    send_sem=ssem.at[k * nm + mi], recv_sem=rsem.at[k * nm + mi],
                    device_id=peer, device_id_type=MESH_ID))
        for cp in remote:
            cp.start()
        for cp in remote:
            cp.wait()
        for cp in local:
            cp.wait()

    outs = pl.pallas_call(
        body, name=name,
        out_shape=[jax.ShapeDtypeStruct((a.shape[0], n) + a.shape[2:], a.dtype) for a in arrs],
        in_specs=[HBM] * na, out_specs=[HBM] * na,
        scratch_shapes=[pltpu.SemaphoreType.DMA((na * nm,)), pltpu.SemaphoreType.DMA((na * nm,)),
                        pltpu.SemaphoreType.DMA((na,))],
    )(*arrs)
    return list(outs)


def _hbm(a):
    return pltpu.with_memory_space_constraint(a, pltpu.HBM)


def _after(x, *tokens):
    for t in tokens:
        x = x + t[0, 0].astype(x.dtype)
    return x


def _exchange_start(name, n_copies, plan, bufs, after=()):
    n, na = len(bufs), len(after)

    def body(*refs):
        ssem, rsem, token = refs[n + na], refs[n + na + 1], refs[2 * n + na + 2]
        for k, (src, dst, dev) in enumerate(plan(refs[:n])):
            pltpu.make_async_remote_copy(src_ref=src, dst_ref=dst, send_sem=ssem.at[k], recv_sem=rsem.at[k],
                                         device_id=dev, device_id_type=MESH_ID).start()
        token[...] = jnp.zeros_like(token)

    outs = pl.pallas_call(
        body, name=name,
        out_shape=(pltpu.SemaphoreType.DMA((n_copies,)), pltpu.SemaphoreType.DMA((n_copies,)),
                   *[pltpu.HBM(b.shape, b.dtype) for b in bufs], jax.ShapeDtypeStruct((SUBLANE, LANE), F32)),
        in_specs=[HBM] * n + [pl.BlockSpec(memory_space=pl.ANY)] * na,
        out_specs=(SEM, SEM, *[HBM] * n, pl.BlockSpec(memory_space=pltpu.VMEM)),
        input_output_aliases={i: 2 + i for i in range(n)},
        compiler_params=pltpu.CompilerParams(has_side_effects=EFFECT),
    )(*[_hbm(b) for b in bufs], *after)
    return outs[0], outs[1], list(outs[2:2 + n]), outs[2 + n]


def _exchange_wait(name, n_copies, plan, started, after):
    ssem, rsem, bufs, _ = started
    n = len(bufs)
    after = list(after) if isinstance(after, (list, tuple)) else [after]

    def body(*refs):
        ssem_ref, rsem_ref = refs[n], refs[n + 1]
        for k, (src, dst, dev) in enumerate(plan(refs[:n])):
            cp = pltpu.make_async_remote_copy(src_ref=src, dst_ref=dst, send_sem=ssem_ref.at[k], recv_sem=rsem_ref.at[k],
                                              device_id=dev, device_id_type=MESH_ID)
            cp.wait_send()
            cp.wait_recv()

    outs = pl.pallas_call(
        body, name=name,
        out_shape=tuple(pltpu.HBM(b.shape, b.dtype) for b in bufs),
        in_specs=[HBM] * n + [SEM, SEM] + [pl.BlockSpec(memory_space=pl.ANY)] * len(after), out_specs=tuple([HBM] * n),
        input_output_aliases={i: i for i in range(n)},
        compiler_params=pltpu.CompilerParams(has_side_effects=EFFECT),
    )(*bufs, ssem, rsem, *after)
    return list(outs)


def _position():
    x, y, c = lax.axis_index("x"), lax.axis_index("y"), lax.axis_index("c")
    return x, y, c, [(1 - x, y), (x, 1 - y), (1 - x, 1 - y)]


def _plan_gather(na):
    def plan(refs):
        x, y, c, chips = _position()
        return [(refs[k], refs[na + k].at[2 * x + y], (px, py, c)) for k in range(na) for px, py in chips]
    return plan


def _plan_scatter(na):
    def plan(refs):
        x, y, c, chips = _position()
        return [(refs[k].at[2 * px + py], refs[na + k].at[2 * x + y], (px, py, c))
                for k in range(na) for px, py in chips]
    return plan


def _plan_pair(na):
    def plan(refs):
        x, y, c, _ = _position()
        return [(refs[k], refs[na + k], (x, y, 1 - c)) for k in range(na)]
    return plan


def _sum_list(arrs, name):
    n = len(arrs)
    r, c = arrs[0].shape
    tr = _pick_rows(r, c * 4 * (n + 1), 4 << 20)

    def body(*refs):
        acc = refs[0][...].astype(F32)
        for j in range(1, n):
            acc = acc + refs[j][...].astype(F32)
        refs[n][...] = acc

    blk = pl.BlockSpec((tr, c), lambda i: (i, 0))
    return pl.pallas_call(
        body, name=name, grid=(r // tr,), in_specs=[blk] * n, out_specs=blk,
        out_shape=jax.ShapeDtypeStruct((r, c), F32), compiler_params=_params(("parallel",)),
    )(*arrs)


def _sum_pieces(own, recv, chip, name):
    _, r, c = own.shape
    tr = _pick_rows(r, c * 2 * 6, 4 << 20)

    def body(chip_ref, own_ref, recv_ref, o_ref):
        acc = own_ref[0].astype(F32)
        for j in range(4):
            acc = acc + jnp.where(chip_ref[0] == j, 0.0, recv_ref[j].astype(F32))
        o_ref[...] = acc.astype(BF16)

    return pl.pallas_call(
        body, name=name,
        grid_spec=pltpu.PrefetchScalarGridSpec(
            num_scalar_prefetch=1, grid=(r // tr,),
            in_specs=[pl.BlockSpec((1, tr, c), lambda i, chip_ref: (chip_ref[0], i, 0)),
                      pl.BlockSpec((4, tr, c), lambda i, chip_ref: (0, i, 0))],
            out_specs=pl.BlockSpec((tr, c), lambda i, chip_ref: (i, 0))),
        out_shape=jax.ShapeDtypeStruct((r, c), BF16), compiler_params=_params(("parallel",)),
    )(chip, own, recv)


def _adam_update(g, w, m, v):
    mn = ADAM_B1 * m + (1.0 - ADAM_B1) * g
    vn = ADAM_B2 * v + (1.0 - ADAM_B2) * jnp.square(g)
    m_hat = mn / (1.0 - ADAM_B1 ** ADAM_STEP)
    v_hat = vn / (1.0 - ADAM_B2 ** ADAM_STEP)
    return -ADAM_LR * (m_hat / (jnp.sqrt(v_hat) + ADAM_EPS) + ADAM_WD * w), mn, vn


def _adamw_layer(grads, w, m, v, layer, prev, name):
    ng = len(grads)
    nl, r, c = w.shape
    tr = _pick_rows(r, c * 4 * (ng + 7), 6 << 20)
    if prev is None:
        prev = [lax.empty((nl, r, c), F32) for _ in range(4)]

    def body(*refs):
        g = refs[0][...].astype(F32)
        for j in range(1, ng):
            g = g + refs[j][...].astype(F32)
        w_ref, m_ref, v_ref = refs[ng:ng + 3]
        go_ref, d_ref, mo_ref, vo_ref = refs[ng + 7:ng + 11]
        d, mn, vn = _adam_update(g, w_ref[0], m_ref[0], v_ref[0])
        go_ref[0] = g
        d_ref[0] = d
        mo_ref[0] = mn
        vo_ref[0] = vn

    gblk = pl.BlockSpec((tr, c), lambda i: (i, 0))
    blk = pl.BlockSpec((1, tr, c), lambda i: (layer, i, 0))
    keep = pl.BlockSpec(memory_space=pl.ANY)
    sds = jax.ShapeDtypeStruct((nl, r, c), F32)
    return pl.pallas_call(
        body, name=name, grid=(r // tr,),
        in_specs=[gblk] * ng + [blk] * 3 + [keep] * 4,
        out_specs=[blk] * 4, out_shape=[sds] * 4,
        input_output_aliases={ng + 3 + i: i for i in range(4)},
        compiler_params=_params(("parallel",)),
    )(*grads, w, m, v, *prev)


def _adamw(gs, w, m, v, name):
    a, s, r, c = gs.shape
    tr = _pick_rows(r, c * 4 * (s + 7), 6 << 20)

    def body(g_ref, w_ref, m_ref, v_ref, go_ref, d_ref, mo_ref, vo_ref):
        g = g_ref[0, 0].astype(F32)
        for j in range(1, s):
            g = g + g_ref[0, j].astype(F32)
        d, mn, vn = _adam_update(g, w_ref[0], m_ref[0], v_ref[0])
        go_ref[0] = g
        d_ref[0] = d
        mo_ref[0] = mn
        vo_ref[0] = vn

    blk = pl.BlockSpec((1, tr, c), lambda i, j: (i, j, 0))
    sds = jax.ShapeDtypeStruct((a, r, c), F32)
    return pl.pallas_call(
        body, name=name, grid=(a, r // tr),
        in_specs=[pl.BlockSpec((1, s, tr, c), lambda i, j: (i, 0, j, 0)), blk, blk, blk],
        out_specs=[blk, blk, blk, blk], out_shape=[sds, sds, sds, sds],
        compiler_params=_params(("parallel", "parallel")),
    )(gs, w, m, v)


def _ada_forward(c_all, w_ada, b_sh):
    nl, d, w = w_ada.shape
    tw = 512

    def body(c_ref, w_ref, b_ref, o_ref):
        cv = c_ref[...]
        act = (cv * jax.nn.sigmoid(cv)).astype(BF16)
        o_ref[0] = _nn(act, w_ref[0].astype(BF16)) + b_ref[0]

    return pl.pallas_call(
        body, name="ada_forward", grid=(nl, w // tw),
        in_specs=[pl.BlockSpec((8, d), lambda l, j: (0, 0)),
                  pl.BlockSpec((1, d, tw), lambda l, j: (l, 0, j)),
                  pl.BlockSpec((1, 1, tw), lambda l, j: (l, 0, j))],
        out_specs=pl.BlockSpec((1, 8, tw), lambda l, j: (l, 0, j)),
        out_shape=jax.ShapeDtypeStruct((nl, 8, w), F32),
        compiler_params=_params(("parallel", "parallel")),
    )(c_all, w_ada, b_sh)


def _ada_weight_grad(c_all_t, dmod):
    nl, nb, w = dmod.shape
    d = c_all_t.shape[0]
    tw = 512

    def body(c_ref, g_ref, o_ref):
        cv = c_ref[...]
        act = cv * jax.nn.sigmoid(cv)
        gv = g_ref[0]
        acc = act[:, 0:1] * gv[0:1, :]
        for b in range(1, nb):
            acc = acc + act[:, b:b + 1] * gv[b:b + 1, :]
        o_ref[0] = acc

    return pl.pallas_call(
        body, name="ada_weight_grad", grid=(nl, w // tw),
        in_specs=[pl.BlockSpec((d, nb), lambda l, j: (0, 0)),
                  pl.BlockSpec((1, nb, tw), lambda l, j: (l, 0, j))],
        out_specs=pl.BlockSpec((1, d, tw), lambda l, j: (l, 0, j)),
        out_shape=jax.ShapeDtypeStruct((nl, d, w), F32),
        compiler_params=_params(("parallel", "parallel")),
    )(c_all_t, dmod)


def _in_proj_fwd(x, vec, w_in_t, layer):
    seq = x.shape[0]
    tm = ROW_TILE

    def body(x_ref, vec_ref, w_ref, q_ref, kv_ref, u4_ref, h_ref):
        n, _, _ = _rms_fwd(x_ref[...], vec_ref[0, V_PRE_MIX:V_PRE_MIX + 1, :])
        h = (n * (1.0 + vec_ref[0, V_SC1:V_SC1 + 1, :]) + vec_ref[0, V_SH1:V_SH1 + 1, :]).astype(BF16)
        h_ref[...] = h
        proj = _nt(h, w_ref[...])
        q_ref[...] = proj[:, :ATTN_W].astype(BF16)
        kv_ref[...] = proj[:, ATTN_W:ATTN_W + 2 * KV_W].astype(BF16)
        u0 = ATTN_W + 2 * KV_W
        for j in range(4):
            u4_ref[j] = proj[:, u0 + j * LANE:u0 + (j + 1) * LANE]

    return pl.pallas_call(
        body, name="in_proj_fwd", grid=(seq // tm,),
        in_specs=[pl.BlockSpec((tm, D_MODEL), lambda i: (i, 0)),
                  pl.BlockSpec((1, 16, D_MODEL), lambda i: (layer, 0, 0)),
                  pl.BlockSpec((IN_W, D_MODEL), lambda i: (0, 0))],
        out_specs=[pl.BlockSpec((tm, ATTN_W), lambda i: (i, 0)),
                   pl.BlockSpec((tm, 2 * KV_W), lambda i: (i, 0)),
                   pl.BlockSpec((4, tm, LANE), lambda i: (0, i, 0)),
                   pl.BlockSpec((tm, D_MODEL), lambda i: (i, 0))],
        out_shape=[jax.ShapeDtypeStruct((seq, ATTN_W), BF16), jax.ShapeDtypeStruct((seq, 2 * KV_W), BF16),
                   jax.ShapeDtypeStruct((4, seq, LANE), F32), jax.ShapeDtypeStruct((seq, D_MODEL), BF16)],
        compiler_params=_params(("parallel",)),
    )(x, vec, w_in_t)


def _in_proj_bwd(dx1, dq, dkv, du4, x, vec, w_in_t, layer):
    seq = x.shape[0]
    tm = ROW_TILE

    def body(dx1_ref, dq_ref, dkv_ref, du4_ref, x_ref, vec_ref, w_ref, dx_ref, dp_ref, dvec_ref):
        i = pl.program_id(0)

        @pl.when(i == 0)
        def _():
            dvec_ref[...] = jnp.zeros_like(dvec_ref)

        dproj = jnp.concatenate([dq_ref[...], dkv_ref[...]] + [du4_ref[j] for j in range(4)], axis=1).astype(BF16)
        dp_ref[...] = dproj
        dh = _nn(dproj, w_ref[...])
        g = vec_ref[0, V_PRE_MIX:V_PRE_MIX + 1, :]
        n, xh, r = _rms_fwd(x_ref[...], g)
        dn = dh * (1.0 + vec_ref[0, V_SC1:V_SC1 + 1, :])
        dxn, dg_rows = _rms_bwd(dn, xh, r, g)
        dx_ref[...] = dx1_ref[...] + dxn
        dvec_ref[V_SH1:V_SH1 + 1, :] += _colsum(dh)
        dvec_ref[V_SC1:V_SC1 + 1, :] += _colsum(dh * n)
        dvec_ref[V_PRE_MIX:V_PRE_MIX + 1, :] += _colsum(dg_rows)

    row = pl.BlockSpec((tm, D_MODEL), lambda i: (i, 0))
    return pl.pallas_call(
        body, name="in_proj_bwd", grid=(seq // tm,),
        in_specs=[row, pl.BlockSpec((tm, ATTN_W), lambda i: (i, 0)), pl.BlockSpec((tm, 2 * KV_W), lambda i: (i, 0)),
                  pl.BlockSpec((4, tm, LANE), lambda i: (0, i, 0)), row,
                  pl.BlockSpec((1, 16, D_MODEL), lambda i: (layer, 0, 0)),
                  pl.BlockSpec((IN_W, D_MODEL), lambda i: (0, 0))],
        out_specs=[row, pl.BlockSpec((tm, IN_W), lambda i: (i, 0)), pl.BlockSpec((16, D_MODEL), lambda i: (0, 0))],
        out_shape=[jax.ShapeDtypeStruct((seq, D_MODEL), F32), jax.ShapeDtypeStruct((seq, IN_W), BF16),
                   jax.ShapeDtypeStruct((16, D_MODEL), F32)],
        compiler_params=_params(("arbitrary",)),
    )(dx1, dq, dkv, du4, x, vec, w_in_t)


def _heads(attn_ref, s4_ref, vec512_ref):
    ga = vec512_ref[0, H_ATTN_G:H_ATTN_G + 1, :]
    gs = vec512_ref[0, H_SSM_G:H_SSM_G + 1, :]
    sv = jnp.concatenate([s4_ref[j] for j in range(4)], axis=1)
    na, ah, ar = _rms_fwd(attn_ref[...], ga)
    ns, sh, sr = _rms_fwd(sv, gs)
    return jnp.concatenate([na, ns], axis=1), (ah, ar, ga), (sh, sr, gs)


def _out_proj_fwd(x, attn, s4, vec, vec512, w_out, layer):
    seq = x.shape[0]
    tm = ROW_TILE

    def body(x_ref, attn_ref, s4_ref, vec_ref, vec512_ref, w_ref, x1_ref):
        heads, _, _ = _heads(attn_ref, s4_ref, vec512_ref)
        mixed = _nn(heads.astype(BF16), w_ref[...])
        nm, _, _ = _rms_fwd(mixed, vec_ref[0, V_POST_MIX:V_POST_MIX + 1, :])
        x1_ref[...] = x_ref[...] + vec_ref[0, V_G1:V_G1 + 1, :] * nm

    row = pl.BlockSpec((tm, D_MODEL), lambda i: (i, 0))
    return pl.pallas_call(
        body, name="out_proj_fwd", grid=(seq // tm,),
        in_specs=[row, pl.BlockSpec((tm, ATTN_W), lambda i: (i, 0)), pl.BlockSpec((4, tm, LANE), lambda i: (0, i, 0)),
                  pl.BlockSpec((1, 16, D_MODEL), lambda i: (layer, 0, 0)),
                  pl.BlockSpec((1, 8, SSM_W), lambda i: (layer, 0, 0)),
                  pl.BlockSpec((D_MODEL, D_MODEL), lambda i: (0, 0))],
        out_specs=row, out_shape=jax.ShapeDtypeStruct((seq, D_MODEL), F32),
        compiler_params=_params(("parallel",)),
    )(x, attn, s4, vec, vec512, w_out)


def _out_proj_bwd(dx1, attn, s4, vec, vec512, w_out, layer):
    seq = dx1.shape[0]
    tm = ROW_TILE

    def body(dx1_ref, attn_ref, s4_ref, vec_ref, vec512_ref, w_ref,
             dattn_ref, ds4_ref, heads_ref, dmixed_ref, dvec_ref, dvec512_ref):
        i = pl.program_id(0)

        @pl.when(i == 0)
        def _():
            dvec_ref[...] = jnp.zeros_like(dvec_ref)
            dvec512_ref[...] = jnp.zeros_like(dvec512_ref)

        heads, (ah, ar, ga), (sh, sr, gs) = _heads(attn_ref, s4_ref, vec512_ref)
        hb = heads.astype(BF16)
        heads_ref[...] = hb
        gm = vec_ref[0, V_POST_MIX:V_POST_MIX + 1, :]
        nm, mh, mr = _rms_fwd(_nn(hb, w_ref[...]), gm)
        dx1v = dx1_ref[...]
        dvec_ref[V_G1:V_G1 + 1, :] += _colsum(dx1v * nm)
        dmixed, dgm_rows = _rms_bwd(dx1v * vec_ref[0, V_G1:V_G1 + 1, :], mh, mr, gm)
        dvec_ref[V_POST_MIX:V_POST_MIX + 1, :] += _colsum(dgm_rows)
        dmb = dmixed.astype(BF16)
        dmixed_ref[...] = dmb
        dheads = _nt(dmb, w_ref[...])
        dattn, dga_rows = _rms_bwd(dheads[:, :ATTN_W], ah, ar, ga)
        ds, dgs_rows = _rms_bwd(dheads[:, ATTN_W:], sh, sr, gs)
        dattn_ref[...] = dattn
        for j in range(4):
            ds4_ref[j] = ds[:, j * LANE:(j + 1) * LANE]
        dvec512_ref[H_ATTN_G:H_ATTN_G + 1, :] += _colsum(dga_rows)
        dvec512_ref[H_SSM_G:H_SSM_G + 1, :] += _colsum(dgs_rows)

    row = pl.BlockSpec((tm, D_MODEL), lambda i: (i, 0))
    return pl.pallas_call(
        body, name="out_proj_bwd", grid=(seq // tm,),
        in_specs=[row, pl.BlockSpec((tm, ATTN_W), lambda i: (i, 0)), pl.BlockSpec((4, tm, LANE), lambda i: (0, i, 0)),
                  pl.BlockSpec((1, 16, D_MODEL), lambda i: (layer, 0, 0)),
                  pl.BlockSpec((1, 8, SSM_W), lambda i: (layer, 0, 0)),
                  pl.BlockSpec((D_MODEL, D_MODEL), lambda i: (0, 0))],
        out_specs=[pl.BlockSpec((tm, ATTN_W), lambda i: (i, 0)), pl.BlockSpec((4, tm, LANE), lambda i: (0, i, 0)),
                   row, row, pl.BlockSpec((16, D_MODEL), lambda i: (0, 0)), pl.BlockSpec((8, SSM_W), lambda i: (0, 0))],
        out_shape=[jax.ShapeDtypeStruct((seq, ATTN_W), F32), jax.ShapeDtypeStruct((4, seq, LANE), F32),
                   jax.ShapeDtypeStruct((seq, D_MODEL), BF16), jax.ShapeDtypeStruct((seq, D_MODEL), BF16),
                   jax.ShapeDtypeStruct((16, D_MODEL), F32), jax.ShapeDtypeStruct((8, SSM_W), F32)],
        compiler_params=_params(("arbitrary",)),
    )(dx1, attn, s4, vec, vec512, w_out)


def _mlp_fwd(x1, vec, w_in_t, w_out, layer):
    seq = x1.shape[0]
    tm = ROW_TILE

    def body(x1_ref, vec_ref, wi_hbm, wo_hbm, x2_ref, wi, wo, sems):
        _load_once(pl.program_id(0), [(wi_hbm, wi), (wo_hbm, wo)], sems)
        x1v = x1_ref[...]
        n, _, _ = _rms_fwd(x1v, vec_ref[0, V_PRE_MLP:V_PRE_MLP + 1, :])
        h = (n * (1.0 + vec_ref[0, V_SC2:V_SC2 + 1, :]) + vec_ref[0, V_SH2:V_SH2 + 1, :]).astype(BF16)
        a = _nt(h, wi[...])
        r = jnp.square(jnp.maximum(a, 0.0)).astype(BF16)
        nf, _, _ = _rms_fwd(_nn(r, wo[...]), vec_ref[0, V_POST_MLP:V_POST_MLP + 1, :])
        x2_ref[...] = x1v + vec_ref[0, V_G2:V_G2 + 1, :] * nf

    row = pl.BlockSpec((tm, D_MODEL), lambda i: (i, 0))
    return pl.pallas_call(
        body, name="mlp_fwd", grid=(seq // tm,),
        in_specs=[row, pl.BlockSpec((1, 16, D_MODEL), lambda i: (layer, 0, 0)), HBM, HBM],
        out_specs=row, out_shape=jax.ShapeDtypeStruct((seq, D_MODEL), F32),
        scratch_shapes=[pltpu.VMEM((D_FF, D_MODEL), BF16), pltpu.VMEM((D_FF, D_MODEL), BF16),
                        pltpu.SemaphoreType.DMA((2,))],
        compiler_params=_params(("arbitrary",)),
    )(x1, vec, w_in_t, w_out)


def _mlp_bwd(dx2, x1, vec, w_in_t, w_out, layer):
    seq = x1.shape[0]
    tm = ROW_TILE

    def body(dx2_ref, x1_ref, vec_ref, wi_hbm, wo_hbm, dx1_ref, h_ref, da_ref, r_ref, df_ref, dvec_ref, wi, wo, sems):
        i = pl.program_id(0)
        _load_once(i, [(wi_hbm, wi), (wo_hbm, wo)], sems)

        @pl.when(i == 0)
        def _():
            dvec_ref[...] = jnp.zeros_like(dvec_ref)

        g_pre = vec_ref[0, V_PRE_MLP:V_PRE_MLP + 1, :]
        g_post = vec_ref[0, V_POST_MLP:V_POST_MLP + 1, :]
        sc2 = vec_ref[0, V_SC2:V_SC2 + 1, :]
        n, xh, xr = _rms_fwd(x1_ref[...], g_pre)
        h = (n * (1.0 + sc2) + vec_ref[0, V_SH2:V_SH2 + 1, :]).astype(BF16)
        h_ref[...] = h
        a = _nt(h, wi[...])
        relu = jnp.maximum(a, 0.0)
        r = jnp.square(relu).astype(BF16)
        r_ref[...] = r
        nf, fh, fr = _rms_fwd(_nn(r, wo[...]), g_post)
        dx2v = dx2_ref[...]
        dvec_ref[V_G2:V_G2 + 1, :] += _colsum(dx2v * nf)
        df, dgp_rows = _rms_bwd(dx2v * vec_ref[0, V_G2:V_G2 + 1, :], fh, fr, g_post)
        dvec_ref[V_POST_MLP:V_POST_MLP + 1, :] += _colsum(dgp_rows)
        dfb = df.astype(BF16)
        df_ref[...] = dfb
        da = (_nt(dfb, wo[...]) * (2.0 * relu)).astype(BF16)
        da_ref[...] = da
        dh = _nn(da, wi[...])
        dvec_ref[V_SH2:V_SH2 + 1, :] += _colsum(dh)
        dvec_ref[V_SC2:V_SC2 + 1, :] += _colsum(dh * n)
        dxn, dg_rows = _rms_bwd(dh * (1.0 + sc2), xh, xr, g_pre)
        dvec_ref[V_PRE_MLP:V_PRE_MLP + 1, :] += _colsum(dg_rows)
        dx1_ref[...] = dx2v + dxn

    row = pl.BlockSpec((tm, D_MODEL), lambda i: (i, 0))
    wide = pl.BlockSpec((tm, D_FF), lambda i: (i, 0))
    return pl.pallas_call(
        body, name="mlp_bwd", grid=(seq // tm,),
        in_specs=[row, row, pl.BlockSpec((1, 16, D_MODEL), lambda i: (layer, 0, 0)), HBM, HBM],
        out_specs=[row, row, wide, wide, row, pl.BlockSpec((16, D_MODEL), lambda i: (0, 0))],
        out_shape=[jax.ShapeDtypeStruct((seq, D_MODEL), F32), jax.ShapeDtypeStruct((seq, D_MODEL), BF16),
                   jax.ShapeDtypeStruct((seq, D_FF), BF16), jax.ShapeDtypeStruct((seq, D_FF), BF16),
                   jax.ShapeDtypeStruct((seq, D_MODEL), BF16), jax.ShapeDtypeStruct((16, D_MODEL), F32)],
        scratch_shapes=[pltpu.VMEM((D_FF, D_MODEL), BF16), pltpu.VMEM((D_FF, D_MODEL), BF16),
                        pltpu.SemaphoreType.DMA((2,))],
        compiler_params=_params(("arbitrary",)),
    )(dx2, x1, vec, w_in_t, w_out)


def _loss_head(y, target):
    seq = y.shape[0]
    tm = ROW_TILE

    def body(y_ref, t_ref, dy_ref, part_ref):
        e = y_ref[...] - t_ref[...]
        dy_ref[...] = e * (1.0 / D_MODEL)
        tot = jnp.sum(jnp.sum(e * e, axis=1, keepdims=True), axis=0, keepdims=True) * (0.5 / D_MODEL)
        part_ref[0] = jnp.broadcast_to(tot, (SUBLANE, LANE))

    row = pl.BlockSpec((tm, D_MODEL), lambda i: (i, 0))
    return pl.pallas_call(
        body, name="loss_head", grid=(seq // tm,),
        in_specs=[row, row],
        out_specs=[row, pl.BlockSpec((1, SUBLANE, LANE), lambda i: (i, 0, 0))],
        out_shape=[jax.ShapeDtypeStruct((seq, D_MODEL), F32), jax.ShapeDtypeStruct((seq // tm, SUBLANE, LANE), F32)],
        compiler_params=_params(("parallel",)),
    )(y, target)


def _matmul_tn(a, b, out_dtype, name, pieces=1):
    kk, m = a.shape
    n = b.shape[1]
    tm = min(m, 512)
    tn = n // pieces if pieces > 1 else min(n, 1280)
    tk = min(kk, 2048)
    nk = kk // tk

    def body(a_ref, b_ref, o_ref, acc):
        k = pl.program_id(2)

        @pl.when(k == 0)
        def _():
            acc[...] = jnp.zeros_like(acc)

        acc[...] += _tn(a_ref[...], b_ref[...])

        @pl.when(k == nk - 1)
        def _():
            if pieces > 1:
                o_ref[0] = acc[...].astype(out_dtype)
            else:
                o_ref[...] = acc[...].astype(out_dtype)

    if pieces > 1:
        out_spec = pl.BlockSpec((1, tm, tn), lambda i, j, k: (j, i, 0))
        out_shape = jax.ShapeDtypeStruct((pieces, m, tn), out_dtype)
    else:
        out_spec = pl.BlockSpec((tm, tn), lambda i, j, k: (i, j))
        out_shape = jax.ShapeDtypeStruct((m, n), out_dtype)
    return pl.pallas_call(
        body, name=name, grid=(m // tm, n // tn, nk),
        in_specs=[pl.BlockSpec((tk, tm), lambda i, j, k: (k, i)), pl.BlockSpec((tk, tn), lambda i, j, k: (k, j))],
        out_specs=out_spec, out_shape=out_shape,
        scratch_shapes=[pltpu.VMEM((tm, tn), F32)],
        compiler_params=_params(("parallel", "parallel", "arbitrary")),
    )(a, b)


def _attn_probs(i, q_h, kband, slope, sink):
    rr = lax.broadcasted_iota(jnp.int32, (BLOCK, 2 * BLOCK), 0)
    jj = lax.broadcasted_iota(jnp.int32, (BLOCK, 2 * BLOCK), 1)
    diff = BLOCK + rr - jj
    valid = (diff >= 0) & (diff < WINDOW) & ((jj >= BLOCK) | (i > 0))
    s = _nt(q_h, kband) * (HEAD_DIM ** -0.5)
    s = jnp.where(valid, s - slope * diff.astype(F32), NEG_INF)
    m = jnp.maximum(jnp.max(s, axis=1, keepdims=True), sink)
    p = jnp.exp(s - m)
    ps = jnp.exp(sink - m)
    inv = 1.0 / (jnp.sum(p, axis=1, keepdims=True) + ps)
    return p * inv, ps * inv


def _bands(kvp, kvc, h):
    kband = jnp.concatenate([kvp[:, h * HEAD_DIM:(h + 1) * HEAD_DIM], kvc[:, h * HEAD_DIM:(h + 1) * HEAD_DIM]], axis=0)
    v0 = KV_W + h * HEAD_DIM
    vband = jnp.concatenate([kvp[:, v0:v0 + HEAD_DIM], kvc[:, v0:v0 + HEAD_DIM]], axis=0)
    return kband, vband


def _attn_fwd(q, kv, sinks):
    seq = q.shape[0]
    nb = seq // BLOCK
    slopes = _alibi_slopes()

    def body(sink_ref, q_ref, kvp_ref, kvc_ref, o_ref):
        i = pl.program_id(0)
        qv, kvp, kvc = q_ref[...], kvp_ref[...], kvc_ref[...]
        for h in range(N_KV):
            kband, vband = _bands(kvp, kvc, h)
            for g in range(Q_PER_KV):
                hq = h * Q_PER_KV + g
                cols = slice(hq * HEAD_DIM, (hq + 1) * HEAD_DIM)
                pr, _ = _attn_probs(i, qv[:, cols], kband, slopes[hq], sink_ref[hq])
                o_ref[:, cols] = _nn(pr.astype(BF16), vband)

    return pl.pallas_call(
        body, name="attn_fwd", grid=(nb,),
        in_specs=[pl.BlockSpec(memory_space=pltpu.SMEM),
                  pl.BlockSpec((BLOCK, ATTN_W), lambda i: (i, 0)),
                  pl.BlockSpec((BLOCK, 2 * KV_W), lambda i: (jnp.maximum(i - 1, 0), 0)),
                  pl.BlockSpec((BLOCK, 2 * KV_W), lambda i: (i, 0))],
        out_specs=pl.BlockSpec((BLOCK, ATTN_W), lambda i: (i, 0)),
        out_shape=jax.ShapeDtypeStruct((seq, ATTN_W), F32),
        compiler_params=_params(("parallel",)),
    )(sinks, q, kv, kv)


def _attn_bwd(q, kv, sinks, dout):
    seq = q.shape[0]
    nb = seq // BLOCK
    slopes = _alibi_slopes()
    scale = HEAD_DIM ** -0.5

    def body(sink_ref, q_ref, kvp_ref, kvc_ref, do_ref, dq_ref, dkv_ref, dsk_ref, prev):
        step = pl.program_id(0)
        i = nb - 1 - step

        @pl.when(step == 0)
        def _():
            prev[...] = jnp.zeros_like(prev)

        qv, kvp, kvc = q_ref[...], kvp_ref[...], kvc_ref[...]
        dov = do_ref[...].astype(BF16)
        dk, dv, dsk = [], [], []
        for h in range(N_KV):
            kband, vband = _bands(kvp, kvc, h)
            dk_h = jnp.zeros((2 * BLOCK, HEAD_DIM), F32)
            dv_h = jnp.zeros((2 * BLOCK, HEAD_DIM), F32)
            for g in range(Q_PER_KV):
                hq = h * Q_PER_KV + g
                cols = slice(hq * HEAD_DIM, (hq + 1) * HEAD_DIM)
                q_h, do_h = qv[:, cols], dov[:, cols]
                pr, ps = _attn_probs(i, q_h, kband, slopes[hq], sink_ref[hq])
                dp = _nt(do_h, vband)
                delta = jnp.sum(pr * dp, axis=1, keepdims=True)
                ds = (pr * (dp - delta) * scale).astype(BF16)
                dsk.append(jnp.broadcast_to(-_colsum(ps * delta), (1, LANE)))
                dq_ref[:, cols] = _nn(ds, kband)
                dk_h = dk_h + _tn(ds, q_h)
                dv_h = dv_h + _tn(pr.astype(BF16), do_h)
            dk.append(dk_h)
            dv.append(dv_h)
        band = jnp.concatenate(dk + dv, axis=1)
        dkv_ref[...] = band[BLOCK:, :] + prev[...]
        prev[...] = band[:BLOCK, :]
        dsk_ref[0] = jnp.concatenate(dsk, axis=0)

    return pl.pallas_call(
        body, name="attn_bwd", grid=(nb,),
        in_specs=[pl.BlockSpec(memory_space=pltpu.SMEM),
                  pl.BlockSpec((BLOCK, ATTN_W), lambda s: (nb - 1 - s, 0)),
                  pl.BlockSpec((BLOCK, 2 * KV_W), lambda s: (jnp.maximum(nb - 2 - s, 0), 0)),
                  pl.BlockSpec((BLOCK, 2 * KV_W), lambda s: (nb - 1 - s, 0)),
                  pl.BlockSpec((BLOCK, ATTN_W), lambda s: (nb - 1 - s, 0))],
        out_specs=[pl.BlockSpec((BLOCK, ATTN_W), lambda s: (nb - 1 - s, 0)),
                   pl.BlockSpec((BLOCK, 2 * KV_W), lambda s: (nb - 1 - s, 0)),
                   pl.BlockSpec((1, N_Q, LANE), lambda s: (nb - 1 - s, 0, 0))],
        out_shape=[jax.ShapeDtypeStruct((seq, ATTN_W), F32), jax.ShapeDtypeStruct((seq, 2 * KV_W), F32),
                   jax.ShapeDtypeStruct((nb, N_Q, LANE), F32)],
        scratch_shapes=[pltpu.VMEM((BLOCK, 2 * KV_W), F32)],
        compiler_params=_params(("arbitrary",)),
    )(sinks, q, kv, kv, dout)


def _discretize(lr, li, ldt, br, bi):
    dt = jnp.exp(ldt)
    mag = jnp.exp(lr * dt)
    ang = li * dt
    ab_r = mag * jnp.cos(ang)
    ab_i = mag * jnp.sin(ang)
    nr = ab_r - 1.0
    ni = ab_i
    den = lr * lr + li * li
    f_r = (nr * lr + ni * li) / den
    f_i = (ni * lr - nr * li) / den
    return ab_r, ab_i, f_r * br - f_i * bi, f_r * bi + f_i * br


def _ssm_prepare(lr, li, ldt, br, bi):
    n = lr.shape[0]
    tn = N_CH
    col = pl.BlockSpec((tn, 1), lambda i: (i, 0))
    mat = pl.BlockSpec((tn, GROUP_W), lambda i: (i, 0))

    def body(lr_ref, li_ref, ldt_ref, br_ref, bi_ref, ar_ref, ai_ref, bbr_ref, bbi_ref):
        ar, ai, bbr, bbi = _discretize(lr_ref[...], li_ref[...], ldt_ref[...], br_ref[...], bi_ref[...])
        ar_ref[...] = ar
        ai_ref[...] = ai
        bbr_ref[...] = bbr
        bbi_ref[...] = bbi

    cs = jax.ShapeDtypeStruct((n, 1), F32)
    ms = jax.ShapeDtypeStruct((n, GROUP_W), F32)
    return pl.pallas_call(
        body, name="ssm_prepare", grid=(n // tn,),
        in_specs=[col, col, col, mat, mat], out_specs=[col, col, mat, mat], out_shape=[cs, cs, ms, ms],
        compiler_params=_params(("parallel",)),
    )(lr, li, ldt, br, bi)


def _ssm_prepare_bwd(lr, li, ldt, br, bi, dar, dai, dbbr, dbbi):
    n = lr.shape[0]
    tn = N_CH
    col = pl.BlockSpec((tn, 1), lambda i: (i, 0))
    mat = pl.BlockSpec((tn, GROUP_W), lambda i: (i, 0))

    def body(lr_ref, li_ref, ldt_ref, br_ref, bi_ref, dar_ref, dai_ref, dbbr_ref, dbbi_ref,
             dlr_ref, dli_ref, dldt_ref, dbr_ref, dbi_ref):
        _, vjp = jax.vjp(_discretize, lr_ref[...], li_ref[...], ldt_ref[...], br_ref[...], bi_ref[...])
        dlr, dli, dldt, dbr, dbi = vjp((dar_ref[...], dai_ref[...], dbbr_ref[...], dbbi_ref[...]))
        dlr_ref[...] = dlr
        dli_ref[...] = dli
        dldt_ref[...] = dldt
        dbr_ref[...] = dbr
        dbi_ref[...] = dbi

    cs = jax.ShapeDtypeStruct((n, 1), F32)
    ms = jax.ShapeDtypeStruct((n, GROUP_W), F32)
    return pl.pallas_call(
        body, name="ssm_prepare_bwd", grid=(n // tn,),
        in_specs=[col, col, col, mat, mat, col, col, mat, mat],
        out_specs=[col, col, col, mat, mat], out_shape=[cs, cs, cs, ms, ms],
        compiler_params=_params(("parallel",)),
    )(lr, li, ldt, br, bi, dar, dai, dbbr, dbbi)


def _load_slabs(src4_ref, dst):
    for s in range(STEPS):
        dst[s * SEGS:(s + 1) * SEGS, :] = jnp.concatenate(
            [src4_ref[j, pl.ds(s, SEGS, stride=STEPS), :] for j in range(4)], axis=1)


def _store_slabs(src, dst4_ref):
    for s in range(STEPS):
        for j in range(4):
            dst4_ref[j, pl.ds(s, SEGS, stride=STEPS), :] = src[s * SEGS:(s + 1) * SEGS, j * LANE:(j + 1) * LANE]


def _power_table(ar_ref, ai_ref, pwr, pwi):
    ar, ai = ar_ref[0], ai_ref[0]
    pr, pi = ar, ai
    pwr[0:1, :] = pr
    pwi[0:1, :] = pi
    for k in range(1, STEPS):
        pr, pi = pr * ar - pi * ai, pr * ai + pi * ar
        pwr[k:k + 1, :] = pr
        pwi[k:k + 1, :] = pi


def _scan_states(ubf, ar_ref, ai_ref, bbr_ref, bbi_ref, pwr, pwi, cin_r, cin_i, hr, hi):
    for k in range(2):
        rows = slice(k * 256, (k + 1) * 256)
        cols = slice(k * HALF_CH, (k + 1) * HALF_CH)
        hr[:, cols] = _nn(ubf[:, rows], bbr_ref[0, k])
        hi[:, cols] = _nn(ubf[:, rows], bbi_ref[0, k])
    for st in range(N_CH // STRIP):
        cs = slice(st * STRIP, (st + 1) * STRIP)
        arb = jnp.broadcast_to(ar_ref[0, :, cs], (SEGS, STRIP))
        aib = jnp.broadcast_to(ai_ref[0, :, cs], (SEGS, STRIP))

        def step(s, carry, cs=cs, arb=arb, aib=aib):
            cr, ci = carry
            rows = pl.ds(pl.multiple_of(s * SEGS, SEGS), SEGS)
            nr = arb * cr - aib * ci + hr[rows, cs]
            ni = arb * ci + aib * cr + hi[rows, cs]
            hr[rows, cs] = nr
            hi[rows, cs] = ni
            return nr, ni

        zero = jnp.zeros((SEGS, STRIP), F32)
        lax.fori_loop(0, STEPS, step, (zero, zero), unroll=True)
    last = slice((STEPS - 1) * SEGS, STEPS * SEGS)
    end_r, end_i = hr[last, :], hi[last, :]
    a64r, a64i = pwr[STEPS - 1:STEPS, :], pwi[STEPS - 1:STEPS, :]
    cr, ci = cin_r, cin_i
    rows_r, rows_i = [], []
    for j in range(SEGS):
        rows_r.append(cr)
        rows_i.append(ci)
        cr, ci = (a64r * cr - a64i * ci + end_r[j:j + 1, :], a64r * ci + a64i * cr + end_i[j:j + 1, :])
    cm_r, cm_i = jnp.concatenate(rows_r, axis=0), jnp.concatenate(rows_i, axis=0)
    for st in range(N_CH // STRIP):
        cs = slice(st * STRIP, (st + 1) * STRIP)
        cmr, cmi = cm_r[:, cs], cm_i[:, cs]

        def fix(s, carry, cs=cs, cmr=cmr, cmi=cmi):
            rows = pl.ds(pl.multiple_of(s * SEGS, SEGS), SEGS)
            pr, pi = pwr[pl.ds(s, 1), cs], pwi[pl.ds(s, 1), cs]
            hr[rows, cs] = hr[rows, cs] + (pr * cmr - pi * cmi)
            hi[rows, cs] = hi[rows, cs] + (pr * cmi + pi * cmr)
            return carry

        lax.fori_loop(0, STEPS, fix, 0, unroll=True)
    return (cm_r, cm_i), (cr, ci)


def _ssm_outputs(u, hr, hi, crt_ref, cit_ref, vec512_ref, wg_ref):
    ys = []
    for k in range(2):
        cols = slice(k * HALF_CH, (k + 1) * HALF_CH)
        ys.append(_nn(hr[:, cols].astype(BF16), crt_ref[0, k]) - _nn(hi[:, cols].astype(BF16), cit_ref[0, k]))
    y = jnp.concatenate(ys, axis=1) + vec512_ref[0, H_DSKIP:H_DSKIP + 1, :] * u
    z, t = _gelu(y)
    gate = jax.nn.sigmoid(_nn(z.astype(BF16), wg_ref[...]) + vec512_ref[0, H_BGLU:H_BGLU + 1, :])
    return y, z, t, gate


def _ssm_specs(layer, nck, rev):
    def chunk(i):
        return nck - 1 - i if rev else i

    return [pl.BlockSpec((4, CHUNK, LANE), lambda i: (0, chunk(i), 0)),
            pl.BlockSpec((1, 1, N_CH), lambda i: (layer, 0, 0)),
            pl.BlockSpec((1, 1, N_CH), lambda i: (layer, 0, 0)),
            pl.BlockSpec((1, 2, 256, HALF_CH), lambda i: (layer, 0, 0, 0)),
            pl.BlockSpec((1, 2, 256, HALF_CH), lambda i: (layer, 0, 0, 0)),
            pl.BlockSpec((1, 2, HALF_CH, 256), lambda i: (layer, 0, 0, 0)),
            pl.BlockSpec((1, 2, HALF_CH, 256), lambda i: (layer, 0, 0, 0)),
            pl.BlockSpec((1, 8, SSM_W), lambda i: (layer, 0, 0)),
            pl.BlockSpec((SSM_W, SSM_W), lambda i: (0, 0))]


def _ssm_fwd(u4, a_r, a_i, bb_r, bb_i, c_rt, c_it, vec512, w_glu, layer):
    seq = u4.shape[1]
    nck = seq // CHUNK

    def body(u4_ref, ar_ref, ai_ref, bbr_ref, bbi_ref, crt_ref, cit_ref, vec512_ref, wg_ref,
             s4_ref, hcr_ref, hci_ref, hr, hi, pwr, pwi, car, cai, ubuf, obuf):
        i = pl.program_id(0)

        @pl.when(i == 0)
        def _():
            car[...] = jnp.zeros_like(car)
            cai[...] = jnp.zeros_like(cai)
            _power_table(ar_ref, ai_ref, pwr, pwi)

        _load_slabs(u4_ref, ubuf)
        u = ubuf[...]
        cin_r, cin_i = car[...], cai[...]
        hcr_ref[0] = jnp.broadcast_to(cin_r, (SEGS, N_CH))
        hci_ref[0] = jnp.broadcast_to(cin_i, (SEGS, N_CH))
        _, (er, ei) = _scan_states(u.astype(BF16), ar_ref, ai_ref, bbr_ref, bbi_ref, pwr, pwi, cin_r, cin_i, hr, hi)
        car[...] = er
        cai[...] = ei
        _, z, _, gate = _ssm_outputs(u, hr, hi, crt_ref, cit_ref, vec512_ref, wg_ref)
        obuf[...] = z * gate
        _store_slabs(obuf, s4_ref)

    return pl.pallas_call(
        body, name="ssm_fwd", grid=(nck,),
        in_specs=_ssm_specs(layer, nck, False),
        out_specs=[pl.BlockSpec((4, CHUNK, LANE), lambda i: (0, i, 0)),
                   pl.BlockSpec((1, SEGS, N_CH), lambda i: (i, 0, 0)),
                   pl.BlockSpec((1, SEGS, N_CH), lambda i: (i, 0, 0))],
        out_shape=[jax.ShapeDtypeStruct((4, seq, LANE), F32), jax.ShapeDtypeStruct((nck, SEGS, N_CH), F32),
                   jax.ShapeDtypeStruct((nck, SEGS, N_CH), F32)],
        scratch_shapes=[pltpu.VMEM((CHUNK, N_CH), F32), pltpu.VMEM((CHUNK, N_CH), F32),
                        pltpu.VMEM((STEPS, N_CH), F32), pltpu.VMEM((STEPS, N_CH), F32),
                        pltpu.VMEM((1, N_CH), F32), pltpu.VMEM((1, N_CH), F32),
                        pltpu.VMEM((CHUNK, SSM_W), F32), pltpu.VMEM((CHUNK, SSM_W), F32)],
        compiler_params=_params(("arbitrary",)),
    )(u4, a_r, a_i, bb_r, bb_i, c_rt, c_it, vec512, w_glu)


def _ssm_bwd(u4, ds4, hc_r, hc_i, a_r, a_i, bb_r, bb_i, c_rt, c_it, vec512, w_glu, layer):
    seq = u4.shape[1]
    nck = seq // CHUNK

    def body(u4_ref, ar_ref, ai_ref, bbr_ref, bbi_ref, crt_ref, cit_ref, vec512_ref, wg_ref, ds4_ref, hcr_ref, hci_ref,
             du4_ref, dbbr_out, dbbi_out, dcrt_out, dcit_out, dar_ref, dai_ref, dwg_ref, dvec_ref,
             hr, hi, gr, gi, pwr, pwi, gcr, gci, accr, acci, ubuf, dbuf, dbbr_ref, dbbi_ref, dcrt_ref, dcit_ref):
        i = pl.program_id(0)

        @pl.when(i == 0)
        def _():
            for ref in (gcr, gci, accr, acci, dbbr_ref, dbbi_ref, dcrt_ref, dcit_ref, dwg_ref, dvec_ref):
                ref[...] = jnp.zeros_like(ref)
            _power_table(ar_ref, ai_ref, pwr, pwi)

        _load_slabs(u4_ref, ubuf)
        u = ubuf[...]
        ubf = u.astype(BF16)
        cin_r, cin_i = hcr_ref[0, 0:1, :], hci_ref[0, 0:1, :]
        (cm_r, cm_i), _ = _scan_states(ubf, ar_ref, ai_ref, bbr_ref, bbi_ref, pwr, pwi, cin_r, cin_i, hr, hi)
        y, z, t, gate = _ssm_outputs(u, hr, hi, crt_ref, cit_ref, vec512_ref, wg_ref)
        _load_slabs(ds4_ref, dbuf)
        ds = dbuf[...]
        da = ds * z * gate * (1.0 - gate)
        dab = da.astype(BF16)
        dz = ds * gate + _nt(dab, wg_ref[...])
        dwg_ref[...] += _tn(z.astype(BF16), dab)
        dvec_ref[H_BGLU:H_BGLU + 1, :] += _colsum(da)
        dy = dz * _gelu_grad(y, t)
        dvec_ref[H_DSKIP:H_DSKIP + 1, :] += _colsum(dy * u)
        du_skip = dy * vec512_ref[0, H_DSKIP:H_DSKIP + 1, :]
        dyb = dy.astype(BF16)
        for k in range(2):
            rows = slice(k * 256, (k + 1) * 256)
            cols = slice(k * HALF_CH, (k + 1) * HALF_CH)
            dcrt_ref[k] += _tn(hr[:, cols].astype(BF16), dyb[:, rows])
            dcit_ref[k] -= _tn(hi[:, cols].astype(BF16), dyb[:, rows])
            gr[:, cols] = _nt(dyb[:, rows], crt_ref[0, k])
            gi[:, cols] = -_nt(dyb[:, rows], cit_ref[0, k])
        for st in range(N_CH // STRIP):
            cs = slice(st * STRIP, (st + 1) * STRIP)
            arb = jnp.broadcast_to(ar_ref[0, :, cs], (SEGS, STRIP))
            aib = jnp.broadcast_to(ai_ref[0, :, cs], (SEGS, STRIP))

            def step(k, carry, cs=cs, arb=arb, aib=aib):
                cr, ci = carry
                rows = pl.ds(pl.multiple_of((STEPS - 1 - k) * SEGS, SEGS), SEGS)
                nr = gr[rows, cs] + (arb * cr + aib * ci)
                ni = gi[rows, cs] + (arb * ci - aib * cr)
                gr[rows, cs] = nr
                gi[rows, cs] = ni
                return nr, ni

            zero = jnp.zeros((SEGS, STRIP), F32)
            lax.fori_loop(0, STEPS, step, (zero, zero), unroll=True)
        first_r, first_i = gr[0:SEGS, :], gi[0:SEGS, :]
        a64r, a64i = pwr[STEPS - 1:STEPS, :], pwi[STEPS - 1:STEPS, :]
        dr_, di_ = gcr[...], gci[...]
        rows_r, rows_i = [None] * SEGS, [None] * SEGS
        for j in range(SEGS - 1, -1, -1):
            rows_r[j], rows_i[j] = dr_, di_
            dr_, di_ = (first_r[j:j + 1, :] + (a64r * dr_ + a64i * di_), first_i[j:j + 1, :] + (a64r * di_ - a64i * dr_))
        gcr[...] = dr_
        gci[...] = di_
        dm_r, dm_i = jnp.concatenate(rows_r, axis=0), jnp.concatenate(rows_i, axis=0)
        for st in range(N_CH // STRIP):
            cs = slice(st * STRIP, (st + 1) * STRIP)
            dmr, dmi = dm_r[:, cs], dm_i[:, cs]

            def fixed(s, cs=cs, dmr=dmr, dmi=dmi):
                rows = pl.ds(pl.multiple_of(s * SEGS, SEGS), SEGS)
                pr, pi = pwr[pl.ds(STEPS - 1 - s, 1), cs], pwi[pl.ds(STEPS - 1 - s, 1), cs]
                g_r = gr[rows, cs] + (pr * dmr + pi * dmi)
                g_i = gi[rows, cs] + (pr * dmi - pi * dmr)
                gr[rows, cs] = g_r
                gi[rows, cs] = g_i
                return g_r, g_i

            g_r, g_i = fixed(jnp.int32(0))
            acc0 = (g_r * cm_r[:, cs] + g_i * cm_i[:, cs], g_i * cm_r[:, cs] - g_r * cm_i[:, cs])

            def step(s, carry, cs=cs, fixed=fixed):
                sr, si = carry
                g_r, g_i = fixed(s)
                prev = pl.ds(pl.multiple_of((s - 1) * SEGS, SEGS), SEGS)
                hpr, hpi = hr[prev, cs], hi[prev, cs]
                return sr + (g_r * hpr + g_i * hpi), si + (g_i * hpr - g_r * hpi)

            sr, si = lax.fori_loop(1, STEPS, step, acc0, unroll=True)
            accr[:, cs] += sr
            acci[:, cs] += si
        grb, gib = gr[...].astype(BF16), gi[...].astype(BF16)
        dus = []
        for k in range(2):
            rows = slice(k * 256, (k + 1) * 256)
            cols = slice(k * HALF_CH, (k + 1) * HALF_CH)
            dus.append(_nt(grb[:, cols], bbr_ref[0, k]) + _nt(gib[:, cols], bbi_ref[0, k]))
            dbbr_ref[k] += _tn(ubf[:, rows], grb[:, cols])
            dbbi_ref[k] += _tn(ubf[:, rows], gib[:, cols])
        dbuf[...] = jnp.concatenate(dus, axis=1) + du_skip
        _store_slabs(dbuf, du4_ref)

        @pl.when(i == nck - 1)
        def _():
            dar_ref[...] = _colsum(accr[...])
            dai_ref[...] = _colsum(acci[...])
            ng = N_GROUPS // 2
            for k in range(2):
                for acc, out in ((dbbr_ref, dbbr_out), (dbbi_ref, dbbi_out)):
                    out[k] = jnp.concatenate(
                        [acc[k, g * GROUP_W:(g + 1) * GROUP_W, g * STATE:(g + 1) * STATE] for g in range(ng)], axis=0)
                for acc, out in ((dcrt_ref, dcrt_out), (dcit_ref, dcit_out)):
                    out[k] = jnp.concatenate(
                        [acc[k, g * STATE:(g + 1) * STATE, g * GROUP_W:(g + 1) * GROUP_W] for g in range(ng)], axis=0)

    rev4 = pl.BlockSpec((4, CHUNK, LANE), lambda i: (0, nck - 1 - i, 0))
    hc_spec = pl.BlockSpec((1, SEGS, N_CH), lambda i: (nck - 1 - i, 0, 0))
    fixed2 = lambda shape: pl.BlockSpec(shape, lambda i: (0,) * len(shape))
    return pl.pallas_call(
        body, name="ssm_bwd", grid=(nck,),
        in_specs=_ssm_specs(layer, nck, True) + [rev4, hc_spec, hc_spec],
        out_specs=[rev4, fixed2((2, 256, STATE)), fixed2((2, 256, STATE)), fixed2((2, HALF_CH, GROUP_W)),
                   fixed2((2, HALF_CH, GROUP_W)), fixed2((1, N_CH)), fixed2((1, N_CH)), fixed2((SSM_W, SSM_W)),
                   fixed2((8, SSM_W))],
        out_shape=[jax.ShapeDtypeStruct((4, seq, LANE), F32),
                   jax.ShapeDtypeStruct((2, 256, STATE), F32), jax.ShapeDtypeStruct((2, 256, STATE), F32),
                   jax.ShapeDtypeStruct((2, HALF_CH, GROUP_W), F32), jax.ShapeDtypeStruct((2, HALF_CH, GROUP_W), F32),
                   jax.ShapeDtypeStruct((1, N_CH), F32), jax.ShapeDtypeStruct((1, N_CH), F32),
                   jax.ShapeDtypeStruct((SSM_W, SSM_W), F32), jax.ShapeDtypeStruct((8, SSM_W), F32)],
        scratch_shapes=[pltpu.VMEM((CHUNK, N_CH), F32), pltpu.VMEM((CHUNK, N_CH), F32),
                        pltpu.VMEM((CHUNK, N_CH), F32), pltpu.VMEM((CHUNK, N_CH), F32),
                        pltpu.VMEM((STEPS, N_CH), F32), pltpu.VMEM((STEPS, N_CH), F32),
                        pltpu.VMEM((1, N_CH), F32), pltpu.VMEM((1, N_CH), F32),
                        pltpu.VMEM((SEGS, N_CH), F32), pltpu.VMEM((SEGS, N_CH), F32),
                        pltpu.VMEM((CHUNK, SSM_W), F32), pltpu.VMEM((CHUNK, SSM_W), F32),
                        pltpu.VMEM((2, 256, HALF_CH), F32), pltpu.VMEM((2, 256, HALF_CH), F32),
                        pltpu.VMEM((2, HALF_CH, 256), F32), pltpu.VMEM((2, HALF_CH, 256), F32)],
        compiler_params=_params(("arbitrary",)),
    )(u4, a_r, a_i, bb_r, bb_i, c_rt, c_it, vec512, w_glu, ds4, hc_r, hc_i)


def _block_diag(t):
    nl, _, ng, a, b = t.shape
    eye = jnp.eye(ng, dtype=t.dtype)
    return jnp.einsum("gh,lkgab->lkgahb", eye, t).reshape(nl, 2, ng * a, ng * b)


def _local_step(x, loss_target, mod, p, comm):
    nl = mod.shape[0]
    pad1024 = jnp.zeros((nl, 16 - 10, D_MODEL), F32)
    vec = jnp.concatenate([mod.reshape(nl, N_MOD, D_MODEL), p["pre_mix_g"][:, None], p["post_mix_g"][:, None],
                           p["pre_mlp_g"][:, None], p["post_mlp_g"][:, None], pad1024], axis=1)
    vec512 = jnp.concatenate([p["attn_out_g"][:, None], p["ssm_out_g"][:, None], p["d_skip"][:, None],
                              p["b_glu"][:, None], jnp.zeros((nl, 4, SSM_W), F32)], axis=1)
    n_all = nl * N_CH
    lr = p["lam_re"].reshape(n_all, 1)
    li = p["lam_im"].reshape(n_all, 1)
    ldt = jnp.broadcast_to(p["log_dt"][:, :, None], (nl, N_GROUPS, STATE)).reshape(n_all, 1)
    br = p["b_re"].reshape(n_all, GROUP_W)
    bi = p["b_im"].reshape(n_all, GROUP_W)
    ab_r, ab_i, bb_r, bb_i = _ssm_prepare(lr, li, ldt, br, bi)
    a_r = ab_r.reshape(nl, 1, N_CH)
    a_i = ab_i.reshape(nl, 1, N_CH)

    def dense_b(bb):
        return _block_diag(bb.reshape(nl, 2, 16, STATE, GROUP_W).transpose(0, 1, 2, 4, 3)).astype(BF16)

    def dense_c(cc):
        return _block_diag(cc.reshape(nl, 2, 16, GROUP_W, STATE).transpose(0, 1, 2, 4, 3)).astype(BF16)

    bbr_d, bbi_d = dense_b(bb_r), dense_b(bb_i)
    crt_d, cit_d = dense_c(p["c_re"]), dense_c(p["c_im"])

    saved = []
    xl = x
    mixer_w, mlp_w = [None] * nl, [None] * nl
    for l in range(nl):
        mixer_w[l], tok = comm.mixer_weights(l, [xl, bbr_d, bbi_d, crt_d, cit_d] if l == 0 else xl)
        w_in_t, w_glu, w_out = mixer_w[l]
        q, kv, u4, h1 = _in_proj_fwd(xl, _after(vec, *tok), w_in_t, l)
        attn = _attn_fwd(q, kv, p["attn_sinks"][l])
        s4, hc_r, hc_i = _ssm_fwd(u4, a_r, a_i, bbr_d, bbi_d, crt_d, cit_d, vec512, w_glu, l)
        x1 = _out_proj_fwd(xl, attn, s4, vec, vec512, w_out, l)
        mlp_w[l] = comm.mlp_weights(l, x1)
        x2 = _mlp_fwd(x1, vec, mlp_w[l][0], mlp_w[l][1], l)
        saved.append((xl, q, kv, u4, h1, attn, s4, hc_r, hc_i, x1))
        xl = x2

    dx, loss_parts = _loss_head(xl, loss_target)
    loss = jnp.sum(loss_parts[:, 0, 0])

    dvec_l, dvec512_l, dsink_l = [None] * nl, [None] * nl, [None] * nl
    dab_r, dab_i, dbb_r, dbb_i, dc_re, dc_im = ([None] * nl for _ in range(6))
    toks = []
    for l in range(nl - 1, -1, -1):
        xl, q, kv, u4, h1, attn, s4, hc_r, hc_i, x1 = saved[l]
        w_in_t, w_glu, w_out = mixer_w[l]
        dx1, h2, da, r, df, dvec_m = _mlp_bwd(dx, x1, _after(vec, *toks), mlp_w[l][0], mlp_w[l][1], l)
        toks = comm.after_mlp_bwd(l, dx1)
        dw_mlp_out = _matmul_tn(r, df, BF16, "dw_mlp_out").reshape(4, D_FF // 4, D_MODEL)
        dw_mlp_in = _matmul_tn(h2, da, BF16, "dw_mlp_in", pieces=4)
        toks = toks + comm.mlp_grads(l, [dw_mlp_in, dw_mlp_out])
        dattn, ds4, heads, dmixed, dvec_o, dvec512_o = _out_proj_bwd(
            dx1, attn, s4, _after(vec, *toks), vec512, w_out, l)
        dw_out = _matmul_tn(heads, dmixed, BF16, "dw_out").reshape(4, D_MODEL // 4, D_MODEL)
        dq, dkv, dsk = _attn_bwd(q, kv, p["attn_sinks"][l], dattn)
        (du4, dbbr, dbbi, dcrt, dcit, dar, dai, dwg, dvec512_s) = _ssm_bwd(
            u4, ds4, hc_r, hc_i, a_r, a_i, bbr_d, bbi_d, crt_d, cit_d, vec512, w_glu, l)
        dw_glu = dwg.astype(BF16).reshape(4, SSM_W // 4, SSM_W)
        dx, dproj, dvec_i = _in_proj_bwd(dx1, dq, dkv, du4, xl, vec, w_in_t, l)
        toks = comm.after_in_proj_bwd(l, dx)
        dw_in = _matmul_tn(h1, dproj, BF16, "dw_in")
        toks = toks + comm.mixer_grads(l, [dw_in.reshape(D_MODEL, 4, IN_W // 4).transpose(1, 0, 2), dw_glu, dw_out])
        dvec_l[l] = dvec_m + dvec_o + dvec_i
        dvec512_l[l] = dvec512_o + dvec512_s
        dsink_l[l] = jnp.sum(dsk[:, :, 0], axis=0)
        dab_r[l], dab_i[l], dbb_r[l], dbb_i[l], dc_re[l], dc_im[l] = dar, dai, dbbr, dbbi, dcrt, dcit

    dvec = _after(jnp.stack(dvec_l), *toks)
    dvec512 = jnp.stack(dvec512_l)
    ng = N_GROUPS // 2

    def b_cols(d):
        return jnp.stack(d).reshape(nl, 2, ng, GROUP_W, STATE).transpose(0, 1, 2, 4, 3).reshape(n_all, GROUP_W)

    def c_param(d):
        return jnp.stack(d).reshape(nl, 2, ng, STATE, GROUP_W).transpose(0, 1, 2, 4, 3).reshape(c_shape)

    dbb_r_c, dbb_i_c = b_cols(dbb_r), b_cols(dbb_i)
    c_shape = (nl, N_GROUPS, GROUP_W, STATE)
    dlr, dli, dldt, dbr, dbi = _ssm_prepare_bwd(
        lr, li, ldt, br, bi, jnp.stack(dab_r).reshape(n_all, 1), jnp.stack(dab_i).reshape(n_all, 1), dbb_r_c, dbb_i_c)
    small = {
        "b_ada": dvec[:, :N_MOD].reshape(nl, N_MOD * D_MODEL),
        "pre_mix_g": dvec[:, V_PRE_MIX], "post_mix_g": dvec[:, V_POST_MIX],
        "pre_mlp_g": dvec[:, V_PRE_MLP], "post_mlp_g": dvec[:, V_POST_MLP],
        "attn_sinks": jnp.stack(dsink_l),
        "lam_re": dlr.reshape(nl, N_GROUPS, STATE), "lam_im": dli.reshape(nl, N_GROUPS, STATE),
        "log_dt": jnp.sum(dldt.reshape(nl, N_GROUPS, STATE), axis=-1),
        "b_re": dbr.reshape(nl, N_GROUPS, STATE, GROUP_W), "b_im": dbi.reshape(nl, N_GROUPS, STATE, GROUP_W),
        "c_re": c_param(dc_re), "c_im": c_param(dc_im),
        "d_skip": dvec512[:, H_DSKIP], "b_glu": dvec512[:, H_BGLU],
        "attn_out_g": dvec512[:, H_ATTN_G], "ssm_out_g": dvec512[:, H_SSM_G],
    }
    return loss, dx, small, small["b_ada"]


WEIGHTS = ["w_ada", "b_ada", "pre_mix_g", "w_in", "attn_sinks", "lam_re", "lam_im", "log_dt", "b_re", "b_im", "c_re",
           "c_im", "d_skip", "w_glu", "b_glu", "attn_out_g", "ssm_out_g", "w_out", "post_mix_g", "pre_mlp_g",
           "w_mlp_in", "w_mlp_out", "post_mlp_g"]
BIG = ["w_in", "w_glu", "w_out", "w_mlp_in", "w_mlp_out"]
SMALL = [n for n in WEIGHTS if n not in BIG and n != "w_ada"]
PACK_ROWS = 256


def _pack(parts):
    rows = []
    for n in SMALL:
        flat = parts[n].reshape(-1)
        pad = (-flat.shape[0]) % (PACK_ROWS * LANE)
        rows.append(jnp.pad(flat, (0, pad)).reshape(-1, LANE))
    return jnp.concatenate(rows, axis=0)


def _unpack(packed, shapes):
    out, r0 = {}, 0
    for n in SMALL:
        size = int(np.prod(shapes[n]))
        rows = -(-size // (PACK_ROWS * LANE)) * PACK_ROWS
        out[n] = packed[r0:r0 + rows].reshape(-1)[:size].reshape(shapes[n])
        r0 += rows
    return out


MIXER = ["w_in", "w_glu", "w_out"]
MLP = ["w_mlp_in", "w_mlp_out"]


class _Exchanges:
    def __init__(self, shards, wts, mom, var, chip):
        self.shards, self.wts, self.mom, self.var, self.chip = shards, wts, mom, var, chip
        self.chip_arr = jnp.reshape(chip, (1,)).astype(jnp.int32)
        self.nl = len(shards["w_in"])
        self.gathers, self.scatters, self.pairs = {}, {}, {}
        self.res = {n: None for n in BIG}

    def _start_gather(self, group, tag, l, after=()):
        srcs = [self.shards[n][l] for n in group]
        lands = [lax.dynamic_update_slice(lax.empty((4,) + s.shape, s.dtype), s[None], (self.chip, 0, 0)) for s in srcs]
        plan = _plan_gather(len(srcs))
        st = _exchange_start(f"gather_{tag}{l}_start", 3 * len(srcs), plan, srcs + lands, after)
        self.gathers[tag, l] = (plan, st)
        return st[3]

    def _wait_gather(self, tag, l, after):
        plan, st = self.gathers.pop((tag, l))
        n = len(st[2]) // 2
        bufs = _exchange_wait(f"gather_{tag}{l}_wait", 3 * n, plan, st, after)
        return [b.reshape(4 * b.shape[1], b.shape[2]) for b in bufs[n:]]

    def _start_layer(self, l, after):
        tok = self._start_gather(MIXER, "mixer", l, after)
        return [tok, self._start_gather(MLP, "mlp", l, [tok])]

    def begin(self, after):
        toks = self._start_layer(0, after)
        return toks + (self._start_layer(1, toks[1:]) if self.nl > 1 else [])

    def mixer_weights(self, l, after):
        w = self._wait_gather("mixer", l, after)
        toks = self._start_layer(l + 2, w[:1]) if l + 2 < self.nl else []
        return w, toks

    def mlp_weights(self, l, after):
        return self._wait_gather("mlp", l, after)

    def _start_scatter(self, tag, l, group, pieces):
        plan = _plan_scatter(len(pieces))
        st = _exchange_start(f"scatter_{tag}{l}_start", 3 * len(pieces), plan,
                             list(pieces) + [lax.empty(p.shape, p.dtype) for p in pieces])
        self.scatters[tag] = (l, group, plan, st)
        return [st[3]]

    def _finish_scatter(self, tag, after):
        l, group, plan, st = self.scatters.pop(tag)
        n = len(group)
        bufs = _exchange_wait(f"scatter_{tag}{l}_wait", 3 * n, plan, st, after)
        sums = [_sum_pieces(bufs[k], bufs[n + k], self.chip_arr, "sum_" + group[k]) for k in range(n)]
        plan2 = _plan_pair(n)
        st2 = _exchange_start(f"pair_{tag}{l}_start", n, plan2, sums + [lax.empty(s.shape, s.dtype) for s in sums])
        self.pairs[tag] = (l, group, plan2, st2)
        return [st2[3]]

    def _finish_pair(self, tag, after):
        l, group, plan, st = self.pairs.pop(tag)
        n = len(group)
        bufs = _exchange_wait(f"pair_{tag}{l}_wait", n, plan, st, after)
        for k, name in enumerate(group):
            self.res[name] = _adamw_layer([bufs[k], bufs[n + k]], self.wts[name], self.mom[name], self.var[name],
                                          l, self.res[name], "adamw_" + name)

    def after_mlp_bwd(self, l, after):
        toks = self._finish_scatter("mixer", after) if "mixer" in self.scatters else []
        if "mlp" in self.pairs:
            self._finish_pair("mlp", after)
        return toks

    def mlp_grads(self, l, pieces):
        return self._start_scatter("mlp", l, MLP, pieces)

    def after_in_proj_bwd(self, l, after):
        toks = self._finish_scatter("mlp", after)
        if "mixer" in self.pairs:
            self._finish_pair("mixer", after)
        return toks

    def mixer_grads(self, l, pieces):
        return self._start_scatter("mixer", l, MIXER, pieces)

    def finish_mixer_scatter(self, after):
        return self._finish_scatter("mixer", after)

    def finish_mlp(self, after):
        self._finish_pair("mlp", after)

    def finish_mixer(self, after):
        self._finish_pair("mixer", after)

    def results(self):
        return self.res


def kernel(x, c, w_ada, b_ada, pre_mix_g, w_in, attn_sinks, lam_re, lam_im, log_dt, b_re, b_im, c_re, c_im, d_skip, w_glu, b_glu, attn_out_g, ssm_out_g, w_out, post_mix_g, pre_mlp_g, w_mlp_in, w_mlp_out, post_mlp_g, loss_target, m_w_ada, m_b_ada, m_pre_mix_g, m_w_in, m_attn_sinks, m_lam_re, m_lam_im, m_log_dt, m_b_re, m_b_im, m_c_re, m_c_im, m_d_skip, m_w_glu, m_b_glu, m_attn_out_g, m_ssm_out_g, m_w_out, m_post_mix_g, m_pre_mlp_g, m_w_mlp_in, m_w_mlp_out, m_post_mlp_g, v_w_ada, v_b_ada, v_pre_mix_g, v_w_in, v_attn_sinks, v_lam_re, v_lam_im, v_log_dt, v_b_re, v_b_im, v_c_re, v_c_im, v_d_skip, v_w_glu, v_b_glu, v_attn_out_g, v_ssm_out_g, v_w_out, v_post_mix_g, v_pre_mlp_g, v_w_mlp_in, v_w_mlp_out, v_post_mlp_g):
    args = locals()
    wts = {n: args[n] for n in WEIGHTS}
    mom = {n: args["m_" + n] for n in WEIGHTS}
    var = {n: args["v_" + n] for n in WEIGHTS}
    nl = w_in.shape[0]
    ix, iy, ic = lax.axis_index("x"), lax.axis_index("y"), lax.axis_index("c")
    chip = 2 * ix + iy
    me = 4 * ix + 2 * iy + ic
    wcols = w_ada.shape[2]

    shards = {"w_in": [w_in[l].astype(BF16).T for l in range(nl)], "w_glu": [w_glu[l].astype(BF16) for l in range(nl)],
              "w_out": [w_out[l].astype(BF16) for l in range(nl)],
              "w_mlp_in": [w_mlp_in[l].astype(BF16).T for l in range(nl)],
              "w_mlp_out": [w_mlp_out[l].astype(BF16) for l in range(nl)]}
    comm = _Exchanges(shards, wts, mom, var, chip)

    c_all = _gather([c.reshape(1, 1, 1, D_MODEL)], "all", "gather_c")[0].reshape(8, D_MODEL)
    b_sh = lax.dynamic_slice(b_ada, (0, chip * wcols), (nl, wcols)).reshape(nl, 1, wcols)
    mod_sh = _ada_forward(c_all, w_ada, b_sh)
    mod_all = _gather([mod_sh.reshape(1, 1, nl * 8, wcols)], "chips", "gather_mod")[0]
    toks = comm.begin([mod_all])
    mod = lax.dynamic_index_in_dim(mod_all.reshape(4, nl, 8, wcols), me, axis=2, keepdims=False)
    mod = mod.transpose(1, 0, 2).reshape(nl, 4 * wcols)

    small_p = {n: wts[n] for n in SMALL}
    small_p["log_dt"] = _after(log_dt, *toks)
    loss, grad_x, small, dmod = _local_step(x[0], loss_target[0], mod, small_p, comm)
    loss = lax.psum(loss, ("x", "y", "c"))

    packed = _pack(small)
    rows = packed.shape[0]
    pair_plan = _plan_pair(1)
    pair_small = _exchange_start("pair_small_start", 1, pair_plan, [packed, lax.empty((rows, LANE), F32)])

    dmod = _after(dmod, pair_small[3])
    dmod_all = _gather([dmod.reshape(1, 1, nl, N_MOD * D_MODEL)], "all", "gather_dmod")[0][0]
    dmod_sh = lax.dynamic_slice(dmod_all, (0, 0, chip * wcols), (8, nl, wcols)).transpose(1, 0, 2)
    g_ada = _ada_weight_grad(c_all.T, dmod_sh)
    res = {"w_ada": _adamw(g_ada[:, None], w_ada, m_w_ada, v_w_ada, "adamw_w_ada")}

    comm.finish_mlp(res["w_ada"][0])
    toks = comm.finish_mixer_scatter(res["w_ada"][0])

    own, other = _exchange_wait("pair_small_wait", 1, pair_plan, pair_small, res["w_ada"][0])
    chip_sum = _after(_sum_list([own, other], "sum_pair_small"), *toks)
    quad_plan = _plan_gather(1)
    quad0 = lax.dynamic_update_slice(lax.empty((4, rows, LANE), F32), chip_sum[None], (chip, 0, 0))
    quad_small = _exchange_start("gather_small_start", 3, quad_plan, [chip_sum, quad0])
    comm.finish_mixer([comm.results()[n][0] for n in MLP] + [_after(dmod, quad_small[3])])
    res.update(comm.results())
    quad = _exchange_wait("gather_small_wait", 3, quad_plan, quad_small, [res[n][0] for n in BIG])[1]
    outs = _adamw(quad[None], _pack({n: wts[n] for n in SMALL})[None], _pack({n: mom[n] for n in SMALL})[None],
                  _pack({n: var[n] for n in SMALL})[None], "adamw_small")
    shapes = {n: wts[n].shape for n in SMALL}
    unpacked = [_unpack(o[0], shapes) for o in outs]
    for n in SMALL:
        res[n] = [u[n] for u in unpacked]

    return (loss, grad_x[None], *[res[n][0] for n in WEIGHTS], *[res[n][1] for n in WEIGHTS],
            *[res[n][2] for n in WEIGHTS], *[res[n][3] for n in WEIGHTS])
```

```python
import functools
import math

import numpy as np
import jax
import jax.numpy as jnp
from jax import lax
from jax.experimental import pallas as pl
from jax.experimental.pallas import tpu as pltpu

F32 = jnp.float32
BF16 = jnp.bfloat16

D_MODEL = 1024
ATTN_W = 512
SSM_W = 512
HEAD_DIM = 64
N_Q = 8
N_KV = 2
Q_PER_KV = 4
KV_W = 128
WINDOW = 128
BLOCK = 128
N_GROUPS = 32
GROUP_W = 16
STATE = 64
N_CH = N_GROUPS * STATE
HALF_CH = N_CH // 2
D_FF = 4096
IN_W = 1280
N_MOD = 6
EPS = 1e-6
NEG_INF = -1e30

ADAM_LR = 0.001
ADAM_B1 = 0.9
ADAM_B2 = 0.999
ADAM_EPS = 1e-08
ADAM_WD = 0.01
ADAM_STEP = 10

ROW_TILE = 256
CHUNK = 256
SEGS = 8
STEPS = CHUNK // SEGS
STRIP = 1024
VMEM_LIMIT_V7X = 56 * 1024 * 1024
LANE = 128
SUBLANE = 8

GELU_K0 = math.sqrt(2.0 / math.pi)
GELU_K1 = 0.044715

V_SH1, V_SC1, V_G1, V_SH2, V_SC2, V_G2, V_PRE_MIX, V_POST_MIX, V_PRE_MLP, V_POST_MLP = range(10)
H_ATTN_G, H_SSM_G, H_DSKIP, H_BGLU = range(4)

HBM = pl.BlockSpec(memory_space=pltpu.HBM)
SEM = pl.BlockSpec(memory_space=pltpu.SEMAPHORE)
EFFECT = pltpu.SideEffectType.DATAFLOW_SIDE_EFFECTING
MESH_ID = pl.DeviceIdType.MESH


def _nn(a, b):
    return lax.dot_general(a, b, (((1,), (0,)), ((), ())), preferred_element_type=F32)


def _nt(a, b):
    return lax.dot_general(a, b, (((1,), (1,)), ((), ())), preferred_element_type=F32)


def _tn(a, b):
    return lax.dot_general(a, b, (((0,), (0,)), ((), ())), preferred_element_type=F32)


def _params(sem):
    return pltpu.CompilerParams(dimension_semantics=sem, vmem_limit_bytes=VMEM_LIMIT_V7X)


def _rms_fwd(x, g):
    r = lax.rsqrt(jnp.mean(x * x, axis=-1, keepdims=True) + EPS)
    xh = x * r
    return xh * g, xh, r


def _rms_bwd(dy, xh, r, g):
    dxh = dy * g
    dx = r * (dxh - xh * jnp.mean(dxh * xh, axis=-1, keepdims=True))
    return dx, dy * xh


def _colsum(t):
    return jnp.sum(t, axis=0, keepdims=True)


def _gelu(y):
    t = jnp.tanh(GELU_K0 * (y + GELU_K1 * (y * y * y)))
    return 0.5 * y * (1.0 + t), t


def _gelu_grad(y, t):
    return 0.5 * (1.0 + t) + 0.5 * y * (1.0 - t * t) * GELU_K0 * (1.0 + 3.0 * GELU_K1 * y * y)


def _alibi_slopes():
    return [float(s) for s in 2.0 ** (-8.0 * np.arange(1, N_Q + 1) / N_Q)]


def _pick_rows(rows, bytes_per_row, budget):
    t = rows
    while t % (2 * SUBLANE) == 0 and t * bytes_per_row > budget:
        t //= 2
    return t


def _load_once(step, pairs, sems):
    @pl.when(step == 0)
    def _():
        cps = [pltpu.make_async_copy(src, dst, sems.at[k]) for k, (src, dst) in enumerate(pairs)]
        for cp in cps:
            cp.start()
        for cp in cps:
            cp.wait()


_GROUPS = {
    "all": ([(0, 0, 1), (0, 1, 0), (0, 1, 1), (1, 0, 0), (1, 0, 1), (1, 1, 0), (1, 1, 1)], (4, 2, 1), 8),
    "chips": ([(1, 0, 0), (0, 1, 0), (1, 1, 0)], (2, 1, 0), 4),
    "pair": ([(0, 0, 1)], (0, 0, 1), 2),
}


def _flip(v, f):
    return 1 - v if f else v


def _gather(arrs, kind, name):
    masks, wts, n = _GROUPS[kind]
    na, nm = len(arrs), len(masks)

    def body(*refs):
        ins, outs = refs[:na], refs[na:2 * na]
        ssem, rsem, lsem = refs[2 * na:]
        x, y, c = lax.axis_index("x"), lax.axis_index("y"), lax.axis_index("c")
        me = wts[0] * x + wts[1] * y + wts[2] * c
        local = [pltpu.make_async_copy(ins[k], outs[k].at[:, pl.ds(me, 1)], lsem.at[k]) for k in range(na)]
        for cp in local:
            cp.start()
        remote = []
        for k in range(na):
            for mi, (fx, fy, fc) in enumerate(masks):
                peer = (_flip(x, fx), _flip(y, fy), _flip(c, fc))
                remote.append(pltpu.make_async_remote_copy(
                    src_ref=ins[k], dst_ref=outs[k].at[:, pl.ds(me, 1)],
                    send_sem=ssem.at[k * nm + mi], recv_sem=rsem.at[k * nm + mi],
                    device_id=peer, device_id_type=MESH_ID))
        for cp in remote:
            cp.start()
        for cp in remote:
            cp.wait()
        for cp in local:
            cp.wait()

    outs = pl.pallas_call(
        body, name=name,
        out_shape=[jax.ShapeDtypeStruct((a.shape[0], n) + a.shape[2:], a.dtype) for a in arrs],
        in_specs=[HBM] * na, out_specs=[HBM] * na,
        scratch_shapes=[pltpu.SemaphoreType.DMA((na * nm,)), pltpu.SemaphoreType.DMA((na * nm,)),
                        pltpu.SemaphoreType.DMA((na,))],
    )(*arrs)
    return list(outs)


def _hbm(a):
    return pltpu.with_memory_space_constraint(a, pltpu.HBM)


def _after(x, *tokens):
    for t in tokens:
        x = x + t[0, 0].astype(x.dtype)
    return x


def _exchange_start(name, n_copies, plan, bufs, after=()):
    n, na = len(bufs), len(after)

    def body(*refs):
        ssem, rsem, token = refs[n + na], refs[n + na + 1], refs[2 * n + na + 2]
        for k, (src, dst, dev) in enumerate(plan(refs[:n])):
            pltpu.make_async_remote_copy(src_ref=src, dst_ref=dst, send_sem=ssem.at[k], recv_sem=rsem.at[k],
                                         device_id=dev, device_id_type=MESH_ID).start()
        token[...] = jnp.zeros_like(token)

    outs = pl.pallas_call(
        body, name=name,
        out_shape=(pltpu.SemaphoreType.DMA((n_copies,)), pltpu.SemaphoreType.DMA((n_copies,)),
                   *[pltpu.HBM(b.shape, b.dtype) for b in bufs], jax.ShapeDtypeStruct((SUBLANE, LANE), F32)),
        in_specs=[HBM] * n + [pl.BlockSpec(memory_space=pl.ANY)] * na,
        out_specs=(SEM, SEM, *[HBM] * n, pl.BlockSpec(memory_space=pltpu.VMEM)),
        input_output_aliases={i: 2 + i for i in range(n)},
        compiler_params=pltpu.CompilerParams(has_side_effects=EFFECT),
    )(*[_hbm(b) for b in bufs], *after)
    return outs[0], outs[1], list(outs[2:2 + n]), outs[2 + n]


def _exchange_wait(name, n_copies, plan, started, after):
    ssem, rsem, bufs, _ = started
    n = len(bufs)
    after = list(after) if isinstance(after, (list, tuple)) else [after]

    def body(*refs):
        ssem_ref, rsem_ref = refs[n], refs[n + 1]
        for k, (src, dst, dev) in enumerate(plan(refs[:n])):
            cp = pltpu.make_async_remote_copy(src_ref=src, dst_ref=dst, send_sem=ssem_ref.at[k], recv_sem=rsem_ref.at[k],
                                              device_id=dev, device_id_type=MESH_ID)
            cp.wait_send()
            cp.wait_recv()

    outs = pl.pallas_call(
        body, name=name,
        out_shape=tuple(pltpu.HBM(b.shape, b.dtype) for b in bufs),
        in_specs=[HBM] * n + [SEM, SEM] + [pl.BlockSpec(memory_space=pl.ANY)] * len(after), out_specs=tuple([HBM] * n),
        input_output_aliases={i: i for i in range(n)},
        compiler_params=pltpu.CompilerParams(has_side_effects=EFFECT),
    )(*bufs, ssem, rsem, *after)
    return list(outs)


def _position():
    x, y, c = lax.axis_index("x"), lax.axis_index("y"), lax.axis_index("c")
    return x, y, c, [(1 - x, y), (x, 1 - y), (1 - x, 1 - y)]


def _plan_gather(na):
    def plan(refs):
        x, y, c, chips = _position()
        return [(refs[k], refs[na + k].at[2 * x + y], (px, py, c)) for k in range(na) for px, py in chips]
    return plan


def _plan_scatter(na):
    def plan(refs):
        x, y, c, chips = _position()
        return [(refs[k].at[2 * px + py], refs[na + k].at[2 * x + y], (px, py, c))
                for k in range(na) for px, py in chips]
    return plan


def _plan_pair(na):
    def plan(refs):
        x, y, c, _ = _position()
        return [(refs[k], refs[na + k], (x, y, 1 - c)) for k in range(na)]
    return plan


def _sum_list(arrs, name):
    n = len(arrs)
    r, c = arrs[0].shape
    tr = _pick_rows(r, c * 4 * (n + 1), 4 << 20)

    def body(*refs):
        acc = refs[0][...].astype(F32)
        for j in range(1, n):
            acc = acc + refs[j][...].astype(F32)
        refs[n][...] = acc

    blk = pl.BlockSpec((tr, c), lambda i: (i, 0))
    return pl.pallas_call(
        body, name=name, grid=(r // tr,), in_specs=[blk] * n, out_specs=blk,
        out_shape=jax.ShapeDtypeStruct((r, c), F32), compiler_params=_params(("parallel",)),
    )(*arrs)


def _sum_pieces(own, recv, chip, name):
    _, r, c = own.shape
    tr = _pick_rows(r, c * 2 * 6, 4 << 20)

    def body(chip_ref, own_ref, recv_ref, o_ref):
        acc = own_ref[0].astype(F32)
        for j in range(4):
            acc = acc + jnp.where(chip_ref[0] == j, 0.0, recv_ref[j].astype(F32))
        o_ref[...] = acc.astype(BF16)

    return pl.pallas_call(
        body, name=name,
        grid_spec=pltpu.PrefetchScalarGridSpec(
            num_scalar_prefetch=1, grid=(r // tr,),
            in_specs=[pl.BlockSpec((1, tr, c), lambda i, chip_ref: (chip_ref[0], i, 0)),
                      pl.BlockSpec((4, tr, c), lambda i, chip_ref: (0, i, 0))],
            out_specs=pl.BlockSpec((tr, c), lambda i, chip_ref: (i, 0))),
        out_shape=jax.ShapeDtypeStruct((r, c), BF16), compiler_params=_params(("parallel",)),
    )(chip, own, recv)


def _adam_update(g, w, m, v):
    mn = ADAM_B1 * m + (1.0 - ADAM_B1) * g
    vn = ADAM_B2 * v + (1.0 - ADAM_B2) * jnp.square(g)
    m_hat = mn / (1.0 - ADAM_B1 ** ADAM_STEP)
    v_hat = vn / (1.0 - ADAM_B2 ** ADAM_STEP)
    return -ADAM_LR * (m_hat / (jnp.sqrt(v_hat) + ADAM_EPS) + ADAM_WD * w), mn, vn


def _adamw_layer(grads, w, m, v, layer, prev, name):
    ng = len(grads)
    nl, r, c = w.shape
    tr = _pick_rows(r, c * 4 * (ng + 7), 6 << 20)
    if prev is None:
        prev = [lax.empty((nl, r, c), F32) for _ in range(4)]

    def body(*refs):
        g = refs[0][...].astype(F32)
        for j in range(1, ng):
            g = g + refs[j][...].astype(F32)
        w_ref, m_ref, v_ref = refs[ng:ng + 3]
        go_ref, d_ref, mo_ref, vo_ref = refs[ng + 7:ng + 11]
        d, mn, vn = _adam_update(g, w_ref[0], m_ref[0], v_ref[0])
        go_ref[0] = g
        d_ref[0] = d
        mo_ref[0] = mn
        vo_ref[0] = vn

    gblk = pl.BlockSpec((tr, c), lambda i: (i, 0))
    blk = pl.BlockSpec((1, tr, c), lambda i: (layer, i, 0))
    keep = pl.BlockSpec(memory_space=pl.ANY)
    sds = jax.ShapeDtypeStruct((nl, r, c), F32)
    return pl.pallas_call(
        body, name=name, grid=(r // tr,),
        in_specs=[gblk] * ng + [blk] * 3 + [keep] * 4,
        out_specs=[blk] * 4, out_shape=[sds] * 4,
        input_output_aliases={ng + 3 + i: i for i in range(4)},
        compiler_params=_params(("parallel",)),
    )(*grads, w, m, v, *prev)


def _adamw(gs, w, m, v, name):
    a, s, r, c = gs.shape
    tr = _pick_rows(r, c * 4 * (s + 7), 6 << 20)

    def body(g_ref, w_ref, m_ref, v_ref, go_ref, d_ref, mo_ref, vo_ref):
        g = g_ref[0, 0].astype(F32)
        for j in range(1, s):
            g = g + g_ref[0, j].astype(F32)
        d, mn, vn = _adam_update(g, w_ref[0], m_ref[0], v_ref[0])
        go_ref[0] = g
        d_ref[0] = d
        mo_ref[0] = mn
        vo_ref[0] = vn

    blk = pl.BlockSpec((1, tr, c), lambda i, j: (i, j, 0))
    sds = jax.ShapeDtypeStruct((a, r, c), F32)
    return pl.pallas_call(
        body, name=name, grid=(a, r // tr),
        in_specs=[pl.BlockSpec((1, s, tr, c), lambda i, j: (i, 0, j, 0)), blk, blk, blk],
        out_specs=[blk, blk, blk, blk], out_shape=[sds, sds, sds, sds],
        compiler_params=_params(("parallel", "parallel")),
    )(gs, w, m, v)


def _ada_forward(c_all, w_ada, b_sh):
    nl, d, w = w_ada.shape
    tw = 512

    def body(c_ref, w_ref, b_ref, o_ref):
        cv = c_ref[...]
        act = (cv * jax.nn.sigmoid(cv)).astype(BF16)
        o_ref[0] = _nn(act, w_ref[0].astype(BF16)) + b_ref[0]

    return pl.pallas_call(
        body, name="ada_forward", grid=(nl, w // tw),
        in_specs=[pl.BlockSpec((8, d), lambda l, j: (0, 0)),
                  pl.BlockSpec((1, d, tw), lambda l, j: (l, 0, j)),
                  pl.BlockSpec((1, 1, tw), lambda l, j: (l, 0, j))],
        out_specs=pl.BlockSpec((1, 8, tw), lambda l, j: (l, 0, j)),
        out_shape=jax.ShapeDtypeStruct((nl, 8, w), F32),
        compiler_params=_params(("parallel", "parallel")),
    )(c_all, w_ada, b_sh)


def _ada_weight_grad(c_all_t, dmod):
    nl, nb, w = dmod.shape
    d = c_all_t.shape[0]
    tw = 512

    def body(c_ref, g_ref, o_ref):
        cv = c_ref[...]
        act = cv * jax.nn.sigmoid(cv)
        gv = g_ref[0]
        acc = act[:, 0:1] * gv[0:1, :]
        for b in range(1, nb):
            acc = acc + act[:, b:b + 1] * gv[b:b + 1, :]
        o_ref[0] = acc

    return pl.pallas_call(
        body, name="ada_weight_grad", grid=(nl, w // tw),
        in_specs=[pl.BlockSpec((d, nb), lambda l, j: (0, 0)),
                  pl.BlockSpec((1, nb, tw), lambda l, j: (l, 0, j))],
        out_specs=pl.BlockSpec((1, d, tw), lambda l, j: (l, 0, j)),
        out_shape=jax.ShapeDtypeStruct((nl, d, w), F32),
        compiler_params=_params(("parallel", "parallel")),
    )(c_all_t, dmod)


def _in_proj_fwd(x, vec, w_in_t, layer):
    seq = x.shape[0]
    tm = ROW_TILE

    def body(x_ref, vec_ref, w_ref, q_ref, kv_ref, u4_ref, h_ref):
        n, _, _ = _rms_fwd(x_ref[...], vec_ref[0, V_PRE_MIX:V_PRE_MIX + 1, :])
        h = (n * (1.0 + vec_ref[0, V_SC1:V_SC1 + 1, :]) + vec_ref[0, V_SH1:V_SH1 + 1, :]).astype(BF16)
        h_ref[...] = h
        proj = _nt(h, w_ref[...])
        q_ref[...] = proj[:, :ATTN_W].astype(BF16)
        kv_ref[...] = proj[:, ATTN_W:ATTN_W + 2 * KV_W].astype(BF16)
        u0 = ATTN_W + 2 * KV_W
        for j in range(4):
            u4_ref[j] = proj[:, u0 + j * LANE:u0 + (j + 1) * LANE]

    return pl.pallas_call(
        body, name="in_proj_fwd", grid=(seq // tm,),
        in_specs=[pl.BlockSpec((tm, D_MODEL), lambda i: (i, 0)),
                  pl.BlockSpec((1, 16, D_MODEL), lambda i: (layer, 0, 0)),
                  pl.BlockSpec((IN_W, D_MODEL), lambda i: (0, 0))],
        out_specs=[pl.BlockSpec((tm, ATTN_W), lambda i: (i, 0)),
                   pl.BlockSpec((tm, 2 * KV_W), lambda i: (i, 0)),
                   pl.BlockSpec((4, tm, LANE), lambda i: (0, i, 0)),
                   pl.BlockSpec((tm, D_MODEL), lambda i: (i, 0))],
        out_shape=[jax.ShapeDtypeStruct((seq, ATTN_W), BF16), jax.ShapeDtypeStruct((seq, 2 * KV_W), BF16),
                   jax.ShapeDtypeStruct((4, seq, LANE), F32), jax.ShapeDtypeStruct((seq, D_MODEL), BF16)],
        compiler_params=_params(("parallel",)),
    )(x, vec, w_in_t)


def _in_proj_bwd(dx1, dq, dkv, du4, x, vec, w_in_t, layer):
    seq = x.shape[0]
    tm = ROW_TILE

    def body(dx1_ref, dq_ref, dkv_ref, du4_ref, x_ref, vec_ref, w_ref, dx_ref, dp_ref, dvec_ref):
        i = pl.program_id(0)

        @pl.when(i == 0)
        def _():
            dvec_ref[...] = jnp.zeros_like(dvec_ref)

        dproj = jnp.concatenate([dq_ref[...], dkv_ref[...]] + [du4_ref[j] for j in range(4)], axis=1).astype(BF16)
        dp_ref[...] = dproj
        dh = _nn(dproj, w_ref[...])
        g = vec_ref[0, V_PRE_MIX:V_PRE_MIX + 1, :]
        n, xh, r = _rms_fwd(x_ref[...], g)
        dn = dh * (1.0 + vec_ref[0, V_SC1:V_SC1 + 1, :])
        dxn, dg_rows = _rms_bwd(dn, xh, r, g)
        dx_ref[...] = dx1_ref[...] + dxn
        dvec_ref[V_SH1:V_SH1 + 1, :] += _colsum(dh)
        dvec_ref[V_SC1:V_SC1 + 1, :] += _colsum(dh * n)
        dvec_ref[V_PRE_MIX:V_PRE_MIX + 1, :] += _colsum(dg_rows)

    row = pl.BlockSpec((tm, D_MODEL), lambda i: (i, 0))
    return pl.pallas_call(
        body, name="in_proj_bwd", grid=(seq // tm,),
        in_specs=[row, pl.BlockSpec((tm, ATTN_W), lambda i: (i, 0)), pl.BlockSpec((tm, 2 * KV_W), lambda i: (i, 0)),
                  pl.BlockSpec((4, tm, LANE), lambda i: (0, i, 0)), row,
                  pl.BlockSpec((1, 16, D_MODEL), lambda i: (layer, 0, 0)),
                  pl.BlockSpec((IN_W, D_MODEL), lambda i: (0, 0))],
        out_specs=[row, pl.BlockSpec((tm, IN_W), lambda i: (i, 0)), pl.BlockSpec((16, D_MODEL), lambda i: (0, 0))],
        out_shape=[jax.ShapeDtypeStruct((seq, D_MODEL), F32), jax.ShapeDtypeStruct((seq, IN_W), BF16),
                   jax.ShapeDtypeStruct((16, D_MODEL), F32)],
        compiler_params=_params(("arbitrary",)),
    )(dx1, dq, dkv, du4, x, vec, w_in_t)


def _heads(attn_ref, s4_ref, vec512_ref):
    ga = vec512_ref[0, H_ATTN_G:H_ATTN_G + 1, :]
    gs = vec512_ref[0, H_SSM_G:H_SSM_G + 1, :]
    sv = jnp.concatenate([s4_ref[j] for j in range(4)], axis=1)
    na, ah, ar = _rms_fwd(attn_ref[...], ga)
    ns, sh, sr = _rms_fwd(sv, gs)
    return jnp.concatenate([na, ns], axis=1), (ah, ar, ga), (sh, sr, gs)


def _out_proj_fwd(x, attn, s4, vec, vec512, w_out, layer):
    seq = x.shape[0]
    tm = ROW_TILE

    def body(x_ref, attn_ref, s4_ref, vec_ref, vec512_ref, w_ref, x1_ref):
        heads, _, _ = _heads(attn_ref, s4_ref, vec512_ref)
        mixed = _nn(heads.astype(BF16), w_ref[...])
        nm, _, _ = _rms_fwd(mixed, vec_ref[0, V_POST_MIX:V_POST_MIX + 1, :])
        x1_ref[...] = x_ref[...] + vec_ref[0, V_G1:V_G1 + 1, :] * nm

    row = pl.BlockSpec((tm, D_MODEL), lambda i: (i, 0))
    return pl.pallas_call(
        body, name="out_proj_fwd", grid=(seq // tm,),
        in_specs=[row, pl.BlockSpec((tm, ATTN_W), lambda i: (i, 0)), pl.BlockSpec((4, tm, LANE), lambda i: (0, i, 0)),
                  pl.BlockSpec((1, 16, D_MODEL), lambda i: (layer, 0, 0)),
                  pl.BlockSpec((1, 8, SSM_W), lambda i: (layer, 0, 0)),
                  pl.BlockSpec((D_MODEL, D_MODEL), lambda i: (0, 0))],
        out_specs=row, out_shape=jax.ShapeDtypeStruct((seq, D_MODEL), F32),
        compiler_params=_params(("parallel",)),
    )(x, attn, s4, vec, vec512, w_out)


def _out_proj_bwd(dx1, attn, s4, vec, vec512, w_out, layer):
    seq = dx1.shape[0]
    tm = ROW_TILE

    def body(dx1_ref, attn_ref, s4_ref, vec_ref, vec512_ref, w_ref,
             dattn_ref, ds4_ref, heads_ref, dmixed_ref, dvec_ref, dvec512_ref):
        i = pl.program_id(0)

        @pl.when(i == 0)
        def _():
            dvec_ref[...] = jnp.zeros_like(dvec_ref)
            dvec512_ref[...] = jnp.zeros_like(dvec512_ref)

        heads, (ah, ar, ga), (sh, sr, gs) = _heads(attn_ref, s4_ref, vec512_ref)
        hb = heads.astype(BF16)
        heads_ref[...] = hb
        gm = vec_ref[0, V_POST_MIX:V_POST_MIX + 1, :]
        nm, mh, mr = _rms_fwd(_nn(hb, w_ref[...]), gm)
        dx1v = dx1_ref[...]
        dvec_ref[V_G1:V_G1 + 1, :] += _colsum(dx1v * nm)
        dmixed, dgm_rows = _rms_bwd(dx1v * vec_ref[0, V_G1:V_G1 + 1, :], mh, mr, gm)
        dvec_ref[V_POST_MIX:V_POST_MIX + 1, :] += _colsum(dgm_rows)
        dmb = dmixed.astype(BF16)
        dmixed_ref[...] = dmb
        dheads = _nt(dmb, w_ref[...])
        dattn, dga_rows = _rms_bwd(dheads[:, :ATTN_W], ah, ar, ga)
        ds, dgs_rows = _rms_bwd(dheads[:, ATTN_W:], sh, sr, gs)
        dattn_ref[...] = dattn
        for j in range(4):
            ds4_ref[j] = ds[:, j * LANE:(j + 1) * LANE]
        dvec512_ref[H_ATTN_G:H_ATTN_G + 1, :] += _colsum(dga_rows)
        dvec512_ref[H_SSM_G:H_SSM_G + 1, :] += _colsum(dgs_rows)

    row = pl.BlockSpec((tm, D_MODEL), lambda i: (i, 0))
    return pl.pallas_call(
        body, name="out_proj_bwd", grid=(seq // tm,),
        in_specs=[row, pl.BlockSpec((tm, ATTN_W), lambda i: (i, 0)), pl.BlockSpec((4, tm, LANE), lambda i: (0, i, 0)),
                  pl.BlockSpec((1, 16, D_MODEL), lambda i: (layer, 0, 0)),
                  pl.BlockSpec((1, 8, SSM_W), lambda i: (layer, 0, 0)),
                  pl.BlockSpec((D_MODEL, D_MODEL), lambda i: (0, 0))],
        out_specs=[pl.BlockSpec((tm, ATTN_W), lambda i: (i, 0)), pl.BlockSpec((4, tm, LANE), lambda i: (0, i, 0)),
                   row, row, pl.BlockSpec((16, D_MODEL), lambda i: (0, 0)), pl.BlockSpec((8, SSM_W), lambda i: (0, 0))],
        out_shape=[jax.ShapeDtypeStruct((seq, ATTN_W), F32), jax.ShapeDtypeStruct((4, seq, LANE), F32),
                   jax.ShapeDtypeStruct((seq, D_MODEL), BF16), jax.ShapeDtypeStruct((seq, D_MODEL), BF16),
                   jax.ShapeDtypeStruct((16, D_MODEL), F32), jax.ShapeDtypeStruct((8, SSM_W), F32)],
        compiler_params=_params(("arbitrary",)),
    )(dx1, attn, s4, vec, vec512, w_out)


def _mlp_fwd(x1, vec, w_in_t, w_out, layer):
    seq = x1.shape[0]
    tm = ROW_TILE

    def body(x1_ref, vec_ref, wi_hbm, wo_hbm, x2_ref, r_ref, f_ref, wi, wo, sems):
        _load_once(pl.program_id(0), [(wi_hbm, wi), (wo_hbm, wo)], sems)
        x1v = x1_ref[...]
        n, _, _ = _rms_fwd(x1v, vec_ref[0, V_PRE_MLP:V_PRE_MLP + 1, :])
        h = (n * (1.0 + vec_ref[0, V_SC2:V_SC2 + 1, :]) + vec_ref[0, V_SH2:V_SH2 + 1, :]).astype(BF16)
        a = _nt(h, wi[...])
        r = jnp.square(jnp.maximum(a, 0.0)).astype(BF16)
        r_ref[...] = r
        f = _nn(r, wo[...])
        f_ref[...] = f
        nf, _, _ = _rms_fwd(f, vec_ref[0, V_POST_MLP:V_POST_MLP + 1, :])
        x2_ref[...] = x1v + vec_ref[0, V_G2:V_G2 + 1, :] * nf

    row = pl.BlockSpec((tm, D_MODEL), lambda i: (i, 0))
    return pl.pallas_call(
        body, name="mlp_fwd", grid=(seq // tm,),
        in_specs=[row, pl.BlockSpec((1, 16, D_MODEL), lambda i: (layer, 0, 0)), HBM, HBM],
        out_specs=[row, pl.BlockSpec((tm, D_FF), lambda i: (i, 0)), row],
        out_shape=[jax.ShapeDtypeStruct((seq, D_MODEL), F32), jax.ShapeDtypeStruct((seq, D_FF), BF16),
                   jax.ShapeDtypeStruct((seq, D_MODEL), F32)],
        scratch_shapes=[pltpu.VMEM((D_FF, D_MODEL), BF16), pltpu.VMEM((D_FF, D_MODEL), BF16),
                        pltpu.SemaphoreType.DMA((2,))],
        compiler_params=_params(("arbitrary",)),
    )(x1, vec, w_in_t, w_out)


def _mlp_bwd(dx2, x1, r, f, vec, w_in_t, w_out, layer):
    seq = x1.shape[0]
    tm = ROW_TILE

    def body(dx2_ref, x1_ref, r_ref, f_ref, vec_ref, wi_hbm, wo_hbm, dx1_ref, h_ref, da_ref, df_ref, dvec_ref,
             wi, wo, sems):
        i = pl.program_id(0)
        _load_once(i, [(wi_hbm, wi), (wo_hbm, wo)], sems)

        @pl.when(i == 0)
        def _():
            dvec_ref[...] = jnp.zeros_like(dvec_ref)

        g_pre = vec_ref[0, V_PRE_MLP:V_PRE_MLP + 1, :]
        g_post = vec_ref[0, V_POST_MLP:V_POST_MLP + 1, :]
        sc2 = vec_ref[0, V_SC2:V_SC2 + 1, :]
        n, xh, xr = _rms_fwd(x1_ref[...], g_pre)
        h_ref[...] = (n * (1.0 + sc2) + vec_ref[0, V_SH2:V_SH2 + 1, :]).astype(BF16)
        relu = jnp.sqrt(r_ref[...].astype(F32))
        nf, fh, fr = _rms_fwd(f_ref[...], g_post)
        dx2v = dx2_ref[...]
        dvec_ref[V_G2:V_G2 + 1, :] += _colsum(dx2v * nf)
        df, dgp_rows = _rms_bwd(dx2v * vec_ref[0, V_G2:V_G2 + 1, :], fh, fr, g_post)
        dvec_ref[V_POST_MLP:V_POST_MLP + 1, :] += _colsum(dgp_rows)
        dfb = df.astype(BF16)
        df_ref[...] = dfb
        da = (_nt(dfb, wo[...]) * (2.0 * relu)).astype(BF16)
        da_ref[...] = da
        dh = _nn(da, wi[...])
        dvec_ref[V_SH2:V_SH2 + 1, :] += _colsum(dh)
        dvec_ref[V_SC2:V_SC2 + 1, :] += _colsum(dh * n)
        dxn, dg_rows = _rms_bwd(dh * (1.0 + sc2), xh, xr, g_pre)
        dvec_ref[V_PRE_MLP:V_PRE_MLP + 1, :] += _colsum(dg_rows)
        dx1_ref[...] = dx2v + dxn

    row = pl.BlockSpec((tm, D_MODEL), lambda i: (i, 0))
    wide = pl.BlockSpec((tm, D_FF), lambda i: (i, 0))
    return pl.pallas_call(
        body, name="mlp_bwd", grid=(seq // tm,),
        in_specs=[row, row, wide, row, pl.BlockSpec((1, 16, D_MODEL), lambda i: (layer, 0, 0)), HBM, HBM],
        out_specs=[row, row, wide, row, pl.BlockSpec((16, D_MODEL), lambda i: (0, 0))],
        out_shape=[jax.ShapeDtypeStruct((seq, D_MODEL), F32), jax.ShapeDtypeStruct((seq, D_MODEL), BF16),
                   jax.ShapeDtypeStruct((seq, D_FF), BF16),
                   jax.ShapeDtypeStruct((seq, D_MODEL), BF16), jax.ShapeDtypeStruct((16, D_MODEL), F32)],
        scratch_shapes=[pltpu.VMEM((D_FF, D_MODEL), BF16), pltpu.VMEM((D_FF, D_MODEL), BF16),
                        pltpu.SemaphoreType.DMA((2,))],
        compiler_params=_params(("arbitrary",)),
    )(dx2, x1, r, f, vec, w_in_t, w_out)


def _loss_head(y, target):
    seq = y.shape[0]
    tm = ROW_TILE

    def body(y_ref, t_ref, dy_ref, part_ref):
        e = y_ref[...] - t_ref[...]
        dy_ref[...] = e * (1.0 / D_MODEL)
        tot = jnp.sum(jnp.sum(e * e, axis=1, keepdims=True), axis=0, keepdims=True) * (0.5 / D_MODEL)
        part_ref[0] = jnp.broadcast_to(tot, (SUBLANE, LANE))

    row = pl.BlockSpec((tm, D_MODEL), lambda i: (i, 0))
    return pl.pallas_call(
        body, name="loss_head", grid=(seq // tm,),
        in_specs=[row, row],
        out_specs=[row, pl.BlockSpec((1, SUBLANE, LANE), lambda i: (i, 0, 0))],
        out_shape=[jax.ShapeDtypeStruct((seq, D_MODEL), F32), jax.ShapeDtypeStruct((seq // tm, SUBLANE, LANE), F32)],
        compiler_params=_params(("parallel",)),
    )(y, target)


def _matmul_tn(a, b, out_dtype, name, pieces=1):
    kk, m = a.shape
    n = b.shape[1]
    tm = min(m, 512)
    tn = n // pieces if pieces > 1 else min(n, 1280)
    tk = min(kk, 2048)
    nk = kk // tk

    def body(a_ref, b_ref, o_ref, acc):
        k = pl.program_id(2)

        @pl.when(k == 0)
        def _():
            acc[...] = jnp.zeros_like(acc)

        acc[...] += _tn(a_ref[...], b_ref[...])

        @pl.when(k == nk - 1)
        def _():
            if pieces > 1:
                o_ref[0] = acc[...].astype(out_dtype)
            else:
                o_ref[...] = acc[...].astype(out_dtype)

    if pieces > 1:
        out_spec = pl.BlockSpec((1, tm, tn), lambda i, j, k: (j, i, 0))
        out_shape = jax.ShapeDtypeStruct((pieces, m, tn), out_dtype)
    else:
        out_spec = pl.BlockSpec((tm, tn), lambda i, j, k: (i, j))
        out_shape = jax.ShapeDtypeStruct((m, n), out_dtype)
    return pl.pallas_call(
        body, name=name, grid=(m // tm, n // tn, nk),
        in_specs=[pl.BlockSpec((tk, tm), lambda i, j, k: (k, i)), pl.BlockSpec((tk, tn), lambda i, j, k: (k, j))],
        out_specs=out_spec, out_shape=out_shape,
        scratch_shapes=[pltpu.VMEM((tm, tn), F32)],
        compiler_params=_params(("parallel", "parallel", "arbitrary")),
    )(a, b)


def _attn_probs(i, q_h, kband, slope, sink):
    rr = lax.broadcasted_iota(jnp.int32, (BLOCK, 2 * BLOCK), 0)
    jj = lax.broadcasted_iota(jnp.int32, (BLOCK, 2 * BLOCK), 1)
    diff = BLOCK + rr - jj
    valid = (diff >= 0) & (diff < WINDOW) & ((jj >= BLOCK) | (i > 0))
    s = _nt(q_h, kband) * (HEAD_DIM ** -0.5)
    s = jnp.where(valid, s - slope * diff.astype(F32), NEG_INF)
    m = jnp.maximum(jnp.max(s, axis=1, keepdims=True), sink)
    p = jnp.exp(s - m)
    ps = jnp.exp(sink - m)
    inv = 1.0 / (jnp.sum(p, axis=1, keepdims=True) + ps)
    return p * inv, ps * inv


def _bands(kvp, kvc, h):
    kband = jnp.concatenate([kvp[:, h * HEAD_DIM:(h + 1) * HEAD_DIM], kvc[:, h * HEAD_DIM:(h + 1) * HEAD_DIM]], axis=0)
    v0 = KV_W + h * HEAD_DIM
    vband = jnp.concatenate([kvp[:, v0:v0 + HEAD_DIM], kvc[:, v0:v0 + HEAD_DIM]], axis=0)
    return kband, vband


def _attn_fwd(q, kv, sinks):
    seq = q.shape[0]
    nb = seq // BLOCK
    slopes = _alibi_slopes()

    def body(sink_ref, q_ref, kvp_ref, kvc_ref, o_ref):
        i = pl.program_id(0)
        qv, kvp, kvc = q_ref[...], kvp_ref[...], kvc_ref[...]
        for h in range(N_KV):
            kband, vband = _bands(kvp, kvc, h)
            for g in range(Q_PER_KV):
                hq = h * Q_PER_KV + g
                cols = slice(hq * HEAD_DIM, (hq + 1) * HEAD_DIM)
                pr, _ = _attn_probs(i, qv[:, cols], kband, slopes[hq], sink_ref[hq])
                o_ref[:, cols] = _nn(pr.astype(BF16), vband)

    return pl.pallas_call(
        body, name="attn_fwd", grid=(nb,),
        in_specs=[pl.BlockSpec(memory_space=pltpu.SMEM),
                  pl.BlockSpec((BLOCK, ATTN_W), lambda i: (i, 0)),
                  pl.BlockSpec((BLOCK, 2 * KV_W), lambda i: (jnp.maximum(i - 1, 0), 0)),
                  pl.BlockSpec((BLOCK, 2 * KV_W), lambda i: (i, 0))],
        out_specs=pl.BlockSpec((BLOCK, ATTN_W), lambda i: (i, 0)),
        out_shape=jax.ShapeDtypeStruct((seq, ATTN_W), F32),
        compiler_params=_params(("parallel",)),
    )(sinks, q, kv, kv)


def _attn_bwd(q, kv, sinks, dout):
    seq = q.shape[0]
    nb = seq // BLOCK
    slopes = _alibi_slopes()
    scale = HEAD_DIM ** -0.5

    def body(sink_ref, q_ref, kvp_ref, kvc_ref, do_ref, dq_ref, dkv_ref, dsk_ref, prev):
        step = pl.program_id(0)
        i = nb - 1 - step

        @pl.when(step == 0)
        def _():
            prev[...] = jnp.zeros_like(prev)

        qv, kvp, kvc = q_ref[...], kvp_ref[...], kvc_ref[...]
        dov = do_ref[...].astype(BF16)
        dk, dv, dsk = [], [], []
        for h in range(N_KV):
            kband, vband = _bands(kvp, kvc, h)
            dk_h = jnp.zeros((2 * BLOCK, HEAD_DIM), F32)
            dv_h = jnp.zeros((2 * BLOCK, HEAD_DIM), F32)
            for g in range(Q_PER_KV):
                hq = h * Q_PER_KV + g
                cols = slice(hq * HEAD_DIM, (hq + 1) * HEAD_DIM)
                q_h, do_h = qv[:, cols], dov[:, cols]
                pr, ps = _attn_probs(i, q_h, kband, slopes[hq], sink_ref[hq])
                dp = _nt(do_h, vband)
                delta = jnp.sum(pr * dp, axis=1, keepdims=True)
                ds = (pr * (dp - delta) * scale).astype(BF16)
                dsk.append(jnp.broadcast_to(-_colsum(ps * delta), (1, LANE)))
                dq_ref[:, cols] = _nn(ds, kband)
                dk_h = dk_h + _tn(ds, q_h)
                dv_h = dv_h + _tn(pr.astype(BF16), do_h)
            dk.append(dk_h)
            dv.append(dv_h)
        band = jnp.concatenate(dk + dv, axis=1)
        dkv_ref[...] = band[BLOCK:, :] + prev[...]
        prev[...] = band[:BLOCK, :]
        dsk_ref[0] = jnp.concatenate(dsk, axis=0)

    return pl.pallas_call(
        body, name="attn_bwd", grid=(nb,),
        in_specs=[pl.BlockSpec(memory_space=pltpu.SMEM),
                  pl.BlockSpec((BLOCK, ATTN_W), lambda s: (nb - 1 - s, 0)),
                  pl.BlockSpec((BLOCK, 2 * KV_W), lambda s: (jnp.maximum(nb - 2 - s, 0), 0)),
                  pl.BlockSpec((BLOCK, 2 * KV_W), lambda s: (nb - 1 - s, 0)),
                  pl.BlockSpec((BLOCK, ATTN_W), lambda s: (nb - 1 - s, 0))],
        out_specs=[pl.BlockSpec((BLOCK, ATTN_W), lambda s: (nb - 1 - s, 0)),
                   pl.BlockSpec((BLOCK, 2 * KV_W), lambda s: (nb - 1 - s, 0)),
                   pl.BlockSpec((1, N_Q, LANE), lambda s: (nb - 1 - s, 0, 0))],
        out_shape=[jax.ShapeDtypeStruct((seq, ATTN_W), F32), jax.ShapeDtypeStruct((seq, 2 * KV_W), F32),
                   jax.ShapeDtypeStruct((nb, N_Q, LANE), F32)],
        scratch_shapes=[pltpu.VMEM((BLOCK, 2 * KV_W), F32)],
        compiler_params=_params(("arbitrary",)),
    )(sinks, q, kv, kv, dout)


def _discretize(lr, li, ldt, br, bi):
    dt = jnp.exp(ldt)
    mag = jnp.exp(lr * dt)
    ang = li * dt
    ab_r = mag * jnp.cos(ang)
    ab_i = mag * jnp.sin(ang)
    nr = ab_r - 1.0
    ni = ab_i
    den = lr * lr + li * li
    f_r = (nr * lr + ni * li) / den
    f_i = (ni * lr - nr * li) / den
    return ab_r, ab_i, f_r * br - f_i * bi, f_r * bi + f_i * br


def _ssm_prepare(lr, li, ldt, br, bi):
    n = lr.shape[0]
    tn = N_CH
    col = pl.BlockSpec((tn, 1), lambda i: (i, 0))
    mat = pl.BlockSpec((tn, GROUP_W), lambda i: (i, 0))

    def body(lr_ref, li_ref, ldt_ref, br_ref, bi_ref, ar_ref, ai_ref, bbr_ref, bbi_ref):
        ar, ai, bbr, bbi = _discretize(lr_ref[...], li_ref[...], ldt_ref[...], br_ref[...], bi_ref[...])
        ar_ref[...] = ar
        ai_ref[...] = ai
        bbr_ref[...] = bbr
        bbi_ref[...] = bbi

    cs = jax.ShapeDtypeStruct((n, 1), F32)
    ms = jax.ShapeDtypeStruct((n, GROUP_W), F32)
    return pl.pallas_call(
        body, name="ssm_prepare", grid=(n // tn,),
        in_specs=[col, col, col, mat, mat], out_specs=[col, col, mat, mat], out_shape=[cs, cs, ms, ms],
        compiler_params=_params(("parallel",)),
    )(lr, li, ldt, br, bi)


def _ssm_prepare_bwd(lr, li, ldt, br, bi, dar, dai, dbbr, dbbi):
    n = lr.shape[0]
    tn = N_CH
    col = pl.BlockSpec((tn, 1), lambda i: (i, 0))
    mat = pl.BlockSpec((tn, GROUP_W), lambda i: (i, 0))

    def body(lr_ref, li_ref, ldt_ref, br_ref, bi_ref, dar_ref, dai_ref, dbbr_ref, dbbi_ref,
             dlr_ref, dli_ref, dldt_ref, dbr_ref, dbi_ref):
        _, vjp = jax.vjp(_discretize, lr_ref[...], li_ref[...], ldt_ref[...], br_ref[...], bi_ref[...])
        dlr, dli, dldt, dbr, dbi = vjp((dar_ref[...], dai_ref[...], dbbr_ref[...], dbbi_ref[...]))
        dlr_ref[...] = dlr
        dli_ref[...] = dli
        dldt_ref[...] = dldt
        dbr_ref[...] = dbr
        dbi_ref[...] = dbi

    cs = jax.ShapeDtypeStruct((n, 1), F32)
    ms = jax.ShapeDtypeStruct((n, GROUP_W), F32)
    return pl.pallas_call(
        body, name="ssm_prepare_bwd", grid=(n // tn,),
        in_specs=[col, col, col, mat, mat, col, col, mat, mat],
        out_specs=[col, col, col, mat, mat], out_shape=[cs, cs, cs, ms, ms],
        compiler_params=_params(("parallel",)),
    )(lr, li, ldt, br, bi, dar, dai, dbbr, dbbi)


def _load_slabs(src4_ref, dst):
    for s in range(STEPS):
        dst[s * SEGS:(s + 1) * SEGS, :] = jnp.concatenate(
            [src4_ref[j, pl.ds(s, SEGS, stride=STEPS), :] for j in range(4)], axis=1)


def _store_slabs(src, dst4_ref):
    for s in range(STEPS):
        for j in range(4):
            dst4_ref[j, pl.ds(s, SEGS, stride=STEPS), :] = src[s * SEGS:(s + 1) * SEGS, j * LANE:(j + 1) * LANE]


def _power_table(ar_ref, ai_ref, pwr, pwi):
    ar, ai = ar_ref[0], ai_ref[0]
    pr, pi = ar, ai
    pwr[0:1, :] = pr
    pwi[0:1, :] = pi
    for k in range(1, STEPS):
        pr, pi = pr * ar - pi * ai, pr * ai + pi * ar
        pwr[k:k + 1, :] = pr
        pwi[k:k + 1, :] = pi


def _scan_states(ubf, ar_ref, ai_ref, bbr_ref, bbi_ref, pwr, pwi, cin_r, cin_i, hr, hi):
    for k in range(2):
        rows = slice(k * 256, (k + 1) * 256)
        cols = slice(k * HALF_CH, (k + 1) * HALF_CH)
        hr[:, cols] = _nn(ubf[:, rows], bbr_ref[0, k])
        hi[:, cols] = _nn(ubf[:, rows], bbi_ref[0, k])
    for st in range(N_CH // STRIP):
        cs = slice(st * STRIP, (st + 1) * STRIP)
        arb = jnp.broadcast_to(ar_ref[0, :, cs], (SEGS, STRIP))
        aib = jnp.broadcast_to(ai_ref[0, :, cs], (SEGS, STRIP))

        def step(s, carry, cs=cs, arb=arb, aib=aib):
            cr, ci = carry
            rows = pl.ds(pl.multiple_of(s * SEGS, SEGS), SEGS)
            nr = arb * cr - aib * ci + hr[rows, cs]
            ni = arb * ci + aib * cr + hi[rows, cs]
            hr[rows, cs] = nr
            hi[rows, cs] = ni
            return nr, ni

        zero = jnp.zeros((SEGS, STRIP), F32)
        lax.fori_loop(0, STEPS, step, (zero, zero), unroll=True)
    last = slice((STEPS - 1) * SEGS, STEPS * SEGS)
    end_r, end_i = hr[last, :], hi[last, :]
    a64r, a64i = pwr[STEPS - 1:STEPS, :], pwi[STEPS - 1:STEPS, :]
    cr, ci = cin_r, cin_i
    rows_r, rows_i = [], []
    for j in range(SEGS):
        rows_r.append(cr)
        rows_i.append(ci)
        cr, ci = (a64r * cr - a64i * ci + end_r[j:j + 1, :], a64r * ci + a64i * cr + end_i[j:j + 1, :])
    cm_r, cm_i = jnp.concatenate(rows_r, axis=0), jnp.concatenate(rows_i, axis=0)
    for st in range(N_CH // STRIP):
        cs = slice(st * STRIP, (st + 1) * STRIP)
        cmr, cmi = cm_r[:, cs], cm_i[:, cs]

        def fix(s, carry, cs=cs, cmr=cmr, cmi=cmi):
            rows = pl.ds(pl.multiple_of(s * SEGS, SEGS), SEGS)
            pr, pi = pwr[pl.ds(s, 1), cs], pwi[pl.ds(s, 1), cs]
            hr[rows, cs] = hr[rows, cs] + (pr * cmr - pi * cmi)
            hi[rows, cs] = hi[rows, cs] + (pr * cmi + pi * cmr)
            return carry

        lax.fori_loop(0, STEPS, fix, 0, unroll=True)
    return (cm_r, cm_i), (cr, ci)


def _ssm_outputs(u, hr, hi, crt_ref, cit_ref, vec512_ref, wg_ref):
    ys = []
    for k in range(2):
        cols = slice(k * HALF_CH, (k + 1) * HALF_CH)
        ys.append(_nn(hr[:, cols].astype(BF16), crt_ref[0, k]) - _nn(hi[:, cols].astype(BF16), cit_ref[0, k]))
    y = jnp.concatenate(ys, axis=1) + vec512_ref[0, H_DSKIP:H_DSKIP + 1, :] * u
    z, t = _gelu(y)
    gate = jax.nn.sigmoid(_nn(z.astype(BF16), wg_ref[...]) + vec512_ref[0, H_BGLU:H_BGLU + 1, :])
    return y, z, t, gate


def _ssm_specs(layer, nck, rev):
    def chunk(i):
        return nck - 1 - i if rev else i

    return [pl.BlockSpec((4, CHUNK, LANE), lambda i: (0, chunk(i), 0)),
            pl.BlockSpec((1, 1, N_CH), lambda i: (layer, 0, 0)),
            pl.BlockSpec((1, 1, N_CH), lambda i: (layer, 0, 0)),
            pl.BlockSpec((1, 2, 256, HALF_CH), lambda i: (layer, 0, 0, 0)),
            pl.BlockSpec((1, 2, 256, HALF_CH), lambda i: (layer, 0, 0, 0)),
            pl.BlockSpec((1, 2, HALF_CH, 256), lambda i: (layer, 0, 0, 0)),
            pl.BlockSpec((1, 2, HALF_CH, 256), lambda i: (layer, 0, 0, 0)),
            pl.BlockSpec((1, 8, SSM_W), lambda i: (layer, 0, 0)),
            pl.BlockSpec((SSM_W, SSM_W), lambda i: (0, 0))]


def _ssm_fwd(u4, a_r, a_i, bb_r, bb_i, c_rt, c_it, vec512, w_glu, layer):
    seq = u4.shape[1]
    nck = seq // CHUNK

    def body(u4_ref, ar_ref, ai_ref, bbr_ref, bbi_ref, crt_ref, cit_ref, vec512_ref, wg_ref,
             s4_ref, hcr_ref, hci_ref, hr, hi, pwr, pwi, car, cai, ubuf, obuf):
        i = pl.program_id(0)

        @pl.when(i == 0)
        def _():
            car[...] = jnp.zeros_like(car)
            cai[...] = jnp.zeros_like(cai)
            _power_table(ar_ref, ai_ref, pwr, pwi)

        _load_slabs(u4_ref, ubuf)
        u = ubuf[...]
        cin_r, cin_i = car[...], cai[...]
        hcr_ref[0] = jnp.broadcast_to(cin_r, (SEGS, N_CH))
        hci_ref[0] = jnp.broadcast_to(cin_i, (SEGS, N_CH))
        _, (er, ei) = _scan_states(u.astype(BF16), ar_ref, ai_ref, bbr_ref, bbi_ref, pwr, pwi, cin_r, cin_i, hr, hi)
        car[...] = er
        cai[...] = ei
        _, z, _, gate = _ssm_outputs(u, hr, hi, crt_ref, cit_ref, vec512_ref, wg_ref)
        obuf[...] = z * gate
        _store_slabs(obuf, s4_ref)

    return pl.pallas_call(
        body, name="ssm_fwd", grid=(nck,),
        in_specs=_ssm_specs(layer, nck, False),
        out_specs=[pl.BlockSpec((4, CHUNK, LANE), lambda i: (0, i, 0)),
                   pl.BlockSpec((1, SEGS, N_CH), lambda i: (i, 0, 0)),
                   pl.BlockSpec((1, SEGS, N_CH), lambda i: (i, 0, 0))],
        out_shape=[jax.ShapeDtypeStruct((4, seq, LANE), F32), jax.ShapeDtypeStruct((nck, SEGS, N_CH), F32),
                   jax.ShapeDtypeStruct((nck, SEGS, N_CH), F32)],
        scratch_shapes=[pltpu.VMEM((CHUNK, N_CH), F32), pltpu.VMEM((CHUNK, N_CH), F32),
                        pltpu.VMEM((STEPS, N_CH), F32), pltpu.VMEM((STEPS, N_CH), F32),
                        pltpu.VMEM((1, N_CH), F32), pltpu.VMEM((1, N_CH), F32),
                        pltpu.VMEM((CHUNK, SSM_W), F32), pltpu.VMEM((CHUNK, SSM_W), F32)],
        compiler_params=_params(("arbitrary",)),
    )(u4, a_r, a_i, bb_r, bb_i, c_rt, c_it, vec512, w_glu)


def _ssm_bwd(u4, ds4, hc_r, hc_i, a_r, a_i, bb_r, bb_i, c_rt, c_it, vec512, w_glu, layer):
    seq = u4.shape[1]
    nck = seq // CHUNK

    def body(u4_ref, ar_ref, ai_ref, bbr_ref, bbi_ref, crt_ref, cit_ref, vec512_ref, wg_ref, ds4_ref, hcr_ref, hci_ref,
             du4_ref, dbbr_out, dbbi_out, dcrt_out, dcit_out, dar_ref, dai_ref, dwg_ref, dvec_ref,
             hr, hi, gr, gi, pwr, pwi, gcr, gci, accr, acci, ubuf, dbuf, dbbr_ref, dbbi_ref, dcrt_ref, dcit_ref):
        i = pl.program_id(0)

        @pl.when(i == 0)
        def _():
            for ref in (gcr, gci, accr, acci, dbbr_ref, dbbi_ref, dcrt_ref, dcit_ref, dwg_ref, dvec_ref):
                ref[...] = jnp.zeros_like(ref)
            _power_table(ar_ref, ai_ref, pwr, pwi)

        _load_slabs(u4_ref, ubuf)
        u = ubuf[...]
        ubf = u.astype(BF16)
        cin_r, cin_i = hcr_ref[0, 0:1, :], hci_ref[0, 0:1, :]
        (cm_r, cm_i), _ = _scan_states(ubf, ar_ref, ai_ref, bbr_ref, bbi_ref, pwr, pwi, cin_r, cin_i, hr, hi)
        y, z, t, gate = _ssm_outputs(u, hr, hi, crt_ref, cit_ref, vec512_ref, wg_ref)
        _load_slabs(ds4_ref, dbuf)
        ds = dbuf[...]
        da = ds * z * gate * (1.0 - gate)
        dab = da.astype(BF16)
        dz = ds * gate + _nt(dab, wg_ref[...])
        dwg_ref[...] += _tn(z.astype(BF16), dab)
        dvec_ref[H_BGLU:H_BGLU + 1, :] += _colsum(da)
        dy = dz * _gelu_grad(y, t)
        dvec_ref[H_DSKIP:H_DSKIP + 1, :] += _colsum(dy * u)
        du_skip = dy * vec512_ref[0, H_DSKIP:H_DSKIP + 1, :]
        dyb = dy.astype(BF16)
        for k in range(2):
            rows = slice(k * 256, (k + 1) * 256)
            cols = slice(k * HALF_CH, (k + 1) * HALF_CH)
            dcrt_ref[k] += _tn(hr[:, cols].astype(BF16), dyb[:, rows])
            dcit_ref[k] -= _tn(hi[:, cols].astype(BF16), dyb[:, rows])
            gr[:, cols] = _nt(dyb[:, rows], crt_ref[0, k])
            gi[:, cols] = -_nt(dyb[:, rows], cit_ref[0, k])
        for st in range(N_CH // STRIP):
            cs = slice(st * STRIP, (st + 1) * STRIP)
            arb = jnp.broadcast_to(ar_ref[0, :, cs], (SEGS, STRIP))
            aib = jnp.broadcast_to(ai_ref[0, :, cs], (SEGS, STRIP))

            def step(k, carry, cs=cs, arb=arb, aib=aib):
                cr, ci = carry
                rows = pl.ds(pl.multiple_of((STEPS - 1 - k) * SEGS, SEGS), SEGS)
                nr = gr[rows, cs] + (arb * cr + aib * ci)
                ni = gi[rows, cs] + (arb * ci - aib * cr)
                gr[rows, cs] = nr
                gi[rows, cs] = ni
                return nr, ni

            zero = jnp.zeros((SEGS, STRIP), F32)
            lax.fori_loop(0, STEPS, step, (zero, zero), unroll=True)
        first_r, first_i = gr[0:SEGS, :], gi[0:SEGS, :]
        a64r, a64i = pwr[STEPS - 1:STEPS, :], pwi[STEPS - 1:STEPS, :]
        dr_, di_ = gcr[...], gci[...]
        rows_r, rows_i = [None] * SEGS, [None] * SEGS
        for j in range(SEGS - 1, -1, -1):
            rows_r[j], rows_i[j] = dr_, di_
            dr_, di_ = (first_r[j:j + 1, :] + (a64r * dr_ + a64i * di_), first_i[j:j + 1, :] + (a64r * di_ - a64i * dr_))
        gcr[...] = dr_
        gci[...] = di_
        dm_r, dm_i = jnp.concatenate(rows_r, axis=0), jnp.concatenate(rows_i, axis=0)
        for st in range(N_CH // STRIP):
            cs = slice(st * STRIP, (st + 1) * STRIP)
            dmr, dmi = dm_r[:, cs], dm_i[:, cs]

            def fixed(s, cs=cs, dmr=dmr, dmi=dmi):
                rows = pl.ds(pl.multiple_of(s * SEGS, SEGS), SEGS)
                pr, pi = pwr[pl.ds(STEPS - 1 - s, 1), cs], pwi[pl.ds(STEPS - 1 - s, 1), cs]
                g_r = gr[rows, cs] + (pr * dmr + pi * dmi)
                g_i = gi[rows, cs] + (pr * dmi - pi * dmr)
                gr[rows, cs] = g_r
                gi[rows, cs] = g_i
                return g_r, g_i

            g_r, g_i = fixed(jnp.int32(0))
            acc0 = (g_r * cm_r[:, cs] + g_i * cm_i[:, cs], g_i * cm_r[:, cs] - g_r * cm_i[:, cs])

            def step(s, carry, cs=cs, fixed=fixed):
                sr, si = carry
                g_r, g_i = fixed(s)
                prev = pl.ds(pl.multiple_of((s - 1) * SEGS, SEGS), SEGS)
                hpr, hpi = hr[prev, cs], hi[prev, cs]
                return sr + (g_r * hpr + g_i * hpi), si + (g_i * hpr - g_r * hpi)

            sr, si = lax.fori_loop(1, STEPS, step, acc0, unroll=True)
            accr[:, cs] += sr
            acci[:, cs] += si
        grb, gib = gr[...].astype(BF16), gi[...].astype(BF16)
        dus = []
        for k in range(2):
            rows = slice(k * 256, (k + 1) * 256)
            cols = slice(k * HALF_CH, (k + 1) * HALF_CH)
            dus.append(_nt(grb[:, cols], bbr_ref[0, k]) + _nt(gib[:, cols], bbi_ref[0, k]))
            dbbr_ref[k] += _tn(ubf[:, rows], grb[:, cols])
            dbbi_ref[k] += _tn(ubf[:, rows], gib[:, cols])
        dbuf[...] = jnp.concatenate(dus, axis=1) + du_skip
        _store_slabs(dbuf, du4_ref)

        @pl.when(i == nck - 1)
        def _():
            dar_ref[...] = _colsum(accr[...])
            dai_ref[...] = _colsum(acci[...])
            ng = N_GROUPS // 2
            for k in range(2):
                for acc, out in ((dbbr_ref, dbbr_out), (dbbi_ref, dbbi_out)):
                    out[k] = jnp.concatenate(
                        [acc[k, g * GROUP_W:(g + 1) * GROUP_W, g * STATE:(g + 1) * STATE] for g in range(ng)], axis=0)
                for acc, out in ((dcrt_ref, dcrt_out), (dcit_ref, dcit_out)):
                    out[k] = jnp.concatenate(
                        [acc[k, g * STATE:(g + 1) * STATE, g * GROUP_W:(g + 1) * GROUP_W] for g in range(ng)], axis=0)

    rev4 = pl.BlockSpec((4, CHUNK, LANE), lambda i: (0, nck - 1 - i, 0))
    hc_spec = pl.BlockSpec((1, SEGS, N_CH), lambda i: (nck - 1 - i, 0, 0))
    fixed2 = lambda shape: pl.BlockSpec(shape, lambda i: (0,) * len(shape))
    return pl.pallas_call(
        body, name="ssm_bwd", grid=(nck,),
        in_specs=_ssm_specs(layer, nck, True) + [rev4, hc_spec, hc_spec],
        out_specs=[rev4, fixed2((2, 256, STATE)), fixed2((2, 256, STATE)), fixed2((2, HALF_CH, GROUP_W)),
                   fixed2((2, HALF_CH, GROUP_W)), fixed2((1, N_CH)), fixed2((1, N_CH)), fixed2((SSM_W, SSM_W)),
                   fixed2((8, SSM_W))],
        out_shape=[jax.ShapeDtypeStruct((4, seq, LANE), F32),
                   jax.ShapeDtypeStruct((2, 256, STATE), F32), jax.ShapeDtypeStruct((2, 256, STATE), F32),
                   jax.ShapeDtypeStruct((2, HALF_CH, GROUP_W), F32), jax.ShapeDtypeStruct((2, HALF_CH, GROUP_W), F32),
                   jax.ShapeDtypeStruct((1, N_CH), F32), jax.ShapeDtypeStruct((1, N_CH), F32),
                   jax.ShapeDtypeStruct((SSM_W, SSM_W), F32), jax.ShapeDtypeStruct((8, SSM_W), F32)],
        scratch_shapes=[pltpu.VMEM((CHUNK, N_CH), F32), pltpu.VMEM((CHUNK, N_CH), F32),
                        pltpu.VMEM((CHUNK, N_CH), F32), pltpu.VMEM((CHUNK, N_CH), F32),
                        pltpu.VMEM((STEPS, N_CH), F32), pltpu.VMEM((STEPS, N_CH), F32),
                        pltpu.VMEM((1, N_CH), F32), pltpu.VMEM((1, N_CH), F32),
                        pltpu.VMEM((SEGS, N_CH), F32), pltpu.VMEM((SEGS, N_CH), F32),
                        pltpu.VMEM((CHUNK, SSM_W), F32), pltpu.VMEM((CHUNK, SSM_W), F32),
                        pltpu.VMEM((2, 256, HALF_CH), F32), pltpu.VMEM((2, 256, HALF_CH), F32),
                        pltpu.VMEM((2, HALF_CH, 256), F32), pltpu.VMEM((2, HALF_CH, 256), F32)],
        compiler_params=_params(("arbitrary",)),
    )(u4, a_r, a_i, bb_r, bb_i, c_rt, c_it, vec512, w_glu, ds4, hc_r, hc_i)


def _block_diag(t):
    nl, _, ng, a, b = t.shape
    eye = jnp.eye(ng, dtype=t.dtype)
    return jnp.einsum("gh,lkgab->lkgahb", eye, t).reshape(nl, 2, ng * a, ng * b)


def _local_step(x, loss_target, mod, p, comm):
    nl = mod.shape[0]
    pad1024 = jnp.zeros((nl, 16 - 10, D_MODEL), F32)
    vec = jnp.concatenate([mod.reshape(nl, N_MOD, D_MODEL), p["pre_mix_g"][:, None], p["post_mix_g"][:, None],
                           p["pre_mlp_g"][:, None], p["post_mlp_g"][:, None], pad1024], axis=1)
    vec512 = jnp.concatenate([p["attn_out_g"][:, None], p["ssm_out_g"][:, None], p["d_skip"][:, None],
                              p["b_glu"][:, None], jnp.zeros((nl, 4, SSM_W), F32)], axis=1)
    n_all = nl * N_CH
    lr = p["lam_re"].reshape(n_all, 1)
    li = p["lam_im"].reshape(n_all, 1)
    ldt = jnp.broadcast_to(p["log_dt"][:, :, None], (nl, N_GROUPS, STATE)).reshape(n_all, 1)
    br = p["b_re"].reshape(n_all, GROUP_W)
    bi = p["b_im"].reshape(n_all, GROUP_W)
    ab_r, ab_i, bb_r, bb_i = _ssm_prepare(lr, li, ldt, br, bi)
    a_r = ab_r.reshape(nl, 1, N_CH)
    a_i = ab_i.reshape(nl, 1, N_CH)

    def dense_b(bb):
        return _block_diag(bb.reshape(nl, 2, 16, STATE, GROUP_W).transpose(0, 1, 2, 4, 3)).astype(BF16)

    def dense_c(cc):
        return _block_diag(cc.reshape(nl, 2, 16, GROUP_W, STATE).transpose(0, 1, 2, 4, 3)).astype(BF16)

    bbr_d, bbi_d = dense_b(bb_r), dense_b(bb_i)
    crt_d, cit_d = dense_c(p["c_re"]), dense_c(p["c_im"])

    saved = []
    xl = x
    mixer_w, mlp_w = [None] * nl, [None] * nl
    for l in range(nl):
        mixer_w[l], tok = comm.mixer_weights(l, [xl, bbr_d, bbi_d, crt_d, cit_d] if l == 0 else xl)
        w_in_t, w_glu, w_out = mixer_w[l]
        q, kv, u4, h1 = _in_proj_fwd(xl, _after(vec, *tok), w_in_t, l)
        attn = _attn_fwd(q, kv, p["attn_sinks"][l])
        s4, hc_r, hc_i = _ssm_fwd(u4, a_r, a_i, bbr_d, bbi_d, crt_d, cit_d, vec512, w_glu, l)
        x1 = _out_proj_fwd(xl, attn, s4, vec, vec512, w_out, l)
        mlp_w[l] = comm.mlp_weights(l, x1)
        x2, r, f = _mlp_fwd(x1, vec, mlp_w[l][0], mlp_w[l][1], l)
        saved.append((xl, q, kv, u4, h1, attn, s4, hc_r, hc_i, x1, r, f))
        xl = x2

    dx, loss_parts = _loss_head(xl, loss_target)
    loss = jnp.sum(loss_parts[:, 0, 0])

    dvec_l, dvec512_l, dsink_l = [None] * nl, [None] * nl, [None] * nl
    dab_r, dab_i, dbb_r, dbb_i, dc_re, dc_im = ([None] * nl for _ in range(6))
    toks = []
    for l in range(nl - 1, -1, -1):
        xl, q, kv, u4, h1, attn, s4, hc_r, hc_i, x1, r, f = saved[l]
        w_in_t, w_glu, w_out = mixer_w[l]
        dx1, h2, da, df, dvec_m = _mlp_bwd(dx, x1, r, f, _after(vec, *toks), mlp_w[l][0], mlp_w[l][1], l)
        toks = comm.after_mlp_bwd(l, dx1)
        dw_mlp_out = _matmul_tn(r, df, BF16, "dw_mlp_out").reshape(4, D_FF // 4, D_MODEL)
        dw_mlp_in = _matmul_tn(h2, da, BF16, "dw_mlp_in", pieces=4)
        toks = toks + comm.mlp_grads(l, [dw_mlp_in, dw_mlp_out])
        dattn, ds4, heads, dmixed, dvec_o, dvec512_o = _out_proj_bwd(
            dx1, attn, s4, _after(vec, *toks), vec512, w_out, l)
        dw_out = _matmul_tn(heads, dmixed, BF16, "dw_out").reshape(4, D_MODEL // 4, D_MODEL)
        dq, dkv, dsk = _attn_bwd(q, kv, p["attn_sinks"][l], dattn)
        (du4, dbbr, dbbi, dcrt, dcit, dar, dai, dwg, dvec512_s) = _ssm_bwd(
            u4, ds4, hc_r, hc_i, a_r, a_i, bbr_d, bbi_d, crt_d, cit_d, vec512, w_glu, l)
        dw_glu = dwg.astype(BF16).reshape(4, SSM_W // 4, SSM_W)
        dx, dproj, dvec_i = _in_proj_bwd(dx1, dq, dkv, du4, xl, vec, w_in_t, l)
        toks = comm.after_in_proj_bwd(l, dx)
        dw_in = _matmul_tn(h1, dproj, BF16, "dw_in")
        toks = toks + comm.mixer_grads(l, [dw_in.reshape(D_MODEL, 4, IN_W // 4).transpose(1, 0, 2), dw_glu, dw_out])
        dvec_l[l] = dvec_m + dvec_o + dvec_i
        dvec512_l[l] = dvec512_o + dvec512_s
        dsink_l[l] = jnp.sum(dsk[:, :, 0], axis=0)
        dab_r[l], dab_i[l], dbb_r[l], dbb_i[l], dc_re[l], dc_im[l] = dar, dai, dbbr, dbbi, dcrt, dcit

    dvec = _after(jnp.stack(dvec_l), *toks)
    dvec512 = jnp.stack(dvec512_l)
    ng = N_GROUPS // 2

    def b_cols(d):
        return jnp.stack(d).reshape(nl, 2, ng, GROUP_W, STATE).transpose(0, 1, 2, 4, 3).reshape(n_all, GROUP_W)

    def c_param(d):
        return jnp.stack(d).reshape(nl, 2, ng, STATE, GROUP_W).transpose(0, 1, 2, 4, 3).reshape(c_shape)

    dbb_r_c, dbb_i_c = b_cols(dbb_r), b_cols(dbb_i)
    c_shape = (nl, N_GROUPS, GROUP_W, STATE)
    dlr, dli, dldt, dbr, dbi = _ssm_prepare_bwd(
        lr, li, ldt, br, bi, jnp.stack(dab_r).reshape(n_all, 1), jnp.stack(dab_i).reshape(n_all, 1), dbb_r_c, dbb_i_c)
    small = {
        "b_ada": dvec[:, :N_MOD].reshape(nl, N_MOD * D_MODEL),
        "pre_mix_g": dvec[:, V_PRE_MIX], "post_mix_g": dvec[:, V_POST_MIX],
        "pre_mlp_g": dvec[:, V_PRE_MLP], "post_mlp_g": dvec[:, V_POST_MLP],
        "attn_sinks": jnp.stack(dsink_l),
        "lam_re": dlr.reshape(nl, N_GROUPS, STATE), "lam_im": dli.reshape(nl, N_GROUPS, STATE),
        "log_dt": jnp.sum(dldt.reshape(nl, N_GROUPS, STATE), axis=-1),
        "b_re": dbr.reshape(nl, N_GROUPS, STATE, GROUP_W), "b_im": dbi.reshape(nl, N_GROUPS, STATE, GROUP_W),
        "c_re": c_param(dc_re), "c_im": c_param(dc_im),
        "d_skip": dvec512[:, H_DSKIP], "b_glu": dvec512[:, H_BGLU],
        "attn_out_g": dvec512[:, H_ATTN_G], "ssm_out_g": dvec512[:, H_SSM_G],
    }
    return loss, dx, small, small["b_ada"]


WEIGHTS = ["w_ada", "b_ada", "pre_mix_g", "w_in", "attn_sinks", "lam_re", "lam_im", "log_dt", "b_re", "b_im", "c_re",
           "c_im", "d_skip", "w_glu", "b_glu", "attn_out_g", "ssm_out_g", "w_out", "post_mix_g", "pre_mlp_g",
           "w_mlp_in", "w_mlp_out", "post_mlp_g"]
BIG = ["w_in", "w_glu", "w_out", "w_mlp_in", "w_mlp_out"]
SMALL = [n for n in WEIGHTS if n not in BIG and n != "w_ada"]
PACK_ROWS = 256


def _pack(parts):
    rows = []
    for n in SMALL:
        flat = parts[n].reshape(-1)
        pad = (-flat.shape[0]) % (PACK_ROWS * LANE)
        rows.append(jnp.pad(flat, (0, pad)).reshape(-1, LANE))
    return jnp.concatenate(rows, axis=0)


def _unpack(packed, shapes):
    out, r0 = {}, 0
    for n in SMALL:
        size = int(np.prod(shapes[n]))
        rows = -(-size // (PACK_ROWS * LANE)) * PACK_ROWS
        out[n] = packed[r0:r0 + rows].reshape(-1)[:size].reshape(shapes[n])
        r0 += rows
    return out


MIXER = ["w_in", "w_glu", "w_out"]
MLP = ["w_mlp_in", "w_mlp_out"]


class _Exchanges:
    def __init__(self, shards, wts, mom, var, chip):
        self.shards, self.wts, self.mom, self.var, self.chip = shards, wts, mom, var, chip
        self.chip_arr = jnp.reshape(chip, (1,)).astype(jnp.int32)
        self.nl = len(shards["w_in"])
        self.gathers, self.scatters, self.pairs = {}, {}, {}
        self.res = {n: None for n in BIG}

    def _start_gather(self, group, tag, l, after=()):
        srcs = [self.shards[n][l] for n in group]
        lands = [lax.dynamic_update_slice(lax.empty((4,) + s.shape, s.dtype), s[None], (self.chip, 0, 0)) for s in srcs]
        plan = _plan_gather(len(srcs))
        st = _exchange_start(f"gather_{tag}{l}_start", 3 * len(srcs), plan, srcs + lands, after)
        self.gathers[tag, l] = (plan, st)
        return st[3]

    def _wait_gather(self, tag, l, after):
        plan, st = self.gathers.pop((tag, l))
        n = len(st[2]) // 2
        bufs = _exchange_wait(f"gather_{tag}{l}_wait", 3 * n, plan, st, after)
        return [b.reshape(4 * b.shape[1], b.shape[2]) for b in bufs[n:]]

    def _start_layer(self, l, after):
        tok = self._start_gather(MIXER, "mixer", l, after)
        return [tok, self._start_gather(MLP, "mlp", l, [tok])]

    def begin(self, after):
        toks = self._start_layer(0, after)
        return toks + (self._start_layer(1, toks[1:]) if self.nl > 1 else [])

    def mixer_weights(self, l, after):
        w = self._wait_gather("mixer", l, after)
        toks = self._start_layer(l + 2, w[:1]) if l + 2 < self.nl else []
        return w, toks

    def mlp_weights(self, l, after):
        return self._wait_gather("mlp", l, after)

    def _start_scatter(self, tag, l, group, pieces):
        plan = _plan_scatter(len(pieces))
        st = _exchange_start(f"scatter_{tag}{l}_start", 3 * len(pieces), plan,
                             list(pieces) + [lax.empty(p.shape, p.dtype) for p in pieces])
        self.scatters[tag] = (l, group, plan, st)
        return [st[3]]

    def _finish_scatter(self, tag, after):
        l, group, plan, st = self.scatters.pop(tag)
        n = len(group)
        bufs = _exchange_wait(f"scatter_{tag}{l}_wait", 3 * n, plan, st, after)
        sums = [_sum_pieces(bufs[k], bufs[n + k], self.chip_arr, "sum_" + group[k]) for k in range(n)]
        plan2 = _plan_pair(n)
        st2 = _exchange_start(f"pair_{tag}{l}_start", n, plan2, sums + [lax.empty(s.shape, s.dtype) for s in sums])
        self.pairs[tag] = (l, group, plan2, st2)
        return [st2[3]]

    def _finish_pair(self, tag, after):
        l, group, plan, st = self.pairs.pop(tag)
        n = len(group)
        bufs = _exchange_wait(f"pair_{tag}{l}_wait", n, plan, st, after)
        for k, name in enumerate(group):
            self.res[name] = _adamw_layer([bufs[k], bufs[n + k]], self.wts[name], self.mom[name], self.var[name],
                                          l, self.res[name], "adamw_" + name)

    def after_mlp_bwd(self, l, after):
        toks = self._finish_scatter("mixer", after) if "mixer" in self.scatters else []
        if "mlp" in self.pairs:
            self._finish_pair("mlp", after)
        return toks

    def mlp_grads(self, l, pieces):
        return self._start_scatter("mlp", l, MLP, pieces)

    def after_in_proj_bwd(self, l, after):
        toks = self._finish_scatter("mlp", after)
        if "mixer" in self.pairs:
            self._finish_pair("mixer", after)
        return toks

    def mixer_grads(self, l, pieces):
        return self._start_scatter("mixer", l, MIXER, pieces)

    def finish_mixer_scatter(self, after):
        return self._finish_scatter("mixer", after)

    def finish_mlp(self, after):
        self._finish_pair("mlp", after)

    def finish_mixer(self, after):
        self._finish_pair("mixer", after)

    def results(self):
        return self.res


def kernel(x, c, w_ada, b_ada, pre_mix_g, w_in, attn_sinks, lam_re, lam_im, log_dt, b_re, b_im, c_re, c_im, d_skip, w_glu, b_glu, attn_out_g, ssm_out_g, w_out, post_mix_g, pre_mlp_g, w_mlp_in, w_mlp_out, post_mlp_g, loss_target, m_w_ada, m_b_ada, m_pre_mix_g, m_w_in, m_attn_sinks, m_lam_re, m_lam_im, m_log_dt, m_b_re, m_b_im, m_c_re, m_c_im, m_d_skip, m_w_glu, m_b_glu, m_attn_out_g, m_ssm_out_g, m_w_out, m_post_mix_g, m_pre_mlp_g, m_w_mlp_in, m_w_mlp_out, m_post_mlp_g, v_w_ada, v_b_ada, v_pre_mix_g, v_w_in, v_attn_sinks, v_lam_re, v_lam_im, v_log_dt, v_b_re, v_b_im, v_c_re, v_c_im, v_d_skip, v_w_glu, v_b_glu, v_attn_out_g, v_ssm_out_g, v_w_out, v_post_mix_g, v_pre_mlp_g, v_w_mlp_in, v_w_mlp_out, v_post_mlp_g):
    args = locals()
    wts = {n: args[n] for n in WEIGHTS}
    mom = {n: args["m_" + n] for n in WEIGHTS}
    var = {n: args["v_" + n] for n in WEIGHTS}
    nl = w_in.shape[0]
    ix, iy, ic = lax.axis_index("x"), lax.axis_index("y"), lax.axis_index("c")
    chip = 2 * ix + iy
    me = 4 * ix + 2 * iy + ic
    wcols = w_ada.shape[2]

    shards = {"w_in": [w_in[l].astype(BF16).T for l in range(nl)], "w_glu": [w_glu[l].astype(BF16) for l in range(nl)],
              "w_out": [w_out[l].astype(BF16) for l in range(nl)],
              "w_mlp_in": [w_mlp_in[l].astype(BF16).T for l in range(nl)],
              "w_mlp_out": [w_mlp_out[l].astype(BF16) for l in range(nl)]}
    comm = _Exchanges(shards, wts, mom, var, chip)

    c_all = _gather([c.reshape(1, 1, 1, D_MODEL)], "all", "gather_c")[0].reshape(8, D_MODEL)
    b_sh = lax.dynamic_slice(b_ada, (0, chip * wcols), (nl, wcols)).reshape(nl, 1, wcols)
    mod_sh = _ada_forward(c_all, w_ada, b_sh)
    mod_all = _gather([mod_sh.reshape(1, 1, nl * 8, wcols)], "chips", "gather_mod")[0]
    toks = comm.begin([mod_all])
    mod = lax.dynamic_index_in_dim(mod_all.reshape(4, nl, 8, wcols), me, axis=2, keepdims=False)
    mod = mod.transpose(1, 0, 2).reshape(nl, 4 * wcols)

    small_p = {n: wts[n] for n in SMALL}
    small_p["log_dt"] = _after(log_dt, *toks)
    loss, grad_x, small, dmod = _local_step(x[0], loss_target[0], mod, small_p, comm)
    loss = lax.psum(loss, ("x", "y", "c"))

    packed = _pack(small)
    rows = packed.shape[0]
    pair_plan = _plan_pair(1)
    pair_small = _exchange_start("pair_small_start", 1, pair_plan, [packed, lax.empty((rows, LANE), F32)])

    dmod = _after(dmod, pair_small[3])
    dmod_all = _gather([dmod.reshape(1, 1, nl, N_MOD * D_MODEL)], "all", "gather_dmod")[0][0]
    dmod_sh = lax.dynamic_slice(dmod_all, (0, 0, chip * wcols), (8, nl, wcols)).transpose(1, 0, 2)
    g_ada = _ada_weight_grad(c_all.T, dmod_sh)
    res = {"w_ada": _adamw(g_ada[:, None], w_ada, m_w_ada, v_w_ada, "adamw_w_ada")}

    comm.finish_mlp(res["w_ada"][0])
    toks = comm.finish_mixer_scatter(res["w_ada"][0])

    own, other = _exchange_wait("pair_small_wait", 1, pair_plan, pair_small, res["w_ada"][0])
    chip_sum = _after(_sum_list([own, other], "sum_pair_small"), *toks)
    quad_plan = _plan_gather(1)
    quad0 = lax.dynamic_update_slice(lax.empty((4, rows, LANE), F32), chip_sum[None], (chip, 0, 0))
    quad_small = _exchange_start("gather_small_start", 3, quad_plan, [chip_sum, quad0])
    comm.finish_mixer([comm.results()[n][0] for n in MLP] + [_after(dmod, quad_small[3])])
    res.update(comm.results())
    quad = _exchange_wait("gather_small_wait", 3, quad_plan, quad_small, [res[n][0] for n in BIG])[1]
    outs = _adamw(quad[None], _pack({n: wts[n] for n in SMALL})[None], _pack({n: mom[n] for n in SMALL})[None],
                  _pack({n: var[n] for n in SMALL})[None], "adamw_small")
    shapes = {n: wts[n].shape for n in SMALL}
    unpacked = [_unpack(o[0], shapes) for o in outs]
    for n in SMALL:
        res[n] = [u[n] for u in unpacked]

    return (loss, grad_x[None], *[res[n][0] for n in WEIGHTS], *[res[n][1] for n in WEIGHTS],
            *[res[n][2] for n in WEIGHTS], *[res[n][3] for n in WEIGHTS])
```

```python
import functools
import math

import numpy as np
import jax
import jax.numpy as jnp
from jax import lax
from jax.experimental import pallas as pl
from jax.experimental.pallas import tpu as pltpu

F32 = jnp.float32
BF16 = jnp.bfloat16

D_MODEL = 1024
ATTN_W = 512
SSM_W = 512
HEAD_DIM = 64
N_Q = 8
N_KV = 2
Q_PER_KV = 4
KV_W = 128
WINDOW = 128
BLOCK = 128
N_GROUPS = 32
GROUP_W = 16
STATE = 64
N_CH = N_GROUPS * STATE
HALF_CH = N_CH // 2
D_FF = 4096
IN_W = 1280
N_MOD = 6
EPS = 1e-6
NEG_INF = -1e30

ADAM_LR = 0.001
ADAM_B1 = 0.9
ADAM_B2 = 0.999
ADAM_EPS = 1e-08
ADAM_WD = 0.01
ADAM_STEP = 10

ROW_TILE = 256
CHUNK = 256
SEGS = 8
STEPS = CHUNK // SEGS
STRIP = 1024
VMEM_LIMIT_V7X = 56 * 1024 * 1024
LANE = 128
SUBLANE = 8

GELU_K0 = math.sqrt(2.0 / math.pi)
GELU_K1 = 0.044715

V_SH1, V_SC1, V_G1, V_SH2, V_SC2, V_G2, V_PRE_MIX, V_POST_MIX, V_PRE_MLP, V_POST_MLP = range(10)
H_ATTN_G, H_SSM_G, H_DSKIP, H_BGLU = range(4)

HBM = pl.BlockSpec(memory_space=pltpu.HBM)
SEM = pl.BlockSpec(memory_space=pltpu.SEMAPHORE)
EFFECT = pltpu.SideEffectType.DATAFLOW_SIDE_EFFECTING
MESH_ID = pl.DeviceIdType.MESH


def _nn(a, b):
    return lax.dot_general(a, b, (((1,), (0,)), ((), ())), preferred_element_type=F32)


def _nt(a, b):
    return lax.dot_general(a, b, (((1,), (1,)), ((), ())), preferred_element_type=F32)


def _tn(a, b):
    return lax.dot_general(a, b, (((0,), (0,)), ((), ())), preferred_element_type=F32)


def _params(sem):
    return pltpu.CompilerParams(dimension_semantics=sem, vmem_limit_bytes=VMEM_LIMIT_V7X)


def _rms_fwd(x, g):
    r = lax.rsqrt(jnp.mean(x * x, axis=-1, keepdims=True) + EPS)
    xh = x * r
    return xh * g, xh, r


def _rms_bwd(dy, xh, r, g):
    dxh = dy * g
    dx = r * (dxh - xh * jnp.mean(dxh * xh, axis=-1, keepdims=True))
    return dx, dy * xh


def _colsum(t):
    return jnp.sum(t, axis=0, keepdims=True)


def _gelu(y):
    t = jnp.tanh(GELU_K0 * (y + GELU_K1 * (y * y * y)))
    return 0.5 * y * (1.0 + t), t


def _gelu_grad(y, t):
    return 0.5 * (1.0 + t) + 0.5 * y * (1.0 - t * t) * GELU_K0 * (1.0 + 3.0 * GELU_K1 * y * y)


def _alibi_slopes():
    return [float(s) for s in 2.0 ** (-8.0 * np.arange(1, N_Q + 1) / N_Q)]


def _pick_rows(rows, bytes_per_row, budget):
    t = rows
    while t % (2 * SUBLANE) == 0 and t * bytes_per_row > budget:
        t //= 2
    return t


def _load_once(step, pairs, sems):
    @pl.when(step == 0)
    def _():
        cps = [pltpu.make_async_copy(src, dst, sems.at[k]) for k, (src, dst) in enumerate(pairs)]
        for cp in cps:
            cp.start()
        for cp in cps:
            cp.wait()


_GROUPS = {
    "all": ([(0, 0, 1), (0, 1, 0), (0, 1, 1), (1, 0, 0), (1, 0, 1), (1, 1, 0), (1, 1, 1)], (4, 2, 1), 8),
    "chips": ([(1, 0, 0), (0, 1, 0), (1, 1, 0)], (2, 1, 0), 4),
    "pair": ([(0, 0, 1)], (0, 0, 1), 2),
}


def _flip(v, f):
    return 1 - v if f else v


def _gather(arrs, kind, name):
    masks, wts, n = _GROUPS[kind]
    na, nm = len(arrs), len(masks)

    def body(*refs):
        ins, outs = refs[:na], refs[na:2 * na]
        ssem, rsem, lsem = refs[2 * na:]
        x, y, c = lax.axis_index("x"), lax.axis_index("y"), lax.axis_index("c")
        me = wts[0] * x + wts[1] * y + wts[2] * c
        local = [pltpu.make_async_copy(ins[k], outs[k].at[:, pl.ds(me, 1)], lsem.at[k]) for k in range(na)]
        for cp in local:
            cp.start()
        remote = []
        for k in range(na):
            for mi, (fx, fy, fc) in enumerate(masks):
                peer = (_flip(x, fx), _flip(y, fy), _flip(c, fc))
                remote.append(pltpu.make_async_remote_copy(
                    src_ref=ins[k], dst_ref=outs[k].at[:, pl.ds(me, 1)],
                    send_sem=ssem.at[k * nm + mi], recv_sem=rsem.at[k * nm + mi],
                    device_id=peer, device_id_type=MESH_ID))
        for cp in remote:
            cp.start()
        for cp in remote:
            cp.wait()
        for cp in local:
            cp.wait()

    outs = pl.pallas_call(
        body, name=name,
        out_shape=[jax.ShapeDtypeStruct((a.shape[0], n) + a.shape[2:], a.dtype) for a in arrs],
        in_specs=[HBM] * na, out_specs=[HBM] * na,
        scratch_shapes=[pltpu.SemaphoreType.DMA((na * nm,)), pltpu.SemaphoreType.DMA((na * nm,)),
                        pltpu.SemaphoreType.DMA((na,))],
    )(*arrs)
    return list(outs)


def _hbm(a):
    return pltpu.with_memory_space_constraint(a, pltpu.HBM)


def _after(x, *tokens):
    for t in tokens:
        x = x + t[0, 0].astype(x.dtype)
    return x


def _exchange_start(name, n_copies, plan, bufs, after=()):
    n, na = len(bufs), len(after)

    def body(*refs):
        ssem, rsem, token = refs[n + na], refs[n + na + 1], refs[2 * n + na + 2]
        for k, (src, dst, dev) in enumerate(plan(refs[:n])):
            pltpu.make_async_remote_copy(src_ref=src, dst_ref=dst, send_sem=ssem.at[k], recv_sem=rsem.at[k],
                                         device_id=dev, device_id_type=MESH_ID).start()
        token[...] = jnp.zeros_like(token)

    outs = pl.pallas_call(
        body, name=name,
        out_shape=(pltpu.SemaphoreType.DMA((n_copies,)), pltpu.SemaphoreType.DMA((n_copies,)),
                   *[pltpu.HBM(b.shape, b.dtype) for b in bufs], jax.ShapeDtypeStruct((SUBLANE, LANE), F32)),
        in_specs=[HBM] * n + [pl.BlockSpec(memory_space=pl.ANY)] * na,
        out_specs=(SEM, SEM, *[HBM] * n, pl.BlockSpec(memory_space=pltpu.VMEM)),
        input_output_aliases={i: 2 + i for i in range(n)},
        compiler_params=pltpu.CompilerParams(has_side_effects=EFFECT),
    )(*[_hbm(b) for b in bufs], *after)
    return outs[0], outs[1], list(outs[2:2 + n]), outs[2 + n]


def _exchange_wait(name, n_copies, plan, started, after):
    ssem, rsem, bufs, _ = started
    n = len(bufs)
    after = list(after) if isinstance(after, (list, tuple)) else [after]

    def body(*refs):
        ssem_ref, rsem_ref = refs[n], refs[n + 1]
        for k, (src, dst, dev) in enumerate(plan(refs[:n])):
            cp = pltpu.make_async_remote_copy(src_ref=src, dst_ref=dst, send_sem=ssem_ref.at[k], recv_sem=rsem_ref.at[k],
                                              device_id=dev, device_id_type=MESH_ID)
            cp.wait_send()
            cp.wait_recv()

    outs = pl.pallas_call(
        body, name=name,
        out_shape=tuple(pltpu.HBM(b.shape, b.dtype) for b in bufs),
        in_specs=[HBM] * n + [SEM, SEM] + [pl.BlockSpec(memory_space=pl.ANY)] * len(after), out_specs=tuple([HBM] * n),
        input_output_aliases={i: i for i in range(n)},
        compiler_params=pltpu.CompilerParams(has_side_effects=EFFECT),
    )(*bufs, ssem, rsem, *after)
    return list(outs)


def _position():
    x, y, c = lax.axis_index("x"), lax.axis_index("y"), lax.axis_index("c")
    return x, y, c, [(1 - x, y), (x, 1 - y), (1 - x, 1 - y)]


def _plan_gather(na):
    def plan(refs):
        x, y, c, chips = _position()
        return [(refs[k], refs[na + k].at[2 * x + y], (px, py, c)) for k in range(na) for px, py in chips]
    return plan


def _plan_scatter(na):
    def plan(refs):
        x, y, c, chips = _position()
        return [(refs[k].at[2 * px + py], refs[na + k].at[2 * x + y], (px, py, c))
                for k in range(na) for px, py in chips]
    return plan


def _plan_pair(na):
    def plan(refs):
        x, y, c, _ = _position()
        return [(refs[k], refs[na + k], (x, y, 1 - c)) for k in range(na)]
    return plan


def _sum_list(arrs, name):
    n = len(arrs)
    r, c = arrs[0].shape
    tr = _pick_rows(r, c * 4 * (n + 1), 4 << 20)

    def body(*refs):
        acc = refs[0][...].astype(F32)
        for j in range(1, n):
            acc = acc + refs[j][...].astype(F32)
        refs[n][...] = acc

    blk = pl.BlockSpec((tr, c), lambda i: (i, 0))
    return pl.pallas_call(
        body, name=name, grid=(r // tr,), in_specs=[blk] * n, out_specs=blk,
        out_shape=jax.ShapeDtypeStruct((r, c), F32), compiler_params=_params(("parallel",)),
    )(*arrs)


def _sum_pieces(own, recv, chip, name):
    _, r, c = own.shape
    tr = _pick_rows(r, c * 2 * 6, 4 << 20)

    def body(chip_ref, own_ref, recv_ref, o_ref):
        acc = own_ref[0].astype(F32)
        for j in range(4):
            acc = acc + jnp.where(chip_ref[0] == j, 0.0, recv_ref[j].astype(F32))
        o_ref[...] = acc.astype(BF16)

    return pl.pallas_call(
        body, name=name,
        grid_spec=pltpu.PrefetchScalarGridSpec(
            num_scalar_prefetch=1, grid=(r // tr,),
            in_specs=[pl.BlockSpec((1, tr, c), lambda i, chip_ref: (chip_ref[0], i, 0)),
                      pl.BlockSpec((4, tr, c), lambda i, chip_ref: (0, i, 0))],
            out_specs=pl.BlockSpec((tr, c), lambda i, chip_ref: (i, 0))),
        out_shape=jax.ShapeDtypeStruct((r, c), BF16), compiler_params=_params(("parallel",)),
    )(chip, own, recv)


def _adam_update(g, w, m, v):
    mn = ADAM_B1 * m + (1.0 - ADAM_B1) * g
    vn = ADAM_B2 * v + (1.0 - ADAM_B2) * jnp.square(g)
    m_hat = mn / (1.0 - ADAM_B1 ** ADAM_STEP)
    v_hat = vn / (1.0 - ADAM_B2 ** ADAM_STEP)
    return -ADAM_LR * (m_hat / (jnp.sqrt(v_hat) + ADAM_EPS) + ADAM_WD * w), mn, vn


def _adamw_layer(grads, w, m, v, layer, prev, name):
    ng = len(grads)
    nl, r, c = w.shape
    tr = _pick_rows(r, c * 4 * (ng + 7), 6 << 20)
    if prev is None:
        prev = [lax.empty((nl, r, c), F32) for _ in range(4)]

    def body(*refs):
        g = refs[0][...].astype(F32)
        for j in range(1, ng):
            g = g + refs[j][...].astype(F32)
        w_ref, m_ref, v_ref = refs[ng:ng + 3]
        go_ref, d_ref, mo_ref, vo_ref = refs[ng + 7:ng + 11]
        d, mn, vn = _adam_update(g, w_ref[0], m_ref[0], v_ref[0])
        go_ref[0] = g
        d_ref[0] = d
        mo_ref[0] = mn
        vo_ref[0] = vn

    gblk = pl.BlockSpec((tr, c), lambda i: (i, 0))
    blk = pl.BlockSpec((1, tr, c), lambda i: (layer, i, 0))
    keep = pl.BlockSpec(memory_space=pl.ANY)
    sds = jax.ShapeDtypeStruct((nl, r, c), F32)
    return pl.pallas_call(
        body, name=name, grid=(r // tr,),
        in_specs=[gblk] * ng + [blk] * 3 + [keep] * 4,
        out_specs=[blk] * 4, out_shape=[sds] * 4,
        input_output_aliases={ng + 3 + i: i for i in range(4)},
        compiler_params=_params(("parallel",)),
    )(*grads, w, m, v, *prev)


def _adamw(gs, w, m, v, name):
    a, s, r, c = gs.shape
    tr = _pick_rows(r, c * 4 * (s + 7), 6 << 20)

    def body(g_ref, w_ref, m_ref, v_ref, go_ref, d_ref, mo_ref, vo_ref):
        g = g_ref[0, 0].astype(F32)
        for j in range(1, s):
            g = g + g_ref[0, j].astype(F32)
        d, mn, vn = _adam_update(g, w_ref[0], m_ref[0], v_ref[0])
        go_ref[0] = g
        d_ref[0] = d
        mo_ref[0] = mn
        vo_ref[0] = vn

    blk = pl.BlockSpec((1, tr, c), lambda i, j: (i, j, 0))
    sds = jax.ShapeDtypeStruct((a, r, c), F32)
    return pl.pallas_call(
        body, name=name, grid=(a, r // tr),
        in_specs=[pl.BlockSpec((1, s, tr, c), lambda i, j: (i, 0, j, 0)), blk, blk, blk],
        out_specs=[blk, blk, blk, blk], out_shape=[sds, sds, sds, sds],
        compiler_params=_params(("parallel", "parallel")),
    )(gs, w, m, v)


def _ada_forward(c_all, w_ada, b_sh):
    nl, d, w = w_ada.shape
    tw = 512

    def body(c_ref, w_ref, b_ref, o_ref):
        cv = c_ref[...]
        act = (cv * jax.nn.sigmoid(cv)).astype(BF16)
        o_ref[0] = _nn(act, w_ref[0].astype(BF16)) + b_ref[0]

    return pl.pallas_call(
        body, name="ada_forward", grid=(nl, w // tw),
        in_specs=[pl.BlockSpec((8, d), lambda l, j: (0, 0)),
                  pl.BlockSpec((1, d, tw), lambda l, j: (l, 0, j)),
                  pl.BlockSpec((1, 1, tw), lambda l, j: (l, 0, j))],
        out_specs=pl.BlockSpec((1, 8, tw), lambda l, j: (l, 0, j)),
        out_shape=jax.ShapeDtypeStruct((nl, 8, w), F32),
        compiler_params=_params(("parallel", "parallel")),
    )(c_all, w_ada, b_sh)


def _ada_weight_grad(c_all_t, dmod):
    nl, nb, w = dmod.shape
    d = c_all_t.shape[0]
    tw = 512

    def body(c_ref, g_ref, o_ref):
        cv = c_ref[...]
        act = cv * jax.nn.sigmoid(cv)
        gv = g_ref[0]
        acc = act[:, 0:1] * gv[0:1, :]
        for b in range(1, nb):
            acc = acc + act[:, b:b + 1] * gv[b:b + 1, :]
        o_ref[0] = acc

    return pl.pallas_call(
        body, name="ada_weight_grad", grid=(nl, w // tw),
        in_specs=[pl.BlockSpec((d, nb), lambda l, j: (0, 0)),
                  pl.BlockSpec((1, nb, tw), lambda l, j: (l, 0, j))],
        out_specs=pl.BlockSpec((1, d, tw), lambda l, j: (l, 0, j)),
        out_shape=jax.ShapeDtypeStruct((nl, d, w), F32),
        compiler_params=_params(("parallel", "parallel")),
    )(c_all_t, dmod)


def _in_proj_fwd(x, vec, w_in_t, layer):
    seq = x.shape[0]
    tm = ROW_TILE

    def body(x_ref, vec_ref, w_ref, q_ref, kv_ref, u4_ref, h_ref):
        n, _, _ = _rms_fwd(x_ref[...], vec_ref[0, V_PRE_MIX:V_PRE_MIX + 1, :])
        h = (n * (1.0 + vec_ref[0, V_SC1:V_SC1 + 1, :]) + vec_ref[0, V_SH1:V_SH1 + 1, :]).astype(BF16)
        h_ref[...] = h
        proj = _nt(h, w_ref[...])
        q_ref[...] = proj[:, :ATTN_W].astype(BF16)
        kv_ref[...] = proj[:, ATTN_W:ATTN_W + 2 * KV_W].astype(BF16)
        u0 = ATTN_W + 2 * KV_W
        for j in range(4):
            u4_ref[j] = proj[:, u0 + j * LANE:u0 + (j + 1) * LANE]

    return pl.pallas_call(
        body, name="in_proj_fwd", grid=(seq // tm,),
        in_specs=[pl.BlockSpec((tm, D_MODEL), lambda i: (i, 0)),
                  pl.BlockSpec((1, 16, D_MODEL), lambda i: (layer, 0, 0)),
                  pl.BlockSpec((IN_W, D_MODEL), lambda i: (0, 0))],
        out_specs=[pl.BlockSpec((tm, ATTN_W), lambda i: (i, 0)),
                   pl.BlockSpec((tm, 2 * KV_W), lambda i: (i, 0)),
                   pl.BlockSpec((4, tm, LANE), lambda i: (0, i, 0)),
                   pl.BlockSpec((tm, D_MODEL), lambda i: (i, 0))],
        out_shape=[jax.ShapeDtypeStruct((seq, ATTN_W), BF16), jax.ShapeDtypeStruct((seq, 2 * KV_W), BF16),
                   jax.ShapeDtypeStruct((4, seq, LANE), F32), jax.ShapeDtypeStruct((seq, D_MODEL), BF16)],
        compiler_params=_params(("parallel",)),
    )(x, vec, w_in_t)


def _in_proj_bwd(dx1, dq, dkv, du4, x, vec, w_in_t, layer):
    seq = x.shape[0]
    tm = ROW_TILE

    def body(dx1_ref, dq_ref, dkv_ref, du4_ref, x_ref, vec_ref, w_ref, dx_ref, dp_ref, dvec_ref):
        i = pl.program_id(0)

        @pl.when(i == 0)
        def _():
            dvec_ref[...] = jnp.zeros_like(dvec_ref)

        dproj = jnp.concatenate([dq_ref[...], dkv_ref[...]] + [du4_ref[j] for j in range(4)], axis=1).astype(BF16)
        dp_ref[...] = dproj
        dh = _nn(dproj, w_ref[...])
        g = vec_ref[0, V_PRE_MIX:V_PRE_MIX + 1, :]
        n, xh, r = _rms_fwd(x_ref[...], g)
        dn = dh * (1.0 + vec_ref[0, V_SC1:V_SC1 + 1, :])
        dxn, dg_rows = _rms_bwd(dn, xh, r, g)
        dx_ref[...] = dx1_ref[...] + dxn
        dvec_ref[V_SH1:V_SH1 + 1, :] += _colsum(dh)
        dvec_ref[V_SC1:V_SC1 + 1, :] += _colsum(dh * n)
        dvec_ref[V_PRE_MIX:V_PRE_MIX + 1, :] += _colsum(dg_rows)

    row = pl.BlockSpec((tm, D_MODEL), lambda i: (i, 0))
    return pl.pallas_call(
        body, name="in_proj_bwd", grid=(seq // tm,),
        in_specs=[row, pl.BlockSpec((tm, ATTN_W), lambda i: (i, 0)), pl.BlockSpec((tm, 2 * KV_W), lambda i: (i, 0)),
                  pl.BlockSpec((4, tm, LANE), lambda i: (0, i, 0)), row,
                  pl.BlockSpec((1, 16, D_MODEL), lambda i: (layer, 0, 0)),
                  pl.BlockSpec((IN_W, D_MODEL), lambda i: (0, 0))],
        out_specs=[row, pl.BlockSpec((tm, IN_W), lambda i: (i, 0)), pl.BlockSpec((16, D_MODEL), lambda i: (0, 0))],
        out_shape=[jax.ShapeDtypeStruct((seq, D_MODEL), F32), jax.ShapeDtypeStruct((seq, IN_W), BF16),
                   jax.ShapeDtypeStruct((16, D_MODEL), F32)],
        compiler_params=_params(("arbitrary",)),
    )(dx1, dq, dkv, du4, x, vec, w_in_t)


def _heads(attn_ref, s4_ref, vec512_ref):
    ga = vec512_ref[0, H_ATTN_G:H_ATTN_G + 1, :]
    gs = vec512_ref[0, H_SSM_G:H_SSM_G + 1, :]
    sv = jnp.concatenate([s4_ref[j] for j in range(4)], axis=1)
    na, ah, ar = _rms_fwd(attn_ref[...], ga)
    ns, sh, sr = _rms_fwd(sv, gs)
    return jnp.concatenate([na, ns], axis=1), (ah, ar, ga), (sh, sr, gs)


def _out_proj_fwd(x, attn, s4, vec, vec512, w_out, layer):
    seq = x.shape[0]
    tm = ROW_TILE

    def body(x_ref, attn_ref, s4_ref, vec_ref, vec512_ref, w_ref, x1_ref):
        heads, _, _ = _heads(attn_ref, s4_ref, vec512_ref)
        mixed = _nn(heads.astype(BF16), w_ref[...])
        nm, _, _ = _rms_fwd(mixed, vec_ref[0, V_POST_MIX:V_POST_MIX + 1, :])
        x1_ref[...] = x_ref[...] + vec_ref[0, V_G1:V_G1 + 1, :] * nm

    row = pl.BlockSpec((tm, D_MODEL), lambda i: (i, 0))
    return pl.pallas_call(
        body, name="out_proj_fwd", grid=(seq // tm,),
        in_specs=[row, pl.BlockSpec((tm, ATTN_W), lambda i: (i, 0)), pl.BlockSpec((4, tm, LANE), lambda i: (0, i, 0)),
                  pl.BlockSpec((1, 16, D_MODEL), lambda i: (layer, 0, 0)),
                  pl.BlockSpec((1, 8, SSM_W), lambda i: (layer, 0, 0)),
                  pl.BlockSpec((D_MODEL, D_MODEL), lambda i: (0, 0))],
        out_specs=row, out_shape=jax.ShapeDtypeStruct((seq, D_MODEL), F32),
        compiler_params=_params(("parallel",)),
    )(x, attn, s4, vec, vec512, w_out)


def _out_proj_bwd(dx1, attn, s4, vec, vec512, w_out, layer):
    seq = dx1.shape[0]
    tm = ROW_TILE

    def body(dx1_ref, attn_ref, s4_ref, vec_ref, vec512_ref, w_ref,
             dattn_ref, ds4_ref, heads_ref, dmixed_ref, dvec_ref, dvec512_ref):
        i = pl.program_id(0)

        @pl.when(i == 0)
        def _():
            dvec_ref[...] = jnp.zeros_like(dvec_ref)
            dvec512_ref[...] = jnp.zeros_like(dvec512_ref)

        heads, (ah, ar, ga), (sh, sr, gs) = _heads(attn_ref, s4_ref, vec512_ref)
        hb = heads.astype(BF16)
        heads_ref[...] = hb
        gm = vec_ref[0, V_POST_MIX:V_POST_MIX + 1, :]
        nm, mh, mr = _rms_fwd(_nn(hb, w_ref[...]), gm)
        dx1v = dx1_ref[...]
        dvec_ref[V_G1:V_G1 + 1, :] += _colsum(dx1v * nm)
        dmixed, dgm_rows = _rms_bwd(dx1v * vec_ref[0, V_G1:V_G1 + 1, :], mh, mr, gm)
        dvec_ref[V_POST_MIX:V_POST_MIX + 1, :] += _colsum(dgm_rows)
        dmb = dmixed.astype(BF16)
        dmixed_ref[...] = dmb
        dheads = _nt(dmb, w_ref[...])
        dattn, dga_rows = _rms_bwd(dheads[:, :ATTN_W], ah, ar, ga)
        ds, dgs_rows = _rms_bwd(dheads[:, ATTN_W:], sh, sr, gs)
        dattn_ref[...] = dattn
        for j in range(4):
            ds4_ref[j] = ds[:, j * LANE:(j + 1) * LANE]
        dvec512_ref[H_ATTN_G:H_ATTN_G + 1, :] += _colsum(dga_rows)
        dvec512_ref[H_SSM_G:H_SSM_G + 1, :] += _colsum(dgs_rows)

    row = pl.BlockSpec((tm, D_MODEL), lambda i: (i, 0))
    return pl.pallas_call(
        body, name="out_proj_bwd", grid=(seq // tm,),
        in_specs=[row, pl.BlockSpec((tm, ATTN_W), lambda i: (i, 0)), pl.BlockSpec((4, tm, LANE), lambda i: (0, i, 0)),
                  pl.BlockSpec((1, 16, D_MODEL), lambda i: (layer, 0, 0)),
                  pl.BlockSpec((1, 8, SSM_W), lambda i: (layer, 0, 0)),
                  pl.BlockSpec((D_MODEL, D_MODEL), lambda i: (0, 0))],
        out_specs=[pl.BlockSpec((tm, ATTN_W), lambda i: (i, 0)), pl.BlockSpec((4, tm, LANE), lambda i: (0, i, 0)),
                   row, row, pl.BlockSpec((16, D_MODEL), lambda i: (0, 0)), pl.BlockSpec((8, SSM_W), lambda i: (0, 0))],
        out_shape=[jax.ShapeDtypeStruct((seq, ATTN_W), F32), jax.ShapeDtypeStruct((4, seq, LANE), F32),
                   jax.ShapeDtypeStruct((seq, D_MODEL), BF16), jax.ShapeDtypeStruct((seq, D_MODEL), BF16),
                   jax.ShapeDtypeStruct((16, D_MODEL), F32), jax.ShapeDtypeStruct((8, SSM_W), F32)],
        compiler_params=_params(("arbitrary",)),
    )(dx1, attn, s4, vec, vec512, w_out)


def _mlp_fwd(x1, vec, w_in_t, w_out, layer):
    seq = x1.shape[0]
    tm = ROW_TILE

    def body(x1_ref, vec_ref, wi_hbm, wo_hbm, x2_ref, r_ref, f_ref, wi, wo, sems):
        _load_once(pl.program_id(0), [(wi_hbm, wi), (wo_hbm, wo)], sems)
        x1v = x1_ref[...]
        n, _, _ = _rms_fwd(x1v, vec_ref[0, V_PRE_MLP:V_PRE_MLP + 1, :])
        h = (n * (1.0 + vec_ref[0, V_SC2:V_SC2 + 1, :]) + vec_ref[0, V_SH2:V_SH2 + 1, :]).astype(BF16)
        a = _nt(h, wi[...])
        r = jnp.square(jnp.maximum(a, 0.0)).astype(BF16)
        r_ref[...] = r
        f = _nn(r, wo[...])
        f_ref[...] = f
        nf, _, _ = _rms_fwd(f, vec_ref[0, V_POST_MLP:V_POST_MLP + 1, :])
        x2_ref[...] = x1v + vec_ref[0, V_G2:V_G2 + 1, :] * nf

    row = pl.BlockSpec((tm, D_MODEL), lambda i: (i, 0))
    return pl.pallas_call(
        body, name="mlp_fwd", grid=(seq // tm,),
        in_specs=[row, pl.BlockSpec((1, 16, D_MODEL), lambda i: (layer, 0, 0)), HBM, HBM],
        out_specs=[row, pl.BlockSpec((tm, D_FF), lambda i: (i, 0)), row],
        out_shape=[jax.ShapeDtypeStruct((seq, D_MODEL), F32), jax.ShapeDtypeStruct((seq, D_FF), BF16),
                   jax.ShapeDtypeStruct((seq, D_MODEL), F32)],
        scratch_shapes=[pltpu.VMEM((D_FF, D_MODEL), BF16), pltpu.VMEM((D_FF, D_MODEL), BF16),
                        pltpu.SemaphoreType.DMA((2,))],
        compiler_params=_params(("arbitrary",)),
    )(x1, vec, w_in_t, w_out)


def _mlp_bwd(dx2, x1, r, f, vec, w_in_t, w_out, layer):
    seq = x1.shape[0]
    tm = ROW_TILE

    def body(dx2_ref, x1_ref, r_ref, f_ref, vec_ref, wi_hbm, wo_hbm, dx1_ref, h_ref, da_ref, df_ref, dvec_ref,
             wi, wo, sems):
        i = pl.program_id(0)
        _load_once(i, [(wi_hbm, wi), (wo_hbm, wo)], sems)

        @pl.when(i == 0)
        def _():
            dvec_ref[...] = jnp.zeros_like(dvec_ref)

        g_pre = vec_ref[0, V_PRE_MLP:V_PRE_MLP + 1, :]
        g_post = vec_ref[0, V_POST_MLP:V_POST_MLP + 1, :]
        sc2 = vec_ref[0, V_SC2:V_SC2 + 1, :]
        n, xh, xr = _rms_fwd(x1_ref[...], g_pre)
        h_ref[...] = (n * (1.0 + sc2) + vec_ref[0, V_SH2:V_SH2 + 1, :]).astype(BF16)
        relu = jnp.sqrt(r_ref[...].astype(F32))
        nf, fh, fr = _rms_fwd(f_ref[...], g_post)
        dx2v = dx2_ref[...]
        dvec_ref[V_G2:V_G2 + 1, :] += _colsum(dx2v * nf)
        df, dgp_rows = _rms_bwd(dx2v * vec_ref[0, V_G2:V_G2 + 1, :], fh, fr, g_post)
        dvec_ref[V_POST_MLP:V_POST_MLP + 1, :] += _colsum(dgp_rows)
        dfb = df.astype(BF16)
        df_ref[...] = dfb
        da = (_nt(dfb, wo[...]) * (2.0 * relu)).astype(BF16)
        da_ref[...] = da
        dh = _nn(da, wi[...])
        dvec_ref[V_SH2:V_SH2 + 1, :] += _colsum(dh)
        dvec_ref[V_SC2:V_SC2 + 1, :] += _colsum(dh * n)
        dxn, dg_rows = _rms_bwd(dh * (1.0 + sc2), xh, xr, g_pre)
        dvec_ref[V_PRE_MLP:V_PRE_MLP + 1, :] += _colsum(dg_rows)
        dx1_ref[...] = dx2v + dxn

    row = pl.BlockSpec((tm, D_MODEL), lambda i: (i, 0))
    wide = pl.BlockSpec((tm, D_FF), lambda i: (i, 0))
    return pl.pallas_call(
        body, name="mlp_bwd", grid=(seq // tm,),
        in_specs=[row, row, wide, row, pl.BlockSpec((1, 16, D_MODEL), lambda i: (layer, 0, 0)), HBM, HBM],
        out_specs=[row, row, wide, row, pl.BlockSpec((16, D_MODEL), lambda i: (0, 0))],
        out_shape=[jax.ShapeDtypeStruct((seq, D_MODEL), F32), jax.ShapeDtypeStruct((seq, D_MODEL), BF16),
                   jax.ShapeDtypeStruct((seq, D_FF), BF16),
                   jax.ShapeDtypeStruct((seq, D_MODEL), BF16), jax.ShapeDtypeStruct((16, D_MODEL), F32)],
        scratch_shapes=[pltpu.VMEM((D_FF, D_MODEL), BF16), pltpu.VMEM((D_FF, D_MODEL), BF16),
                        pltpu.SemaphoreType.DMA((2,))],
        compiler_params=_params(("arbitrary",)),
    )(dx2, x1, r, f, vec, w_in_t, w_out)


def _loss_head(y, target):
    seq = y.shape[0]
    tm = ROW_TILE

    def body(y_ref, t_ref, dy_ref, part_ref):
        e = y_ref[...] - t_ref[...]
        dy_ref[...] = e * (1.0 / D_MODEL)
        tot = jnp.sum(jnp.sum(e * e, axis=1, keepdims=True), axis=0, keepdims=True) * (0.5 / D_MODEL)
        part_ref[0] = jnp.broadcast_to(tot, (SUBLANE, LANE))

    row = pl.BlockSpec((tm, D_MODEL), lambda i: (i, 0))
    return pl.pallas_call(
        body, name="loss_head", grid=(seq // tm,),
        in_specs=[row, row],
        out_specs=[row, pl.BlockSpec((1, SUBLANE, LANE), lambda i: (i, 0, 0))],
        out_shape=[jax.ShapeDtypeStruct((seq, D_MODEL), F32), jax.ShapeDtypeStruct((seq // tm, SUBLANE, LANE), F32)],
        compiler_params=_params(("parallel",)),
    )(y, target)


def _matmul_tn(a, b, out_dtype, name, pieces=1):
    kk, m = a.shape
    n = b.shape[1]
    tm = min(m, 512)
    tn = n // pieces if pieces > 1 else min(n, 1280)
    tk = min(kk, 2048)
    nk = kk // tk

    def body(a_ref, b_ref, o_ref, acc):
        k = pl.program_id(2)

        @pl.when(k == 0)
        def _():
            acc[...] = jnp.zeros_like(acc)

        acc[...] += _tn(a_ref[...], b_ref[...])

        @pl.when(k == nk - 1)
        def _():
            if pieces > 1:
                o_ref[0] = acc[...].astype(out_dtype)
            else:
                o_ref[...] = acc[...].astype(out_dtype)

    if pieces > 1:
        out_spec = pl.BlockSpec((1, tm, tn), lambda i, j, k: (j, i, 0))
        out_shape = jax.ShapeDtypeStruct((pieces, m, tn), out_dtype)
    else:
        out_spec = pl.BlockSpec((tm, tn), lambda i, j, k: (i, j))
        out_shape = jax.ShapeDtypeStruct((m, n), out_dtype)
    return pl.pallas_call(
        body, name=name, grid=(m // tm, n // tn, nk),
        in_specs=[pl.BlockSpec((tk, tm), lambda i, j, k: (k, i)), pl.BlockSpec((tk, tn), lambda i, j, k: (k, j))],
        out_specs=out_spec, out_shape=out_shape,
        scratch_shapes=[pltpu.VMEM((tm, tn), F32)],
        compiler_params=_params(("parallel", "parallel", "arbitrary")),
    )(a, b)


def _attn_probs(i, q_h, kband, slope, sink):
    rr = lax.broadcasted_iota(jnp.int32, (BLOCK, 2 * BLOCK), 0)
    jj = lax.broadcasted_iota(jnp.int32, (BLOCK, 2 * BLOCK), 1)
    diff = BLOCK + rr - jj
    valid = (diff >= 0) & (diff < WINDOW) & ((jj >= BLOCK) | (i > 0))
    s = _nt(q_h, kband) * (HEAD_DIM ** -0.5)
    s = jnp.where(valid, s - slope * diff.astype(F32), NEG_INF)
    m = jnp.maximum(jnp.max(s, axis=1, keepdims=True), sink)
    p = jnp.exp(s - m)
    ps = jnp.exp(sink - m)
    inv = 1.0 / (jnp.sum(p, axis=1, keepdims=True) + ps)
    return p * inv, ps * inv


def _bands(kvp, kvc, h):
    kband = jnp.concatenate([kvp[:, h * HEAD_DIM:(h + 1) * HEAD_DIM], kvc[:, h * HEAD_DIM:(h + 1) * HEAD_DIM]], axis=0)
    v0 = KV_W + h * HEAD_DIM
    vband = jnp.concatenate([kvp[:, v0:v0 + HEAD_DIM], kvc[:, v0:v0 + HEAD_DIM]], axis=0)
    return kband, vband


def _attn_fwd(q, kv, sinks):
    seq = q.shape[0]
    nb = seq // BLOCK
    slopes = _alibi_slopes()

    def body(sink_ref, q_ref, kvp_ref, kvc_ref, o_ref):
        i = pl.program_id(0)
        qv, kvp, kvc = q_ref[...], kvp_ref[...], kvc_ref[...]
        for h in range(N_KV):
            kband, vband = _bands(kvp, kvc, h)
            for g in range(Q_PER_KV):
                hq = h * Q_PER_KV + g
                cols = slice(hq * HEAD_DIM, (hq + 1) * HEAD_DIM)
                pr, _ = _attn_probs(i, qv[:, cols], kband, slopes[hq], sink_ref[hq])
                o_ref[:, cols] = _nn(pr.astype(BF16), vband)

    return pl.pallas_call(
        body, name="attn_fwd", grid=(nb,),
        in_specs=[pl.BlockSpec(memory_space=pltpu.SMEM),
                  pl.BlockSpec((BLOCK, ATTN_W), lambda i: (i, 0)),
                  pl.BlockSpec((BLOCK, 2 * KV_W), lambda i: (jnp.maximum(i - 1, 0), 0)),
                  pl.BlockSpec((BLOCK, 2 * KV_W), lambda i: (i, 0))],
        out_specs=pl.BlockSpec((BLOCK, ATTN_W), lambda i: (i, 0)),
        out_shape=jax.ShapeDtypeStruct((seq, ATTN_W), F32),
        compiler_params=_params(("parallel",)),
    )(sinks, q, kv, kv)


def _attn_bwd(q, kv, sinks, dout):
    seq = q.shape[0]
    nb = seq // BLOCK
    slopes = _alibi_slopes()
    scale = HEAD_DIM ** -0.5

    def body(sink_ref, q_ref, kvp_ref, kvc_ref, do_ref, dq_ref, dkv_ref, dsk_ref, prev):
        step = pl.program_id(0)
        i = nb - 1 - step

        @pl.when(step == 0)
        def _():
            prev[...] = jnp.zeros_like(prev)

        qv, kvp, kvc = q_ref[...], kvp_ref[...], kvc_ref[...]
        dov = do_ref[...].astype(BF16)
        dk, dv, dsk = [], [], []
        for h in range(N_KV):
            kband, vband = _bands(kvp, kvc, h)
            dk_h = jnp.zeros((2 * BLOCK, HEAD_DIM), F32)
            dv_h = jnp.zeros((2 * BLOCK, HEAD_DIM), F32)
            for g in range(Q_PER_KV):
                hq = h * Q_PER_KV + g
                cols = slice(hq * HEAD_DIM, (hq + 1) * HEAD_DIM)
                q_h, do_h = qv[:, cols], dov[:, cols]
                pr, ps = _attn_probs(i, q_h, kband, slopes[hq], sink_ref[hq])
                dp = _nt(do_h, vband)
                delta = jnp.sum(pr * dp, axis=1, keepdims=True)
                ds = (pr * (dp - delta) * scale).astype(BF16)
                dsk.append(jnp.broadcast_to(-_colsum(ps * delta), (1, LANE)))
                dq_ref[:, cols] = _nn(ds, kband)
                dk_h = dk_h + _tn(ds, q_h)
                dv_h = dv_h + _tn(pr.astype(BF16), do_h)
            dk.append(dk_h)
            dv.append(dv_h)
        band = jnp.concatenate(dk + dv, axis=1)
        dkv_ref[...] = band[BLOCK:, :] + prev[...]
        prev[...] = band[:BLOCK, :]
        dsk_ref[0] = jnp.concatenate(dsk, axis=0)

    return pl.pallas_call(
        body, name="attn_bwd", grid=(nb,),
        in_specs=[pl.BlockSpec(memory_space=pltpu.SMEM),
                  pl.BlockSpec((BLOCK, ATTN_W), lambda s: (nb - 1 - s, 0)),
                  pl.BlockSpec((BLOCK, 2 * KV_W), lambda s: (jnp.maximum(nb - 2 - s, 0), 0)),
                  pl.BlockSpec((BLOCK, 2 * KV_W), lambda s: (nb - 1 - s, 0)),
                  pl.BlockSpec((BLOCK, ATTN_W), lambda s: (nb - 1 - s, 0))],
        out_specs=[pl.BlockSpec((BLOCK, ATTN_W), lambda s: (nb - 1 - s, 0)),
                   pl.BlockSpec((BLOCK, 2 * KV_W), lambda s: (nb - 1 - s, 0)),
                   pl.BlockSpec((1, N_Q, LANE), lambda s: (nb - 1 - s, 0, 0))],
        out_shape=[jax.ShapeDtypeStruct((seq, ATTN_W), F32), jax.ShapeDtypeStruct((seq, 2 * KV_W), F32),
                   jax.ShapeDtypeStruct((nb, N_Q, LANE), F32)],
        scratch_shapes=[pltpu.VMEM((BLOCK, 2 * KV_W), F32)],
        compiler_params=_params(("arbitrary",)),
    )(sinks, q, kv, kv, dout)


def _discretize(lr, li, ldt, br, bi):
    dt = jnp.exp(ldt)
    mag = jnp.exp(lr * dt)
    ang = li * dt
    ab_r = mag * jnp.cos(ang)
    ab_i = mag * jnp.sin(ang)
    nr = ab_r - 1.0
    ni = ab_i
    den = lr * lr + li * li
    f_r = (nr * lr + ni * li) / den
    f_i = (ni * lr - nr * li) / den
    return ab_r, ab_i, f_r * br - f_i * bi, f_r * bi + f_i * br


def _ssm_prepare(lr, li, ldt, br, bi):
    n = lr.shape[0]
    tn = N_CH
    col = pl.BlockSpec((tn, 1), lambda i: (i, 0))
    mat = pl.BlockSpec((tn, GROUP_W), lambda i: (i, 0))

    def body(lr_ref, li_ref, ldt_ref, br_ref, bi_ref, ar_ref, ai_ref, bbr_ref, bbi_ref):
        ar, ai, bbr, bbi = _discretize(lr_ref[...], li_ref[...], ldt_ref[...], br_ref[...], bi_ref[...])
        ar_ref[...] = ar
        ai_ref[...] = ai
        bbr_ref[...] = bbr
        bbi_ref[...] = bbi

    cs = jax.ShapeDtypeStruct((n, 1), F32)
    ms = jax.ShapeDtypeStruct((n, GROUP_W), F32)
    return pl.pallas_call(
        body, name="ssm_prepare", grid=(n // tn,),
        in_specs=[col, col, col, mat, mat], out_specs=[col, col, mat, mat], out_shape=[cs, cs, ms, ms],
        compiler_params=_params(("parallel",)),
    )(lr, li, ldt, br, bi)


def _ssm_prepare_bwd(lr, li, ldt, br, bi, dar, dai, dbbr, dbbi):
    n = lr.shape[0]
    tn = N_CH
    col = pl.BlockSpec((tn, 1), lambda i: (i, 0))
    mat = pl.BlockSpec((tn, GROUP_W), lambda i: (i, 0))

    def body(lr_ref, li_ref, ldt_ref, br_ref, bi_ref, dar_ref, dai_ref, dbbr_ref, dbbi_ref,
             dlr_ref, dli_ref, dldt_ref, dbr_ref, dbi_ref):
        _, vjp = jax.vjp(_discretize, lr_ref[...], li_ref[...], ldt_ref[...], br_ref[...], bi_ref[...])
        dlr, dli, dldt, dbr, dbi = vjp((dar_ref[...], dai_ref[...], dbbr_ref[...], dbbi_ref[...]))
        dlr_ref[...] = dlr
        dli_ref[...] = dli
        dldt_ref[...] = dldt
        dbr_ref[...] = dbr
        dbi_ref[...] = dbi

    cs = jax.ShapeDtypeStruct((n, 1), F32)
    ms = jax.ShapeDtypeStruct((n, GROUP_W), F32)
    return pl.pallas_call(
        body, name="ssm_prepare_bwd", grid=(n // tn,),
        in_specs=[col, col, col, mat, mat, col, col, mat, mat],
        out_specs=[col, col, col, mat, mat], out_shape=[cs, cs, cs, ms, ms],
        compiler_params=_params(("parallel",)),
    )(lr, li, ldt, br, bi, dar, dai, dbbr, dbbi)


def _load_slabs(src4_ref, dst):
    for s in range(STEPS):
        dst[s * SEGS:(s + 1) * SEGS, :] = jnp.concatenate(
            [src4_ref[j, pl.ds(s, SEGS, stride=STEPS), :] for j in range(4)], axis=1)


def _store_slabs(src, dst4_ref):
    for s in range(STEPS):
        for j in range(4):
            dst4_ref[j, pl.ds(s, SEGS, stride=STEPS), :] = src[s * SEGS:(s + 1) * SEGS, j * LANE:(j + 1) * LANE]


def _power_table(ar_ref, ai_ref, pwr, pwi):
    ar, ai = ar_ref[0], ai_ref[0]
    pr, pi = ar, ai
    pwr[0:1, :] = pr
    pwi[0:1, :] = pi
    for k in range(1, STEPS):
        pr, pi = pr * ar - pi * ai, pr * ai + pi * ar
        pwr[k:k + 1, :] = pr
        pwi[k:k + 1, :] = pi


def _scan_states(ubf, ar_ref, ai_ref, bbr_ref, bbi_ref, pwr, pwi, cin_r, cin_i, hr, hi):
    for k in range(2):
        rows = slice(k * 256, (k + 1) * 256)
        cols = slice(k * HALF_CH, (k + 1) * HALF_CH)
        hr[:, cols] = _nn(ubf[:, rows], bbr_ref[0, k])
        hi[:, cols] = _nn(ubf[:, rows], bbi_ref[0, k])
    for st in range(N_CH // STRIP):
        cs = slice(st * STRIP, (st + 1) * STRIP)
        arb = jnp.broadcast_to(ar_ref[0, :, cs], (SEGS, STRIP))
        aib = jnp.broadcast_to(ai_ref[0, :, cs], (SEGS, STRIP))

        def step(s, carry, cs=cs, arb=arb, aib=aib):
            cr, ci = carry
            rows = pl.ds(pl.multiple_of(s * SEGS, SEGS), SEGS)
            nr = arb * cr - aib * ci + hr[rows, cs]
            ni = arb * ci + aib * cr + hi[rows, cs]
            hr[rows, cs] = nr
            hi[rows, cs] = ni
            return nr, ni

        zero = jnp.zeros((SEGS, STRIP), F32)
        lax.fori_loop(0, STEPS, step, (zero, zero), unroll=True)
    last = slice((STEPS - 1) * SEGS, STEPS * SEGS)
    end_r, end_i = hr[last, :], hi[last, :]
    a64r, a64i = pwr[STEPS - 1:STEPS, :], pwi[STEPS - 1:STEPS, :]
    cr, ci = cin_r, cin_i
    rows_r, rows_i = [], []
    for j in range(SEGS):
        rows_r.append(cr)
        rows_i.append(ci)
        cr, ci = (a64r * cr - a64i * ci + end_r[j:j + 1, :], a64r * ci + a64i * cr + end_i[j:j + 1, :])
    cm_r, cm_i = jnp.concatenate(rows_r, axis=0), jnp.concatenate(rows_i, axis=0)
    for st in range(N_CH // STRIP):
        cs = slice(st * STRIP, (st + 1) * STRIP)
        cmr, cmi = cm_r[:, cs], cm_i[:, cs]

        def fix(s, carry, cs=cs, cmr=cmr, cmi=cmi):
            rows = pl.ds(pl.multiple_of(s * SEGS, SEGS), SEGS)
            pr, pi = pwr[pl.ds(s, 1), cs], pwi[pl.ds(s, 1), cs]
            hr[rows, cs] = hr[rows, cs] + (pr * cmr - pi * cmi)
            hi[rows, cs] = hi[rows, cs] + (pr * cmi + pi * cmr)
            return carry

        lax.fori_loop(0, STEPS, fix, 0, unroll=True)
    return (cm_r, cm_i), (cr, ci)


def _ssm_outputs(u, hr, hi, crt_ref, cit_ref, vec512_ref, wg_ref):
    ys = []
    for k in range(2):
        cols = slice(k * HALF_CH, (k + 1) * HALF_CH)
        ys.append(_nn(hr[:, cols].astype(BF16), crt_ref[0, k]) - _nn(hi[:, cols].astype(BF16), cit_ref[0, k]))
    y = jnp.concatenate(ys, axis=1) + vec512_ref[0, H_DSKIP:H_DSKIP + 1, :] * u
    z, t = _gelu(y)
    gate = jax.nn.sigmoid(_nn(z.astype(BF16), wg_ref[...]) + vec512_ref[0, H_BGLU:H_BGLU + 1, :])
    return y, z, t, gate


def _ssm_specs(layer, nck, rev):
    def chunk(i):
        return nck - 1 - i if rev else i

    return [pl.BlockSpec((4, CHUNK, LANE), lambda i: (0, chunk(i), 0)),
            pl.BlockSpec((1, 1, N_CH), lambda i: (layer, 0, 0)),
            pl.BlockSpec((1, 1, N_CH), lambda i: (layer, 0, 0)),
            pl.BlockSpec((1, 2, 256, HALF_CH), lambda i: (layer, 0, 0, 0)),
            pl.BlockSpec((1, 2, 256, HALF_CH), lambda i: (layer, 0, 0, 0)),
            pl.BlockSpec((1, 2, HALF_CH, 256), lambda i: (layer, 0, 0, 0)),
            pl.BlockSpec((1, 2, HALF_CH, 256), lambda i: (layer, 0, 0, 0)),
            pl.BlockSpec((1, 8, SSM_W), lambda i: (layer, 0, 0)),
            pl.BlockSpec((SSM_W, SSM_W), lambda i: (0, 0))]


def _ssm_fwd(u4, a_r, a_i, bb_r, bb_i, c_rt, c_it, vec512, w_glu, layer):
    seq = u4.shape[1]
    nck = seq // CHUNK

    def body(u4_ref, ar_ref, ai_ref, bbr_ref, bbi_ref, crt_ref, cit_ref, vec512_ref, wg_ref,
             s4_ref, cmr_ref, cmi_ref, hr, hi, pwr, pwi, car, cai, ubuf, obuf):
        i = pl.program_id(0)

        @pl.when(i == 0)
        def _():
            car[...] = jnp.zeros_like(car)
            cai[...] = jnp.zeros_like(cai)
            _power_table(ar_ref, ai_ref, pwr, pwi)

        _load_slabs(u4_ref, ubuf)
        u = ubuf[...]
        (cm_r, cm_i), (er, ei) = _scan_states(u.astype(BF16), ar_ref, ai_ref, bbr_ref, bbi_ref, pwr, pwi,
                                              car[...], cai[...], hr, hi)
        cmr_ref[0] = cm_r
        cmi_ref[0] = cm_i
        car[...] = er
        cai[...] = ei
        _, z, _, gate = _ssm_outputs(u, hr, hi, crt_ref, cit_ref, vec512_ref, wg_ref)
        obuf[...] = z * gate
        _store_slabs(obuf, s4_ref)

    return pl.pallas_call(
        body, name="ssm_fwd", grid=(nck,),
        in_specs=_ssm_specs(layer, nck, False),
        out_specs=[pl.BlockSpec((4, CHUNK, LANE), lambda i: (0, i, 0)),
                   pl.BlockSpec((1, SEGS, N_CH), lambda i: (i, 0, 0)),
                   pl.BlockSpec((1, SEGS, N_CH), lambda i: (i, 0, 0)),
                   pl.BlockSpec((CHUNK, N_CH), lambda i: (i, 0)),
                   pl.BlockSpec((CHUNK, N_CH), lambda i: (i, 0))],
        out_shape=[jax.ShapeDtypeStruct((4, seq, LANE), F32), jax.ShapeDtypeStruct((nck, SEGS, N_CH), F32),
                   jax.ShapeDtypeStruct((nck, SEGS, N_CH), F32), jax.ShapeDtypeStruct((seq, N_CH), F32),
                   jax.ShapeDtypeStruct((seq, N_CH), F32)],
        scratch_shapes=[pltpu.VMEM((STEPS, N_CH), F32), pltpu.VMEM((STEPS, N_CH), F32),
                        pltpu.VMEM((1, N_CH), F32), pltpu.VMEM((1, N_CH), F32),
                        pltpu.VMEM((CHUNK, SSM_W), F32), pltpu.VMEM((CHUNK, SSM_W), F32)],
        compiler_params=_params(("arbitrary",)),
    )(u4, a_r, a_i, bb_r, bb_i, c_rt, c_it, vec512, w_glu)


def _ssm_bwd(u4, ds4, cm_r, cm_i, h_r, h_i, a_r, a_i, bb_r, bb_i, c_rt, c_it, vec512, w_glu, layer):
    seq = u4.shape[1]
    nck = seq // CHUNK

    def body(u4_ref, ar_ref, ai_ref, bbr_ref, bbi_ref, crt_ref, cit_ref, vec512_ref, wg_ref, ds4_ref, cmr_ref, cmi_ref,
             hr, hi, du4_ref, dbbr_out, dbbi_out, dcrt_out, dcit_out, dar_ref, dai_ref, dwg_ref, dvec_ref,
             gr, gi, pwr, pwi, gcr, gci, accr, acci, ubuf, dbuf, dbbr_ref, dbbi_ref, dcrt_ref, dcit_ref):
        i = pl.program_id(0)

        @pl.when(i == 0)
        def _():
            for ref in (gcr, gci, accr, acci, dbbr_ref, dbbi_ref, dcrt_ref, dcit_ref, dwg_ref, dvec_ref):
                ref[...] = jnp.zeros_like(ref)
            _power_table(ar_ref, ai_ref, pwr, pwi)

        _load_slabs(u4_ref, ubuf)
        u = ubuf[...]
        ubf = u.astype(BF16)
        cm_r, cm_i = cmr_ref[0], cmi_ref[0]
        y, z, t, gate = _ssm_outputs(u, hr, hi, crt_ref, cit_ref, vec512_ref, wg_ref)
        _load_slabs(ds4_ref, dbuf)
        ds = dbuf[...]
        da = ds * z * gate * (1.0 - gate)
        dab = da.astype(BF16)
        dz = ds * gate + _nt(dab, wg_ref[...])
        dwg_ref[...] += _tn(z.astype(BF16), dab)
        dvec_ref[H_BGLU:H_BGLU + 1, :] += _colsum(da)
        dy = dz * _gelu_grad(y, t)
        dvec_ref[H_DSKIP:H_DSKIP + 1, :] += _colsum(dy * u)
        du_skip = dy * vec512_ref[0, H_DSKIP:H_DSKIP + 1, :]
        dyb = dy.astype(BF16)
        for k in range(2):
            rows = slice(k * 256, (k + 1) * 256)
            cols = slice(k * HALF_CH, (k + 1) * HALF_CH)
            dcrt_ref[k] += _tn(hr[:, cols].astype(BF16), dyb[:, rows])
            dcit_ref[k] -= _tn(hi[:, cols].astype(BF16), dyb[:, rows])
            gr[:, cols] = _nt(dyb[:, rows], crt_ref[0, k])
            gi[:, cols] = -_nt(dyb[:, rows], cit_ref[0, k])
        for st in range(N_CH // STRIP):
            cs = slice(st * STRIP, (st + 1) * STRIP)
            arb = jnp.broadcast_to(ar_ref[0, :, cs], (SEGS, STRIP))
            aib = jnp.broadcast_to(ai_ref[0, :, cs], (SEGS, STRIP))

            def step(k, carry, cs=cs, arb=arb, aib=aib):
                cr, ci = carry
                rows = pl.ds(pl.multiple_of((STEPS - 1 - k) * SEGS, SEGS), SEGS)
                nr = gr[rows, cs] + (arb * cr + aib * ci)
                ni = gi[rows, cs] + (arb * ci - aib * cr)
                gr[rows, cs] = nr
                gi[rows, cs] = ni
                return nr, ni

            zero = jnp.zeros((SEGS, STRIP), F32)
            lax.fori_loop(0, STEPS, step, (zero, zero), unroll=True)
        first_r, first_i = gr[0:SEGS, :], gi[0:SEGS, :]
        a64r, a64i = pwr[STEPS - 1:STEPS, :], pwi[STEPS - 1:STEPS, :]
        dr_, di_ = gcr[...], gci[...]
        rows_r, rows_i = [None] * SEGS, [None] * SEGS
        for j in range(SEGS - 1, -1, -1):
            rows_r[j], rows_i[j] = dr_, di_
            dr_, di_ = (first_r[j:j + 1, :] + (a64r * dr_ + a64i * di_), first_i[j:j + 1, :] + (a64r * di_ - a64i * dr_))
        gcr[...] = dr_
        gci[...] = di_
        dm_r, dm_i = jnp.concatenate(rows_r, axis=0), jnp.concatenate(rows_i, axis=0)
        for st in range(N_CH // STRIP):
            cs = slice(st * STRIP, (st + 1) * STRIP)
            dmr, dmi = dm_r[:, cs], dm_i[:, cs]

            def fixed(s, cs=cs, dmr=dmr, dmi=dmi):
                rows = pl.ds(pl.multiple_of(s * SEGS, SEGS), SEGS)
                pr, pi = pwr[pl.ds(STEPS - 1 - s, 1), cs], pwi[pl.ds(STEPS - 1 - s, 1), cs]
                g_r = gr[rows, cs] + (pr * dmr + pi * dmi)
                g_i = gi[rows, cs] + (pr * dmi - pi * dmr)
                gr[rows, cs] = g_r
                gi[rows, cs] = g_i
                return g_r, g_i

            g_r, g_i = fixed(jnp.int32(0))
            acc0 = (g_r * cm_r[:, cs] + g_i * cm_i[:, cs], g_i * cm_r[:, cs] - g_r * cm_i[:, cs])

            def step(s, carry, cs=cs, fixed=fixed):
                sr, si = carry
                g_r, g_i = fixed(s)
                prev = pl.ds(pl.multiple_of((s - 1) * SEGS, SEGS), SEGS)
                hpr, hpi = hr[prev, cs], hi[prev, cs]
                return sr + (g_r * hpr + g_i * hpi), si + (g_i * hpr - g_r * hpi)

            sr, si = lax.fori_loop(1, STEPS, step, acc0, unroll=True)
            accr[:, cs] += sr
            acci[:, cs] += si
        grb, gib = gr[...].astype(BF16), gi[...].astype(BF16)
        dus = []
        for k in range(2):
            rows = slice(k * 256, (k + 1) * 256)
            cols = slice(k * HALF_CH, (k + 1) * HALF_CH)
            dus.append(_nt(grb[:, cols], bbr_ref[0, k]) + _nt(gib[:, cols], bbi_ref[0, k]))
            dbbr_ref[k] += _tn(ubf[:, rows], grb[:, cols])
            dbbi_ref[k] += _tn(ubf[:, rows], gib[:, cols])
        dbuf[...] = jnp.concatenate(dus, axis=1) + du_skip
        _store_slabs(dbuf, du4_ref)

        @pl.when(i == nck - 1)
        def _():
            dar_ref[...] = _colsum(accr[...])
            dai_ref[...] = _colsum(acci[...])
            ng = N_GROUPS // 2
            for k in range(2):
                for acc, out in ((dbbr_ref, dbbr_out), (dbbi_ref, dbbi_out)):
                    out[k] = jnp.concatenate(
                        [acc[k, g * GROUP_W:(g + 1) * GROUP_W, g * STATE:(g + 1) * STATE] for g in range(ng)], axis=0)
                for acc, out in ((dcrt_ref, dcrt_out), (dcit_ref, dcit_out)):
                    out[k] = jnp.concatenate(
                        [acc[k, g * STATE:(g + 1) * STATE, g * GROUP_W:(g + 1) * GROUP_W] for g in range(ng)], axis=0)

    rev4 = pl.BlockSpec((4, CHUNK, LANE), lambda i: (0, nck - 1 - i, 0))
    hc_spec = pl.BlockSpec((1, SEGS, N_CH), lambda i: (nck - 1 - i, 0, 0))
    h_spec = pl.BlockSpec((CHUNK, N_CH), lambda i: (nck - 1 - i, 0))
    fixed2 = lambda shape: pl.BlockSpec(shape, lambda i: (0,) * len(shape))
    return pl.pallas_call(
        body, name="ssm_bwd", grid=(nck,),
        in_specs=_ssm_specs(layer, nck, True) + [rev4, hc_spec, hc_spec, h_spec, h_spec],
        out_specs=[rev4, fixed2((2, 256, STATE)), fixed2((2, 256, STATE)), fixed2((2, HALF_CH, GROUP_W)),
                   fixed2((2, HALF_CH, GROUP_W)), fixed2((1, N_CH)), fixed2((1, N_CH)), fixed2((SSM_W, SSM_W)),
                   fixed2((8, SSM_W))],
        out_shape=[jax.ShapeDtypeStruct((4, seq, LANE), F32),
                   jax.ShapeDtypeStruct((2, 256, STATE), F32), jax.ShapeDtypeStruct((2, 256, STATE), F32),
                   jax.ShapeDtypeStruct((2, HALF_CH, GROUP_W), F32), jax.ShapeDtypeStruct((2, HALF_CH, GROUP_W), F32),
                   jax.ShapeDtypeStruct((1, N_CH), F32), jax.ShapeDtypeStruct((1, N_CH), F32),
                   jax.ShapeDtypeStruct((SSM_W, SSM_W), F32), jax.ShapeDtypeStruct((8, SSM_W), F32)],
        scratch_shapes=[pltpu.VMEM((CHUNK, N_CH), F32), pltpu.VMEM((CHUNK, N_CH), F32),
                        pltpu.VMEM((STEPS, N_CH), F32), pltpu.VMEM((STEPS, N_CH), F32),
                        pltpu.VMEM((1, N_CH), F32), pltpu.VMEM((1, N_CH), F32),
                        pltpu.VMEM((SEGS, N_CH), F32), pltpu.VMEM((SEGS, N_CH), F32),
                        pltpu.VMEM((CHUNK, SSM_W), F32), pltpu.VMEM((CHUNK, SSM_W), F32),
                        pltpu.VMEM((2, 256, HALF_CH), F32), pltpu.VMEM((2, 256, HALF_CH), F32),
                        pltpu.VMEM((2, HALF_CH, 256), F32), pltpu.VMEM((2, HALF_CH, 256), F32)],
        compiler_params=_params(("arbitrary",)),
    )(u4, a_r, a_i, bb_r, bb_i, c_rt, c_it, vec512, w_glu, ds4, cm_r, cm_i, h_r, h_i)


def _block_diag(t):
    nl, _, ng, a, b = t.shape
    eye = jnp.eye(ng, dtype=t.dtype)
    return jnp.einsum("gh,lkgab->lkgahb", eye, t).reshape(nl, 2, ng * a, ng * b)


def _local_step(x, loss_target, mod, p, comm):
    nl = mod.shape[0]
    pad1024 = jnp.zeros((nl, 16 - 10, D_MODEL), F32)
    vec = jnp.concatenate([mod.reshape(nl, N_MOD, D_MODEL), p["pre_mix_g"][:, None], p["post_mix_g"][:, None],
                           p["pre_mlp_g"][:, None], p["post_mlp_g"][:, None], pad1024], axis=1)
    vec512 = jnp.concatenate([p["attn_out_g"][:, None], p["ssm_out_g"][:, None], p["d_skip"][:, None],
                              p["b_glu"][:, None], jnp.zeros((nl, 4, SSM_W), F32)], axis=1)
    n_all = nl * N_CH
    lr = p["lam_re"].reshape(n_all, 1)
    li = p["lam_im"].reshape(n_all, 1)
    ldt = jnp.broadcast_to(p["log_dt"][:, :, None], (nl, N_GROUPS, STATE)).reshape(n_all, 1)
    br = p["b_re"].reshape(n_all, GROUP_W)
    bi = p["b_im"].reshape(n_all, GROUP_W)
    ab_r, ab_i, bb_r, bb_i = _ssm_prepare(lr, li, ldt, br, bi)
    a_r = ab_r.reshape(nl, 1, N_CH)
    a_i = ab_i.reshape(nl, 1, N_CH)

    def dense_b(bb):
        return _block_diag(bb.reshape(nl, 2, 16, STATE, GROUP_W).transpose(0, 1, 2, 4, 3)).astype(BF16)

    def dense_c(cc):
        return _block_diag(cc.reshape(nl, 2, 16, GROUP_W, STATE).transpose(0, 1, 2, 4, 3)).astype(BF16)

    bbr_d, bbi_d = dense_b(bb_r), dense_b(bb_i)
    crt_d, cit_d = dense_c(p["c_re"]), dense_c(p["c_im"])

    saved = []
    xl = x
    mixer_w, mlp_w = [None] * nl, [None] * nl
    for l in range(nl):
        mixer_w[l], tok = comm.mixer_weights(l, [xl, bbr_d, bbi_d, crt_d, cit_d] if l == 0 else xl)
        w_in_t, w_glu, w_out = mixer_w[l]
        q, kv, u4, h1 = _in_proj_fwd(xl, _after(vec, *tok), w_in_t, l)
        attn = _attn_fwd(q, kv, p["attn_sinks"][l])
        s4, *states = _ssm_fwd(u4, a_r, a_i, bbr_d, bbi_d, crt_d, cit_d, vec512, w_glu, l)
        x1 = _out_proj_fwd(xl, attn, s4, vec, vec512, w_out, l)
        mlp_w[l] = comm.mlp_weights(l, x1)
        x2, r, f = _mlp_fwd(x1, vec, mlp_w[l][0], mlp_w[l][1], l)
        saved.append((xl, q, kv, u4, h1, attn, s4, states, x1, r, f))
        xl = x2

    dx, loss_parts = _loss_head(xl, loss_target)
    loss = jnp.sum(loss_parts[:, 0, 0])

    dvec_l, dvec512_l, dsink_l = [None] * nl, [None] * nl, [None] * nl
    dab_r, dab_i, dbb_r, dbb_i, dc_re, dc_im = ([None] * nl for _ in range(6))
    toks = []
    for l in range(nl - 1, -1, -1):
        xl, q, kv, u4, h1, attn, s4, states, x1, r, f = saved[l]
        w_in_t, w_glu, w_out = mixer_w[l]
        dx1, h2, da, df, dvec_m = _mlp_bwd(dx, x1, r, f, _after(vec, *toks), mlp_w[l][0], mlp_w[l][1], l)
        toks = comm.after_mlp_bwd(l, dx1)
        dw_mlp_out = _matmul_tn(r, df, BF16, "dw_mlp_out").reshape(4, D_FF // 4, D_MODEL)
        dw_mlp_in = _matmul_tn(h2, da, BF16, "dw_mlp_in", pieces=4)
        toks = toks + comm.mlp_grads(l, [dw_mlp_in, dw_mlp_out])
        dattn, ds4, heads, dmixed, dvec_o, dvec512_o = _out_proj_bwd(
            dx1, attn, s4, _after(vec, *toks), vec512, w_out, l)
        dw_out = _matmul_tn(heads, dmixed, BF16, "dw_out").reshape(4, D_MODEL // 4, D_MODEL)
        dq, dkv, dsk = _attn_bwd(q, kv, p["attn_sinks"][l], dattn)
        (du4, dbbr, dbbi, dcrt, dcit, dar, dai, dwg, dvec512_s) = _ssm_bwd(
            u4, ds4, *states, a_r, a_i, bbr_d, bbi_d, crt_d, cit_d, vec512, w_glu, l)
        dw_glu = dwg.astype(BF16).reshape(4, SSM_W // 4, SSM_W)
        dx, dproj, dvec_i = _in_proj_bwd(dx1, dq, dkv, du4, xl, vec, w_in_t, l)
        toks = comm.after_in_proj_bwd(l, dx)
        dw_in = _matmul_tn(h1, dproj, BF16, "dw_in")
        toks = toks + comm.mixer_grads(l, [dw_in.reshape(D_MODEL, 4, IN_W // 4).transpose(1, 0, 2), dw_glu, dw_out])
        dvec_l[l] = dvec_m + dvec_o + dvec_i
        dvec512_l[l] = dvec512_o + dvec512_s
        dsink_l[l] = jnp.sum(dsk[:, :, 0], axis=0)
        dab_r[l], dab_i[l], dbb_r[l], dbb_i[l], dc_re[l], dc_im[l] = dar, dai, dbbr, dbbi, dcrt, dcit

    dvec = _after(jnp.stack(dvec_l), *toks)
    dvec512 = jnp.stack(dvec512_l)
    ng = N_GROUPS // 2

    def b_cols(d):
        return jnp.stack(d).reshape(nl, 2, ng, GROUP_W, STATE).transpose(0, 1, 2, 4, 3).reshape(n_all, GROUP_W)

    def c_param(d):
        return jnp.stack(d).reshape(nl, 2, ng, STATE, GROUP_W).transpose(0, 1, 2, 4, 3).reshape(c_shape)

    dbb_r_c, dbb_i_c = b_cols(dbb_r), b_cols(dbb_i)
    c_shape = (nl, N_GROUPS, GROUP_W, STATE)
    dlr, dli, dldt, dbr, dbi = _ssm_prepare_bwd(
        lr, li, ldt, br, bi, jnp.stack(dab_r).reshape(n_all, 1), jnp.stack(dab_i).reshape(n_all, 1), dbb_r_c, dbb_i_c)
    small = {
        "b_ada": dvec[:, :N_MOD].reshape(nl, N_MOD * D_MODEL),
        "pre_mix_g": dvec[:, V_PRE_MIX], "post_mix_g": dvec[:, V_POST_MIX],
        "pre_mlp_g": dvec[:, V_PRE_MLP], "post_mlp_g": dvec[:, V_POST_MLP],
        "attn_sinks": jnp.stack(dsink_l),
        "lam_re": dlr.reshape(nl, N_GROUPS, STATE), "lam_im": dli.reshape(nl, N_GROUPS, STATE),
        "log_dt": jnp.sum(dldt.reshape(nl, N_GROUPS, STATE), axis=-1),
        "b_re": dbr.reshape(nl, N_GROUPS, STATE, GROUP_W), "b_im": dbi.reshape(nl, N_GROUPS, STATE, GROUP_W),
        "c_re": c_param(dc_re), "c_im": c_param(dc_im),
        "d_skip": dvec512[:, H_DSKIP], "b_glu": dvec512[:, H_BGLU],
        "attn_out_g": dvec512[:, H_ATTN_G], "ssm_out_g": dvec512[:, H_SSM_G],
    }
    return loss, dx, small, small["b_ada"]


WEIGHTS = ["w_ada", "b_ada", "pre_mix_g", "w_in", "attn_sinks", "lam_re", "lam_im", "log_dt", "b_re", "b_im", "c_re",
           "c_im", "d_skip", "w_glu", "b_glu", "attn_out_g", "ssm_out_g", "w_out", "post_mix_g", "pre_mlp_g",
           "w_mlp_in", "w_mlp_out", "post_mlp_g"]
BIG = ["w_in", "w_glu", "w_out", "w_mlp_in", "w_mlp_out"]
SMALL = [n for n in WEIGHTS if n not in BIG and n != "w_ada"]
PACK_ROWS = 256


def _pack(parts):
    rows = []
    for n in SMALL:
        flat = parts[n].reshape(-1)
        pad = (-flat.shape[0]) % (PACK_ROWS * LANE)
        rows.append(jnp.pad(flat, (0, pad)).reshape(-1, LANE))
    return jnp.concatenate(rows, axis=0)


def _unpack(packed, shapes):
    out, r0 = {}, 0
    for n in SMALL:
        size = int(np.prod(shapes[n]))
        rows = -(-size // (PACK_ROWS * LANE)) * PACK_ROWS
        out[n] = packed[r0:r0 + rows].reshape(-1)[:size].reshape(shapes[n])
        r0 += rows
    return out


MIXER = ["w_in", "w_glu", "w_out"]
MLP = ["w_mlp_in", "w_mlp_out"]


class _Exchanges:
    def __init__(self, shards, wts, mom, var, chip):
        self.shards, self.wts, self.mom, self.var, self.chip = shards, wts, mom, var, chip
        self.chip_arr = jnp.reshape(chip, (1,)).astype(jnp.int32)
        self.nl = len(shards["w_in"])
        self.gathers, self.scatters, self.pairs = {}, {}, {}
        self.res = {n: None for n in BIG}

    def _start_gather(self, group, tag, l, after=()):
        srcs = [self.shards[n][l] for n in group]
        lands = [lax.dynamic_update_slice(lax.empty((4,) + s.shape, s.dtype), s[None], (self.chip, 0, 0)) for s in srcs]
        plan = _plan_gather(len(srcs))
        st = _exchange_start(f"gather_{tag}{l}_start", 3 * len(srcs), plan, srcs + lands, after)
        self.gathers[tag, l] = (plan, st)
        return st[3]

    def _wait_gather(self, tag, l, after):
        plan, st = self.gathers.pop((tag, l))
        n = len(st[2]) // 2
        bufs = _exchange_wait(f"gather_{tag}{l}_wait", 3 * n, plan, st, after)
        return [b.reshape(4 * b.shape[1], b.shape[2]) for b in bufs[n:]]

    def _start_layer(self, l, after):
        tok = self._start_gather(MIXER, "mixer", l, after)
        return [tok, self._start_gather(MLP, "mlp", l, [tok])]

    def begin(self, after):
        toks = self._start_layer(0, after)
        return toks + (self._start_layer(1, toks[1:]) if self.nl > 1 else [])

    def mixer_weights(self, l, after):
        w = self._wait_gather("mixer", l, after)
        toks = self._start_layer(l + 2, w[:1]) if l + 2 < self.nl else []
        return w, toks

    def mlp_weights(self, l, after):
        return self._wait_gather("mlp", l, after)

    def _start_scatter(self, tag, l, group, pieces):
        plan = _plan_scatter(len(pieces))
        st = _exchange_start(f"scatter_{tag}{l}_start", 3 * len(pieces), plan,
                             list(pieces) + [lax.empty(p.shape, p.dtype) for p in pieces])
        self.scatters[tag] = (l, group, plan, st)
        return [st[3]]

    def _finish_scatter(self, tag, after):
        l, group, plan, st = self.scatters.pop(tag)
        n = len(group)
        bufs = _exchange_wait(f"scatter_{tag}{l}_wait", 3 * n, plan, st, after)
        sums = [_sum_pieces(bufs[k], bufs[n + k], self.chip_arr, "sum_" + group[k]) for k in range(n)]
        plan2 = _plan_pair(n)
        st2 = _exchange_start(f"pair_{tag}{l}_start", n, plan2, sums + [lax.empty(s.shape, s.dtype) for s in sums])
        self.pairs[tag] = (l, group, plan2, st2)
        return [st2[3]]

    def _finish_pair(self, tag, after):
        l, group, plan, st = self.pairs.pop(tag)
        n = len(group)
        bufs = _exchange_wait(f"pair_{tag}{l}_wait", n, plan, st, after)
        for k, name in enumerate(group):
            self.res[name] = _adamw_layer([bufs[k], bufs[n + k]], self.wts[name], self.mom[name], self.var[name],
                                          l, self.res[name], "adamw_" + name)

    def after_mlp_bwd(self, l, after):
        toks = self._finish_scatter("mixer", after) if "mixer" in self.scatters else []
        if "mlp" in self.pairs:
            self._finish_pair("mlp", after)
        return toks

    def mlp_grads(self, l, pieces):
        return self._start_scatter("mlp", l, MLP, pieces)

    def after_in_proj_bwd(self, l, after):
        toks = self._finish_scatter("mlp", after)
        if "mixer" in self.pairs:
            self._finish_pair("mixer", after)
        return toks

    def mixer_grads(self, l, pieces):
        return self._start_scatter("mixer", l, MIXER, pieces)

    def finish_mixer_scatter(self, after):
        return self._finish_scatter("mixer", after)

    def finish_mlp(self, after):
        self._finish_pair("mlp", after)

    def finish_mixer(self, after):
        self._finish_pair("mixer", after)

    def results(self):
        return self.res


def kernel(x, c, w_ada, b_ada, pre_mix_g, w_in, attn_sinks, lam_re, lam_im, log_dt, b_re, b_im, c_re, c_im, d_skip, w_glu, b_glu, attn_out_g, ssm_out_g, w_out, post_mix_g, pre_mlp_g, w_mlp_in, w_mlp_out, post_mlp_g, loss_target, m_w_ada, m_b_ada, m_pre_mix_g, m_w_in, m_attn_sinks, m_lam_re, m_lam_im, m_log_dt, m_b_re, m_b_im, m_c_re, m_c_im, m_d_skip, m_w_glu, m_b_glu, m_attn_out_g, m_ssm_out_g, m_w_out, m_post_mix_g, m_pre_mlp_g, m_w_mlp_in, m_w_mlp_out, m_post_mlp_g, v_w_ada, v_b_ada, v_pre_mix_g, v_w_in, v_attn_sinks, v_lam_re, v_lam_im, v_log_dt, v_b_re, v_b_im, v_c_re, v_c_im, v_d_skip, v_w_glu, v_b_glu, v_attn_out_g, v_ssm_out_g, v_w_out, v_post_mix_g, v_pre_mlp_g, v_w_mlp_in, v_w_mlp_out, v_post_mlp_g):
    args = locals()
    wts = {n: args[n] for n in WEIGHTS}
    mom = {n: args["m_" + n] for n in WEIGHTS}
    var = {n: args["v_" + n] for n in WEIGHTS}
    nl = w_in.shape[0]
    ix, iy, ic = lax.axis_index("x"), lax.axis_index("y"), lax.axis_index("c")
    chip = 2 * ix + iy
    me = 4 * ix + 2 * iy + ic
    wcols = w_ada.shape[2]

    shards = {"w_in": [w_in[l].astype(BF16).T for l in range(nl)], "w_glu": [w_glu[l].astype(BF16) for l in range(nl)],
              "w_out": [w_out[l].astype(BF16) for l in range(nl)],
              "w_mlp_in": [w_mlp_in[l].astype(BF16).T for l in range(nl)],
              "w_mlp_out": [w_mlp_out[l].astype(BF16) for l in range(nl)]}
    comm = _Exchanges(shards, wts, mom, var, chip)

    c_all = _gather([c.reshape(1, 1, 1, D_MODEL)], "all", "gather_c")[0].reshape(8, D_MODEL)
    b_sh = lax.dynamic_slice(b_ada, (0, chip * wcols), (nl, wcols)).reshape(nl, 1, wcols)
    mod_sh = _ada_forward(c_all, w_ada, b_sh)
    mod_all = _gather([mod_sh.reshape(1, 1, nl * 8, wcols)], "chips", "gather_mod")[0]
    toks = comm.begin([mod_all])
    mod = lax.dynamic_index_in_dim(mod_all.reshape(4, nl, 8, wcols), me, axis=2, keepdims=False)
    mod = mod.transpose(1, 0, 2).reshape(nl, 4 * wcols)

    small_p = {n: wts[n] for n in SMALL}
    small_p["log_dt"] = _after(log_dt, *toks)
    loss, grad_x, small, dmod = _local_step(x[0], loss_target[0], mod, small_p, comm)
    loss = lax.psum(loss, ("x", "y", "c"))

    packed = _pack(small)
    rows = packed.shape[0]
    pair_plan = _plan_pair(1)
    pair_small = _exchange_start("pair_small_start", 1, pair_plan, [packed, lax.empty((rows, LANE), F32)])

    dmod = _after(dmod, pair_small[3])
    dmod_all = _gather([dmod.reshape(1, 1, nl, N_MOD * D_MODEL)], "all", "gather_dmod")[0][0]
    dmod_sh = lax.dynamic_slice(dmod_all, (0, 0, chip * wcols), (8, nl, wcols)).transpose(1, 0, 2)
    g_ada = _ada_weight_grad(c_all.T, dmod_sh)
    res = {"w_ada": _adamw(g_ada[:, None], w_ada, m_w_ada, v_w_ada, "adamw_w_ada")}

    comm.finish_mlp(res["w_ada"][0])
    toks = comm.finish_mixer_scatter(res["w_ada"][0])

    own, other = _exchange_wait("pair_small_wait", 1, pair_plan, pair_small, res["w_ada"][0])
    chip_sum = _after(_sum_list([own, other], "sum_pair_small"), *toks)
    quad_plan = _plan_gather(1)
    quad0 = lax.dynamic_update_slice(lax.empty((4, rows, LANE), F32), chip_sum[None], (chip, 0, 0))
    quad_small = _exchange_start("gather_small_start", 3, quad_plan, [chip_sum, quad0])
    comm.finish_mixer([comm.results()[n][0] for n in MLP] + [_after(dmod, quad_small[3])])
    res.update(comm.results())
    quad = _exchange_wait("gather_small_wait", 3, quad_plan, quad_small, [res[n][0] for n in BIG])[1]
    outs = _adamw(quad[None], _pack({n: wts[n] for n in SMALL})[None], _pack({n: mom[n] for n in SMALL})[None],
                  _pack({n: var[n] for n in SMALL})[None], "adamw_small")
    shapes = {n: wts[n].shape for n in SMALL}
    unpacked = [_unpack(o[0], shapes) for o in outs]
    for n in SMALL:
        res[n] = [u[n] for u in unpacked]

    return (loss, grad_x[None], *[res[n][0] for n in WEIGHTS], *[res[n][1] for n in WEIGHTS],
            *[res[n][2] for n in WEIGHTS], *[res[n][3] for n in WEIGHTS])
```

```python
import functools
import math

import numpy as np
import jax
import jax.numpy as jnp
from jax import lax
from jax.experimental import pallas as pl
from jax.experimental.pallas import tpu as pltpu

F32 = jnp.float32
BF16 = jnp.bfloat16

D_MODEL = 1024
ATTN_W = 512
SSM_W = 512
HEAD_DIM = 64
N_Q = 8
N_KV = 2
Q_PER_KV = 4
KV_W = 128
WINDOW = 128
BLOCK = 128
N_GROUPS = 32
GROUP_W = 16
STATE = 64
N_CH = N_GROUPS * STATE
HALF_CH = N_CH // 2
D_FF = 4096
IN_W = 1280
N_MOD = 6
EPS = 1e-6
NEG_INF = -1e30

ADAM_LR = 0.001
ADAM_B1 = 0.9
ADAM_B2 = 0.999
ADAM_EPS = 1e-08
ADAM_WD = 0.01
ADAM_STEP = 10

ROW_TILE = 256
PROJ_TILE = 512
CHUNK = 256
SEGS = 8
STEPS = CHUNK // SEGS
STRIP = 1024
VMEM_LIMIT_V7X = 56 * 1024 * 1024
LANE = 128
SUBLANE = 8

GELU_K0 = math.sqrt(2.0 / math.pi)
GELU_K1 = 0.044715

V_SH1, V_SC1, V_G1, V_SH2, V_SC2, V_G2, V_PRE_MIX, V_POST_MIX, V_PRE_MLP, V_POST_MLP = range(10)
H_ATTN_G, H_SSM_G, H_DSKIP, H_BGLU = range(4)

HBM = pl.BlockSpec(memory_space=pltpu.HBM)
SEM = pl.BlockSpec(memory_space=pltpu.SEMAPHORE)
EFFECT = pltpu.SideEffectType.DATAFLOW_SIDE_EFFECTING
MESH_ID = pl.DeviceIdType.MESH


def _nn(a, b):
    return lax.dot_general(a, b, (((1,), (0,)), ((), ())), preferred_element_type=F32)


def _nt(a, b):
    return lax.dot_general(a, b, (((1,), (1,)), ((), ())), preferred_element_type=F32)


def _tn(a, b):
    return lax.dot_general(a, b, (((0,), (0,)), ((), ())), preferred_element_type=F32)


def _params(sem):
    return pltpu.CompilerParams(dimension_semantics=sem, vmem_limit_bytes=VMEM_LIMIT_V7X)


def _rms_fwd(x, g):
    r = lax.rsqrt(jnp.mean(x * x, axis=-1, keepdims=True) + EPS)
    xh = x * r
    return xh * g, xh, r


def _rms_bwd(dy, xh, r, g):
    dxh = dy * g
    dx = r * (dxh - xh * jnp.mean(dxh * xh, axis=-1, keepdims=True))
    return dx, dy * xh


def _colsum(t):
    return jnp.sum(t, axis=0, keepdims=True)


def _gelu(y):
    t = jnp.tanh(GELU_K0 * (y + GELU_K1 * (y * y * y)))
    return 0.5 * y * (1.0 + t), t


def _gelu_grad(y, t):
    return 0.5 * (1.0 + t) + 0.5 * y * (1.0 - t * t) * GELU_K0 * (1.0 + 3.0 * GELU_K1 * y * y)


def _alibi_slopes():
    return [float(s) for s in 2.0 ** (-8.0 * np.arange(1, N_Q + 1) / N_Q)]


def _pick_rows(rows, bytes_per_row, budget):
    t = rows
    while t % (2 * SUBLANE) == 0 and t * bytes_per_row > budget:
        t //= 2
    return t


def _load_once(step, pairs, sems):
    @pl.when(step == 0)
    def _():
        cps = [pltpu.make_async_copy(src, dst, sems.at[k]) for k, (src, dst) in enumerate(pairs)]
        for cp in cps:
            cp.start()
        for cp in cps:
            cp.wait()


_GROUPS = {
    "all": ([(0, 0, 1), (0, 1, 0), (0, 1, 1), (1, 0, 0), (1, 0, 1), (1, 1, 0), (1, 1, 1)], (4, 2, 1), 8),
    "chips": ([(1, 0, 0), (0, 1, 0), (1, 1, 0)], (2, 1, 0), 4),
    "pair": ([(0, 0, 1)], (0, 0, 1), 2),
}


def _flip(v, f):
    return 1 - v if f else v


def _gather(arrs, kind, name):
    masks, wts, n = _GROUPS[kind]
    na, nm = len(arrs), len(masks)

    def body(*refs):
        ins, outs = refs[:na], refs[na:2 * na]
        ssem, rsem, lsem = refs[2 * na:]
        x, y, c = lax.axis_index("x"), lax.axis_index("y"), lax.axis_index("c")
        me = wts[0] * x + wts[1] * y + wts[2] * c
        local = [pltpu.make_async_copy(ins[k], outs[k].at[:, pl.ds(me, 1)], lsem.at[k]) for k in range(na)]
        for cp in local:
            cp.start()
        remote = []
        for k in range(na):
            for mi, (fx, fy, fc) in enumerate(masks):
                peer = (_flip(x, fx), _flip(y, fy), _flip(c, fc))
                remote.append(pltpu.make_async_remote_copy(
                    src_ref=ins[k], dst_ref=outs[k].at[:, pl.ds(me, 1)],
                    send_sem=ssem.at[k * nm + mi], recv_sem=rsem.at[k * nm + mi],
                    device_id=peer, device_id_type=MESH_ID))
        for cp in remote:
            cp.start()
        for cp in remote:
            cp.wait()
        for cp in local:
            cp.wait()

    outs = pl.pallas_call(
        body, name=name,
        out_shape=[jax.ShapeDtypeStruct((a.shape[0], n) + a.shape[2:], a.dtype) for a in arrs],
        in_specs=[HBM] * na, out_specs=[HBM] * na,
        scratch_shapes=[pltpu.SemaphoreType.DMA((na * nm,)), pltpu.SemaphoreType.DMA((na * nm,)),
                        pltpu.SemaphoreType.DMA((na,))],
    )(*arrs)
    return list(outs)


def _hbm(a):
    return pltpu.with_memory_space_constraint(a, pltpu.HBM)


def _after(x, *tokens):
    for t in tokens:
        x = x + t[0, 0].astype(x.dtype)
    return x


def _exchange_start(name, n_copies, plan, bufs, after=()):
    n, na = len(bufs), len(after)

    def body(*refs):
        ssem, rsem, token = refs[n + na], refs[n + na + 1], refs[2 * n + na + 2]
        for k, (src, dst, dev) in enumerate(plan(refs[:n])):
            pltpu.make_async_remote_copy(src_ref=src, dst_ref=dst, send_sem=ssem.at[k], recv_sem=rsem.at[k],
                                         device_id=dev, device_id_type=MESH_ID).start()
        token[...] = jnp.zeros_like(token)

    outs = pl.pallas_call(
        body, name=name,
        out_shape=(pltpu.SemaphoreType.DMA((n_copies,)), pltpu.SemaphoreType.DMA((n_copies,)),
                   *[pltpu.HBM(b.shape, b.dtype) for b in bufs], jax.ShapeDtypeStruct((SUBLANE, LANE), F32)),
        in_specs=[HBM] * n + [pl.BlockSpec(memory_space=pl.ANY)] * na,
        out_specs=(SEM, SEM, *[HBM] * n, pl.BlockSpec(memory_space=pltpu.VMEM)),
        input_output_aliases={i: 2 + i for i in range(n)},
        compiler_params=pltpu.CompilerParams(has_side_effects=EFFECT),
    )(*[_hbm(b) for b in bufs], *after)
    return outs[0], outs[1], list(outs[2:2 + n]), outs[2 + n]


def _exchange_wait(name, n_copies, plan, started, after):
    ssem, rsem, bufs, _ = started
    n = len(bufs)
    after = list(after) if isinstance(after, (list, tuple)) else [after]

    def body(*refs):
        ssem_ref, rsem_ref = refs[n], refs[n + 1]
        for k, (src, dst, dev) in enumerate(plan(refs[:n])):
            cp = pltpu.make_async_remote_copy(src_ref=src, dst_ref=dst, send_sem=ssem_ref.at[k], recv_sem=rsem_ref.at[k],
                                              device_id=dev, device_id_type=MESH_ID)
            cp.wait_send()
            cp.wait_recv()

    outs = pl.pallas_call(
        body, name=name,
        out_shape=tuple(pltpu.HBM(b.shape, b.dtype) for b in bufs),
        in_specs=[HBM] * n + [SEM, SEM] + [pl.BlockSpec(memory_space=pl.ANY)] * len(after), out_specs=tuple([HBM] * n),
        input_output_aliases={i: i for i in range(n)},
        compiler_params=pltpu.CompilerParams(has_side_effects=EFFECT),
    )(*bufs, ssem, rsem, *after)
    return list(outs)


def _position():
    x, y, c = lax.axis_index("x"), lax.axis_index("y"), lax.axis_index("c")
    return x, y, c, [(1 - x, y), (x, 1 - y), (1 - x, 1 - y)]


def _plan_gather(na):
    def plan(refs):
        x, y, c, chips = _position()
        return [(refs[k], refs[na + k].at[2 * x + y], (px, py, c)) for k in range(na) for px, py in chips]
    return plan


def _plan_scatter(na):
    def plan(refs):
        x, y, c, chips = _position()
        return [(refs[k].at[2 * px + py], refs[na + k].at[2 * x + y], (px, py, c))
                for k in range(na) for px, py in chips]
    return plan


def _plan_pair(na):
    def plan(refs):
        x, y, c, _ = _position()
        return [(refs[k], refs[na + k], (x, y, 1 - c)) for k in range(na)]
    return plan


def _sum_list(arrs, name):
    n = len(arrs)
    r, c = arrs[0].shape
    tr = _pick_rows(r, c * 4 * (n + 1), 4 << 20)

    def body(*refs):
        acc = refs[0][...].astype(F32)
        for j in range(1, n):
            acc = acc + refs[j][...].astype(F32)
        refs[n][...] = acc

    blk = pl.BlockSpec((tr, c), lambda i: (i, 0))
    return pl.pallas_call(
        body, name=name, grid=(r // tr,), in_specs=[blk] * n, out_specs=blk,
        out_shape=jax.ShapeDtypeStruct((r, c), F32), compiler_params=_params(("parallel",)),
    )(*arrs)


def _sum_pieces(own, recv, chip, name):
    _, r, c = own.shape
    tr = _pick_rows(r, c * 2 * 6, 4 << 20)

    def body(chip_ref, own_ref, recv_ref, o_ref):
        acc = own_ref[0].astype(F32)
        for j in range(4):
            acc = acc + jnp.where(chip_ref[0] == j, 0.0, recv_ref[j].astype(F32))
        o_ref[...] = acc.astype(BF16)

    return pl.pallas_call(
        body, name=name,
        grid_spec=pltpu.PrefetchScalarGridSpec(
            num_scalar_prefetch=1, grid=(r // tr,),
            in_specs=[pl.BlockSpec((1, tr, c), lambda i, chip_ref: (chip_ref[0], i, 0)),
                      pl.BlockSpec((4, tr, c), lambda i, chip_ref: (0, i, 0))],
            out_specs=pl.BlockSpec((tr, c), lambda i, chip_ref: (i, 0))),
        out_shape=jax.ShapeDtypeStruct((r, c), BF16), compiler_params=_params(("parallel",)),
    )(chip, own, recv)


def _adam_update(g, w, m, v):
    mn = ADAM_B1 * m + (1.0 - ADAM_B1) * g
    vn = ADAM_B2 * v + (1.0 - ADAM_B2) * jnp.square(g)
    m_hat = mn / (1.0 - ADAM_B1 ** ADAM_STEP)
    v_hat = vn / (1.0 - ADAM_B2 ** ADAM_STEP)
    return -ADAM_LR * (m_hat / (jnp.sqrt(v_hat) + ADAM_EPS) + ADAM_WD * w), mn, vn


def _adamw_layer(grads, w, m, v, layer, prev, name):
    ng = len(grads)
    nl, r, c = w.shape
    tr = _pick_rows(r, c * 4 * (ng + 7), 6 << 20)
    if prev is None:
        prev = [lax.empty((nl, r, c), F32) for _ in range(4)]

    def body(*refs):
        g = refs[0][...].astype(F32)
        for j in range(1, ng):
            g = g + refs[j][...].astype(F32)
        w_ref, m_ref, v_ref = refs[ng:ng + 3]
        go_ref, d_ref, mo_ref, vo_ref = refs[ng + 7:ng + 11]
        d, mn, vn = _adam_update(g, w_ref[0], m_ref[0], v_ref[0])
        go_ref[0] = g
        d_ref[0] = d
        mo_ref[0] = mn
        vo_ref[0] = vn

    gblk = pl.BlockSpec((tr, c), lambda i: (i, 0))
    blk = pl.BlockSpec((1, tr, c), lambda i: (layer, i, 0))
    keep = pl.BlockSpec(memory_space=pl.ANY)
    sds = jax.ShapeDtypeStruct((nl, r, c), F32)
    return pl.pallas_call(
        body, name=name, grid=(r // tr,),
        in_specs=[gblk] * ng + [blk] * 3 + [keep] * 4,
        out_specs=[blk] * 4, out_shape=[sds] * 4,
        input_output_aliases={ng + 3 + i: i for i in range(4)},
        compiler_params=_params(("parallel",)),
    )(*grads, w, m, v, *prev)


def _adamw(gs, w, m, v, name):
    a, s, r, c = gs.shape
    tr = _pick_rows(r, c * 4 * (s + 7), 6 << 20)

    def body(g_ref, w_ref, m_ref, v_ref, go_ref, d_ref, mo_ref, vo_ref):
        g = g_ref[0, 0].astype(F32)
        for j in range(1, s):
            g = g + g_ref[0, j].astype(F32)
        d, mn, vn = _adam_update(g, w_ref[0], m_ref[0], v_ref[0])
        go_ref[0] = g
        d_ref[0] = d
        mo_ref[0] = mn
        vo_ref[0] = vn

    blk = pl.BlockSpec((1, tr, c), lambda i, j: (i, j, 0))
    sds = jax.ShapeDtypeStruct((a, r, c), F32)
    return pl.pallas_call(
        body, name=name, grid=(a, r // tr),
        in_specs=[pl.BlockSpec((1, s, tr, c), lambda i, j: (i, 0, j, 0)), blk, blk, blk],
        out_specs=[blk, blk, blk, blk], out_shape=[sds, sds, sds, sds],
        compiler_params=_params(("parallel", "parallel")),
    )(gs, w, m, v)


def _ada_forward(c_all, w_ada, b_sh):
    nl, d, w = w_ada.shape
    tw = 512

    def body(c_ref, w_ref, b_ref, o_ref):
        cv = c_ref[...]
        act = (cv * jax.nn.sigmoid(cv)).astype(BF16)
        o_ref[0] = _nn(act, w_ref[0].astype(BF16)) + b_ref[0]

    return pl.pallas_call(
        body, name="ada_forward", grid=(nl, w // tw),
        in_specs=[pl.BlockSpec((8, d), lambda l, j: (0, 0)),
                  pl.BlockSpec((1, d, tw), lambda l, j: (l, 0, j)),
                  pl.BlockSpec((1, 1, tw), lambda l, j: (l, 0, j))],
        out_specs=pl.BlockSpec((1, 8, tw), lambda l, j: (l, 0, j)),
        out_shape=jax.ShapeDtypeStruct((nl, 8, w), F32),
        compiler_params=_params(("parallel", "parallel")),
    )(c_all, w_ada, b_sh)


def _ada_weight_grad(c_all_t, dmod):
    nl, nb, w = dmod.shape
    d = c_all_t.shape[0]
    tw = 512

    def body(c_ref, g_ref, o_ref):
        cv = c_ref[...]
        act = cv * jax.nn.sigmoid(cv)
        gv = g_ref[0]
        acc = act[:, 0:1] * gv[0:1, :]
        for b in range(1, nb):
            acc = acc + act[:, b:b + 1] * gv[b:b + 1, :]
        o_ref[0] = acc

    return pl.pallas_call(
        body, name="ada_weight_grad", grid=(nl, w // tw),
        in_specs=[pl.BlockSpec((d, nb), lambda l, j: (0, 0)),
                  pl.BlockSpec((1, nb, tw), lambda l, j: (l, 0, j))],
        out_specs=pl.BlockSpec((1, d, tw), lambda l, j: (l, 0, j)),
        out_shape=jax.ShapeDtypeStruct((nl, d, w), F32),
        compiler_params=_params(("parallel", "parallel")),
    )(c_all_t, dmod)


def _in_proj_fwd(x, vec, w_in_t, layer):
    seq = x.shape[0]
    tm = PROJ_TILE

    def body(x_ref, vec_ref, w_ref, q_ref, kv_ref, u4_ref, h_ref):
        n, _, _ = _rms_fwd(x_ref[...], vec_ref[0, V_PRE_MIX:V_PRE_MIX + 1, :])
        h = (n * (1.0 + vec_ref[0, V_SC1:V_SC1 + 1, :]) + vec_ref[0, V_SH1:V_SH1 + 1, :]).astype(BF16)
        h_ref[...] = h
        proj = _nt(h, w_ref[...])
        q_ref[...] = proj[:, :ATTN_W].astype(BF16)
        kv_ref[...] = proj[:, ATTN_W:ATTN_W + 2 * KV_W].astype(BF16)
        u0 = ATTN_W + 2 * KV_W
        for j in range(4):
            u4_ref[j] = proj[:, u0 + j * LANE:u0 + (j + 1) * LANE]

    return pl.pallas_call(
        body, name="in_proj_fwd", grid=(seq // tm,),
        in_specs=[pl.BlockSpec((tm, D_MODEL), lambda i: (i, 0)),
                  pl.BlockSpec((1, 16, D_MODEL), lambda i: (layer, 0, 0)),
                  pl.BlockSpec((IN_W, D_MODEL), lambda i: (0, 0))],
        out_specs=[pl.BlockSpec((tm, ATTN_W), lambda i: (i, 0)),
                   pl.BlockSpec((tm, 2 * KV_W), lambda i: (i, 0)),
                   pl.BlockSpec((4, tm, LANE), lambda i: (0, i, 0)),
                   pl.BlockSpec((tm, D_MODEL), lambda i: (i, 0))],
        out_shape=[jax.ShapeDtypeStruct((seq, ATTN_W), BF16), jax.ShapeDtypeStruct((seq, 2 * KV_W), BF16),
                   jax.ShapeDtypeStruct((4, seq, LANE), F32), jax.ShapeDtypeStruct((seq, D_MODEL), BF16)],
        compiler_params=_params(("parallel",)),
    )(x, vec, w_in_t)


def _in_proj_bwd(dx1, dq, dkv, du4, x, vec, w_in_t, layer):
    seq = x.shape[0]
    tm = PROJ_TILE

    def body(dx1_ref, dq_ref, dkv_ref, du4_ref, x_ref, vec_ref, w_ref, dx_ref, dp_ref, dvec_ref):
        i = pl.program_id(0)

        @pl.when(i == 0)
        def _():
            dvec_ref[...] = jnp.zeros_like(dvec_ref)

        dproj = jnp.concatenate([dq_ref[...], dkv_ref[...]] + [du4_ref[j] for j in range(4)], axis=1).astype(BF16)
        dp_ref[...] = dproj
        dh = _nn(dproj, w_ref[...])
        g = vec_ref[0, V_PRE_MIX:V_PRE_MIX + 1, :]
        n, xh, r = _rms_fwd(x_ref[...], g)
        dn = dh * (1.0 + vec_ref[0, V_SC1:V_SC1 + 1, :])
        dxn, dg_rows = _rms_bwd(dn, xh, r, g)
        dx_ref[...] = dx1_ref[...] + dxn
        dvec_ref[V_SH1:V_SH1 + 1, :] += _colsum(dh)
        dvec_ref[V_SC1:V_SC1 + 1, :] += _colsum(dh * n)
        dvec_ref[V_PRE_MIX:V_PRE_MIX + 1, :] += _colsum(dg_rows)

    row = pl.BlockSpec((tm, D_MODEL), lambda i: (i, 0))
    return pl.pallas_call(
        body, name="in_proj_bwd", grid=(seq // tm,),
        in_specs=[row, pl.BlockSpec((tm, ATTN_W), lambda i: (i, 0)), pl.BlockSpec((tm, 2 * KV_W), lambda i: (i, 0)),
                  pl.BlockSpec((4, tm, LANE), lambda i: (0, i, 0)), row,
                  pl.BlockSpec((1, 16, D_MODEL), lambda i: (layer, 0, 0)),
                  pl.BlockSpec((IN_W, D_MODEL), lambda i: (0, 0))],
        out_specs=[row, pl.BlockSpec((tm, IN_W), lambda i: (i, 0)), pl.BlockSpec((16, D_MODEL), lambda i: (0, 0))],
        out_shape=[jax.ShapeDtypeStruct((seq, D_MODEL), F32), jax.ShapeDtypeStruct((seq, IN_W), BF16),
                   jax.ShapeDtypeStruct((16, D_MODEL), F32)],
        compiler_params=_params(("arbitrary",)),
    )(dx1, dq, dkv, du4, x, vec, w_in_t)


def _heads(attn_ref, s4_ref, vec512_ref):
    ga = vec512_ref[0, H_ATTN_G:H_ATTN_G + 1, :]
    gs = vec512_ref[0, H_SSM_G:H_SSM_G + 1, :]
    sv = jnp.concatenate([s4_ref[j] for j in range(4)], axis=1)
    na, ah, ar = _rms_fwd(attn_ref[...], ga)
    ns, sh, sr = _rms_fwd(sv, gs)
    return jnp.concatenate([na, ns], axis=1), (ah, ar, ga), (sh, sr, gs)


def _out_proj_fwd(x, attn, s4, vec, vec512, w_out, layer):
    seq = x.shape[0]
    tm = PROJ_TILE

    def body(x_ref, attn_ref, s4_ref, vec_ref, vec512_ref, w_ref, x1_ref):
        heads, _, _ = _heads(attn_ref, s4_ref, vec512_ref)
        mixed = _nn(heads.astype(BF16), w_ref[...])
        nm, _, _ = _rms_fwd(mixed, vec_ref[0, V_POST_MIX:V_POST_MIX + 1, :])
        x1_ref[...] = x_ref[...] + vec_ref[0, V_G1:V_G1 + 1, :] * nm

    row = pl.BlockSpec((tm, D_MODEL), lambda i: (i, 0))
    return pl.pallas_call(
        body, name="out_proj_fwd", grid=(seq // tm,),
        in_specs=[row, pl.BlockSpec((tm, ATTN_W), lambda i: (i, 0)), pl.BlockSpec((4, tm, LANE), lambda i: (0, i, 0)),
                  pl.BlockSpec((1, 16, D_MODEL), lambda i: (layer, 0, 0)),
                  pl.BlockSpec((1, 8, SSM_W), lambda i: (layer, 0, 0)),
                  pl.BlockSpec((D_MODEL, D_MODEL), lambda i: (0, 0))],
        out_specs=row, out_shape=jax.ShapeDtypeStruct((seq, D_MODEL), F32),
        compiler_params=_params(("parallel",)),
    )(x, attn, s4, vec, vec512, w_out)


def _out_proj_bwd(dx1, attn, s4, vec, vec512, w_out, layer):
    seq = dx1.shape[0]
    tm = PROJ_TILE

    def body(dx1_ref, attn_ref, s4_ref, vec_ref, vec512_ref, w_ref,
             dattn_ref, ds4_ref, heads_ref, dmixed_ref, dvec_ref, dvec512_ref):
        i = pl.program_id(0)

        @pl.when(i == 0)
        def _():
            dvec_ref[...] = jnp.zeros_like(dvec_ref)
            dvec512_ref[...] = jnp.zeros_like(dvec512_ref)

        heads, (ah, ar, ga), (sh, sr, gs) = _heads(attn_ref, s4_ref, vec512_ref)
        hb = heads.astype(BF16)
        heads_ref[...] = hb
        gm = vec_ref[0, V_POST_MIX:V_POST_MIX + 1, :]
        nm, mh, mr = _rms_fwd(_nn(hb, w_ref[...]), gm)
        dx1v = dx1_ref[...]
        dvec_ref[V_G1:V_G1 + 1, :] += _colsum(dx1v * nm)
        dmixed, dgm_rows = _rms_bwd(dx1v * vec_ref[0, V_G1:V_G1 + 1, :], mh, mr, gm)
        dvec_ref[V_POST_MIX:V_POST_MIX + 1, :] += _colsum(dgm_rows)
        dmb = dmixed.astype(BF16)
        dmixed_ref[...] = dmb
        dheads = _nt(dmb, w_ref[...])
        dattn, dga_rows = _rms_bwd(dheads[:, :ATTN_W], ah, ar, ga)
        ds, dgs_rows = _rms_bwd(dheads[:, ATTN_W:], sh, sr, gs)
        dattn_ref[...] = dattn
        for j in range(4):
            ds4_ref[j] = ds[:, j * LANE:(j + 1) * LANE]
        dvec512_ref[H_ATTN_G:H_ATTN_G + 1, :] += _colsum(dga_rows)
        dvec512_ref[H_SSM_G:H_SSM_G + 1, :] += _colsum(dgs_rows)

    row = pl.BlockSpec((tm, D_MODEL), lambda i: (i, 0))
    return pl.pallas_call(
        body, name="out_proj_bwd", grid=(seq // tm,),
        in_specs=[row, pl.BlockSpec((tm, ATTN_W), lambda i: (i, 0)), pl.BlockSpec((4, tm, LANE), lambda i: (0, i, 0)),
                  pl.BlockSpec((1, 16, D_MODEL), lambda i: (layer, 0, 0)),
                  pl.BlockSpec((1, 8, SSM_W), lambda i: (layer, 0, 0)),
                  pl.BlockSpec((D_MODEL, D_MODEL), lambda i: (0, 0))],
        out_specs=[pl.BlockSpec((tm, ATTN_W), lambda i: (i, 0)), pl.BlockSpec((4, tm, LANE), lambda i: (0, i, 0)),
                   row, row, pl.BlockSpec((16, D_MODEL), lambda i: (0, 0)), pl.BlockSpec((8, SSM_W), lambda i: (0, 0))],
        out_shape=[jax.ShapeDtypeStruct((seq, ATTN_W), F32), jax.ShapeDtypeStruct((4, seq, LANE), F32),
                   jax.ShapeDtypeStruct((seq, D_MODEL), BF16), jax.ShapeDtypeStruct((seq, D_MODEL), BF16),
                   jax.ShapeDtypeStruct((16, D_MODEL), F32), jax.ShapeDtypeStruct((8, SSM_W), F32)],
        compiler_params=_params(("arbitrary",)),
    )(dx1, attn, s4, vec, vec512, w_out)


def _mlp_fwd(x1, vec, w_in_t, w_out, layer):
    seq = x1.shape[0]
    tm = ROW_TILE

    def body(x1_ref, vec_ref, wi_hbm, wo_hbm, x2_ref, r_ref, f_ref, wi, wo, sems):
        _load_once(pl.program_id(0), [(wi_hbm, wi), (wo_hbm, wo)], sems)
        x1v = x1_ref[...]
        n, _, _ = _rms_fwd(x1v, vec_ref[0, V_PRE_MLP:V_PRE_MLP + 1, :])
        h = (n * (1.0 + vec_ref[0, V_SC2:V_SC2 + 1, :]) + vec_ref[0, V_SH2:V_SH2 + 1, :]).astype(BF16)
        a = _nt(h, wi[...])
        r = jnp.square(jnp.maximum(a, 0.0)).astype(BF16)
        r_ref[...] = r
        f = _nn(r, wo[...])
        f_ref[...] = f
        nf, _, _ = _rms_fwd(f, vec_ref[0, V_POST_MLP:V_POST_MLP + 1, :])
        x2_ref[...] = x1v + vec_ref[0, V_G2:V_G2 + 1, :] * nf

    row = pl.BlockSpec((tm, D_MODEL), lambda i: (i, 0))
    return pl.pallas_call(
        body, name="mlp_fwd", grid=(seq // tm,),
        in_specs=[row, pl.BlockSpec((1, 16, D_MODEL), lambda i: (layer, 0, 0)), HBM, HBM],
        out_specs=[row, pl.BlockSpec((tm, D_FF), lambda i: (i, 0)), row],
        out_shape=[jax.ShapeDtypeStruct((seq, D_MODEL), F32), jax.ShapeDtypeStruct((seq, D_FF), BF16),
                   jax.ShapeDtypeStruct((seq, D_MODEL), F32)],
        scratch_shapes=[pltpu.VMEM((D_FF, D_MODEL), BF16), pltpu.VMEM((D_FF, D_MODEL), BF16),
                        pltpu.SemaphoreType.DMA((2,))],
        compiler_params=_params(("arbitrary",)),
    )(x1, vec, w_in_t, w_out)


def _mlp_bwd(dx2, x1, r, f, vec, w_in_t, w_out, layer):
    seq = x1.shape[0]
    tm = ROW_TILE

    def body(dx2_ref, x1_ref, r_ref, f_ref, vec_ref, wi_hbm, wo_hbm, dx1_ref, h_ref, da_ref, df_ref, dvec_ref,
             wi, wo, sems):
        i = pl.program_id(0)
        _load_once(i, [(wi_hbm, wi), (wo_hbm, wo)], sems)

        @pl.when(i == 0)
        def _():
            dvec_ref[...] = jnp.zeros_like(dvec_ref)

        g_pre = vec_ref[0, V_PRE_MLP:V_PRE_MLP + 1, :]
        g_post = vec_ref[0, V_POST_MLP:V_POST_MLP + 1, :]
        sc2 = vec_ref[0, V_SC2:V_SC2 + 1, :]
        n, xh, xr = _rms_fwd(x1_ref[...], g_pre)
        h_ref[...] = (n * (1.0 + sc2) + vec_ref[0, V_SH2:V_SH2 + 1, :]).astype(BF16)
        relu = jnp.sqrt(r_ref[...].astype(F32))
        nf, fh, fr = _rms_fwd(f_ref[...], g_post)
        dx2v = dx2_ref[...]
        dvec_ref[V_G2:V_G2 + 1, :] += _colsum(dx2v * nf)
        df, dgp_rows = _rms_bwd(dx2v * vec_ref[0, V_G2:V_G2 + 1, :], fh, fr, g_post)
        dvec_ref[V_POST_MLP:V_POST_MLP + 1, :] += _colsum(dgp_rows)
        dfb = df.astype(BF16)
        df_ref[...] = dfb
        da = (_nt(dfb, wo[...]) * (2.0 * relu)).astype(BF16)
        da_ref[...] = da
        dh = _nn(da, wi[...])
        dvec_ref[V_SH2:V_SH2 + 1, :] += _colsum(dh)
        dvec_ref[V_SC2:V_SC2 + 1, :] += _colsum(dh * n)
        dxn, dg_rows = _rms_bwd(dh * (1.0 + sc2), xh, xr, g_pre)
        dvec_ref[V_PRE_MLP:V_PRE_MLP + 1, :] += _colsum(dg_rows)
        dx1_ref[...] = dx2v + dxn

    row = pl.BlockSpec((tm, D_MODEL), lambda i: (i, 0))
    wide = pl.BlockSpec((tm, D_FF), lambda i: (i, 0))
    return pl.pallas_call(
        body, name="mlp_bwd", grid=(seq // tm,),
        in_specs=[row, row, wide, row, pl.BlockSpec((1, 16, D_MODEL), lambda i: (layer, 0, 0)), HBM, HBM],
        out_specs=[row, row, wide, row, pl.BlockSpec((16, D_MODEL), lambda i: (0, 0))],
        out_shape=[jax.ShapeDtypeStruct((seq, D_MODEL), F32), jax.ShapeDtypeStruct((seq, D_MODEL), BF16),
                   jax.ShapeDtypeStruct((seq, D_FF), BF16),
                   jax.ShapeDtypeStruct((seq, D_MODEL), BF16), jax.ShapeDtypeStruct((16, D_MODEL), F32)],
        scratch_shapes=[pltpu.VMEM((D_FF, D_MODEL), BF16), pltpu.VMEM((D_FF, D_MODEL), BF16),
                        pltpu.SemaphoreType.DMA((2,))],
        compiler_params=_params(("arbitrary",)),
    )(dx2, x1, r, f, vec, w_in_t, w_out)


def _loss_head(y, target):
    seq = y.shape[0]
    tm = ROW_TILE

    def body(y_ref, t_ref, dy_ref, part_ref):
        e = y_ref[...] - t_ref[...]
        dy_ref[...] = e * (1.0 / D_MODEL)
        tot = jnp.sum(jnp.sum(e * e, axis=1, keepdims=True), axis=0, keepdims=True) * (0.5 / D_MODEL)
        part_ref[0] = jnp.broadcast_to(tot, (SUBLANE, LANE))

    row = pl.BlockSpec((tm, D_MODEL), lambda i: (i, 0))
    return pl.pallas_call(
        body, name="loss_head", grid=(seq // tm,),
        in_specs=[row, row],
        out_specs=[row, pl.BlockSpec((1, SUBLANE, LANE), lambda i: (i, 0, 0))],
        out_shape=[jax.ShapeDtypeStruct((seq, D_MODEL), F32), jax.ShapeDtypeStruct((seq // tm, SUBLANE, LANE), F32)],
        compiler_params=_params(("parallel",)),
    )(y, target)


def _matmul_tn(a, b, out_dtype, name, pieces=1):
    kk, m = a.shape
    n = b.shape[1]
    tm = min(m, 512)
    tn = n // pieces if pieces > 1 else min(n, 1280)
    tk = min(kk, 2048)
    nk = kk // tk

    def body(a_ref, b_ref, o_ref, acc):
        k = pl.program_id(2)

        @pl.when(k == 0)
        def _():
            acc[...] = jnp.zeros_like(acc)

        acc[...] += _tn(a_ref[...], b_ref[...])

        @pl.when(k == nk - 1)
        def _():
            if pieces > 1:
                o_ref[0] = acc[...].astype(out_dtype)
            else:
                o_ref[...] = acc[...].astype(out_dtype)

    if pieces > 1:
        out_spec = pl.BlockSpec((1, tm, tn), lambda i, j, k: (j, i, 0))
        out_shape = jax.ShapeDtypeStruct((pieces, m, tn), out_dtype)
    else:
        out_spec = pl.BlockSpec((tm, tn), lambda i, j, k: (i, j))
        out_shape = jax.ShapeDtypeStruct((m, n), out_dtype)
    return pl.pallas_call(
        body, name=name, grid=(m // tm, n // tn, nk),
        in_specs=[pl.BlockSpec((tk, tm), lambda i, j, k: (k, i)), pl.BlockSpec((tk, tn), lambda i, j, k: (k, j))],
        out_specs=out_spec, out_shape=out_shape,
        scratch_shapes=[pltpu.VMEM((tm, tn), F32)],
        compiler_params=_params(("parallel", "parallel", "arbitrary")),
    )(a, b)


def _attn_probs(i, qk, slope, sink):
    rr = lax.broadcasted_iota(jnp.int32, (BLOCK, 2 * BLOCK), 0)
    jj = lax.broadcasted_iota(jnp.int32, (BLOCK, 2 * BLOCK), 1)
    diff = BLOCK + rr - jj
    valid = (diff >= 0) & (diff < WINDOW) & ((jj >= BLOCK) | (i > 0))
    s = qk * (HEAD_DIM ** -0.5)
    s = jnp.where(valid, s - slope * diff.astype(F32), NEG_INF)
    m = jnp.maximum(jnp.max(s, axis=1, keepdims=True), sink)
    p = jnp.exp(s - m)
    ps = jnp.exp(sink - m)
    inv = 1.0 / (jnp.sum(p, axis=1, keepdims=True) + ps)
    return p * inv, ps * inv


def _bands(kvp, kvc, h):
    kband = jnp.concatenate([kvp[:, h * HEAD_DIM:(h + 1) * HEAD_DIM], kvc[:, h * HEAD_DIM:(h + 1) * HEAD_DIM]], axis=0)
    v0 = KV_W + h * HEAD_DIM
    vband = jnp.concatenate([kvp[:, v0:v0 + HEAD_DIM], kvc[:, v0:v0 + HEAD_DIM]], axis=0)
    return kband, vband


def _attn_fwd(q, kv, sinks):
    seq = q.shape[0]
    nb = seq // BLOCK
    slopes = _alibi_slopes()

    def body(sink_ref, q_ref, kvp_ref, kvc_ref, o_ref):
        i = pl.program_id(0)
        qv, kvp, kvc = q_ref[...], kvp_ref[...], kvc_ref[...]
        bands = [_bands(kvp, kvc, h) for h in range(N_KV)]
        cols = [slice(hq * HEAD_DIM, (hq + 1) * HEAD_DIM) for hq in range(N_Q)]
        qk = [_nt(qv[:, cols[hq]], bands[hq // Q_PER_KV][0]) for hq in range(N_Q)]
        pr = [_attn_probs(i, qk[hq], slopes[hq], sink_ref[hq])[0].astype(BF16) for hq in range(N_Q)]
        for hq in range(N_Q):
            o_ref[:, cols[hq]] = _nn(pr[hq], bands[hq // Q_PER_KV][1])

    return pl.pallas_call(
        body, name="attn_fwd", grid=(nb,),
        in_specs=[pl.BlockSpec(memory_space=pltpu.SMEM),
                  pl.BlockSpec((BLOCK, ATTN_W), lambda i: (i, 0)),
                  pl.BlockSpec((BLOCK, 2 * KV_W), lambda i: (jnp.maximum(i - 1, 0), 0)),
                  pl.BlockSpec((BLOCK, 2 * KV_W), lambda i: (i, 0))],
        out_specs=pl.BlockSpec((BLOCK, ATTN_W), lambda i: (i, 0)),
        out_shape=jax.ShapeDtypeStruct((seq, ATTN_W), F32),
        compiler_params=_params(("parallel",)),
    )(sinks, q, kv, kv)


def _attn_bwd(q, kv, sinks, dout):
    seq = q.shape[0]
    nb = seq // BLOCK
    slopes = _alibi_slopes()
    scale = HEAD_DIM ** -0.5

    def body(sink_ref, q_ref, kvp_ref, kvc_ref, do_ref, dq_ref, dkv_ref, dsk_ref, prev):
        step = pl.program_id(0)
        i = nb - 1 - step

        @pl.when(step == 0)
        def _():
            prev[...] = jnp.zeros_like(prev)

        qv, kvp, kvc = q_ref[...], kvp_ref[...], kvc_ref[...]
        dov = do_ref[...].astype(BF16)
        bands = [_bands(kvp, kvc, h) for h in range(N_KV)]
        cols = [slice(hq * HEAD_DIM, (hq + 1) * HEAD_DIM) for hq in range(N_Q)]
        qk = [_nt(qv[:, cols[hq]], bands[hq // Q_PER_KV][0]) for hq in range(N_Q)]
        dp = [_nt(dov[:, cols[hq]], bands[hq // Q_PER_KV][1]) for hq in range(N_Q)]
        prb, dsb, dsk = [], [], []
        for hq in range(N_Q):
            pr, ps = _attn_probs(i, qk[hq], slopes[hq], sink_ref[hq])
            delta = jnp.sum(pr * dp[hq], axis=1, keepdims=True)
            dsb.append((pr * (dp[hq] - delta) * scale).astype(BF16))
            prb.append(pr.astype(BF16))
            dsk.append(jnp.broadcast_to(-_colsum(ps * delta), (1, LANE)))
        for hq in range(N_Q):
            dq_ref[:, cols[hq]] = _nn(dsb[hq], bands[hq // Q_PER_KV][0])
        dk, dv = [], []
        for h in range(N_KV):
            heads = range(h * Q_PER_KV, (h + 1) * Q_PER_KV)
            dk.append(sum(_tn(dsb[hq], qv[:, cols[hq]]) for hq in heads))
            dv.append(sum(_tn(prb[hq], dov[:, cols[hq]]) for hq in heads))
        band = jnp.concatenate(dk + dv, axis=1)
        dkv_ref[...] = band[BLOCK:, :] + prev[...]
        prev[...] = band[:BLOCK, :]
        dsk_ref[0] = jnp.concatenate(dsk, axis=0)

    return pl.pallas_call(
        body, name="attn_bwd", grid=(nb,),
        in_specs=[pl.BlockSpec(memory_space=pltpu.SMEM),
                  pl.BlockSpec((BLOCK, ATTN_W), lambda s: (nb - 1 - s, 0)),
                  pl.BlockSpec((BLOCK, 2 * KV_W), lambda s: (jnp.maximum(nb - 2 - s, 0), 0)),
                  pl.BlockSpec((BLOCK, 2 * KV_W), lambda s: (nb - 1 - s, 0)),
                  pl.BlockSpec((BLOCK, ATTN_W), lambda s: (nb - 1 - s, 0))],
        out_specs=[pl.BlockSpec((BLOCK, ATTN_W), lambda s: (nb - 1 - s, 0)),
                   pl.BlockSpec((BLOCK, 2 * KV_W), lambda s: (nb - 1 - s, 0)),
                   pl.BlockSpec((1, N_Q, LANE), lambda s: (nb - 1 - s, 0, 0))],
        out_shape=[jax.ShapeDtypeStruct((seq, ATTN_W), F32), jax.ShapeDtypeStruct((seq, 2 * KV_W), F32),
                   jax.ShapeDtypeStruct((nb, N_Q, LANE), F32)],
        scratch_shapes=[pltpu.VMEM((BLOCK, 2 * KV_W), F32)],
        compiler_params=_params(("arbitrary",)),
    )(sinks, q, kv, kv, dout)


def _discretize(lr, li, ldt, br, bi):
    dt = jnp.exp(ldt)
    mag = jnp.exp(lr * dt)
    ang = li * dt
    ab_r = mag * jnp.cos(ang)
    ab_i = mag * jnp.sin(ang)
    nr = ab_r - 1.0
    ni = ab_i
    den = lr * lr + li * li
    f_r = (nr * lr + ni * li) / den
    f_i = (ni * lr - nr * li) / den
    return ab_r, ab_i, f_r * br - f_i * bi, f_r * bi + f_i * br


def _ssm_prepare(lr, li, ldt, br, bi):
    n = lr.shape[0]
    tn = N_CH
    col = pl.BlockSpec((tn, 1), lambda i: (i, 0))
    mat = pl.BlockSpec((tn, GROUP_W), lambda i: (i, 0))

    def body(lr_ref, li_ref, ldt_ref, br_ref, bi_ref, ar_ref, ai_ref, bbr_ref, bbi_ref):
        ar, ai, bbr, bbi = _discretize(lr_ref[...], li_ref[...], ldt_ref[...], br_ref[...], bi_ref[...])
        ar_ref[...] = ar
        ai_ref[...] = ai
        bbr_ref[...] = bbr
        bbi_ref[...] = bbi

    cs = jax.ShapeDtypeStruct((n, 1), F32)
    ms = jax.ShapeDtypeStruct((n, GROUP_W), F32)
    return pl.pallas_call(
        body, name="ssm_prepare", grid=(n // tn,),
        in_specs=[col, col, col, mat, mat], out_specs=[col, col, mat, mat], out_shape=[cs, cs, ms, ms],
        compiler_params=_params(("parallel",)),
    )(lr, li, ldt, br, bi)


def _ssm_prepare_bwd(lr, li, ldt, br, bi, dar, dai, dbbr, dbbi):
    n = lr.shape[0]
    tn = N_CH
    col = pl.BlockSpec((tn, 1), lambda i: (i, 0))
    mat = pl.BlockSpec((tn, GROUP_W), lambda i: (i, 0))

    def body(lr_ref, li_ref, ldt_ref, br_ref, bi_ref, dar_ref, dai_ref, dbbr_ref, dbbi_ref,
             dlr_ref, dli_ref, dldt_ref, dbr_ref, dbi_ref):
        _, vjp = jax.vjp(_discretize, lr_ref[...], li_ref[...], ldt_ref[...], br_ref[...], bi_ref[...])
        dlr, dli, dldt, dbr, dbi = vjp((dar_ref[...], dai_ref[...], dbbr_ref[...], dbbi_ref[...]))
        dlr_ref[...] = dlr
        dli_ref[...] = dli
        dldt_ref[...] = dldt
        dbr_ref[...] = dbr
        dbi_ref[...] = dbi

    cs = jax.ShapeDtypeStruct((n, 1), F32)
    ms = jax.ShapeDtypeStruct((n, GROUP_W), F32)
    return pl.pallas_call(
        body, name="ssm_prepare_bwd", grid=(n // tn,),
        in_specs=[col, col, col, mat, mat, col, col, mat, mat],
        out_specs=[col, col, col, mat, mat], out_shape=[cs, cs, cs, ms, ms],
        compiler_params=_params(("parallel",)),
    )(lr, li, ldt, br, bi, dar, dai, dbbr, dbbi)


def _load_slabs(src4_ref, dst):
    for s in range(STEPS):
        dst[s * SEGS:(s + 1) * SEGS, :] = jnp.concatenate(
            [src4_ref[j, pl.ds(s, SEGS, stride=STEPS), :] for j in range(4)], axis=1)


def _store_slabs(src, dst4_ref):
    for s in range(STEPS):
        for j in range(4):
            dst4_ref[j, pl.ds(s, SEGS, stride=STEPS), :] = src[s * SEGS:(s + 1) * SEGS, j * LANE:(j + 1) * LANE]


def _power_table(ar_ref, ai_ref, pwr, pwi):
    ar, ai = ar_ref[0], ai_ref[0]
    pr, pi = ar, ai
    pwr[0:1, :] = pr
    pwi[0:1, :] = pi
    for k in range(1, STEPS):
        pr, pi = pr * ar - pi * ai, pr * ai + pi * ar
        pwr[k:k + 1, :] = pr
        pwi[k:k + 1, :] = pi


def _scan_states(ubf, ar_ref, ai_ref, bbr_ref, bbi_ref, pwr, pwi, cin_r, cin_i, hr, hi):
    for k in range(2):
        rows = slice(k * 256, (k + 1) * 256)
        cols = slice(k * HALF_CH, (k + 1) * HALF_CH)
        hr[:, cols] = _nn(ubf[:, rows], bbr_ref[0, k])
        hi[:, cols] = _nn(ubf[:, rows], bbi_ref[0, k])
    for st in range(N_CH // STRIP):
        cs = slice(st * STRIP, (st + 1) * STRIP)
        arb = jnp.broadcast_to(ar_ref[0, :, cs], (SEGS, STRIP))
        aib = jnp.broadcast_to(ai_ref[0, :, cs], (SEGS, STRIP))

        def step(s, carry, cs=cs, arb=arb, aib=aib):
            cr, ci = carry
            rows = pl.ds(pl.multiple_of(s * SEGS, SEGS), SEGS)
            nr = arb * cr - aib * ci + hr[rows, cs]
            ni = arb * ci + aib * cr + hi[rows, cs]
            hr[rows, cs] = nr
            hi[rows, cs] = ni
            return nr, ni

        zero = jnp.zeros((SEGS, STRIP), F32)
        lax.fori_loop(0, STEPS, step, (zero, zero), unroll=True)
    last = slice((STEPS - 1) * SEGS, STEPS * SEGS)
    end_r, end_i = hr[last, :], hi[last, :]
    a64r, a64i = pwr[STEPS - 1:STEPS, :], pwi[STEPS - 1:STEPS, :]
    cr, ci = cin_r, cin_i
    rows_r, rows_i = [], []
    for j in range(SEGS):
        rows_r.append(cr)
        rows_i.append(ci)
        cr, ci = (a64r * cr - a64i * ci + end_r[j:j + 1, :], a64r * ci + a64i * cr + end_i[j:j + 1, :])
    cm_r, cm_i = jnp.concatenate(rows_r, axis=0), jnp.concatenate(rows_i, axis=0)
    for st in range(N_CH // STRIP):
        cs = slice(st * STRIP, (st + 1) * STRIP)
        cmr, cmi = cm_r[:, cs], cm_i[:, cs]

        def fix(s, carry, cs=cs, cmr=cmr, cmi=cmi):
            rows = pl.ds(pl.multiple_of(s * SEGS, SEGS), SEGS)
            pr, pi = pwr[pl.ds(s, 1), cs], pwi[pl.ds(s, 1), cs]
            hr[rows, cs] = hr[rows, cs] + (pr * cmr - pi * cmi)
            hi[rows, cs] = hi[rows, cs] + (pr * cmi + pi * cmr)
            return carry

        lax.fori_loop(0, STEPS, fix, 0, unroll=True)
    return (cm_r, cm_i), (cr, ci)


def _ssm_outputs(u, hr, hi, crt_ref, cit_ref, vec512_ref, wg_ref):
    ys = []
    for k in range(2):
        cols = slice(k * HALF_CH, (k + 1) * HALF_CH)
        ys.append(_nn(hr[:, cols].astype(BF16), crt_ref[0, k]) - _nn(hi[:, cols].astype(BF16), cit_ref[0, k]))
    y = jnp.concatenate(ys, axis=1) + vec512_ref[0, H_DSKIP:H_DSKIP + 1, :] * u
    z, t = _gelu(y)
    gate = jax.nn.sigmoid(_nn(z.astype(BF16), wg_ref[...]) + vec512_ref[0, H_BGLU:H_BGLU + 1, :])
    return y, z, t, gate


def _ssm_specs(layer, nck, rev):
    def chunk(i):
        return nck - 1 - i if rev else i

    return [pl.BlockSpec((4, CHUNK, LANE), lambda i: (0, chunk(i), 0)),
            pl.BlockSpec((1, 1, N_CH), lambda i: (layer, 0, 0)),
            pl.BlockSpec((1, 1, N_CH), lambda i: (layer, 0, 0)),
            pl.BlockSpec((1, 2, 256, HALF_CH), lambda i: (layer, 0, 0, 0)),
            pl.BlockSpec((1, 2, 256, HALF_CH), lambda i: (layer, 0, 0, 0)),
            pl.BlockSpec((1, 2, HALF_CH, 256), lambda i: (layer, 0, 0, 0)),
            pl.BlockSpec((1, 2, HALF_CH, 256), lambda i: (layer, 0, 0, 0)),
            pl.BlockSpec((1, 8, SSM_W), lambda i: (layer, 0, 0)),
            pl.BlockSpec((SSM_W, SSM_W), lambda i: (0, 0))]


def _ssm_fwd(u4, a_r, a_i, bb_r, bb_i, c_rt, c_it, vec512, w_glu, layer):
    seq = u4.shape[1]
    nck = seq // CHUNK

    def body(u4_ref, ar_ref, ai_ref, bbr_ref, bbi_ref, crt_ref, cit_ref, vec512_ref, wg_ref,
             s4_ref, cmr_ref, cmi_ref, hr, hi, pwr, pwi, car, cai, ubuf, obuf):
        i = pl.program_id(0)

        @pl.when(i == 0)
        def _():
            car[...] = jnp.zeros_like(car)
            cai[...] = jnp.zeros_like(cai)
            _power_table(ar_ref, ai_ref, pwr, pwi)

        _load_slabs(u4_ref, ubuf)
        u = ubuf[...]
        (cm_r, cm_i), (er, ei) = _scan_states(u.astype(BF16), ar_ref, ai_ref, bbr_ref, bbi_ref, pwr, pwi,
                                              car[...], cai[...], hr, hi)
        cmr_ref[0] = cm_r
        cmi_ref[0] = cm_i
        car[...] = er
        cai[...] = ei
        _, z, _, gate = _ssm_outputs(u, hr, hi, crt_ref, cit_ref, vec512_ref, wg_ref)
        obuf[...] = z * gate
        _store_slabs(obuf, s4_ref)

    return pl.pallas_call(
        body, name="ssm_fwd", grid=(nck,),
        in_specs=_ssm_specs(layer, nck, False),
        out_specs=[pl.BlockSpec((4, CHUNK, LANE), lambda i: (0, i, 0)),
                   pl.BlockSpec((1, SEGS, N_CH), lambda i: (i, 0, 0)),
                   pl.BlockSpec((1, SEGS, N_CH), lambda i: (i, 0, 0)),
                   pl.BlockSpec((CHUNK, N_CH), lambda i: (i, 0)),
                   pl.BlockSpec((CHUNK, N_CH), lambda i: (i, 0))],
        out_shape=[jax.ShapeDtypeStruct((4, seq, LANE), F32), jax.ShapeDtypeStruct((nck, SEGS, N_CH), F32),
                   jax.ShapeDtypeStruct((nck, SEGS, N_CH), F32), jax.ShapeDtypeStruct((seq, N_CH), F32),
                   jax.ShapeDtypeStruct((seq, N_CH), F32)],
        scratch_shapes=[pltpu.VMEM((STEPS, N_CH), F32), pltpu.VMEM((STEPS, N_CH), F32),
                        pltpu.VMEM((1, N_CH), F32), pltpu.VMEM((1, N_CH), F32),
                        pltpu.VMEM((CHUNK, SSM_W), F32), pltpu.VMEM((CHUNK, SSM_W), F32)],
        compiler_params=_params(("arbitrary",)),
    )(u4, a_r, a_i, bb_r, bb_i, c_rt, c_it, vec512, w_glu)


def _ssm_bwd(u4, ds4, cm_r, cm_i, h_r, h_i, a_r, a_i, bb_r, bb_i, c_rt, c_it, vec512, w_glu, layer):
    seq = u4.shape[1]
    nck = seq // CHUNK

    def body(u4_ref, ar_ref, ai_ref, bbr_ref, bbi_ref, crt_ref, cit_ref, vec512_ref, wg_ref, ds4_ref, cmr_ref, cmi_ref,
             hr, hi, du4_ref, dbbr_out, dbbi_out, dcrt_out, dcit_out, dar_ref, dai_ref, dwg_ref, dvec_ref,
             gr, gi, pwr, pwi, gcr, gci, accr, acci, ubuf, dbuf, dbbr_ref, dbbi_ref, dcrt_ref, dcit_ref):
        i = pl.program_id(0)

        @pl.when(i == 0)
        def _():
            for ref in (gcr, gci, accr, acci, dbbr_ref, dbbi_ref, dcrt_ref, dcit_ref, dwg_ref, dvec_ref):
                ref[...] = jnp.zeros_like(ref)
            _power_table(ar_ref, ai_ref, pwr, pwi)

        _load_slabs(u4_ref, ubuf)
        u = ubuf[...]
        ubf = u.astype(BF16)
        cm_r, cm_i = cmr_ref[0], cmi_ref[0]
        y, z, t, gate = _ssm_outputs(u, hr, hi, crt_ref, cit_ref, vec512_ref, wg_ref)
        _load_slabs(ds4_ref, dbuf)
        ds = dbuf[...]
        da = ds * z * gate * (1.0 - gate)
        dab = da.astype(BF16)
        dz = ds * gate + _nt(dab, wg_ref[...])
        dwg_ref[...] += _tn(z.astype(BF16), dab)
        dvec_ref[H_BGLU:H_BGLU + 1, :] += _colsum(da)
        dy = dz * _gelu_grad(y, t)
        dvec_ref[H_DSKIP:H_DSKIP + 1, :] += _colsum(dy * u)
        du_skip = dy * vec512_ref[0, H_DSKIP:H_DSKIP + 1, :]
        dyb = dy.astype(BF16)
        for k in range(2):
            rows = slice(k * 256, (k + 1) * 256)
            cols = slice(k * HALF_CH, (k + 1) * HALF_CH)
            dcrt_ref[k] += _tn(hr[:, cols].astype(BF16), dyb[:, rows])
            dcit_ref[k] -= _tn(hi[:, cols].astype(BF16), dyb[:, rows])
            gr[:, cols] = _nt(dyb[:, rows], crt_ref[0, k])
            gi[:, cols] = -_nt(dyb[:, rows], cit_ref[0, k])
        for st in range(N_CH // STRIP):
            cs = slice(st * STRIP, (st + 1) * STRIP)
            arb = jnp.broadcast_to(ar_ref[0, :, cs], (SEGS, STRIP))
            aib = jnp.broadcast_to(ai_ref[0, :, cs], (SEGS, STRIP))

            def step(k, carry, cs=cs, arb=arb, aib=aib):
                cr, ci = carry
                rows = pl.ds(pl.multiple_of((STEPS - 1 - k) * SEGS, SEGS), SEGS)
                nr = gr[rows, cs] + (arb * cr + aib * ci)
                ni = gi[rows, cs] + (arb * ci - aib * cr)
                gr[rows, cs] = nr
                gi[rows, cs] = ni
                return nr, ni

            zero = jnp.zeros((SEGS, STRIP), F32)
            lax.fori_loop(0, STEPS, step, (zero, zero), unroll=True)
        first_r, first_i = gr[0:SEGS, :], gi[0:SEGS, :]
        a64r, a64i = pwr[STEPS - 1:STEPS, :], pwi[STEPS - 1:STEPS, :]
        dr_, di_ = gcr[...], gci[...]
        rows_r, rows_i = [None] * SEGS, [None] * SEGS
        for j in range(SEGS - 1, -1, -1):
            rows_r[j], rows_i[j] = dr_, di_
            dr_, di_ = (first_r[j:j + 1, :] + (a64r * dr_ + a64i * di_), first_i[j:j + 1, :] + (a64r * di_ - a64i * dr_))
        gcr[...] = dr_
        gci[...] = di_
        dm_r, dm_i = jnp.concatenate(rows_r, axis=0), jnp.concatenate(rows_i, axis=0)
        for st in range(N_CH // STRIP):
            cs = slice(st * STRIP, (st + 1) * STRIP)
            dmr, dmi = dm_r[:, cs], dm_i[:, cs]

            def fixed(s, cs=cs, dmr=dmr, dmi=dmi):
                rows = pl.ds(pl.multiple_of(s * SEGS, SEGS), SEGS)
                pr, pi = pwr[pl.ds(STEPS - 1 - s, 1), cs], pwi[pl.ds(STEPS - 1 - s, 1), cs]
                g_r = gr[rows, cs] + (pr * dmr + pi * dmi)
                g_i = gi[rows, cs] + (pr * dmi - pi * dmr)
                gr[rows, cs] = g_r
                gi[rows, cs] = g_i
                return g_r, g_i

            g_r, g_i = fixed(jnp.int32(0))
            acc0 = (g_r * cm_r[:, cs] + g_i * cm_i[:, cs], g_i * cm_r[:, cs] - g_r * cm_i[:, cs])

            def step(s, carry, cs=cs, fixed=fixed):
                sr, si = carry
                g_r, g_i = fixed(s)
                prev = pl.ds(pl.multiple_of((s - 1) * SEGS, SEGS), SEGS)
                hpr, hpi = hr[prev, cs], hi[prev, cs]
                return sr + (g_r * hpr + g_i * hpi), si + (g_i * hpr - g_r * hpi)

            sr, si = lax.fori_loop(1, STEPS, step, acc0, unroll=True)
            accr[:, cs] += sr
            acci[:, cs] += si
        grb, gib = gr[...].astype(BF16), gi[...].astype(BF16)
        dus = []
        for k in range(2):
            rows = slice(k * 256, (k + 1) * 256)
            cols = slice(k * HALF_CH, (k + 1) * HALF_CH)
            dus.append(_nt(grb[:, cols], bbr_ref[0, k]) + _nt(gib[:, cols], bbi_ref[0, k]))
            dbbr_ref[k] += _tn(ubf[:, rows], grb[:, cols])
            dbbi_ref[k] += _tn(ubf[:, rows], gib[:, cols])
        dbuf[...] = jnp.concatenate(dus, axis=1) + du_skip
        _store_slabs(dbuf, du4_ref)

        @pl.when(i == nck - 1)
        def _():
            dar_ref[...] = _colsum(accr[...])
            dai_ref[...] = _colsum(acci[...])
            ng = N_GROUPS // 2
            for k in range(2):
                for acc, out in ((dbbr_ref, dbbr_out), (dbbi_ref, dbbi_out)):
                    out[k] = jnp.concatenate(
                        [acc[k, g * GROUP_W:(g + 1) * GROUP_W, g * STATE:(g + 1) * STATE] for g in range(ng)], axis=0)
                for acc, out in ((dcrt_ref, dcrt_out), (dcit_ref, dcit_out)):
                    out[k] = jnp.concatenate(
                        [acc[k, g * STATE:(g + 1) * STATE, g * GROUP_W:(g + 1) * GROUP_W] for g in range(ng)], axis=0)

    rev4 = pl.BlockSpec((4, CHUNK, LANE), lambda i: (0, nck - 1 - i, 0))
    hc_spec = pl.BlockSpec((1, SEGS, N_CH), lambda i: (nck - 1 - i, 0, 0))
    h_spec = pl.BlockSpec((CHUNK, N_CH), lambda i: (nck - 1 - i, 0))
    fixed2 = lambda shape: pl.BlockSpec(shape, lambda i: (0,) * len(shape))
    return pl.pallas_call(
        body, name="ssm_bwd", grid=(nck,),
        in_specs=_ssm_specs(layer, nck, True) + [rev4, hc_spec, hc_spec, h_spec, h_spec],
        out_specs=[rev4, fixed2((2, 256, STATE)), fixed2((2, 256, STATE)), fixed2((2, HALF_CH, GROUP_W)),
                   fixed2((2, HALF_CH, GROUP_W)), fixed2((1, N_CH)), fixed2((1, N_CH)), fixed2((SSM_W, SSM_W)),
                   fixed2((8, SSM_W))],
        out_shape=[jax.ShapeDtypeStruct((4, seq, LANE), F32),
                   jax.ShapeDtypeStruct((2, 256, STATE), F32), jax.ShapeDtypeStruct((2, 256, STATE), F32),
                   jax.ShapeDtypeStruct((2, HALF_CH, GROUP_W), F32), jax.ShapeDtypeStruct((2, HALF_CH, GROUP_W), F32),
                   jax.ShapeDtypeStruct((1, N_CH), F32), jax.ShapeDtypeStruct((1, N_CH), F32),
                   jax.ShapeDtypeStruct((SSM_W, SSM_W), F32), jax.ShapeDtypeStruct((8, SSM_W), F32)],
        scratch_shapes=[pltpu.VMEM((CHUNK, N_CH), F32), pltpu.VMEM((CHUNK, N_CH), F32),
                        pltpu.VMEM((STEPS, N_CH), F32), pltpu.VMEM((STEPS, N_CH), F32),
                        pltpu.VMEM((1, N_CH), F32), pltpu.VMEM((1, N_CH), F32),
                        pltpu.VMEM((SEGS, N_CH), F32), pltpu.VMEM((SEGS, N_CH), F32),
                        pltpu.VMEM((CHUNK, SSM_W), F32), pltpu.VMEM((CHUNK, SSM_W), F32),
                        pltpu.VMEM((2, 256, HALF_CH), F32), pltpu.VMEM((2, 256, HALF_CH), F32),
                        pltpu.VMEM((2, HALF_CH, 256), F32), pltpu.VMEM((2, HALF_CH, 256), F32)],
        compiler_params=_params(("arbitrary",)),
    )(u4, a_r, a_i, bb_r, bb_i, c_rt, c_it, vec512, w_glu, ds4, cm_r, cm_i, h_r, h_i)


def _block_diag(t):
    nl, _, ng, a, b = t.shape
    eye = jnp.eye(ng, dtype=t.dtype)
    return jnp.einsum("gh,lkgab->lkgahb", eye, t).reshape(nl, 2, ng * a, ng * b)


def _local_step(x, loss_target, mod, p, comm):
    nl = mod.shape[0]
    pad1024 = jnp.zeros((nl, 16 - 10, D_MODEL), F32)
    vec = jnp.concatenate([mod.reshape(nl, N_MOD, D_MODEL), p["pre_mix_g"][:, None], p["post_mix_g"][:, None],
                           p["pre_mlp_g"][:, None], p["post_mlp_g"][:, None], pad1024], axis=1)
    vec512 = jnp.concatenate([p["attn_out_g"][:, None], p["ssm_out_g"][:, None], p["d_skip"][:, None],
                              p["b_glu"][:, None], jnp.zeros((nl, 4, SSM_W), F32)], axis=1)
    n_all = nl * N_CH
    lr = p["lam_re"].reshape(n_all, 1)
    li = p["lam_im"].reshape(n_all, 1)
    ldt = jnp.broadcast_to(p["log_dt"][:, :, None], (nl, N_GROUPS, STATE)).reshape(n_all, 1)
    br = p["b_re"].reshape(n_all, GROUP_W)
    bi = p["b_im"].reshape(n_all, GROUP_W)
    ab_r, ab_i, bb_r, bb_i = _ssm_prepare(lr, li, ldt, br, bi)
    a_r = ab_r.reshape(nl, 1, N_CH)
    a_i = ab_i.reshape(nl, 1, N_CH)

    def dense_b(bb):
        return _block_diag(bb.reshape(nl, 2, 16, STATE, GROUP_W).transpose(0, 1, 2, 4, 3)).astype(BF16)

    def dense_c(cc):
        return _block_diag(cc.reshape(nl, 2, 16, GROUP_W, STATE).transpose(0, 1, 2, 4, 3)).astype(BF16)

    bbr_d, bbi_d = dense_b(bb_r), dense_b(bb_i)
    crt_d, cit_d = dense_c(p["c_re"]), dense_c(p["c_im"])

    saved = []
    xl = x
    mixer_w, mlp_w = [None] * nl, [None] * nl
    for l in range(nl):
        mixer_w[l], tok = comm.mixer_weights(l, [xl, bbr_d, bbi_d, crt_d, cit_d] if l == 0 else xl)
        w_in_t, w_glu, w_out = mixer_w[l]
        q, kv, u4, h1 = _in_proj_fwd(xl, _after(vec, *tok), w_in_t, l)
        attn = _attn_fwd(q, kv, p["attn_sinks"][l])
        s4, *states = _ssm_fwd(u4, a_r, a_i, bbr_d, bbi_d, crt_d, cit_d, vec512, w_glu, l)
        x1 = _out_proj_fwd(xl, attn, s4, vec, vec512, w_out, l)
        mlp_w[l] = comm.mlp_weights(l, x1)
        x2, r, f = _mlp_fwd(x1, vec, mlp_w[l][0], mlp_w[l][1], l)
        saved.append((xl, q, kv, u4, h1, attn, s4, states, x1, r, f))
        xl = x2

    dx, loss_parts = _loss_head(xl, loss_target)
    loss = jnp.sum(loss_parts[:, 0, 0])

    dvec_l, dvec512_l, dsink_l = [None] * nl, [None] * nl, [None] * nl
    dab_r, dab_i, dbb_r, dbb_i, dc_re, dc_im = ([None] * nl for _ in range(6))
    toks = []
    for l in range(nl - 1, -1, -1):
        xl, q, kv, u4, h1, attn, s4, states, x1, r, f = saved[l]
        w_in_t, w_glu, w_out = mixer_w[l]
        dx1, h2, da, df, dvec_m = _mlp_bwd(dx, x1, r, f, _after(vec, *toks), mlp_w[l][0], mlp_w[l][1], l)
        toks = comm.after_mlp_bwd(l, dx1)
        dw_mlp_out = _matmul_tn(r, df, BF16, "dw_mlp_out").reshape(4, D_FF // 4, D_MODEL)
        dw_mlp_in = _matmul_tn(h2, da, BF16, "dw_mlp_in", pieces=4)
        toks = toks + comm.mlp_grads(l, [dw_mlp_in, dw_mlp_out])
        dattn, ds4, heads, dmixed, dvec_o, dvec512_o = _out_proj_bwd(
            dx1, attn, s4, _after(vec, *toks), vec512, w_out, l)
        dw_out = _matmul_tn(heads, dmixed, BF16, "dw_out").reshape(4, D_MODEL // 4, D_MODEL)
        dq, dkv, dsk = _attn_bwd(q, kv, p["attn_sinks"][l], dattn)
        (du4, dbbr, dbbi, dcrt, dcit, dar, dai, dwg, dvec512_s) = _ssm_bwd(
            u4, ds4, *states, a_r, a_i, bbr_d, bbi_d, crt_d, cit_d, vec512, w_glu, l)
        dw_glu = dwg.astype(BF16).reshape(4, SSM_W // 4, SSM_W)
        dx, dproj, dvec_i = _in_proj_bwd(dx1, dq, dkv, du4, xl, vec, w_in_t, l)
        toks = comm.after_in_proj_bwd(l, dx)
        dw_in = _matmul_tn(h1, dproj, BF16, "dw_in")
        toks = toks + comm.mixer_grads(l, [dw_in.reshape(D_MODEL, 4, IN_W // 4).transpose(1, 0, 2), dw_glu, dw_out])
        dvec_l[l] = dvec_m + dvec_o + dvec_i
        dvec512_l[l] = dvec512_o + dvec512_s
        dsink_l[l] = jnp.sum(dsk[:, :, 0], axis=0)
        dab_r[l], dab_i[l], dbb_r[l], dbb_i[l], dc_re[l], dc_im[l] = dar, dai, dbbr, dbbi, dcrt, dcit

    dvec = _after(jnp.stack(dvec_l), *toks)
    dvec512 = jnp.stack(dvec512_l)
    ng = N_GROUPS // 2

    def b_cols(d):
        return jnp.stack(d).reshape(nl, 2, ng, GROUP_W, STATE).transpose(0, 1, 2, 4, 3).reshape(n_all, GROUP_W)

    def c_param(d):
        return jnp.stack(d).reshape(nl, 2, ng, STATE, GROUP_W).transpose(0, 1, 2, 4, 3).reshape(c_shape)

    dbb_r_c, dbb_i_c = b_cols(dbb_r), b_cols(dbb_i)
    c_shape = (nl, N_GROUPS, GROUP_W, STATE)
    dlr, dli, dldt, dbr, dbi = _ssm_prepare_bwd(
        lr, li, ldt, br, bi, jnp.stack(dab_r).reshape(n_all, 1), jnp.stack(dab_i).reshape(n_all, 1), dbb_r_c, dbb_i_c)
    small = {
        "b_ada": dvec[:, :N_MOD].reshape(nl, N_MOD * D_MODEL),
        "pre_mix_g": dvec[:, V_PRE_MIX], "post_mix_g": dvec[:, V_POST_MIX],
        "pre_mlp_g": dvec[:, V_PRE_MLP], "post_mlp_g": dvec[:, V_POST_MLP],
        "attn_sinks": jnp.stack(dsink_l),
        "lam_re": dlr.reshape(nl, N_GROUPS, STATE), "lam_im": dli.reshape(nl, N_GROUPS, STATE),
        "log_dt": jnp.sum(dldt.reshape(nl, N_GROUPS, STATE), axis=-1),
        "b_re": dbr.reshape(nl, N_GROUPS, STATE, GROUP_W), "b_im": dbi.reshape(nl, N_GROUPS, STATE, GROUP_W),
        "c_re": c_param(dc_re), "c_im": c_param(dc_im),
        "d_skip": dvec512[:, H_DSKIP], "b_glu": dvec512[:, H_BGLU],
        "attn_out_g": dvec512[:, H_ATTN_G], "ssm_out_g": dvec512[:, H_SSM_G],
    }
    return loss, dx, small, small["b_ada"]


WEIGHTS = ["w_ada", "b_ada", "pre_mix_g", "w_in", "attn_sinks", "lam_re", "lam_im", "log_dt", "b_re", "b_im", "c_re",
           "c_im", "d_skip", "w_glu", "b_glu", "attn_out_g", "ssm_out_g", "w_out", "post_mix_g", "pre_mlp_g",
           "w_mlp_in", "w_mlp_out", "post_mlp_g"]
BIG = ["w_in", "w_glu", "w_out", "w_mlp_in", "w_mlp_out"]
SMALL = [n for n in WEIGHTS if n not in BIG and n != "w_ada"]
PACK_ROWS = 256


def _pack(parts):
    rows = []
    for n in SMALL:
        flat = parts[n].reshape(-1)
        pad = (-flat.shape[0]) % (PACK_ROWS * LANE)
        rows.append(jnp.pad(flat, (0, pad)).reshape(-1, LANE))
    return jnp.concatenate(rows, axis=0)


def _unpack(packed, shapes):
    out, r0 = {}, 0
    for n in SMALL:
        size = int(np.prod(shapes[n]))
        rows = -(-size // (PACK_ROWS * LANE)) * PACK_ROWS
        out[n] = packed[r0:r0 + rows].reshape(-1)[:size].reshape(shapes[n])
        r0 += rows
    return out


MIXER = ["w_in", "w_glu", "w_out"]
MLP = ["w_mlp_in", "w_mlp_out"]


class _Exchanges:
    def __init__(self, shards, wts, mom, var, chip):
        self.shards, self.wts, self.mom, self.var, self.chip = shards, wts, mom, var, chip
        self.chip_arr = jnp.reshape(chip, (1,)).astype(jnp.int32)
        self.nl = len(shards["w_in"])
        self.gathers, self.scatters, self.pairs = {}, {}, {}
        self.res = {n: None for n in BIG}

    def _start_gather(self, group, tag, l, after=()):
        srcs = [self.shards[n][l] for n in group]
        lands = [lax.dynamic_update_slice(lax.empty((4,) + s.shape, s.dtype), s[None], (self.chip, 0, 0)) for s in srcs]
        plan = _plan_gather(len(srcs))
        st = _exchange_start(f"gather_{tag}{l}_start", 3 * len(srcs), plan, srcs + lands, after)
        self.gathers[tag, l] = (plan, st)
        return st[3]

    def _wait_gather(self, tag, l, after):
        plan, st = self.gathers.pop((tag, l))
        n = len(st[2]) // 2
        bufs = _exchange_wait(f"gather_{tag}{l}_wait", 3 * n, plan, st, after)
        return [b.reshape(4 * b.shape[1], b.shape[2]) for b in bufs[n:]]

    def _start_layer(self, l, after):
        tok = self._start_gather(MIXER, "mixer", l, after)
        return [tok, self._start_gather(MLP, "mlp", l, [tok])]

    def begin(self, after):
        toks = self._start_layer(0, after)
        return toks + (self._start_layer(1, toks[1:]) if self.nl > 1 else [])

    def mixer_weights(self, l, after):
        w = self._wait_gather("mixer", l, after)
        toks = self._start_layer(l + 2, w[:1]) if l + 2 < self.nl else []
        return w, toks

    def mlp_weights(self, l, after):
        return self._wait_gather("mlp", l, after)

    def _start_scatter(self, tag, l, group, pieces):
        plan = _plan_scatter(len(pieces))
        st = _exchange_start(f"scatter_{tag}{l}_start", 3 * len(pieces), plan,
                             list(pieces) + [lax.empty(p.shape, p.dtype) for p in pieces])
        self.scatters[tag] = (l, group, plan, st)
        return [st[3]]

    def _finish_scatter(self, tag, after):
        l, group, plan, st = self.scatters.pop(tag)
        n = len(group)
        bufs = _exchange_wait(f"scatter_{tag}{l}_wait", 3 * n, plan, st, after)
        sums = [_sum_pieces(bufs[k], bufs[n + k], self.chip_arr, "sum_" + group[k]) for k in range(n)]
        plan2 = _plan_pair(n)
        st2 = _exchange_start(f"pair_{tag}{l}_start", n, plan2, sums + [lax.empty(s.shape, s.dtype) for s in sums])
        self.pairs[tag] = (l, group, plan2, st2)
        return [st2[3]]

    def _finish_pair(self, tag, after):
        l, group, plan, st = self.pairs.pop(tag)
        n = len(group)
        bufs = _exchange_wait(f"pair_{tag}{l}_wait", n, plan, st, after)
        for k, name in enumerate(group):
            self.res[name] = _adamw_layer([bufs[k], bufs[n + k]], self.wts[name], self.mom[name], self.var[name],
                                          l, self.res[name], "adamw_" + name)

    def after_mlp_bwd(self, l, after):
        toks = self._finish_scatter("mixer", after) if "mixer" in self.scatters else []
        if "mlp" in self.pairs:
            self._finish_pair("mlp", after)
        return toks

    def mlp_grads(self, l, pieces):
        return self._start_scatter("mlp", l, MLP, pieces)

    def after_in_proj_bwd(self, l, after):
        toks = self._finish_scatter("mlp", after)
        if "mixer" in self.pairs:
            self._finish_pair("mixer", after)
        return toks

    def mixer_grads(self, l, pieces):
        return self._start_scatter("mixer", l, MIXER, pieces)

    def finish_mixer_scatter(self, after):
        return self._finish_scatter("mixer", after)

    def finish_mlp(self, after):
        self._finish_pair("mlp", after)

    def finish_mixer(self, after):
        self._finish_pair("mixer", after)

    def results(self):
        return self.res


def kernel(x, c, w_ada, b_ada, pre_mix_g, w_in, attn_sinks, lam_re, lam_im, log_dt, b_re, b_im, c_re, c_im, d_skip, w_glu, b_glu, attn_out_g, ssm_out_g, w_out, post_mix_g, pre_mlp_g, w_mlp_in, w_mlp_out, post_mlp_g, loss_target, m_w_ada, m_b_ada, m_pre_mix_g, m_w_in, m_attn_sinks, m_lam_re, m_lam_im, m_log_dt, m_b_re, m_b_im, m_c_re, m_c_im, m_d_skip, m_w_glu, m_b_glu, m_attn_out_g, m_ssm_out_g, m_w_out, m_post_mix_g, m_pre_mlp_g, m_w_mlp_in, m_w_mlp_out, m_post_mlp_g, v_w_ada, v_b_ada, v_pre_mix_g, v_w_in, v_attn_sinks, v_lam_re, v_lam_im, v_log_dt, v_b_re, v_b_im, v_c_re, v_c_im, v_d_skip, v_w_glu, v_b_glu, v_attn_out_g, v_ssm_out_g, v_w_out, v_post_mix_g, v_pre_mlp_g, v_w_mlp_in, v_w_mlp_out, v_post_mlp_g):
    args = locals()
    wts = {n: args[n] for n in WEIGHTS}
    mom = {n: args["m_" + n] for n in WEIGHTS}
    var = {n: args["v_" + n] for n in WEIGHTS}
    nl = w_in.shape[0]
    ix, iy, ic = lax.axis_index("x"), lax.axis_index("y"), lax.axis_index("c")
    chip = 2 * ix + iy
    me = 4 * ix + 2 * iy + ic
    wcols = w_ada.shape[2]

    shards = {"w_in": [w_in[l].astype(BF16).T for l in range(nl)], "w_glu": [w_glu[l].astype(BF16) for l in range(nl)],
              "w_out": [w_out[l].astype(BF16) for l in range(nl)],
              "w_mlp_in": [w_mlp_in[l].astype(BF16).T for l in range(nl)],
              "w_mlp_out": [w_mlp_out[l].astype(BF16) for l in range(nl)]}
    comm = _Exchanges(shards, wts, mom, var, chip)

    c_all = _gather([c.reshape(1, 1, 1, D_MODEL)], "all", "gather_c")[0].reshape(8, D_MODEL)
    b_sh = lax.dynamic_slice(b_ada, (0, chip * wcols), (nl, wcols)).reshape(nl, 1, wcols)
    mod_sh = _ada_forward(c_all, w_ada, b_sh)
    mod_all = _gather([mod_sh.reshape(1, 1, nl * 8, wcols)], "chips", "gather_mod")[0]
    toks = comm.begin([mod_all])
    mod = lax.dynamic_index_in_dim(mod_all.reshape(4, nl, 8, wcols), me, axis=2, keepdims=False)
    mod = mod.transpose(1, 0, 2).reshape(nl, 4 * wcols)

    small_p = {n: wts[n] for n in SMALL}
    small_p["log_dt"] = _after(log_dt, *toks)
    loss, grad_x, small, dmod = _local_step(x[0], loss_target[0], mod, small_p, comm)
    loss = lax.psum(loss, ("x", "y", "c"))

    packed = _pack(small)
    rows = packed.shape[0]
    pair_plan = _plan_pair(1)
    pair_small = _exchange_start("pair_small_start", 1, pair_plan, [packed, lax.empty((rows, LANE), F32)])

    dmod = _after(dmod, pair_small[3])
    dmod_all = _gather([dmod.reshape(1, 1, nl, N_MOD * D_MODEL)], "all", "gather_dmod")[0][0]
    dmod_sh = lax.dynamic_slice(dmod_all, (0, 0, chip * wcols), (8, nl, wcols)).transpose(1, 0, 2)
    g_ada = _ada_weight_grad(c_all.T, dmod_sh)
    res = {"w_ada": _adamw(g_ada[:, None], w_ada, m_w_ada, v_w_ada, "adamw_w_ada")}

    comm.finish_mlp(res["w_ada"][0])
    toks = comm.finish_mixer_scatter(res["w_ada"][0])

    own, other = _exchange_wait("pair_small_wait", 1, pair_plan, pair_small, res["w_ada"][0])
    chip_sum = _after(_sum_list([own, other], "sum_pair_small"), *toks)
    quad_plan = _plan_gather(1)
    quad0 = lax.dynamic_update_slice(lax.empty((4, rows, LANE), F32), chip_sum[None], (chip, 0, 0))
    quad_small = _exchange_start("gather_small_start", 3, quad_plan, [chip_sum, quad0])
    comm.finish_mixer([comm.results()[n][0] for n in MLP] + [_after(dmod, quad_small[3])])
    res.update(comm.results())
    quad = _exchange_wait("gather_small_wait", 3, quad_plan, quad_small, [res[n][0] for n in BIG])[1]
    outs = _adamw(quad[None], _pack({n: wts[n] for n in SMALL})[None], _pack({n: mom[n] for n in SMALL})[None],
                  _pack({n: var[n] for n in SMALL})[None], "adamw_small")
    shapes = {n: wts[n].shape for n in SMALL}
    unpacked = [_unpack(o[0], shapes) for o in outs]
    for n in SMALL:
        res[n] = [u[n] for u in unpacked]

    return (loss, grad_x[None], *[res[n][0] for n in WEIGHTS], *[res[n][1] for n in WEIGHTS],
            *[res[n][2] for n in WEIGHTS], *[res[n][3] for n in WEIGHTS])
```

```python
import functools
import math

import numpy as np
import jax
import jax.numpy as jnp
from jax import lax
from jax.experimental import pallas as pl
from jax.experimental.pallas import tpu as pltpu

F32 = jnp.float32
BF16 = jnp.bfloat16

D_MODEL = 1024
ATTN_W = 512
SSM_W = 512
HEAD_DIM = 64
N_Q = 8
N_KV = 2
Q_PER_KV = 4
KV_W = 128
WINDOW = 128
BLOCK = 128
N_GROUPS = 32
GROUP_W = 16
STATE = 64
N_CH = N_GROUPS * STATE
HALF_CH = N_CH // 2
D_FF = 4096
IN_W = 1280
N_MOD = 6
EPS = 1e-6
NEG_INF = -1e30

ADAM_LR = 0.001
ADAM_B1 = 0.9
ADAM_B2 = 0.999
ADAM_EPS = 1e-08
ADAM_WD = 0.01
ADAM_STEP = 10

ROW_TILE = 256
PROJ_TILE = 512
CHUNK = 512
SEGS = 8
STEPS = CHUNK // SEGS
STRIP = 1024
VMEM_LIMIT_V7X = 56 * 1024 * 1024
LANE = 128
SUBLANE = 8

GELU_K0 = math.sqrt(2.0 / math.pi)
GELU_K1 = 0.044715

V_SH1, V_SC1, V_G1, V_SH2, V_SC2, V_G2, V_PRE_MIX, V_POST_MIX, V_PRE_MLP, V_POST_MLP = range(10)
H_ATTN_G, H_SSM_G, H_DSKIP, H_BGLU = range(4)

HBM = pl.BlockSpec(memory_space=pltpu.HBM)
SEM = pl.BlockSpec(memory_space=pltpu.SEMAPHORE)
EFFECT = pltpu.SideEffectType.DATAFLOW_SIDE_EFFECTING
MESH_ID = pl.DeviceIdType.MESH


def _nn(a, b):
    return lax.dot_general(a, b, (((1,), (0,)), ((), ())), preferred_element_type=F32)


def _nt(a, b):
    return lax.dot_general(a, b, (((1,), (1,)), ((), ())), preferred_element_type=F32)


def _tn(a, b):
    return lax.dot_general(a, b, (((0,), (0,)), ((), ())), preferred_element_type=F32)


def _params(sem):
    return pltpu.CompilerParams(dimension_semantics=sem, vmem_limit_bytes=VMEM_LIMIT_V7X)


def _rms_fwd(x, g):
    r = lax.rsqrt(jnp.mean(x * x, axis=-1, keepdims=True) + EPS)
    xh = x * r
    return xh * g, xh, r


def _rms_bwd(dy, xh, r, g):
    dxh = dy * g
    dx = r * (dxh - xh * jnp.mean(dxh * xh, axis=-1, keepdims=True))
    return dx, dy * xh


def _colsum(t):
    return jnp.sum(t, axis=0, keepdims=True)


def _gelu(y):
    t = jnp.tanh(GELU_K0 * (y + GELU_K1 * (y * y * y)))
    return 0.5 * y * (1.0 + t), t


def _gelu_grad(y, t):
    return 0.5 * (1.0 + t) + 0.5 * y * (1.0 - t * t) * GELU_K0 * (1.0 + 3.0 * GELU_K1 * y * y)


def _alibi_slopes():
    return [float(s) for s in 2.0 ** (-8.0 * np.arange(1, N_Q + 1) / N_Q)]


def _pick_rows(rows, bytes_per_row, budget):
    t = rows
    while t % (2 * SUBLANE) == 0 and t * bytes_per_row > budget:
        t //= 2
    return t


def _load_once(step, pairs, sems):
    @pl.when(step == 0)
    def _():
        cps = [pltpu.make_async_copy(src, dst, sems.at[k]) for k, (src, dst) in enumerate(pairs)]
        for cp in cps:
            cp.start()
        for cp in cps:
            cp.wait()


_GROUPS = {
    "all": ([(0, 0, 1), (0, 1, 0), (0, 1, 1), (1, 0, 0), (1, 0, 1), (1, 1, 0), (1, 1, 1)], (4, 2, 1), 8),
    "chips": ([(1, 0, 0), (0, 1, 0), (1, 1, 0)], (2, 1, 0), 4),
    "pair": ([(0, 0, 1)], (0, 0, 1), 2),
}


def _flip(v, f):
    return 1 - v if f else v


def _gather(arrs, kind, name):
    masks, wts, n = _GROUPS[kind]
    na, nm = len(arrs), len(masks)

    def body(*refs):
        ins, outs = refs[:na], refs[na:2 * na]
        ssem, rsem, lsem = refs[2 * na:]
        x, y, c = lax.axis_index("x"), lax.axis_index("y"), lax.axis_index("c")
        me = wts[0] * x + wts[1] * y + wts[2] * c
        local = [pltpu.make_async_copy(ins[k], outs[k].at[:, pl.ds(me, 1)], lsem.at[k]) for k in range(na)]
        for cp in local:
            cp.start()
        remote = []
        for k in range(na):
            for mi, (fx, fy, fc) in enumerate(masks):
                peer = (_flip(x, fx), _flip(y, fy), _flip(c, fc))
                remote.append(pltpu.make_async_remote_copy(
                    src_ref=ins[k], dst_ref=outs[k].at[:, pl.ds(me, 1)],
                    send_sem=ssem.at[k * nm + mi], recv_sem=rsem.at[k * nm + mi],
                    device_id=peer, device_id_type=MESH_ID))
        for cp in remote:
            cp.start()
        for cp in remote:
            cp.wait()
        for cp in local:
            cp.wait()

    outs = pl.pallas_call(
        body, name=name,
        out_shape=[jax.ShapeDtypeStruct((a.shape[0], n) + a.shape[2:], a.dtype) for a in arrs],
        in_specs=[HBM] * na, out_specs=[HBM] * na,
        scratch_shapes=[pltpu.SemaphoreType.DMA((na * nm,)), pltpu.SemaphoreType.DMA((na * nm,)),
                        pltpu.SemaphoreType.DMA((na,))],
    )(*arrs)
    return list(outs)


def _hbm(a):
    return pltpu.with_memory_space_constraint(a, pltpu.HBM)


def _after(x, *tokens):
    for t in tokens:
        x = x + t[0, 0].astype(x.dtype)
    return x


def _exchange_start(name, n_copies, plan, bufs, after=()):
    n, na = len(bufs), len(after)

    def body(*refs):
        ssem, rsem, token = refs[n + na], refs[n + na + 1], refs[2 * n + na + 2]
        for k, (src, dst, dev) in enumerate(plan(refs[:n])):
            pltpu.make_async_remote_copy(src_ref=src, dst_ref=dst, send_sem=ssem.at[k], recv_sem=rsem.at[k],
                                         device_id=dev, device_id_type=MESH_ID).start()
        token[...] = jnp.zeros_like(token)

    outs = pl.pallas_call(
        body, name=name,
        out_shape=(pltpu.SemaphoreType.DMA((n_copies,)), pltpu.SemaphoreType.DMA((n_copies,)),
                   *[pltpu.HBM(b.shape, b.dtype) for b in bufs], jax.ShapeDtypeStruct((SUBLANE, LANE), F32)),
        in_specs=[HBM] * n + [pl.BlockSpec(memory_space=pl.ANY)] * na,
        out_specs=(SEM, SEM, *[HBM] * n, pl.BlockSpec(memory_space=pltpu.VMEM)),
        input_output_aliases={i: 2 + i for i in range(n)},
        compiler_params=pltpu.CompilerParams(has_side_effects=EFFECT),
    )(*[_hbm(b) for b in bufs], *after)
    return outs[0], outs[1], list(outs[2:2 + n]), outs[2 + n]


def _exchange_wait(name, n_copies, plan, started, after):
    ssem, rsem, bufs, _ = started
    n = len(bufs)
    after = list(after) if isinstance(after, (list, tuple)) else [after]

    def body(*refs):
        ssem_ref, rsem_ref = refs[n], refs[n + 1]
        for k, (src, dst, dev) in enumerate(plan(refs[:n])):
            cp = pltpu.make_async_remote_copy(src_ref=src, dst_ref=dst, send_sem=ssem_ref.at[k], recv_sem=rsem_ref.at[k],
                                              device_id=dev, device_id_type=MESH_ID)
            cp.wait_send()
            cp.wait_recv()

    outs = pl.pallas_call(
        body, name=name,
        out_shape=tuple(pltpu.HBM(b.shape, b.dtype) for b in bufs),
        in_specs=[HBM] * n + [SEM, SEM] + [pl.BlockSpec(memory_space=pl.ANY)] * len(after), out_specs=tuple([HBM] * n),
        input_output_aliases={i: i for i in range(n)},
        compiler_params=pltpu.CompilerParams(has_side_effects=EFFECT),
    )(*bufs, ssem, rsem, *after)
    return list(outs)


def _position():
    x, y, c = lax.axis_index("x"), lax.axis_index("y"), lax.axis_index("c")
    return x, y, c, [(1 - x, y), (x, 1 - y), (1 - x, 1 - y)]


def _plan_gather(na):
    def plan(refs):
        x, y, c, chips = _position()
        return [(refs[k], refs[na + k].at[2 * x + y], (px, py, c)) for k in range(na) for px, py in chips]
    return plan


def _plan_scatter(na):
    def plan(refs):
        x, y, c, chips = _position()
        return [(refs[k].at[2 * px + py], refs[na + k].at[2 * x + y], (px, py, c))
                for k in range(na) for px, py in chips]
    return plan


def _plan_pair(na):
    def plan(refs):
        x, y, c, _ = _position()
        return [(refs[k], refs[na + k], (x, y, 1 - c)) for k in range(na)]
    return plan


def _sum_list(arrs, name):
    n = len(arrs)
    r, c = arrs[0].shape
    tr = _pick_rows(r, c * 4 * (n + 1), 4 << 20)

    def body(*refs):
        acc = refs[0][...].astype(F32)
        for j in range(1, n):
            acc = acc + refs[j][...].astype(F32)
        refs[n][...] = acc

    blk = pl.BlockSpec((tr, c), lambda i: (i, 0))
    return pl.pallas_call(
        body, name=name, grid=(r // tr,), in_specs=[blk] * n, out_specs=blk,
        out_shape=jax.ShapeDtypeStruct((r, c), F32), compiler_params=_params(("parallel",)),
    )(*arrs)


def _sum_pieces(own, recv, chip, name):
    _, r, c = own.shape
    tr = _pick_rows(r, c * 2 * 6, 4 << 20)

    def body(chip_ref, own_ref, recv_ref, o_ref):
        acc = own_ref[0].astype(F32)
        for j in range(4):
            acc = acc + jnp.where(chip_ref[0] == j, 0.0, recv_ref[j].astype(F32))
        o_ref[...] = acc.astype(BF16)

    return pl.pallas_call(
        body, name=name,
        grid_spec=pltpu.PrefetchScalarGridSpec(
            num_scalar_prefetch=1, grid=(r // tr,),
            in_specs=[pl.BlockSpec((1, tr, c), lambda i, chip_ref: (chip_ref[0], i, 0)),
                      pl.BlockSpec((4, tr, c), lambda i, chip_ref: (0, i, 0))],
            out_specs=pl.BlockSpec((tr, c), lambda i, chip_ref: (i, 0))),
        out_shape=jax.ShapeDtypeStruct((r, c), BF16), compiler_params=_params(("parallel",)),
    )(chip, own, recv)


def _adam_update(g, w, m, v):
    mn = ADAM_B1 * m + (1.0 - ADAM_B1) * g
    vn = ADAM_B2 * v + (1.0 - ADAM_B2) * jnp.square(g)
    m_hat = mn / (1.0 - ADAM_B1 ** ADAM_STEP)
    v_hat = vn / (1.0 - ADAM_B2 ** ADAM_STEP)
    return -ADAM_LR * (m_hat / (jnp.sqrt(v_hat) + ADAM_EPS) + ADAM_WD * w), mn, vn


def _adamw_layer(grads, w, m, v, layer, prev, name):
    ng = len(grads)
    nl, r, c = w.shape
    tr = _pick_rows(r, c * 4 * (ng + 7), 6 << 20)
    if prev is None:
        prev = [lax.empty((nl, r, c), F32) for _ in range(4)]

    def body(*refs):
        g = refs[0][...].astype(F32)
        for j in range(1, ng):
            g = g + refs[j][...].astype(F32)
        w_ref, m_ref, v_ref = refs[ng:ng + 3]
        go_ref, d_ref, mo_ref, vo_ref = refs[ng + 7:ng + 11]
        d, mn, vn = _adam_update(g, w_ref[0], m_ref[0], v_ref[0])
        go_ref[0] = g
        d_ref[0] = d
        mo_ref[0] = mn
        vo_ref[0] = vn

    gblk = pl.BlockSpec((tr, c), lambda i: (i, 0))
    blk = pl.BlockSpec((1, tr, c), lambda i: (layer, i, 0))
    keep = pl.BlockSpec(memory_space=pl.ANY)
    sds = jax.ShapeDtypeStruct((nl, r, c), F32)
    return pl.pallas_call(
        body, name=name, grid=(r // tr,),
        in_specs=[gblk] * ng + [blk] * 3 + [keep] * 4,
        out_specs=[blk] * 4, out_shape=[sds] * 4,
        input_output_aliases={ng + 3 + i: i for i in range(4)},
        compiler_params=_params(("parallel",)),
    )(*grads, w, m, v, *prev)


def _adamw(gs, w, m, v, name):
    a, s, r, c = gs.shape
    tr = _pick_rows(r, c * 4 * (s + 7), 6 << 20)

    def body(g_ref, w_ref, m_ref, v_ref, go_ref, d_ref, mo_ref, vo_ref):
        g = g_ref[0, 0].astype(F32)
        for j in range(1, s):
            g = g + g_ref[0, j].astype(F32)
        d, mn, vn = _adam_update(g, w_ref[0], m_ref[0], v_ref[0])
        go_ref[0] = g
        d_ref[0] = d
        mo_ref[0] = mn
        vo_ref[0] = vn

    blk = pl.BlockSpec((1, tr, c), lambda i, j: (i, j, 0))
    sds = jax.ShapeDtypeStruct((a, r, c), F32)
    return pl.pallas_call(
        body, name=name, grid=(a, r // tr),
        in_specs=[pl.BlockSpec((1, s, tr, c), lambda i, j: (i, 0, j, 0)), blk, blk, blk],
        out_specs=[blk, blk, blk, blk], out_shape=[sds, sds, sds, sds],
        compiler_params=_params(("parallel", "parallel")),
    )(gs, w, m, v)


def _ada_forward(c_all, w_ada, b_sh):
    nl, d, w = w_ada.shape
    tw = 512

    def body(c_ref, w_ref, b_ref, o_ref):
        cv = c_ref[...]
        act = (cv * jax.nn.sigmoid(cv)).astype(BF16)
        o_ref[0] = _nn(act, w_ref[0].astype(BF16)) + b_ref[0]

    return pl.pallas_call(
        body, name="ada_forward", grid=(nl, w // tw),
        in_specs=[pl.BlockSpec((8, d), lambda l, j: (0, 0)),
                  pl.BlockSpec((1, d, tw), lambda l, j: (l, 0, j)),
                  pl.BlockSpec((1, 1, tw), lambda l, j: (l, 0, j))],
        out_specs=pl.BlockSpec((1, 8, tw), lambda l, j: (l, 0, j)),
        out_shape=jax.ShapeDtypeStruct((nl, 8, w), F32),
        compiler_params=_params(("parallel", "parallel")),
    )(c_all, w_ada, b_sh)


def _ada_weight_grad(c_all_t, dmod):
    nl, nb, w = dmod.shape
    d = c_all_t.shape[0]
    tw = 512

    def body(c_ref, g_ref, o_ref):
        cv = c_ref[...]
        act = cv * jax.nn.sigmoid(cv)
        gv = g_ref[0]
        acc = act[:, 0:1] * gv[0:1, :]
        for b in range(1, nb):
            acc = acc + act[:, b:b + 1] * gv[b:b + 1, :]
        o_ref[0] = acc

    return pl.pallas_call(
        body, name="ada_weight_grad", grid=(nl, w // tw),
        in_specs=[pl.BlockSpec((d, nb), lambda l, j: (0, 0)),
                  pl.BlockSpec((1, nb, tw), lambda l, j: (l, 0, j))],
        out_specs=pl.BlockSpec((1, d, tw), lambda l, j: (l, 0, j)),
        out_shape=jax.ShapeDtypeStruct((nl, d, w), F32),
        compiler_params=_params(("parallel", "parallel")),
    )(c_all_t, dmod)


def _in_proj_fwd(x, vec, w_in_t, layer):
    seq = x.shape[0]
    tm = PROJ_TILE

    def body(x_ref, vec_ref, w_ref, q_ref, kv_ref, u4_ref, h_ref):
        n, _, _ = _rms_fwd(x_ref[...], vec_ref[0, V_PRE_MIX:V_PRE_MIX + 1, :])
        h = (n * (1.0 + vec_ref[0, V_SC1:V_SC1 + 1, :]) + vec_ref[0, V_SH1:V_SH1 + 1, :]).astype(BF16)
        h_ref[...] = h
        proj = _nt(h, w_ref[...])
        q_ref[...] = proj[:, :ATTN_W].astype(BF16)
        kv_ref[...] = proj[:, ATTN_W:ATTN_W + 2 * KV_W].astype(BF16)
        u0 = ATTN_W + 2 * KV_W
        for j in range(4):
            u4_ref[j] = proj[:, u0 + j * LANE:u0 + (j + 1) * LANE]

    return pl.pallas_call(
        body, name="in_proj_fwd", grid=(seq // tm,),
        in_specs=[pl.BlockSpec((tm, D_MODEL), lambda i: (i, 0)),
                  pl.BlockSpec((1, 16, D_MODEL), lambda i: (layer, 0, 0)),
                  pl.BlockSpec((IN_W, D_MODEL), lambda i: (0, 0))],
        out_specs=[pl.BlockSpec((tm, ATTN_W), lambda i: (i, 0)),
                   pl.BlockSpec((tm, 2 * KV_W), lambda i: (i, 0)),
                   pl.BlockSpec((4, tm, LANE), lambda i: (0, i, 0)),
                   pl.BlockSpec((tm, D_MODEL), lambda i: (i, 0))],
        out_shape=[jax.ShapeDtypeStruct((seq, ATTN_W), BF16), jax.ShapeDtypeStruct((seq, 2 * KV_W), BF16),
                   jax.ShapeDtypeStruct((4, seq, LANE), F32), jax.ShapeDtypeStruct((seq, D_MODEL), BF16)],
        compiler_params=_params(("parallel",)),
    )(x, vec, w_in_t)


def _in_proj_bwd(dx1, dq, dkv, du4, x, vec, w_in_t, layer):
    seq = x.shape[0]
    tm = PROJ_TILE

    def body(dx1_ref, dq_ref, dkv_ref, du4_ref, x_ref, vec_ref, w_ref, dx_ref, dp_ref, dvec_ref):
        i = pl.program_id(0)

        @pl.when(i == 0)
        def _():
            dvec_ref[...] = jnp.zeros_like(dvec_ref)

        dproj = jnp.concatenate([dq_ref[...], dkv_ref[...]] + [du4_ref[j] for j in range(4)], axis=1).astype(BF16)
        dp_ref[...] = dproj
        dh = _nn(dproj, w_ref[...])
        g = vec_ref[0, V_PRE_MIX:V_PRE_MIX + 1, :]
        n, xh, r = _rms_fwd(x_ref[...], g)
        dn = dh * (1.0 + vec_ref[0, V_SC1:V_SC1 + 1, :])
        dxn, dg_rows = _rms_bwd(dn, xh, r, g)
        dx_ref[...] = dx1_ref[...] + dxn
        dvec_ref[V_SH1:V_SH1 + 1, :] += _colsum(dh)
        dvec_ref[V_SC1:V_SC1 + 1, :] += _colsum(dh * n)
        dvec_ref[V_PRE_MIX:V_PRE_MIX + 1, :] += _colsum(dg_rows)

    row = pl.BlockSpec((tm, D_MODEL), lambda i: (i, 0))
    return pl.pallas_call(
        body, name="in_proj_bwd", grid=(seq // tm,),
        in_specs=[row, pl.BlockSpec((tm, ATTN_W), lambda i: (i, 0)), pl.BlockSpec((tm, 2 * KV_W), lambda i: (i, 0)),
                  pl.BlockSpec((4, tm, LANE), lambda i: (0, i, 0)), row,
                  pl.BlockSpec((1, 16, D_MODEL), lambda i: (layer, 0, 0)),
                  pl.BlockSpec((IN_W, D_MODEL), lambda i: (0, 0))],
        out_specs=[row, pl.BlockSpec((tm, IN_W), lambda i: (i, 0)), pl.BlockSpec((16, D_MODEL), lambda i: (0, 0))],
        out_shape=[jax.ShapeDtypeStruct((seq, D_MODEL), F32), jax.ShapeDtypeStruct((seq, IN_W), BF16),
                   jax.ShapeDtypeStruct((16, D_MODEL), F32)],
        compiler_params=_params(("arbitrary",)),
    )(dx1, dq, dkv, du4, x, vec, w_in_t)


def _heads(attn_ref, s4_ref, vec512_ref):
    ga = vec512_ref[0, H_ATTN_G:H_ATTN_G + 1, :]
    gs = vec512_ref[0, H_SSM_G:H_SSM_G + 1, :]
    sv = jnp.concatenate([s4_ref[j] for j in range(4)], axis=1)
    na, ah, ar = _rms_fwd(attn_ref[...], ga)
    ns, sh, sr = _rms_fwd(sv, gs)
    return jnp.concatenate([na, ns], axis=1), (ah, ar, ga), (sh, sr, gs)


def _out_proj_fwd(x, attn, s4, vec, vec512, w_out, layer):
    seq = x.shape[0]
    tm = PROJ_TILE

    def body(x_ref, attn_ref, s4_ref, vec_ref, vec512_ref, w_ref, x1_ref):
        heads, _, _ = _heads(attn_ref, s4_ref, vec512_ref)
        mixed = _nn(heads.astype(BF16), w_ref[...])
        nm, _, _ = _rms_fwd(mixed, vec_ref[0, V_POST_MIX:V_POST_MIX + 1, :])
        x1_ref[...] = x_ref[...] + vec_ref[0, V_G1:V_G1 + 1, :] * nm

    row = pl.BlockSpec((tm, D_MODEL), lambda i: (i, 0))
    return pl.pallas_call(
        body, name="out_proj_fwd", grid=(seq // tm,),
        in_specs=[row, pl.BlockSpec((tm, ATTN_W), lambda i: (i, 0)), pl.BlockSpec((4, tm, LANE), lambda i: (0, i, 0)),
                  pl.BlockSpec((1, 16, D_MODEL), lambda i: (layer, 0, 0)),
                  pl.BlockSpec((1, 8, SSM_W), lambda i: (layer, 0, 0)),
                  pl.BlockSpec((D_MODEL, D_MODEL), lambda i: (0, 0))],
        out_specs=row, out_shape=jax.ShapeDtypeStruct((seq, D_MODEL), F32),
        compiler_params=_params(("parallel",)),
    )(x, attn, s4, vec, vec512, w_out)


def _out_proj_bwd(dx1, attn, s4, vec, vec512, w_out, layer):
    seq = dx1.shape[0]
    tm = PROJ_TILE

    def body(dx1_ref, attn_ref, s4_ref, vec_ref, vec512_ref, w_ref,
             dattn_ref, ds4_ref, heads_ref, dmixed_ref, dvec_ref, dvec512_ref):
        i = pl.program_id(0)

        @pl.when(i == 0)
        def _():
            dvec_ref[...] = jnp.zeros_like(dvec_ref)
            dvec512_ref[...] = jnp.zeros_like(dvec512_ref)

        heads, (ah, ar, ga), (sh, sr, gs) = _heads(attn_ref, s4_ref, vec512_ref)
        hb = heads.astype(BF16)
        heads_ref[...] = hb
        gm = vec_ref[0, V_POST_MIX:V_POST_MIX + 1, :]
        nm, mh, mr = _rms_fwd(_nn(hb, w_ref[...]), gm)
        dx1v = dx1_ref[...]
        dvec_ref[V_G1:V_G1 + 1, :] += _colsum(dx1v * nm)
        dmixed, dgm_rows = _rms_bwd(dx1v * vec_ref[0, V_G1:V_G1 + 1, :], mh, mr, gm)
        dvec_ref[V_POST_MIX:V_POST_MIX + 1, :] += _colsum(dgm_rows)
        dmb = dmixed.astype(BF16)
        dmixed_ref[...] = dmb
        dheads = _nt(dmb, w_ref[...])
        dattn, dga_rows = _rms_bwd(dheads[:, :ATTN_W], ah, ar, ga)
        ds, dgs_rows = _rms_bwd(dheads[:, ATTN_W:], sh, sr, gs)
        dattn_ref[...] = dattn
        for j in range(4):
            ds4_ref[j] = ds[:, j * LANE:(j + 1) * LANE]
        dvec512_ref[H_ATTN_G:H_ATTN_G + 1, :] += _colsum(dga_rows)
        dvec512_ref[H_SSM_G:H_SSM_G + 1, :] += _colsum(dgs_rows)

    row = pl.BlockSpec((tm, D_MODEL), lambda i: (i, 0))
    return pl.pallas_call(
        body, name="out_proj_bwd", grid=(seq // tm,),
        in_specs=[row, pl.BlockSpec((tm, ATTN_W), lambda i: (i, 0)), pl.BlockSpec((4, tm, LANE), lambda i: (0, i, 0)),
                  pl.BlockSpec((1, 16, D_MODEL), lambda i: (layer, 0, 0)),
                  pl.BlockSpec((1, 8, SSM_W), lambda i: (layer, 0, 0)),
                  pl.BlockSpec((D_MODEL, D_MODEL), lambda i: (0, 0))],
        out_specs=[pl.BlockSpec((tm, ATTN_W), lambda i: (i, 0)), pl.BlockSpec((4, tm, LANE), lambda i: (0, i, 0)),
                   row, row, pl.BlockSpec((16, D_MODEL), lambda i: (0, 0)), pl.BlockSpec((8, SSM_W), lambda i: (0, 0))],
        out_shape=[jax.ShapeDtypeStruct((seq, ATTN_W), F32), jax.ShapeDtypeStruct((4, seq, LANE), F32),
                   jax.ShapeDtypeStruct((seq, D_MODEL), BF16), jax.ShapeDtypeStruct((seq, D_MODEL), BF16),
                   jax.ShapeDtypeStruct((16, D_MODEL), F32), jax.ShapeDtypeStruct((8, SSM_W), F32)],
        compiler_params=_params(("arbitrary",)),
    )(dx1, attn, s4, vec, vec512, w_out)


def _mlp_fwd(x1, vec, w_in_t, w_out, layer):
    seq = x1.shape[0]
    tm = ROW_TILE

    def body(x1_ref, vec_ref, wi_hbm, wo_hbm, x2_ref, r_ref, f_ref, wi, wo, sems):
        _load_once(pl.program_id(0), [(wi_hbm, wi), (wo_hbm, wo)], sems)
        x1v = x1_ref[...]
        n, _, _ = _rms_fwd(x1v, vec_ref[0, V_PRE_MLP:V_PRE_MLP + 1, :])
        h = (n * (1.0 + vec_ref[0, V_SC2:V_SC2 + 1, :]) + vec_ref[0, V_SH2:V_SH2 + 1, :]).astype(BF16)
        a = _nt(h, wi[...])
        r = jnp.square(jnp.maximum(a, 0.0)).astype(BF16)
        r_ref[...] = r
        f = _nn(r, wo[...])
        f_ref[...] = f
        nf, _, _ = _rms_fwd(f, vec_ref[0, V_POST_MLP:V_POST_MLP + 1, :])
        x2_ref[...] = x1v + vec_ref[0, V_G2:V_G2 + 1, :] * nf

    row = pl.BlockSpec((tm, D_MODEL), lambda i: (i, 0))
    return pl.pallas_call(
        body, name="mlp_fwd", grid=(seq // tm,),
        in_specs=[row, pl.BlockSpec((1, 16, D_MODEL), lambda i: (layer, 0, 0)), HBM, HBM],
        out_specs=[row, pl.BlockSpec((tm, D_FF), lambda i: (i, 0)), row],
        out_shape=[jax.ShapeDtypeStruct((seq, D_MODEL), F32), jax.ShapeDtypeStruct((seq, D_FF), BF16),
                   jax.ShapeDtypeStruct((seq, D_MODEL), F32)],
        scratch_shapes=[pltpu.VMEM((D_FF, D_MODEL), BF16), pltpu.VMEM((D_FF, D_MODEL), BF16),
                        pltpu.SemaphoreType.DMA((2,))],
        compiler_params=_params(("arbitrary",)),
    )(x1, vec, w_in_t, w_out)


def _mlp_bwd(dx2, x1, r, f, vec, w_in_t, w_out, layer):
    seq = x1.shape[0]
    tm = ROW_TILE

    def body(dx2_ref, x1_ref, r_ref, f_ref, vec_ref, wi_hbm, wo_hbm, dx1_ref, h_ref, da_ref, df_ref, dvec_ref,
             wi, wo, sems):
        i = pl.program_id(0)
        _load_once(i, [(wi_hbm, wi), (wo_hbm, wo)], sems)

        @pl.when(i == 0)
        def _():
            dvec_ref[...] = jnp.zeros_like(dvec_ref)

        g_pre = vec_ref[0, V_PRE_MLP:V_PRE_MLP + 1, :]
        g_post = vec_ref[0, V_POST_MLP:V_POST_MLP + 1, :]
        sc2 = vec_ref[0, V_SC2:V_SC2 + 1, :]
        n, xh, xr = _rms_fwd(x1_ref[...], g_pre)
        h_ref[...] = (n * (1.0 + sc2) + vec_ref[0, V_SH2:V_SH2 + 1, :]).astype(BF16)
        relu = jnp.sqrt(r_ref[...].astype(F32))
        nf, fh, fr = _rms_fwd(f_ref[...], g_post)
        dx2v = dx2_ref[...]
        dvec_ref[V_G2:V_G2 + 1, :] += _colsum(dx2v * nf)
        df, dgp_rows = _rms_bwd(dx2v * vec_ref[0, V_G2:V_G2 + 1, :], fh, fr, g_post)
        dvec_ref[V_POST_MLP:V_POST_MLP + 1, :] += _colsum(dgp_rows)
        dfb = df.astype(BF16)
        df_ref[...] = dfb
        da = (_nt(dfb, wo[...]) * (2.0 * relu)).astype(BF16)
        da_ref[...] = da
        dh = _nn(da, wi[...])
        dvec_ref[V_SH2:V_SH2 + 1, :] += _colsum(dh)
        dvec_ref[V_SC2:V_SC2 + 1, :] += _colsum(dh * n)
        dxn, dg_rows = _rms_bwd(dh * (1.0 + sc2), xh, xr, g_pre)
        dvec_ref[V_PRE_MLP:V_PRE_MLP + 1, :] += _colsum(dg_rows)
        dx1_ref[...] = dx2v + dxn

    row = pl.BlockSpec((tm, D_MODEL), lambda i: (i, 0))
    wide = pl.BlockSpec((tm, D_FF), lambda i: (i, 0))
    return pl.pallas_call(
        body, name="mlp_bwd", grid=(seq // tm,),
        in_specs=[row, row, wide, row, pl.BlockSpec((1, 16, D_MODEL), lambda i: (layer, 0, 0)), HBM, HBM],
        out_specs=[row, row, wide, row, pl.BlockSpec((16, D_MODEL), lambda i: (0, 0))],
        out_shape=[jax.ShapeDtypeStruct((seq, D_MODEL), F32), jax.ShapeDtypeStruct((seq, D_MODEL), BF16),
                   jax.ShapeDtypeStruct((seq, D_FF), BF16),
                   jax.ShapeDtypeStruct((seq, D_MODEL), BF16), jax.ShapeDtypeStruct((16, D_MODEL), F32)],
        scratch_shapes=[pltpu.VMEM((D_FF, D_MODEL), BF16), pltpu.VMEM((D_FF, D_MODEL), BF16),
                        pltpu.SemaphoreType.DMA((2,))],
        compiler_params=_params(("arbitrary",)),
    )(dx2, x1, r, f, vec, w_in_t, w_out)


def _loss_head(y, target):
    seq = y.shape[0]
    tm = ROW_TILE

    def body(y_ref, t_ref, dy_ref, part_ref):
        e = y_ref[...] - t_ref[...]
        dy_ref[...] = e * (1.0 / D_MODEL)
        tot = jnp.sum(jnp.sum(e * e, axis=1, keepdims=True), axis=0, keepdims=True) * (0.5 / D_MODEL)
        part_ref[0] = jnp.broadcast_to(tot, (SUBLANE, LANE))

    row = pl.BlockSpec((tm, D_MODEL), lambda i: (i, 0))
    return pl.pallas_call(
        body, name="loss_head", grid=(seq // tm,),
        in_specs=[row, row],
        out_specs=[row, pl.BlockSpec((1, SUBLANE, LANE), lambda i: (i, 0, 0))],
        out_shape=[jax.ShapeDtypeStruct((seq, D_MODEL), F32), jax.ShapeDtypeStruct((seq // tm, SUBLANE, LANE), F32)],
        compiler_params=_params(("parallel",)),
    )(y, target)


def _matmul_tn(a, b, out_dtype, name, pieces=1):
    kk, m = a.shape
    n = b.shape[1]
    tm = min(m, 512)
    tn = n // pieces if pieces > 1 else min(n, 1280)
    tk = min(kk, 2048)
    nk = kk // tk

    def body(a_ref, b_ref, o_ref, acc):
        k = pl.program_id(2)

        @pl.when(k == 0)
        def _():
            acc[...] = jnp.zeros_like(acc)

        acc[...] += _tn(a_ref[...], b_ref[...])

        @pl.when(k == nk - 1)
        def _():
            if pieces > 1:
                o_ref[0] = acc[...].astype(out_dtype)
            else:
                o_ref[...] = acc[...].astype(out_dtype)

    if pieces > 1:
        out_spec = pl.BlockSpec((1, tm, tn), lambda i, j, k: (j, i, 0))
        out_shape = jax.ShapeDtypeStruct((pieces, m, tn), out_dtype)
    else:
        out_spec = pl.BlockSpec((tm, tn), lambda i, j, k: (i, j))
        out_shape = jax.ShapeDtypeStruct((m, n), out_dtype)
    return pl.pallas_call(
        body, name=name, grid=(m // tm, n // tn, nk),
        in_specs=[pl.BlockSpec((tk, tm), lambda i, j, k: (k, i)), pl.BlockSpec((tk, tn), lambda i, j, k: (k, j))],
        out_specs=out_spec, out_shape=out_shape,
        scratch_shapes=[pltpu.VMEM((tm, tn), F32)],
        compiler_params=_params(("parallel", "parallel", "arbitrary")),
    )(a, b)


def _attn_probs(i, qk, slope, sink):
    rr = lax.broadcasted_iota(jnp.int32, (BLOCK, 2 * BLOCK), 0)
    jj = lax.broadcasted_iota(jnp.int32, (BLOCK, 2 * BLOCK), 1)
    diff = BLOCK + rr - jj
    valid = (diff >= 0) & (diff < WINDOW) & ((jj >= BLOCK) | (i > 0))
    s = qk * (HEAD_DIM ** -0.5)
    s = jnp.where(valid, s - slope * diff.astype(F32), NEG_INF)
    m = jnp.maximum(jnp.max(s, axis=1, keepdims=True), sink)
    p = jnp.exp(s - m)
    ps = jnp.exp(sink - m)
    inv = 1.0 / (jnp.sum(p, axis=1, keepdims=True) + ps)
    return p * inv, ps * inv


def _bands(kvp, kvc, h):
    kband = jnp.concatenate([kvp[:, h * HEAD_DIM:(h + 1) * HEAD_DIM], kvc[:, h * HEAD_DIM:(h + 1) * HEAD_DIM]], axis=0)
    v0 = KV_W + h * HEAD_DIM
    vband = jnp.concatenate([kvp[:, v0:v0 + HEAD_DIM], kvc[:, v0:v0 + HEAD_DIM]], axis=0)
    return kband, vband


def _attn_fwd(q, kv, sinks):
    seq = q.shape[0]
    nb = seq // BLOCK
    slopes = _alibi_slopes()

    def body(sink_ref, q_ref, kvp_ref, kvc_ref, o_ref):
        i = pl.program_id(0)
        qv, kvp, kvc = q_ref[...], kvp_ref[...], kvc_ref[...]
        bands = [_bands(kvp, kvc, h) for h in range(N_KV)]
        cols = [slice(hq * HEAD_DIM, (hq + 1) * HEAD_DIM) for hq in range(N_Q)]
        qk = [_nt(qv[:, cols[hq]], bands[hq // Q_PER_KV][0]) for hq in range(N_Q)]
        pr = [_attn_probs(i, qk[hq], slopes[hq], sink_ref[hq])[0].astype(BF16) for hq in range(N_Q)]
        for hq in range(N_Q):
            o_ref[:, cols[hq]] = _nn(pr[hq], bands[hq // Q_PER_KV][1])

    return pl.pallas_call(
        body, name="attn_fwd", grid=(nb,),
        in_specs=[pl.BlockSpec(memory_space=pltpu.SMEM),
                  pl.BlockSpec((BLOCK, ATTN_W), lambda i: (i, 0)),
                  pl.BlockSpec((BLOCK, 2 * KV_W), lambda i: (jnp.maximum(i - 1, 0), 0)),
                  pl.BlockSpec((BLOCK, 2 * KV_W), lambda i: (i, 0))],
        out_specs=pl.BlockSpec((BLOCK, ATTN_W), lambda i: (i, 0)),
        out_shape=jax.ShapeDtypeStruct((seq, ATTN_W), F32),
        compiler_params=_params(("parallel",)),
    )(sinks, q, kv, kv)


def _attn_bwd(q, kv, sinks, dout):
    seq = q.shape[0]
    nb = seq // BLOCK
    slopes = _alibi_slopes()
    scale = HEAD_DIM ** -0.5

    def body(sink_ref, q_ref, kvp_ref, kvc_ref, do_ref, dq_ref, dkv_ref, dsk_ref, prev):
        step = pl.program_id(0)
        i = nb - 1 - step

        @pl.when(step == 0)
        def _():
            prev[...] = jnp.zeros_like(prev)

        qv, kvp, kvc = q_ref[...], kvp_ref[...], kvc_ref[...]
        dov = do_ref[...].astype(BF16)
        bands = [_bands(kvp, kvc, h) for h in range(N_KV)]
        cols = [slice(hq * HEAD_DIM, (hq + 1) * HEAD_DIM) for hq in range(N_Q)]
        qk = [_nt(qv[:, cols[hq]], bands[hq // Q_PER_KV][0]) for hq in range(N_Q)]
        dp = [_nt(dov[:, cols[hq]], bands[hq // Q_PER_KV][1]) for hq in range(N_Q)]
        prb, dsb, dsk = [], [], []
        for hq in range(N_Q):
            pr, ps = _attn_probs(i, qk[hq], slopes[hq], sink_ref[hq])
            delta = jnp.sum(pr * dp[hq], axis=1, keepdims=True)
            dsb.append((pr * (dp[hq] - delta) * scale).astype(BF16))
            prb.append(pr.astype(BF16))
            dsk.append(jnp.broadcast_to(-_colsum(ps * delta), (1, LANE)))
        for hq in range(N_Q):
            dq_ref[:, cols[hq]] = _nn(dsb[hq], bands[hq // Q_PER_KV][0])
        dk, dv = [], []
        for h in range(N_KV):
            heads = range(h * Q_PER_KV, (h + 1) * Q_PER_KV)
            dk.append(sum(_tn(dsb[hq], qv[:, cols[hq]]) for hq in heads))
            dv.append(sum(_tn(prb[hq], dov[:, cols[hq]]) for hq in heads))
        band = jnp.concatenate(dk + dv, axis=1)
        dkv_ref[...] = band[BLOCK:, :] + prev[...]
        prev[...] = band[:BLOCK, :]
        dsk_ref[0] = jnp.concatenate(dsk, axis=0)

    return pl.pallas_call(
        body, name="attn_bwd", grid=(nb,),
        in_specs=[pl.BlockSpec(memory_space=pltpu.SMEM),
                  pl.BlockSpec((BLOCK, ATTN_W), lambda s: (nb - 1 - s, 0)),
                  pl.BlockSpec((BLOCK, 2 * KV_W), lambda s: (jnp.maximum(nb - 2 - s, 0), 0)),
                  pl.BlockSpec((BLOCK, 2 * KV_W), lambda s: (nb - 1 - s, 0)),
                  pl.BlockSpec((BLOCK, ATTN_W), lambda s: (nb - 1 - s, 0))],
        out_specs=[pl.BlockSpec((BLOCK, ATTN_W), lambda s: (nb - 1 - s, 0)),
                   pl.BlockSpec((BLOCK, 2 * KV_W), lambda s: (nb - 1 - s, 0)),
                   pl.BlockSpec((1, N_Q, LANE), lambda s: (nb - 1 - s, 0, 0))],
        out_shape=[jax.ShapeDtypeStruct((seq, ATTN_W), F32), jax.ShapeDtypeStruct((seq, 2 * KV_W), F32),
                   jax.ShapeDtypeStruct((nb, N_Q, LANE), F32)],
        scratch_shapes=[pltpu.VMEM((BLOCK, 2 * KV_W), F32)],
        compiler_params=_params(("arbitrary",)),
    )(sinks, q, kv, kv, dout)


def _discretize(lr, li, ldt, br, bi):
    dt = jnp.exp(ldt)
    mag = jnp.exp(lr * dt)
    ang = li * dt
    ab_r = mag * jnp.cos(ang)
    ab_i = mag * jnp.sin(ang)
    nr = ab_r - 1.0
    ni = ab_i
    den = lr * lr + li * li
    f_r = (nr * lr + ni * li) / den
    f_i = (ni * lr - nr * li) / den
    return ab_r, ab_i, f_r * br - f_i * bi, f_r * bi + f_i * br


def _ssm_prepare(lr, li, ldt, br, bi):
    n = lr.shape[0]
    tn = N_CH
    col = pl.BlockSpec((tn, 1), lambda i: (i, 0))
    mat = pl.BlockSpec((tn, GROUP_W), lambda i: (i, 0))

    def body(lr_ref, li_ref, ldt_ref, br_ref, bi_ref, ar_ref, ai_ref, bbr_ref, bbi_ref):
        ar, ai, bbr, bbi = _discretize(lr_ref[...], li_ref[...], ldt_ref[...], br_ref[...], bi_ref[...])
        ar_ref[...] = ar
        ai_ref[...] = ai
        bbr_ref[...] = bbr
        bbi_ref[...] = bbi

    cs = jax.ShapeDtypeStruct((n, 1), F32)
    ms = jax.ShapeDtypeStruct((n, GROUP_W), F32)
    return pl.pallas_call(
        body, name="ssm_prepare", grid=(n // tn,),
        in_specs=[col, col, col, mat, mat], out_specs=[col, col, mat, mat], out_shape=[cs, cs, ms, ms],
        compiler_params=_params(("parallel",)),
    )(lr, li, ldt, br, bi)


def _ssm_prepare_bwd(lr, li, ldt, br, bi, dar, dai, dbbr, dbbi):
    n = lr.shape[0]
    tn = N_CH
    col = pl.BlockSpec((tn, 1), lambda i: (i, 0))
    mat = pl.BlockSpec((tn, GROUP_W), lambda i: (i, 0))

    def body(lr_ref, li_ref, ldt_ref, br_ref, bi_ref, dar_ref, dai_ref, dbbr_ref, dbbi_ref,
             dlr_ref, dli_ref, dldt_ref, dbr_ref, dbi_ref):
        _, vjp = jax.vjp(_discretize, lr_ref[...], li_ref[...], ldt_ref[...], br_ref[...], bi_ref[...])
        dlr, dli, dldt, dbr, dbi = vjp((dar_ref[...], dai_ref[...], dbbr_ref[...], dbbi_ref[...]))
        dlr_ref[...] = dlr
        dli_ref[...] = dli
        dldt_ref[...] = dldt
        dbr_ref[...] = dbr
        dbi_ref[...] = dbi

    cs = jax.ShapeDtypeStruct((n, 1), F32)
    ms = jax.ShapeDtypeStruct((n, GROUP_W), F32)
    return pl.pallas_call(
        body, name="ssm_prepare_bwd", grid=(n // tn,),
        in_specs=[col, col, col, mat, mat, col, col, mat, mat],
        out_specs=[col, col, col, mat, mat], out_shape=[cs, cs, cs, ms, ms],
        compiler_params=_params(("parallel",)),
    )(lr, li, ldt, br, bi, dar, dai, dbbr, dbbi)


def _load_slabs(src4_ref, dst):
    for s in range(STEPS):
        dst[s * SEGS:(s + 1) * SEGS, :] = jnp.concatenate(
            [src4_ref[j, pl.ds(s, SEGS, stride=STEPS), :] for j in range(4)], axis=1)


def _store_slabs(src, dst4_ref):
    for s in range(STEPS):
        for j in range(4):
            dst4_ref[j, pl.ds(s, SEGS, stride=STEPS), :] = src[s * SEGS:(s + 1) * SEGS, j * LANE:(j + 1) * LANE]


def _power_table(ar_ref, ai_ref, pwr, pwi):
    ar, ai = ar_ref[0], ai_ref[0]
    pr, pi = ar, ai
    pwr[0:1, :] = pr
    pwi[0:1, :] = pi
    for k in range(1, STEPS):
        pr, pi = pr * ar - pi * ai, pr * ai + pi * ar
        pwr[k:k + 1, :] = pr
        pwi[k:k + 1, :] = pi


def _scan_states(ubf, ar_ref, ai_ref, bbr_ref, bbi_ref, pwr, pwi, cin_r, cin_i, hr, hi):
    for k in range(2):
        rows = slice(k * 256, (k + 1) * 256)
        cols = slice(k * HALF_CH, (k + 1) * HALF_CH)
        hr[:, cols] = _nn(ubf[:, rows], bbr_ref[0, k])
        hi[:, cols] = _nn(ubf[:, rows], bbi_ref[0, k])
    for st in range(N_CH // STRIP):
        cs = slice(st * STRIP, (st + 1) * STRIP)
        arb = jnp.broadcast_to(ar_ref[0, :, cs], (SEGS, STRIP))
        aib = jnp.broadcast_to(ai_ref[0, :, cs], (SEGS, STRIP))

        def step(s, carry, cs=cs, arb=arb, aib=aib):
            cr, ci = carry
            rows = pl.ds(pl.multiple_of(s * SEGS, SEGS), SEGS)
            nr = arb * cr - aib * ci + hr[rows, cs]
            ni = arb * ci + aib * cr + hi[rows, cs]
            hr[rows, cs] = nr
            hi[rows, cs] = ni
            return nr, ni

        zero = jnp.zeros((SEGS, STRIP), F32)
        lax.fori_loop(0, STEPS, step, (zero, zero), unroll=True)
    last = slice((STEPS - 1) * SEGS, STEPS * SEGS)
    end_r, end_i = hr[last, :], hi[last, :]
    a64r, a64i = pwr[STEPS - 1:STEPS, :], pwi[STEPS - 1:STEPS, :]
    cr, ci = cin_r, cin_i
    rows_r, rows_i = [], []
    for j in range(SEGS):
        rows_r.append(cr)
        rows_i.append(ci)
        cr, ci = (a64r * cr - a64i * ci + end_r[j:j + 1, :], a64r * ci + a64i * cr + end_i[j:j + 1, :])
    cm_r, cm_i = jnp.concatenate(rows_r, axis=0), jnp.concatenate(rows_i, axis=0)
    for st in range(N_CH // STRIP):
        cs = slice(st * STRIP, (st + 1) * STRIP)
        cmr, cmi = cm_r[:, cs], cm_i[:, cs]

        def fix(s, carry, cs=cs, cmr=cmr, cmi=cmi):
            rows = pl.ds(pl.multiple_of(s * SEGS, SEGS), SEGS)
            pr, pi = pwr[pl.ds(s, 1), cs], pwi[pl.ds(s, 1), cs]
            hr[rows, cs] = hr[rows, cs] + (pr * cmr - pi * cmi)
            hi[rows, cs] = hi[rows, cs] + (pr * cmi + pi * cmr)
            return carry

        lax.fori_loop(0, STEPS, fix, 0, unroll=True)
    return (cm_r, cm_i), (cr, ci)


def _ssm_outputs(u, hr, hi, crt_ref, cit_ref, vec512_ref, wg_ref):
    ys = []
    for k in range(2):
        cols = slice(k * HALF_CH, (k + 1) * HALF_CH)
        ys.append(_nn(hr[:, cols].astype(BF16), crt_ref[0, k]) - _nn(hi[:, cols].astype(BF16), cit_ref[0, k]))
    y = jnp.concatenate(ys, axis=1) + vec512_ref[0, H_DSKIP:H_DSKIP + 1, :] * u
    z, t = _gelu(y)
    gate = jax.nn.sigmoid(_nn(z.astype(BF16), wg_ref[...]) + vec512_ref[0, H_BGLU:H_BGLU + 1, :])
    return y, z, t, gate


def _ssm_specs(layer, nck, rev):
    def chunk(i):
        return nck - 1 - i if rev else i

    return [pl.BlockSpec((4, CHUNK, LANE), lambda i: (0, chunk(i), 0)),
            pl.BlockSpec((1, 1, N_CH), lambda i: (layer, 0, 0)),
            pl.BlockSpec((1, 1, N_CH), lambda i: (layer, 0, 0)),
            pl.BlockSpec((1, 2, 256, HALF_CH), lambda i: (layer, 0, 0, 0)),
            pl.BlockSpec((1, 2, 256, HALF_CH), lambda i: (layer, 0, 0, 0)),
            pl.BlockSpec((1, 2, HALF_CH, 256), lambda i: (layer, 0, 0, 0)),
            pl.BlockSpec((1, 2, HALF_CH, 256), lambda i: (layer, 0, 0, 0)),
            pl.BlockSpec((1, 8, SSM_W), lambda i: (layer, 0, 0)),
            pl.BlockSpec((SSM_W, SSM_W), lambda i: (0, 0))]


def _ssm_fwd(u4, a_r, a_i, bb_r, bb_i, c_rt, c_it, vec512, w_glu, layer):
    seq = u4.shape[1]
    nck = seq // CHUNK

    def body(u4_ref, ar_ref, ai_ref, bbr_ref, bbi_ref, crt_ref, cit_ref, vec512_ref, wg_ref,
             s4_ref, cmr_ref, cmi_ref, hr, hi, pwr, pwi, car, cai, ubuf, obuf):
        i = pl.program_id(0)

        @pl.when(i == 0)
        def _():
            car[...] = jnp.zeros_like(car)
            cai[...] = jnp.zeros_like(cai)
            _power_table(ar_ref, ai_ref, pwr, pwi)

        _load_slabs(u4_ref, ubuf)
        u = ubuf[...]
        (cm_r, cm_i), (er, ei) = _scan_states(u.astype(BF16), ar_ref, ai_ref, bbr_ref, bbi_ref, pwr, pwi,
                                              car[...], cai[...], hr, hi)
        cmr_ref[0] = cm_r
        cmi_ref[0] = cm_i
        car[...] = er
        cai[...] = ei
        _, z, _, gate = _ssm_outputs(u, hr, hi, crt_ref, cit_ref, vec512_ref, wg_ref)
        obuf[...] = z * gate
        _store_slabs(obuf, s4_ref)

    return pl.pallas_call(
        body, name="ssm_fwd", grid=(nck,),
        in_specs=_ssm_specs(layer, nck, False),
        out_specs=[pl.BlockSpec((4, CHUNK, LANE), lambda i: (0, i, 0)),
                   pl.BlockSpec((1, SEGS, N_CH), lambda i: (i, 0, 0)),
                   pl.BlockSpec((1, SEGS, N_CH), lambda i: (i, 0, 0)),
                   pl.BlockSpec((CHUNK, N_CH), lambda i: (i, 0)),
                   pl.BlockSpec((CHUNK, N_CH), lambda i: (i, 0))],
        out_shape=[jax.ShapeDtypeStruct((4, seq, LANE), F32), jax.ShapeDtypeStruct((nck, SEGS, N_CH), F32),
                   jax.ShapeDtypeStruct((nck, SEGS, N_CH), F32), jax.ShapeDtypeStruct((seq, N_CH), F32),
                   jax.ShapeDtypeStruct((seq, N_CH), F32)],
        scratch_shapes=[pltpu.VMEM((STEPS, N_CH), F32), pltpu.VMEM((STEPS, N_CH), F32),
                        pltpu.VMEM((1, N_CH), F32), pltpu.VMEM((1, N_CH), F32),
                        pltpu.VMEM((CHUNK, SSM_W), F32), pltpu.VMEM((CHUNK, SSM_W), F32)],
        compiler_params=_params(("arbitrary",)),
    )(u4, a_r, a_i, bb_r, bb_i, c_rt, c_it, vec512, w_glu)


def _ssm_bwd(u4, ds4, cm_r, cm_i, h_r, h_i, a_r, a_i, bb_r, bb_i, c_rt, c_it, vec512, w_glu, layer):
    seq = u4.shape[1]
    nck = seq // CHUNK

    def body(u4_ref, ar_ref, ai_ref, bbr_ref, bbi_ref, crt_ref, cit_ref, vec512_ref, wg_ref, ds4_ref, cmr_ref, cmi_ref,
             hr, hi, du4_ref, dbbr_out, dbbi_out, dcrt_out, dcit_out, dar_ref, dai_ref, dwg_ref, dvec_ref,
             gr, gi, pwr, pwi, gcr, gci, accr, acci, ubuf, dbuf, dbbr_ref, dbbi_ref, dcrt_ref, dcit_ref):
        i = pl.program_id(0)

        @pl.when(i == 0)
        def _():
            for ref in (gcr, gci, accr, acci, dbbr_ref, dbbi_ref, dcrt_ref, dcit_ref, dwg_ref, dvec_ref):
                ref[...] = jnp.zeros_like(ref)
            _power_table(ar_ref, ai_ref, pwr, pwi)

        _load_slabs(u4_ref, ubuf)
        u = ubuf[...]
        ubf = u.astype(BF16)
        cm_r, cm_i = cmr_ref[0], cmi_ref[0]
        y, z, t, gate = _ssm_outputs(u, hr, hi, crt_ref, cit_ref, vec512_ref, wg_ref)
        _load_slabs(ds4_ref, dbuf)
        ds = dbuf[...]
        da = ds * z * gate * (1.0 - gate)
        dab = da.astype(BF16)
        dz = ds * gate + _nt(dab, wg_ref[...])
        dwg_ref[...] += _tn(z.astype(BF16), dab)
        dvec_ref[H_BGLU:H_BGLU + 1, :] += _colsum(da)
        dy = dz * _gelu_grad(y, t)
        dvec_ref[H_DSKIP:H_DSKIP + 1, :] += _colsum(dy * u)
        du_skip = dy * vec512_ref[0, H_DSKIP:H_DSKIP + 1, :]
        dyb = dy.astype(BF16)
        for k in range(2):
            rows = slice(k * 256, (k + 1) * 256)
            cols = slice(k * HALF_CH, (k + 1) * HALF_CH)
            dcrt_ref[k] += _tn(hr[:, cols].astype(BF16), dyb[:, rows])
            dcit_ref[k] -= _tn(hi[:, cols].astype(BF16), dyb[:, rows])
            gr[:, cols] = _nt(dyb[:, rows], crt_ref[0, k])
            gi[:, cols] = -_nt(dyb[:, rows], cit_ref[0, k])
        for st in range(N_CH // STRIP):
            cs = slice(st * STRIP, (st + 1) * STRIP)
            arb = jnp.broadcast_to(ar_ref[0, :, cs], (SEGS, STRIP))
            aib = jnp.broadcast_to(ai_ref[0, :, cs], (SEGS, STRIP))

            def step(k, carry, cs=cs, arb=arb, aib=aib):
                cr, ci = carry
                rows = pl.ds(pl.multiple_of((STEPS - 1 - k) * SEGS, SEGS), SEGS)
                nr = gr[rows, cs] + (arb * cr + aib * ci)
                ni = gi[rows, cs] + (arb * ci - aib * cr)
                gr[rows, cs] = nr
                gi[rows, cs] = ni
                return nr, ni

            zero = jnp.zeros((SEGS, STRIP), F32)
            lax.fori_loop(0, STEPS, step, (zero, zero), unroll=True)
        first_r, first_i = gr[0:SEGS, :], gi[0:SEGS, :]
        a64r, a64i = pwr[STEPS - 1:STEPS, :], pwi[STEPS - 1:STEPS, :]
        dr_, di_ = gcr[...], gci[...]
        rows_r, rows_i = [None] * SEGS, [None] * SEGS
        for j in range(SEGS - 1, -1, -1):
            rows_r[j], rows_i[j] = dr_, di_
            dr_, di_ = (first_r[j:j + 1, :] + (a64r * dr_ + a64i * di_), first_i[j:j + 1, :] + (a64r * di_ - a64i * dr_))
        gcr[...] = dr_
        gci[...] = di_
        dm_r, dm_i = jnp.concatenate(rows_r, axis=0), jnp.concatenate(rows_i, axis=0)
        for st in range(N_CH // STRIP):
            cs = slice(st * STRIP, (st + 1) * STRIP)
            dmr, dmi = dm_r[:, cs], dm_i[:, cs]

            def fixed(s, cs=cs, dmr=dmr, dmi=dmi):
                rows = pl.ds(pl.multiple_of(s * SEGS, SEGS), SEGS)
                pr, pi = pwr[pl.ds(STEPS - 1 - s, 1), cs], pwi[pl.ds(STEPS - 1 - s, 1), cs]
                g_r = gr[rows, cs] + (pr * dmr + pi * dmi)
                g_i = gi[rows, cs] + (pr * dmi - pi * dmr)
                gr[rows, cs] = g_r
                gi[rows, cs] = g_i
                return g_r, g_i

            g_r, g_i = fixed(jnp.int32(0))
            acc0 = (g_r * cm_r[:, cs] + g_i * cm_i[:, cs], g_i * cm_r[:, cs] - g_r * cm_i[:, cs])

            def step(s, carry, cs=cs, fixed=fixed):
                sr, si = carry
                g_r, g_i = fixed(s)
                prev = pl.ds(pl.multiple_of((s - 1) * SEGS, SEGS), SEGS)
                hpr, hpi = hr[prev, cs], hi[prev, cs]
                return sr + (g_r * hpr + g_i * hpi), si + (g_i * hpr - g_r * hpi)

            sr, si = lax.fori_loop(1, STEPS, step, acc0, unroll=True)
            accr[:, cs] += sr
            acci[:, cs] += si
        grb, gib = gr[...].astype(BF16), gi[...].astype(BF16)
        dus = []
        for k in range(2):
            rows = slice(k * 256, (k + 1) * 256)
            cols = slice(k * HALF_CH, (k + 1) * HALF_CH)
            dus.append(_nt(grb[:, cols], bbr_ref[0, k]) + _nt(gib[:, cols], bbi_ref[0, k]))
            dbbr_ref[k] += _tn(ubf[:, rows], grb[:, cols])
            dbbi_ref[k] += _tn(ubf[:, rows], gib[:, cols])
        dbuf[...] = jnp.concatenate(dus, axis=1) + du_skip
        _store_slabs(dbuf, du4_ref)

        @pl.when(i == nck - 1)
        def _():
            dar_ref[...] = _colsum(accr[...])
            dai_ref[...] = _colsum(acci[...])
            ng = N_GROUPS // 2
            for k in range(2):
                for acc, out in ((dbbr_ref, dbbr_out), (dbbi_ref, dbbi_out)):
                    out[k] = jnp.concatenate(
                        [acc[k, g * GROUP_W:(g + 1) * GROUP_W, g * STATE:(g + 1) * STATE] for g in range(ng)], axis=0)
                for acc, out in ((dcrt_ref, dcrt_out), (dcit_ref, dcit_out)):
                    out[k] = jnp.concatenate(
                        [acc[k, g * STATE:(g + 1) * STATE, g * GROUP_W:(g + 1) * GROUP_W] for g in range(ng)], axis=0)

    rev4 = pl.BlockSpec((4, CHUNK, LANE), lambda i: (0, nck - 1 - i, 0))
    hc_spec = pl.BlockSpec((1, SEGS, N_CH), lambda i: (nck - 1 - i, 0, 0))
    h_spec = pl.BlockSpec((CHUNK, N_CH), lambda i: (nck - 1 - i, 0))
    fixed2 = lambda shape: pl.BlockSpec(shape, lambda i: (0,) * len(shape))
    return pl.pallas_call(
        body, name="ssm_bwd", grid=(nck,),
        in_specs=_ssm_specs(layer, nck, True) + [rev4, hc_spec, hc_spec, h_spec, h_spec],
        out_specs=[rev4, fixed2((2, 256, STATE)), fixed2((2, 256, STATE)), fixed2((2, HALF_CH, GROUP_W)),
                   fixed2((2, HALF_CH, GROUP_W)), fixed2((1, N_CH)), fixed2((1, N_CH)), fixed2((SSM_W, SSM_W)),
                   fixed2((8, SSM_W))],
        out_shape=[jax.ShapeDtypeStruct((4, seq, LANE), F32),
                   jax.ShapeDtypeStruct((2, 256, STATE), F32), jax.ShapeDtypeStruct((2, 256, STATE), F32),
                   jax.ShapeDtypeStruct((2, HALF_CH, GROUP_W), F32), jax.ShapeDtypeStruct((2, HALF_CH, GROUP_W), F32),
                   jax.ShapeDtypeStruct((1, N_CH), F32), jax.ShapeDtypeStruct((1, N_CH), F32),
                   jax.ShapeDtypeStruct((SSM_W, SSM_W), F32), jax.ShapeDtypeStruct((8, SSM_W), F32)],
        scratch_shapes=[pltpu.VMEM((CHUNK, N_CH), F32), pltpu.VMEM((CHUNK, N_CH), F32),
                        pltpu.VMEM((STEPS, N_CH), F32), pltpu.VMEM((STEPS, N_CH), F32),
                        pltpu.VMEM((1, N_CH), F32), pltpu.VMEM((1, N_CH), F32),
                        pltpu.VMEM((SEGS, N_CH), F32), pltpu.VMEM((SEGS, N_CH), F32),
                        pltpu.VMEM((CHUNK, SSM_W), F32), pltpu.VMEM((CHUNK, SSM_W), F32),
                        pltpu.VMEM((2, 256, HALF_CH), F32), pltpu.VMEM((2, 256, HALF_CH), F32),
                        pltpu.VMEM((2, HALF_CH, 256), F32), pltpu.VMEM((2, HALF_CH, 256), F32)],
        compiler_params=_params(("arbitrary",)),
    )(u4, a_r, a_i, bb_r, bb_i, c_rt, c_it, vec512, w_glu, ds4, cm_r, cm_i, h_r, h_i)


def _block_diag(t):
    nl, _, ng, a, b = t.shape
    eye = jnp.eye(ng, dtype=t.dtype)
    return jnp.einsum("gh,lkgab->lkgahb", eye, t).reshape(nl, 2, ng * a, ng * b)


def _local_step(x, loss_target, mod, p, comm):
    nl = mod.shape[0]
    pad1024 = jnp.zeros((nl, 16 - 10, D_MODEL), F32)
    vec = jnp.concatenate([mod.reshape(nl, N_MOD, D_MODEL), p["pre_mix_g"][:, None], p["post_mix_g"][:, None],
                           p["pre_mlp_g"][:, None], p["post_mlp_g"][:, None], pad1024], axis=1)
    vec512 = jnp.concatenate([p["attn_out_g"][:, None], p["ssm_out_g"][:, None], p["d_skip"][:, None],
                              p["b_glu"][:, None], jnp.zeros((nl, 4, SSM_W), F32)], axis=1)
    n_all = nl * N_CH
    lr = p["lam_re"].reshape(n_all, 1)
    li = p["lam_im"].reshape(n_all, 1)
    ldt = jnp.broadcast_to(p["log_dt"][:, :, None], (nl, N_GROUPS, STATE)).reshape(n_all, 1)
    br = p["b_re"].reshape(n_all, GROUP_W)
    bi = p["b_im"].reshape(n_all, GROUP_W)
    ab_r, ab_i, bb_r, bb_i = _ssm_prepare(lr, li, ldt, br, bi)
    a_r = ab_r.reshape(nl, 1, N_CH)
    a_i = ab_i.reshape(nl, 1, N_CH)

    def dense_b(bb):
        return _block_diag(bb.reshape(nl, 2, 16, STATE, GROUP_W).transpose(0, 1, 2, 4, 3)).astype(BF16)

    def dense_c(cc):
        return _block_diag(cc.reshape(nl, 2, 16, GROUP_W, STATE).transpose(0, 1, 2, 4, 3)).astype(BF16)

    bbr_d, bbi_d = dense_b(bb_r), dense_b(bb_i)
    crt_d, cit_d = dense_c(p["c_re"]), dense_c(p["c_im"])

    saved = []
    xl = x
    mixer_w, mlp_w = [None] * nl, [None] * nl
    for l in range(nl):
        mixer_w[l], tok = comm.mixer_weights(l, [xl, bbr_d, bbi_d, crt_d, cit_d] if l == 0 else xl)
        w_in_t, w_glu, w_out = mixer_w[l]
        q, kv, u4, h1 = _in_proj_fwd(xl, _after(vec, *tok), w_in_t, l)
        attn = _attn_fwd(q, kv, p["attn_sinks"][l])
        s4, *states = _ssm_fwd(u4, a_r, a_i, bbr_d, bbi_d, crt_d, cit_d, vec512, w_glu, l)
        x1 = _out_proj_fwd(xl, attn, s4, vec, vec512, w_out, l)
        mlp_w[l] = comm.mlp_weights(l, x1)
        x2, r, f = _mlp_fwd(x1, vec, mlp_w[l][0], mlp_w[l][1], l)
        saved.append((xl, q, kv, u4, h1, attn, s4, states, x1, r, f))
        xl = x2

    dx, loss_parts = _loss_head(xl, loss_target)
    loss = jnp.sum(loss_parts[:, 0, 0])

    dvec_l, dvec512_l, dsink_l = [None] * nl, [None] * nl, [None] * nl
    dab_r, dab_i, dbb_r, dbb_i, dc_re, dc_im = ([None] * nl for _ in range(6))
    toks = []
    for l in range(nl - 1, -1, -1):
        xl, q, kv, u4, h1, attn, s4, states, x1, r, f = saved[l]
        w_in_t, w_glu, w_out = mixer_w[l]
        dx1, h2, da, df, dvec_m = _mlp_bwd(dx, x1, r, f, _after(vec, *toks), mlp_w[l][0], mlp_w[l][1], l)
        toks = comm.after_mlp_bwd(l, dx1)
        dw_mlp_out = _matmul_tn(r, df, BF16, "dw_mlp_out").reshape(4, D_FF // 4, D_MODEL)
        dw_mlp_in = _matmul_tn(h2, da, BF16, "dw_mlp_in", pieces=4)
        toks = toks + comm.mlp_grads(l, [dw_mlp_in, dw_mlp_out])
        dattn, ds4, heads, dmixed, dvec_o, dvec512_o = _out_proj_bwd(
            dx1, attn, s4, _after(vec, *toks), vec512, w_out, l)
        dw_out = _matmul_tn(heads, dmixed, BF16, "dw_out").reshape(4, D_MODEL // 4, D_MODEL)
        dq, dkv, dsk = _attn_bwd(q, kv, p["attn_sinks"][l], dattn)
        (du4, dbbr, dbbi, dcrt, dcit, dar, dai, dwg, dvec512_s) = _ssm_bwd(
            u4, ds4, *states, a_r, a_i, bbr_d, bbi_d, crt_d, cit_d, vec512, w_glu, l)
        dw_glu = dwg.astype(BF16).reshape(4, SSM_W // 4, SSM_W)
        dx, dproj, dvec_i = _in_proj_bwd(dx1, dq, dkv, du4, xl, vec, w_in_t, l)
        toks = comm.after_in_proj_bwd(l, dx)
        dw_in = _matmul_tn(h1, dproj, BF16, "dw_in")
        toks = toks + comm.mixer_grads(l, [dw_in.reshape(D_MODEL, 4, IN_W // 4).transpose(1, 0, 2), dw_glu, dw_out])
        dvec_l[l] = dvec_m + dvec_o + dvec_i
        dvec512_l[l] = dvec512_o + dvec512_s
        dsink_l[l] = jnp.sum(dsk[:, :, 0], axis=0)
        dab_r[l], dab_i[l], dbb_r[l], dbb_i[l], dc_re[l], dc_im[l] = dar, dai, dbbr, dbbi, dcrt, dcit

    dvec = _after(jnp.stack(dvec_l), *toks)
    dvec512 = jnp.stack(dvec512_l)
    ng = N_GROUPS // 2

    def b_cols(d):
        return jnp.stack(d).reshape(nl, 2, ng, GROUP_W, STATE).transpose(0, 1, 2, 4, 3).reshape(n_all, GROUP_W)

    def c_param(d):
        return jnp.stack(d).reshape(nl, 2, ng, STATE, GROUP_W).transpose(0, 1, 2, 4, 3).reshape(c_shape)

    dbb_r_c, dbb_i_c = b_cols(dbb_r), b_cols(dbb_i)
    c_shape = (nl, N_GROUPS, GROUP_W, STATE)
    dlr, dli, dldt, dbr, dbi = _ssm_prepare_bwd(
        lr, li, ldt, br, bi, jnp.stack(dab_r).reshape(n_all, 1), jnp.stack(dab_i).reshape(n_all, 1), dbb_r_c, dbb_i_c)
    small = {
        "b_ada": dvec[:, :N_MOD].reshape(nl, N_MOD * D_MODEL),
        "pre_mix_g": dvec[:, V_PRE_MIX], "post_mix_g": dvec[:, V_POST_MIX],
        "pre_mlp_g": dvec[:, V_PRE_MLP], "post_mlp_g": dvec[:, V_POST_MLP],
        "attn_sinks": jnp.stack(dsink_l),
        "lam_re": dlr.reshape(nl, N_GROUPS, STATE), "lam_im": dli.reshape(nl, N_GROUPS, STATE),
        "log_dt": jnp.sum(dldt.reshape(nl, N_GROUPS, STATE), axis=-1),
        "b_re": dbr.reshape(nl, N_GROUPS, STATE, GROUP_W), "b_im": dbi.reshape(nl, N_GROUPS, STATE, GROUP_W),
        "c_re": c_param(dc_re), "c_im": c_param(dc_im),
        "d_skip": dvec512[:, H_DSKIP], "b_glu": dvec512[:, H_BGLU],
        "attn_out_g": dvec512[:, H_ATTN_G], "ssm_out_g": dvec512[:, H_SSM_G],
    }
    return loss, dx, small, small["b_ada"]


WEIGHTS = ["w_ada", "b_ada", "pre_mix_g", "w_in", "attn_sinks", "lam_re", "lam_im", "log_dt", "b_re", "b_im", "c_re",
           "c_im", "d_skip", "w_glu", "b_glu", "attn_out_g", "ssm_out_g", "w_out", "post_mix_g", "pre_mlp_g",
           "w_mlp_in", "w_mlp_out", "post_mlp_g"]
BIG = ["w_in", "w_glu", "w_out", "w_mlp_in", "w_mlp_out"]
SMALL = [n for n in WEIGHTS if n not in BIG and n != "w_ada"]
PACK_ROWS = 256


def _pack(parts):
    rows = []
    for n in SMALL:
        flat = parts[n].reshape(-1)
        pad = (-flat.shape[0]) % (PACK_ROWS * LANE)
        rows.append(jnp.pad(flat, (0, pad)).reshape(-1, LANE))
    return jnp.concatenate(rows, axis=0)


def _unpack(packed, shapes):
    out, r0 = {}, 0
    for n in SMALL:
        size = int(np.prod(shapes[n]))
        rows = -(-size // (PACK_ROWS * LANE)) * PACK_ROWS
        out[n] = packed[r0:r0 + rows].reshape(-1)[:size].reshape(shapes[n])
        r0 += rows
    return out


MIXER = ["w_in", "w_glu", "w_out"]
MLP = ["w_mlp_in", "w_mlp_out"]


class _Exchanges:
    def __init__(self, shards, wts, mom, var, chip):
        self.shards, self.wts, self.mom, self.var, self.chip = shards, wts, mom, var, chip
        self.chip_arr = jnp.reshape(chip, (1,)).astype(jnp.int32)
        self.nl = len(shards["w_in"])
        self.gathers, self.scatters, self.pairs = {}, {}, {}
        self.res = {n: None for n in BIG}

    def _start_gather(self, group, tag, l, after=()):
        srcs = [self.shards[n][l] for n in group]
        lands = [lax.dynamic_update_slice(lax.empty((4,) + s.shape, s.dtype), s[None], (self.chip, 0, 0)) for s in srcs]
        plan = _plan_gather(len(srcs))
        st = _exchange_start(f"gather_{tag}{l}_start", 3 * len(srcs), plan, srcs + lands, after)
        self.gathers[tag, l] = (plan, st)
        return st[3]

    def _wait_gather(self, tag, l, after):
        plan, st = self.gathers.pop((tag, l))
        n = len(st[2]) // 2
        bufs = _exchange_wait(f"gather_{tag}{l}_wait", 3 * n, plan, st, after)
        return [b.reshape(4 * b.shape[1], b.shape[2]) for b in bufs[n:]]

    def _start_layer(self, l, after):
        tok = self._start_gather(MIXER, "mixer", l, after)
        return [tok, self._start_gather(MLP, "mlp", l, [tok])]

    def begin(self, after):
        toks = self._start_layer(0, after)
        return toks + (self._start_layer(1, toks[1:]) if self.nl > 1 else [])

    def mixer_weights(self, l, after):
        w = self._wait_gather("mixer", l, after)
        toks = self._start_layer(l + 2, w[:1]) if l + 2 < self.nl else []
        return w, toks

    def mlp_weights(self, l, after):
        return self._wait_gather("mlp", l, after)

    def _start_scatter(self, tag, l, group, pieces):
        plan = _plan_scatter(len(pieces))
        st = _exchange_start(f"scatter_{tag}{l}_start", 3 * len(pieces), plan,
                             list(pieces) + [lax.empty(p.shape, p.dtype) for p in pieces])
        self.scatters[tag] = (l, group, plan, st)
        return [st[3]]

    def _finish_scatter(self, tag, after):
        l, group, plan, st = self.scatters.pop(tag)
        n = len(group)
        bufs = _exchange_wait(f"scatter_{tag}{l}_wait", 3 * n, plan, st, after)
        sums = [_sum_pieces(bufs[k], bufs[n + k], self.chip_arr, "sum_" + group[k]) for k in range(n)]
        plan2 = _plan_pair(n)
        st2 = _exchange_start(f"pair_{tag}{l}_start", n, plan2, sums + [lax.empty(s.shape, s.dtype) for s in sums])
        self.pairs[tag] = (l, group, plan2, st2)
        return [st2[3]]

    def _finish_pair(self, tag, after):
        l, group, plan, st = self.pairs.pop(tag)
        n = len(group)
        bufs = _exchange_wait(f"pair_{tag}{l}_wait", n, plan, st, after)
        for k, name in enumerate(group):
            self.res[name] = _adamw_layer([bufs[k], bufs[n + k]], self.wts[name], self.mom[name], self.var[name],
                                          l, self.res[name], "adamw_" + name)

    def after_mlp_bwd(self, l, after):
        toks = self._finish_scatter("mixer", after) if "mixer" in self.scatters else []
        if "mlp" in self.pairs:
            self._finish_pair("mlp", after)
        return toks

    def mlp_grads(self, l, pieces):
        return self._start_scatter("mlp", l, MLP, pieces)

    def after_in_proj_bwd(self, l, after):
        toks = self._finish_scatter("mlp", after)
        if "mixer" in self.pairs:
            self._finish_pair("mixer", after)
        return toks

    def mixer_grads(self, l, pieces):
        return self._start_scatter("mixer", l, MIXER, pieces)

    def finish_mixer_scatter(self, after):
        return self._finish_scatter("mixer", after)

    def finish_mlp(self, after):
        self._finish_pair("mlp", after)

    def finish_mixer(self, after):
        self._finish_pair("mixer", after)

    def results(self):
        return self.res


def kernel(x, c, w_ada, b_ada, pre_mix_g, w_in, attn_sinks, lam_re, lam_im, log_dt, b_re, b_im, c_re, c_im, d_skip, w_glu, b_glu, attn_out_g, ssm_out_g, w_out, post_mix_g, pre_mlp_g, w_mlp_in, w_mlp_out, post_mlp_g, loss_target, m_w_ada, m_b_ada, m_pre_mix_g, m_w_in, m_attn_sinks, m_lam_re, m_lam_im, m_log_dt, m_b_re, m_b_im, m_c_re, m_c_im, m_d_skip, m_w_glu, m_b_glu, m_attn_out_g, m_ssm_out_g, m_w_out, m_post_mix_g, m_pre_mlp_g, m_w_mlp_in, m_w_mlp_out, m_post_mlp_g, v_w_ada, v_b_ada, v_pre_mix_g, v_w_in, v_attn_sinks, v_lam_re, v_lam_im, v_log_dt, v_b_re, v_b_im, v_c_re, v_c_im, v_d_skip, v_w_glu, v_b_glu, v_attn_out_g, v_ssm_out_g, v_w_out, v_post_mix_g, v_pre_mlp_g, v_w_mlp_in, v_w_mlp_out, v_post_mlp_g):
    args = locals()
    wts = {n: args[n] for n in WEIGHTS}
    mom = {n: args["m_" + n] for n in WEIGHTS}
    var = {n: args["v_" + n] for n in WEIGHTS}
    nl = w_in.shape[0]
    ix, iy, ic = lax.axis_index("x"), lax.axis_index("y"), lax.axis_index("c")
    chip = 2 * ix + iy
    me = 4 * ix + 2 * iy + ic
    wcols = w_ada.shape[2]

    shards = {"w_in": [w_in[l].astype(BF16).T for l in range(nl)], "w_glu": [w_glu[l].astype(BF16) for l in range(nl)],
              "w_out": [w_out[l].astype(BF16) for l in range(nl)],
              "w_mlp_in": [w_mlp_in[l].astype(BF16).T for l in range(nl)],
              "w_mlp_out": [w_mlp_out[l].astype(BF16) for l in range(nl)]}
    comm = _Exchanges(shards, wts, mom, var, chip)

    c_all = _gather([c.reshape(1, 1, 1, D_MODEL)], "all", "gather_c")[0].reshape(8, D_MODEL)
    b_sh = lax.dynamic_slice(b_ada, (0, chip * wcols), (nl, wcols)).reshape(nl, 1, wcols)
    mod_sh = _ada_forward(c_all, w_ada, b_sh)
    mod_all = _gather([mod_sh.reshape(1, 1, nl * 8, wcols)], "chips", "gather_mod")[0]
    toks = comm.begin([mod_all])
    mod = lax.dynamic_index_in_dim(mod_all.reshape(4, nl, 8, wcols), me, axis=2, keepdims=False)
    mod = mod.transpose(1, 0, 2).reshape(nl, 4 * wcols)

    small_p = {n: wts[n] for n in SMALL}
    small_p["log_dt"] = _after(log_dt, *toks)
    loss, grad_x, small, dmod = _local_step(x[0], loss_target[0], mod, small_p, comm)
    loss = lax.psum(loss, ("x", "y", "c"))

    packed = _pack(small)
    rows = packed.shape[0]
    pair_plan = _plan_pair(1)
    pair_small = _exchange_start("pair_small_start", 1, pair_plan, [packed, lax.empty((rows, LANE), F32)])

    dmod = _after(dmod, pair_small[3])
    dmod_all = _gather([dmod.reshape(1, 1, nl, N_MOD * D_MODEL)], "all", "gather_dmod")[0][0]
    dmod_sh = lax.dynamic_slice(dmod_all, (0, 0, chip * wcols), (8, nl, wcols)).transpose(1, 0, 2)
    g_ada = _ada_weight_grad(c_all.T, dmod_sh)
    res = {"w_ada": _adamw(g_ada[:, None], w_ada, m_w_ada, v_w_ada, "adamw_w_ada")}

    comm.finish_mlp(res["w_ada"][0])
    toks = comm.finish_mixer_scatter(res["w_ada"][0])

    own, other = _exchange_wait("pair_small_wait", 1, pair_plan, pair_small, res["w_ada"][0])
    chip_sum = _after(_sum_list([own, other], "sum_pair_small"), *toks)
    quad_plan = _plan_gather(1)
    quad0 = lax.dynamic_update_slice(lax.empty((4, rows, LANE), F32), chip_sum[None], (chip, 0, 0))
    quad_small = _exchange_start("gather_small_start", 3, quad_plan, [chip_sum, quad0])
    comm.finish_mixer([comm.results()[n][0] for n in MLP] + [_after(dmod, quad_small[3])])
    res.update(comm.results())
    quad = _exchange_wait("gather_small_wait", 3, quad_plan, quad_small, [res[n][0] for n in BIG])[1]
    outs = _adamw(quad[None], _pack({n: wts[n] for n in SMALL})[None], _pack({n: mom[n] for n in SMALL})[None],
                  _pack({n: var[n] for n in SMALL})[None], "adamw_small")
    shapes = {n: wts[n].shape for n in SMALL}
    unpacked = [_unpack(o[0], shapes) for o in outs]
    for n in SMALL:
        res[n] = [u[n] for u in unpacked]

    return (loss, grad_x[None], *[res[n][0] for n in WEIGHTS], *[res[n][1] for n in WEIGHTS],
            *[res[n][2] for n in WEIGHTS], *[res[n][3] for n in WEIGHTS])
```

```python
import functools
import math

import numpy as np
import jax
import jax.numpy as jnp
from jax import lax
from jax.experimental import pallas as pl
from jax.experimental.pallas import tpu as pltpu

F32 = jnp.float32
BF16 = jnp.bfloat16

D_MODEL = 1024
ATTN_W = 512
SSM_W = 512
HEAD_DIM = 64
N_Q = 8
N_KV = 2
Q_PER_KV = 4
KV_W = 128
WINDOW = 128
BLOCK = 128
N_GROUPS = 32
GROUP_W = 16
STATE = 64
N_CH = N_GROUPS * STATE
HALF_CH = N_CH // 2
D_FF = 4096
IN_W = 1280
N_MOD = 6
EPS = 1e-6
NEG_INF = -1e30

ADAM_LR = 0.001
ADAM_B1 = 0.9
ADAM_B2 = 0.999
ADAM_EPS = 1e-08
ADAM_WD = 0.01
ADAM_STEP = 10

ROW_TILE = 256
PROJ_TILE = 512
CHUNK = 512
SEGS = 8
STEPS = CHUNK // SEGS
STRIP = 512
VMEM_LIMIT_V7X = 56 * 1024 * 1024
LANE = 128
SUBLANE = 8

GELU_K0 = math.sqrt(2.0 / math.pi)
GELU_K1 = 0.044715

V_SH1, V_SC1, V_G1, V_SH2, V_SC2, V_G2, V_PRE_MIX, V_POST_MIX, V_PRE_MLP, V_POST_MLP = range(10)
H_ATTN_G, H_SSM_G, H_DSKIP, H_BGLU = range(4)

HBM = pl.BlockSpec(memory_space=pltpu.HBM)
SEM = pl.BlockSpec(memory_space=pltpu.SEMAPHORE)
EFFECT = pltpu.SideEffectType.DATAFLOW_SIDE_EFFECTING
MESH_ID = pl.DeviceIdType.MESH


def _nn(a, b):
    return lax.dot_general(a, b, (((1,), (0,)), ((), ())), preferred_element_type=F32)


def _nt(a, b):
    return lax.dot_general(a, b, (((1,), (1,)), ((), ())), preferred_element_type=F32)


def _tn(a, b):
    return lax.dot_general(a, b, (((0,), (0,)), ((), ())), preferred_element_type=F32)


def _params(sem):
    return pltpu.CompilerParams(dimension_semantics=sem, vmem_limit_bytes=VMEM_LIMIT_V7X)


def _rms_fwd(x, g):
    r = lax.rsqrt(jnp.mean(x * x, axis=-1, keepdims=True) + EPS)
    xh = x * r
    return xh * g, xh, r


def _rms_bwd(dy, xh, r, g):
    dxh = dy * g
    dx = r * (dxh - xh * jnp.mean(dxh * xh, axis=-1, keepdims=True))
    return dx, dy * xh


def _colsum(t):
    return jnp.sum(t, axis=0, keepdims=True)


def _gelu(y):
    t = jnp.tanh(GELU_K0 * (y + GELU_K1 * (y * y * y)))
    return 0.5 * y * (1.0 + t), t


def _gelu_grad(y, t):
    return 0.5 * (1.0 + t) + 0.5 * y * (1.0 - t * t) * GELU_K0 * (1.0 + 3.0 * GELU_K1 * y * y)


def _alibi_slopes():
    return [float(s) for s in 2.0 ** (-8.0 * np.arange(1, N_Q + 1) / N_Q)]


def _pick_rows(rows, bytes_per_row, budget):
    t = rows
    while t % (2 * SUBLANE) == 0 and t * bytes_per_row > budget:
        t //= 2
    return t


def _load_once(step, pairs, sems):
    @pl.when(step == 0)
    def _():
        cps = [pltpu.make_async_copy(src, dst, sems.at[k]) for k, (src, dst) in enumerate(pairs)]
        for cp in cps:
            cp.start()
        for cp in cps:
            cp.wait()


_GROUPS = {
    "all": ([(0, 0, 1), (0, 1, 0), (0, 1, 1), (1, 0, 0), (1, 0, 1), (1, 1, 0), (1, 1, 1)], (4, 2, 1), 8),
    "chips": ([(1, 0, 0), (0, 1, 0), (1, 1, 0)], (2, 1, 0), 4),
    "pair": ([(0, 0, 1)], (0, 0, 1), 2),
}


def _flip(v, f):
    return 1 - v if f else v


def _gather(arrs, kind, name):
    masks, wts, n = _GROUPS[kind]
    na, nm = len(arrs), len(masks)

    def body(*refs):
        ins, outs = refs[:na], refs[na:2 * na]
        ssem, rsem, lsem = refs[2 * na:]
        x, y, c = lax.axis_index("x"), lax.axis_index("y"), lax.axis_index("c")
        me = wts[0] * x + wts[1] * y + wts[2] * c
        local = [pltpu.make_async_copy(ins[k], outs[k].at[:, pl.ds(me, 1)], lsem.at[k]) for k in range(na)]
        for cp in local:
            cp.start()
        remote = []
        for k in range(na):
            for mi, (fx, fy, fc) in enumerate(masks):
                peer = (_flip(x, fx), _flip(y, fy), _flip(c, fc))
                remote.append(pltpu.make_async_remote_copy(
                    src_ref=ins[k], dst_ref=outs[k].at[:, pl.ds(me, 1)],
                    send_sem=ssem.at[k * nm + mi], recv_sem=rsem.at[k * nm + mi],
                    device_id=peer, device_id_type=MESH_ID))
        for cp in remote:
            cp.start()
        for cp in remote:
            cp.wait()
        for cp in local:
            cp.wait()

    outs = pl.pallas_call(
        body, name=name,
        out_shape=[jax.ShapeDtypeStruct((a.shape[0], n) + a.shape[2:], a.dtype) for a in arrs],
        in_specs=[HBM] * na, out_specs=[HBM] * na,
        scratch_shapes=[pltpu.SemaphoreType.DMA((na * nm,)), pltpu.SemaphoreType.DMA((na * nm,)),
                        pltpu.SemaphoreType.DMA((na,))],
    )(*arrs)
    return list(outs)


def _hbm(a):
    return pltpu.with_memory_space_constraint(a, pltpu.HBM)


def _after(x, *tokens):
    for t in tokens:
        x = x + t[0, 0].astype(x.dtype)
    return x


def _exchange_start(name, n_copies, plan, bufs, after=()):
    n, na = len(bufs), len(after)

    def body(*refs):
        ssem, rsem, token = refs[n + na], refs[n + na + 1], refs[2 * n + na + 2]
        for k, (src, dst, dev) in enumerate(plan(refs[:n])):
            pltpu.make_async_remote_copy(src_ref=src, dst_ref=dst, send_sem=ssem.at[k], recv_sem=rsem.at[k],
                                         device_id=dev, device_id_type=MESH_ID).start()
        token[...] = jnp.zeros_like(token)

    outs = pl.pallas_call(
        body, name=name,
        out_shape=(pltpu.SemaphoreType.DMA((n_copies,)), pltpu.SemaphoreType.DMA((n_copies,)),
                   *[pltpu.HBM(b.shape, b.dtype) for b in bufs], jax.ShapeDtypeStruct((SUBLANE, LANE), F32)),
        in_specs=[HBM] * n + [pl.BlockSpec(memory_space=pl.ANY)] * na,
        out_specs=(SEM, SEM, *[HBM] * n, pl.BlockSpec(memory_space=pltpu.VMEM)),
        input_output_aliases={i: 2 + i for i in range(n)},
        compiler_params=pltpu.CompilerParams(has_side_effects=EFFECT),
    )(*[_hbm(b) for b in bufs], *after)
    return outs[0], outs[1], list(outs[2:2 + n]), outs[2 + n]


def _exchange_wait(name, n_copies, plan, started, after):
    ssem, rsem, bufs, _ = started
    n = len(bufs)
    after = list(after) if isinstance(after, (list, tuple)) else [after]

    def body(*refs):
        ssem_ref, rsem_ref = refs[n], refs[n + 1]
        for k, (src, dst, dev) in enumerate(plan(refs[:n])):
            cp = pltpu.make_async_remote_copy(src_ref=src, dst_ref=dst, send_sem=ssem_ref.at[k], recv_sem=rsem_ref.at[k],
                                              device_id=dev, device_id_type=MESH_ID)
            cp.wait_send()
            cp.wait_recv()

    outs = pl.pallas_call(
        body, name=name,
        out_shape=tuple(pltpu.HBM(b.shape, b.dtype) for b in bufs),
        in_specs=[HBM] * n + [SEM, SEM] + [pl.BlockSpec(memory_space=pl.ANY)] * len(after), out_specs=tuple([HBM] * n),
        input_output_aliases={i: i for i in range(n)},
        compiler_params=pltpu.CompilerParams(has_side_effects=EFFECT),
    )(*bufs, ssem, rsem, *after)
    return list(outs)


def _position():
    x, y, c = lax.axis_index("x"), lax.axis_index("y"), lax.axis_index("c")
    return x, y, c, [(1 - x, y), (x, 1 - y), (1 - x, 1 - y)]


def _plan_gather(na):
    def plan(refs):
        x, y, c, chips = _position()
        return [(refs[k], refs[na + k].at[2 * x + y], (px, py, c)) for k in range(na) for px, py in chips]
    return plan


def _plan_scatter(na):
    def plan(refs):
        x, y, c, chips = _position()
        return [(refs[k].at[2 * px + py], refs[na + k].at[2 * x + y], (px, py, c))
                for k in range(na) for px, py in chips]
    return plan


def _plan_pair(na):
    def plan(refs):
        x, y, c, _ = _position()
        return [(refs[k], refs[na + k], (x, y, 1 - c)) for k in range(na)]
    return plan


def _sum_list(arrs, name):
    n = len(arrs)
    r, c = arrs[0].shape
    tr = _pick_rows(r, c * 4 * (n + 1), 4 << 20)

    def body(*refs):
        acc = refs[0][...].astype(F32)
        for j in range(1, n):
            acc = acc + refs[j][...].astype(F32)
        refs[n][...] = acc

    blk = pl.BlockSpec((tr, c), lambda i: (i, 0))
    return pl.pallas_call(
        body, name=name, grid=(r // tr,), in_specs=[blk] * n, out_specs=blk,
        out_shape=jax.ShapeDtypeStruct((r, c), F32), compiler_params=_params(("parallel",)),
    )(*arrs)


def _sum_pieces(own, recv, chip, name):
    _, r, c = own.shape
    tr = _pick_rows(r, c * 2 * 6, 4 << 20)

    def body(chip_ref, own_ref, recv_ref, o_ref):
        acc = own_ref[0].astype(F32)
        for j in range(4):
            acc = acc + jnp.where(chip_ref[0] == j, 0.0, recv_ref[j].astype(F32))
        o_ref[...] = acc.astype(BF16)

    return pl.pallas_call(
        body, name=name,
        grid_spec=pltpu.PrefetchScalarGridSpec(
            num_scalar_prefetch=1, grid=(r // tr,),
            in_specs=[pl.BlockSpec((1, tr, c), lambda i, chip_ref: (chip_ref[0], i, 0)),
                      pl.BlockSpec((4, tr, c), lambda i, chip_ref: (0, i, 0))],
            out_specs=pl.BlockSpec((tr, c), lambda i, chip_ref: (i, 0))),
        out_shape=jax.ShapeDtypeStruct((r, c), BF16), compiler_params=_params(("parallel",)),
    )(chip, own, recv)


def _adam_update(g, w, m, v):
    mn = ADAM_B1 * m + (1.0 - ADAM_B1) * g
    vn = ADAM_B2 * v + (1.0 - ADAM_B2) * jnp.square(g)
    m_hat = mn / (1.0 - ADAM_B1 ** ADAM_STEP)
    v_hat = vn / (1.0 - ADAM_B2 ** ADAM_STEP)
    return -ADAM_LR * (m_hat / (jnp.sqrt(v_hat) + ADAM_EPS) + ADAM_WD * w), mn, vn


def _adamw_layer(grads, w, m, v, layer, prev, name):
    ng = len(grads)
    nl, r, c = w.shape
    tr = _pick_rows(r, c * 4 * (ng + 7), 6 << 20)
    if prev is None:
        prev = [lax.empty((nl, r, c), F32) for _ in range(4)]

    def body(*refs):
        g = refs[0][...].astype(F32)
        for j in range(1, ng):
            g = g + refs[j][...].astype(F32)
        w_ref, m_ref, v_ref = refs[ng:ng + 3]
        go_ref, d_ref, mo_ref, vo_ref = refs[ng + 7:ng + 11]
        d, mn, vn = _adam_update(g, w_ref[0], m_ref[0], v_ref[0])
        go_ref[0] = g
        d_ref[0] = d
        mo_ref[0] = mn
        vo_ref[0] = vn

    gblk = pl.BlockSpec((tr, c), lambda i: (i, 0))
    blk = pl.BlockSpec((1, tr, c), lambda i: (layer, i, 0))
    keep = pl.BlockSpec(memory_space=pl.ANY)
    sds = jax.ShapeDtypeStruct((nl, r, c), F32)
    return pl.pallas_call(
        body, name=name, grid=(r // tr,),
        in_specs=[gblk] * ng + [blk] * 3 + [keep] * 4,
        out_specs=[blk] * 4, out_shape=[sds] * 4,
        input_output_aliases={ng + 3 + i: i for i in range(4)},
        compiler_params=_params(("parallel",)),
    )(*grads, w, m, v, *prev)


def _adamw(gs, w, m, v, name):
    a, s, r, c = gs.shape
    tr = _pick_rows(r, c * 4 * (s + 7), 6 << 20)

    def body(g_ref, w_ref, m_ref, v_ref, go_ref, d_ref, mo_ref, vo_ref):
        g = g_ref[0, 0].astype(F32)
        for j in range(1, s):
            g = g + g_ref[0, j].astype(F32)
        d, mn, vn = _adam_update(g, w_ref[0], m_ref[0], v_ref[0])
        go_ref[0] = g
        d_ref[0] = d
        mo_ref[0] = mn
        vo_ref[0] = vn

    blk = pl.BlockSpec((1, tr, c), lambda i, j: (i, j, 0))
    sds = jax.ShapeDtypeStruct((a, r, c), F32)
    return pl.pallas_call(
        body, name=name, grid=(a, r // tr),
        in_specs=[pl.BlockSpec((1, s, tr, c), lambda i, j: (i, 0, j, 0)), blk, blk, blk],
        out_specs=[blk, blk, blk, blk], out_shape=[sds, sds, sds, sds],
        compiler_params=_params(("parallel", "parallel")),
    )(gs, w, m, v)


def _ada_forward(c_all, w_ada, b_sh):
    nl, d, w = w_ada.shape
    tw = 512

    def body(c_ref, w_ref, b_ref, o_ref):
        cv = c_ref[...]
        act = (cv * jax.nn.sigmoid(cv)).astype(BF16)
        o_ref[0] = _nn(act, w_ref[0].astype(BF16)) + b_ref[0]

    return pl.pallas_call(
        body, name="ada_forward", grid=(nl, w // tw),
        in_specs=[pl.BlockSpec((8, d), lambda l, j: (0, 0)),
                  pl.BlockSpec((1, d, tw), lambda l, j: (l, 0, j)),
                  pl.BlockSpec((1, 1, tw), lambda l, j: (l, 0, j))],
        out_specs=pl.BlockSpec((1, 8, tw), lambda l, j: (l, 0, j)),
        out_shape=jax.ShapeDtypeStruct((nl, 8, w), F32),
        compiler_params=_params(("parallel", "parallel")),
    )(c_all, w_ada, b_sh)


def _ada_weight_grad(c_all_t, dmod):
    nl, nb, w = dmod.shape
    d = c_all_t.shape[0]
    tw = 512

    def body(c_ref, g_ref, o_ref):
        cv = c_ref[...]
        act = cv * jax.nn.sigmoid(cv)
        gv = g_ref[0]
        acc = act[:, 0:1] * gv[0:1, :]
        for b in range(1, nb):
            acc = acc + act[:, b:b + 1] * gv[b:b + 1, :]
        o_ref[0] = acc

    return pl.pallas_call(
        body, name="ada_weight_grad", grid=(nl, w // tw),
        in_specs=[pl.BlockSpec((d, nb), lambda l, j: (0, 0)),
                  pl.BlockSpec((1, nb, tw), lambda l, j: (l, 0, j))],
        out_specs=pl.BlockSpec((1, d, tw), lambda l, j: (l, 0, j)),
        out_shape=jax.ShapeDtypeStruct((nl, d, w), F32),
        compiler_params=_params(("parallel", "parallel")),
    )(c_all_t, dmod)


def _in_proj_fwd(x, vec, w_in_t, layer):
    seq = x.shape[0]
    tm = PROJ_TILE

    def body(x_ref, vec_ref, w_ref, q_ref, kv_ref, u4_ref, h_ref):
        n, _, _ = _rms_fwd(x_ref[...], vec_ref[0, V_PRE_MIX:V_PRE_MIX + 1, :])
        h = (n * (1.0 + vec_ref[0, V_SC1:V_SC1 + 1, :]) + vec_ref[0, V_SH1:V_SH1 + 1, :]).astype(BF16)
        h_ref[...] = h
        proj = _nt(h, w_ref[...])
        q_ref[...] = proj[:, :ATTN_W].astype(BF16)
        kv_ref[...] = proj[:, ATTN_W:ATTN_W + 2 * KV_W].astype(BF16)
        u0 = ATTN_W + 2 * KV_W
        for j in range(4):
            u4_ref[j] = proj[:, u0 + j * LANE:u0 + (j + 1) * LANE]

    return pl.pallas_call(
        body, name="in_proj_fwd", grid=(seq // tm,),
        in_specs=[pl.BlockSpec((tm, D_MODEL), lambda i: (i, 0)),
                  pl.BlockSpec((1, 16, D_MODEL), lambda i: (layer, 0, 0)),
                  pl.BlockSpec((IN_W, D_MODEL), lambda i: (0, 0))],
        out_specs=[pl.BlockSpec((tm, ATTN_W), lambda i: (i, 0)),
                   pl.BlockSpec((tm, 2 * KV_W), lambda i: (i, 0)),
                   pl.BlockSpec((4, tm, LANE), lambda i: (0, i, 0)),
                   pl.BlockSpec((tm, D_MODEL), lambda i: (i, 0))],
        out_shape=[jax.ShapeDtypeStruct((seq, ATTN_W), BF16), jax.ShapeDtypeStruct((seq, 2 * KV_W), BF16),
                   jax.ShapeDtypeStruct((4, seq, LANE), F32), jax.ShapeDtypeStruct((seq, D_MODEL), BF16)],
        compiler_params=_params(("parallel",)),
    )(x, vec, w_in_t)


def _in_proj_bwd(dx1, dq, dkv, du4, x, vec, w_in_t, layer):
    seq = x.shape[0]
    tm = PROJ_TILE

    def body(dx1_ref, dq_ref, dkv_ref, du4_ref, x_ref, vec_ref, w_ref, dx_ref, dp_ref, dvec_ref):
        i = pl.program_id(0)

        @pl.when(i == 0)
        def _():
            dvec_ref[...] = jnp.zeros_like(dvec_ref)

        dproj = jnp.concatenate([dq_ref[...], dkv_ref[...]] + [du4_ref[j] for j in range(4)], axis=1).astype(BF16)
        dp_ref[...] = dproj
        dh = _nn(dproj, w_ref[...])
        g = vec_ref[0, V_PRE_MIX:V_PRE_MIX + 1, :]
        n, xh, r = _rms_fwd(x_ref[...], g)
        dn = dh * (1.0 + vec_ref[0, V_SC1:V_SC1 + 1, :])
        dxn, dg_rows = _rms_bwd(dn, xh, r, g)
        dx_ref[...] = dx1_ref[...] + dxn
        dvec_ref[V_SH1:V_SH1 + 1, :] += _colsum(dh)
        dvec_ref[V_SC1:V_SC1 + 1, :] += _colsum(dh * n)
        dvec_ref[V_PRE_MIX:V_PRE_MIX + 1, :] += _colsum(dg_rows)

    row = pl.BlockSpec((tm, D_MODEL), lambda i: (i, 0))
    return pl.pallas_call(
        body, name="in_proj_bwd", grid=(seq // tm,),
        in_specs=[row, pl.BlockSpec((tm, ATTN_W), lambda i: (i, 0)), pl.BlockSpec((tm, 2 * KV_W), lambda i: (i, 0)),
                  pl.BlockSpec((4, tm, LANE), lambda i: (0, i, 0)), row,
                  pl.BlockSpec((1, 16, D_MODEL), lambda i: (layer, 0, 0)),
                  pl.BlockSpec((IN_W, D_MODEL), lambda i: (0, 0))],
        out_specs=[row, pl.BlockSpec((tm, IN_W), lambda i: (i, 0)), pl.BlockSpec((16, D_MODEL), lambda i: (0, 0))],
        out_shape=[jax.ShapeDtypeStruct((seq, D_MODEL), F32), jax.ShapeDtypeStruct((seq, IN_W), BF16),
                   jax.ShapeDtypeStruct((16, D_MODEL), F32)],
        compiler_params=_params(("arbitrary",)),
    )(dx1, dq, dkv, du4, x, vec, w_in_t)


def _heads(attn_ref, s4_ref, vec512_ref):
    ga = vec512_ref[0, H_ATTN_G:H_ATTN_G + 1, :]
    gs = vec512_ref[0, H_SSM_G:H_SSM_G + 1, :]
    sv = jnp.concatenate([s4_ref[j] for j in range(4)], axis=1)
    na, ah, ar = _rms_fwd(attn_ref[...], ga)
    ns, sh, sr = _rms_fwd(sv, gs)
    return jnp.concatenate([na, ns], axis=1), (ah, ar, ga), (sh, sr, gs)


def _out_proj_fwd(x, attn, s4, vec, vec512, w_out, layer):
    seq = x.shape[0]
    tm = PROJ_TILE

    def body(x_ref, attn_ref, s4_ref, vec_ref, vec512_ref, w_ref, x1_ref):
        heads, _, _ = _heads(attn_ref, s4_ref, vec512_ref)
        mixed = _nn(heads.astype(BF16), w_ref[...])
        nm, _, _ = _rms_fwd(mixed, vec_ref[0, V_POST_MIX:V_POST_MIX + 1, :])
        x1_ref[...] = x_ref[...] + vec_ref[0, V_G1:V_G1 + 1, :] * nm

    row = pl.BlockSpec((tm, D_MODEL), lambda i: (i, 0))
    return pl.pallas_call(
        body, name="out_proj_fwd", grid=(seq // tm,),
        in_specs=[row, pl.BlockSpec((tm, ATTN_W), lambda i: (i, 0)), pl.BlockSpec((4, tm, LANE), lambda i: (0, i, 0)),
                  pl.BlockSpec((1, 16, D_MODEL), lambda i: (layer, 0, 0)),
                  pl.BlockSpec((1, 8, SSM_W), lambda i: (layer, 0, 0)),
                  pl.BlockSpec((D_MODEL, D_MODEL), lambda i: (0, 0))],
        out_specs=row, out_shape=jax.ShapeDtypeStruct((seq, D_MODEL), F32),
        compiler_params=_params(("parallel",)),
    )(x, attn, s4, vec, vec512, w_out)


def _out_proj_bwd(dx1, attn, s4, vec, vec512, w_out, layer):
    seq = dx1.shape[0]
    tm = PROJ_TILE

    def body(dx1_ref, attn_ref, s4_ref, vec_ref, vec512_ref, w_ref,
             dattn_ref, ds4_ref, heads_ref, dmixed_ref, dvec_ref, dvec512_ref):
        i = pl.program_id(0)

        @pl.when(i == 0)
        def _():
            dvec_ref[...] = jnp.zeros_like(dvec_ref)
            dvec512_ref[...] = jnp.zeros_like(dvec512_ref)

        heads, (ah, ar, ga), (sh, sr, gs) = _heads(attn_ref, s4_ref, vec512_ref)
        hb = heads.astype(BF16)
        heads_ref[...] = hb
        gm = vec_ref[0, V_POST_MIX:V_POST_MIX + 1, :]
        nm, mh, mr = _rms_fwd(_nn(hb, w_ref[...]), gm)
        dx1v = dx1_ref[...]
        dvec_ref[V_G1:V_G1 + 1, :] += _colsum(dx1v * nm)
        dmixed, dgm_rows = _rms_bwd(dx1v * vec_ref[0, V_G1:V_G1 + 1, :], mh, mr, gm)
        dvec_ref[V_POST_MIX:V_POST_MIX + 1, :] += _colsum(dgm_rows)
        dmb = dmixed.astype(BF16)
        dmixed_ref[...] = dmb
        dheads = _nt(dmb, w_ref[...])
        dattn, dga_rows = _rms_bwd(dheads[:, :ATTN_W], ah, ar, ga)
        ds, dgs_rows = _rms_bwd(dheads[:, ATTN_W:], sh, sr, gs)
        dattn_ref[...] = dattn
        for j in range(4):
            ds4_ref[j] = ds[:, j * LANE:(j + 1) * LANE]
        dvec512_ref[H_ATTN_G:H_ATTN_G + 1, :] += _colsum(dga_rows)
        dvec512_ref[H_SSM_G:H_SSM_G + 1, :] += _colsum(dgs_rows)

    row = pl.BlockSpec((tm, D_MODEL), lambda i: (i, 0))
    return pl.pallas_call(
        body, name="out_proj_bwd", grid=(seq // tm,),
        in_specs=[row, pl.BlockSpec((tm, ATTN_W), lambda i: (i, 0)), pl.BlockSpec((4, tm, LANE), lambda i: (0, i, 0)),
                  pl.BlockSpec((1, 16, D_MODEL), lambda i: (layer, 0, 0)),
                  pl.BlockSpec((1, 8, SSM_W), lambda i: (layer, 0, 0)),
                  pl.BlockSpec((D_MODEL, D_MODEL), lambda i: (0, 0))],
        out_specs=[pl.BlockSpec((tm, ATTN_W), lambda i: (i, 0)), pl.BlockSpec((4, tm, LANE), lambda i: (0, i, 0)),
                   row, row, pl.BlockSpec((16, D_MODEL), lambda i: (0, 0)), pl.BlockSpec((8, SSM_W), lambda i: (0, 0))],
        out_shape=[jax.ShapeDtypeStruct((seq, ATTN_W), F32), jax.ShapeDtypeStruct((4, seq, LANE), F32),
                   jax.ShapeDtypeStruct((seq, D_MODEL), BF16), jax.ShapeDtypeStruct((seq, D_MODEL), BF16),
                   jax.ShapeDtypeStruct((16, D_MODEL), F32), jax.ShapeDtypeStruct((8, SSM_W), F32)],
        compiler_params=_params(("arbitrary",)),
    )(dx1, attn, s4, vec, vec512, w_out)


def _mlp_fwd(x1, vec, w_in_t, w_out, layer):
    seq = x1.shape[0]
    tm = ROW_TILE

    def body(x1_ref, vec_ref, wi_hbm, wo_hbm, x2_ref, r_ref, f_ref, wi, wo, sems):
        _load_once(pl.program_id(0), [(wi_hbm, wi), (wo_hbm, wo)], sems)
        x1v = x1_ref[...]
        n, _, _ = _rms_fwd(x1v, vec_ref[0, V_PRE_MLP:V_PRE_MLP + 1, :])
        h = (n * (1.0 + vec_ref[0, V_SC2:V_SC2 + 1, :]) + vec_ref[0, V_SH2:V_SH2 + 1, :]).astype(BF16)
        a = _nt(h, wi[...])
        r = jnp.square(jnp.maximum(a, 0.0)).astype(BF16)
        r_ref[...] = r
        f = _nn(r, wo[...])
        f_ref[...] = f
        nf, _, _ = _rms_fwd(f, vec_ref[0, V_POST_MLP:V_POST_MLP + 1, :])
        x2_ref[...] = x1v + vec_ref[0, V_G2:V_G2 + 1, :] * nf

    row = pl.BlockSpec((tm, D_MODEL), lambda i: (i, 0))
    return pl.pallas_call(
        body, name="mlp_fwd", grid=(seq // tm,),
        in_specs=[row, pl.BlockSpec((1, 16, D_MODEL), lambda i: (layer, 0, 0)), HBM, HBM],
        out_specs=[row, pl.BlockSpec((tm, D_FF), lambda i: (i, 0)), row],
        out_shape=[jax.ShapeDtypeStruct((seq, D_MODEL), F32), jax.ShapeDtypeStruct((seq, D_FF), BF16),
                   jax.ShapeDtypeStruct((seq, D_MODEL), F32)],
        scratch_shapes=[pltpu.VMEM((D_FF, D_MODEL), BF16), pltpu.VMEM((D_FF, D_MODEL), BF16),
                        pltpu.SemaphoreType.DMA((2,))],
        compiler_params=_params(("arbitrary",)),
    )(x1, vec, w_in_t, w_out)


def _mlp_bwd(dx2, x1, r, f, vec, w_in_t, w_out, layer):
    seq = x1.shape[0]
    tm = ROW_TILE

    def body(dx2_ref, x1_ref, r_ref, f_ref, vec_ref, wi_hbm, wo_hbm, dx1_ref, h_ref, da_ref, df_ref, dvec_ref,
             wi, wo, sems):
        i = pl.program_id(0)
        _load_once(i, [(wi_hbm, wi), (wo_hbm, wo)], sems)

        @pl.when(i == 0)
        def _():
            dvec_ref[...] = jnp.zeros_like(dvec_ref)

        g_pre = vec_ref[0, V_PRE_MLP:V_PRE_MLP + 1, :]
        g_post = vec_ref[0, V_POST_MLP:V_POST_MLP + 1, :]
        sc2 = vec_ref[0, V_SC2:V_SC2 + 1, :]
        n, xh, xr = _rms_fwd(x1_ref[...], g_pre)
        h_ref[...] = (n * (1.0 + sc2) + vec_ref[0, V_SH2:V_SH2 + 1, :]).astype(BF16)
        relu = jnp.sqrt(r_ref[...].astype(F32))
        nf, fh, fr = _rms_fwd(f_ref[...], g_post)
        dx2v = dx2_ref[...]
        dvec_ref[V_G2:V_G2 + 1, :] += _colsum(dx2v * nf)
        df, dgp_rows = _rms_bwd(dx2v * vec_ref[0, V_G2:V_G2 + 1, :], fh, fr, g_post)
        dvec_ref[V_POST_MLP:V_POST_MLP + 1, :] += _colsum(dgp_rows)
        dfb = df.astype(BF16)
        df_ref[...] = dfb
        da = (_nt(dfb, wo[...]) * (2.0 * relu)).astype(BF16)
        da_ref[...] = da
        dh = _nn(da, wi[...])
        dvec_ref[V_SH2:V_SH2 + 1, :] += _colsum(dh)
        dvec_ref[V_SC2:V_SC2 + 1, :] += _colsum(dh * n)
        dxn, dg_rows = _rms_bwd(dh * (1.0 + sc2), xh, xr, g_pre)
        dvec_ref[V_PRE_MLP:V_PRE_MLP + 1, :] += _colsum(dg_rows)
        dx1_ref[...] = dx2v + dxn

    row = pl.BlockSpec((tm, D_MODEL), lambda i: (i, 0))
    wide = pl.BlockSpec((tm, D_FF), lambda i: (i, 0))
    return pl.pallas_call(
        body, name="mlp_bwd", grid=(seq // tm,),
        in_specs=[row, row, wide, row, pl.BlockSpec((1, 16, D_MODEL), lambda i: (layer, 0, 0)), HBM, HBM],
        out_specs=[row, row, wide, row, pl.BlockSpec((16, D_MODEL), lambda i: (0, 0))],
        out_shape=[jax.ShapeDtypeStruct((seq, D_MODEL), F32), jax.ShapeDtypeStruct((seq, D_MODEL), BF16),
                   jax.ShapeDtypeStruct((seq, D_FF), BF16),
                   jax.ShapeDtypeStruct((seq, D_MODEL), BF16), jax.ShapeDtypeStruct((16, D_MODEL), F32)],
        scratch_shapes=[pltpu.VMEM((D_FF, D_MODEL), BF16), pltpu.VMEM((D_FF, D_MODEL), BF16),
                        pltpu.SemaphoreType.DMA((2,))],
        compiler_params=_params(("arbitrary",)),
    )(dx2, x1, r, f, vec, w_in_t, w_out)


def _loss_head(y, target):
    seq = y.shape[0]
    tm = ROW_TILE

    def body(y_ref, t_ref, dy_ref, part_ref):
        e = y_ref[...] - t_ref[...]
        dy_ref[...] = e * (1.0 / D_MODEL)
        tot = jnp.sum(jnp.sum(e * e, axis=1, keepdims=True), axis=0, keepdims=True) * (0.5 / D_MODEL)
        part_ref[0] = jnp.broadcast_to(tot, (SUBLANE, LANE))

    row = pl.BlockSpec((tm, D_MODEL), lambda i: (i, 0))
    return pl.pallas_call(
        body, name="loss_head", grid=(seq // tm,),
        in_specs=[row, row],
        out_specs=[row, pl.BlockSpec((1, SUBLANE, LANE), lambda i: (i, 0, 0))],
        out_shape=[jax.ShapeDtypeStruct((seq, D_MODEL), F32), jax.ShapeDtypeStruct((seq // tm, SUBLANE, LANE), F32)],
        compiler_params=_params(("parallel",)),
    )(y, target)


def _matmul_tn(a, b, out_dtype, name, pieces=1):
    kk, m = a.shape
    n = b.shape[1]
    tm = min(m, 512)
    tn = n // pieces if pieces > 1 else min(n, 1280)
    tk = min(kk, 2048)
    nk = kk // tk

    def body(a_ref, b_ref, o_ref, acc):
        k = pl.program_id(2)

        @pl.when(k == 0)
        def _():
            acc[...] = jnp.zeros_like(acc)

        acc[...] += _tn(a_ref[...], b_ref[...])

        @pl.when(k == nk - 1)
        def _():
            if pieces > 1:
                o_ref[0] = acc[...].astype(out_dtype)
            else:
                o_ref[...] = acc[...].astype(out_dtype)

    if pieces > 1:
        out_spec = pl.BlockSpec((1, tm, tn), lambda i, j, k: (j, i, 0))
        out_shape = jax.ShapeDtypeStruct((pieces, m, tn), out_dtype)
    else:
        out_spec = pl.BlockSpec((tm, tn), lambda i, j, k: (i, j))
        out_shape = jax.ShapeDtypeStruct((m, n), out_dtype)
    return pl.pallas_call(
        body, name=name, grid=(m // tm, n // tn, nk),
        in_specs=[pl.BlockSpec((tk, tm), lambda i, j, k: (k, i)), pl.BlockSpec((tk, tn), lambda i, j, k: (k, j))],
        out_specs=out_spec, out_shape=out_shape,
        scratch_shapes=[pltpu.VMEM((tm, tn), F32)],
        compiler_params=_params(("parallel", "parallel", "arbitrary")),
    )(a, b)


def _attn_probs(i, qk, slope, sink):
    rr = lax.broadcasted_iota(jnp.int32, (BLOCK, 2 * BLOCK), 0)
    jj = lax.broadcasted_iota(jnp.int32, (BLOCK, 2 * BLOCK), 1)
    diff = BLOCK + rr - jj
    valid = (diff >= 0) & (diff < WINDOW) & ((jj >= BLOCK) | (i > 0))
    s = qk * (HEAD_DIM ** -0.5)
    s = jnp.where(valid, s - slope * diff.astype(F32), NEG_INF)
    m = jnp.maximum(jnp.max(s, axis=1, keepdims=True), sink)
    p = jnp.exp(s - m)
    ps = jnp.exp(sink - m)
    inv = 1.0 / (jnp.sum(p, axis=1, keepdims=True) + ps)
    return p * inv, ps * inv


def _bands(kvp, kvc, h):
    kband = jnp.concatenate([kvp[:, h * HEAD_DIM:(h + 1) * HEAD_DIM], kvc[:, h * HEAD_DIM:(h + 1) * HEAD_DIM]], axis=0)
    v0 = KV_W + h * HEAD_DIM
    vband = jnp.concatenate([kvp[:, v0:v0 + HEAD_DIM], kvc[:, v0:v0 + HEAD_DIM]], axis=0)
    return kband, vband


def _attn_fwd(q, kv, sinks):
    seq = q.shape[0]
    nb = seq // BLOCK
    slopes = _alibi_slopes()

    def body(sink_ref, q_ref, kvp_ref, kvc_ref, o_ref):
        i = pl.program_id(0)
        qv, kvp, kvc = q_ref[...], kvp_ref[...], kvc_ref[...]
        bands = [_bands(kvp, kvc, h) for h in range(N_KV)]
        cols = [slice(hq * HEAD_DIM, (hq + 1) * HEAD_DIM) for hq in range(N_Q)]
        qk = [_nt(qv[:, cols[hq]], bands[hq // Q_PER_KV][0]) for hq in range(N_Q)]
        pr = [_attn_probs(i, qk[hq], slopes[hq], sink_ref[hq])[0].astype(BF16) for hq in range(N_Q)]
        for hq in range(N_Q):
            o_ref[:, cols[hq]] = _nn(pr[hq], bands[hq // Q_PER_KV][1])

    return pl.pallas_call(
        body, name="attn_fwd", grid=(nb,),
        in_specs=[pl.BlockSpec(memory_space=pltpu.SMEM),
                  pl.BlockSpec((BLOCK, ATTN_W), lambda i: (i, 0)),
                  pl.BlockSpec((BLOCK, 2 * KV_W), lambda i: (jnp.maximum(i - 1, 0), 0)),
                  pl.BlockSpec((BLOCK, 2 * KV_W), lambda i: (i, 0))],
        out_specs=pl.BlockSpec((BLOCK, ATTN_W), lambda i: (i, 0)),
        out_shape=jax.ShapeDtypeStruct((seq, ATTN_W), F32),
        compiler_params=_params(("parallel",)),
    )(sinks, q, kv, kv)


def _attn_bwd(q, kv, sinks, dout):
    seq = q.shape[0]
    nb = seq // BLOCK
    slopes = _alibi_slopes()
    scale = HEAD_DIM ** -0.5

    def body(sink_ref, q_ref, kvp_ref, kvc_ref, do_ref, dq_ref, dkv_ref, dsk_ref, prev):
        step = pl.program_id(0)
        i = nb - 1 - step

        @pl.when(step == 0)
        def _():
            prev[...] = jnp.zeros_like(prev)

        qv, kvp, kvc = q_ref[...], kvp_ref[...], kvc_ref[...]
        dov = do_ref[...].astype(BF16)
        bands = [_bands(kvp, kvc, h) for h in range(N_KV)]
        cols = [slice(hq * HEAD_DIM, (hq + 1) * HEAD_DIM) for hq in range(N_Q)]
        qk = [_nt(qv[:, cols[hq]], bands[hq // Q_PER_KV][0]) for hq in range(N_Q)]
        dp = [_nt(dov[:, cols[hq]], bands[hq // Q_PER_KV][1]) for hq in range(N_Q)]
        prb, dsb, dsk = [], [], []
        for hq in range(N_Q):
            pr, ps = _attn_probs(i, qk[hq], slopes[hq], sink_ref[hq])
            delta = jnp.sum(pr * dp[hq], axis=1, keepdims=True)
            dsb.append((pr * (dp[hq] - delta) * scale).astype(BF16))
            prb.append(pr.astype(BF16))
            dsk.append(jnp.broadcast_to(-_colsum(ps * delta), (1, LANE)))
        for hq in range(N_Q):
            dq_ref[:, cols[hq]] = _nn(dsb[hq], bands[hq // Q_PER_KV][0])
        dk, dv = [], []
        for h in range(N_KV):
            heads = range(h * Q_PER_KV, (h + 1) * Q_PER_KV)
            dk.append(sum(_tn(dsb[hq], qv[:, cols[hq]]) for hq in heads))
            dv.append(sum(_tn(prb[hq], dov[:, cols[hq]]) for hq in heads))
        band = jnp.concatenate(dk + dv, axis=1)
        dkv_ref[...] = band[BLOCK:, :] + prev[...]
        prev[...] = band[:BLOCK, :]
        dsk_ref[0] = jnp.concatenate(dsk, axis=0)

    return pl.pallas_call(
        body, name="attn_bwd", grid=(nb,),
        in_specs=[pl.BlockSpec(memory_space=pltpu.SMEM),
                  pl.BlockSpec((BLOCK, ATTN_W), lambda s: (nb - 1 - s, 0)),
                  pl.BlockSpec((BLOCK, 2 * KV_W), lambda s: (jnp.maximum(nb - 2 - s, 0), 0)),
                  pl.BlockSpec((BLOCK, 2 * KV_W), lambda s: (nb - 1 - s, 0)),
                  pl.BlockSpec((BLOCK, ATTN_W), lambda s: (nb - 1 - s, 0))],
        out_specs=[pl.BlockSpec((BLOCK, ATTN_W), lambda s: (nb - 1 - s, 0)),
                   pl.BlockSpec((BLOCK, 2 * KV_W), lambda s: (nb - 1 - s, 0)),
                   pl.BlockSpec((1, N_Q, LANE), lambda s: (nb - 1 - s, 0, 0))],
        out_shape=[jax.ShapeDtypeStruct((seq, ATTN_W), F32), jax.ShapeDtypeStruct((seq, 2 * KV_W), F32),
                   jax.ShapeDtypeStruct((nb, N_Q, LANE), F32)],
        scratch_shapes=[pltpu.VMEM((BLOCK, 2 * KV_W), F32)],
        compiler_params=_params(("arbitrary",)),
    )(sinks, q, kv, kv, dout)


def _discretize(lr, li, ldt, br, bi):
    dt = jnp.exp(ldt)
    mag = jnp.exp(lr * dt)
    ang = li * dt
    ab_r = mag * jnp.cos(ang)
    ab_i = mag * jnp.sin(ang)
    nr = ab_r - 1.0
    ni = ab_i
    den = lr * lr + li * li
    f_r = (nr * lr + ni * li) / den
    f_i = (ni * lr - nr * li) / den
    return ab_r, ab_i, f_r * br - f_i * bi, f_r * bi + f_i * br


def _ssm_prepare(lr, li, ldt, br, bi):
    n = lr.shape[0]
    tn = N_CH
    col = pl.BlockSpec((tn, 1), lambda i: (i, 0))
    mat = pl.BlockSpec((tn, GROUP_W), lambda i: (i, 0))

    def body(lr_ref, li_ref, ldt_ref, br_ref, bi_ref, ar_ref, ai_ref, bbr_ref, bbi_ref):
        ar, ai, bbr, bbi = _discretize(lr_ref[...], li_ref[...], ldt_ref[...], br_ref[...], bi_ref[...])
        ar_ref[...] = ar
        ai_ref[...] = ai
        bbr_ref[...] = bbr
        bbi_ref[...] = bbi

    cs = jax.ShapeDtypeStruct((n, 1), F32)
    ms = jax.ShapeDtypeStruct((n, GROUP_W), F32)
    return pl.pallas_call(
        body, name="ssm_prepare", grid=(n // tn,),
        in_specs=[col, col, col, mat, mat], out_specs=[col, col, mat, mat], out_shape=[cs, cs, ms, ms],
        compiler_params=_params(("parallel",)),
    )(lr, li, ldt, br, bi)


def _ssm_prepare_bwd(lr, li, ldt, br, bi, dar, dai, dbbr, dbbi):
    n = lr.shape[0]
    tn = N_CH
    col = pl.BlockSpec((tn, 1), lambda i: (i, 0))
    mat = pl.BlockSpec((tn, GROUP_W), lambda i: (i, 0))

    def body(lr_ref, li_ref, ldt_ref, br_ref, bi_ref, dar_ref, dai_ref, dbbr_ref, dbbi_ref,
             dlr_ref, dli_ref, dldt_ref, dbr_ref, dbi_ref):
        _, vjp = jax.vjp(_discretize, lr_ref[...], li_ref[...], ldt_ref[...], br_ref[...], bi_ref[...])
        dlr, dli, dldt, dbr, dbi = vjp((dar_ref[...], dai_ref[...], dbbr_ref[...], dbbi_ref[...]))
        dlr_ref[...] = dlr
        dli_ref[...] = dli
        dldt_ref[...] = dldt
        dbr_ref[...] = dbr
        dbi_ref[...] = dbi

    cs = jax.ShapeDtypeStruct((n, 1), F32)
    ms = jax.ShapeDtypeStruct((n, GROUP_W), F32)
    return pl.pallas_call(
        body, name="ssm_prepare_bwd", grid=(n // tn,),
        in_specs=[col, col, col, mat, mat, col, col, mat, mat],
        out_specs=[col, col, col, mat, mat], out_shape=[cs, cs, cs, ms, ms],
        compiler_params=_params(("parallel",)),
    )(lr, li, ldt, br, bi, dar, dai, dbbr, dbbi)


def _load_slabs(src4_ref, dst):
    for s in range(STEPS):
        dst[s * SEGS:(s + 1) * SEGS, :] = jnp.concatenate(
            [src4_ref[j, pl.ds(s, SEGS, stride=STEPS), :] for j in range(4)], axis=1)


def _store_slabs(src, dst4_ref):
    for s in range(STEPS):
        for j in range(4):
            dst4_ref[j, pl.ds(s, SEGS, stride=STEPS), :] = src[s * SEGS:(s + 1) * SEGS, j * LANE:(j + 1) * LANE]


def _power_table(ar_ref, ai_ref, pwr, pwi):
    ar, ai = ar_ref[0], ai_ref[0]
    pr, pi = ar, ai
    pwr[0:1, :] = pr
    pwi[0:1, :] = pi
    for k in range(1, STEPS):
        pr, pi = pr * ar - pi * ai, pr * ai + pi * ar
        pwr[k:k + 1, :] = pr
        pwi[k:k + 1, :] = pi


def _scan_states(ubf, ar_ref, ai_ref, bbr_ref, bbi_ref, pwr, pwi, cin_r, cin_i, hr, hi):
    for k in range(2):
        rows = slice(k * 256, (k + 1) * 256)
        cols = slice(k * HALF_CH, (k + 1) * HALF_CH)
        hr[:, cols] = _nn(ubf[:, rows], bbr_ref[0, k])
        hi[:, cols] = _nn(ubf[:, rows], bbi_ref[0, k])
    for st in range(N_CH // STRIP):
        cs = slice(st * STRIP, (st + 1) * STRIP)
        arb = jnp.broadcast_to(ar_ref[0, :, cs], (SEGS, STRIP))
        aib = jnp.broadcast_to(ai_ref[0, :, cs], (SEGS, STRIP))

        def step(s, carry, cs=cs, arb=arb, aib=aib):
            cr, ci = carry
            rows = pl.ds(pl.multiple_of(s * SEGS, SEGS), SEGS)
            nr = arb * cr - aib * ci + hr[rows, cs]
            ni = arb * ci + aib * cr + hi[rows, cs]
            hr[rows, cs] = nr
            hi[rows, cs] = ni
            return nr, ni

        zero = jnp.zeros((SEGS, STRIP), F32)
        lax.fori_loop(0, STEPS, step, (zero, zero), unroll=True)
    last = slice((STEPS - 1) * SEGS, STEPS * SEGS)
    end_r, end_i = hr[last, :], hi[last, :]
    a64r, a64i = pwr[STEPS - 1:STEPS, :], pwi[STEPS - 1:STEPS, :]
    cr, ci = cin_r, cin_i
    rows_r, rows_i = [], []
    for j in range(SEGS):
        rows_r.append(cr)
        rows_i.append(ci)
        cr, ci = (a64r * cr - a64i * ci + end_r[j:j + 1, :], a64r * ci + a64i * cr + end_i[j:j + 1, :])
    cm_r, cm_i = jnp.concatenate(rows_r, axis=0), jnp.concatenate(rows_i, axis=0)
    for st in range(N_CH // STRIP):
        cs = slice(st * STRIP, (st + 1) * STRIP)
        cmr, cmi = cm_r[:, cs], cm_i[:, cs]

        def fix(s, carry, cs=cs, cmr=cmr, cmi=cmi):
            rows = pl.ds(pl.multiple_of(s * SEGS, SEGS), SEGS)
            pr, pi = pwr[pl.ds(s, 1), cs], pwi[pl.ds(s, 1), cs]
            hr[rows, cs] = hr[rows, cs] + (pr * cmr - pi * cmi)
            hi[rows, cs] = hi[rows, cs] + (pr * cmi + pi * cmr)
            return carry

        lax.fori_loop(0, STEPS, fix, 0, unroll=True)
    return (cm_r, cm_i), (cr, ci)


def _ssm_outputs(u, hr, hi, crt_ref, cit_ref, vec512_ref, wg_ref):
    ys = []
    for k in range(2):
        cols = slice(k * HALF_CH, (k + 1) * HALF_CH)
        ys.append(_nn(hr[:, cols].astype(BF16), crt_ref[0, k]) - _nn(hi[:, cols].astype(BF16), cit_ref[0, k]))
    y = jnp.concatenate(ys, axis=1) + vec512_ref[0, H_DSKIP:H_DSKIP + 1, :] * u
    z, t = _gelu(y)
    gate = jax.nn.sigmoid(_nn(z.astype(BF16), wg_ref[...]) + vec512_ref[0, H_BGLU:H_BGLU + 1, :])
    return y, z, t, gate


def _ssm_specs(layer, nck, rev):
    def chunk(i):
        return nck - 1 - i if rev else i

    return [pl.BlockSpec((4, CHUNK, LANE), lambda i: (0, chunk(i), 0)),
            pl.BlockSpec((1, 1, N_CH), lambda i: (layer, 0, 0)),
            pl.BlockSpec((1, 1, N_CH), lambda i: (layer, 0, 0)),
            pl.BlockSpec((1, 2, 256, HALF_CH), lambda i: (layer, 0, 0, 0)),
            pl.BlockSpec((1, 2, 256, HALF_CH), lambda i: (layer, 0, 0, 0)),
            pl.BlockSpec((1, 2, HALF_CH, 256), lambda i: (layer, 0, 0, 0)),
            pl.BlockSpec((1, 2, HALF_CH, 256), lambda i: (layer, 0, 0, 0)),
            pl.BlockSpec((1, 8, SSM_W), lambda i: (layer, 0, 0)),
            pl.BlockSpec((SSM_W, SSM_W), lambda i: (0, 0))]


def _ssm_fwd(u4, a_r, a_i, bb_r, bb_i, c_rt, c_it, vec512, w_glu, layer):
    seq = u4.shape[1]
    nck = seq // CHUNK

    def body(u4_ref, ar_ref, ai_ref, bbr_ref, bbi_ref, crt_ref, cit_ref, vec512_ref, wg_ref,
             s4_ref, cmr_ref, cmi_ref, hr, hi, pwr, pwi, car, cai, ubuf, obuf):
        i = pl.program_id(0)

        @pl.when(i == 0)
        def _():
            car[...] = jnp.zeros_like(car)
            cai[...] = jnp.zeros_like(cai)
            _power_table(ar_ref, ai_ref, pwr, pwi)

        _load_slabs(u4_ref, ubuf)
        u = ubuf[...]
        (cm_r, cm_i), (er, ei) = _scan_states(u.astype(BF16), ar_ref, ai_ref, bbr_ref, bbi_ref, pwr, pwi,
                                              car[...], cai[...], hr, hi)
        cmr_ref[0] = cm_r
        cmi_ref[0] = cm_i
        car[...] = er
        cai[...] = ei
        _, z, _, gate = _ssm_outputs(u, hr, hi, crt_ref, cit_ref, vec512_ref, wg_ref)
        obuf[...] = z * gate
        _store_slabs(obuf, s4_ref)

    return pl.pallas_call(
        body, name="ssm_fwd", grid=(nck,),
        in_specs=_ssm_specs(layer, nck, False),
        out_specs=[pl.BlockSpec((4, CHUNK, LANE), lambda i: (0, i, 0)),
                   pl.BlockSpec((1, SEGS, N_CH), lambda i: (i, 0, 0)),
                   pl.BlockSpec((1, SEGS, N_CH), lambda i: (i, 0, 0)),
                   pl.BlockSpec((CHUNK, N_CH), lambda i: (i, 0)),
                   pl.BlockSpec((CHUNK, N_CH), lambda i: (i, 0))],
        out_shape=[jax.ShapeDtypeStruct((4, seq, LANE), F32), jax.ShapeDtypeStruct((nck, SEGS, N_CH), F32),
                   jax.ShapeDtypeStruct((nck, SEGS, N_CH), F32), jax.ShapeDtypeStruct((seq, N_CH), F32),
                   jax.ShapeDtypeStruct((seq, N_CH), F32)],
        scratch_shapes=[pltpu.VMEM((STEPS, N_CH), F32), pltpu.VMEM((STEPS, N_CH), F32),
                        pltpu.VMEM((1, N_CH), F32), pltpu.VMEM((1, N_CH), F32),
                        pltpu.VMEM((CHUNK, SSM_W), F32), pltpu.VMEM((CHUNK, SSM_W), F32)],
        compiler_params=_params(("arbitrary",)),
    )(u4, a_r, a_i, bb_r, bb_i, c_rt, c_it, vec512, w_glu)


def _ssm_bwd(u4, ds4, cm_r, cm_i, h_r, h_i, a_r, a_i, bb_r, bb_i, c_rt, c_it, vec512, w_glu, layer):
    seq = u4.shape[1]
    nck = seq // CHUNK

    def body(u4_ref, ar_ref, ai_ref, bbr_ref, bbi_ref, crt_ref, cit_ref, vec512_ref, wg_ref, ds4_ref, cmr_ref, cmi_ref,
             hr, hi, du4_ref, dbbr_out, dbbi_out, dcrt_out, dcit_out, dar_ref, dai_ref, dwg_ref, dvec_ref,
             gr, gi, pwr, pwi, gcr, gci, accr, acci, ubuf, dbuf, dbbr_ref, dbbi_ref, dcrt_ref, dcit_ref):
        i = pl.program_id(0)

        @pl.when(i == 0)
        def _():
            for ref in (gcr, gci, accr, acci, dbbr_ref, dbbi_ref, dcrt_ref, dcit_ref, dwg_ref, dvec_ref):
                ref[...] = jnp.zeros_like(ref)
            _power_table(ar_ref, ai_ref, pwr, pwi)

        _load_slabs(u4_ref, ubuf)
        u = ubuf[...]
        ubf = u.astype(BF16)
        cm_r, cm_i = cmr_ref[0], cmi_ref[0]
        y, z, t, gate = _ssm_outputs(u, hr, hi, crt_ref, cit_ref, vec512_ref, wg_ref)
        _load_slabs(ds4_ref, dbuf)
        ds = dbuf[...]
        da = ds * z * gate * (1.0 - gate)
        dab = da.astype(BF16)
        dz = ds * gate + _nt(dab, wg_ref[...])
        dwg_ref[...] += _tn(z.astype(BF16), dab)
        dvec_ref[H_BGLU:H_BGLU + 1, :] += _colsum(da)
        dy = dz * _gelu_grad(y, t)
        dvec_ref[H_DSKIP:H_DSKIP + 1, :] += _colsum(dy * u)
        du_skip = dy * vec512_ref[0, H_DSKIP:H_DSKIP + 1, :]
        dyb = dy.astype(BF16)
        for k in range(2):
            rows = slice(k * 256, (k + 1) * 256)
            cols = slice(k * HALF_CH, (k + 1) * HALF_CH)
            dcrt_ref[k] += _tn(hr[:, cols].astype(BF16), dyb[:, rows])
            dcit_ref[k] -= _tn(hi[:, cols].astype(BF16), dyb[:, rows])
            gr[:, cols] = _nt(dyb[:, rows], crt_ref[0, k])
            gi[:, cols] = -_nt(dyb[:, rows], cit_ref[0, k])
        for st in range(N_CH // STRIP):
            cs = slice(st * STRIP, (st + 1) * STRIP)
            arb = jnp.broadcast_to(ar_ref[0, :, cs], (SEGS, STRIP))
            aib = jnp.broadcast_to(ai_ref[0, :, cs], (SEGS, STRIP))

            def step(k, carry, cs=cs, arb=arb, aib=aib):
                cr, ci = carry
                rows = pl.ds(pl.multiple_of((STEPS - 1 - k) * SEGS, SEGS), SEGS)
                nr = gr[rows, cs] + (arb * cr + aib * ci)
                ni = gi[rows, cs] + (arb * ci - aib * cr)
                gr[rows, cs] = nr
                gi[rows, cs] = ni
                return nr, ni

            zero = jnp.zeros((SEGS, STRIP), F32)
            lax.fori_loop(0, STEPS, step, (zero, zero), unroll=True)
        first_r, first_i = gr[0:SEGS, :], gi[0:SEGS, :]
        a64r, a64i = pwr[STEPS - 1:STEPS, :], pwi[STEPS - 1:STEPS, :]
        dr_, di_ = gcr[...], gci[...]
        rows_r, rows_i = [None] * SEGS, [None] * SEGS
        for j in range(SEGS - 1, -1, -1):
            rows_r[j], rows_i[j] = dr_, di_
            dr_, di_ = (first_r[j:j + 1, :] + (a64r * dr_ + a64i * di_), first_i[j:j + 1, :] + (a64r * di_ - a64i * dr_))
        gcr[...] = dr_
        gci[...] = di_
        dm_r, dm_i = jnp.concatenate(rows_r, axis=0), jnp.concatenate(rows_i, axis=0)
        for st in range(N_CH // STRIP):
            cs = slice(st * STRIP, (st + 1) * STRIP)
            dmr, dmi = dm_r[:, cs], dm_i[:, cs]

            def fixed(s, cs=cs, dmr=dmr, dmi=dmi):
                rows = pl.ds(pl.multiple_of(s * SEGS, SEGS), SEGS)
                pr, pi = pwr[pl.ds(STEPS - 1 - s, 1), cs], pwi[pl.ds(STEPS - 1 - s, 1), cs]
                g_r = gr[rows, cs] + (pr * dmr + pi * dmi)
                g_i = gi[rows, cs] + (pr * dmi - pi * dmr)
                gr[rows, cs] = g_r
                gi[rows, cs] = g_i
                return g_r, g_i

            g_r, g_i = fixed(jnp.int32(0))
            acc0 = (g_r * cm_r[:, cs] + g_i * cm_i[:, cs], g_i * cm_r[:, cs] - g_r * cm_i[:, cs])

            def step(s, carry, cs=cs, fixed=fixed):
                sr, si = carry
                g_r, g_i = fixed(s)
                prev = pl.ds(pl.multiple_of((s - 1) * SEGS, SEGS), SEGS)
                hpr, hpi = hr[prev, cs], hi[prev, cs]
                return sr + (g_r * hpr + g_i * hpi), si + (g_i * hpr - g_r * hpi)

            sr, si = lax.fori_loop(1, STEPS, step, acc0, unroll=True)
            accr[:, cs] += sr
            acci[:, cs] += si
        grb, gib = gr[...].astype(BF16), gi[...].astype(BF16)
        dus = []
        for k in range(2):
            rows = slice(k * 256, (k + 1) * 256)
            cols = slice(k * HALF_CH, (k + 1) * HALF_CH)
            dus.append(_nt(grb[:, cols], bbr_ref[0, k]) + _nt(gib[:, cols], bbi_ref[0, k]))
            dbbr_ref[k] += _tn(ubf[:, rows], grb[:, cols])
            dbbi_ref[k] += _tn(ubf[:, rows], gib[:, cols])
        dbuf[...] = jnp.concatenate(dus, axis=1) + du_skip
        _store_slabs(dbuf, du4_ref)

        @pl.when(i == nck - 1)
        def _():
            dar_ref[...] = _colsum(accr[...])
            dai_ref[...] = _colsum(acci[...])
            ng = N_GROUPS // 2
            for k in range(2):
                for acc, out in ((dbbr_ref, dbbr_out), (dbbi_ref, dbbi_out)):
                    out[k] = jnp.concatenate(
                        [acc[k, g * GROUP_W:(g + 1) * GROUP_W, g * STATE:(g + 1) * STATE] for g in range(ng)], axis=0)
                for acc, out in ((dcrt_ref, dcrt_out), (dcit_ref, dcit_out)):
                    out[k] = jnp.concatenate(
                        [acc[k, g * STATE:(g + 1) * STATE, g * GROUP_W:(g + 1) * GROUP_W] for g in range(ng)], axis=0)

    rev4 = pl.BlockSpec((4, CHUNK, LANE), lambda i: (0, nck - 1 - i, 0))
    hc_spec = pl.BlockSpec((1, SEGS, N_CH), lambda i: (nck - 1 - i, 0, 0))
    h_spec = pl.BlockSpec((CHUNK, N_CH), lambda i: (nck - 1 - i, 0))
    fixed2 = lambda shape: pl.BlockSpec(shape, lambda i: (0,) * len(shape))
    return pl.pallas_call(
        body, name="ssm_bwd", grid=(nck,),
        in_specs=_ssm_specs(layer, nck, True) + [rev4, hc_spec, hc_spec, h_spec, h_spec],
        out_specs=[rev4, fixed2((2, 256, STATE)), fixed2((2, 256, STATE)), fixed2((2, HALF_CH, GROUP_W)),
                   fixed2((2, HALF_CH, GROUP_W)), fixed2((1, N_CH)), fixed2((1, N_CH)), fixed2((SSM_W, SSM_W)),
                   fixed2((8, SSM_W))],
        out_shape=[jax.ShapeDtypeStruct((4, seq, LANE), F32),
                   jax.ShapeDtypeStruct((2, 256, STATE), F32), jax.ShapeDtypeStruct((2, 256, STATE), F32),
                   jax.ShapeDtypeStruct((2, HALF_CH, GROUP_W), F32), jax.ShapeDtypeStruct((2, HALF_CH, GROUP_W), F32),
                   jax.ShapeDtypeStruct((1, N_CH), F32), jax.ShapeDtypeStruct((1, N_CH), F32),
                   jax.ShapeDtypeStruct((SSM_W, SSM_W), F32), jax.ShapeDtypeStruct((8, SSM_W), F32)],
        scratch_shapes=[pltpu.VMEM((CHUNK, N_CH), F32), pltpu.VMEM((CHUNK, N_CH), F32),
                        pltpu.VMEM((STEPS, N_CH), F32), pltpu.VMEM((STEPS, N_CH), F32),
                        pltpu.VMEM((1, N_CH), F32), pltpu.VMEM((1, N_CH), F32),
                        pltpu.VMEM((SEGS, N_CH), F32), pltpu.VMEM((SEGS, N_CH), F32),
                        pltpu.VMEM((CHUNK, SSM_W), F32), pltpu.VMEM((CHUNK, SSM_W), F32),
                        pltpu.VMEM((2, 256, HALF_CH), F32), pltpu.VMEM((2, 256, HALF_CH), F32),
                        pltpu.VMEM((2, HALF_CH, 256), F32), pltpu.VMEM((2, HALF_CH, 256), F32)],
        compiler_params=_params(("arbitrary",)),
    )(u4, a_r, a_i, bb_r, bb_i, c_rt, c_it, vec512, w_glu, ds4, cm_r, cm_i, h_r, h_i)


def _block_diag(t):
    nl, _, ng, a, b = t.shape
    eye = jnp.eye(ng, dtype=t.dtype)
    return jnp.einsum("gh,lkgab->lkgahb", eye, t).reshape(nl, 2, ng * a, ng * b)


def _local_step(x, loss_target, mod, p, comm):
    nl = mod.shape[0]
    pad1024 = jnp.zeros((nl, 16 - 10, D_MODEL), F32)
    vec = jnp.concatenate([mod.reshape(nl, N_MOD, D_MODEL), p["pre_mix_g"][:, None], p["post_mix_g"][:, None],
                           p["pre_mlp_g"][:, None], p["post_mlp_g"][:, None], pad1024], axis=1)
    vec512 = jnp.concatenate([p["attn_out_g"][:, None], p["ssm_out_g"][:, None], p["d_skip"][:, None],
                              p["b_glu"][:, None], jnp.zeros((nl, 4, SSM_W), F32)], axis=1)
    n_all = nl * N_CH
    lr = p["lam_re"].reshape(n_all, 1)
    li = p["lam_im"].reshape(n_all, 1)
    ldt = jnp.broadcast_to(p["log_dt"][:, :, None], (nl, N_GROUPS, STATE)).reshape(n_all, 1)
    br = p["b_re"].reshape(n_all, GROUP_W)
    bi = p["b_im"].reshape(n_all, GROUP_W)
    ab_r, ab_i, bb_r, bb_i = _ssm_prepare(lr, li, ldt, br, bi)
    a_r = ab_r.reshape(nl, 1, N_CH)
    a_i = ab_i.reshape(nl, 1, N_CH)

    def dense_b(bb):
        return _block_diag(bb.reshape(nl, 2, 16, STATE, GROUP_W).transpose(0, 1, 2, 4, 3)).astype(BF16)

    def dense_c(cc):
        return _block_diag(cc.reshape(nl, 2, 16, GROUP_W, STATE).transpose(0, 1, 2, 4, 3)).astype(BF16)

    bbr_d, bbi_d = dense_b(bb_r), dense_b(bb_i)
    crt_d, cit_d = dense_c(p["c_re"]), dense_c(p["c_im"])

    saved = []
    xl = x
    mixer_w, mlp_w = [None] * nl, [None] * nl
    for l in range(nl):
        mixer_w[l], tok = comm.mixer_weights(l, [xl, bbr_d, bbi_d, crt_d, cit_d] if l == 0 else xl)
        w_in_t, w_glu, w_out = mixer_w[l]
        q, kv, u4, h1 = _in_proj_fwd(xl, _after(vec, *tok), w_in_t, l)
        attn = _attn_fwd(q, kv, p["attn_sinks"][l])
        s4, *states = _ssm_fwd(u4, a_r, a_i, bbr_d, bbi_d, crt_d, cit_d, vec512, w_glu, l)
        x1 = _out_proj_fwd(xl, attn, s4, vec, vec512, w_out, l)
        mlp_w[l] = comm.mlp_weights(l, x1)
        x2, r, f = _mlp_fwd(x1, vec, mlp_w[l][0], mlp_w[l][1], l)
        saved.append((xl, q, kv, u4, h1, attn, s4, states, x1, r, f))
        xl = x2

    dx, loss_parts = _loss_head(xl, loss_target)
    loss = jnp.sum(loss_parts[:, 0, 0])

    dvec_l, dvec512_l, dsink_l = [None] * nl, [None] * nl, [None] * nl
    dab_r, dab_i, dbb_r, dbb_i, dc_re, dc_im = ([None] * nl for _ in range(6))
    toks = []
    for l in range(nl - 1, -1, -1):
        xl, q, kv, u4, h1, attn, s4, states, x1, r, f = saved[l]
        w_in_t, w_glu, w_out = mixer_w[l]
        dx1, h2, da, df, dvec_m = _mlp_bwd(dx, x1, r, f, _after(vec, *toks), mlp_w[l][0], mlp_w[l][1], l)
        toks = comm.after_mlp_bwd(l, dx1)
        dw_mlp_out = _matmul_tn(r, df, BF16, "dw_mlp_out").reshape(4, D_FF // 4, D_MODEL)
        dw_mlp_in = _matmul_tn(h2, da, BF16, "dw_mlp_in", pieces=4)
        toks = toks + comm.mlp_grads(l, [dw_mlp_in, dw_mlp_out])
        dattn, ds4, heads, dmixed, dvec_o, dvec512_o = _out_proj_bwd(
            dx1, attn, s4, _after(vec, *toks), vec512, w_out, l)
        dw_out = _matmul_tn(heads, dmixed, BF16, "dw_out").reshape(4, D_MODEL // 4, D_MODEL)
        dq, dkv, dsk = _attn_bwd(q, kv, p["attn_sinks"][l], dattn)
        (du4, dbbr, dbbi, dcrt, dcit, dar, dai, dwg, dvec512_s) = _ssm_bwd(
            u4, ds4, *states, a_r, a_i, bbr_d, bbi_d, crt_d, cit_d, vec512, w_glu, l)
        dw_glu = dwg.astype(BF16).reshape(4, SSM_W // 4, SSM_W)
        dx, dproj, dvec_i = _in_proj_bwd(dx1, dq, dkv, du4, xl, vec, w_in_t, l)
        toks = comm.after_in_proj_bwd(l, dx)
        dw_in = _matmul_tn(h1, dproj, BF16, "dw_in")
        toks = toks + comm.mixer_grads(l, [dw_in.reshape(D_MODEL, 4, IN_W // 4).transpose(1, 0, 2), dw_glu, dw_out])
        dvec_l[l] = dvec_m + dvec_o + dvec_i
        dvec512_l[l] = dvec512_o + dvec512_s
        dsink_l[l] = jnp.sum(dsk[:, :, 0], axis=0)
        dab_r[l], dab_i[l], dbb_r[l], dbb_i[l], dc_re[l], dc_im[l] = dar, dai, dbbr, dbbi, dcrt, dcit

    dvec = _after(jnp.stack(dvec_l), *toks)
    dvec512 = jnp.stack(dvec512_l)
    ng = N_GROUPS // 2

    def b_cols(d):
        return jnp.stack(d).reshape(nl, 2, ng, GROUP_W, STATE).transpose(0, 1, 2, 4, 3).reshape(n_all, GROUP_W)

    def c_param(d):
        return jnp.stack(d).reshape(nl, 2, ng, STATE, GROUP_W).transpose(0, 1, 2, 4, 3).reshape(c_shape)

    dbb_r_c, dbb_i_c = b_cols(dbb_r), b_cols(dbb_i)
    c_shape = (nl, N_GROUPS, GROUP_W, STATE)
    dlr, dli, dldt, dbr, dbi = _ssm_prepare_bwd(
        lr, li, ldt, br, bi, jnp.stack(dab_r).reshape(n_all, 1), jnp.stack(dab_i).reshape(n_all, 1), dbb_r_c, dbb_i_c)
    small = {
        "b_ada": dvec[:, :N_MOD].reshape(nl, N_MOD * D_MODEL),
        "pre_mix_g": dvec[:, V_PRE_MIX], "post_mix_g": dvec[:, V_POST_MIX],
        "pre_mlp_g": dvec[:, V_PRE_MLP], "post_mlp_g": dvec[:, V_POST_MLP],
        "attn_sinks": jnp.stack(dsink_l),
        "lam_re": dlr.reshape(nl, N_GROUPS, STATE), "lam_im": dli.reshape(nl, N_GROUPS, STATE),
        "log_dt": jnp.sum(dldt.reshape(nl, N_GROUPS, STATE), axis=-1),
        "b_re": dbr.reshape(nl, N_GROUPS, STATE, GROUP_W), "b_im": dbi.reshape(nl, N_GROUPS, STATE, GROUP_W),
        "c_re": c_param(dc_re), "c_im": c_param(dc_im),
        "d_skip": dvec512[:, H_DSKIP], "b_glu": dvec512[:, H_BGLU],
        "attn_out_g": dvec512[:, H_ATTN_G], "ssm_out_g": dvec512[:, H_SSM_G],
    }
    return loss, dx, small, small["b_ada"]


WEIGHTS = ["w_ada", "b_ada", "pre_mix_g", "w_in", "attn_sinks", "lam_re", "lam_im", "log_dt", "b_re", "b_im", "c_re",
           "c_im", "d_skip", "w_glu", "b_glu", "attn_out_g", "ssm_out_g", "w_out", "post_mix_g", "pre_mlp_g",
           "w_mlp_in", "w_mlp_out", "post_mlp_g"]
BIG = ["w_in", "w_glu", "w_out", "w_mlp_in", "w_mlp_out"]
SMALL = [n for n in WEIGHTS if n not in BIG and n != "w_ada"]
PACK_ROWS = 256


def _pack(parts):
    rows = []
    for n in SMALL:
        flat = parts[n].reshape(-1)
        pad = (-flat.shape[0]) % (PACK_ROWS * LANE)
        rows.append(jnp.pad(flat, (0, pad)).reshape(-1, LANE))
    return jnp.concatenate(rows, axis=0)


def _unpack(packed, shapes):
    out, r0 = {}, 0
    for n in SMALL:
        size = int(np.prod(shapes[n]))
        rows = -(-size // (PACK_ROWS * LANE)) * PACK_ROWS
        out[n] = packed[r0:r0 + rows].reshape(-1)[:size].reshape(shapes[n])
        r0 += rows
    return out


MIXER = ["w_in", "w_glu", "w_out"]
MLP = ["w_mlp_in", "w_mlp_out"]


class _Exchanges:
    def __init__(self, shards, wts, mom, var, chip):
        self.shards, self.wts, self.mom, self.var, self.chip = shards, wts, mom, var, chip
        self.chip_arr = jnp.reshape(chip, (1,)).astype(jnp.int32)
        self.nl = len(shards["w_in"])
        self.gathers, self.scatters, self.pairs = {}, {}, {}
        self.res = {n: None for n in BIG}

    def _start_gather(self, group, tag, l, after=()):
        srcs = [self.shards[n][l] for n in group]
        lands = [lax.dynamic_update_slice(lax.empty((4,) + s.shape, s.dtype), s[None], (self.chip, 0, 0)) for s in srcs]
        plan = _plan_gather(len(srcs))
        st = _exchange_start(f"gather_{tag}{l}_start", 3 * len(srcs), plan, srcs + lands, after)
        self.gathers[tag, l] = (plan, st)
        return st[3]

    def _wait_gather(self, tag, l, after):
        plan, st = self.gathers.pop((tag, l))
        n = len(st[2]) // 2
        bufs = _exchange_wait(f"gather_{tag}{l}_wait", 3 * n, plan, st, after)
        return [b.reshape(4 * b.shape[1], b.shape[2]) for b in bufs[n:]]

    def _start_layer(self, l, after):
        tok = self._start_gather(MIXER, "mixer", l, after)
        return [tok, self._start_gather(MLP, "mlp", l, [tok])]

    def begin(self, after):
        toks = self._start_layer(0, after)
        return toks + (self._start_layer(1, toks[1:]) if self.nl > 1 else [])

    def mixer_weights(self, l, after):
        w = self._wait_gather("mixer", l, after)
        toks = self._start_layer(l + 2, w[:1]) if l + 2 < self.nl else []
        return w, toks

    def mlp_weights(self, l, after):
        return self._wait_gather("mlp", l, after)

    def _start_scatter(self, tag, l, group, pieces):
        plan = _plan_scatter(len(pieces))
        st = _exchange_start(f"scatter_{tag}{l}_start", 3 * len(pieces), plan,
                             list(pieces) + [lax.empty(p.shape, p.dtype) for p in pieces])
        self.scatters[tag] = (l, group, plan, st)
        return [st[3]]

    def _finish_scatter(self, tag, after):
        l, group, plan, st = self.scatters.pop(tag)
        n = len(group)
        bufs = _exchange_wait(f"scatter_{tag}{l}_wait", 3 * n, plan, st, after)
        sums = [_sum_pieces(bufs[k], bufs[n + k], self.chip_arr, "sum_" + group[k]) for k in range(n)]
        plan2 = _plan_pair(n)
        st2 = _exchange_start(f"pair_{tag}{l}_start", n, plan2, sums + [lax.empty(s.shape, s.dtype) for s in sums])
        self.pairs[tag] = (l, group, plan2, st2)
        return [st2[3]]

    def _finish_pair(self, tag, after):
        l, group, plan, st = self.pairs.pop(tag)
        n = len(group)
        bufs = _exchange_wait(f"pair_{tag}{l}_wait", n, plan, st, after)
        for k, name in enumerate(group):
            self.res[name] = _adamw_layer([bufs[k], bufs[n + k]], self.wts[name], self.mom[name], self.var[name],
                                          l, self.res[name], "adamw_" + name)

    def after_mlp_bwd(self, l, after):
        toks = self._finish_scatter("mixer", after) if "mixer" in self.scatters else []
        if "mlp" in self.pairs:
            self._finish_pair("mlp", after)
        return toks

    def mlp_grads(self, l, pieces):
        return self._start_scatter("mlp", l, MLP, pieces)

    def after_in_proj_bwd(self, l, after):
        toks = self._finish_scatter("mlp", after)
        if "mixer" in self.pairs:
            self._finish_pair("mixer", after)
        return toks

    def mixer_grads(self, l, pieces):
        return self._start_scatter("mixer", l, MIXER, pieces)

    def finish_mixer_scatter(self, after):
        return self._finish_scatter("mixer", after)

    def finish_mlp(self, after):
        self._finish_pair("mlp", after)

    def finish_mixer(self, after):
        self._finish_pair("mixer", after)

    def results(self):
        return self.res


def kernel(x, c, w_ada, b_ada, pre_mix_g, w_in, attn_sinks, lam_re, lam_im, log_dt, b_re, b_im, c_re, c_im, d_skip, w_glu, b_glu, attn_out_g, ssm_out_g, w_out, post_mix_g, pre_mlp_g, w_mlp_in, w_mlp_out, post_mlp_g, loss_target, m_w_ada, m_b_ada, m_pre_mix_g, m_w_in, m_attn_sinks, m_lam_re, m_lam_im, m_log_dt, m_b_re, m_b_im, m_c_re, m_c_im, m_d_skip, m_w_glu, m_b_glu, m_attn_out_g, m_ssm_out_g, m_w_out, m_post_mix_g, m_pre_mlp_g, m_w_mlp_in, m_w_mlp_out, m_post_mlp_g, v_w_ada, v_b_ada, v_pre_mix_g, v_w_in, v_attn_sinks, v_lam_re, v_lam_im, v_log_dt, v_b_re, v_b_im, v_c_re, v_c_im, v_d_skip, v_w_glu, v_b_glu, v_attn_out_g, v_ssm_out_g, v_w_out, v_post_mix_g, v_pre_mlp_g, v_w_mlp_in, v_w_mlp_out, v_post_mlp_g):
    args = locals()
    wts = {n: args[n] for n in WEIGHTS}
    mom = {n: args["m_" + n] for n in WEIGHTS}
    var = {n: args["v_" + n] for n in WEIGHTS}
    nl = w_in.shape[0]
    ix, iy, ic = lax.axis_index("x"), lax.axis_index("y"), lax.axis_index("c")
    chip = 2 * ix + iy
    me = 4 * ix + 2 * iy + ic
    wcols = w_ada.shape[2]

    shards = {"w_in": [w_in[l].astype(BF16).T for l in range(nl)], "w_glu": [w_glu[l].astype(BF16) for l in range(nl)],
              "w_out": [w_out[l].astype(BF16) for l in range(nl)],
              "w_mlp_in": [w_mlp_in[l].astype(BF16).T for l in range(nl)],
              "w_mlp_out": [w_mlp_out[l].astype(BF16) for l in range(nl)]}
    comm = _Exchanges(shards, wts, mom, var, chip)

    c_all = _gather([c.reshape(1, 1, 1, D_MODEL)], "all", "gather_c")[0].reshape(8, D_MODEL)
    b_sh = lax.dynamic_slice(b_ada, (0, chip * wcols), (nl, wcols)).reshape(nl, 1, wcols)
    mod_sh = _ada_forward(c_all, w_ada, b_sh)
    mod_all = _gather([mod_sh.reshape(1, 1, nl * 8, wcols)], "chips", "gather_mod")[0]
    toks = comm.begin([mod_all])
    mod = lax.dynamic_index_in_dim(mod_all.reshape(4, nl, 8, wcols), me, axis=2, keepdims=False)
    mod = mod.transpose(1, 0, 2).reshape(nl, 4 * wcols)

    small_p = {n: wts[n] for n in SMALL}
    small_p["log_dt"] = _after(log_dt, *toks)
    loss, grad_x, small, dmod = _local_step(x[0], loss_target[0], mod, small_p, comm)
    loss = lax.psum(loss, ("x", "y", "c"))

    packed = _pack(small)
    rows = packed.shape[0]
    pair_plan = _plan_pair(1)
    pair_small = _exchange_start("pair_small_start", 1, pair_plan, [packed, lax.empty((rows, LANE), F32)])

    dmod = _after(dmod, pair_small[3])
    dmod_all = _gather([dmod.reshape(1, 1, nl, N_MOD * D_MODEL)], "all", "gather_dmod")[0][0]
    dmod_sh = lax.dynamic_slice(dmod_all, (0, 0, chip * wcols), (8, nl, wcols)).transpose(1, 0, 2)
    g_ada = _ada_weight_grad(c_all.T, dmod_sh)
    res = {"w_ada": _adamw(g_ada[:, None], w_ada, m_w_ada, v_w_ada, "adamw_w_ada")}

    comm.finish_mlp(res["w_ada"][0])
    toks = comm.finish_mixer_scatter(res["w_ada"][0])

    own, other = _exchange_wait("pair_small_wait", 1, pair_plan, pair_small, res["w_ada"][0])
    chip_sum = _after(_sum_list([own, other], "sum_pair_small"), *toks)
    quad_plan = _plan_gather(1)
    quad0 = lax.dynamic_update_slice(lax.empty((4, rows, LANE), F32), chip_sum[None], (chip, 0, 0))
    quad_small = _exchange_start("gather_small_start", 3, quad_plan, [chip_sum, quad0])
    comm.finish_mixer([comm.results()[n][0] for n in MLP] + [_after(dmod, quad_small[3])])
    res.update(comm.results())
    quad = _exchange_wait("gather_small_wait", 3, quad_plan, quad_small, [res[n][0] for n in BIG])[1]
    outs = _adamw(quad[None], _pack({n: wts[n] for n in SMALL})[None], _pack({n: mom[n] for n in SMALL})[None],
                  _pack({n: var[n] for n in SMALL})[None], "adamw_small")
    shapes = {n: wts[n].shape for n in SMALL}
    unpacked = [_unpack(o[0], shapes) for o in outs]
    for n in SMALL:
        res[n] = [u[n] for u in unpacked]

    return (loss, grad_x[None], *[res[n][0] for n in WEIGHTS], *[res[n][1] for n in WEIGHTS],
            *[res[n][2] for n in WEIGHTS], *[res[n][3] for n in WEIGHTS])
```
